```python
import math
import jax, jax.numpy as jnp
from jax import lax
import numpy as np

D_MODEL = 2048
BATCH = 4
SEQ = 2048
DEPTH = 1

D_MIX = D_MODEL
D_HY = D_MIX // 2
D_ML = D_MIX - D_HY
HY_GROUPS = 8
HY_ORDER = 2
HY_SHORT = 3
HY_EMB = 33
HY_FILT_HID = 64
HY_DECAY_TARGET = 1e-2
HY_FAST_PCT = 0.3
HY_SLOW_PCT = 1.5
ML_HEADS = 8
ML_HEAD_DIM = D_ML // ML_HEADS
ML_SHORT = 3
ML_CHUNK = 128
N_GATE_COLS = 4 * ML_HEADS
D_IN = 3 * D_HY + 4 * D_ML + N_GATE_COLS
N_EXPERTS = 32
TOP_K = 4
D_FF = D_MODEL
SWIGLU_LIMIT = 7.0
SWIGLU_ALPHA = 1.702
MOE_BLOCK = 128
LN_EPS = 1e-5
DN_ALPHA = (2 * DEPTH) ** 0.25
DN_BETA = (8 * DEPTH) ** -0.25

kernel_name = 'hyena_mlstm_parallel_moe_deepnorm'

F32 = jnp.float32


def layer_norm(u, g, b):
    uf = u.astype(F32)
    mu = uf.mean(-1, keepdims=True)
    var = jnp.square(uf - mu).mean(-1, keepdims=True)
    return ((uf - mu) * lax.rsqrt(var + LN_EPS)).astype(u.dtype) * g + b


def group_norm(u, g, groups):
    shp = u.shape
    uf = u.astype(F32).reshape(*shp[:-1], groups, shp[-1] // groups)
    mu = uf.mean(-1, keepdims=True)
    var = jnp.square(uf - mu).mean(-1, keepdims=True)
    y = ((uf - mu) * lax.rsqrt(var + LN_EPS)).reshape(shp)
    return y.astype(u.dtype) * g


def centred_conv(u, w, b):
    width = w.shape[0]
    p = width // 2
    L = u.shape[1]
    up = jnp.pad(u, ((0, 0), (p, p), (0, 0)))
    return sum(w[j] * up[:, j:j + L] for j in range(width)) + b


def hyena_filter_spectrum(w1, b1, w2, b2, w3, freq, L):
    t = jnp.linspace(0.0, 1.0, L, dtype=F32)[:, None]
    bands = (HY_EMB - 1) // 2
    fb = jnp.linspace(1e-4, bands - 1, bands, dtype=F32)[None]
    w = 2.0 * math.pi * jnp.arange(L, dtype=F32)[:, None] / L
    z = jnp.concatenate([t, jnp.cos(fb * w), -jnp.sin(fb * w)], -1)
    h = jnp.sin(freq[0].astype(F32) * (z @ w1.astype(F32) + b1.astype(F32)))
    h = jnp.sin(freq[1].astype(F32) * (h @ w2.astype(F32) + b2.astype(F32)))
    h = (h @ w3.astype(F32)).reshape(L, HY_ORDER, 2, D_HY)
    deltas = jnp.abs(jnp.linspace(math.log(HY_DECAY_TARGET) / HY_SLOW_PCT,
                                  math.log(HY_DECAY_TARGET) / HY_FAST_PCT, D_HY, dtype=F32))
    h = h * jnp.exp(-t[:, :, None, None] * deltas)
    fwd, bwd = h[:, :, 0], h[:, :, 1]
    k = jnp.concatenate([fwd, jnp.zeros_like(fwd[:1]), jnp.flip(bwd[1:], 0)], 0)
    k = k / jnp.sum(jnp.abs(k), axis=0, keepdims=True)
    return jnp.fft.rfft(k, axis=0)


def long_conv(z, kf):
    L = z.shape[1]
    zf = jnp.fft.rfft(z, n=2 * L, axis=1)
    return jnp.fft.irfft(zf * kf[None], n=2 * L, axis=1)[:, :L]


def mlstm_chunkwise(q, k, v, ig, fg):
    B, H, L, d = q.shape
    nc = L // ML_CHUNK
    lf = jax.nn.log_sigmoid(fg)

    def chunks(a):
        a = a.reshape(B, H, nc, ML_CHUNK, *a.shape[3:])
        return jnp.moveaxis(a, 2, 0)

    tril = jnp.tril(jnp.ones((ML_CHUNK, ML_CHUNK), bool))

    def step(carry, xs):
        C, n, m = carry
        qc, kc, vc, ic, lfc = xs
        b = jnp.cumsum(lfc, axis=-1)
        b_last = b[..., -1]
        dmat = jnp.where(tril, b[..., :, None] - b[..., None, :] + ic[..., None, :], -jnp.inf)
        inter = b + m[..., None]
        m_t = jnp.maximum(inter, dmat.max(-1))
        s = jnp.einsum('bhtd,bhsd->bhts', qc, kc) * jnp.exp(dmat - m_t[..., None])
        inter_w = jnp.exp(inter - m_t)
        num = jnp.einsum('bhts,bhsd->bhtd', s, vc) + inter_w[..., None] * jnp.einsum('bhvk,bhtk->bhtv', C, qc)
        den = s.sum(-1) + inter_w * jnp.einsum('bhk,bhtk->bht', n, qc)
        h = num / jnp.maximum(jnp.abs(den), jnp.exp(-m_t))[..., None]
        g = b_last[..., None] - b + ic
        m_new = jnp.maximum(b_last + m, g.max(-1))
        wg = jnp.exp(g - m_new[..., None])
        decay = jnp.exp(b_last + m - m_new)
        C_new = decay[..., None, None] * C + jnp.einsum('bhs,bhsv,bhsk->bhvk', wg, vc, kc)
        n_new = decay[..., None] * n + jnp.einsum('bhs,bhsk->bhk', wg, kc)
        return (C_new, n_new, m_new), h

    init = (jnp.zeros((B, H, d, d), F32), jnp.zeros((B, H, d), F32), jnp.zeros((B, H), F32))
    _, hs = lax.scan(step, init, (chunks(q), chunks(k), chunks(v), chunks(ig), chunks(lf)))
    return jnp.moveaxis(hs, 0, 2).reshape(B, H, L, d)


def hybrid_mixer(x, w_in, b_in, hy_conv_w, hy_conv_b, hy_filt_w1, hy_filt_b1, hy_filt_w2,
                 hy_filt_b2, hy_filt_w3, hy_filt_freq, hy_skip, hy_norm_w, ml_conv_w,
                 ml_conv_b, ml_norm_w, w_out, b_out):
    B, L, _ = x.shape
    proj = x @ w_in + b_in
    o1 = 3 * D_HY
    o2 = o1 + 2 * D_ML
    o3 = o2 + D_ML
    o4 = o3 + D_ML
    hy_u, ml_qk, ml_v, ml_o, ml_g = (proj[..., :o1], proj[..., o1:o2], proj[..., o2:o3],
                                     proj[..., o3:o4], proj[..., o4:])

    hy_u = centred_conv(hy_u, hy_conv_w, hy_conv_b).astype(F32)
    v, x1, x2 = jnp.split(hy_u, 3, axis=-1)
    kf = hyena_filter_spectrum(hy_filt_w1, hy_filt_b1, hy_filt_w2, hy_filt_b2, hy_filt_w3,
                               hy_filt_freq, L)
    skip = hy_skip.astype(F32)
    z = x1 * (long_conv(v, kf[:, 0]) + skip[0] * v)
    z = x2 * (long_conv(z, kf[:, 1]) + skip[1] * z)
    y_hy = group_norm(z, hy_norm_w.astype(F32), HY_GROUPS).astype(x.dtype)

    qk = jax.nn.silu(centred_conv(ml_qk, ml_conv_w, ml_conv_b))
    q, k = jnp.split(qk, 2, axis=-1)

    def to_heads(a):
        return a.reshape(B, L, ML_HEADS, ML_HEAD_DIM).transpose(0, 2, 1, 3).astype(F32)

    q, k, vm = to_heads(q), to_heads(k) * (ML_HEAD_DIM ** -0.5), to_heads(ml_v)
    gates = ml_g.astype(F32).reshape(B, L, 2, 2, ML_HEADS).transpose(2, 3, 0, 4, 1)
    h_f = mlstm_chunkwise(q, k, vm, gates[0, 0], gates[0, 1])

    def rev(a):
        return jnp.flip(a, axis=2)

    h_b = rev(mlstm_chunkwise(rev(q), rev(k), rev(vm), rev(gates[1, 0]), rev(gates[1, 1])))
    h = (h_f + h_b).transpose(0, 2, 1, 3).reshape(B, L, D_ML)
    y_ml = (group_norm(h, ml_norm_w.astype(F32), ML_HEADS) * jax.nn.sigmoid(ml_o.astype(F32))).astype(x.dtype)

    return jnp.concatenate([y_hy, y_ml], axis=-1) @ w_out + b_out


def moe_ffn(x, router_w, router_b, w_gu, b_gu, w_down, b_down):
    B, L, D = x.shape
    T = B * L
    TK = T * TOP_K
    xt = x.reshape(T, D)
    logits = (xt @ router_w + router_b).astype(F32)
    top_v, top_i = lax.top_k(logits, TOP_K)
    gates = jax.nn.softmax(top_v, axis=-1)
    e_flat = top_i.reshape(-1)
    tok_flat = jnp.arange(TK) // TOP_K
    order = jnp.argsort(e_flat)
    e_sorted = e_flat[order]
    tok_sorted = tok_flat[order]
    counts = jnp.bincount(e_flat, length=N_EXPERTS)
    starts = jnp.cumsum(counts) - counts
    padded = (counts + MOE_BLOCK - 1) // MOE_BLOCK * MOE_BLOCK
    pstarts = jnp.cumsum(padded) - padded
    dest = pstarts[e_sorted] + (jnp.arange(TK) - starts[e_sorted])
    n_blocks = (TK + MOE_BLOCK - 1) // MOE_BLOCK + N_EXPERTS
    n_rows = n_blocks * MOE_BLOCK
    buf = jnp.zeros((n_rows, D), x.dtype).at[dest].set(xt[tok_sorted])
    block_e = jnp.clip(jnp.searchsorted(pstarts + padded, jnp.arange(n_blocks) * MOE_BLOCK,
                                        side='right'), 0, N_EXPERTS - 1)

    def expert_block(args):
        xb, e = args
        hgu = xb @ w_gu[e] + b_gu[e]
        gate, up = hgu[:, :D_FF], hgu[:, D_FF:]
        gate = jnp.minimum(gate, SWIGLU_LIMIT)
        up = jnp.clip(up, -SWIGLU_LIMIT, SWIGLU_LIMIT)
        act = (up + 1.0) * (gate * jax.nn.sigmoid(SWIGLU_ALPHA * gate))
        return act @ w_down[e] + b_down[e]

    out_buf = lax.map(expert_block, (buf.reshape(n_blocks, MOE_BLOCK, D), block_e)).reshape(n_rows, D)
    y_assign = out_buf[dest] * gates.reshape(-1)[order][:, None].astype(x.dtype)
    y = jax.ops.segment_sum(y_assign, tok_sorted, num_segments=T)
    return y.reshape(B, L, D)


def setup_inputs(seed: int = 0) -> dict:
    key = jax.random.key(seed)
    ks = jax.random.split(key, 32)

    def nrm(k, shape, s):
        return jax.random.normal(k, shape, F32) * s

    x = nrm(ks[0], (BATCH, SEQ, D_MODEL), 1.0)
    w_in = nrm(ks[1], (DEPTH, D_MODEL, D_IN), D_MODEL ** -0.5)
    b_main = nrm(ks[2], (DEPTH, D_IN - N_GATE_COLS), 0.02)
    ig_bias = nrm(ks[3], (DEPTH, 2, 1, ML_HEADS), 0.1)
    fg_bias = jnp.linspace(3.0, 6.0, ML_HEADS, dtype=F32)[None, None, None, :] + nrm(ks[4], (DEPTH, 2, 1, ML_HEADS), 0.1)
    gate_bias = jnp.concatenate([ig_bias, fg_bias], axis=2).reshape(DEPTH, N_GATE_COLS)
    b_in = jnp.concatenate([b_main, gate_bias], axis=-1)
    return {
        'x': x,
        'w_in': w_in,
        'b_in': b_in,
        'hy_conv_w': nrm(ks[5], (DEPTH, HY_SHORT, 3 * D_HY), HY_SHORT ** -0.5),
        'hy_conv_b': nrm(ks[6], (DEPTH, 3 * D_HY), 0.02),
        'hy_filt_w1': nrm(ks[7], (DEPTH, HY_EMB, HY_FILT_HID), HY_EMB ** -0.5),
        'hy_filt_b1': nrm(ks[8], (DEPTH, HY_FILT_HID), 0.02),
        'hy_filt_w2': nrm(ks[9], (DEPTH, HY_FILT_HID, HY_FILT_HID), HY_FILT_HID ** -0.5),
        'hy_filt_b2': nrm(ks[10], (DEPTH, HY_FILT_HID), 0.02),
        'hy_filt_w3': nrm(ks[11], (DEPTH, HY_FILT_HID, HY_ORDER * 2 * D_HY), HY_FILT_HID ** -0.5),
        'hy_filt_freq': 1.0 + nrm(ks[12], (DEPTH, 2, HY_FILT_HID), 0.1),
        'hy_skip': nrm(ks[13], (DEPTH, HY_ORDER, D_HY), 1.0),
        'hy_norm_w': 1.0 + nrm(ks[14], (DEPTH, D_HY), 0.02),
        'ml_conv_w': nrm(ks[15], (DEPTH, ML_SHORT, 2 * D_ML), ML_SHORT ** -0.5),
        'ml_conv_b': nrm(ks[16], (DEPTH, 2 * D_ML), 0.02),
        'ml_norm_w': 1.0 + nrm(ks[17], (DEPTH, D_ML), 0.02),
        'w_out': nrm(ks[18], (DEPTH, D_MIX, D_MODEL), D_MIX ** -0.5 * DN_BETA),
        'b_out': nrm(ks[19], (DEPTH, D_MODEL), 0.02),
        'ln1_g': 1.0 + nrm(ks[20], (DEPTH, D_MODEL), 0.02),
        'ln1_b': nrm(ks[21], (DEPTH, D_MODEL), 0.02),
        'router_w': nrm(ks[22], (DEPTH, D_MODEL, N_EXPERTS), D_MODEL ** -0.5),
        'router_b': nrm(ks[23], (DEPTH, N_EXPERTS), 0.01),
        'w_gu': nrm(ks[24], (DEPTH, N_EXPERTS, D_MODEL, 2 * D_FF), D_MODEL ** -0.5),
        'b_gu': nrm(ks[25], (DEPTH, N_EXPERTS, 2 * D_FF), 0.02),
        'w_down': nrm(ks[26], (DEPTH, N_EXPERTS, D_FF, D_MODEL), D_FF ** -0.5 * DN_BETA),
        'b_down': nrm(ks[27], (DEPTH, N_EXPERTS, D_MODEL), 0.02),
        'ln2_g': 1.0 + nrm(ks[28], (DEPTH, D_MODEL), 0.02),
        'ln2_b': nrm(ks[29], (DEPTH, D_MODEL), 0.02),
    }


def reference(x, w_in, b_in, hy_conv_w, hy_conv_b, hy_filt_w1, hy_filt_b1, hy_filt_w2,
              hy_filt_b2, hy_filt_w3, hy_filt_freq, hy_skip, hy_norm_w, ml_conv_w, ml_conv_b,
              ml_norm_w, w_out, b_out, ln1_g, ln1_b, router_w, router_b, w_gu, b_gu, w_down,
              b_down, ln2_g, ln2_b):
    for l in range(DEPTH):
        mix = hybrid_mixer(x, w_in[l], b_in[l], hy_conv_w[l], hy_conv_b[l], hy_filt_w1[l],
                           hy_filt_b1[l], hy_filt_w2[l], hy_filt_b2[l], hy_filt_w3[l],
                           hy_filt_freq[l], hy_skip[l], hy_norm_w[l], ml_conv_w[l], ml_conv_b[l],
                           ml_norm_w[l], w_out[l], b_out[l])
        x = layer_norm(DN_ALPHA * x + mix, ln1_g[l], ln1_b[l])
        ff = moe_ffn(x, router_w[l], router_b[l], w_gu[l], b_gu[l], w_down[l], b_down[l])
        x = layer_norm(DN_ALPHA * x + ff, ln2_g[l], ln2_b[l])
    return x
```

```python
import functools
import math

import jax
import jax.numpy as jnp
from jax import lax
from jax.experimental import pallas as pl
from jax.experimental.pallas import tpu as pltpu

F32 = jnp.float32
BF16 = jnp.bfloat16
I32 = jnp.int32
HP = lax.Precision.HIGHEST

D_MODEL = 2048
D_HY = 1024
D_ML = 1024
ML_HEADS = 8
HEAD_DIM = 128
CHUNK = 128
N_GATE_COLS = 32
HY_EMB = 33
N_EXPERTS = 32
TOP_K = 4
D_FF = 2048
SWIGLU_LIMIT = 7.0
SWIGLU_ALPHA = 1.702
LN_EPS = 1e-5
DN_ALPHA = 2.0 ** 0.25
HY_DECAY_TARGET = 1e-2
HY_FAST_PCT = 0.3
HY_SLOW_PCT = 1.5

LANES = 128
ROW_CHUNKS = D_MODEL // LANES
VMEM_LIMIT = 56 * 1024 * 1024

PROJ_TM = 1024
PROJ_TN = 512
HY_CW = 256
HY_FC = 512
OUT_TM = 256
RT_TM = 256
EXP_ROWS = 1280
EXP_SUB = 256
EXP_TF = 256
CMB_TM = 128


def _cparams(sem):
    return pltpu.CompilerParams(dimension_semantics=sem, vmem_limit_bytes=VMEM_LIMIT)


def _const_spec(shape, index_map):
    return pl.BlockSpec(shape, index_map, pipeline_mode=pl.Buffered(1))


def _proj_kernel(x_ref, w_ref, b_ref, o_ref, xb_ref):
    @pl.when(pl.program_id(1) == 0)
    def _():
        xb_ref[...] = x_ref[...].astype(BF16)

    acc = jnp.dot(xb_ref[...], w_ref[...], preferred_element_type=F32) + b_ref[...]
    o_ref[...] = acc.astype(o_ref.dtype)


def _in_proj(x2d, w_bf, b_row):
    m, k = x2d.shape
    n = w_bf.shape[1]
    return pl.pallas_call(
        _proj_kernel,
        grid=(m // PROJ_TM, n // PROJ_TN),
        in_specs=[pl.BlockSpec((PROJ_TM, k), lambda i, j: (i, 0)),
                  pl.BlockSpec((k, PROJ_TN), lambda i, j: (0, j)),
                  pl.BlockSpec((1, PROJ_TN), lambda i, j: (0, j))],
        out_specs=pl.BlockSpec((PROJ_TM, PROJ_TN), lambda i, j: (i, j)),
        out_shape=jax.ShapeDtypeStruct((m, n), BF16),
        scratch_shapes=[pltpu.VMEM((PROJ_TM, k), BF16)],
        compiler_params=_cparams(("arbitrary", "arbitrary")),
        name="in_proj")(x2d, w_bf, b_row)


def _gate_kernel(x_ref, w_ref, b_ref, o_ref):
    o_ref[...] = jnp.dot(x_ref[...], w_ref[...], precision=HP, preferred_element_type=F32) + b_ref[...]


def _gate_proj(x2d, w_pad, b_pad):
    m, k = x2d.shape
    tm = 512
    return pl.pallas_call(
        _gate_kernel,
        grid=(m // tm,),
        in_specs=[pl.BlockSpec((tm, k), lambda i: (i, 0)),
                  pl.BlockSpec((k, LANES), lambda i: (0, 0)),
                  pl.BlockSpec((1, LANES), lambda i: (0, 0))],
        out_specs=pl.BlockSpec((tm, LANES), lambda i: (i, 0)),
        out_shape=jax.ShapeDtypeStruct((m, LANES), F32),
        compiler_params=_cparams(("arbitrary",)),
        name="gate_proj")(x2d, w_pad, b_pad)


def _filter_kernel(z_ref, w1_ref, b1_ref, w2_ref, b2_ref, fq_ref, w3f_ref, w3b_ref, dl_ref,
                   c_ref, s_ref, kr_ref, ki_ref, kn_ref):
    seq = z_ref.shape[0]
    inv_n = 1.0 / (2 * seq)
    z = z_ref[...]
    h = jnp.sin(fq_ref[0:1, :] * (jnp.dot(z, w1_ref[...], precision=HP, preferred_element_type=F32) + b1_ref[...]))
    h = jnp.sin(fq_ref[1:2, :] * (jnp.dot(h, w2_ref[...], precision=HP, preferred_element_type=F32) + b2_ref[...]))
    win = jnp.exp(-z[:, 0:1] * dl_ref[...])
    fwd = jnp.dot(h, w3f_ref[...], precision=HP, preferred_element_type=F32) * win
    bwd = jnp.dot(h, w3b_ref[...], precision=HP, preferred_element_type=F32) * win
    row = lax.broadcasted_iota(I32, fwd.shape, 0)
    bwd = jnp.where(row == 0, 0.0, bwd)
    inv = 1.0 / jnp.sum(jnp.abs(fwd) + jnp.abs(bwd), axis=0, keepdims=True)
    ks = (fwd + bwd) * inv
    kd = (fwd - bwd) * inv
    kr = jnp.dot(c_ref[...], ks.astype(BF16), preferred_element_type=F32)
    ki = -jnp.dot(s_ref[...], kd.astype(BF16), preferred_element_type=F32)
    wf = jnp.where(row == 0, inv_n, 2.0 * inv_n)
    kr_ref[0] = kr * wf
    ki_ref[0] = ki * wf
    sgn = jnp.where((row & 1) == 0, 1.0, -1.0)
    kn_ref[0] = jnp.sum(ks * sgn, axis=0, keepdims=True) * inv_n


def _hyena_filters(zpad, w1pad, b1, w2, b2, freq, w3, deltas, cmat, smat):
    seq = zpad.shape[0]
    nb = D_HY // HY_CW
    hid = w2.shape[0]
    full = lambda shape: pl.BlockSpec(shape, lambda o, c: (0,) * len(shape))
    out_sds = jax.ShapeDtypeStruct((2, seq, D_HY), F32)
    return pl.pallas_call(
        _filter_kernel,
        grid=(2, nb),
        in_specs=[full(zpad.shape), full(w1pad.shape), full(b1.shape), full(w2.shape), full(b2.shape),
                  full(freq.shape),
                  pl.BlockSpec((hid, HY_CW), lambda o, c: (0, o * 2 * nb + c)),
                  pl.BlockSpec((hid, HY_CW), lambda o, c: (0, o * 2 * nb + nb + c)),
                  pl.BlockSpec((1, HY_CW), lambda o, c: (0, c)),
                  _const_spec((seq, seq), lambda o, c: (0, 0)),
                  _const_spec((seq, seq), lambda o, c: (0, 0))],
        out_specs=[pl.BlockSpec((1, seq, HY_CW), lambda o, c: (o, 0, c)),
                   pl.BlockSpec((1, seq, HY_CW), lambda o, c: (o, 0, c)),
                   pl.BlockSpec((1, 1, HY_CW), lambda o, c: (o, 0, c))],
        out_shape=[out_sds, out_sds, jax.ShapeDtypeStruct((2, 1, D_HY), F32)],
        compiler_params=_cparams(("arbitrary", "arbitrary")),
        name="hyena_filters")(zpad, w1pad, b1, w2, b2, freq, w3, w3, deltas, cmat, smat)


def _short_conv(u, w_ref, b_ref, row, seq):
    prev = jnp.where(row == 0, 0.0, pltpu.roll(u, 1, 0))
    nxt = jnp.where(row == seq - 1, 0.0, pltpu.roll(u, seq - 1, 0))
    return w_ref[0:1, :] * prev + w_ref[1:2, :] * u + w_ref[2:3, :] * nxt + b_ref[...]


def _hyena_kernel(uv_ref, u1_ref, u2_ref, wv_ref, w1_ref, w2_ref, bv_ref, b1_ref, b2_ref,
                  c_ref, s_ref, kr_ref, ki_ref, kn_ref, skip_ref, nw_ref, o_ref,
                  z_ref, x_ref, zb_ref, pr_ref, pi_ref, ny_ref):
    seq = uv_ref.shape[1]
    cw = uv_ref.shape[2]
    nblk = seq // HY_FC
    row = lax.broadcasted_iota(I32, (seq, LANES), 0)
    sgn = jnp.where((row & 1) == 0, 1.0, -1.0)
    sgn_blk = sgn[:HY_FC, :]
    groups = [slice(g * LANES, (g + 1) * LANES) for g in range(cw // LANES)]

    def conv_group(u_ref, w_ref, b_ref, gs):
        return _short_conv(u_ref[0, :, gs].astype(F32), w_ref.at[:, gs], b_ref.at[:, gs], row, seq)

    def set_input(z, gs, o):
        z_ref[:, gs] = z
        zb_ref[:, gs] = z.astype(BF16)
        ny_ref[:, gs] = jnp.sum(z * sgn, axis=0, keepdims=True) * kn_ref[o, :, gs]

    def spectrum(o):
        for fb in range(nblk):
            fs = slice(fb * HY_FC, (fb + 1) * HY_FC)
            a = jnp.dot(c_ref[fs, :], zb_ref[...], preferred_element_type=F32)
            b = jnp.dot(s_ref[fs, :], zb_ref[...], preferred_element_type=F32)
            kr = kr_ref[o, fs, :]
            ki = ki_ref[o, fs, :]
            pr_ref[fs, :] = (a * kr + b * ki).astype(BF16)
            pi_ref[fs, :] = (a * ki - b * kr).astype(BF16)

    def conv_rows(tb):
        ts = slice(tb * HY_FC, (tb + 1) * HY_FC)
        y = jnp.dot(c_ref[ts, :], pr_ref[...], preferred_element_type=F32)
        y = y - jnp.dot(s_ref[ts, :], pi_ref[...], preferred_element_type=F32)
        return ts, y + ny_ref[...] * jnp.concatenate([sgn_blk] * (cw // LANES), axis=-1)

    for gs in groups:
        set_input(conv_group(uv_ref, wv_ref, bv_ref, gs), gs, 0)
        x_ref[:, gs] = conv_group(u1_ref, w1_ref, b1_ref, gs)
    spectrum(0)
    for tb in range(nblk):
        ts, y = conv_rows(tb)
        x_ref[ts, :] = x_ref[ts, :] * (y + skip_ref[0:1, :] * z_ref[ts, :])
    for gs in groups:
        set_input(x_ref[:, gs], gs, 1)
        x_ref[:, gs] = conv_group(u2_ref, w2_ref, b2_ref, gs)
    spectrum(1)
    for tb in range(nblk):
        ts, y = conv_rows(tb)
        z = x_ref[ts, :] * (y + skip_ref[1:2, :] * z_ref[ts, :])
        for gs in groups:
            zg = z[:, gs]
            mu = jnp.mean(zg, axis=-1, keepdims=True)
            zc = zg - mu
            var = jnp.mean(zc * zc, axis=-1, keepdims=True)
            o_ref[0, ts, gs] = (zc * lax.rsqrt(var + LN_EPS) * nw_ref[:, gs]).astype(o_ref.dtype)


def _hyena(proj3, conv_w, conv_b, cmat, smat, kr, ki, kn, skip, norm_w):
    bsz, seq, _ = proj3.shape
    nb = D_HY // HY_CW
    u_spec = lambda off: pl.BlockSpec((1, seq, HY_CW), lambda c, b: (b, 0, off + c))
    w_spec = lambda off: pl.BlockSpec((3, HY_CW), lambda c, b: (0, off + c))
    b_spec = lambda off: pl.BlockSpec((1, HY_CW), lambda c, b: (0, off + c))
    return pl.pallas_call(
        _hyena_kernel,
        grid=(nb, bsz),
        in_specs=[u_spec(0), u_spec(nb), u_spec(2 * nb),
                  w_spec(0), w_spec(nb), w_spec(2 * nb),
                  b_spec(0), b_spec(nb), b_spec(2 * nb),
                  _const_spec((seq, seq), lambda c, b: (0, 0)),
                  _const_spec((seq, seq), lambda c, b: (0, 0)),
                  _const_spec((2, seq, HY_CW), lambda c, b: (0, 0, c)),
                  _const_spec((2, seq, HY_CW), lambda c, b: (0, 0, c)),
                  pl.BlockSpec((2, 1, HY_CW), lambda c, b: (0, 0, c)),
                  pl.BlockSpec((2, HY_CW), lambda c, b: (0, c)),
                  pl.BlockSpec((1, HY_CW), lambda c, b: (0, c))],
        out_specs=pl.BlockSpec((1, seq, HY_CW), lambda c, b: (b, 0, c)),
        out_shape=jax.ShapeDtypeStruct((bsz, seq, D_HY), BF16),
        scratch_shapes=[pltpu.VMEM((seq, HY_CW), F32), pltpu.VMEM((seq, HY_CW), F32),
                        pltpu.VMEM((seq, HY_CW), BF16), pltpu.VMEM((seq, HY_CW), BF16),
                        pltpu.VMEM((seq, HY_CW), BF16), pltpu.VMEM((1, HY_CW), F32)],
        compiler_params=_cparams(("arbitrary", "arbitrary")),
        name="hyena")(proj3, proj3, proj3, conv_w, conv_w, conv_w, conv_b, conv_b, conv_b,
                      cmat, smat, kr, ki, kn, skip, norm_w)


def _mlstm_kernel(qp_ref, kp_ref, v_ref, og_ref, wq_ref, wk_ref, bq_ref, bk_ref, gc_ref, gr_ref,
                  nw_ref, o_ref, qb_ref, kb_ref, bcol_ref, brow_ref, hacc_ref):
    seq = qp_ref.shape[1]
    d = qp_ref.shape[2]
    nchunk = seq // CHUNK
    row = lax.broadcasted_iota(I32, (seq, d), 0)
    q = _short_conv(qp_ref[0].astype(F32), wq_ref, bq_ref, row, seq)
    k = _short_conv(kp_ref[0].astype(F32), wk_ref, bk_ref, row, seq)
    qb_ref[...] = (q * jax.nn.sigmoid(q)).astype(BF16)
    kb_ref[...] = ((k * jax.nn.sigmoid(k)) * (d ** -0.5)).astype(BF16)
    hacc_ref[...] = jnp.zeros_like(hacc_ref)

    ti = lax.broadcasted_iota(I32, (CHUNK, CHUNK), 0)
    si = lax.broadcasted_iota(I32, (CHUNK, CHUNK), 1)
    lower = ti >= si
    upper = ti <= si
    lower_f = lower.astype(F32)
    upper_f = upper.astype(F32)

    lf_col = jax.nn.log_sigmoid(gc_ref[0, 0])
    lf_row = jax.nn.log_sigmoid(gr_ref[0, 0])
    col4 = lax.broadcasted_iota(I32, (CHUNK, 4), 1)
    row4 = lax.broadcasted_iota(I32, (4, CHUNK), 0)
    for c in range(nchunk):
        cs = slice(c * CHUNK, (c + 1) * CHUNK)
        blk = lf_col[cs, :]
        pre = jnp.dot(lower_f, blk, precision=HP, preferred_element_type=F32)
        suf = jnp.dot(upper_f, blk, precision=HP, preferred_element_type=F32)
        bcol_ref[cs, :] = jnp.where(col4 < 2, pre, suf)
        blk_r = lf_row[:, cs]
        pre_r = jnp.dot(blk_r, upper_f, precision=HP, preferred_element_type=F32)
        suf_r = jnp.dot(blk_r, lower_f, precision=HP, preferred_element_type=F32)
        brow_ref[:, cs] = jnp.where(row4 < 2, pre_r, suf_r)

    def chunk_step(c, state, direction):
        cmat, nvec, m = state
        r0 = pl.multiple_of(c * CHUNK, CHUNK)
        rs = pl.ds(r0, CHUNK)
        fcol = 2 * direction + 1
        icol = 2 * direction
        qc = qb_ref[rs, :]
        kc = kb_ref[rs, :]
        vc = v_ref[0, rs, :]
        bc = bcol_ref[rs, fcol:fcol + 1]
        ic = gc_ref[0, 0, rs, icol:icol + 1]
        br = brow_ref[fcol:fcol + 1, rs]
        ir = gr_ref[0, 0, icol:icol + 1, rs]
        mask = lower if direction == 0 else upper
        b_last = br[:, CHUNK - 1:CHUNK] if direction == 0 else br[:, 0:1]
        dmat = jnp.where(mask, bc - br + ir, -jnp.inf)
        inter = bc + m
        m_t = jnp.maximum(inter, jnp.max(dmat, axis=-1, keepdims=True))
        p = jnp.exp(dmat - m_t)
        s = lax.dot_general(qc, kc, (((1,), (1,)), ((), ())), preferred_element_type=F32) * p
        inter_w = jnp.exp(inter - m_t)
        cq = lax.dot_general(qc, cmat.astype(BF16), (((1,), (1,)), ((), ())), preferred_element_type=F32)
        num = jnp.dot(s.astype(BF16), vc, preferred_element_type=F32) + inter_w * cq
        nq = jnp.sum(qc.astype(F32) * nvec, axis=-1, keepdims=True)
        den = jnp.sum(s, axis=-1, keepdims=True) + inter_w * nq
        h = num / jnp.maximum(jnp.abs(den), jnp.exp(-m_t))
        g = b_last - bc + ic
        m_new = jnp.maximum(b_last + m, jnp.max(g, axis=0, keepdims=True))
        wg = jnp.exp(g - m_new)
        decay = jnp.exp(b_last + m - m_new)
        wv = (wg * vc.astype(F32)).astype(BF16)
        c_new = decay * cmat + lax.dot_general(wv, kc, (((0,), (0,)), ((), ())), preferred_element_type=F32)
        n_new = decay * nvec + jnp.sum(wg * kc.astype(F32), axis=0, keepdims=True)
        hacc_ref[rs, :] += h
        return (c_new, n_new, m_new)

    def body(i, carry):
        sf, sb = carry
        sf = chunk_step(i, sf, 0)
        sb = chunk_step(nchunk - 1 - i, sb, 1)
        return (sf, sb)

    zero_state = (jnp.zeros((d, d), F32), jnp.zeros((1, d), F32), jnp.zeros((1, 1), F32))
    lax.fori_loop(0, nchunk, body, (zero_state, zero_state))

    h = hacc_ref[...]
    mu = jnp.mean(h, axis=-1, keepdims=True)
    hc = h - mu
    var = jnp.mean(hc * hc, axis=-1, keepdims=True)
    y = hc * lax.rsqrt(var + LN_EPS) * nw_ref[...] * jax.nn.sigmoid(og_ref[0].astype(F32))
    o_ref[0] = y.astype(o_ref.dtype)


def _mlstm(proj3, conv_w, conv_b, gcol, grow, norm_w):
    bsz, seq, _ = proj3.shape
    d = HEAD_DIM
    hy_blocks = 3 * D_HY // d
    qoff, koff, voff, ooff = hy_blocks, hy_blocks + ML_HEADS, hy_blocks + 2 * ML_HEADS, hy_blocks + 3 * ML_HEADS
    p_spec = lambda off: pl.BlockSpec((1, seq, d), lambda b, h: (b, 0, off + h))
    return pl.pallas_call(
        _mlstm_kernel,
        grid=(bsz, ML_HEADS),
        in_specs=[p_spec(qoff), p_spec(koff), p_spec(voff), p_spec(ooff),
                  pl.BlockSpec((3, d), lambda b, h: (0, h)),
                  pl.BlockSpec((3, d), lambda b, h: (0, ML_HEADS + h)),
                  pl.BlockSpec((1, d), lambda b, h: (0, h)),
                  pl.BlockSpec((1, d), lambda b, h: (0, ML_HEADS + h)),
                  pl.BlockSpec((1, 1, seq, 4), lambda b, h: (b, h, 0, 0)),
                  pl.BlockSpec((1, 1, 4, seq), lambda b, h: (b, h, 0, 0)),
                  pl.BlockSpec((1, d), lambda b, h: (0, h))],
        out_specs=pl.BlockSpec((1, seq, d), lambda b, h: (b, 0, h)),
        out_shape=jax.ShapeDtypeStruct((bsz, seq, D_ML), BF16),
        scratch_shapes=[pltpu.VMEM((seq, d), BF16), pltpu.VMEM((seq, d), BF16),
                        pltpu.VMEM((seq, 4), F32), pltpu.VMEM((4, seq), F32),
                        pltpu.VMEM((seq, d), F32)],
        compiler_params=_cparams(("arbitrary", "arbitrary")),
        name="mlstm")(proj3, proj3, proj3, proj3, conv_w, conv_w, conv_b, conv_b, gcol, grow, norm_w)


def _load_rows(ref, start, nrows):
    parts = [ref[pl.ds(start * ROW_CHUNKS + c, nrows, stride=ROW_CHUNKS), :] for c in range(ROW_CHUNKS)]
    return jnp.concatenate(parts, axis=-1)


def _layer_norm(u, g, b):
    mu = jnp.mean(u, axis=-1, keepdims=True)
    uc = u - mu
    var = jnp.mean(uc * uc, axis=-1, keepdims=True)
    return uc * lax.rsqrt(var + LN_EPS) * g + b


def _outproj_kernel(yh_ref, ym_ref, x_ref, wa_ref, wb_ref, b_ref, g_ref, be_ref, o_ref):
    tm = x_ref.shape[0]
    mix = (jnp.dot(yh_ref[...], wa_ref[...], preferred_element_type=F32)
           + jnp.dot(ym_ref[...], wb_ref[...], preferred_element_type=F32) + b_ref[...])
    y = _layer_norm(DN_ALPHA * x_ref[...] + mix, g_ref[...], be_ref[...])
    for c in range(ROW_CHUNKS):
        o_ref[pl.ds(c, tm, stride=ROW_CHUNKS), :] = y[:, c * LANES:(c + 1) * LANES]


def _out_proj_ln(y_hy, y_ml, x2d, w_out_bf, b_out, g, be):
    t = x2d.shape[0]
    tm = OUT_TM
    vec = lambda: pl.BlockSpec((1, D_MODEL), lambda i: (0, 0))
    return pl.pallas_call(
        _outproj_kernel,
        grid=(t // tm,),
        in_specs=[pl.BlockSpec((tm, D_HY), lambda i: (i, 0)),
                  pl.BlockSpec((tm, D_ML), lambda i: (i, 0)),
                  pl.BlockSpec((tm, D_MODEL), lambda i: (i, 0)),
                  _const_spec((D_HY, D_MODEL), lambda i: (0, 0)),
                  _const_spec((D_ML, D_MODEL), lambda i: (1, 0)),
                  vec(), vec(), vec()],
        out_specs=pl.BlockSpec((tm * ROW_CHUNKS, LANES), lambda i: (i, 0)),
        out_shape=jax.ShapeDtypeStruct((t * ROW_CHUNKS, LANES), F32),
        compiler_params=_cparams(("arbitrary",)),
        name="out_proj_ln1")(y_hy, y_ml, x2d, w_out_bf, w_out_bf, b_out, g, be)


def _router_kernel(x_ref, w_ref, b_ref, ti_ref, tg_ref, tp_ref, cnt_ref):
    tm = ti_ref.shape[0]

    @pl.when(pl.program_id(0) == 0)
    def _():
        cnt_ref[...] = jnp.zeros_like(cnt_ref)

    x = _load_rows(x_ref, 0, tm)
    logits = jnp.dot(x, w_ref[...], precision=HP, preferred_element_type=F32) + b_ref[...]
    lane = lax.broadcasted_iota(I32, (tm, LANES), 1)
    work = logits
    vals, idxs = [], []
    chosen = jnp.zeros((tm, LANES), F32)
    for _ in range(TOP_K):
        mx = jnp.max(work, axis=-1, keepdims=True)
        idx = jnp.min(jnp.where(work == mx, lane, LANES), axis=-1, keepdims=True)
        hit = lane == idx
        vals.append(mx)
        idxs.append(idx)
        chosen = jnp.where(hit, 1.0, chosen)
        work = jnp.where(hit, -jnp.inf, work)
    exps = [jnp.exp(v - vals[0]) for v in vals]
    den = exps[0] + exps[1] + exps[2] + exps[3]
    ri = lax.broadcasted_iota(I32, (tm, tm), 0)
    ci = lax.broadcasted_iota(I32, (tm, tm), 1)
    strict_lower = (ri > ci).astype(BF16)
    carry = cnt_ref[...]
    slot = carry + jnp.dot(strict_lower, chosen.astype(BF16), preferred_element_type=F32)
    ti = jnp.zeros((tm, LANES), I32)
    tg = jnp.zeros((tm, LANES), F32)
    tp = jnp.zeros((tm, LANES), F32)
    for k in range(TOP_K):
        sk = jnp.sum(jnp.where(lane == idxs[k], slot, 0.0), axis=-1, keepdims=True)
        ti = jnp.where(lane == k, idxs[k], ti)
        tg = jnp.where(lane == k, exps[k] / den, tg)
        tp = jnp.where(lane == k, sk, tp)
    ti_ref[...] = ti
    tg_ref[...] = tg
    tp_ref[...] = tp.astype(I32)
    cnt_ref[...] = carry + jnp.sum(chosen, axis=0, keepdims=True)


def _router(x1c, w_pad, b_pad):
    t = x1c.shape[0] // ROW_CHUNKS
    tm = RT_TM
    o_spec = lambda: pl.BlockSpec((tm, LANES), lambda i: (i, 0))
    return pl.pallas_call(
        _router_kernel,
        grid=(t // tm,),
        in_specs=[pl.BlockSpec((tm * ROW_CHUNKS, LANES), lambda i: (i, 0)),
                  pl.BlockSpec((D_MODEL, LANES), lambda i: (0, 0)),
                  pl.BlockSpec((1, LANES), lambda i: (0, 0))],
        out_specs=[o_spec(), o_spec(), o_spec(), pl.BlockSpec((1, LANES), lambda i: (0, 0))],
        out_shape=[jax.ShapeDtypeStruct((t, LANES), I32), jax.ShapeDtypeStruct((t, LANES), F32),
                   jax.ShapeDtypeStruct((t, LANES), I32), jax.ShapeDtypeStruct((1, LANES), F32)],
        compiler_params=_cparams(("arbitrary",)),
        name="router")(x1c, w_pad, b_pad)


def _expert_kernel(te_ref, tr_ref, tb_ref, rt_ref, x_hbm, wg_ref, wu_ref, wd_ref, bg_ref, bu_ref, bd_ref,
                   o_ref, stage_ref, xb_ref, wgu_ref, wdn_ref, sem):
    s = pl.program_id(0)
    j = pl.program_id(1)
    n_tiles = pl.num_programs(0)
    rows = tr_ref[s]
    tf = wg_ref.shape[2]

    def row_copy(tok, r):
        return pltpu.make_async_copy(
            x_hbm.at[pl.ds(pl.multiple_of(tok * ROW_CHUNKS, ROW_CHUNKS), ROW_CHUNKS), :],
            stage_ref.at[pl.ds(pl.multiple_of(r * ROW_CHUNKS, ROW_CHUNKS), ROW_CHUNKS), :], sem)

    def start_gather(tile):
        base = tile * EXP_ROWS

        def body(r, carry):
            row_copy(rt_ref[base + r], r).start()
            return carry

        lax.fori_loop(0, tr_ref[tile], body, 0)

    def wait_gather(tile):
        def body(r, carry):
            row_copy(0, r).wait()
            return carry

        lax.fori_loop(0, tr_ref[tile], body, 0)

    @pl.when(jnp.logical_and(s == 0, j == 0))
    def _():
        stage_ref[...] = jnp.zeros_like(stage_ref)
        start_gather(0)

    @pl.when(rows > 0)
    def _():
        nsub = (rows + EXP_SUB - 1) // EXP_SUB

        @pl.when(j == 0)
        def _():
            wait_gather(s)

            def cvt(i, carry):
                r0 = pl.multiple_of(i * EXP_SUB, EXP_SUB)
                for c in range(ROW_CHUNKS):
                    piece = stage_ref[pl.ds(r0 * ROW_CHUNKS + c, EXP_SUB, stride=ROW_CHUNKS), :]
                    xb_ref[pl.ds(r0, EXP_SUB), c * LANES:(c + 1) * LANES] = piece.astype(BF16)
                return carry

            lax.fori_loop(0, nsub, cvt, 0)

        @pl.when(jnp.logical_and(j == 1, s + 1 < n_tiles))
        def _():
            start_gather(jnp.minimum(s + 1, n_tiles - 1))

        wgu_ref[:, :tf] = wg_ref[0].astype(BF16)
        wgu_ref[:, tf:] = wu_ref[0].astype(BF16)
        wdn_ref[...] = wd_ref[0].astype(BF16)
        bgu = jnp.concatenate([bg_ref[0], bu_ref[0]], axis=-1)

        def sub(i, carry):
            r0 = pl.multiple_of(i * EXP_SUB, EXP_SUB)
            xs = xb_ref[pl.ds(r0, EXP_SUB), :]
            hgu = jnp.dot(xs, wgu_ref[...], preferred_element_type=F32) + bgu
            gate = jnp.minimum(hgu[:, :tf], SWIGLU_LIMIT)
            up = jnp.clip(hgu[:, tf:], -SWIGLU_LIMIT, SWIGLU_LIMIT)
            act = (up + 1.0) * (gate * jax.nn.sigmoid(SWIGLU_ALPHA * gate))
            part = jnp.dot(act.astype(BF16), wdn_ref[...], preferred_element_type=F32)

            @pl.when(j == 0)
            def _():
                for c in range(ROW_CHUNKS):
                    cs = slice(c * LANES, (c + 1) * LANES)
                    o_ref[pl.ds(r0 * ROW_CHUNKS + c, EXP_SUB, stride=ROW_CHUNKS), :] = part[:, cs] + bd_ref[0][:, cs]

            @pl.when(j > 0)
            def _():
                for c in range(ROW_CHUNKS):
                    cs = slice(c * LANES, (c + 1) * LANES)
                    o_ref[pl.ds(r0 * ROW_CHUNKS + c, EXP_SUB, stride=ROW_CHUNKS), :] += part[:, cs]

            return carry

        lax.fori_loop(0, nsub, sub, 0)


def _experts(tile_e, tile_rows, tile_blk, row_tok, x1c, w_gu, b_gu, w_down, b_down):
    n_tiles = tile_e.shape[0]
    nf = D_FF // EXP_TF

    def jmap(s, j, tr):
        return jnp.where(tr[s] > 0, j, nf - 1)

    grid_spec = pltpu.PrefetchScalarGridSpec(
        num_scalar_prefetch=4,
        grid=(n_tiles, nf),
        in_specs=[pl.BlockSpec(memory_space=pl.ANY),
                  pl.BlockSpec((1, D_MODEL, EXP_TF), lambda s, j, te, tr, tb, rt: (te[s], 0, jmap(s, j, tr))),
                  pl.BlockSpec((1, D_MODEL, EXP_TF), lambda s, j, te, tr, tb, rt: (te[s], 0, nf + jmap(s, j, tr))),
                  pl.BlockSpec((1, EXP_TF, D_MODEL), lambda s, j, te, tr, tb, rt: (te[s], jmap(s, j, tr), 0)),
                  pl.BlockSpec((1, 1, EXP_TF), lambda s, j, te, tr, tb, rt: (te[s], 0, jmap(s, j, tr))),
                  pl.BlockSpec((1, 1, EXP_TF), lambda s, j, te, tr, tb, rt: (te[s], 0, nf + jmap(s, j, tr))),
                  pl.BlockSpec((1, 1, D_MODEL), lambda s, j, te, tr, tb, rt: (te[s], 0, 0))],
        out_specs=pl.BlockSpec((EXP_ROWS * ROW_CHUNKS, LANES), lambda s, j, te, tr, tb, rt: (tb[s], 0)),
        scratch_shapes=[pltpu.VMEM((EXP_ROWS * ROW_CHUNKS, LANES), F32),
                        pltpu.VMEM((EXP_ROWS, D_MODEL), BF16),
                        pltpu.VMEM((D_MODEL, 2 * EXP_TF), BF16),
                        pltpu.VMEM((EXP_TF, D_MODEL), BF16),
                        pltpu.SemaphoreType.DMA(())])
    return pl.pallas_call(
        _expert_kernel,
        grid_spec=grid_spec,
        out_shape=jax.ShapeDtypeStruct((n_tiles * EXP_ROWS * ROW_CHUNKS, LANES), F32),
        compiler_params=_cparams(("arbitrary", "arbitrary")),
        name="experts")(tile_e, tile_rows, tile_blk, row_tok, x1c, w_gu, w_gu, w_down, b_gu, b_gu, b_down)


def _combine_kernel(dest_ref, y_hbm, x_ref, tg_ref, g_ref, be_ref, o_ref, buf_ref, sem):
    tm = o_ref.shape[0]
    base = pl.program_id(0) * tm * TOP_K

    def row_copy(src_row, k, t):
        return pltpu.make_async_copy(
            y_hbm.at[pl.ds(pl.multiple_of(src_row * ROW_CHUNKS, ROW_CHUNKS), ROW_CHUNKS), :],
            buf_ref.at[k, pl.ds(pl.multiple_of(t * ROW_CHUNKS, ROW_CHUNKS), ROW_CHUNKS), :], sem)

    def start(t, carry):
        for k in range(TOP_K):
            row_copy(dest_ref[base + t * TOP_K + k], k, t).start()
        return carry

    lax.fori_loop(0, tm, start, 0)

    def wait(t, carry):
        for k in range(TOP_K):
            row_copy(0, k, t).wait()
        return carry

    lax.fori_loop(0, tm, wait, 0)

    tg = tg_ref[...]
    ff = jnp.zeros((tm, D_MODEL), F32)
    for k in range(TOP_K):
        ff = ff + tg[:, k:k + 1] * _load_rows(buf_ref.at[k], 0, tm)
    x1 = _load_rows(x_ref, 0, tm)
    o_ref[...] = _layer_norm(DN_ALPHA * x1 + ff, g_ref[...], be_ref[...])


def _combine_ln(dest_flat, y_buf, x1c, tg, g, be):
    t = tg.shape[0]
    tm = CMB_TM
    grid_spec = pltpu.PrefetchScalarGridSpec(
        num_scalar_prefetch=1,
        grid=(t // tm,),
        in_specs=[pl.BlockSpec(memory_space=pl.ANY),
                  pl.BlockSpec((tm * ROW_CHUNKS, LANES), lambda i, d: (i, 0)),
                  pl.BlockSpec((tm, LANES), lambda i, d: (i, 0)),
                  pl.BlockSpec((1, D_MODEL), lambda i, d: (0, 0)),
                  pl.BlockSpec((1, D_MODEL), lambda i, d: (0, 0))],
        out_specs=pl.BlockSpec((tm, D_MODEL), lambda i, d: (i, 0)),
        scratch_shapes=[pltpu.VMEM((TOP_K, tm * ROW_CHUNKS, LANES), F32), pltpu.SemaphoreType.DMA(())])
    return pl.pallas_call(
        _combine_kernel,
        grid_spec=grid_spec,
        out_shape=jax.ShapeDtypeStruct((t, D_MODEL), F32),
        compiler_params=_cparams(("arbitrary",)),
        name="combine_ln2")(dest_flat, y_buf, x1c, tg, g, be)


def _dft_tables(seq):
    n = 2 * seq
    f = jnp.arange(seq, dtype=I32)
    ang = ((f[:, None] * f[None, :]) % n).astype(F32) * (2.0 * math.pi / n)
    return jnp.cos(ang).astype(BF16), jnp.sin(ang).astype(BF16)


def _filter_features(seq):
    t = jnp.linspace(0.0, 1.0, seq, dtype=F32)[:, None]
    bands = (HY_EMB - 1) // 2
    fb = jnp.linspace(1e-4, bands - 1, bands, dtype=F32)[None]
    w = 2.0 * math.pi * jnp.arange(seq, dtype=F32)[:, None] / seq
    z = jnp.concatenate([t, jnp.cos(fb * w), -jnp.sin(fb * w)], -1)
    return jnp.pad(z, ((0, 0), (0, LANES - HY_EMB)))


def _mixer(x, w_in, b_in, hy_conv_w, hy_conv_b, hy_filt_w1, hy_filt_b1, hy_filt_w2, hy_filt_b2,
           hy_filt_w3, hy_filt_freq, hy_skip, hy_norm_w, ml_conv_w, ml_conv_b, ml_norm_w):
    bsz, seq, _ = x.shape
    t = bsz * seq
    x2d = x.reshape(t, D_MODEL)
    n_main = w_in.shape[1] - N_GATE_COLS
    proj = _in_proj(x2d, w_in[:, :n_main].astype(BF16), b_in[None, :n_main])
    proj3 = proj.reshape(bsz, seq, n_main)
    wg = jnp.pad(w_in[:, n_main:], ((0, 0), (0, LANES - N_GATE_COLS)))
    bg = jnp.pad(b_in[None, n_main:], ((0, 0), (0, LANES - N_GATE_COLS)))
    gates = _gate_proj(x2d, wg, bg)[:, :N_GATE_COLS]
    g5 = gates.reshape(bsz, seq, 4, ML_HEADS)
    gcol = g5.transpose(0, 3, 1, 2)
    grow = g5.transpose(0, 3, 2, 1)

    cmat, smat = _dft_tables(seq)
    zpad = _filter_features(seq)
    w1pad = jnp.pad(hy_filt_w1, ((0, LANES - HY_EMB), (0, 0)))
    deltas = jnp.abs(jnp.linspace(math.log(HY_DECAY_TARGET) / HY_SLOW_PCT,
                                  math.log(HY_DECAY_TARGET) / HY_FAST_PCT, D_HY, dtype=F32))[None]
    kr, ki, kn = _hyena_filters(zpad, w1pad, hy_filt_b1[None], hy_filt_w2, hy_filt_b2[None],
                                hy_filt_freq, hy_filt_w3, deltas, cmat, smat)
    y_hy = _hyena(proj3, hy_conv_w, hy_conv_b[None], cmat, smat, kr, ki, kn, hy_skip, hy_norm_w[None])
    y_ml = _mlstm(proj3, ml_conv_w, ml_conv_b[None], gcol, grow, ml_norm_w[None])
    return y_hy.reshape(t, D_HY), y_ml.reshape(t, D_ML), x2d


def _moe_tables(top_i, slot, counts):
    t = top_i.shape[0]
    n_tiles = N_EXPERTS + (t * TOP_K) // EXP_ROWS
    nsup = (counts + EXP_ROWS - 1) // EXP_ROWS
    ends = jnp.cumsum(nsup)
    starts = ends - nsup
    total = ends[-1]
    s_idx = jnp.arange(n_tiles, dtype=I32)
    valid = s_idx < total
    last = jnp.maximum(total - 1, 0)
    s_eff = jnp.where(valid, s_idx, last)
    tile_e = jnp.clip(jnp.searchsorted(ends, s_eff, side='right'), 0, N_EXPERTS - 1).astype(I32)
    local = s_eff - starts[tile_e]
    tile_rows = jnp.where(valid, jnp.clip(counts[tile_e] - local * EXP_ROWS, 0, EXP_ROWS), 0).astype(I32)
    tile_blk = s_eff.astype(I32)
    dest = (starts[top_i] * EXP_ROWS + slot).astype(I32)
    tok = jnp.broadcast_to(jnp.arange(t, dtype=I32)[:, None], dest.shape)
    row_tok = jnp.zeros((n_tiles * EXP_ROWS,), I32).at[dest.reshape(-1)].set(tok.reshape(-1))
    return tile_e, tile_rows, tile_blk, row_tok, dest.reshape(-1)


def kernel(x, w_in, b_in, hy_conv_w, hy_conv_b, hy_filt_w1, hy_filt_b1, hy_filt_w2, hy_filt_b2, hy_filt_w3, hy_filt_freq, hy_skip, hy_norm_w, ml_conv_w, ml_conv_b, ml_norm_w, w_out, b_out, ln1_g, ln1_b, router_w, router_b, w_gu, b_gu, w_down, b_down, ln2_g, ln2_b):
    bsz, seq, _ = x.shape
    l = 0
    y_hy, y_ml, x2d = _mixer(x, w_in[l], b_in[l], hy_conv_w[l], hy_conv_b[l], hy_filt_w1[l], hy_filt_b1[l],
                             hy_filt_w2[l], hy_filt_b2[l], hy_filt_w3[l], hy_filt_freq[l], hy_skip[l],
                             hy_norm_w[l], ml_conv_w[l], ml_conv_b[l], ml_norm_w[l])
    x1c = _out_proj_ln(y_hy, y_ml, x2d, w_out[l].astype(BF16), b_out[l][None], ln1_g[l][None], ln1_b[l][None])
    rw = jnp.pad(router_w[l], ((0, 0), (0, LANES - N_EXPERTS)))
    rb = jnp.pad(router_b[l][None], ((0, 0), (0, LANES - N_EXPERTS)), constant_values=-1e30)
    top_i, top_g, slot, cnt = _router(x1c, rw, rb)
    counts = cnt[0, :N_EXPERTS].astype(I32)
    tile_e, tile_rows, tile_blk, row_tok, dest = _moe_tables(top_i[:, :TOP_K], slot[:, :TOP_K], counts)
    y_buf = _experts(tile_e, tile_rows, tile_blk, row_tok, x1c, w_gu[l], b_gu[l][:, None, :], w_down[l],
                     b_down[l][:, None, :])
    out = _combine_ln(dest, y_buf, x1c, top_g, ln2_g[l][None], ln2_b[l][None])
    return out.reshape(bsz, seq, D_MODEL)
```

```python
import functools
import math

import jax
import jax.numpy as jnp
from jax import lax
from jax.experimental import pallas as pl
from jax.experimental.pallas import tpu as pltpu

F32 = jnp.float32
BF16 = jnp.bfloat16
I32 = jnp.int32
HP = lax.Precision.HIGHEST

D_MODEL = 2048
D_HY = 1024
D_ML = 1024
ML_HEADS = 8
HEAD_DIM = 128
CHUNK = 128
N_GATE_COLS = 32
HY_EMB = 33
N_EXPERTS = 32
TOP_K = 4
D_FF = 2048
SWIGLU_LIMIT = 7.0
SWIGLU_ALPHA = 1.702
LN_EPS = 1e-5
DN_ALPHA = 2.0 ** 0.25
HY_DECAY_TARGET = 1e-2
HY_FAST_PCT = 0.3
HY_SLOW_PCT = 1.5

LANES = 128
ROW_CHUNKS = D_MODEL // LANES
VMEM_LIMIT = 56 * 1024 * 1024

PROJ_TM = 1024
PROJ_TN = 512
HY_CW = 256
HY_FC = 512
OUT_TM = 256
RT_TM = 256
EXP_ROWS = 1152
EXP_CHUNK = 128
EXP_TF = 256
CMB_TM = 128


def _cparams(sem):
    return pltpu.CompilerParams(dimension_semantics=sem, vmem_limit_bytes=VMEM_LIMIT)


def _const_spec(shape, index_map):
    return pl.BlockSpec(shape, index_map, pipeline_mode=pl.Buffered(1))


def _proj_kernel(x_ref, w_ref, b_ref, o_ref, xb_ref):
    @pl.when(pl.program_id(1) == 0)
    def _():
        xb_ref[...] = x_ref[...].astype(BF16)

    acc = jnp.dot(xb_ref[...], w_ref[...], preferred_element_type=F32) + b_ref[...]
    o_ref[...] = acc.astype(o_ref.dtype)


def _in_proj(x2d, w_bf, b_row):
    m, k = x2d.shape
    n = w_bf.shape[1]
    return pl.pallas_call(
        _proj_kernel,
        grid=(m // PROJ_TM, n // PROJ_TN),
        in_specs=[pl.BlockSpec((PROJ_TM, k), lambda i, j: (i, 0)),
                  pl.BlockSpec((k, PROJ_TN), lambda i, j: (0, j)),
                  pl.BlockSpec((1, PROJ_TN), lambda i, j: (0, j))],
        out_specs=pl.BlockSpec((PROJ_TM, PROJ_TN), lambda i, j: (i, j)),
        out_shape=jax.ShapeDtypeStruct((m, n), BF16),
        scratch_shapes=[pltpu.VMEM((PROJ_TM, k), BF16)],
        compiler_params=_cparams(("arbitrary", "arbitrary")),
        name="in_proj")(x2d, w_bf, b_row)


def _gate_kernel(x_ref, w_ref, b_ref, o_ref):
    o_ref[...] = jnp.dot(x_ref[...], w_ref[...], precision=HP, preferred_element_type=F32) + b_ref[...]


def _gate_proj(x2d, w_pad, b_pad):
    m, k = x2d.shape
    tm = 512
    return pl.pallas_call(
        _gate_kernel,
        grid=(m // tm,),
        in_specs=[pl.BlockSpec((tm, k), lambda i: (i, 0)),
                  pl.BlockSpec((k, LANES), lambda i: (0, 0)),
                  pl.BlockSpec((1, LANES), lambda i: (0, 0))],
        out_specs=pl.BlockSpec((tm, LANES), lambda i: (i, 0)),
        out_shape=jax.ShapeDtypeStruct((m, LANES), F32),
        compiler_params=_cparams(("arbitrary",)),
        name="gate_proj")(x2d, w_pad, b_pad)


def _filter_kernel(z_ref, w1_ref, b1_ref, w2_ref, b2_ref, fq_ref, w3f_ref, w3b_ref, dl_ref,
                   c_ref, s_ref, kr_ref, ki_ref, kn_ref):
    seq = z_ref.shape[0]
    inv_n = 1.0 / (2 * seq)
    z = z_ref[...]
    h = jnp.sin(fq_ref[0:1, :] * (jnp.dot(z, w1_ref[...], precision=HP, preferred_element_type=F32) + b1_ref[...]))
    h = jnp.sin(fq_ref[1:2, :] * (jnp.dot(h, w2_ref[...], precision=HP, preferred_element_type=F32) + b2_ref[...]))
    win = jnp.exp(-z[:, 0:1] * dl_ref[...])
    fwd = jnp.dot(h, w3f_ref[...], precision=HP, preferred_element_type=F32) * win
    bwd = jnp.dot(h, w3b_ref[...], precision=HP, preferred_element_type=F32) * win
    row = lax.broadcasted_iota(I32, fwd.shape, 0)
    bwd = jnp.where(row == 0, 0.0, bwd)
    inv = 1.0 / jnp.sum(jnp.abs(fwd) + jnp.abs(bwd), axis=0, keepdims=True)
    ks = (fwd + bwd) * inv
    kd = (fwd - bwd) * inv
    kr = jnp.dot(c_ref[...], ks.astype(BF16), preferred_element_type=F32)
    ki = -jnp.dot(s_ref[...], kd.astype(BF16), preferred_element_type=F32)
    wf = jnp.where(row == 0, inv_n, 2.0 * inv_n)
    kr_ref[0] = kr * wf
    ki_ref[0] = ki * wf
    sgn = jnp.where((row & 1) == 0, 1.0, -1.0)
    kn_ref[0] = jnp.sum(ks * sgn, axis=0, keepdims=True) * inv_n


def _hyena_filters(zpad, w1pad, b1, w2, b2, freq, w3, deltas, cmat, smat):
    seq = zpad.shape[0]
    nb = D_HY // HY_CW
    hid = w2.shape[0]
    full = lambda shape: pl.BlockSpec(shape, lambda o, c: (0,) * len(shape))
    out_sds = jax.ShapeDtypeStruct((2, seq, D_HY), F32)
    return pl.pallas_call(
        _filter_kernel,
        grid=(2, nb),
        in_specs=[full(zpad.shape), full(w1pad.shape), full(b1.shape), full(w2.shape), full(b2.shape),
                  full(freq.shape),
                  pl.BlockSpec((hid, HY_CW), lambda o, c: (0, o * 2 * nb + c)),
                  pl.BlockSpec((hid, HY_CW), lambda o, c: (0, o * 2 * nb + nb + c)),
                  pl.BlockSpec((1, HY_CW), lambda o, c: (0, c)),
                  _const_spec((seq, seq), lambda o, c: (0, 0)),
                  _const_spec((seq, seq), lambda o, c: (0, 0))],
        out_specs=[pl.BlockSpec((1, seq, HY_CW), lambda o, c: (o, 0, c)),
                   pl.BlockSpec((1, seq, HY_CW), lambda o, c: (o, 0, c)),
                   pl.BlockSpec((1, 1, HY_CW), lambda o, c: (o, 0, c))],
        out_shape=[out_sds, out_sds, jax.ShapeDtypeStruct((2, 1, D_HY), F32)],
        compiler_params=_cparams(("arbitrary", "arbitrary")),
        name="hyena_filters")(zpad, w1pad, b1, w2, b2, freq, w3, w3, deltas, cmat, smat)


def _short_conv(u, w_ref, b_ref, row, seq):
    prev = jnp.where(row == 0, 0.0, pltpu.roll(u, 1, 0))
    nxt = jnp.where(row == seq - 1, 0.0, pltpu.roll(u, seq - 1, 0))
    return w_ref[0:1, :] * prev + w_ref[1:2, :] * u + w_ref[2:3, :] * nxt + b_ref[...]


def _hyena_kernel(uv_ref, u1_ref, u2_ref, wv_ref, w1_ref, w2_ref, bv_ref, b1_ref, b2_ref,
                  c_ref, s_ref, kr_ref, ki_ref, kn_ref, skip_ref, nw_ref, o_ref,
                  z_ref, x_ref, zb_ref, pr_ref, pi_ref, ny_ref):
    seq = uv_ref.shape[1]
    cw = uv_ref.shape[2]
    nblk = seq // HY_FC
    row = lax.broadcasted_iota(I32, (seq, LANES), 0)
    sgn = jnp.where((row & 1) == 0, 1.0, -1.0)
    sgn_blk = sgn[:HY_FC, :]
    groups = [slice(g * LANES, (g + 1) * LANES) for g in range(cw // LANES)]

    def conv_group(u_ref, w_ref, b_ref, gs):
        return _short_conv(u_ref[0, :, gs].astype(F32), w_ref.at[:, gs], b_ref.at[:, gs], row, seq)

    def set_input(z, gs, o):
        z_ref[:, gs] = z
        zb_ref[:, gs] = z.astype(BF16)
        ny_ref[:, gs] = jnp.sum(z * sgn, axis=0, keepdims=True) * kn_ref[o, :, gs]

    def spectrum(o):
        for fb in range(nblk):
            fs = slice(fb * HY_FC, (fb + 1) * HY_FC)
            a = jnp.dot(c_ref[fs, :], zb_ref[...], preferred_element_type=F32)
            b = jnp.dot(s_ref[fs, :], zb_ref[...], preferred_element_type=F32)
            kr = kr_ref[o, fs, :]
            ki = ki_ref[o, fs, :]
            pr_ref[fs, :] = (a * kr + b * ki).astype(BF16)
            pi_ref[fs, :] = (a * ki - b * kr).astype(BF16)

    def conv_rows(tb):
        ts = slice(tb * HY_FC, (tb + 1) * HY_FC)
        y = jnp.dot(c_ref[ts, :], pr_ref[...], preferred_element_type=F32)
        y = y - jnp.dot(s_ref[ts, :], pi_ref[...], preferred_element_type=F32)
        return ts, y + ny_ref[...] * jnp.concatenate([sgn_blk] * (cw // LANES), axis=-1)

    for gs in groups:
        set_input(conv_group(uv_ref, wv_ref, bv_ref, gs), gs, 0)
        x_ref[:, gs] = conv_group(u1_ref, w1_ref, b1_ref, gs)
    spectrum(0)
    for tb in range(nblk):
        ts, y = conv_rows(tb)
        x_ref[ts, :] = x_ref[ts, :] * (y + skip_ref[0:1, :] * z_ref[ts, :])
    for gs in groups:
        set_input(x_ref[:, gs], gs, 1)
        x_ref[:, gs] = conv_group(u2_ref, w2_ref, b2_ref, gs)
    spectrum(1)
    for tb in range(nblk):
        ts, y = conv_rows(tb)
        z = x_ref[ts, :] * (y + skip_ref[1:2, :] * z_ref[ts, :])
        for gs in groups:
            zg = z[:, gs]
            mu = jnp.mean(zg, axis=-1, keepdims=True)
            zc = zg - mu
            var = jnp.mean(zc * zc, axis=-1, keepdims=True)
            o_ref[0, ts, gs] = (zc * lax.rsqrt(var + LN_EPS) * nw_ref[:, gs]).astype(o_ref.dtype)


def _hyena(proj3, conv_w, conv_b, cmat, smat, kr, ki, kn, skip, norm_w):
    bsz, seq, _ = proj3.shape
    nb = D_HY // HY_CW
    u_spec = lambda off: pl.BlockSpec((1, seq, HY_CW), lambda c, b: (b, 0, off + c))
    w_spec = lambda off: pl.BlockSpec((3, HY_CW), lambda c, b: (0, off + c))
    b_spec = lambda off: pl.BlockSpec((1, HY_CW), lambda c, b: (0, off + c))
    return pl.pallas_call(
        _hyena_kernel,
        grid=(nb, bsz),
        in_specs=[u_spec(0), u_spec(nb), u_spec(2 * nb),
                  w_spec(0), w_spec(nb), w_spec(2 * nb),
                  b_spec(0), b_spec(nb), b_spec(2 * nb),
                  _const_spec((seq, seq), lambda c, b: (0, 0)),
                  _const_spec((seq, seq), lambda c, b: (0, 0)),
                  _const_spec((2, seq, HY_CW), lambda c, b: (0, 0, c)),
                  _const_spec((2, seq, HY_CW), lambda c, b: (0, 0, c)),
                  pl.BlockSpec((2, 1, HY_CW), lambda c, b: (0, 0, c)),
                  pl.BlockSpec((2, HY_CW), lambda c, b: (0, c)),
                  pl.BlockSpec((1, HY_CW), lambda c, b: (0, c))],
        out_specs=pl.BlockSpec((1, seq, HY_CW), lambda c, b: (b, 0, c)),
        out_shape=jax.ShapeDtypeStruct((bsz, seq, D_HY), BF16),
        scratch_shapes=[pltpu.VMEM((seq, HY_CW), F32), pltpu.VMEM((seq, HY_CW), F32),
                        pltpu.VMEM((seq, HY_CW), BF16), pltpu.VMEM((seq, HY_CW), BF16),
                        pltpu.VMEM((seq, HY_CW), BF16), pltpu.VMEM((1, HY_CW), F32)],
        compiler_params=_cparams(("arbitrary", "arbitrary")),
        name="hyena")(proj3, proj3, proj3, conv_w, conv_w, conv_w, conv_b, conv_b, conv_b,
                      cmat, smat, kr, ki, kn, skip, norm_w)


def _mlstm_kernel(qp_ref, kp_ref, v_ref, og_ref, wq_ref, wk_ref, bq_ref, bk_ref, gc_ref, gr_ref,
                  nw_ref, o_ref, qb_ref, kb_ref, bcol_ref, brow_ref, hacc_ref):
    seq = qp_ref.shape[1]
    d = qp_ref.shape[2]
    nchunk = seq // CHUNK
    row = lax.broadcasted_iota(I32, (seq, d), 0)
    q = _short_conv(qp_ref[0].astype(F32), wq_ref, bq_ref, row, seq)
    k = _short_conv(kp_ref[0].astype(F32), wk_ref, bk_ref, row, seq)
    qb_ref[...] = (q * jax.nn.sigmoid(q)).astype(BF16)
    kb_ref[...] = ((k * jax.nn.sigmoid(k)) * (d ** -0.5)).astype(BF16)
    hacc_ref[...] = jnp.zeros_like(hacc_ref)

    ti = lax.broadcasted_iota(I32, (CHUNK, CHUNK), 0)
    si = lax.broadcasted_iota(I32, (CHUNK, CHUNK), 1)
    lower = ti >= si
    upper = ti <= si
    lower_f = lower.astype(F32)
    upper_f = upper.astype(F32)

    lf_col = jax.nn.log_sigmoid(gc_ref[0, 0])
    lf_row = jax.nn.log_sigmoid(gr_ref[0, 0])
    col4 = lax.broadcasted_iota(I32, (CHUNK, 4), 1)
    row4 = lax.broadcasted_iota(I32, (4, CHUNK), 0)
    for c in range(nchunk):
        cs = slice(c * CHUNK, (c + 1) * CHUNK)
        blk = lf_col[cs, :]
        pre = jnp.dot(lower_f, blk, precision=HP, preferred_element_type=F32)
        suf = jnp.dot(upper_f, blk, precision=HP, preferred_element_type=F32)
        bcol_ref[cs, :] = jnp.where(col4 < 2, pre, suf)
        blk_r = lf_row[:, cs]
        pre_r = jnp.dot(blk_r, upper_f, precision=HP, preferred_element_type=F32)
        suf_r = jnp.dot(blk_r, lower_f, precision=HP, preferred_element_type=F32)
        brow_ref[:, cs] = jnp.where(row4 < 2, pre_r, suf_r)

    def chunk_step(c, state, direction):
        cmat, nvec, m = state
        r0 = pl.multiple_of(c * CHUNK, CHUNK)
        rs = pl.ds(r0, CHUNK)
        fcol = 2 * direction + 1
        icol = 2 * direction
        qc = qb_ref[rs, :]
        kc = kb_ref[rs, :]
        vc = v_ref[0, rs, :]
        bc = bcol_ref[rs, fcol:fcol + 1]
        ic = gc_ref[0, 0, rs, icol:icol + 1]
        br = brow_ref[fcol:fcol + 1, rs]
        ir = gr_ref[0, 0, icol:icol + 1, rs]
        mask = lower if direction == 0 else upper
        b_last = br[:, CHUNK - 1:CHUNK] if direction == 0 else br[:, 0:1]
        dmat = jnp.where(mask, bc - br + ir, -jnp.inf)
        inter = bc + m
        m_t = jnp.maximum(inter, jnp.max(dmat, axis=-1, keepdims=True))
        p = jnp.exp(dmat - m_t)
        s = lax.dot_general(qc, kc, (((1,), (1,)), ((), ())), preferred_element_type=F32) * p
        inter_w = jnp.exp(inter - m_t)
        cq = lax.dot_general(qc, cmat.astype(BF16), (((1,), (1,)), ((), ())), preferred_element_type=F32)
        num = jnp.dot(s.astype(BF16), vc, preferred_element_type=F32) + inter_w * cq
        nq = jnp.sum(qc.astype(F32) * nvec, axis=-1, keepdims=True)
        den = jnp.sum(s, axis=-1, keepdims=True) + inter_w * nq
        h = num / jnp.maximum(jnp.abs(den), jnp.exp(-m_t))
        g = b_last - bc + ic
        m_new = jnp.maximum(b_last + m, jnp.max(g, axis=0, keepdims=True))
        wg = jnp.exp(g - m_new)
        decay = jnp.exp(b_last + m - m_new)
        wv = (wg * vc.astype(F32)).astype(BF16)
        c_new = decay * cmat + lax.dot_general(wv, kc, (((0,), (0,)), ((), ())), preferred_element_type=F32)
        n_new = decay * nvec + jnp.sum(wg * kc.astype(F32), axis=0, keepdims=True)
        hacc_ref[rs, :] += h
        return (c_new, n_new, m_new)

    def body(i, carry):
        sf, sb = carry
        sf = chunk_step(i, sf, 0)
        sb = chunk_step(nchunk - 1 - i, sb, 1)
        return (sf, sb)

    zero_state = (jnp.zeros((d, d), F32), jnp.zeros((1, d), F32), jnp.zeros((1, 1), F32))
    lax.fori_loop(0, nchunk, body, (zero_state, zero_state))

    h = hacc_ref[...]
    mu = jnp.mean(h, axis=-1, keepdims=True)
    hc = h - mu
    var = jnp.mean(hc * hc, axis=-1, keepdims=True)
    y = hc * lax.rsqrt(var + LN_EPS) * nw_ref[...] * jax.nn.sigmoid(og_ref[0].astype(F32))
    o_ref[0] = y.astype(o_ref.dtype)


def _mlstm(proj3, conv_w, conv_b, gcol, grow, norm_w):
    bsz, seq, _ = proj3.shape
    d = HEAD_DIM
    hy_blocks = 3 * D_HY // d
    qoff, koff, voff, ooff = hy_blocks, hy_blocks + ML_HEADS, hy_blocks + 2 * ML_HEADS, hy_blocks + 3 * ML_HEADS
    p_spec = lambda off: pl.BlockSpec((1, seq, d), lambda b, h: (b, 0, off + h))
    return pl.pallas_call(
        _mlstm_kernel,
        grid=(bsz, ML_HEADS),
        in_specs=[p_spec(qoff), p_spec(koff), p_spec(voff), p_spec(ooff),
                  pl.BlockSpec((3, d), lambda b, h: (0, h)),
                  pl.BlockSpec((3, d), lambda b, h: (0, ML_HEADS + h)),
                  pl.BlockSpec((1, d), lambda b, h: (0, h)),
                  pl.BlockSpec((1, d), lambda b, h: (0, ML_HEADS + h)),
                  pl.BlockSpec((1, 1, seq, 4), lambda b, h: (b, h, 0, 0)),
                  pl.BlockSpec((1, 1, 4, seq), lambda b, h: (b, h, 0, 0)),
                  pl.BlockSpec((1, d), lambda b, h: (0, h))],
        out_specs=pl.BlockSpec((1, seq, d), lambda b, h: (b, 0, h)),
        out_shape=jax.ShapeDtypeStruct((bsz, seq, D_ML), BF16),
        scratch_shapes=[pltpu.VMEM((seq, d), BF16), pltpu.VMEM((seq, d), BF16),
                        pltpu.VMEM((seq, 4), F32), pltpu.VMEM((4, seq), F32),
                        pltpu.VMEM((seq, d), F32)],
        compiler_params=_cparams(("arbitrary", "arbitrary")),
        name="mlstm")(proj3, proj3, proj3, proj3, conv_w, conv_w, conv_b, conv_b, gcol, grow, norm_w)


def _to_slabs(y):
    n = y.shape[0]
    parts = jnp.stack([y[:, c * LANES:(c + 1) * LANES] for c in range(ROW_CHUNKS)], axis=0)
    return pltpu.einshape("crl->rcl", parts).reshape(n * ROW_CHUNKS, LANES)


def _from_slabs(v):
    n = v.shape[0] // ROW_CHUNKS
    parts = pltpu.einshape("rcl->crl", v.reshape(n, ROW_CHUNKS, LANES))
    return jnp.concatenate([parts[c] for c in range(ROW_CHUNKS)], axis=-1)


def _layer_norm(u, g, b):
    mu = jnp.mean(u, axis=-1, keepdims=True)
    uc = u - mu
    var = jnp.mean(uc * uc, axis=-1, keepdims=True)
    return uc * lax.rsqrt(var + LN_EPS) * g + b


def _outproj_kernel(yh_ref, ym_ref, x_ref, wa_ref, wb_ref, b_ref, g_ref, be_ref, o_ref, oc_ref):
    mix = (jnp.dot(yh_ref[...], wa_ref[...], preferred_element_type=F32)
           + jnp.dot(ym_ref[...], wb_ref[...], preferred_element_type=F32) + b_ref[...])
    y = _layer_norm(DN_ALPHA * x_ref[...] + mix, g_ref[...], be_ref[...])
    o_ref[...] = y
    oc_ref[...] = _to_slabs(y)


def _out_proj_ln(y_hy, y_ml, x2d, w_out_bf, b_out, g, be):
    t = x2d.shape[0]
    tm = OUT_TM
    vec = lambda: pl.BlockSpec((1, D_MODEL), lambda i: (0, 0))
    return pl.pallas_call(
        _outproj_kernel,
        grid=(t // tm,),
        in_specs=[pl.BlockSpec((tm, D_HY), lambda i: (i, 0)),
                  pl.BlockSpec((tm, D_ML), lambda i: (i, 0)),
                  pl.BlockSpec((tm, D_MODEL), lambda i: (i, 0)),
                  _const_spec((D_HY, D_MODEL), lambda i: (0, 0)),
                  _const_spec((D_ML, D_MODEL), lambda i: (1, 0)),
                  vec(), vec(), vec()],
        out_specs=[pl.BlockSpec((tm, D_MODEL), lambda i: (i, 0)),
                   pl.BlockSpec((tm * ROW_CHUNKS, LANES), lambda i: (i, 0))],
        out_shape=[jax.ShapeDtypeStruct((t, D_MODEL), F32),
                   jax.ShapeDtypeStruct((t * ROW_CHUNKS, LANES), F32)],
        compiler_params=_cparams(("arbitrary",)),
        name="out_proj_ln1")(y_hy, y_ml, x2d, w_out_bf, w_out_bf, b_out, g, be)


def _router_kernel(x_ref, w_ref, b_ref, ti_ref, tg_ref, tp_ref, cnt_ref):
    tm = ti_ref.shape[0]

    @pl.when(pl.program_id(0) == 0)
    def _():
        cnt_ref[...] = jnp.zeros_like(cnt_ref)

    logits = jnp.dot(x_ref[...], w_ref[...], precision=HP, preferred_element_type=F32) + b_ref[...]
    lane = lax.broadcasted_iota(I32, (tm, LANES), 1)
    work = logits
    vals, idxs = [], []
    chosen = jnp.zeros((tm, LANES), F32)
    for _ in range(TOP_K):
        mx = jnp.max(work, axis=-1, keepdims=True)
        idx = jnp.min(jnp.where(work == mx, lane, LANES), axis=-1, keepdims=True)
        hit = lane == idx
        vals.append(mx)
        idxs.append(idx)
        chosen = jnp.where(hit, 1.0, chosen)
        work = jnp.where(hit, -jnp.inf, work)
    exps = [jnp.exp(v - vals[0]) for v in vals]
    den = exps[0] + exps[1] + exps[2] + exps[3]
    ri = lax.broadcasted_iota(I32, (tm, tm), 0)
    ci = lax.broadcasted_iota(I32, (tm, tm), 1)
    strict_lower = (ri > ci).astype(BF16)
    carry = cnt_ref[...]
    slot = carry + jnp.dot(strict_lower, chosen.astype(BF16), preferred_element_type=F32)
    ti = jnp.zeros((tm, LANES), I32)
    tg = jnp.zeros((tm, LANES), F32)
    tp = jnp.zeros((tm, LANES), F32)
    for k in range(TOP_K):
        sk = jnp.sum(jnp.where(lane == idxs[k], slot, 0.0), axis=-1, keepdims=True)
        ti = jnp.where(lane == k, idxs[k], ti)
        tg = jnp.where(lane == k, exps[k] / den, tg)
        tp = jnp.where(lane == k, sk, tp)
    ti_ref[...] = ti
    tg_ref[...] = tg
    tp_ref[...] = tp.astype(I32)
    cnt_ref[...] = carry + jnp.sum(chosen, axis=0, keepdims=True)


def _router(x1, w_pad, b_pad):
    t = x1.shape[0]
    tm = RT_TM
    o_spec = lambda: pl.BlockSpec((tm, LANES), lambda i: (i, 0))
    return pl.pallas_call(
        _router_kernel,
        grid=(t // tm,),
        in_specs=[pl.BlockSpec((tm, D_MODEL), lambda i: (i, 0)),
                  pl.BlockSpec((D_MODEL, LANES), lambda i: (0, 0)),
                  pl.BlockSpec((1, LANES), lambda i: (0, 0))],
        out_specs=[o_spec(), o_spec(), o_spec(), pl.BlockSpec((1, LANES), lambda i: (0, 0))],
        out_shape=[jax.ShapeDtypeStruct((t, LANES), I32), jax.ShapeDtypeStruct((t, LANES), F32),
                   jax.ShapeDtypeStruct((t, LANES), I32), jax.ShapeDtypeStruct((1, LANES), F32)],
        compiler_params=_cparams(("arbitrary",)),
        name="router")(x1, w_pad, b_pad)


def _expert_kernel(te_ref, tr_ref, rt_ref, x_hbm, wg_ref, wu_ref, wd_ref, bg_ref, bu_ref, bd_ref, y_hbm,
                   stage_ref, xb_ref, acc_ref, ring_ref, gsem, osem):
    s = pl.program_id(0)
    j = pl.program_id(1)
    n_tiles = pl.num_programs(0)
    nf = pl.num_programs(1)
    rows = tr_ref[s]
    slab = EXP_CHUNK * ROW_CHUNKS

    def row_copy(tok, r):
        return pltpu.make_async_copy(
            x_hbm.at[pl.ds(pl.multiple_of(tok * ROW_CHUNKS, ROW_CHUNKS), ROW_CHUNKS), :],
            stage_ref.at[pl.ds(pl.multiple_of(r * ROW_CHUNKS, ROW_CHUNKS), ROW_CHUNKS), :], gsem)

    def start_gather(tile):
        base = tile * EXP_ROWS

        def body(r, carry):
            row_copy(rt_ref[base + r], r).start()
            return carry

        lax.fori_loop(0, tr_ref[tile], body, 0)

    def wait_gather(tile):
        def body(r, carry):
            row_copy(0, r).wait()
            return carry

        lax.fori_loop(0, tr_ref[tile], body, 0)

    @pl.when(jnp.logical_and(s == 0, j == 0))
    def _():
        stage_ref[...] = jnp.zeros_like(stage_ref)
        start_gather(0)

    @pl.when(rows > 0)
    def _():
        @pl.when(j == 0)
        def _():
            wait_gather(s)
            for i in range(EXP_ROWS // EXP_CHUNK):
                xb_ref[i * EXP_CHUNK:(i + 1) * EXP_CHUNK, :] = _from_slabs(
                    stage_ref[i * slab:(i + 1) * slab, :]).astype(BF16)

        @pl.when(jnp.logical_and(j == 1, s + 1 < n_tiles))
        def _():
            start_gather(jnp.minimum(s + 1, n_tiles - 1))

        xb = xb_ref[...]
        gate = jnp.dot(xb, wg_ref[0].astype(BF16), preferred_element_type=F32) + bg_ref[0]
        up = jnp.dot(xb, wu_ref[0].astype(BF16), preferred_element_type=F32) + bu_ref[0]
        gate = jnp.minimum(gate, SWIGLU_LIMIT)
        up = jnp.clip(up, -SWIGLU_LIMIT, SWIGLU_LIMIT)
        act = (up + 1.0) * (gate * jax.nn.sigmoid(SWIGLU_ALPHA * gate))
        part = jnp.dot(act.astype(BF16), wd_ref[0].astype(BF16), preferred_element_type=F32)

        @pl.when(j == 0)
        def _():
            acc_ref[...] = part + bd_ref[0]

        @pl.when(j > 0)
        def _():
            acc_ref[...] += part

        @pl.when(j == nf - 1)
        def _():
            nchunk = (rows + EXP_CHUNK - 1) // EXP_CHUNK

            def chunk_copy(i, slot):
                dst0 = pl.multiple_of((s * EXP_ROWS + i * EXP_CHUNK) * ROW_CHUNKS, slab)
                return pltpu.make_async_copy(ring_ref.at[slot], y_hbm.at[pl.ds(dst0, slab), :], osem.at[slot])

            def emit(i, carry):
                slot = i % 2

                @pl.when(i >= 2)
                def _():
                    chunk_copy(i - 2, slot).wait()

                r0 = pl.multiple_of(i * EXP_CHUNK, EXP_CHUNK)
                ring_ref[slot] = _to_slabs(acc_ref[pl.ds(r0, EXP_CHUNK), :])
                chunk_copy(i, slot).start()
                return carry

            lax.fori_loop(0, nchunk, emit, 0)
            for back in range(2):
                @pl.when(nchunk > back)
                def _():
                    last = nchunk - 1 - back
                    chunk_copy(last, last % 2).wait()


def _experts(tile_e, tile_rows, row_tok, x1c, w_gu, b_gu, w_down, b_down):
    n_tiles = tile_e.shape[0]
    nf = D_FF // EXP_TF

    def jmap(s, j, tr):
        return jnp.where(tr[s] > 0, j, nf - 1)

    grid_spec = pltpu.PrefetchScalarGridSpec(
        num_scalar_prefetch=3,
        grid=(n_tiles, nf),
        in_specs=[pl.BlockSpec(memory_space=pl.ANY),
                  pl.BlockSpec((1, D_MODEL, EXP_TF), lambda s, j, te, tr, rt: (te[s], 0, jmap(s, j, tr))),
                  pl.BlockSpec((1, D_MODEL, EXP_TF), lambda s, j, te, tr, rt: (te[s], 0, nf + jmap(s, j, tr))),
                  pl.BlockSpec((1, EXP_TF, D_MODEL), lambda s, j, te, tr, rt: (te[s], jmap(s, j, tr), 0)),
                  pl.BlockSpec((1, 1, EXP_TF), lambda s, j, te, tr, rt: (te[s], 0, jmap(s, j, tr))),
                  pl.BlockSpec((1, 1, EXP_TF), lambda s, j, te, tr, rt: (te[s], 0, nf + jmap(s, j, tr))),
                  pl.BlockSpec((1, 1, D_MODEL), lambda s, j, te, tr, rt: (te[s], 0, 0))],
        out_specs=pl.BlockSpec(memory_space=pl.ANY),
        scratch_shapes=[pltpu.VMEM((EXP_ROWS * ROW_CHUNKS, LANES), F32),
                        pltpu.VMEM((EXP_ROWS, D_MODEL), BF16),
                        pltpu.VMEM((EXP_ROWS, D_MODEL), F32),
                        pltpu.VMEM((2, EXP_CHUNK * ROW_CHUNKS, LANES), F32),
                        pltpu.SemaphoreType.DMA(()),
                        pltpu.SemaphoreType.DMA((2,))])
    return pl.pallas_call(
        _expert_kernel,
        grid_spec=grid_spec,
        out_shape=jax.ShapeDtypeStruct((n_tiles * EXP_ROWS * ROW_CHUNKS, LANES), F32),
        compiler_params=_cparams(("arbitrary", "arbitrary")),
        name="experts")(tile_e, tile_rows, row_tok, x1c, w_gu, w_gu, w_down, b_gu, b_gu, b_down)


def _combine_kernel(dest_ref, y_hbm, x_ref, tg_ref, g_ref, be_ref, o_ref, buf_ref, sem):
    tm = o_ref.shape[0]
    i = pl.program_id(0)
    n = pl.num_programs(0)

    def row_copy(src_row, slot, k, t):
        return pltpu.make_async_copy(
            y_hbm.at[pl.ds(pl.multiple_of(src_row * ROW_CHUNKS, ROW_CHUNKS), ROW_CHUNKS), :],
            buf_ref.at[slot, k, pl.ds(pl.multiple_of(t * ROW_CHUNKS, ROW_CHUNKS), ROW_CHUNKS), :],
            sem.at[slot])

    def start_tile(tile, slot):
        base = tile * tm * TOP_K

        def body(t, carry):
            for k in range(TOP_K):
                row_copy(dest_ref[base + t * TOP_K + k], slot, k, t).start()
            return carry

        lax.fori_loop(0, tm, body, 0)

    def wait_tile(slot):
        def body(t, carry):
            for k in range(TOP_K):
                row_copy(0, slot, k, t).wait()
            return carry

        lax.fori_loop(0, tm, body, 0)

    @pl.when(i == 0)
    def _():
        start_tile(0, 0)

    @pl.when(i + 1 < n)
    def _():
        start_tile(jnp.minimum(i + 1, n - 1), (i + 1) % 2)

    slot = i % 2
    wait_tile(slot)
    tg = tg_ref[...]
    ff = jnp.zeros((tm, D_MODEL), F32)
    for k in range(TOP_K):
        ff = ff + tg[:, k:k + 1] * _from_slabs(buf_ref[slot, k])
    o_ref[...] = _layer_norm(DN_ALPHA * x_ref[...] + ff, g_ref[...], be_ref[...])


def _combine_ln(dest_flat, y_buf, x1, tg, g, be):
    t = tg.shape[0]
    tm = CMB_TM
    grid_spec = pltpu.PrefetchScalarGridSpec(
        num_scalar_prefetch=1,
        grid=(t // tm,),
        in_specs=[pl.BlockSpec(memory_space=pl.ANY),
                  pl.BlockSpec((tm, D_MODEL), lambda i, d: (i, 0)),
                  pl.BlockSpec((tm, LANES), lambda i, d: (i, 0)),
                  pl.BlockSpec((1, D_MODEL), lambda i, d: (0, 0)),
                  pl.BlockSpec((1, D_MODEL), lambda i, d: (0, 0))],
        out_specs=pl.BlockSpec((tm, D_MODEL), lambda i, d: (i, 0)),
        scratch_shapes=[pltpu.VMEM((2, TOP_K, tm * ROW_CHUNKS, LANES), F32), pltpu.SemaphoreType.DMA((2,))])
    return pl.pallas_call(
        _combine_kernel,
        grid_spec=grid_spec,
        out_shape=jax.ShapeDtypeStruct((t, D_MODEL), F32),
        compiler_params=_cparams(("arbitrary",)),
        name="combine_ln2")(dest_flat, y_buf, x1, tg, g, be)


def _dft_tables(seq):
    n = 2 * seq
    f = jnp.arange(seq, dtype=I32)
    ang = ((f[:, None] * f[None, :]) % n).astype(F32) * (2.0 * math.pi / n)
    return jnp.cos(ang).astype(BF16), jnp.sin(ang).astype(BF16)


def _filter_features(seq):
    t = jnp.linspace(0.0, 1.0, seq, dtype=F32)[:, None]
    bands = (HY_EMB - 1) // 2
    fb = jnp.linspace(1e-4, bands - 1, bands, dtype=F32)[None]
    w = 2.0 * math.pi * jnp.arange(seq, dtype=F32)[:, None] / seq
    z = jnp.concatenate([t, jnp.cos(fb * w), -jnp.sin(fb * w)], -1)
    return jnp.pad(z, ((0, 0), (0, LANES - HY_EMB)))


def _mixer(x, w_in, b_in, hy_conv_w, hy_conv_b, hy_filt_w1, hy_filt_b1, hy_filt_w2, hy_filt_b2,
           hy_filt_w3, hy_filt_freq, hy_skip, hy_norm_w, ml_conv_w, ml_conv_b, ml_norm_w):
    bsz, seq, _ = x.shape
    t = bsz * seq
    x2d = x.reshape(t, D_MODEL)
    n_main = w_in.shape[1] - N_GATE_COLS
    proj = _in_proj(x2d, w_in[:, :n_main].astype(BF16), b_in[None, :n_main])
    proj3 = proj.reshape(bsz, seq, n_main)
    wg = jnp.pad(w_in[:, n_main:], ((0, 0), (0, LANES - N_GATE_COLS)))
    bg = jnp.pad(b_in[None, n_main:], ((0, 0), (0, LANES - N_GATE_COLS)))
    gates = _gate_proj(x2d, wg, bg)[:, :N_GATE_COLS]
    g5 = gates.reshape(bsz, seq, 4, ML_HEADS)
    gcol = g5.transpose(0, 3, 1, 2)
    grow = g5.transpose(0, 3, 2, 1)

    cmat, smat = _dft_tables(seq)
    zpad = _filter_features(seq)
    w1pad = jnp.pad(hy_filt_w1, ((0, LANES - HY_EMB), (0, 0)))
    deltas = jnp.abs(jnp.linspace(math.log(HY_DECAY_TARGET) / HY_SLOW_PCT,
                                  math.log(HY_DECAY_TARGET) / HY_FAST_PCT, D_HY, dtype=F32))[None]
    kr, ki, kn = _hyena_filters(zpad, w1pad, hy_filt_b1[None], hy_filt_w2, hy_filt_b2[None],
                                hy_filt_freq, hy_filt_w3, deltas, cmat, smat)
    y_hy = _hyena(proj3, hy_conv_w, hy_conv_b[None], cmat, smat, kr, ki, kn, hy_skip, hy_norm_w[None])
    y_ml = _mlstm(proj3, ml_conv_w, ml_conv_b[None], gcol, grow, ml_norm_w[None])
    return y_hy.reshape(t, D_HY), y_ml.reshape(t, D_ML), x2d


def _moe_tables(top_i, slot, counts):
    t = top_i.shape[0]
    n_tiles = N_EXPERTS + (t * TOP_K) // EXP_ROWS
    ntile = (counts + EXP_ROWS - 1) // EXP_ROWS
    ends = jnp.cumsum(ntile)
    starts = ends - ntile
    total = ends[-1]
    s_idx = jnp.arange(n_tiles, dtype=I32)
    valid = s_idx < total
    s_eff = jnp.where(valid, s_idx, jnp.maximum(total - 1, 0))
    tile_e = jnp.minimum(jnp.sum((s_eff[:, None] >= ends[None, :]).astype(I32), axis=1), N_EXPERTS - 1)
    local = s_eff - starts[tile_e]
    tile_rows = jnp.where(valid, jnp.clip(counts[tile_e] - local * EXP_ROWS, 0, EXP_ROWS), 0).astype(I32)
    dest = (starts[top_i] * EXP_ROWS + slot).astype(I32)
    tok = jnp.broadcast_to(jnp.arange(t, dtype=I32)[:, None], dest.shape)
    row_tok = jnp.zeros((n_tiles * EXP_ROWS,), I32).at[dest.reshape(-1)].set(tok.reshape(-1))
    return tile_e.astype(I32), tile_rows, row_tok, dest.reshape(-1)


def kernel(x, w_in, b_in, hy_conv_w, hy_conv_b, hy_filt_w1, hy_filt_b1, hy_filt_w2, hy_filt_b2, hy_filt_w3, hy_filt_freq, hy_skip, hy_norm_w, ml_conv_w, ml_conv_b, ml_norm_w, w_out, b_out, ln1_g, ln1_b, router_w, router_b, w_gu, b_gu, w_down, b_down, ln2_g, ln2_b):
    bsz, seq, _ = x.shape
    l = 0
    y_hy, y_ml, x2d = _mixer(x, w_in[l], b_in[l], hy_conv_w[l], hy_conv_b[l], hy_filt_w1[l], hy_filt_b1[l],
                             hy_filt_w2[l], hy_filt_b2[l], hy_filt_w3[l], hy_filt_freq[l], hy_skip[l],
                             hy_norm_w[l], ml_conv_w[l], ml_conv_b[l], ml_norm_w[l])
    x1, x1c = _out_proj_ln(y_hy, y_ml, x2d, w_out[l].astype(BF16), b_out[l][None], ln1_g[l][None],
                           ln1_b[l][None])
    rw = jnp.pad(router_w[l], ((0, 0), (0, LANES - N_EXPERTS)))
    rb = jnp.pad(router_b[l][None], ((0, 0), (0, LANES - N_EXPERTS)), constant_values=-1e30)
    top_i, top_g, slot, cnt = _router(x1, rw, rb)
    counts = cnt[0, :N_EXPERTS].astype(I32)
    tile_e, tile_rows, row_tok, dest = _moe_tables(top_i[:, :TOP_K], slot[:, :TOP_K], counts)
    y_buf = _experts(tile_e, tile_rows, row_tok, x1c, w_gu[l], b_gu[l][:, None, :], w_down[l],
                     b_down[l][:, None, :])
    out = _combine_ln(dest, y_buf, x1, top_g, ln2_g[l][None], ln2_b[l][None])
    return out.reshape(bsz, seq, D_MODEL)
```

```python
import functools
import math

import jax
import jax.numpy as jnp
from jax import lax
from jax.experimental import pallas as pl
from jax.experimental.pallas import tpu as pltpu

F32 = jnp.float32
BF16 = jnp.bfloat16
I32 = jnp.int32
HP = lax.Precision.HIGHEST

D_MODEL = 2048
D_HY = 1024
D_ML = 1024
ML_HEADS = 8
HEAD_DIM = 128
CHUNK = 128
N_GATE_COLS = 32
HY_EMB = 33
N_EXPERTS = 32
TOP_K = 4
D_FF = 2048
SWIGLU_LIMIT = 7.0
SWIGLU_ALPHA = 1.702
LN_EPS = 1e-5
DN_ALPHA = 2.0 ** 0.25
HY_DECAY_TARGET = 1e-2
HY_FAST_PCT = 0.3
HY_SLOW_PCT = 1.5

LANES = 128
ROW_CHUNKS = D_MODEL // LANES
VMEM_LIMIT = 56 * 1024 * 1024

PROJ_TM = 1024
PROJ_TN = 512
HY_CW = 256
HY_FC = 512
OUT_TM = 256
RT_TM = 256
EXP_ROWS = 1152
EXP_CHUNK = 128
EXP_TF = 256
GATHER_UNROLL = 8
CMB_TM = 128
DFT_TB = 128


def _cparams(sem):
    return pltpu.CompilerParams(dimension_semantics=sem, vmem_limit_bytes=VMEM_LIMIT)


def _const_spec(shape, index_map):
    return pl.BlockSpec(shape, index_map, pipeline_mode=pl.Buffered(1))


def _proj_kernel(x_ref, w_ref, b_ref, o_ref, xb_ref):
    @pl.when(pl.program_id(1) == 0)
    def _():
        xb_ref[...] = x_ref[...].astype(BF16)

    acc = lax.dot_general(xb_ref[...], w_ref[...], (((1,), (1,)), ((), ())), preferred_element_type=F32)
    o_ref[...] = (acc + b_ref[...]).astype(o_ref.dtype)


def _in_proj(x2d, wt_bf, b_row):
    m, k = x2d.shape
    n = wt_bf.shape[0]
    return pl.pallas_call(
        _proj_kernel,
        grid=(m // PROJ_TM, n // PROJ_TN),
        in_specs=[pl.BlockSpec((PROJ_TM, k), lambda i, j: (i, 0)),
                  pl.BlockSpec((PROJ_TN, k), lambda i, j: (j, 0)),
                  pl.BlockSpec((1, PROJ_TN), lambda i, j: (0, j))],
        out_specs=pl.BlockSpec((PROJ_TM, PROJ_TN), lambda i, j: (i, j)),
        out_shape=jax.ShapeDtypeStruct((m, n), BF16),
        scratch_shapes=[pltpu.VMEM((PROJ_TM, k), BF16)],
        compiler_params=_cparams(("arbitrary", "arbitrary")),
        name="in_proj")(x2d, wt_bf, b_row)


def _gate_kernel(x_ref, w_ref, b_ref, o_ref):
    o_ref[...] = lax.dot_general(x_ref[...], w_ref[...], (((1,), (1,)), ((), ())), precision=HP,
                                 preferred_element_type=F32) + b_ref[...]


def _gate_proj(x2d, wt_pad, b_pad):
    m, k = x2d.shape
    tm = 512
    return pl.pallas_call(
        _gate_kernel,
        grid=(m // tm,),
        in_specs=[pl.BlockSpec((tm, k), lambda i: (i, 0)),
                  pl.BlockSpec((LANES, k), lambda i: (0, 0)),
                  pl.BlockSpec((1, LANES), lambda i: (0, 0))],
        out_specs=pl.BlockSpec((tm, LANES), lambda i: (i, 0)),
        out_shape=jax.ShapeDtypeStruct((m, LANES), F32),
        compiler_params=_cparams(("arbitrary",)),
        name="gate_proj")(x2d, wt_pad, b_pad)


def _filter_kernel(z_ref, w1_ref, b1_ref, w2_ref, b2_ref, fq_ref, w3f_ref, w3b_ref, dl_ref,
                   c_ref, s_ref, kr_ref, ki_ref, kn_ref):
    seq = z_ref.shape[0]
    inv_n = 1.0 / (2 * seq)
    z = z_ref[...]
    h = jnp.sin(fq_ref[0:1, :] * (jnp.dot(z, w1_ref[...], precision=HP, preferred_element_type=F32) + b1_ref[...]))
    h = jnp.sin(fq_ref[1:2, :] * (jnp.dot(h, w2_ref[...], precision=HP, preferred_element_type=F32) + b2_ref[...]))
    win = jnp.exp(-z[:, 0:1] * dl_ref[...])
    fwd = jnp.dot(h, w3f_ref[...], precision=HP, preferred_element_type=F32) * win
    bwd = jnp.dot(h, w3b_ref[...], precision=HP, preferred_element_type=F32) * win
    row = lax.broadcasted_iota(I32, fwd.shape, 0)
    bwd = jnp.where(row == 0, 0.0, bwd)
    inv = 1.0 / jnp.sum(jnp.abs(fwd) + jnp.abs(bwd), axis=0, keepdims=True)
    ks = (fwd + bwd) * inv
    kd = (fwd - bwd) * inv
    kr = jnp.dot(c_ref[...], ks.astype(BF16), preferred_element_type=F32)
    ki = -jnp.dot(s_ref[...], kd.astype(BF16), preferred_element_type=F32)
    wf = jnp.where(row == 0, inv_n, 2.0 * inv_n)
    kr_ref[0] = kr * wf
    ki_ref[0] = ki * wf
    sgn = jnp.where((row & 1) == 0, 1.0, -1.0)
    kn_ref[0] = jnp.sum(ks * sgn, axis=0, keepdims=True) * inv_n


def _hyena_filters(zpad, w1pad, b1, w2, b2, freq, w3, deltas, cmat, smat):
    seq = zpad.shape[0]
    nb = D_HY // HY_CW
    hid = w2.shape[0]
    full = lambda shape: pl.BlockSpec(shape, lambda o, c: (0,) * len(shape))
    out_sds = jax.ShapeDtypeStruct((2, seq, D_HY), F32)
    return pl.pallas_call(
        _filter_kernel,
        grid=(2, nb),
        in_specs=[full(zpad.shape), full(w1pad.shape), full(b1.shape), full(w2.shape), full(b2.shape),
                  full(freq.shape),
                  pl.BlockSpec((hid, HY_CW), lambda o, c: (0, o * 2 * nb + c)),
                  pl.BlockSpec((hid, HY_CW), lambda o, c: (0, o * 2 * nb + nb + c)),
                  pl.BlockSpec((1, HY_CW), lambda o, c: (0, c)),
                  _const_spec((seq, seq), lambda o, c: (0, 0)),
                  _const_spec((seq, seq), lambda o, c: (0, 0))],
        out_specs=[pl.BlockSpec((1, seq, HY_CW), lambda o, c: (o, 0, c)),
                   pl.BlockSpec((1, seq, HY_CW), lambda o, c: (o, 0, c)),
                   pl.BlockSpec((1, 1, HY_CW), lambda o, c: (o, 0, c))],
        out_shape=[out_sds, out_sds, jax.ShapeDtypeStruct((2, 1, D_HY), F32)],
        compiler_params=_cparams(("arbitrary", "arbitrary")),
        name="hyena_filters")(zpad, w1pad, b1, w2, b2, freq, w3, w3, deltas, cmat, smat)


def _short_conv(u, w_ref, b_ref, row, seq):
    prev = jnp.where(row == 0, 0.0, pltpu.roll(u, 1, 0))
    nxt = jnp.where(row == seq - 1, 0.0, pltpu.roll(u, seq - 1, 0))
    return w_ref[0:1, :] * prev + w_ref[1:2, :] * u + w_ref[2:3, :] * nxt + b_ref[...]


def _hyena_kernel(uv_ref, u1_ref, u2_ref, wv_ref, w1_ref, w2_ref, bv_ref, b1_ref, b2_ref,
                  c_ref, s_ref, kr_ref, ki_ref, kn_ref, skip_ref, nw_ref, o_ref,
                  z_ref, x_ref, zb_ref, pr_ref, pi_ref, ny_ref):
    seq = uv_ref.shape[1]
    cw = uv_ref.shape[2]
    nblk = seq // HY_FC
    row = lax.broadcasted_iota(I32, (seq, LANES), 0)
    sgn = jnp.where((row & 1) == 0, 1.0, -1.0)
    sgn_blk = sgn[:HY_FC, :]
    groups = [slice(g * LANES, (g + 1) * LANES) for g in range(cw // LANES)]

    def conv_group(u_ref, w_ref, b_ref, gs):
        return _short_conv(u_ref[0, :, gs].astype(F32), w_ref.at[:, gs], b_ref.at[:, gs], row, seq)

    def set_input(z, gs, o):
        z_ref[:, gs] = z
        zb_ref[:, gs] = z.astype(BF16)
        ny_ref[:, gs] = jnp.sum(z * sgn, axis=0, keepdims=True) * kn_ref[o, :, gs]

    def spectrum(o):
        for fb in range(nblk):
            fs = slice(fb * HY_FC, (fb + 1) * HY_FC)
            a = jnp.dot(c_ref[fs, :], zb_ref[...], preferred_element_type=F32)
            b = jnp.dot(s_ref[fs, :], zb_ref[...], preferred_element_type=F32)
            kr = kr_ref[o, fs, :]
            ki = ki_ref[o, fs, :]
            pr_ref[fs, :] = (a * kr + b * ki).astype(BF16)
            pi_ref[fs, :] = (a * ki - b * kr).astype(BF16)

    def conv_rows(tb):
        ts = slice(tb * HY_FC, (tb + 1) * HY_FC)
        y = jnp.dot(c_ref[ts, :], pr_ref[...], preferred_element_type=F32)
        y = y - jnp.dot(s_ref[ts, :], pi_ref[...], preferred_element_type=F32)
        return ts, y + ny_ref[...] * jnp.concatenate([sgn_blk] * (cw // LANES), axis=-1)

    for gs in groups:
        set_input(conv_group(uv_ref, wv_ref, bv_ref, gs), gs, 0)
        x_ref[:, gs] = conv_group(u1_ref, w1_ref, b1_ref, gs)
    spectrum(0)
    for tb in range(nblk):
        ts, y = conv_rows(tb)
        x_ref[ts, :] = x_ref[ts, :] * (y + skip_ref[0:1, :] * z_ref[ts, :])
    for gs in groups:
        set_input(x_ref[:, gs], gs, 1)
        x_ref[:, gs] = conv_group(u2_ref, w2_ref, b2_ref, gs)
    spectrum(1)
    for tb in range(nblk):
        ts, y = conv_rows(tb)
        z = x_ref[ts, :] * (y + skip_ref[1:2, :] * z_ref[ts, :])
        for gs in groups:
            zg = z[:, gs]
            mu = jnp.mean(zg, axis=-1, keepdims=True)
            zc = zg - mu
            var = jnp.mean(zc * zc, axis=-1, keepdims=True)
            o_ref[0, ts, gs] = (zc * lax.rsqrt(var + LN_EPS) * nw_ref[:, gs]).astype(o_ref.dtype)


def _hyena(proj3, conv_w, conv_b, cmat, smat, kr, ki, kn, skip, norm_w):
    bsz, seq, _ = proj3.shape
    nb = D_HY // HY_CW
    u_spec = lambda off: pl.BlockSpec((1, seq, HY_CW), lambda c, b: (b, 0, off + c))
    w_spec = lambda off: pl.BlockSpec((3, HY_CW), lambda c, b: (0, off + c))
    b_spec = lambda off: pl.BlockSpec((1, HY_CW), lambda c, b: (0, off + c))
    return pl.pallas_call(
        _hyena_kernel,
        grid=(nb, bsz),
        in_specs=[u_spec(0), u_spec(nb), u_spec(2 * nb),
                  w_spec(0), w_spec(nb), w_spec(2 * nb),
                  b_spec(0), b_spec(nb), b_spec(2 * nb),
                  _const_spec((seq, seq), lambda c, b: (0, 0)),
                  _const_spec((seq, seq), lambda c, b: (0, 0)),
                  _const_spec((2, seq, HY_CW), lambda c, b: (0, 0, c)),
                  _const_spec((2, seq, HY_CW), lambda c, b: (0, 0, c)),
                  pl.BlockSpec((2, 1, HY_CW), lambda c, b: (0, 0, c)),
                  pl.BlockSpec((2, HY_CW), lambda c, b: (0, c)),
                  pl.BlockSpec((1, HY_CW), lambda c, b: (0, c))],
        out_specs=pl.BlockSpec((1, seq, HY_CW), lambda c, b: (b, 0, c)),
        out_shape=jax.ShapeDtypeStruct((bsz, seq, D_HY), BF16),
        scratch_shapes=[pltpu.VMEM((seq, HY_CW), F32), pltpu.VMEM((seq, HY_CW), F32),
                        pltpu.VMEM((seq, HY_CW), BF16), pltpu.VMEM((seq, HY_CW), BF16),
                        pltpu.VMEM((seq, HY_CW), BF16), pltpu.VMEM((1, HY_CW), F32)],
        compiler_params=_cparams(("arbitrary", "arbitrary")),
        name="hyena")(proj3, proj3, proj3, conv_w, conv_w, conv_w, conv_b, conv_b, conv_b,
                      cmat, smat, kr, ki, kn, skip, norm_w)


def _mlstm_kernel(qp_ref, kp_ref, v_ref, og_ref, wq_ref, wk_ref, bq_ref, bk_ref, gc_ref, gr_ref,
                  nw_ref, o_ref, qb_ref, kb_ref, bcol_ref, brow_ref, hacc_ref):
    seq = qp_ref.shape[1]
    d = qp_ref.shape[2]
    nchunk = seq // CHUNK
    row = lax.broadcasted_iota(I32, (seq, d), 0)
    q = _short_conv(qp_ref[0].astype(F32), wq_ref, bq_ref, row, seq)
    k = _short_conv(kp_ref[0].astype(F32), wk_ref, bk_ref, row, seq)
    qb_ref[...] = (q * jax.nn.sigmoid(q)).astype(BF16)
    kb_ref[...] = ((k * jax.nn.sigmoid(k)) * (d ** -0.5)).astype(BF16)
    hacc_ref[...] = jnp.zeros_like(hacc_ref)

    ti = lax.broadcasted_iota(I32, (CHUNK, CHUNK), 0)
    si = lax.broadcasted_iota(I32, (CHUNK, CHUNK), 1)
    lower = ti >= si
    upper = ti <= si
    lower_f = lower.astype(F32)
    upper_f = upper.astype(F32)

    lf_col = jax.nn.log_sigmoid(gc_ref[0, 0])
    lf_row = jax.nn.log_sigmoid(gr_ref[0, 0])
    col4 = lax.broadcasted_iota(I32, (CHUNK, 4), 1)
    row4 = lax.broadcasted_iota(I32, (4, CHUNK), 0)
    for c in range(nchunk):
        cs = slice(c * CHUNK, (c + 1) * CHUNK)
        blk = lf_col[cs, :]
        pre = jnp.dot(lower_f, blk, precision=HP, preferred_element_type=F32)
        suf = jnp.dot(upper_f, blk, precision=HP, preferred_element_type=F32)
        bcol_ref[cs, :] = jnp.where(col4 < 2, pre, suf)
        blk_r = lf_row[:, cs]
        pre_r = jnp.dot(blk_r, upper_f, precision=HP, preferred_element_type=F32)
        suf_r = jnp.dot(blk_r, lower_f, precision=HP, preferred_element_type=F32)
        brow_ref[:, cs] = jnp.where(row4 < 2, pre_r, suf_r)

    def chunk_step(c, state, direction):
        cmat, nvec, m = state
        r0 = pl.multiple_of(c * CHUNK, CHUNK)
        rs = pl.ds(r0, CHUNK)
        fcol = 2 * direction + 1
        icol = 2 * direction
        qc = qb_ref[rs, :]
        kc = kb_ref[rs, :]
        vc = v_ref[0, rs, :]
        bc = bcol_ref[rs, fcol:fcol + 1]
        ic = gc_ref[0, 0, rs, icol:icol + 1]
        br = brow_ref[fcol:fcol + 1, rs]
        ir = gr_ref[0, 0, icol:icol + 1, rs]
        mask = lower if direction == 0 else upper
        b_last = br[:, CHUNK - 1:CHUNK] if direction == 0 else br[:, 0:1]
        dmat = jnp.where(mask, bc - br + ir, -jnp.inf)
        inter = bc + m
        m_t = jnp.maximum(inter, jnp.max(dmat, axis=-1, keepdims=True))
        p = jnp.exp(dmat - m_t)
        s = lax.dot_general(qc, kc, (((1,), (1,)), ((), ())), preferred_element_type=F32) * p
        inter_w = jnp.exp(inter - m_t)
        cq = lax.dot_general(qc, cmat.astype(BF16), (((1,), (1,)), ((), ())), preferred_element_type=F32)
        num = jnp.dot(s.astype(BF16), vc, preferred_element_type=F32) + inter_w * cq
        nq = jnp.sum(qc.astype(F32) * nvec, axis=-1, keepdims=True)
        den = jnp.sum(s, axis=-1, keepdims=True) + inter_w * nq
        h = num / jnp.maximum(jnp.abs(den), jnp.exp(-m_t))
        g = b_last - bc + ic
        m_new = jnp.maximum(b_last + m, jnp.max(g, axis=0, keepdims=True))
        wg = jnp.exp(g - m_new)
        decay = jnp.exp(b_last + m - m_new)
        wv = (wg * vc.astype(F32)).astype(BF16)
        c_new = decay * cmat + lax.dot_general(wv, kc, (((0,), (0,)), ((), ())), preferred_element_type=F32)
        n_new = decay * nvec + jnp.sum(wg * kc.astype(F32), axis=0, keepdims=True)
        hacc_ref[rs, :] += h
        return (c_new, n_new, m_new)

    def body(i, carry):
        sf, sb = carry
        sf = chunk_step(i, sf, 0)
        sb = chunk_step(nchunk - 1 - i, sb, 1)
        return (sf, sb)

    zero_state = (jnp.zeros((d, d), F32), jnp.zeros((1, d), F32), jnp.zeros((1, 1), F32))
    lax.fori_loop(0, nchunk, body, (zero_state, zero_state))

    h = hacc_ref[...]
    mu = jnp.mean(h, axis=-1, keepdims=True)
    hc = h - mu
    var = jnp.mean(hc * hc, axis=-1, keepdims=True)
    y = hc * lax.rsqrt(var + LN_EPS) * nw_ref[...] * jax.nn.sigmoid(og_ref[0].astype(F32))
    o_ref[0] = y.astype(o_ref.dtype)


def _mlstm(proj3, conv_w, conv_b, gcol, grow, norm_w):
    bsz, seq, _ = proj3.shape
    d = HEAD_DIM
    hy_blocks = 3 * D_HY // d
    qoff, koff, voff, ooff = hy_blocks, hy_blocks + ML_HEADS, hy_blocks + 2 * ML_HEADS, hy_blocks + 3 * ML_HEADS
    p_spec = lambda off: pl.BlockSpec((1, seq, d), lambda b, h: (b, 0, off + h))
    return pl.pallas_call(
        _mlstm_kernel,
        grid=(bsz, ML_HEADS),
        in_specs=[p_spec(qoff), p_spec(koff), p_spec(voff), p_spec(ooff),
                  pl.BlockSpec((3, d), lambda b, h: (0, h)),
                  pl.BlockSpec((3, d), lambda b, h: (0, ML_HEADS + h)),
                  pl.BlockSpec((1, d), lambda b, h: (0, h)),
                  pl.BlockSpec((1, d), lambda b, h: (0, ML_HEADS + h)),
                  pl.BlockSpec((1, 1, seq, 4), lambda b, h: (b, h, 0, 0)),
                  pl.BlockSpec((1, 1, 4, seq), lambda b, h: (b, h, 0, 0)),
                  pl.BlockSpec((1, d), lambda b, h: (0, h))],
        out_specs=pl.BlockSpec((1, seq, d), lambda b, h: (b, 0, h)),
        out_shape=jax.ShapeDtypeStruct((bsz, seq, D_ML), BF16),
        scratch_shapes=[pltpu.VMEM((seq, d), BF16), pltpu.VMEM((seq, d), BF16),
                        pltpu.VMEM((seq, 4), F32), pltpu.VMEM((4, seq), F32),
                        pltpu.VMEM((seq, d), F32)],
        compiler_params=_cparams(("arbitrary", "arbitrary")),
        name="mlstm")(proj3, proj3, proj3, proj3, conv_w, conv_w, conv_b, conv_b, gcol, grow, norm_w)


def _to_slabs(y):
    n = y.shape[0]
    parts = jnp.stack([y[:, c * LANES:(c + 1) * LANES] for c in range(ROW_CHUNKS)], axis=0)
    return pltpu.einshape("crl->rcl", parts).reshape(n * ROW_CHUNKS, LANES)


def _from_slabs(v):
    n = v.shape[0] // ROW_CHUNKS
    parts = pltpu.einshape("rcl->crl", v.reshape(n, ROW_CHUNKS, LANES))
    return jnp.concatenate([parts[c] for c in range(ROW_CHUNKS)], axis=-1)


def _layer_norm(u, g, b):
    mu = jnp.mean(u, axis=-1, keepdims=True)
    uc = u - mu
    var = jnp.mean(uc * uc, axis=-1, keepdims=True)
    return uc * lax.rsqrt(var + LN_EPS) * g + b


def _outproj_kernel(yh_ref, ym_ref, x_ref, wa_ref, wb_ref, b_ref, g_ref, be_ref, o_ref, oc_ref):
    mix = (jnp.dot(yh_ref[...], wa_ref[...], preferred_element_type=F32)
           + jnp.dot(ym_ref[...], wb_ref[...], preferred_element_type=F32) + b_ref[...])
    y = _layer_norm(DN_ALPHA * x_ref[...] + mix, g_ref[...], be_ref[...])
    o_ref[...] = y
    oc_ref[...] = _to_slabs(y)


def _out_proj_ln(y_hy, y_ml, x2d, w_out_bf, b_out, g, be):
    t = x2d.shape[0]
    tm = OUT_TM
    vec = lambda: pl.BlockSpec((1, D_MODEL), lambda i: (0, 0))
    return pl.pallas_call(
        _outproj_kernel,
        grid=(t // tm,),
        in_specs=[pl.BlockSpec((tm, D_HY), lambda i: (i, 0)),
                  pl.BlockSpec((tm, D_ML), lambda i: (i, 0)),
                  pl.BlockSpec((tm, D_MODEL), lambda i: (i, 0)),
                  _const_spec((D_HY, D_MODEL), lambda i: (0, 0)),
                  _const_spec((D_ML, D_MODEL), lambda i: (1, 0)),
                  vec(), vec(), vec()],
        out_specs=[pl.BlockSpec((tm, D_MODEL), lambda i: (i, 0)),
                   pl.BlockSpec((tm * ROW_CHUNKS, LANES), lambda i: (i, 0))],
        out_shape=[jax.ShapeDtypeStruct((t, D_MODEL), F32),
                   jax.ShapeDtypeStruct((t * ROW_CHUNKS, LANES), F32)],
        compiler_params=_cparams(("arbitrary",)),
        name="out_proj_ln1")(y_hy, y_ml, x2d, w_out_bf, w_out_bf, b_out, g, be)


def _router_kernel(x_ref, w_ref, b_ref, ti_ref, tg_ref, tp_ref, cnt_ref):
    tm = ti_ref.shape[0]

    @pl.when(pl.program_id(0) == 0)
    def _():
        cnt_ref[...] = jnp.zeros_like(cnt_ref)

    logits = jnp.dot(x_ref[...], w_ref[...], precision=HP, preferred_element_type=F32) + b_ref[...]
    lane = lax.broadcasted_iota(I32, (tm, LANES), 1)
    work = logits
    vals, idxs = [], []
    chosen = jnp.zeros((tm, LANES), F32)
    for _ in range(TOP_K):
        mx = jnp.max(work, axis=-1, keepdims=True)
        idx = jnp.min(jnp.where(work == mx, lane, LANES), axis=-1, keepdims=True)
        hit = lane == idx
        vals.append(mx)
        idxs.append(idx)
        chosen = jnp.where(hit, 1.0, chosen)
        work = jnp.where(hit, -jnp.inf, work)
    exps = [jnp.exp(v - vals[0]) for v in vals]
    den = exps[0] + exps[1] + exps[2] + exps[3]
    ri = lax.broadcasted_iota(I32, (tm, tm), 0)
    ci = lax.broadcasted_iota(I32, (tm, tm), 1)
    strict_lower = (ri > ci).astype(BF16)
    carry = cnt_ref[...]
    slot = carry + jnp.dot(strict_lower, chosen.astype(BF16), preferred_element_type=F32)
    ti = jnp.zeros((tm, LANES), I32)
    tg = jnp.zeros((tm, LANES), F32)
    tp = jnp.zeros((tm, LANES), F32)
    for k in range(TOP_K):
        sk = jnp.sum(jnp.where(lane == idxs[k], slot, 0.0), axis=-1, keepdims=True)
        ti = jnp.where(lane == k, idxs[k], ti)
        tg = jnp.where(lane == k, exps[k] / den, tg)
        tp = jnp.where(lane == k, sk, tp)
    ti_ref[...] = ti
    tg_ref[...] = tg
    tp_ref[...] = tp.astype(I32)
    cnt_ref[...] = carry + jnp.sum(chosen, axis=0, keepdims=True)


def _router(x1, w_pad, b_pad):
    t = x1.shape[0]
    tm = RT_TM
    o_spec = lambda: pl.BlockSpec((tm, LANES), lambda i: (i, 0))
    return pl.pallas_call(
        _router_kernel,
        grid=(t // tm,),
        in_specs=[pl.BlockSpec((tm, D_MODEL), lambda i: (i, 0)),
                  pl.BlockSpec((D_MODEL, LANES), lambda i: (0, 0)),
                  pl.BlockSpec((1, LANES), lambda i: (0, 0))],
        out_specs=[o_spec(), o_spec(), o_spec(), pl.BlockSpec((1, LANES), lambda i: (0, 0))],
        out_shape=[jax.ShapeDtypeStruct((t, LANES), I32), jax.ShapeDtypeStruct((t, LANES), F32),
                   jax.ShapeDtypeStruct((t, LANES), I32), jax.ShapeDtypeStruct((1, LANES), F32)],
        compiler_params=_cparams(("arbitrary",)),
        name="router")(x1, w_pad, b_pad)


def _expert_kernel(te_ref, tr_ref, tb_ref, rt_ref, x_hbm, wg_ref, wu_ref, wd_ref, bg_ref, bu_ref, bd_ref, y_hbm,
                   stage_ref, xb_ref, acc_ref, ring_ref, gsem, osem):
    s = pl.program_id(0)
    j = pl.program_id(1)
    n_tiles = pl.num_programs(0)
    nf = pl.num_programs(1)
    rows = tr_ref[s]
    slab = EXP_CHUNK * ROW_CHUNKS

    def row_copy(tok, r):
        return pltpu.make_async_copy(
            x_hbm.at[pl.ds(pl.multiple_of(tok * ROW_CHUNKS, ROW_CHUNKS), ROW_CHUNKS), :],
            stage_ref.at[pl.ds(pl.multiple_of(r * ROW_CHUNKS, ROW_CHUNKS), ROW_CHUNKS), :], gsem)

    def start_gather(tile):
        base = tb_ref[tile]
        n = tr_ref[tile]
        ngroup = n // GATHER_UNROLL

        def group(g, carry):
            for u in range(GATHER_UNROLL):
                r = g * GATHER_UNROLL + u
                row_copy(rt_ref[base + r], r).start()
            return carry

        def single(r, carry):
            row_copy(rt_ref[base + r], r).start()
            return carry

        lax.fori_loop(0, ngroup, group, 0)
        lax.fori_loop(ngroup * GATHER_UNROLL, n, single, 0)

    def wait_gather(tile):
        n = tr_ref[tile]
        piece = 1
        while piece <= EXP_ROWS:
            @pl.when((n & piece) != 0)
            def _(piece=piece):
                pltpu.make_async_copy(x_hbm.at[pl.ds(0, piece * ROW_CHUNKS), :],
                                      stage_ref.at[pl.ds(0, piece * ROW_CHUNKS), :], gsem).wait()
            piece *= 2

    @pl.when(jnp.logical_and(s == 0, j == 0))
    def _():
        stage_ref[...] = jnp.zeros_like(stage_ref)
        acc_ref[...] = jnp.zeros_like(acc_ref)
        start_gather(0)

    @pl.when(rows > 0)
    def _():
        @pl.when(j == 0)
        def _():
            wait_gather(s)
            for i in range(EXP_ROWS // EXP_CHUNK):
                xb_ref[i * EXP_CHUNK:(i + 1) * EXP_CHUNK, :] = _from_slabs(
                    stage_ref[i * slab:(i + 1) * slab, :]).astype(BF16)

        @pl.when(jnp.logical_and(j == 1, s + 1 < n_tiles))
        def _():
            start_gather(jnp.minimum(s + 1, n_tiles - 1))

        xb = xb_ref[...]
        gate = jnp.dot(xb, wg_ref[0].astype(BF16), preferred_element_type=F32) + bg_ref[0]
        up = jnp.dot(xb, wu_ref[0].astype(BF16), preferred_element_type=F32) + bu_ref[0]
        gate = jnp.minimum(gate, SWIGLU_LIMIT)
        up = jnp.clip(up, -SWIGLU_LIMIT, SWIGLU_LIMIT)
        act = (up + 1.0) * (gate * jax.nn.sigmoid(SWIGLU_ALPHA * gate))
        part = jnp.dot(act.astype(BF16), wd_ref[0].astype(BF16), preferred_element_type=F32)

        acc_ref[...] = jnp.where(j == 0, jnp.broadcast_to(bd_ref[0], part.shape), acc_ref[...]) + part

        @pl.when(j == nf - 1)
        def _():
            nchunk = (rows + EXP_CHUNK - 1) // EXP_CHUNK

            def chunk_copy(i, slot):
                dst0 = pl.multiple_of((s * EXP_ROWS + i * EXP_CHUNK) * ROW_CHUNKS, slab)
                return pltpu.make_async_copy(ring_ref.at[slot], y_hbm.at[pl.ds(dst0, slab), :], osem.at[slot])

            def emit(i, carry):
                slot = i % 2

                @pl.when(i >= 2)
                def _():
                    chunk_copy(i - 2, slot).wait()

                r0 = pl.multiple_of(i * EXP_CHUNK, EXP_CHUNK)
                ring_ref[slot] = _to_slabs(acc_ref[pl.ds(r0, EXP_CHUNK), :])
                chunk_copy(i, slot).start()
                return carry

            lax.fori_loop(0, nchunk, emit, 0)
            for back in range(2):
                @pl.when(nchunk > back)
                def _():
                    last = nchunk - 1 - back
                    chunk_copy(last, last % 2).wait()


def _experts(tile_e, tile_rows, tile_base, row_tok, x1c, w_gu, b_gu, w_down, b_down):
    n_tiles = tile_e.shape[0]
    nf = D_FF // EXP_TF

    def jmap(s, j, tr):
        return jnp.where(tr[s] > 0, j, nf - 1)

    grid_spec = pltpu.PrefetchScalarGridSpec(
        num_scalar_prefetch=4,
        grid=(n_tiles, nf),
        in_specs=[pl.BlockSpec(memory_space=pl.ANY),
                  pl.BlockSpec((1, D_MODEL, EXP_TF), lambda s, j, te, tr, tb, rt: (te[s], 0, jmap(s, j, tr))),
                  pl.BlockSpec((1, D_MODEL, EXP_TF), lambda s, j, te, tr, tb, rt: (te[s], 0, nf + jmap(s, j, tr))),
                  pl.BlockSpec((1, EXP_TF, D_MODEL), lambda s, j, te, tr, tb, rt: (te[s], jmap(s, j, tr), 0)),
                  pl.BlockSpec((1, 1, EXP_TF), lambda s, j, te, tr, tb, rt: (te[s], 0, jmap(s, j, tr))),
                  pl.BlockSpec((1, 1, EXP_TF), lambda s, j, te, tr, tb, rt: (te[s], 0, nf + jmap(s, j, tr))),
                  pl.BlockSpec((1, 1, D_MODEL), lambda s, j, te, tr, tb, rt: (te[s], 0, 0))],
        out_specs=pl.BlockSpec(memory_space=pl.ANY),
        scratch_shapes=[pltpu.VMEM((EXP_ROWS * ROW_CHUNKS, LANES), F32),
                        pltpu.VMEM((EXP_ROWS, D_MODEL), BF16),
                        pltpu.VMEM((EXP_ROWS, D_MODEL), F32),
                        pltpu.VMEM((2, EXP_CHUNK * ROW_CHUNKS, LANES), F32),
                        pltpu.SemaphoreType.DMA(()),
                        pltpu.SemaphoreType.DMA((2,))])
    return pl.pallas_call(
        _expert_kernel,
        grid_spec=grid_spec,
        out_shape=jax.ShapeDtypeStruct((n_tiles * EXP_ROWS * ROW_CHUNKS, LANES), F32),
        compiler_params=_cparams(("arbitrary", "arbitrary")),
        name="experts")(tile_e, tile_rows, tile_base, row_tok, x1c, w_gu, w_gu, w_down, b_gu, b_gu, b_down)


def _combine_kernel(dest_ref, y_hbm, x_ref, tg_ref, g_ref, be_ref, o_ref, buf_ref, sem):
    tm = o_ref.shape[0]
    i = pl.program_id(0)
    n = pl.num_programs(0)

    def row_copy(src_row, slot, k, t):
        return pltpu.make_async_copy(
            y_hbm.at[pl.ds(pl.multiple_of(src_row * ROW_CHUNKS, ROW_CHUNKS), ROW_CHUNKS), :],
            buf_ref.at[slot, k, pl.ds(pl.multiple_of(t * ROW_CHUNKS, ROW_CHUNKS), ROW_CHUNKS), :],
            sem.at[slot])

    def start_tile(tile, slot):
        base = tile * tm * TOP_K

        def body(t2, carry):
            for u in range(2):
                t = t2 * 2 + u
                for k in range(TOP_K):
                    row_copy(dest_ref[base + t * TOP_K + k], slot, k, t).start()
            return carry

        lax.fori_loop(0, tm // 2, body, 0)

    def wait_tile(slot):
        for k in range(TOP_K):
            pltpu.make_async_copy(y_hbm.at[pl.ds(0, tm * ROW_CHUNKS), :], buf_ref.at[slot, k], sem.at[slot]).wait()

    @pl.when(i == 0)
    def _():
        start_tile(0, 0)

    @pl.when(i + 1 < n)
    def _():
        start_tile(jnp.minimum(i + 1, n - 1), (i + 1) % 2)

    slot = i % 2
    wait_tile(slot)
    tg = tg_ref[...]
    ff = jnp.zeros((tm, D_MODEL), F32)
    for k in range(TOP_K):
        ff = ff + tg[:, k:k + 1] * _from_slabs(buf_ref[slot, k])
    o_ref[...] = _layer_norm(DN_ALPHA * x_ref[...] + ff, g_ref[...], be_ref[...])


def _combine_ln(dest_flat, y_buf, x1, tg, g, be):
    t = tg.shape[0]
    tm = CMB_TM
    grid_spec = pltpu.PrefetchScalarGridSpec(
        num_scalar_prefetch=1,
        grid=(t // tm,),
        in_specs=[pl.BlockSpec(memory_space=pl.ANY),
                  pl.BlockSpec((tm, D_MODEL), lambda i, d: (i, 0)),
                  pl.BlockSpec((tm, LANES), lambda i, d: (i, 0)),
                  pl.BlockSpec((1, D_MODEL), lambda i, d: (0, 0)),
                  pl.BlockSpec((1, D_MODEL), lambda i, d: (0, 0))],
        out_specs=pl.BlockSpec((tm, D_MODEL), lambda i, d: (i, 0)),
        scratch_shapes=[pltpu.VMEM((2, TOP_K, tm * ROW_CHUNKS, LANES), F32), pltpu.SemaphoreType.DMA((2,))])
    return pl.pallas_call(
        _combine_kernel,
        grid_spec=grid_spec,
        out_shape=jax.ShapeDtypeStruct((t, D_MODEL), F32),
        compiler_params=_cparams(("arbitrary",)),
        name="combine_ln2")(dest_flat, y_buf, x1, tg, g, be)


def _dft_kernel(cd_ref, sd_ref, ca_ref, sa_ref, c_ref, s_ref):
    cd, sd = cd_ref[...], sd_ref[...]
    ca, sa = ca_ref[0], sa_ref[0]
    c_ref[...] = (cd * ca - sd * sa).astype(c_ref.dtype)
    s_ref[...] = (sd * ca + cd * sa).astype(s_ref.dtype)


def _dft_tables(seq):
    n = 2 * seq
    nblk = seq // DFT_TB
    t = jnp.arange(seq, dtype=I32)

    def angles(f):
        return ((f[:, None] * t[None, :]) % n).astype(F32) * (2.0 * math.pi / n)

    ang_d = angles(jnp.arange(DFT_TB, dtype=I32))
    ang_a = angles(jnp.arange(nblk, dtype=I32) * DFT_TB)[:, None, :]
    small = pl.BlockSpec((DFT_TB, seq), lambda a: (0, 0))
    base = pl.BlockSpec((1, 1, seq), lambda a: (a, 0, 0))
    out = pl.BlockSpec((DFT_TB, seq), lambda a: (a, 0))
    sds = jax.ShapeDtypeStruct((seq, seq), BF16)
    return pl.pallas_call(
        _dft_kernel, grid=(nblk,), in_specs=[small, small, base, base], out_specs=[out, out],
        out_shape=[sds, sds], compiler_params=_cparams(("arbitrary",)),
        name="dft_tables")(jnp.cos(ang_d), jnp.sin(ang_d), jnp.cos(ang_a), jnp.sin(ang_a))


def _filter_features(seq):
    t = jnp.linspace(0.0, 1.0, seq, dtype=F32)[:, None]
    bands = (HY_EMB - 1) // 2
    fb = jnp.linspace(1e-4, bands - 1, bands, dtype=F32)[None]
    w = 2.0 * math.pi * jnp.arange(seq, dtype=F32)[:, None] / seq
    z = jnp.concatenate([t, jnp.cos(fb * w), -jnp.sin(fb * w)], -1)
    return jnp.pad(z, ((0, 0), (0, LANES - HY_EMB)))


def _mixer(x, w_in, b_in, hy_conv_w, hy_conv_b, hy_filt_w1, hy_filt_b1, hy_filt_w2, hy_filt_b2,
           hy_filt_w3, hy_filt_freq, hy_skip, hy_norm_w, ml_conv_w, ml_conv_b, ml_norm_w):
    bsz, seq, _ = x.shape
    t = bsz * seq
    x2d = x.reshape(t, D_MODEL)
    n_main = w_in.shape[1] - N_GATE_COLS
    w_t = jnp.swapaxes(w_in, 0, 1)
    proj = _in_proj(x2d, w_t[:n_main].astype(BF16), b_in[None, :n_main])
    proj3 = proj.reshape(bsz, seq, n_main)
    wg = jnp.pad(w_t[n_main:], ((0, LANES - N_GATE_COLS), (0, 0)))
    bg = jnp.pad(b_in[None, n_main:], ((0, 0), (0, LANES - N_GATE_COLS)))
    gates = _gate_proj(x2d, wg, bg)[:, :N_GATE_COLS]
    g5 = gates.reshape(bsz, seq, 4, ML_HEADS)
    gcol = g5.transpose(0, 3, 1, 2)
    grow = g5.transpose(0, 3, 2, 1)

    cmat, smat = _dft_tables(seq)
    zpad = _filter_features(seq)
    w1pad = jnp.pad(hy_filt_w1, ((0, LANES - HY_EMB), (0, 0)))
    deltas = jnp.abs(jnp.linspace(math.log(HY_DECAY_TARGET) / HY_SLOW_PCT,
                                  math.log(HY_DECAY_TARGET) / HY_FAST_PCT, D_HY, dtype=F32))[None]
    kr, ki, kn = _hyena_filters(zpad, w1pad, hy_filt_b1[None], hy_filt_w2, hy_filt_b2[None],
                                hy_filt_freq, hy_filt_w3, deltas, cmat, smat)
    y_hy = _hyena(proj3, hy_conv_w, hy_conv_b[None], cmat, smat, kr, ki, kn, hy_skip, hy_norm_w[None])
    y_ml = _mlstm(proj3, ml_conv_w, ml_conv_b[None], gcol, grow, ml_norm_w[None])
    return y_hy.reshape(t, D_HY), y_ml.reshape(t, D_ML), x2d


def _moe_tables(top_i, slot, counts):
    t = top_i.shape[0]
    n_tiles = N_EXPERTS + (t * TOP_K) // EXP_ROWS
    ntile = (counts + EXP_ROWS - 1) // EXP_ROWS
    ends = jnp.cumsum(ntile)
    starts = ends - ntile
    total = ends[-1]
    s_idx = jnp.arange(n_tiles, dtype=I32)
    valid = s_idx < total
    s_eff = jnp.where(valid, s_idx, jnp.maximum(total - 1, 0))
    tile_e = jnp.minimum(jnp.sum((s_eff[:, None] >= ends[None, :]).astype(I32), axis=1), N_EXPERTS - 1)
    local = s_eff - starts[tile_e]
    tile_rows = jnp.where(valid, jnp.clip(counts[tile_e] - local * EXP_ROWS, 0, EXP_ROWS), 0).astype(I32)
    tok = jnp.arange(t, dtype=I32)[:, None]
    row_tok = jnp.sort((top_i * t + tok).reshape(-1)) % t
    first = jnp.cumsum(counts) - counts
    tile_base = (first[tile_e] + local * EXP_ROWS).astype(I32)
    onehot = top_i[:, :, None] == jnp.arange(N_EXPERTS, dtype=I32)
    dest = jnp.sum(jnp.where(onehot, starts * EXP_ROWS, 0), axis=-1) + slot
    return tile_e.astype(I32), tile_rows, tile_base, row_tok.astype(I32), dest.astype(I32).reshape(-1)


def kernel(x, w_in, b_in, hy_conv_w, hy_conv_b, hy_filt_w1, hy_filt_b1, hy_filt_w2, hy_filt_b2, hy_filt_w3, hy_filt_freq, hy_skip, hy_norm_w, ml_conv_w, ml_conv_b, ml_norm_w, w_out, b_out, ln1_g, ln1_b, router_w, router_b, w_gu, b_gu, w_down, b_down, ln2_g, ln2_b):
    bsz, seq, _ = x.shape
    l = 0
    y_hy, y_ml, x2d = _mixer(x, w_in[l], b_in[l], hy_conv_w[l], hy_conv_b[l], hy_filt_w1[l], hy_filt_b1[l],
                             hy_filt_w2[l], hy_filt_b2[l], hy_filt_w3[l], hy_filt_freq[l], hy_skip[l],
                             hy_norm_w[l], ml_conv_w[l], ml_conv_b[l], ml_norm_w[l])
    x1, x1c = _out_proj_ln(y_hy, y_ml, x2d, w_out[l].astype(BF16), b_out[l][None], ln1_g[l][None],
                           ln1_b[l][None])
    rw = jnp.pad(router_w[l], ((0, 0), (0, LANES - N_EXPERTS)))
    rb = jnp.pad(router_b[l][None], ((0, 0), (0, LANES - N_EXPERTS)), constant_values=-1e30)
    top_i, top_g, slot, cnt = _router(x1, rw, rb)
    counts = cnt[0, :N_EXPERTS].astype(I32)
    tile_e, tile_rows, tile_base, row_tok, dest = _moe_tables(top_i[:, :TOP_K], slot[:, :TOP_K], counts)
    y_buf = _experts(tile_e, tile_rows, tile_base, row_tok, x1c, w_gu[l], b_gu[l][:, None, :], w_down[l],
                     b_down[l][:, None, :])
    out = _combine_ln(dest, y_buf, x1, top_g, ln2_g[l][None], ln2_b[l][None])
    return out.reshape(bsz, seq, D_MODEL)
```

```python
import functools
import math

import jax
import jax.numpy as jnp
from jax import lax
from jax.experimental import pallas as pl
from jax.experimental.pallas import tpu as pltpu

F32 = jnp.float32
BF16 = jnp.bfloat16
I32 = jnp.int32
HP = lax.Precision.HIGHEST

D_MODEL = 2048
D_HY = 1024
D_ML = 1024
ML_HEADS = 8
HEAD_DIM = 128
CHUNK = 128
N_GATE_COLS = 32
HY_EMB = 33
N_EXPERTS = 32
TOP_K = 4
D_FF = 2048
SWIGLU_LIMIT = 7.0
SWIGLU_ALPHA = 1.702
LN_EPS = 1e-5
DN_ALPHA = 2.0 ** 0.25
HY_DECAY_TARGET = 1e-2
HY_FAST_PCT = 0.3
HY_SLOW_PCT = 1.5

LANES = 128
ROW_CHUNKS = D_MODEL // LANES
VMEM_LIMIT = 56 * 1024 * 1024

PROJ_TM = 1024
PROJ_TN = 512
HY_CW = 256
HY_FC = 512
OUT_TM = 512
RT_TM = 256
EXP_ROWS = 1152
EXP_CHUNK = 128
EXP_TF = 256
GATHER_UNROLL = 8
CMB_TM = 128
DFT_TB = 128


def _cparams(sem):
    return pltpu.CompilerParams(dimension_semantics=sem, vmem_limit_bytes=VMEM_LIMIT)


def _const_spec(shape, index_map):
    return pl.BlockSpec(shape, index_map, pipeline_mode=pl.Buffered(1))


def _proj_kernel(x_ref, w_ref, b_ref, o_ref, xb_ref):
    @pl.when(pl.program_id(1) == 0)
    def _():
        xb_ref[...] = x_ref[...].astype(BF16)

    acc = lax.dot_general(xb_ref[...], w_ref[...], (((1,), (1,)), ((), ())), preferred_element_type=F32)
    o_ref[...] = (acc + b_ref[...]).astype(o_ref.dtype)


def _in_proj(x2d, wt_bf, b_row):
    m, k = x2d.shape
    n = wt_bf.shape[0]
    return pl.pallas_call(
        _proj_kernel,
        grid=(m // PROJ_TM, n // PROJ_TN),
        in_specs=[pl.BlockSpec((PROJ_TM, k), lambda i, j: (i, 0)),
                  pl.BlockSpec((PROJ_TN, k), lambda i, j: (j, 0)),
                  pl.BlockSpec((1, PROJ_TN), lambda i, j: (0, j))],
        out_specs=pl.BlockSpec((PROJ_TM, PROJ_TN), lambda i, j: (i, j)),
        out_shape=jax.ShapeDtypeStruct((m, n), BF16),
        scratch_shapes=[pltpu.VMEM((PROJ_TM, k), BF16)],
        compiler_params=_cparams(("arbitrary", "arbitrary")),
        name="in_proj")(x2d, wt_bf, b_row)


def _gate_kernel(x_ref, w_ref, b_ref, o_ref):
    o_ref[...] = lax.dot_general(x_ref[...], w_ref[...], (((1,), (1,)), ((), ())), precision=HP,
                                 preferred_element_type=F32) + b_ref[...]


def _gate_proj(x2d, wt_pad, b_pad):
    m, k = x2d.shape
    tm = 512
    return pl.pallas_call(
        _gate_kernel,
        grid=(m // tm,),
        in_specs=[pl.BlockSpec((tm, k), lambda i: (i, 0)),
                  pl.BlockSpec((LANES, k), lambda i: (0, 0)),
                  pl.BlockSpec((1, LANES), lambda i: (0, 0))],
        out_specs=pl.BlockSpec((tm, LANES), lambda i: (i, 0)),
        out_shape=jax.ShapeDtypeStruct((m, LANES), F32),
        compiler_params=_cparams(("arbitrary",)),
        name="gate_proj")(x2d, wt_pad, b_pad)


def _filter_kernel(z_ref, w1_ref, b1_ref, w2_ref, b2_ref, fq_ref, w3f_ref, w3b_ref, dl_ref,
                   c_ref, s_ref, kr_ref, ki_ref, kn_ref, h_ref):
    seq = z_ref.shape[0]
    inv_n = 1.0 / (2 * seq)
    z = z_ref[...]

    @pl.when(jnp.logical_and(pl.program_id(0) == 0, pl.program_id(1) == 0))
    def _():
        h1 = jnp.sin(fq_ref[0:1, :] * (jnp.dot(z, w1_ref[...], precision=HP, preferred_element_type=F32)
                                       + b1_ref[...]))
        h_ref[...] = jnp.sin(fq_ref[1:2, :] * (jnp.dot(h1, w2_ref[...], precision=HP, preferred_element_type=F32)
                                               + b2_ref[...]))

    h = h_ref[...]
    win = jnp.exp(-z[:, 0:1] * dl_ref[...])
    fwd = jnp.dot(h, w3f_ref[...], precision=HP, preferred_element_type=F32) * win
    bwd = jnp.dot(h, w3b_ref[...], precision=HP, preferred_element_type=F32) * win
    row = lax.broadcasted_iota(I32, fwd.shape, 0)
    bwd = jnp.where(row == 0, 0.0, bwd)
    inv = 1.0 / jnp.sum(jnp.abs(fwd) + jnp.abs(bwd), axis=0, keepdims=True)
    ks = (fwd + bwd) * inv
    kd = (fwd - bwd) * inv
    kr = jnp.dot(c_ref[...], ks.astype(BF16), preferred_element_type=F32)
    ki = -jnp.dot(s_ref[...], kd.astype(BF16), preferred_element_type=F32)
    wf = jnp.where(row == 0, inv_n, 2.0 * inv_n)
    kr_ref[0] = kr * wf
    ki_ref[0] = ki * wf
    sgn = jnp.where((row & 1) == 0, 1.0, -1.0)
    kn_ref[0] = jnp.sum(ks * sgn, axis=0, keepdims=True) * inv_n


def _hyena_filters(zpad, w1pad, b1, w2, b2, freq, w3, deltas, cmat, smat):
    seq = zpad.shape[0]
    nb = D_HY // HY_CW
    hid = w2.shape[0]
    full = lambda shape: pl.BlockSpec(shape, lambda o, c: (0,) * len(shape))
    out_sds = jax.ShapeDtypeStruct((2, seq, D_HY), F32)
    return pl.pallas_call(
        _filter_kernel,
        grid=(2, nb),
        in_specs=[full(zpad.shape), full(w1pad.shape), full(b1.shape), full(w2.shape), full(b2.shape),
                  full(freq.shape),
                  pl.BlockSpec((hid, HY_CW), lambda o, c: (0, o * 2 * nb + c)),
                  pl.BlockSpec((hid, HY_CW), lambda o, c: (0, o * 2 * nb + nb + c)),
                  pl.BlockSpec((1, HY_CW), lambda o, c: (0, c)),
                  _const_spec((seq, seq), lambda o, c: (0, 0)),
                  _const_spec((seq, seq), lambda o, c: (0, 0))],
        out_specs=[pl.BlockSpec((1, seq, HY_CW), lambda o, c: (o, 0, c)),
                   pl.BlockSpec((1, seq, HY_CW), lambda o, c: (o, 0, c)),
                   pl.BlockSpec((1, 1, HY_CW), lambda o, c: (o, 0, c))],
        out_shape=[out_sds, out_sds, jax.ShapeDtypeStruct((2, 1, D_HY), F32)],
        scratch_shapes=[pltpu.VMEM((seq, hid), F32)],
        compiler_params=_cparams(("arbitrary", "arbitrary")),
        name="hyena_filters")(zpad, w1pad, b1, w2, b2, freq, w3, w3, deltas, cmat, smat)


def _short_conv(u, w_ref, b_ref, row, seq):
    prev = jnp.where(row == 0, 0.0, pltpu.roll(u, 1, 0))
    nxt = jnp.where(row == seq - 1, 0.0, pltpu.roll(u, seq - 1, 0))
    return w_ref[0:1, :] * prev + w_ref[1:2, :] * u + w_ref[2:3, :] * nxt + b_ref[...]


def _hyena_kernel(uv_ref, u1_ref, u2_ref, wv_ref, w1_ref, w2_ref, bv_ref, b1_ref, b2_ref,
                  c_ref, s_ref, kr_ref, ki_ref, kn_ref, skip_ref, nw_ref, o_ref,
                  z_ref, x_ref, zb_ref, pr_ref, pi_ref, ny_ref):
    seq = uv_ref.shape[1]
    cw = uv_ref.shape[2]
    nblk = seq // HY_FC
    row = lax.broadcasted_iota(I32, (seq, LANES), 0)
    sgn = jnp.where((row & 1) == 0, 1.0, -1.0)
    sgn_blk = sgn[:HY_FC, :]
    groups = [slice(g * LANES, (g + 1) * LANES) for g in range(cw // LANES)]

    def conv_group(u_ref, w_ref, b_ref, gs):
        return _short_conv(u_ref[0, :, gs].astype(F32), w_ref.at[:, gs], b_ref.at[:, gs], row, seq)

    def set_input(z, gs, o):
        z_ref[:, gs] = z
        zb_ref[:, gs] = z.astype(BF16)
        ny_ref[:, gs] = jnp.sum(z * sgn, axis=0, keepdims=True) * kn_ref[o, :, gs]

    def spectrum(o):
        for fb in range(nblk):
            fs = slice(fb * HY_FC, (fb + 1) * HY_FC)
            a = jnp.dot(c_ref[fs, :], zb_ref[...], preferred_element_type=F32)
            b = jnp.dot(s_ref[fs, :], zb_ref[...], preferred_element_type=F32)
            kr = kr_ref[o, fs, :]
            ki = ki_ref[o, fs, :]
            pr_ref[fs, :] = (a * kr + b * ki).astype(BF16)
            pi_ref[fs, :] = (a * ki - b * kr).astype(BF16)

    def conv_rows(tb):
        ts = slice(tb * HY_FC, (tb + 1) * HY_FC)
        y = jnp.dot(c_ref[ts, :], pr_ref[...], preferred_element_type=F32)
        y = y - jnp.dot(s_ref[ts, :], pi_ref[...], preferred_element_type=F32)
        return ts, y + ny_ref[...] * jnp.concatenate([sgn_blk] * (cw // LANES), axis=-1)

    for gs in groups:
        set_input(conv_group(uv_ref, wv_ref, bv_ref, gs), gs, 0)
        x_ref[:, gs] = conv_group(u1_ref, w1_ref, b1_ref, gs)
    spectrum(0)
    for tb in range(nblk):
        ts, y = conv_rows(tb)
        x_ref[ts, :] = x_ref[ts, :] * (y + skip_ref[0:1, :] * z_ref[ts, :])
    for gs in groups:
        set_input(x_ref[:, gs], gs, 1)
        x_ref[:, gs] = conv_group(u2_ref, w2_ref, b2_ref, gs)
    spectrum(1)
    for tb in range(nblk):
        ts, y = conv_rows(tb)
        z = x_ref[ts, :] * (y + skip_ref[1:2, :] * z_ref[ts, :])
        for gs in groups:
            zg = z[:, gs]
            mu = jnp.mean(zg, axis=-1, keepdims=True)
            zc = zg - mu
            var = jnp.mean(zc * zc, axis=-1, keepdims=True)
            o_ref[0, ts, gs] = (zc * lax.rsqrt(var + LN_EPS) * nw_ref[:, gs]).astype(o_ref.dtype)


def _hyena(proj3, conv_w, conv_b, cmat, smat, kr, ki, kn, skip, norm_w):
    bsz, seq, _ = proj3.shape
    nb = D_HY // HY_CW
    u_spec = lambda off: pl.BlockSpec((1, seq, HY_CW), lambda c, b: (b, 0, off + c))
    w_spec = lambda off: pl.BlockSpec((3, HY_CW), lambda c, b: (0, off + c))
    b_spec = lambda off: pl.BlockSpec((1, HY_CW), lambda c, b: (0, off + c))
    return pl.pallas_call(
        _hyena_kernel,
        grid=(nb, bsz),
        in_specs=[u_spec(0), u_spec(nb), u_spec(2 * nb),
                  w_spec(0), w_spec(nb), w_spec(2 * nb),
                  b_spec(0), b_spec(nb), b_spec(2 * nb),
                  _const_spec((seq, seq), lambda c, b: (0, 0)),
                  _const_spec((seq, seq), lambda c, b: (0, 0)),
                  _const_spec((2, seq, HY_CW), lambda c, b: (0, 0, c)),
                  _const_spec((2, seq, HY_CW), lambda c, b: (0, 0, c)),
                  pl.BlockSpec((2, 1, HY_CW), lambda c, b: (0, 0, c)),
                  pl.BlockSpec((2, HY_CW), lambda c, b: (0, c)),
                  pl.BlockSpec((1, HY_CW), lambda c, b: (0, c))],
        out_specs=pl.BlockSpec((1, seq, HY_CW), lambda c, b: (b, 0, c)),
        out_shape=jax.ShapeDtypeStruct((bsz, seq, D_HY), BF16),
        scratch_shapes=[pltpu.VMEM((seq, HY_CW), F32), pltpu.VMEM((seq, HY_CW), F32),
                        pltpu.VMEM((seq, HY_CW), BF16), pltpu.VMEM((seq, HY_CW), BF16),
                        pltpu.VMEM((seq, HY_CW), BF16), pltpu.VMEM((1, HY_CW), F32)],
        compiler_params=_cparams(("arbitrary", "arbitrary")),
        name="hyena")(proj3, proj3, proj3, conv_w, conv_w, conv_w, conv_b, conv_b, conv_b,
                      cmat, smat, kr, ki, kn, skip, norm_w)


def _mlstm_kernel(qp_ref, kp_ref, v_ref, og_ref, wq_ref, wk_ref, bq_ref, bk_ref, gr_ref, gt_ref,
                  nw_ref, o_ref, qb_ref, kb_ref, hacc_ref, cst_ref, nst_ref):
    seq = qp_ref.shape[1]
    d = qp_ref.shape[2]
    nchunk = seq // CHUNK
    row = lax.broadcasted_iota(I32, (seq, d), 0)
    q = _short_conv(qp_ref[0].astype(F32), wq_ref, bq_ref, row, seq)
    k = _short_conv(kp_ref[0].astype(F32), wk_ref, bk_ref, row, seq)
    qb_ref[...] = (q * jax.nn.sigmoid(q)).astype(BF16)
    kb_ref[...] = ((k * jax.nn.sigmoid(k)) * (d ** -0.5)).astype(BF16)

    ti = lax.broadcasted_iota(I32, (CHUNK, CHUNK), 0)
    si = lax.broadcasted_iota(I32, (CHUNK, CHUNK), 1)
    lower = ti >= si
    upper = ti <= si
    lower_f = lower.astype(F32)
    upper_f = upper.astype(F32)
    nt = (((1,), (1,)), ((), ()))
    tn = (((0,), (0,)), ((), ()))
    chunk_rows = [slice(c * CHUNK, (c + 1) * CHUNK) for c in range(nchunk)]

    for direction in range(2):
        f_idx, i_idx = 2 * direction + 1, 2 * direction
        mask = lower if direction == 0 else upper
        order = list(range(nchunk)) if direction == 0 else list(range(nchunk - 1, -1, -1))
        lf_r = jax.nn.log_sigmoid(gr_ref[0, 0, f_idx])
        b_r = jnp.dot(lf_r, upper_f if direction == 0 else lower_f, precision=HP, preferred_element_type=F32)
        rterm = b_r - gr_ref[0, 0, i_idx]
        b_last = jnp.sum(lf_r, axis=-1, keepdims=True)
        lf_c = jax.nn.log_sigmoid(gt_ref[0, 0, f_idx])
        b_c = jnp.dot(lower_f if direction == 0 else upper_f, lf_c, precision=HP, preferred_element_type=F32)
        i_c = gt_ref[0, 0, i_idx]

        bcol, gcol, gmax, rowmax, blast = [], [], [], [], []
        for c in range(nchunk):
            bc = jnp.broadcast_to(b_c[:, c:c + 1], (CHUNK, CHUNK))
            ic = jnp.broadcast_to(i_c[:, c:c + 1], (CHUNK, CHUNK))
            bl = jnp.broadcast_to(b_last[c:c + 1, :], (1, CHUNK))
            dmat = jnp.where(mask, bc - rterm[c:c + 1, :], -jnp.inf)
            g = bl - bc + ic
            bcol.append(bc)
            gcol.append(g)
            blast.append(bl)
            rowmax.append(jnp.max(dmat, axis=-1, keepdims=True))
            gmax.append(jnp.max(g, axis=0, keepdims=True))

        m = jnp.zeros((1, CHUNK), F32)
        m_in, m_out = [None] * nchunk, [None] * nchunk
        for c in order:
            m_in[c] = m
            m = jnp.maximum(blast[c] + m, gmax[c])
            m_out[c] = m

        cmat = jnp.zeros((d, d), F32)
        nvec = jnp.zeros((1, d), F32)
        for c in order:
            kc = kb_ref[chunk_rows[c], :]
            vc = v_ref[0, chunk_rows[c], :]
            cst_ref[c] = cmat.astype(BF16)
            nst_ref[c] = nvec
            wg = jnp.exp(gcol[c] - m_out[c])
            decay = jnp.exp(blast[c] + m_in[c] - m_out[c])
            upd = lax.dot_general((wg * vc.astype(F32)).astype(BF16), kc, tn, preferred_element_type=F32)
            cmat = decay * cmat + upd
            nvec = decay * nvec + jnp.sum(wg * kc.astype(F32), axis=0, keepdims=True)

        for c in range(nchunk):
            rs = chunk_rows[c]
            qc = qb_ref[rs, :]
            kc = kb_ref[rs, :]
            vc = v_ref[0, rs, :]
            dmat = jnp.where(mask, bcol[c] - rterm[c:c + 1, :], -jnp.inf)
            inter = bcol[c] + m_in[c]
            m_t = jnp.maximum(inter, rowmax[c])
            p = jnp.exp(dmat - m_t)
            inter_w = jnp.exp(inter - m_t)
            s = lax.dot_general(qc, kc, nt, preferred_element_type=F32) * p
            cq = lax.dot_general(qc, cst_ref[c], nt, preferred_element_type=F32)
            num = jnp.dot(s.astype(BF16), vc, preferred_element_type=F32) + inter_w * cq
            nq = jnp.sum(qc.astype(F32) * nst_ref[c], axis=-1, keepdims=True)
            den = jnp.sum(s, axis=-1, keepdims=True) + inter_w * nq
            h = num / jnp.maximum(jnp.abs(den), jnp.exp(-m_t))
            if direction == 0:
                hacc_ref[rs, :] = h
            else:
                hacc_ref[rs, :] += h

    h = hacc_ref[...]
    mu = jnp.mean(h, axis=-1, keepdims=True)
    hc = h - mu
    var = jnp.mean(hc * hc, axis=-1, keepdims=True)
    y = hc * lax.rsqrt(var + LN_EPS) * nw_ref[...] * jax.nn.sigmoid(og_ref[0].astype(F32))
    o_ref[0] = y.astype(o_ref.dtype)


def _mlstm(proj3, conv_w, conv_b, grow, gtr, norm_w):
    bsz, seq, _ = proj3.shape
    d = HEAD_DIM
    nchunk = seq // CHUNK
    hy_blocks = 3 * D_HY // d
    qoff, koff, voff, ooff = hy_blocks, hy_blocks + ML_HEADS, hy_blocks + 2 * ML_HEADS, hy_blocks + 3 * ML_HEADS
    p_spec = lambda off: pl.BlockSpec((1, seq, d), lambda b, h: (b, 0, off + h))
    return pl.pallas_call(
        _mlstm_kernel,
        grid=(bsz, ML_HEADS),
        in_specs=[p_spec(qoff), p_spec(koff), p_spec(voff), p_spec(ooff),
                  pl.BlockSpec((3, d), lambda b, h: (0, h)),
                  pl.BlockSpec((3, d), lambda b, h: (0, ML_HEADS + h)),
                  pl.BlockSpec((1, d), lambda b, h: (0, h)),
                  pl.BlockSpec((1, d), lambda b, h: (0, ML_HEADS + h)),
                  pl.BlockSpec((1, 1, 4, nchunk, CHUNK), lambda b, h: (b, h, 0, 0, 0)),
                  pl.BlockSpec((1, 1, 4, CHUNK, nchunk), lambda b, h: (b, h, 0, 0, 0)),
                  pl.BlockSpec((1, d), lambda b, h: (0, h))],
        out_specs=pl.BlockSpec((1, seq, d), lambda b, h: (b, 0, h)),
        out_shape=jax.ShapeDtypeStruct((bsz, seq, D_ML), BF16),
        scratch_shapes=[pltpu.VMEM((seq, d), BF16), pltpu.VMEM((seq, d), BF16),
                        pltpu.VMEM((seq, d), F32),
                        pltpu.VMEM((nchunk, d, d), BF16), pltpu.VMEM((nchunk, 1, d), F32)],
        compiler_params=_cparams(("arbitrary", "arbitrary")),
        name="mlstm")(proj3, proj3, proj3, proj3, conv_w, conv_w, conv_b, conv_b, grow, gtr, norm_w)


def _to_slabs(y):
    n = y.shape[0]
    parts = jnp.stack([y[:, c * LANES:(c + 1) * LANES] for c in range(ROW_CHUNKS)], axis=0)
    return pltpu.einshape("crl->rcl", parts).reshape(n * ROW_CHUNKS, LANES)


def _from_slabs(v):
    n = v.shape[0] // ROW_CHUNKS
    parts = pltpu.einshape("rcl->crl", v.reshape(n, ROW_CHUNKS, LANES))
    return jnp.concatenate([parts[c] for c in range(ROW_CHUNKS)], axis=-1)


def _layer_norm(u, g, b):
    mu = jnp.mean(u, axis=-1, keepdims=True)
    uc = u - mu
    var = jnp.mean(uc * uc, axis=-1, keepdims=True)
    return uc * lax.rsqrt(var + LN_EPS) * g + b


def _outproj_kernel(yh_ref, ym_ref, x_ref, wa_ref, wb_ref, b_ref, g_ref, be_ref, o_ref, oc_ref):
    mix = (jnp.dot(yh_ref[...], wa_ref[...], preferred_element_type=F32)
           + jnp.dot(ym_ref[...], wb_ref[...], preferred_element_type=F32) + b_ref[...])
    y = _layer_norm(DN_ALPHA * x_ref[...] + mix, g_ref[...], be_ref[...])
    o_ref[...] = y
    oc_ref[...] = _to_slabs(y)


def _out_proj_ln(y_hy, y_ml, x2d, w_out_bf, b_out, g, be):
    t = x2d.shape[0]
    tm = OUT_TM
    vec = lambda: pl.BlockSpec((1, D_MODEL), lambda i: (0, 0))
    return pl.pallas_call(
        _outproj_kernel,
        grid=(t // tm,),
        in_specs=[pl.BlockSpec((tm, D_HY), lambda i: (i, 0)),
                  pl.BlockSpec((tm, D_ML), lambda i: (i, 0)),
                  pl.BlockSpec((tm, D_MODEL), lambda i: (i, 0)),
                  _const_spec((D_HY, D_MODEL), lambda i: (0, 0)),
                  _const_spec((D_ML, D_MODEL), lambda i: (1, 0)),
                  vec(), vec(), vec()],
        out_specs=[pl.BlockSpec((tm, D_MODEL), lambda i: (i, 0)),
                   pl.BlockSpec((tm * ROW_CHUNKS, LANES), lambda i: (i, 0))],
        out_shape=[jax.ShapeDtypeStruct((t, D_MODEL), F32),
                   jax.ShapeDtypeStruct((t * ROW_CHUNKS, LANES), F32)],
        compiler_params=_cparams(("arbitrary",)),
        name="out_proj_ln1")(y_hy, y_ml, x2d, w_out_bf, w_out_bf, b_out, g, be)


def _router_kernel(x_ref, w_ref, b_ref, ti_ref, tg_ref, tp_ref, cnt_ref):
    tm = ti_ref.shape[0]

    @pl.when(pl.program_id(0) == 0)
    def _():
        cnt_ref[...] = jnp.zeros_like(cnt_ref)

    logits = jnp.dot(x_ref[...], w_ref[...], precision=HP, preferred_element_type=F32) + b_ref[...]
    lane = lax.broadcasted_iota(I32, (tm, LANES), 1)
    work = logits
    vals, idxs = [], []
    chosen = jnp.zeros((tm, LANES), F32)
    for _ in range(TOP_K):
        mx = jnp.max(work, axis=-1, keepdims=True)
        idx = jnp.min(jnp.where(work == mx, lane, LANES), axis=-1, keepdims=True)
        hit = lane == idx
        vals.append(mx)
        idxs.append(idx)
        chosen = jnp.where(hit, 1.0, chosen)
        work = jnp.where(hit, -jnp.inf, work)
    exps = [jnp.exp(v - vals[0]) for v in vals]
    den = exps[0] + exps[1] + exps[2] + exps[3]
    ri = lax.broadcasted_iota(I32, (tm, tm), 0)
    ci = lax.broadcasted_iota(I32, (tm, tm), 1)
    strict_lower = (ri > ci).astype(BF16)
    carry = cnt_ref[...]
    slot = carry + jnp.dot(strict_lower, chosen.astype(BF16), preferred_element_type=F32)
    ti = jnp.zeros((tm, LANES), I32)
    tg = jnp.zeros((tm, LANES), F32)
    tp = jnp.zeros((tm, LANES), F32)
    for k in range(TOP_K):
        sk = jnp.sum(jnp.where(lane == idxs[k], slot, 0.0), axis=-1, keepdims=True)
        ti = jnp.where(lane == k, idxs[k], ti)
        tg = jnp.where(lane == k, exps[k] / den, tg)
        tp = jnp.where(lane == k, sk, tp)
    ti_ref[...] = ti
    tg_ref[...] = tg
    tp_ref[...] = tp.astype(I32)
    cnt_ref[...] = carry + jnp.sum(chosen, axis=0, keepdims=True)


def _router(x1, w_pad, b_pad):
    t = x1.shape[0]
    tm = RT_TM
    o_spec = lambda: pl.BlockSpec((tm, LANES), lambda i: (i, 0))
    return pl.pallas_call(
        _router_kernel,
        grid=(t // tm,),
        in_specs=[pl.BlockSpec((tm, D_MODEL), lambda i: (i, 0)),
                  pl.BlockSpec((D_MODEL, LANES), lambda i: (0, 0)),
                  pl.BlockSpec((1, LANES), lambda i: (0, 0))],
        out_specs=[o_spec(), o_spec(), o_spec(), pl.BlockSpec((1, LANES), lambda i: (0, 0))],
        out_shape=[jax.ShapeDtypeStruct((t, LANES), I32), jax.ShapeDtypeStruct((t, LANES), F32),
                   jax.ShapeDtypeStruct((t, LANES), I32), jax.ShapeDtypeStruct((1, LANES), F32)],
        compiler_params=_cparams(("arbitrary",)),
        name="router")(x1, w_pad, b_pad)


def _expert_kernel(te_ref, tr_ref, tb_ref, rt_ref, x_hbm, wg_ref, wu_ref, wd_ref, bg_ref, bu_ref, bd_ref, y_hbm,
                   stage_ref, xb_ref, acc_ref, ring_ref, gsem, osem):
    s = pl.program_id(0)
    j = pl.program_id(1)
    n_tiles = pl.num_programs(0)
    nf = pl.num_programs(1)
    rows = tr_ref[s]
    slab = EXP_CHUNK * ROW_CHUNKS

    def row_copy(tok, r):
        return pltpu.make_async_copy(
            x_hbm.at[pl.ds(pl.multiple_of(tok * ROW_CHUNKS, ROW_CHUNKS), ROW_CHUNKS), :],
            stage_ref.at[pl.ds(pl.multiple_of(r * ROW_CHUNKS, ROW_CHUNKS), ROW_CHUNKS), :], gsem)

    def start_gather(tile):
        base = tb_ref[tile]
        n = tr_ref[tile]
        ngroup = n // GATHER_UNROLL

        def group(g, carry):
            for u in range(GATHER_UNROLL):
                r = g * GATHER_UNROLL + u
                row_copy(rt_ref[base + r], r).start()
            return carry

        def single(r, carry):
            row_copy(rt_ref[base + r], r).start()
            return carry

        lax.fori_loop(0, ngroup, group, 0)
        lax.fori_loop(ngroup * GATHER_UNROLL, n, single, 0)

    def wait_gather(tile):
        n = tr_ref[tile]
        piece = 1
        while piece <= EXP_ROWS:
            @pl.when((n & piece) != 0)
            def _(piece=piece):
                pltpu.make_async_copy(x_hbm.at[pl.ds(0, piece * ROW_CHUNKS), :],
                                      stage_ref.at[pl.ds(0, piece * ROW_CHUNKS), :], gsem).wait()
            piece *= 2

    @pl.when(jnp.logical_and(s == 0, j == 0))
    def _():
        stage_ref[...] = jnp.zeros_like(stage_ref)
        acc_ref[...] = jnp.zeros_like(acc_ref)
        start_gather(0)

    @pl.when(rows > 0)
    def _():
        @pl.when(j == 0)
        def _():
            wait_gather(s)
            for i in range(EXP_ROWS // EXP_CHUNK):
                xb_ref[i * EXP_CHUNK:(i + 1) * EXP_CHUNK, :] = _from_slabs(
                    stage_ref[i * slab:(i + 1) * slab, :]).astype(BF16)

        @pl.when(jnp.logical_and(j == 1, s + 1 < n_tiles))
        def _():
            start_gather(jnp.minimum(s + 1, n_tiles - 1))

        xb = xb_ref[...]
        gate = jnp.dot(xb, wg_ref[0].astype(BF16), preferred_element_type=F32) + bg_ref[0]
        up = jnp.dot(xb, wu_ref[0].astype(BF16), preferred_element_type=F32) + bu_ref[0]
        gate = jnp.minimum(gate, SWIGLU_LIMIT)
        up = jnp.clip(up, -SWIGLU_LIMIT, SWIGLU_LIMIT)
        act = (up + 1.0) * (gate * jax.nn.sigmoid(SWIGLU_ALPHA * gate))
        part = jnp.dot(act.astype(BF16), wd_ref[0].astype(BF16), preferred_element_type=F32)

        acc_ref[...] = jnp.where(j == 0, jnp.broadcast_to(bd_ref[0], part.shape), acc_ref[...]) + part

        @pl.when(j == nf - 1)
        def _():
            nchunk = (rows + EXP_CHUNK - 1) // EXP_CHUNK

            def chunk_copy(i, slot):
                dst0 = pl.multiple_of((s * EXP_ROWS + i * EXP_CHUNK) * ROW_CHUNKS, slab)
                return pltpu.make_async_copy(ring_ref.at[slot], y_hbm.at[pl.ds(dst0, slab), :], osem.at[slot])

            def emit(i, carry):
                slot = i % 2

                @pl.when(i >= 2)
                def _():
                    chunk_copy(i - 2, slot).wait()

                r0 = pl.multiple_of(i * EXP_CHUNK, EXP_CHUNK)
                ring_ref[slot] = _to_slabs(acc_ref[pl.ds(r0, EXP_CHUNK), :])
                chunk_copy(i, slot).start()
                return carry

            lax.fori_loop(0, nchunk, emit, 0)
            for back in range(2):
                @pl.when(nchunk > back)
                def _():
                    last = nchunk - 1 - back
                    chunk_copy(last, last % 2).wait()


def _experts(tile_e, tile_rows, tile_base, row_tok, x1c, w_gu, b_gu, w_down, b_down):
    n_tiles = tile_e.shape[0]
    nf = D_FF // EXP_TF

    def jmap(s, j, tr):
        return jnp.where(tr[s] > 0, j, nf - 1)

    grid_spec = pltpu.PrefetchScalarGridSpec(
        num_scalar_prefetch=4,
        grid=(n_tiles, nf),
        in_specs=[pl.BlockSpec(memory_space=pl.ANY),
                  pl.BlockSpec((1, D_MODEL, EXP_TF), lambda s, j, te, tr, tb, rt: (te[s], 0, jmap(s, j, tr))),
                  pl.BlockSpec((1, D_MODEL, EXP_TF), lambda s, j, te, tr, tb, rt: (te[s], 0, nf + jmap(s, j, tr))),
                  pl.BlockSpec((1, EXP_TF, D_MODEL), lambda s, j, te, tr, tb, rt: (te[s], jmap(s, j, tr), 0)),
                  pl.BlockSpec((1, 1, EXP_TF), lambda s, j, te, tr, tb, rt: (te[s], 0, jmap(s, j, tr))),
                  pl.BlockSpec((1, 1, EXP_TF), lambda s, j, te, tr, tb, rt: (te[s], 0, nf + jmap(s, j, tr))),
                  pl.BlockSpec((1, 1, D_MODEL), lambda s, j, te, tr, tb, rt: (te[s], 0, 0))],
        out_specs=pl.BlockSpec(memory_space=pl.ANY),
        scratch_shapes=[pltpu.VMEM((EXP_ROWS * ROW_CHUNKS, LANES), F32),
                        pltpu.VMEM((EXP_ROWS, D_MODEL), BF16),
                        pltpu.VMEM((EXP_ROWS, D_MODEL), F32),
                        pltpu.VMEM((2, EXP_CHUNK * ROW_CHUNKS, LANES), F32),
                        pltpu.SemaphoreType.DMA(()),
                        pltpu.SemaphoreType.DMA((2,))])
    return pl.pallas_call(
        _expert_kernel,
        grid_spec=grid_spec,
        out_shape=jax.ShapeDtypeStruct((n_tiles * EXP_ROWS * ROW_CHUNKS, LANES), F32),
        compiler_params=_cparams(("arbitrary", "arbitrary")),
        name="experts")(tile_e, tile_rows, tile_base, row_tok, x1c, w_gu, w_gu, w_down, b_gu, b_gu, b_down)


def _combine_kernel(dest_ref, y_hbm, x_ref, tg_ref, g_ref, be_ref, o_ref, buf_ref, sem):
    tm = o_ref.shape[0]
    i = pl.program_id(0)
    n = pl.num_programs(0)

    def row_copy(src_row, slot, k, t):
        return pltpu.make_async_copy(
            y_hbm.at[pl.ds(pl.multiple_of(src_row * ROW_CHUNKS, ROW_CHUNKS), ROW_CHUNKS), :],
            buf_ref.at[slot, k, pl.ds(pl.multiple_of(t * ROW_CHUNKS, ROW_CHUNKS), ROW_CHUNKS), :],
            sem.at[slot])

    def start_tile(tile, slot):
        base = tile * tm * TOP_K

        def body(t2, carry):
            for u in range(2):
                t = t2 * 2 + u
                for k in range(TOP_K):
                    row_copy(dest_ref[base + t * TOP_K + k], slot, k, t).start()
            return carry

        lax.fori_loop(0, tm // 2, body, 0)

    def wait_tile(slot):
        for k in range(TOP_K):
            pltpu.make_async_copy(y_hbm.at[pl.ds(0, tm * ROW_CHUNKS), :], buf_ref.at[slot, k], sem.at[slot]).wait()

    @pl.when(i == 0)
    def _():
        start_tile(0, 0)

    @pl.when(i + 1 < n)
    def _():
        start_tile(jnp.minimum(i + 1, n - 1), (i + 1) % 2)

    slot = i % 2
    wait_tile(slot)
    tg = tg_ref[...]
    ff = jnp.zeros((tm, D_MODEL), F32)
    for k in range(TOP_K):
        ff = ff + tg[:, k:k + 1] * _from_slabs(buf_ref[slot, k])
    o_ref[...] = _layer_norm(DN_ALPHA * x_ref[...] + ff, g_ref[...], be_ref[...])


def _combine_ln(dest_flat, y_buf, x1, tg, g, be):
    t = tg.shape[0]
    tm = CMB_TM
    grid_spec = pltpu.PrefetchScalarGridSpec(
        num_scalar_prefetch=1,
        grid=(t // tm,),
        in_specs=[pl.BlockSpec(memory_space=pl.ANY),
                  pl.BlockSpec((tm, D_MODEL), lambda i, d: (i, 0)),
                  pl.BlockSpec((tm, LANES), lambda i, d: (i, 0)),
                  pl.BlockSpec((1, D_MODEL), lambda i, d: (0, 0)),
                  pl.BlockSpec((1, D_MODEL), lambda i, d: (0, 0))],
        out_specs=pl.BlockSpec((tm, D_MODEL), lambda i, d: (i, 0)),
        scratch_shapes=[pltpu.VMEM((2, TOP_K, tm * ROW_CHUNKS, LANES), F32), pltpu.SemaphoreType.DMA((2,))])
    return pl.pallas_call(
        _combine_kernel,
        grid_spec=grid_spec,
        out_shape=jax.ShapeDtypeStruct((t, D_MODEL), F32),
        compiler_params=_cparams(("arbitrary",)),
        name="combine_ln2")(dest_flat, y_buf, x1, tg, g, be)


def _dft_kernel(cd_ref, sd_ref, ca_ref, sa_ref, c_ref, s_ref):
    cd, sd = cd_ref[...], sd_ref[...]
    ca, sa = ca_ref[0], sa_ref[0]
    c_ref[...] = (cd * ca - sd * sa).astype(c_ref.dtype)
    s_ref[...] = (sd * ca + cd * sa).astype(s_ref.dtype)


def _dft_tables(seq):
    n = 2 * seq
    nblk = seq // DFT_TB
    t = jnp.arange(seq, dtype=I32)

    def angles(f):
        return ((f[:, None] * t[None, :]) % n).astype(F32) * (2.0 * math.pi / n)

    ang_d = angles(jnp.arange(DFT_TB, dtype=I32))
    ang_a = angles(jnp.arange(nblk, dtype=I32) * DFT_TB)[:, None, :]
    small = pl.BlockSpec((DFT_TB, seq), lambda a: (0, 0))
    base = pl.BlockSpec((1, 1, seq), lambda a: (a, 0, 0))
    out = pl.BlockSpec((DFT_TB, seq), lambda a: (a, 0))
    sds = jax.ShapeDtypeStruct((seq, seq), BF16)
    return pl.pallas_call(
        _dft_kernel, grid=(nblk,), in_specs=[small, small, base, base], out_specs=[out, out],
        out_shape=[sds, sds], compiler_params=_cparams(("arbitrary",)),
        name="dft_tables")(jnp.cos(ang_d), jnp.sin(ang_d), jnp.cos(ang_a), jnp.sin(ang_a))


def _filter_features(seq):
    t = jnp.linspace(0.0, 1.0, seq, dtype=F32)[:, None]
    bands = (HY_EMB - 1) // 2
    fb = jnp.linspace(1e-4, bands - 1, bands, dtype=F32)[None]
    w = 2.0 * math.pi * jnp.arange(seq, dtype=F32)[:, None] / seq
    z = jnp.concatenate([t, jnp.cos(fb * w), -jnp.sin(fb * w)], -1)
    return jnp.pad(z, ((0, 0), (0, LANES - HY_EMB)))


def _mixer(x, w_in, b_in, hy_conv_w, hy_conv_b, hy_filt_w1, hy_filt_b1, hy_filt_w2, hy_filt_b2,
           hy_filt_w3, hy_filt_freq, hy_skip, hy_norm_w, ml_conv_w, ml_conv_b, ml_norm_w):
    bsz, seq, _ = x.shape
    t = bsz * seq
    x2d = x.reshape(t, D_MODEL)
    n_main = w_in.shape[1] - N_GATE_COLS
    w_t = jnp.swapaxes(w_in, 0, 1)
    proj = _in_proj(x2d, w_t[:n_main].astype(BF16), b_in[None, :n_main])
    proj3 = proj.reshape(bsz, seq, n_main)
    wg = jnp.pad(w_t[n_main:], ((0, LANES - N_GATE_COLS), (0, 0)))
    bg = jnp.pad(b_in[None, n_main:], ((0, 0), (0, LANES - N_GATE_COLS)))
    gates = _gate_proj(x2d, wg, bg)[:, :N_GATE_COLS]
    g5 = gates.reshape(bsz, seq, 4, ML_HEADS)
    grow = g5.transpose(0, 3, 2, 1).reshape(bsz, ML_HEADS, 4, seq // CHUNK, CHUNK)
    gtr = grow.transpose(0, 1, 2, 4, 3)

    cmat, smat = _dft_tables(seq)
    zpad = _filter_features(seq)
    w1pad = jnp.pad(hy_filt_w1, ((0, LANES - HY_EMB), (0, 0)))
    deltas = jnp.abs(jnp.linspace(math.log(HY_DECAY_TARGET) / HY_SLOW_PCT,
                                  math.log(HY_DECAY_TARGET) / HY_FAST_PCT, D_HY, dtype=F32))[None]
    kr, ki, kn = _hyena_filters(zpad, w1pad, hy_filt_b1[None], hy_filt_w2, hy_filt_b2[None],
                                hy_filt_freq, hy_filt_w3, deltas, cmat, smat)
    y_hy = _hyena(proj3, hy_conv_w, hy_conv_b[None], cmat, smat, kr, ki, kn, hy_skip, hy_norm_w[None])
    y_ml = _mlstm(proj3, ml_conv_w, ml_conv_b[None], grow, gtr, ml_norm_w[None])
    return y_hy.reshape(t, D_HY), y_ml.reshape(t, D_ML), x2d


def _moe_tables(top_i, slot, counts):
    t = top_i.shape[0]
    n_tiles = N_EXPERTS + (t * TOP_K) // EXP_ROWS
    ntile = (counts + EXP_ROWS - 1) // EXP_ROWS
    ends = jnp.cumsum(ntile)
    starts = ends - ntile
    total = ends[-1]
    s_idx = jnp.arange(n_tiles, dtype=I32)
    valid = s_idx < total
    s_eff = jnp.where(valid, s_idx, jnp.maximum(total - 1, 0))
    tile_e = jnp.minimum(jnp.sum((s_eff[:, None] >= ends[None, :]).astype(I32), axis=1), N_EXPERTS - 1)
    local = s_eff - starts[tile_e]
    tile_rows = jnp.where(valid, jnp.clip(counts[tile_e] - local * EXP_ROWS, 0, EXP_ROWS), 0).astype(I32)
    tok = jnp.arange(t, dtype=I32)[:, None]
    row_tok = jnp.sort((top_i * t + tok).reshape(-1)) % t
    first = jnp.cumsum(counts) - counts
    tile_base = (first[tile_e] + local * EXP_ROWS).astype(I32)
    onehot = top_i[:, :, None] == jnp.arange(N_EXPERTS, dtype=I32)
    dest = jnp.sum(jnp.where(onehot, starts * EXP_ROWS, 0), axis=-1) + slot
    return tile_e.astype(I32), tile_rows, tile_base, row_tok.astype(I32), dest.astype(I32).reshape(-1)


def kernel(x, w_in, b_in, hy_conv_w, hy_conv_b, hy_filt_w1, hy_filt_b1, hy_filt_w2, hy_filt_b2, hy_filt_w3, hy_filt_freq, hy_skip, hy_norm_w, ml_conv_w, ml_conv_b, ml_norm_w, w_out, b_out, ln1_g, ln1_b, router_w, router_b, w_gu, b_gu, w_down, b_down, ln2_g, ln2_b):
    bsz, seq, _ = x.shape
    l = 0
    y_hy, y_ml, x2d = _mixer(x, w_in[l], b_in[l], hy_conv_w[l], hy_conv_b[l], hy_filt_w1[l], hy_filt_b1[l],
                             hy_filt_w2[l], hy_filt_b2[l], hy_filt_w3[l], hy_filt_freq[l], hy_skip[l],
                             hy_norm_w[l], ml_conv_w[l], ml_conv_b[l], ml_norm_w[l])
    x1, x1c = _out_proj_ln(y_hy, y_ml, x2d, w_out[l].astype(BF16), b_out[l][None], ln1_g[l][None],
                           ln1_b[l][None])
    rw = jnp.pad(router_w[l], ((0, 0), (0, LANES - N_EXPERTS)))
    rb = jnp.pad(router_b[l][None], ((0, 0), (0, LANES - N_EXPERTS)), constant_values=-1e30)
    top_i, top_g, slot, cnt = _router(x1, rw, rb)
    counts = cnt[0, :N_EXPERTS].astype(I32)
    tile_e, tile_rows, tile_base, row_tok, dest = _moe_tables(top_i[:, :TOP_K], slot[:, :TOP_K], counts)
    y_buf = _experts(tile_e, tile_rows, tile_base, row_tok, x1c, w_gu[l], b_gu[l][:, None, :], w_down[l],
                     b_down[l][:, None, :])
    out = _combine_ln(dest, y_buf, x1, top_g, ln2_g[l][None], ln2_b[l][None])
    return out.reshape(bsz, seq, D_MODEL)
```

```python
import functools
import math

import jax
import jax.numpy as jnp
from jax import lax
from jax.experimental import pallas as pl
from jax.experimental.pallas import tpu as pltpu

F32 = jnp.float32
BF16 = jnp.bfloat16
I32 = jnp.int32
HP = lax.Precision.HIGHEST

D_MODEL = 2048
D_HY = 1024
D_ML = 1024
ML_HEADS = 8
HEAD_DIM = 128
CHUNK = 128
N_GATE_COLS = 32
HY_EMB = 33
N_EXPERTS = 32
TOP_K = 4
D_FF = 2048
SWIGLU_LIMIT = 7.0
SWIGLU_ALPHA = 1.702
LN_EPS = 1e-5
DN_ALPHA = 2.0 ** 0.25
HY_DECAY_TARGET = 1e-2
HY_FAST_PCT = 0.3
HY_SLOW_PCT = 1.5

LANES = 128
ROW_CHUNKS = D_MODEL // LANES
VMEM_LIMIT = 56 * 1024 * 1024

PROJ_TM = 1024
PROJ_TN = 512
HY_CW = 256
HY_FC = 512
OUT_TM = 512
RT_TM = 256
EXP_ROWS = 1152
EXP_CHUNK = 128
EXP_TF = 256
GATHER_UNROLL = 8
CMB_TM = 128
DFT_TB = 128


def _cparams(sem):
    return pltpu.CompilerParams(dimension_semantics=sem, vmem_limit_bytes=VMEM_LIMIT)


def _const_spec(shape, index_map):
    return pl.BlockSpec(shape, index_map, pipeline_mode=pl.Buffered(1))


def _proj_kernel(x_ref, w_ref, b_ref, o_ref, xb_ref):
    @pl.when(pl.program_id(1) == 0)
    def _():
        xb_ref[...] = x_ref[...].astype(BF16)

    acc = lax.dot_general(xb_ref[...], w_ref[...], (((1,), (1,)), ((), ())), preferred_element_type=F32)
    o_ref[...] = (acc + b_ref[...]).astype(o_ref.dtype)


def _in_proj(x2d, wt_bf, b_row):
    m, k = x2d.shape
    n = wt_bf.shape[0]
    return pl.pallas_call(
        _proj_kernel,
        grid=(m // PROJ_TM, n // PROJ_TN),
        in_specs=[pl.BlockSpec((PROJ_TM, k), lambda i, j: (i, 0)),
                  pl.BlockSpec((PROJ_TN, k), lambda i, j: (j, 0)),
                  pl.BlockSpec((1, PROJ_TN), lambda i, j: (0, j))],
        out_specs=pl.BlockSpec((PROJ_TM, PROJ_TN), lambda i, j: (i, j)),
        out_shape=jax.ShapeDtypeStruct((m, n), BF16),
        scratch_shapes=[pltpu.VMEM((PROJ_TM, k), BF16)],
        compiler_params=_cparams(("arbitrary", "arbitrary")),
        name="in_proj")(x2d, wt_bf, b_row)


def _gate_kernel(x_ref, w_ref, b_ref, o_ref):
    o_ref[...] = lax.dot_general(x_ref[...], w_ref[...], (((1,), (1,)), ((), ())), precision=HP,
                                 preferred_element_type=F32) + b_ref[...]


def _gate_proj(x2d, wt_pad, b_pad):
    m, k = x2d.shape
    tm = 512
    return pl.pallas_call(
        _gate_kernel,
        grid=(m // tm,),
        in_specs=[pl.BlockSpec((tm, k), lambda i: (i, 0)),
                  pl.BlockSpec((LANES, k), lambda i: (0, 0)),
                  pl.BlockSpec((1, LANES), lambda i: (0, 0))],
        out_specs=pl.BlockSpec((tm, LANES), lambda i: (i, 0)),
        out_shape=jax.ShapeDtypeStruct((m, LANES), F32),
        compiler_params=_cparams(("arbitrary",)),
        name="gate_proj")(x2d, wt_pad, b_pad)


def _filter_kernel(z_ref, w1_ref, b1_ref, w2_ref, b2_ref, fq_ref, w3f_ref, w3b_ref, dl_ref,
                   c_ref, s_ref, kr_ref, ki_ref, kn_ref, h_ref):
    seq = z_ref.shape[0]
    inv_n = 1.0 / (2 * seq)
    z = z_ref[...]

    @pl.when(jnp.logical_and(pl.program_id(0) == 0, pl.program_id(1) == 0))
    def _():
        h1 = jnp.sin(fq_ref[0:1, :] * (jnp.dot(z, w1_ref[...], precision=HP, preferred_element_type=F32)
                                       + b1_ref[...]))
        h_ref[...] = jnp.sin(fq_ref[1:2, :] * (jnp.dot(h1, w2_ref[...], precision=HP, preferred_element_type=F32)
                                               + b2_ref[...]))

    h = h_ref[...]
    win = jnp.exp(-z[:, 0:1] * dl_ref[...])
    fwd = jnp.dot(h, w3f_ref[...], precision=HP, preferred_element_type=F32) * win
    bwd = jnp.dot(h, w3b_ref[...], precision=HP, preferred_element_type=F32) * win
    row = lax.broadcasted_iota(I32, fwd.shape, 0)
    bwd = jnp.where(row == 0, 0.0, bwd)
    inv = 1.0 / jnp.sum(jnp.abs(fwd) + jnp.abs(bwd), axis=0, keepdims=True)
    ks = (fwd + bwd) * inv
    kd = (fwd - bwd) * inv
    kr = jnp.dot(c_ref[...], ks.astype(BF16), preferred_element_type=F32)
    ki = -jnp.dot(s_ref[...], kd.astype(BF16), preferred_element_type=F32)
    wf = jnp.where(row == 0, inv_n, 2.0 * inv_n)
    kr_ref[0] = kr * wf
    ki_ref[0] = ki * wf
    sgn = jnp.where((row & 1) == 0, 1.0, -1.0)
    kn_ref[0] = jnp.sum(ks * sgn, axis=0, keepdims=True) * inv_n


def _hyena_filters(zpad, w1pad, b1, w2, b2, freq, w3, deltas, cmat, smat):
    seq = zpad.shape[0]
    nb = D_HY // HY_CW
    hid = w2.shape[0]
    full = lambda shape: pl.BlockSpec(shape, lambda o, c: (0,) * len(shape))
    out_sds = jax.ShapeDtypeStruct((2, seq, D_HY), F32)
    return pl.pallas_call(
        _filter_kernel,
        grid=(2, nb),
        in_specs=[full(zpad.shape), full(w1pad.shape), full(b1.shape), full(w2.shape), full(b2.shape),
                  full(freq.shape),
                  pl.BlockSpec((hid, HY_CW), lambda o, c: (0, o * 2 * nb + c)),
                  pl.BlockSpec((hid, HY_CW), lambda o, c: (0, o * 2 * nb + nb + c)),
                  pl.BlockSpec((1, HY_CW), lambda o, c: (0, c)),
                  _const_spec((seq, seq), lambda o, c: (0, 0)),
                  _const_spec((seq, seq), lambda o, c: (0, 0))],
        out_specs=[pl.BlockSpec((1, seq, HY_CW), lambda o, c: (o, 0, c)),
                   pl.BlockSpec((1, seq, HY_CW), lambda o, c: (o, 0, c)),
                   pl.BlockSpec((1, 1, HY_CW), lambda o, c: (o, 0, c))],
        out_shape=[out_sds, out_sds, jax.ShapeDtypeStruct((2, 1, D_HY), F32)],
        scratch_shapes=[pltpu.VMEM((seq, hid), F32)],
        compiler_params=_cparams(("arbitrary", "arbitrary")),
        name="hyena_filters")(zpad, w1pad, b1, w2, b2, freq, w3, w3, deltas, cmat, smat)


def _short_conv(u, w_ref, b_ref, row, seq):
    prev = jnp.where(row == 0, 0.0, pltpu.roll(u, 1, 0))
    nxt = jnp.where(row == seq - 1, 0.0, pltpu.roll(u, seq - 1, 0))
    return w_ref[0:1, :] * prev + w_ref[1:2, :] * u + w_ref[2:3, :] * nxt + b_ref[...]


def _hyena_kernel(uv_ref, u1_ref, u2_ref, wv_ref, w1_ref, w2_ref, bv_ref, b1_ref, b2_ref,
                  c_ref, s_ref, kr_ref, ki_ref, kn_ref, skip_ref, nw_ref, o_ref,
                  z_ref, x_ref, zb_ref, pr_ref, pi_ref, ny_ref):
    seq = uv_ref.shape[1]
    cw = uv_ref.shape[2]
    nblk = seq // HY_FC
    row = lax.broadcasted_iota(I32, (seq, LANES), 0)
    sgn = jnp.where((row & 1) == 0, 1.0, -1.0)
    sgn_blk = sgn[:HY_FC, :]
    groups = [slice(g * LANES, (g + 1) * LANES) for g in range(cw // LANES)]

    def conv_group(u_ref, w_ref, b_ref, gs):
        return _short_conv(u_ref[0, :, gs].astype(F32), w_ref.at[:, gs], b_ref.at[:, gs], row, seq)

    def set_input(z, gs, o):
        z_ref[:, gs] = z
        zb_ref[:, gs] = z.astype(BF16)
        ny_ref[:, gs] = jnp.sum(z * sgn, axis=0, keepdims=True) * kn_ref[o, :, gs]

    def spectrum(o):
        for fb in range(nblk):
            fs = slice(fb * HY_FC, (fb + 1) * HY_FC)
            a = jnp.dot(c_ref[fs, :], zb_ref[...], preferred_element_type=F32)
            b = jnp.dot(s_ref[fs, :], zb_ref[...], preferred_element_type=F32)
            kr = kr_ref[o, fs, :]
            ki = ki_ref[o, fs, :]
            pr_ref[fs, :] = (a * kr + b * ki).astype(BF16)
            pi_ref[fs, :] = (a * ki - b * kr).astype(BF16)

    def conv_rows(tb):
        ts = slice(tb * HY_FC, (tb + 1) * HY_FC)
        y = jnp.dot(c_ref[ts, :], pr_ref[...], preferred_element_type=F32)
        y = y - jnp.dot(s_ref[ts, :], pi_ref[...], preferred_element_type=F32)
        return ts, y + ny_ref[...] * jnp.concatenate([sgn_blk] * (cw // LANES), axis=-1)

    for gs in groups:
        set_input(conv_group(uv_ref, wv_ref, bv_ref, gs), gs, 0)
        x_ref[:, gs] = conv_group(u1_ref, w1_ref, b1_ref, gs)
    spectrum(0)
    for tb in range(nblk):
        ts, y = conv_rows(tb)
        x_ref[ts, :] = x_ref[ts, :] * (y + skip_ref[0:1, :] * z_ref[ts, :])
    for gs in groups:
        set_input(x_ref[:, gs], gs, 1)
        x_ref[:, gs] = conv_group(u2_ref, w2_ref, b2_ref, gs)
    spectrum(1)
    for tb in range(nblk):
        ts, y = conv_rows(tb)
        z = x_ref[ts, :] * (y + skip_ref[1:2, :] * z_ref[ts, :])
        for gs in groups:
            zg = z[:, gs]
            mu = jnp.mean(zg, axis=-1, keepdims=True)
            zc = zg - mu
            var = jnp.mean(zc * zc, axis=-1, keepdims=True)
            o_ref[0, ts, gs] = (zc * lax.rsqrt(var + LN_EPS) * nw_ref[:, gs]).astype(o_ref.dtype)


def _hyena(proj3, conv_w, conv_b, cmat, smat, kr, ki, kn, skip, norm_w):
    bsz, seq, _ = proj3.shape
    nb = D_HY // HY_CW
    u_spec = lambda off: pl.BlockSpec((1, seq, HY_CW), lambda c, b: (b, 0, off + c))
    w_spec = lambda off: pl.BlockSpec((3, HY_CW), lambda c, b: (0, off + c))
    b_spec = lambda off: pl.BlockSpec((1, HY_CW), lambda c, b: (0, off + c))
    return pl.pallas_call(
        _hyena_kernel,
        grid=(nb, bsz),
        in_specs=[u_spec(0), u_spec(nb), u_spec(2 * nb),
                  w_spec(0), w_spec(nb), w_spec(2 * nb),
                  b_spec(0), b_spec(nb), b_spec(2 * nb),
                  _const_spec((seq, seq), lambda c, b: (0, 0)),
                  _const_spec((seq, seq), lambda c, b: (0, 0)),
                  _const_spec((2, seq, HY_CW), lambda c, b: (0, 0, c)),
                  _const_spec((2, seq, HY_CW), lambda c, b: (0, 0, c)),
                  pl.BlockSpec((2, 1, HY_CW), lambda c, b: (0, 0, c)),
                  pl.BlockSpec((2, HY_CW), lambda c, b: (0, c)),
                  pl.BlockSpec((1, HY_CW), lambda c, b: (0, c))],
        out_specs=pl.BlockSpec((1, seq, HY_CW), lambda c, b: (b, 0, c)),
        out_shape=jax.ShapeDtypeStruct((bsz, seq, D_HY), BF16),
        scratch_shapes=[pltpu.VMEM((seq, HY_CW), F32), pltpu.VMEM((seq, HY_CW), F32),
                        pltpu.VMEM((seq, HY_CW), BF16), pltpu.VMEM((seq, HY_CW), BF16),
                        pltpu.VMEM((seq, HY_CW), BF16), pltpu.VMEM((1, HY_CW), F32)],
        compiler_params=_cparams(("arbitrary", "arbitrary")),
        name="hyena")(proj3, proj3, proj3, conv_w, conv_w, conv_w, conv_b, conv_b, conv_b,
                      cmat, smat, kr, ki, kn, skip, norm_w)


def _mlstm_kernel(qp_ref, kp_ref, v_ref, og_ref, wq_ref, wk_ref, bq_ref, bk_ref, gr_ref, gt_ref,
                  nw_ref, o_ref, qb_ref, kb_ref, hacc_ref, cst_ref, nst_ref):
    seq = qp_ref.shape[1]
    d = qp_ref.shape[2]
    nchunk = seq // CHUNK
    row = lax.broadcasted_iota(I32, (seq, d), 0)
    q = _short_conv(qp_ref[0].astype(F32), wq_ref, bq_ref, row, seq)
    k = _short_conv(kp_ref[0].astype(F32), wk_ref, bk_ref, row, seq)
    qb_ref[...] = (q * jax.nn.sigmoid(q)).astype(BF16)
    kb_ref[...] = ((k * jax.nn.sigmoid(k)) * (d ** -0.5)).astype(BF16)

    ti = lax.broadcasted_iota(I32, (CHUNK, CHUNK), 0)
    si = lax.broadcasted_iota(I32, (CHUNK, CHUNK), 1)
    lower = ti >= si
    upper = ti <= si
    lower_f = lower.astype(F32)
    upper_f = upper.astype(F32)
    nt = (((1,), (1,)), ((), ()))
    tn = (((0,), (0,)), ((), ()))
    chunk_rows = [slice(c * CHUNK, (c + 1) * CHUNK) for c in range(nchunk)]

    for direction in range(2):
        f_idx, i_idx = 2 * direction + 1, 2 * direction
        mask = lower if direction == 0 else upper
        order = list(range(nchunk)) if direction == 0 else list(range(nchunk - 1, -1, -1))
        lf_r = jax.nn.log_sigmoid(gr_ref[0, 0, f_idx])
        b_r = jnp.dot(lf_r, upper_f if direction == 0 else lower_f, precision=HP, preferred_element_type=F32)
        rterm = b_r - gr_ref[0, 0, i_idx]
        b_last = jnp.sum(lf_r, axis=-1, keepdims=True)
        lf_c = jax.nn.log_sigmoid(gt_ref[0, 0, f_idx])
        b_c = jnp.dot(lower_f if direction == 0 else upper_f, lf_c, precision=HP, preferred_element_type=F32)
        i_c = gt_ref[0, 0, i_idx]

        bcol, gcol, gmax, rowmax, blast = [], [], [], [], []
        for c in range(nchunk):
            bc = jnp.broadcast_to(b_c[:, c:c + 1], (CHUNK, CHUNK))
            ic = jnp.broadcast_to(i_c[:, c:c + 1], (CHUNK, CHUNK))
            bl = jnp.broadcast_to(b_last[c:c + 1, :], (1, CHUNK))
            dmat = jnp.where(mask, bc - rterm[c:c + 1, :], -jnp.inf)
            g = bl - bc + ic
            bcol.append(bc)
            gcol.append(g)
            blast.append(bl)
            rowmax.append(jnp.max(dmat, axis=-1, keepdims=True))
            gmax.append(jnp.max(g, axis=0, keepdims=True))

        m = jnp.zeros((1, CHUNK), F32)
        m_in, m_out = [None] * nchunk, [None] * nchunk
        for c in order:
            m_in[c] = m
            m = jnp.maximum(blast[c] + m, gmax[c])
            m_out[c] = m

        cmat = jnp.zeros((d, d), F32)
        nvec = jnp.zeros((1, d), F32)
        for c in order:
            kc = kb_ref[chunk_rows[c], :]
            vc = v_ref[0, chunk_rows[c], :]
            cst_ref[c] = cmat.astype(BF16)
            nst_ref[c] = nvec
            wg = jnp.exp(gcol[c] - m_out[c])
            decay = jnp.exp(blast[c] + m_in[c] - m_out[c])
            upd = lax.dot_general((wg * vc.astype(F32)).astype(BF16), kc, tn, preferred_element_type=F32)
            cmat = decay * cmat + upd
            nvec = decay * nvec + jnp.sum(wg * kc.astype(F32), axis=0, keepdims=True)

        for c in range(nchunk):
            rs = chunk_rows[c]
            qc = qb_ref[rs, :]
            kc = kb_ref[rs, :]
            vc = v_ref[0, rs, :]
            dmat = jnp.where(mask, bcol[c] - rterm[c:c + 1, :], -jnp.inf)
            inter = bcol[c] + m_in[c]
            m_t = jnp.maximum(inter, rowmax[c])
            p = jnp.exp(dmat - m_t)
            inter_w = jnp.exp(inter - m_t)
            s = lax.dot_general(qc, kc, nt, preferred_element_type=F32) * p
            cq = lax.dot_general(qc, cst_ref[c], nt, preferred_element_type=F32)
            num = jnp.dot(s.astype(BF16), vc, preferred_element_type=F32) + inter_w * cq
            nq = jnp.sum(qc.astype(F32) * nst_ref[c], axis=-1, keepdims=True)
            den = jnp.sum(s, axis=-1, keepdims=True) + inter_w * nq
            h = num / jnp.maximum(jnp.abs(den), jnp.exp(-m_t))
            if direction == 0:
                hacc_ref[rs, :] = h
            else:
                hacc_ref[rs, :] += h

    h = hacc_ref[...]
    mu = jnp.mean(h, axis=-1, keepdims=True)
    hc = h - mu
    var = jnp.mean(hc * hc, axis=-1, keepdims=True)
    y = hc * lax.rsqrt(var + LN_EPS) * nw_ref[...] * jax.nn.sigmoid(og_ref[0].astype(F32))
    o_ref[0] = y.astype(o_ref.dtype)


def _mlstm(proj3, conv_w, conv_b, grow, gtr, norm_w):
    bsz, seq, _ = proj3.shape
    d = HEAD_DIM
    nchunk = seq // CHUNK
    hy_blocks = 3 * D_HY // d
    qoff, koff, voff, ooff = hy_blocks, hy_blocks + ML_HEADS, hy_blocks + 2 * ML_HEADS, hy_blocks + 3 * ML_HEADS
    p_spec = lambda off: pl.BlockSpec((1, seq, d), lambda b, h: (b, 0, off + h))
    return pl.pallas_call(
        _mlstm_kernel,
        grid=(bsz, ML_HEADS),
        in_specs=[p_spec(qoff), p_spec(koff), p_spec(voff), p_spec(ooff),
                  pl.BlockSpec((3, d), lambda b, h: (0, h)),
                  pl.BlockSpec((3, d), lambda b, h: (0, ML_HEADS + h)),
                  pl.BlockSpec((1, d), lambda b, h: (0, h)),
                  pl.BlockSpec((1, d), lambda b, h: (0, ML_HEADS + h)),
                  pl.BlockSpec((1, 1, 4, nchunk, CHUNK), lambda b, h: (b, h, 0, 0, 0)),
                  pl.BlockSpec((1, 1, 4, CHUNK, nchunk), lambda b, h: (b, h, 0, 0, 0)),
                  pl.BlockSpec((1, d), lambda b, h: (0, h))],
        out_specs=pl.BlockSpec((1, seq, d), lambda b, h: (b, 0, h)),
        out_shape=jax.ShapeDtypeStruct((bsz, seq, D_ML), BF16),
        scratch_shapes=[pltpu.VMEM((seq, d), BF16), pltpu.VMEM((seq, d), BF16),
                        pltpu.VMEM((seq, d), F32),
                        pltpu.VMEM((nchunk, d, d), BF16), pltpu.VMEM((nchunk, 1, d), F32)],
        compiler_params=_cparams(("arbitrary", "arbitrary")),
        name="mlstm")(proj3, proj3, proj3, proj3, conv_w, conv_w, conv_b, conv_b, grow, gtr, norm_w)


def _to_slabs(y):
    n = y.shape[0]
    parts = jnp.stack([y[:, c * LANES:(c + 1) * LANES] for c in range(ROW_CHUNKS)], axis=0)
    return pltpu.einshape("crl->rcl", parts).reshape(n * ROW_CHUNKS, LANES)


def _from_slabs(v):
    n = v.shape[0] // ROW_CHUNKS
    parts = pltpu.einshape("rcl->crl", v.reshape(n, ROW_CHUNKS, LANES))
    return jnp.concatenate([parts[c] for c in range(ROW_CHUNKS)], axis=-1)


def _layer_norm(u, g, b):
    mu = jnp.mean(u, axis=-1, keepdims=True)
    uc = u - mu
    var = jnp.mean(uc * uc, axis=-1, keepdims=True)
    return uc * lax.rsqrt(var + LN_EPS) * g + b


def _outproj_kernel(yh_ref, ym_ref, x_ref, wa_ref, wb_ref, b_ref, g_ref, be_ref, o_ref, oc_ref):
    mix = (jnp.dot(yh_ref[...], wa_ref[...], preferred_element_type=F32)
           + jnp.dot(ym_ref[...], wb_ref[...], preferred_element_type=F32) + b_ref[...])
    y = _layer_norm(DN_ALPHA * x_ref[...] + mix, g_ref[...], be_ref[...])
    o_ref[...] = y
    oc_ref[...] = _to_slabs(y)


def _out_proj_ln(y_hy, y_ml, x2d, w_out_bf, b_out, g, be):
    t = x2d.shape[0]
    tm = OUT_TM
    vec = lambda: pl.BlockSpec((1, D_MODEL), lambda i: (0, 0))
    return pl.pallas_call(
        _outproj_kernel,
        grid=(t // tm,),
        in_specs=[pl.BlockSpec((tm, D_HY), lambda i: (i, 0)),
                  pl.BlockSpec((tm, D_ML), lambda i: (i, 0)),
                  pl.BlockSpec((tm, D_MODEL), lambda i: (i, 0)),
                  _const_spec((D_HY, D_MODEL), lambda i: (0, 0)),
                  _const_spec((D_ML, D_MODEL), lambda i: (1, 0)),
                  vec(), vec(), vec()],
        out_specs=[pl.BlockSpec((tm, D_MODEL), lambda i: (i, 0)),
                   pl.BlockSpec((tm * ROW_CHUNKS, LANES), lambda i: (i, 0))],
        out_shape=[jax.ShapeDtypeStruct((t, D_MODEL), F32),
                   jax.ShapeDtypeStruct((t * ROW_CHUNKS, LANES), F32)],
        compiler_params=_cparams(("arbitrary",)),
        name="out_proj_ln1")(y_hy, y_ml, x2d, w_out_bf, w_out_bf, b_out, g, be)


def _router_kernel(x_ref, w_ref, b_ref, ti_ref, tg_ref, tp_ref, cnt_ref):
    tm = ti_ref.shape[0]

    @pl.when(pl.program_id(0) == 0)
    def _():
        cnt_ref[...] = jnp.zeros_like(cnt_ref)

    logits = jnp.dot(x_ref[...], w_ref[...], precision=HP, preferred_element_type=F32) + b_ref[...]
    lane = lax.broadcasted_iota(I32, (tm, LANES), 1)
    work = logits
    vals, idxs = [], []
    chosen = jnp.zeros((tm, LANES), F32)
    for _ in range(TOP_K):
        mx = jnp.max(work, axis=-1, keepdims=True)
        idx = jnp.min(jnp.where(work == mx, lane, LANES), axis=-1, keepdims=True)
        hit = lane == idx
        vals.append(mx)
        idxs.append(idx)
        chosen = jnp.where(hit, 1.0, chosen)
        work = jnp.where(hit, -jnp.inf, work)
    exps = [jnp.exp(v - vals[0]) for v in vals]
    den = exps[0] + exps[1] + exps[2] + exps[3]
    ri = lax.broadcasted_iota(I32, (tm, tm), 0)
    ci = lax.broadcasted_iota(I32, (tm, tm), 1)
    strict_lower = (ri > ci).astype(BF16)
    carry = cnt_ref[...]
    slot = carry + jnp.dot(strict_lower, chosen.astype(BF16), preferred_element_type=F32)
    ti = jnp.zeros((tm, LANES), I32)
    tg = jnp.zeros((tm, LANES), F32)
    tp = jnp.zeros((tm, LANES), F32)
    for k in range(TOP_K):
        sk = jnp.sum(jnp.where(lane == idxs[k], slot, 0.0), axis=-1, keepdims=True)
        ti = jnp.where(lane == k, idxs[k], ti)
        tg = jnp.where(lane == k, exps[k] / den, tg)
        tp = jnp.where(lane == k, sk, tp)
    ti_ref[...] = ti
    tg_ref[...] = tg
    tp_ref[...] = tp.astype(I32)
    cnt_ref[...] = carry + jnp.sum(chosen, axis=0, keepdims=True)


def _router(x1, w_pad, b_pad):
    t = x1.shape[0]
    tm = RT_TM
    o_spec = lambda: pl.BlockSpec((tm, LANES), lambda i: (i, 0))
    return pl.pallas_call(
        _router_kernel,
        grid=(t // tm,),
        in_specs=[pl.BlockSpec((tm, D_MODEL), lambda i: (i, 0)),
                  pl.BlockSpec((D_MODEL, LANES), lambda i: (0, 0)),
                  pl.BlockSpec((1, LANES), lambda i: (0, 0))],
        out_specs=[o_spec(), o_spec(), o_spec(), pl.BlockSpec((1, LANES), lambda i: (0, 0))],
        out_shape=[jax.ShapeDtypeStruct((t, LANES), I32), jax.ShapeDtypeStruct((t, LANES), F32),
                   jax.ShapeDtypeStruct((t, LANES), I32), jax.ShapeDtypeStruct((1, LANES), F32)],
        compiler_params=_cparams(("arbitrary",)),
        name="router")(x1, w_pad, b_pad)


def _expert_kernel(te_ref, tr_ref, tb_ref, rt_ref, x_hbm, wg_ref, wu_ref, wd_ref, bg_ref, bu_ref, bd_ref, y_hbm,
                   stage_ref, xb_ref, acc_ref, ring_ref, wgb_ref, wub_ref, wdb_ref, gsem, osem):
    g = pl.program_id(0)
    n_items = pl.num_programs(0) - 1
    nf = D_FF // EXP_TF
    n_tiles = n_items // nf
    item = jnp.maximum(g - 1, 0)
    s = item // nf
    j = item % nf
    rows = tr_ref[s]
    active = jnp.logical_and(g >= 1, rows > 0)
    cast_slot = g % 2
    use_slot = (g + 1) % 2
    slab = EXP_CHUNK * ROW_CHUNKS
    per_step = EXP_ROWS // nf

    def row_copy(tok, r):
        return pltpu.make_async_copy(
            x_hbm.at[pl.ds(pl.multiple_of(tok * ROW_CHUNKS, ROW_CHUNKS), ROW_CHUNKS), :],
            stage_ref.at[pl.ds(pl.multiple_of(r * ROW_CHUNKS, ROW_CHUNKS), ROW_CHUNKS), :], gsem)

    def wait_gather():
        pltpu.make_async_copy(x_hbm.at[pl.ds(0, EXP_ROWS * ROW_CHUNKS), :], stage_ref, gsem).wait()

    def cast_weights():
        wgb_ref[cast_slot] = wg_ref[0].astype(BF16)
        wub_ref[cast_slot] = wu_ref[0].astype(BF16)
        wdb_ref[cast_slot] = wd_ref[0].astype(BF16)

    @pl.when(g == 0)
    def _():
        stage_ref[...] = jnp.zeros_like(stage_ref)
        acc_ref[...] = jnp.zeros_like(acc_ref)
        base = tb_ref[0]

        def group(q, carry):
            for u in range(GATHER_UNROLL):
                r = q * GATHER_UNROLL + u
                row_copy(rt_ref[base + r], r).start()
            return carry

        lax.fori_loop(0, EXP_ROWS // GATHER_UNROLL, group, 0)

    @pl.when(jnp.logical_not(active))
    def _():
        cast_weights()

    @pl.when(active)
    def _():
        @pl.when(j == 0)
        def _():
            wait_gather()
            for i in range(EXP_ROWS // EXP_CHUNK):
                xb_ref[i * EXP_CHUNK:(i + 1) * EXP_CHUNK, :] = _from_slabs(
                    stage_ref[i * slab:(i + 1) * slab, :]).astype(BF16)

        cast_weights()
        nxt_base = tb_ref[jnp.minimum(s + 1, n_tiles - 1)]
        for u in range(per_step):
            r = j * per_step + u
            row_copy(rt_ref[nxt_base + r], r).start()

        xb = xb_ref[...]
        gate = jnp.dot(xb, wgb_ref[use_slot], preferred_element_type=F32) + bg_ref[0]
        up = jnp.dot(xb, wub_ref[use_slot], preferred_element_type=F32) + bu_ref[0]
        gate = jnp.minimum(gate, SWIGLU_LIMIT)
        up = jnp.clip(up, -SWIGLU_LIMIT, SWIGLU_LIMIT)
        act = (up + 1.0) * (gate * jax.nn.sigmoid(SWIGLU_ALPHA * gate))
        part = jnp.dot(act.astype(BF16), wdb_ref[use_slot], preferred_element_type=F32)

        acc_ref[...] = jnp.where(j == 0, jnp.broadcast_to(bd_ref[0], part.shape), acc_ref[...]) + part

        @pl.when(j == nf - 1)
        def _():
            nchunk = (rows + EXP_CHUNK - 1) // EXP_CHUNK

            def chunk_copy(i, slot):
                dst0 = pl.multiple_of((s * EXP_ROWS + i * EXP_CHUNK) * ROW_CHUNKS, slab)
                return pltpu.make_async_copy(ring_ref.at[slot], y_hbm.at[pl.ds(dst0, slab), :], osem.at[slot])

            def emit(i, carry):
                slot = i % 2

                @pl.when(i >= 2)
                def _():
                    chunk_copy(i - 2, slot).wait()

                r0 = pl.multiple_of(i * EXP_CHUNK, EXP_CHUNK)
                ring_ref[slot] = _to_slabs(acc_ref[pl.ds(r0, EXP_CHUNK), :])
                chunk_copy(i, slot).start()
                return carry

            lax.fori_loop(0, nchunk, emit, 0)
            for back in range(2):
                @pl.when(nchunk > back)
                def _():
                    last = nchunk - 1 - back
                    chunk_copy(last, last % 2).wait()

    @pl.when(g == n_items)
    def _():
        wait_gather()


def _experts(tile_e, tile_rows, tile_base, row_tok, x1c, w_gu, b_gu, w_down, b_down):
    n_tiles = tile_e.shape[0]
    nf = D_FF // EXP_TF
    n_items = n_tiles * nf

    def item_block(item, te, tr):
        s = item // nf
        return te[s], jnp.where(tr[s] > 0, item % nf, nf - 1)

    def cast_item(g, te, tr):
        return item_block(jnp.minimum(g, n_items - 1), te, tr)

    def use_item(g, te, tr):
        return item_block(jnp.maximum(g - 1, 0), te, tr)

    def w_gate(g, te, tr, tb, rt):
        e, j = cast_item(g, te, tr)
        return e, 0, j

    def w_up(g, te, tr, tb, rt):
        e, j = cast_item(g, te, tr)
        return e, 0, nf + j

    def w_down_map(g, te, tr, tb, rt):
        e, j = cast_item(g, te, tr)
        return e, j, 0

    def b_gate(g, te, tr, tb, rt):
        e, j = use_item(g, te, tr)
        return e, 0, j

    def b_up(g, te, tr, tb, rt):
        e, j = use_item(g, te, tr)
        return e, 0, nf + j

    def b_down_map(g, te, tr, tb, rt):
        e, _ = use_item(g, te, tr)
        return e, 0, 0

    grid_spec = pltpu.PrefetchScalarGridSpec(
        num_scalar_prefetch=4,
        grid=(n_items + 1,),
        in_specs=[pl.BlockSpec(memory_space=pl.ANY),
                  pl.BlockSpec((1, D_MODEL, EXP_TF), w_gate),
                  pl.BlockSpec((1, D_MODEL, EXP_TF), w_up),
                  pl.BlockSpec((1, EXP_TF, D_MODEL), w_down_map),
                  pl.BlockSpec((1, 1, EXP_TF), b_gate),
                  pl.BlockSpec((1, 1, EXP_TF), b_up),
                  pl.BlockSpec((1, 1, D_MODEL), b_down_map)],
        out_specs=pl.BlockSpec(memory_space=pl.ANY),
        scratch_shapes=[pltpu.VMEM((EXP_ROWS * ROW_CHUNKS, LANES), F32),
                        pltpu.VMEM((EXP_ROWS, D_MODEL), BF16),
                        pltpu.VMEM((EXP_ROWS, D_MODEL), F32),
                        pltpu.VMEM((2, EXP_CHUNK * ROW_CHUNKS, LANES), F32),
                        pltpu.VMEM((2, D_MODEL, EXP_TF), BF16),
                        pltpu.VMEM((2, D_MODEL, EXP_TF), BF16),
                        pltpu.VMEM((2, EXP_TF, D_MODEL), BF16),
                        pltpu.SemaphoreType.DMA(()),
                        pltpu.SemaphoreType.DMA((2,))])
    return pl.pallas_call(
        _expert_kernel,
        grid_spec=grid_spec,
        out_shape=jax.ShapeDtypeStruct((n_tiles * EXP_ROWS * ROW_CHUNKS, LANES), F32),
        compiler_params=_cparams(("arbitrary",)),
        name="experts")(tile_e, tile_rows, tile_base, row_tok, x1c, w_gu, w_gu, w_down, b_gu, b_gu, b_down)


def _combine_kernel(dest_ref, y_hbm, x_ref, tg_ref, g_ref, be_ref, o_ref, buf_ref, sem):
    tm = o_ref.shape[0]
    i = pl.program_id(0)
    n = pl.num_programs(0)

    def row_copy(src_row, slot, k, t):
        return pltpu.make_async_copy(
            y_hbm.at[pl.ds(pl.multiple_of(src_row * ROW_CHUNKS, ROW_CHUNKS), ROW_CHUNKS), :],
            buf_ref.at[slot, k, pl.ds(pl.multiple_of(t * ROW_CHUNKS, ROW_CHUNKS), ROW_CHUNKS), :],
            sem.at[slot])

    def start_tile(tile, slot):
        base = tile * tm * TOP_K

        def body(t2, carry):
            for u in range(2):
                t = t2 * 2 + u
                for k in range(TOP_K):
                    row_copy(dest_ref[base + t * TOP_K + k], slot, k, t).start()
            return carry

        lax.fori_loop(0, tm // 2, body, 0)

    def wait_tile(slot):
        for k in range(TOP_K):
            pltpu.make_async_copy(y_hbm.at[pl.ds(0, tm * ROW_CHUNKS), :], buf_ref.at[slot, k], sem.at[slot]).wait()

    @pl.when(i == 0)
    def _():
        start_tile(0, 0)

    @pl.when(i + 1 < n)
    def _():
        start_tile(jnp.minimum(i + 1, n - 1), (i + 1) % 2)

    slot = i % 2
    wait_tile(slot)
    tg = tg_ref[...]
    ff = jnp.zeros((tm, D_MODEL), F32)
    for k in range(TOP_K):
        ff = ff + tg[:, k:k + 1] * _from_slabs(buf_ref[slot, k])
    o_ref[...] = _layer_norm(DN_ALPHA * x_ref[...] + ff, g_ref[...], be_ref[...])


def _combine_ln(dest_flat, y_buf, x1, tg, g, be):
    t = tg.shape[0]
    tm = CMB_TM
    grid_spec = pltpu.PrefetchScalarGridSpec(
        num_scalar_prefetch=1,
        grid=(t // tm,),
        in_specs=[pl.BlockSpec(memory_space=pl.ANY),
                  pl.BlockSpec((tm, D_MODEL), lambda i, d: (i, 0)),
                  pl.BlockSpec((tm, LANES), lambda i, d: (i, 0)),
                  pl.BlockSpec((1, D_MODEL), lambda i, d: (0, 0)),
                  pl.BlockSpec((1, D_MODEL), lambda i, d: (0, 0))],
        out_specs=pl.BlockSpec((tm, D_MODEL), lambda i, d: (i, 0)),
        scratch_shapes=[pltpu.VMEM((2, TOP_K, tm * ROW_CHUNKS, LANES), F32), pltpu.SemaphoreType.DMA((2,))])
    return pl.pallas_call(
        _combine_kernel,
        grid_spec=grid_spec,
        out_shape=jax.ShapeDtypeStruct((t, D_MODEL), F32),
        compiler_params=_cparams(("arbitrary",)),
        name="combine_ln2")(dest_flat, y_buf, x1, tg, g, be)


def _dft_kernel(cd_ref, sd_ref, ca_ref, sa_ref, c_ref, s_ref):
    cd, sd = cd_ref[...], sd_ref[...]
    ca, sa = ca_ref[0], sa_ref[0]
    c_ref[...] = (cd * ca - sd * sa).astype(c_ref.dtype)
    s_ref[...] = (sd * ca + cd * sa).astype(s_ref.dtype)


def _dft_tables(seq):
    n = 2 * seq
    nblk = seq // DFT_TB
    t = jnp.arange(seq, dtype=I32)

    def angles(f):
        return ((f[:, None] * t[None, :]) % n).astype(F32) * (2.0 * math.pi / n)

    ang_d = angles(jnp.arange(DFT_TB, dtype=I32))
    ang_a = angles(jnp.arange(nblk, dtype=I32) * DFT_TB)[:, None, :]
    small = pl.BlockSpec((DFT_TB, seq), lambda a: (0, 0))
    base = pl.BlockSpec((1, 1, seq), lambda a: (a, 0, 0))
    out = pl.BlockSpec((DFT_TB, seq), lambda a: (a, 0))
    sds = jax.ShapeDtypeStruct((seq, seq), BF16)
    return pl.pallas_call(
        _dft_kernel, grid=(nblk,), in_specs=[small, small, base, base], out_specs=[out, out],
        out_shape=[sds, sds], compiler_params=_cparams(("arbitrary",)),
        name="dft_tables")(jnp.cos(ang_d), jnp.sin(ang_d), jnp.cos(ang_a), jnp.sin(ang_a))


def _filter_features(seq):
    t = jnp.linspace(0.0, 1.0, seq, dtype=F32)[:, None]
    bands = (HY_EMB - 1) // 2
    fb = jnp.linspace(1e-4, bands - 1, bands, dtype=F32)[None]
    w = 2.0 * math.pi * jnp.arange(seq, dtype=F32)[:, None] / seq
    z = jnp.concatenate([t, jnp.cos(fb * w), -jnp.sin(fb * w)], -1)
    return jnp.pad(z, ((0, 0), (0, LANES - HY_EMB)))


def _mixer(x, w_in, b_in, hy_conv_w, hy_conv_b, hy_filt_w1, hy_filt_b1, hy_filt_w2, hy_filt_b2,
           hy_filt_w3, hy_filt_freq, hy_skip, hy_norm_w, ml_conv_w, ml_conv_b, ml_norm_w):
    bsz, seq, _ = x.shape
    t = bsz * seq
    x2d = x.reshape(t, D_MODEL)
    n_main = w_in.shape[1] - N_GATE_COLS
    w_t = jnp.swapaxes(w_in, 0, 1)
    proj = _in_proj(x2d, w_t[:n_main].astype(BF16), b_in[None, :n_main])
    proj3 = proj.reshape(bsz, seq, n_main)
    wg = jnp.pad(w_t[n_main:], ((0, LANES - N_GATE_COLS), (0, 0)))
    bg = jnp.pad(b_in[None, n_main:], ((0, 0), (0, LANES - N_GATE_COLS)))
    gates = _gate_proj(x2d, wg, bg)[:, :N_GATE_COLS]
    g5 = gates.reshape(bsz, seq, 4, ML_HEADS)
    grow = g5.transpose(0, 3, 2, 1).reshape(bsz, ML_HEADS, 4, seq // CHUNK, CHUNK)
    gtr = grow.transpose(0, 1, 2, 4, 3)

    cmat, smat = _dft_tables(seq)
    zpad = _filter_features(seq)
    w1pad = jnp.pad(hy_filt_w1, ((0, LANES - HY_EMB), (0, 0)))
    deltas = jnp.abs(jnp.linspace(math.log(HY_DECAY_TARGET) / HY_SLOW_PCT,
                                  math.log(HY_DECAY_TARGET) / HY_FAST_PCT, D_HY, dtype=F32))[None]
    kr, ki, kn = _hyena_filters(zpad, w1pad, hy_filt_b1[None], hy_filt_w2, hy_filt_b2[None],
                                hy_filt_freq, hy_filt_w3, deltas, cmat, smat)
    y_hy = _hyena(proj3, hy_conv_w, hy_conv_b[None], cmat, smat, kr, ki, kn, hy_skip, hy_norm_w[None])
    y_ml = _mlstm(proj3, ml_conv_w, ml_conv_b[None], grow, gtr, ml_norm_w[None])
    return y_hy.reshape(t, D_HY), y_ml.reshape(t, D_ML), x2d


def _moe_tables(top_i, slot, counts):
    t = top_i.shape[0]
    n_tiles = N_EXPERTS + (t * TOP_K) // EXP_ROWS
    ntile = (counts + EXP_ROWS - 1) // EXP_ROWS
    ends = jnp.cumsum(ntile)
    starts = ends - ntile
    total = ends[-1]
    s_idx = jnp.arange(n_tiles, dtype=I32)
    valid = s_idx < total
    s_eff = jnp.where(valid, s_idx, jnp.maximum(total - 1, 0))
    tile_e = jnp.minimum(jnp.sum((s_eff[:, None] >= ends[None, :]).astype(I32), axis=1), N_EXPERTS - 1)
    local = s_eff - starts[tile_e]
    tile_rows = jnp.where(valid, jnp.clip(counts[tile_e] - local * EXP_ROWS, 0, EXP_ROWS), 0).astype(I32)
    tok = jnp.arange(t, dtype=I32)[:, None]
    row_tok = jnp.pad(jnp.sort((top_i * t + tok).reshape(-1)) % t, (0, EXP_ROWS))
    first = jnp.cumsum(counts) - counts
    tile_base = (first[tile_e] + local * EXP_ROWS).astype(I32)
    onehot = top_i[:, :, None] == jnp.arange(N_EXPERTS, dtype=I32)
    dest = jnp.sum(jnp.where(onehot, starts * EXP_ROWS, 0), axis=-1) + slot
    return tile_e.astype(I32), tile_rows, tile_base, row_tok.astype(I32), dest.astype(I32).reshape(-1)


def kernel(x, w_in, b_in, hy_conv_w, hy_conv_b, hy_filt_w1, hy_filt_b1, hy_filt_w2, hy_filt_b2, hy_filt_w3, hy_filt_freq, hy_skip, hy_norm_w, ml_conv_w, ml_conv_b, ml_norm_w, w_out, b_out, ln1_g, ln1_b, router_w, router_b, w_gu, b_gu, w_down, b_down, ln2_g, ln2_b):
    bsz, seq, _ = x.shape
    l = 0
    y_hy, y_ml, x2d = _mixer(x, w_in[l], b_in[l], hy_conv_w[l], hy_conv_b[l], hy_filt_w1[l], hy_filt_b1[l],
                             hy_filt_w2[l], hy_filt_b2[l], hy_filt_w3[l], hy_filt_freq[l], hy_skip[l],
                             hy_norm_w[l], ml_conv_w[l], ml_conv_b[l], ml_norm_w[l])
    x1, x1c = _out_proj_ln(y_hy, y_ml, x2d, w_out[l].astype(BF16), b_out[l][None], ln1_g[l][None],
                           ln1_b[l][None])
    rw = jnp.pad(router_w[l], ((0, 0), (0, LANES - N_EXPERTS)))
    rb = jnp.pad(router_b[l][None], ((0, 0), (0, LANES - N_EXPERTS)), constant_values=-1e30)
    top_i, top_g, slot, cnt = _router(x1, rw, rb)
    counts = cnt[0, :N_EXPERTS].astype(I32)
    tile_e, tile_rows, tile_base, row_tok, dest = _moe_tables(top_i[:, :TOP_K], slot[:, :TOP_K], counts)
    y_buf = _experts(tile_e, tile_rows, tile_base, row_tok, x1c, w_gu[l], b_gu[l][:, None, :], w_down[l],
                     b_down[l][:, None, :])
    out = _combine_ln(dest, y_buf, x1, top_g, ln2_g[l][None], ln2_b[l][None])
    return out.reshape(bsz, seq, D_MODEL)
```

```python
import functools
import math

import jax
import jax.numpy as jnp
from jax import lax
from jax.experimental import pallas as pl
from jax.experimental.pallas import tpu as pltpu

F32 = jnp.float32
BF16 = jnp.bfloat16
I32 = jnp.int32
HP = lax.Precision.HIGHEST

D_MODEL = 2048
D_HY = 1024
D_ML = 1024
ML_HEADS = 8
HEAD_DIM = 128
CHUNK = 128
N_GATE_COLS = 32
HY_EMB = 33
N_EXPERTS = 32
TOP_K = 4
D_FF = 2048
SWIGLU_LIMIT = 7.0
SWIGLU_ALPHA = 1.702
LN_EPS = 1e-5
DN_ALPHA = 2.0 ** 0.25
HY_DECAY_TARGET = 1e-2
HY_FAST_PCT = 0.3
HY_SLOW_PCT = 1.5

LANES = 128
ROW_CHUNKS = D_MODEL // LANES
VMEM_LIMIT = 56 * 1024 * 1024

PROJ_TM = 2048
PROJ_TN = 512
HY_CW = 256
HY_FC = 512
OUT_TM = 512
RT_TM = 256
EXP_ROWS = 1152
EXP_M_SIZES = (1024, 1088, 1152)
EXP_CHUNK = 128
EXP_TF = 256
GATHER_UNROLL = 8
CMB_TM = 128
DFT_TB = 128


def _cparams(sem):
    return pltpu.CompilerParams(dimension_semantics=sem, vmem_limit_bytes=VMEM_LIMIT)


def _split(a):
    hi = a.astype(BF16)
    return hi, (a - hi.astype(F32)).astype(BF16)


def _dot3(a, b, dims):
    a_hi, a_lo = _split(a)
    b_hi, b_lo = _split(b)
    mm = functools.partial(lax.dot_general, dimension_numbers=dims, preferred_element_type=F32)
    return mm(a_hi, b_hi) + (mm(a_hi, b_lo) + mm(a_lo, b_hi))


NN = (((1,), (0,)), ((), ()))
NT = (((1,), (1,)), ((), ()))


def _const_spec(shape, index_map):
    return pl.BlockSpec(shape, index_map, pipeline_mode=pl.Buffered(1))


def _proj_kernel(x_ref, w_ref, b_ref, o_ref, xb_ref):
    @pl.when(pl.program_id(1) == 0)
    def _():
        xb_ref[...] = x_ref[...].astype(BF16)

    acc = lax.dot_general(xb_ref[...], w_ref[...], (((1,), (1,)), ((), ())), preferred_element_type=F32)
    o_ref[...] = (acc + b_ref[...]).astype(o_ref.dtype)


def _in_proj(x2d, wt_bf, b_row):
    m, k = x2d.shape
    n = wt_bf.shape[0]
    return pl.pallas_call(
        _proj_kernel,
        grid=(m // PROJ_TM, n // PROJ_TN),
        in_specs=[pl.BlockSpec((PROJ_TM, k), lambda i, j: (i, 0)),
                  pl.BlockSpec((PROJ_TN, k), lambda i, j: (j, 0)),
                  pl.BlockSpec((1, PROJ_TN), lambda i, j: (0, j))],
        out_specs=pl.BlockSpec((PROJ_TM, PROJ_TN), lambda i, j: (i, j)),
        out_shape=jax.ShapeDtypeStruct((m, n), BF16),
        scratch_shapes=[pltpu.VMEM((PROJ_TM, k), BF16)],
        compiler_params=_cparams(("arbitrary", "arbitrary")),
        name="in_proj")(x2d, wt_bf, b_row)


def _gate_kernel(x_ref, w_ref, b_ref, o_ref):
    o_ref[...] = _dot3(x_ref[...], w_ref[...], NT) + b_ref[...]


def _gate_proj(x2d, wt_pad, b_pad):
    m, k = x2d.shape
    tm = 512
    return pl.pallas_call(
        _gate_kernel,
        grid=(m // tm,),
        in_specs=[pl.BlockSpec((tm, k), lambda i: (i, 0)),
                  pl.BlockSpec((LANES, k), lambda i: (0, 0)),
                  pl.BlockSpec((1, LANES), lambda i: (0, 0))],
        out_specs=pl.BlockSpec((tm, LANES), lambda i: (i, 0)),
        out_shape=jax.ShapeDtypeStruct((m, LANES), F32),
        compiler_params=_cparams(("arbitrary",)),
        name="gate_proj")(x2d, wt_pad, b_pad)


def _filter_kernel(z_ref, w1_ref, b1_ref, w2_ref, b2_ref, fq_ref, w3f_ref, w3b_ref, dl_ref,
                   c_ref, s_ref, kr_ref, ki_ref, kn_ref, h_ref):
    seq = z_ref.shape[0]
    inv_n = 1.0 / (2 * seq)
    z = z_ref[...]

    @pl.when(jnp.logical_and(pl.program_id(0) == 0, pl.program_id(1) == 0))
    def _():
        h1 = jnp.sin(fq_ref[0:1, :] * (jnp.dot(z, w1_ref[...], precision=HP, preferred_element_type=F32)
                                       + b1_ref[...]))
        h_ref[...] = jnp.sin(fq_ref[1:2, :] * (jnp.dot(h1, w2_ref[...], precision=HP, preferred_element_type=F32)
                                               + b2_ref[...]))

    h = h_ref[...]
    win = jnp.exp(-z[:, 0:1] * dl_ref[...])
    fwd = _dot3(h, w3f_ref[...], NN) * win
    bwd = _dot3(h, w3b_ref[...], NN) * win
    row = lax.broadcasted_iota(I32, fwd.shape, 0)
    bwd = jnp.where(row == 0, 0.0, bwd)
    inv = 1.0 / jnp.sum(jnp.abs(fwd) + jnp.abs(bwd), axis=0, keepdims=True)
    ks = (fwd + bwd) * inv
    kd = (fwd - bwd) * inv
    kr = jnp.dot(c_ref[...], ks.astype(BF16), preferred_element_type=F32)
    ki = -jnp.dot(s_ref[...], kd.astype(BF16), preferred_element_type=F32)
    wf = jnp.where(row == 0, inv_n, 2.0 * inv_n)
    kr_ref[0] = kr * wf
    ki_ref[0] = ki * wf
    sgn = jnp.where((row & 1) == 0, 1.0, -1.0)
    kn_ref[0] = jnp.sum(ks * sgn, axis=0, keepdims=True) * inv_n


def _hyena_filters(zpad, w1pad, b1, w2, b2, freq, w3, deltas, cmat, smat):
    seq = zpad.shape[0]
    nb = D_HY // HY_CW
    hid = w2.shape[0]
    full = lambda shape: pl.BlockSpec(shape, lambda o, c: (0,) * len(shape))
    out_sds = jax.ShapeDtypeStruct((2, seq, D_HY), F32)
    return pl.pallas_call(
        _filter_kernel,
        grid=(2, nb),
        in_specs=[full(zpad.shape), full(w1pad.shape), full(b1.shape), full(w2.shape), full(b2.shape),
                  full(freq.shape),
                  pl.BlockSpec((hid, HY_CW), lambda o, c: (0, o * 2 * nb + c)),
                  pl.BlockSpec((hid, HY_CW), lambda o, c: (0, o * 2 * nb + nb + c)),
                  pl.BlockSpec((1, HY_CW), lambda o, c: (0, c)),
                  _const_spec((seq, seq), lambda o, c: (0, 0)),
                  _const_spec((seq, seq), lambda o, c: (0, 0))],
        out_specs=[pl.BlockSpec((1, seq, HY_CW), lambda o, c: (o, 0, c)),
                   pl.BlockSpec((1, seq, HY_CW), lambda o, c: (o, 0, c)),
                   pl.BlockSpec((1, 1, HY_CW), lambda o, c: (o, 0, c))],
        out_shape=[out_sds, out_sds, jax.ShapeDtypeStruct((2, 1, D_HY), F32)],
        scratch_shapes=[pltpu.VMEM((seq, hid), F32)],
        compiler_params=_cparams(("arbitrary", "arbitrary")),
        name="hyena_filters")(zpad, w1pad, b1, w2, b2, freq, w3, w3, deltas, cmat, smat)


def _short_conv(u, w_ref, b_ref, row, seq):
    prev = jnp.where(row == 0, 0.0, pltpu.roll(u, 1, 0))
    nxt = jnp.where(row == seq - 1, 0.0, pltpu.roll(u, seq - 1, 0))
    return w_ref[0:1, :] * prev + w_ref[1:2, :] * u + w_ref[2:3, :] * nxt + b_ref[...]


def _hyena_kernel(uv_ref, u1_ref, u2_ref, wv_ref, w1_ref, w2_ref, bv_ref, b1_ref, b2_ref,
                  c_ref, s_ref, kr_ref, ki_ref, kn_ref, skip_ref, nw_ref, o_ref,
                  z_ref, x_ref, zb_ref, pr_ref, pi_ref, ny_ref):
    seq = uv_ref.shape[1]
    cw = uv_ref.shape[2]
    nblk = seq // HY_FC
    row = lax.broadcasted_iota(I32, (seq, LANES), 0)
    sgn = jnp.where((row & 1) == 0, 1.0, -1.0)
    sgn_blk = sgn[:HY_FC, :]
    groups = [slice(g * LANES, (g + 1) * LANES) for g in range(cw // LANES)]

    def conv_group(u_ref, w_ref, b_ref, gs):
        return _short_conv(u_ref[0, :, gs].astype(F32), w_ref.at[:, gs], b_ref.at[:, gs], row, seq)

    def set_input(z, gs, o):
        z_ref[:, gs] = z
        zb_ref[:, gs] = z.astype(BF16)
        ny_ref[:, gs] = jnp.sum(z * sgn, axis=0, keepdims=True) * kn_ref[o, :, gs]

    def spectrum(o):
        for fb in range(nblk):
            fs = slice(fb * HY_FC, (fb + 1) * HY_FC)
            a = jnp.dot(c_ref[fs, :], zb_ref[...], preferred_element_type=F32)
            b = jnp.dot(s_ref[fs, :], zb_ref[...], preferred_element_type=F32)
            kr = kr_ref[o, fs, :]
            ki = ki_ref[o, fs, :]
            pr_ref[fs, :] = (a * kr + b * ki).astype(BF16)
            pi_ref[fs, :] = (a * ki - b * kr).astype(BF16)

    def conv_rows(tb):
        ts = slice(tb * HY_FC, (tb + 1) * HY_FC)
        y = jnp.dot(c_ref[ts, :], pr_ref[...], preferred_element_type=F32)
        y = y - jnp.dot(s_ref[ts, :], pi_ref[...], preferred_element_type=F32)
        return ts, y + ny_ref[...] * jnp.concatenate([sgn_blk] * (cw // LANES), axis=-1)

    for gs in groups:
        set_input(conv_group(uv_ref, wv_ref, bv_ref, gs), gs, 0)
        x_ref[:, gs] = conv_group(u1_ref, w1_ref, b1_ref, gs)
    spectrum(0)
    for tb in range(nblk):
        ts, y = conv_rows(tb)
        x_ref[ts, :] = x_ref[ts, :] * (y + skip_ref[0:1, :] * z_ref[ts, :])
    for gs in groups:
        set_input(x_ref[:, gs], gs, 1)
        x_ref[:, gs] = conv_group(u2_ref, w2_ref, b2_ref, gs)
    spectrum(1)
    for tb in range(nblk):
        ts, y = conv_rows(tb)
        z = x_ref[ts, :] * (y + skip_ref[1:2, :] * z_ref[ts, :])
        for gs in groups:
            zg = z[:, gs]
            mu = jnp.mean(zg, axis=-1, keepdims=True)
            zc = zg - mu
            var = jnp.mean(zc * zc, axis=-1, keepdims=True)
            o_ref[0, ts, gs] = (zc * lax.rsqrt(var + LN_EPS) * nw_ref[:, gs]).astype(o_ref.dtype)


def _hyena(proj3, conv_w, conv_b, cmat, smat, kr, ki, kn, skip, norm_w):
    bsz, seq, _ = proj3.shape
    nb = D_HY // HY_CW
    u_spec = lambda off: pl.BlockSpec((1, seq, HY_CW), lambda c, b: (b, 0, off + c))
    w_spec = lambda off: pl.BlockSpec((3, HY_CW), lambda c, b: (0, off + c))
    b_spec = lambda off: pl.BlockSpec((1, HY_CW), lambda c, b: (0, off + c))
    return pl.pallas_call(
        _hyena_kernel,
        grid=(nb, bsz),
        in_specs=[u_spec(0), u_spec(nb), u_spec(2 * nb),
                  w_spec(0), w_spec(nb), w_spec(2 * nb),
                  b_spec(0), b_spec(nb), b_spec(2 * nb),
                  _const_spec((seq, seq), lambda c, b: (0, 0)),
                  _const_spec((seq, seq), lambda c, b: (0, 0)),
                  _const_spec((2, seq, HY_CW), lambda c, b: (0, 0, c)),
                  _const_spec((2, seq, HY_CW), lambda c, b: (0, 0, c)),
                  pl.BlockSpec((2, 1, HY_CW), lambda c, b: (0, 0, c)),
                  pl.BlockSpec((2, HY_CW), lambda c, b: (0, c)),
                  pl.BlockSpec((1, HY_CW), lambda c, b: (0, c))],
        out_specs=pl.BlockSpec((1, seq, HY_CW), lambda c, b: (b, 0, c)),
        out_shape=jax.ShapeDtypeStruct((bsz, seq, D_HY), BF16),
        scratch_shapes=[pltpu.VMEM((seq, HY_CW), F32), pltpu.VMEM((seq, HY_CW), F32),
                        pltpu.VMEM((seq, HY_CW), BF16), pltpu.VMEM((seq, HY_CW), BF16),
                        pltpu.VMEM((seq, HY_CW), BF16), pltpu.VMEM((1, HY_CW), F32)],
        compiler_params=_cparams(("arbitrary", "arbitrary")),
        name="hyena")(proj3, proj3, proj3, conv_w, conv_w, conv_w, conv_b, conv_b, conv_b,
                      cmat, smat, kr, ki, kn, skip, norm_w)


def _mlstm_kernel(qp_ref, kp_ref, v_ref, og_ref, wq_ref, wk_ref, bq_ref, bk_ref, gr_ref, gt_ref,
                  nw_ref, o_ref, qb_ref, kb_ref, hacc_ref, cst_ref, nst_ref):
    seq = qp_ref.shape[1]
    d = qp_ref.shape[2]
    nchunk = seq // CHUNK
    row = lax.broadcasted_iota(I32, (seq, d), 0)
    q = _short_conv(qp_ref[0].astype(F32), wq_ref, bq_ref, row, seq)
    k = _short_conv(kp_ref[0].astype(F32), wk_ref, bk_ref, row, seq)
    qb_ref[...] = (q * jax.nn.sigmoid(q)).astype(BF16)
    kb_ref[...] = ((k * jax.nn.sigmoid(k)) * (d ** -0.5)).astype(BF16)

    ti = lax.broadcasted_iota(I32, (CHUNK, CHUNK), 0)
    si = lax.broadcasted_iota(I32, (CHUNK, CHUNK), 1)
    lower = ti >= si
    upper = ti <= si
    lower_f = lower.astype(F32)
    upper_f = upper.astype(F32)
    nt = (((1,), (1,)), ((), ()))
    tn = (((0,), (0,)), ((), ()))
    chunk_rows = [slice(c * CHUNK, (c + 1) * CHUNK) for c in range(nchunk)]

    for direction in range(2):
        f_idx, i_idx = 2 * direction + 1, 2 * direction
        mask = lower if direction == 0 else upper
        order = list(range(nchunk)) if direction == 0 else list(range(nchunk - 1, -1, -1))
        lf_r = jax.nn.log_sigmoid(gr_ref[0, 0, f_idx])
        b_r = jnp.dot(lf_r, upper_f if direction == 0 else lower_f, precision=HP, preferred_element_type=F32)
        rterm = b_r - gr_ref[0, 0, i_idx]
        b_last = jnp.sum(lf_r, axis=-1, keepdims=True)
        lf_c = jax.nn.log_sigmoid(gt_ref[0, 0, f_idx])
        b_c = jnp.dot(lower_f if direction == 0 else upper_f, lf_c, precision=HP, preferred_element_type=F32)
        i_c = gt_ref[0, 0, i_idx]

        bcol, gcol, gmax, rowmax, blast = [], [], [], [], []
        for c in range(nchunk):
            bc = jnp.broadcast_to(b_c[:, c:c + 1], (CHUNK, CHUNK))
            ic = jnp.broadcast_to(i_c[:, c:c + 1], (CHUNK, CHUNK))
            bl = jnp.broadcast_to(b_last[c:c + 1, :], (1, CHUNK))
            dmat = jnp.where(mask, bc - rterm[c:c + 1, :], -jnp.inf)
            g = bl - bc + ic
            bcol.append(bc)
            gcol.append(g)
            blast.append(bl)
            rowmax.append(jnp.max(dmat, axis=-1, keepdims=True))
            gmax.append(jnp.max(g, axis=0, keepdims=True))

        m = jnp.zeros((1, CHUNK), F32)
        m_in, m_out = [None] * nchunk, [None] * nchunk
        for c in order:
            m_in[c] = m
            m = jnp.maximum(blast[c] + m, gmax[c])
            m_out[c] = m

        cmat = jnp.zeros((d, d), F32)
        nvec = jnp.zeros((1, d), F32)
        for c in order:
            kc = kb_ref[chunk_rows[c], :]
            vc = v_ref[0, chunk_rows[c], :]
            cst_ref[c] = cmat.astype(BF16)
            nst_ref[c] = nvec
            wg = jnp.exp(gcol[c] - m_out[c])
            decay = jnp.exp(blast[c] + m_in[c] - m_out[c])
            upd = lax.dot_general((wg * vc.astype(F32)).astype(BF16), kc, tn, preferred_element_type=F32)
            cmat = decay * cmat + upd
            nvec = decay * nvec + jnp.sum(wg * kc.astype(F32), axis=0, keepdims=True)

        for c in range(nchunk):
            rs = chunk_rows[c]
            qc = qb_ref[rs, :]
            kc = kb_ref[rs, :]
            vc = v_ref[0, rs, :]
            dmat = jnp.where(mask, bcol[c] - rterm[c:c + 1, :], -jnp.inf)
            inter = bcol[c] + m_in[c]
            m_t = jnp.maximum(inter, rowmax[c])
            p = jnp.exp(dmat - m_t)
            inter_w = jnp.exp(inter - m_t)
            s = lax.dot_general(qc, kc, nt, preferred_element_type=F32) * p
            cq = lax.dot_general(qc, cst_ref[c], nt, preferred_element_type=F32)
            num = jnp.dot(s.astype(BF16), vc, preferred_element_type=F32) + inter_w * cq
            nq = jnp.sum(qc.astype(F32) * nst_ref[c], axis=-1, keepdims=True)
            den = jnp.sum(s, axis=-1, keepdims=True) + inter_w * nq
            h = num / jnp.maximum(jnp.abs(den), jnp.exp(-m_t))
            if direction == 0:
                hacc_ref[rs, :] = h
            else:
                hacc_ref[rs, :] += h

    h = hacc_ref[...]
    mu = jnp.mean(h, axis=-1, keepdims=True)
    hc = h - mu
    var = jnp.mean(hc * hc, axis=-1, keepdims=True)
    y = hc * lax.rsqrt(var + LN_EPS) * nw_ref[...] * jax.nn.sigmoid(og_ref[0].astype(F32))
    o_ref[0] = y.astype(o_ref.dtype)


def _mlstm(proj3, conv_w, conv_b, grow, gtr, norm_w):
    bsz, seq, _ = proj3.shape
    d = HEAD_DIM
    nchunk = seq // CHUNK
    hy_blocks = 3 * D_HY // d
    qoff, koff, voff, ooff = hy_blocks, hy_blocks + ML_HEADS, hy_blocks + 2 * ML_HEADS, hy_blocks + 3 * ML_HEADS
    p_spec = lambda off: pl.BlockSpec((1, seq, d), lambda b, h: (b, 0, off + h))
    return pl.pallas_call(
        _mlstm_kernel,
        grid=(bsz, ML_HEADS),
        in_specs=[p_spec(qoff), p_spec(koff), p_spec(voff), p_spec(ooff),
                  pl.BlockSpec((3, d), lambda b, h: (0, h)),
                  pl.BlockSpec((3, d), lambda b, h: (0, ML_HEADS + h)),
                  pl.BlockSpec((1, d), lambda b, h: (0, h)),
                  pl.BlockSpec((1, d), lambda b, h: (0, ML_HEADS + h)),
                  pl.BlockSpec((1, 1, 4, nchunk, CHUNK), lambda b, h: (b, h, 0, 0, 0)),
                  pl.BlockSpec((1, 1, 4, CHUNK, nchunk), lambda b, h: (b, h, 0, 0, 0)),
                  pl.BlockSpec((1, d), lambda b, h: (0, h))],
        out_specs=pl.BlockSpec((1, seq, d), lambda b, h: (b, 0, h)),
        out_shape=jax.ShapeDtypeStruct((bsz, seq, D_ML), BF16),
        scratch_shapes=[pltpu.VMEM((seq, d), BF16), pltpu.VMEM((seq, d), BF16),
                        pltpu.VMEM((seq, d), F32),
                        pltpu.VMEM((nchunk, d, d), BF16), pltpu.VMEM((nchunk, 1, d), F32)],
        compiler_params=_cparams(("arbitrary", "arbitrary")),
        name="mlstm")(proj3, proj3, proj3, proj3, conv_w, conv_w, conv_b, conv_b, grow, gtr, norm_w)


def _to_slabs(y):
    n = y.shape[0]
    parts = jnp.stack([y[:, c * LANES:(c + 1) * LANES] for c in range(ROW_CHUNKS)], axis=0)
    return pltpu.einshape("crl->rcl", parts).reshape(n * ROW_CHUNKS, LANES)


def _from_slabs(v):
    n = v.shape[0] // ROW_CHUNKS
    parts = pltpu.einshape("rcl->crl", v.reshape(n, ROW_CHUNKS, LANES))
    return jnp.concatenate([parts[c] for c in range(ROW_CHUNKS)], axis=-1)


def _layer_norm(u, g, b):
    mu = jnp.mean(u, axis=-1, keepdims=True)
    uc = u - mu
    var = jnp.mean(uc * uc, axis=-1, keepdims=True)
    return uc * lax.rsqrt(var + LN_EPS) * g + b


def _outproj_kernel(yh_ref, ym_ref, x_ref, wa_ref, wb_ref, b_ref, g_ref, be_ref, o_ref, oc_ref):
    mix = (jnp.dot(yh_ref[...], wa_ref[...], preferred_element_type=F32)
           + jnp.dot(ym_ref[...], wb_ref[...], preferred_element_type=F32) + b_ref[...])
    y = _layer_norm(DN_ALPHA * x_ref[...] + mix, g_ref[...], be_ref[...])
    o_ref[...] = y
    oc_ref[...] = _to_slabs(y)


def _out_proj_ln(y_hy, y_ml, x2d, w_out_bf, b_out, g, be):
    t = x2d.shape[0]
    tm = OUT_TM
    vec = lambda: pl.BlockSpec((1, D_MODEL), lambda i: (0, 0))
    return pl.pallas_call(
        _outproj_kernel,
        grid=(t // tm,),
        in_specs=[pl.BlockSpec((tm, D_HY), lambda i: (i, 0)),
                  pl.BlockSpec((tm, D_ML), lambda i: (i, 0)),
                  pl.BlockSpec((tm, D_MODEL), lambda i: (i, 0)),
                  _const_spec((D_HY, D_MODEL), lambda i: (0, 0)),
                  _const_spec((D_ML, D_MODEL), lambda i: (1, 0)),
                  vec(), vec(), vec()],
        out_specs=[pl.BlockSpec((tm, D_MODEL), lambda i: (i, 0)),
                   pl.BlockSpec((tm * ROW_CHUNKS, LANES), lambda i: (i, 0))],
        out_shape=[jax.ShapeDtypeStruct((t, D_MODEL), F32),
                   jax.ShapeDtypeStruct((t * ROW_CHUNKS, LANES), F32)],
        compiler_params=_cparams(("arbitrary",)),
        name="out_proj_ln1")(y_hy, y_ml, x2d, w_out_bf, w_out_bf, b_out, g, be)


def _router_kernel(x_ref, w_ref, b_ref, ti_ref, tg_ref, tp_ref, cnt_ref):
    tm = ti_ref.shape[0]

    @pl.when(pl.program_id(0) == 0)
    def _():
        cnt_ref[...] = jnp.zeros_like(cnt_ref)

    logits = _dot3(x_ref[...], w_ref[...], NN) + b_ref[...]
    lane = lax.broadcasted_iota(I32, (tm, LANES), 1)
    work = logits
    vals, idxs = [], []
    chosen = jnp.zeros((tm, LANES), F32)
    for _ in range(TOP_K):
        mx = jnp.max(work, axis=-1, keepdims=True)
        idx = jnp.min(jnp.where(work == mx, lane, LANES), axis=-1, keepdims=True)
        hit = lane == idx
        vals.append(mx)
        idxs.append(idx)
        chosen = jnp.where(hit, 1.0, chosen)
        work = jnp.where(hit, -jnp.inf, work)
    exps = [jnp.exp(v - vals[0]) for v in vals]
    den = exps[0] + exps[1] + exps[2] + exps[3]
    ri = lax.broadcasted_iota(I32, (tm, tm), 0)
    ci = lax.broadcasted_iota(I32, (tm, tm), 1)
    strict_lower = (ri > ci).astype(BF16)
    carry = cnt_ref[...]
    slot = carry + jnp.dot(strict_lower, chosen.astype(BF16), preferred_element_type=F32)
    ti = jnp.zeros((tm, LANES), I32)
    tg = jnp.zeros((tm, LANES), F32)
    tp = jnp.zeros((tm, LANES), F32)
    for k in range(TOP_K):
        sk = jnp.sum(jnp.where(lane == idxs[k], slot, 0.0), axis=-1, keepdims=True)
        ti = jnp.where(lane == k, idxs[k], ti)
        tg = jnp.where(lane == k, exps[k] / den, tg)
        tp = jnp.where(lane == k, sk, tp)
    ti_ref[...] = ti
    tg_ref[...] = tg
    tp_ref[...] = tp.astype(I32)
    cnt_ref[...] = carry + jnp.sum(chosen, axis=0, keepdims=True)


def _router(x1, w_pad, b_pad):
    t = x1.shape[0]
    tm = RT_TM
    o_spec = lambda: pl.BlockSpec((tm, LANES), lambda i: (i, 0))
    return pl.pallas_call(
        _router_kernel,
        grid=(t // tm,),
        in_specs=[pl.BlockSpec((tm, D_MODEL), lambda i: (i, 0)),
                  pl.BlockSpec((D_MODEL, LANES), lambda i: (0, 0)),
                  pl.BlockSpec((1, LANES), lambda i: (0, 0))],
        out_specs=[o_spec(), o_spec(), o_spec(), pl.BlockSpec((1, LANES), lambda i: (0, 0))],
        out_shape=[jax.ShapeDtypeStruct((t, LANES), I32), jax.ShapeDtypeStruct((t, LANES), F32),
                   jax.ShapeDtypeStruct((t, LANES), I32), jax.ShapeDtypeStruct((1, LANES), F32)],
        compiler_params=_cparams(("arbitrary",)),
        name="router")(x1, w_pad, b_pad)


def _expert_kernel(te_ref, tr_ref, tb_ref, rt_ref, x_hbm, wg_ref, wu_ref, wd_ref, bg_ref, bu_ref, bd_ref, y_hbm,
                   stage_ref, xb_ref, acc_ref, ring_ref, wgb_ref, wub_ref, wdb_ref, gsem, osem):
    g = pl.program_id(0)
    n_items = pl.num_programs(0) - 1
    nf = D_FF // EXP_TF
    n_tiles = n_items // nf
    item = jnp.maximum(g - 1, 0)
    s = item // nf
    j = item % nf
    rows = tr_ref[s]
    active = jnp.logical_and(g >= 1, rows > 0)
    cast_slot = g % 2
    use_slot = (g + 1) % 2
    slab = EXP_CHUNK * ROW_CHUNKS
    per_step = EXP_ROWS // nf

    def row_copy(tok, r):
        return pltpu.make_async_copy(
            x_hbm.at[pl.ds(pl.multiple_of(tok * ROW_CHUNKS, ROW_CHUNKS), ROW_CHUNKS), :],
            stage_ref.at[pl.ds(pl.multiple_of(r * ROW_CHUNKS, ROW_CHUNKS), ROW_CHUNKS), :], gsem)

    def wait_gather():
        pltpu.make_async_copy(x_hbm.at[pl.ds(0, EXP_ROWS * ROW_CHUNKS), :], stage_ref, gsem).wait()

    def cast_weights():
        wgb_ref[cast_slot] = wg_ref[0].astype(BF16)
        wub_ref[cast_slot] = wu_ref[0].astype(BF16)
        wdb_ref[cast_slot] = wd_ref[0].astype(BF16)

    @pl.when(g == 0)
    def _():
        stage_ref[...] = jnp.zeros_like(stage_ref)
        acc_ref[...] = jnp.zeros_like(acc_ref)
        base = tb_ref[0]

        def group(q, carry):
            for u in range(GATHER_UNROLL):
                r = q * GATHER_UNROLL + u
                row_copy(rt_ref[base + r], r).start()
            return carry

        lax.fori_loop(0, EXP_ROWS // GATHER_UNROLL, group, 0)

    @pl.when(jnp.logical_not(active))
    def _():
        cast_weights()

    @pl.when(active)
    def _():
        @pl.when(j == 0)
        def _():
            wait_gather()
            for i in range(EXP_ROWS // EXP_CHUNK):
                xb_ref[i * EXP_CHUNK:(i + 1) * EXP_CHUNK, :] = _from_slabs(
                    stage_ref[i * slab:(i + 1) * slab, :]).astype(BF16)

        def step_body(m):
            cast_weights()
            nxt_base = tb_ref[jnp.minimum(s + 1, n_tiles - 1)]
            for u in range(per_step):
                r = j * per_step + u
                row_copy(rt_ref[nxt_base + r], r).start()

            xb = xb_ref[0:m, :]
            gate = jnp.dot(xb, wgb_ref[use_slot], preferred_element_type=F32) + bg_ref[0]
            up = jnp.dot(xb, wub_ref[use_slot], preferred_element_type=F32) + bu_ref[0]
            gate = jnp.minimum(gate, SWIGLU_LIMIT)
            up = jnp.clip(up, -SWIGLU_LIMIT, SWIGLU_LIMIT)
            act = (up + 1.0) * (gate * jax.nn.sigmoid(SWIGLU_ALPHA * gate))
            part = jnp.dot(act.astype(BF16), wdb_ref[use_slot], preferred_element_type=F32)
            acc_ref[0:m, :] = jnp.where(j == 0, jnp.broadcast_to(bd_ref[0], part.shape), acc_ref[0:m, :]) + part

        lo = 0
        for m in EXP_M_SIZES:
            @pl.when(jnp.logical_and(rows > lo, rows <= m))
            def _(m=m):
                step_body(m)
            lo = m

        @pl.when(j == nf - 1)
        def _():
            nchunk = (rows + EXP_CHUNK - 1) // EXP_CHUNK

            def chunk_copy(i, slot):
                dst0 = pl.multiple_of((s * EXP_ROWS + i * EXP_CHUNK) * ROW_CHUNKS, slab)
                return pltpu.make_async_copy(ring_ref.at[slot], y_hbm.at[pl.ds(dst0, slab), :], osem.at[slot])

            def emit(i, carry):
                slot = i % 2

                @pl.when(i >= 2)
                def _():
                    chunk_copy(i - 2, slot).wait()

                r0 = pl.multiple_of(i * EXP_CHUNK, EXP_CHUNK)
                ring_ref[slot] = _to_slabs(acc_ref[pl.ds(r0, EXP_CHUNK), :])
                chunk_copy(i, slot).start()
                return carry

            lax.fori_loop(0, nchunk, emit, 0)
            for back in range(2):
                @pl.when(nchunk > back)
                def _():
                    last = nchunk - 1 - back
                    chunk_copy(last, last % 2).wait()

    @pl.when(g == n_items)
    def _():
        wait_gather()


def _experts(tile_e, tile_rows, tile_base, row_tok, x1c, w_gu, b_gu, w_down, b_down):
    n_tiles = tile_e.shape[0]
    nf = D_FF // EXP_TF
    n_items = n_tiles * nf

    def item_block(item, te, tr):
        s = item // nf
        return te[s], jnp.where(tr[s] > 0, item % nf, nf - 1)

    def cast_item(g, te, tr):
        return item_block(jnp.minimum(g, n_items - 1), te, tr)

    def use_item(g, te, tr):
        return item_block(jnp.maximum(g - 1, 0), te, tr)

    def w_gate(g, te, tr, tb, rt):
        e, j = cast_item(g, te, tr)
        return e, 0, j

    def w_up(g, te, tr, tb, rt):
        e, j = cast_item(g, te, tr)
        return e, 0, nf + j

    def w_down_map(g, te, tr, tb, rt):
        e, j = cast_item(g, te, tr)
        return e, j, 0

    def b_gate(g, te, tr, tb, rt):
        e, j = use_item(g, te, tr)
        return e, 0, j

    def b_up(g, te, tr, tb, rt):
        e, j = use_item(g, te, tr)
        return e, 0, nf + j

    def b_down_map(g, te, tr, tb, rt):
        e, _ = use_item(g, te, tr)
        return e, 0, 0

    grid_spec = pltpu.PrefetchScalarGridSpec(
        num_scalar_prefetch=4,
        grid=(n_items + 1,),
        in_specs=[pl.BlockSpec(memory_space=pl.ANY),
                  pl.BlockSpec((1, D_MODEL, EXP_TF), w_gate),
                  pl.BlockSpec((1, D_MODEL, EXP_TF), w_up),
                  pl.BlockSpec((1, EXP_TF, D_MODEL), w_down_map),
                  pl.BlockSpec((1, 1, EXP_TF), b_gate),
                  pl.BlockSpec((1, 1, EXP_TF), b_up),
                  pl.BlockSpec((1, 1, D_MODEL), b_down_map)],
        out_specs=pl.BlockSpec(memory_space=pl.ANY),
        scratch_shapes=[pltpu.VMEM((EXP_ROWS * ROW_CHUNKS, LANES), F32),
                        pltpu.VMEM((EXP_ROWS, D_MODEL), BF16),
                        pltpu.VMEM((EXP_ROWS, D_MODEL), F32),
                        pltpu.VMEM((2, EXP_CHUNK * ROW_CHUNKS, LANES), F32),
                        pltpu.VMEM((2, D_MODEL, EXP_TF), BF16),
                        pltpu.VMEM((2, D_MODEL, EXP_TF), BF16),
                        pltpu.VMEM((2, EXP_TF, D_MODEL), BF16),
                        pltpu.SemaphoreType.DMA(()),
                        pltpu.SemaphoreType.DMA((2,))])
    return pl.pallas_call(
        _expert_kernel,
        grid_spec=grid_spec,
        out_shape=jax.ShapeDtypeStruct((n_tiles * EXP_ROWS * ROW_CHUNKS, LANES), F32),
        compiler_params=_cparams(("arbitrary",)),
        name="experts")(tile_e, tile_rows, tile_base, row_tok, x1c, w_gu, w_gu, w_down, b_gu, b_gu, b_down)


def _combine_kernel(dest_ref, y_hbm, x_ref, tg_ref, g_ref, be_ref, o_ref, buf_ref, sem):
    tm = o_ref.shape[0]
    i = pl.program_id(0)
    n = pl.num_programs(0)

    def row_copy(src_row, slot, k, t):
        return pltpu.make_async_copy(
            y_hbm.at[pl.ds(pl.multiple_of(src_row * ROW_CHUNKS, ROW_CHUNKS), ROW_CHUNKS), :],
            buf_ref.at[slot, k, pl.ds(pl.multiple_of(t * ROW_CHUNKS, ROW_CHUNKS), ROW_CHUNKS), :],
            sem.at[slot])

    def start_tile(tile, slot):
        base = tile * tm * TOP_K

        def body(t2, carry):
            for u in range(2):
                t = t2 * 2 + u
                for k in range(TOP_K):
                    row_copy(dest_ref[base + t * TOP_K + k], slot, k, t).start()
            return carry

        lax.fori_loop(0, tm // 2, body, 0)

    def wait_tile(slot):
        for k in range(TOP_K):
            pltpu.make_async_copy(y_hbm.at[pl.ds(0, tm * ROW_CHUNKS), :], buf_ref.at[slot, k], sem.at[slot]).wait()

    @pl.when(i == 0)
    def _():
        start_tile(0, 0)

    @pl.when(i + 1 < n)
    def _():
        start_tile(jnp.minimum(i + 1, n - 1), (i + 1) % 2)

    slot = i % 2
    wait_tile(slot)
    tg = tg_ref[...]
    ff = jnp.zeros((tm, D_MODEL), F32)
    for k in range(TOP_K):
        ff = ff + tg[:, k:k + 1] * _from_slabs(buf_ref[slot, k])
    o_ref[...] = _layer_norm(DN_ALPHA * x_ref[...] + ff, g_ref[...], be_ref[...])


def _combine_ln(dest_flat, y_buf, x1, tg, g, be):
    t = tg.shape[0]
    tm = CMB_TM
    grid_spec = pltpu.PrefetchScalarGridSpec(
        num_scalar_prefetch=1,
        grid=(t // tm,),
        in_specs=[pl.BlockSpec(memory_space=pl.ANY),
                  pl.BlockSpec((tm, D_MODEL), lambda i, d: (i, 0)),
                  pl.BlockSpec((tm, LANES), lambda i, d: (i, 0)),
                  pl.BlockSpec((1, D_MODEL), lambda i, d: (0, 0)),
                  pl.BlockSpec((1, D_MODEL), lambda i, d: (0, 0))],
        out_specs=pl.BlockSpec((tm, D_MODEL), lambda i, d: (i, 0)),
        scratch_shapes=[pltpu.VMEM((2, TOP_K, tm * ROW_CHUNKS, LANES), F32), pltpu.SemaphoreType.DMA((2,))])
    return pl.pallas_call(
        _combine_kernel,
        grid_spec=grid_spec,
        out_shape=jax.ShapeDtypeStruct((t, D_MODEL), F32),
        compiler_params=_cparams(("arbitrary",)),
        name="combine_ln2")(dest_flat, y_buf, x1, tg, g, be)


def _dft_kernel(cd_ref, sd_ref, ca_ref, sa_ref, c_ref, s_ref):
    cd, sd = cd_ref[...], sd_ref[...]
    ca, sa = ca_ref[0], sa_ref[0]
    c_ref[...] = (cd * ca - sd * sa).astype(c_ref.dtype)
    s_ref[...] = (sd * ca + cd * sa).astype(s_ref.dtype)


def _dft_tables(seq):
    n = 2 * seq
    nblk = seq // DFT_TB
    t = jnp.arange(seq, dtype=I32)

    def angles(f):
        return ((f[:, None] * t[None, :]) % n).astype(F32) * (2.0 * math.pi / n)

    ang_d = angles(jnp.arange(DFT_TB, dtype=I32))
    ang_a = angles(jnp.arange(nblk, dtype=I32) * DFT_TB)[:, None, :]
    small = pl.BlockSpec((DFT_TB, seq), lambda a: (0, 0))
    base = pl.BlockSpec((1, 1, seq), lambda a: (a, 0, 0))
    out = pl.BlockSpec((DFT_TB, seq), lambda a: (a, 0))
    sds = jax.ShapeDtypeStruct((seq, seq), BF16)
    return pl.pallas_call(
        _dft_kernel, grid=(nblk,), in_specs=[small, small, base, base], out_specs=[out, out],
        out_shape=[sds, sds], compiler_params=_cparams(("arbitrary",)),
        name="dft_tables")(jnp.cos(ang_d), jnp.sin(ang_d), jnp.cos(ang_a), jnp.sin(ang_a))


def _filter_features(seq):
    t = jnp.linspace(0.0, 1.0, seq, dtype=F32)[:, None]
    bands = (HY_EMB - 1) // 2
    fb = jnp.linspace(1e-4, bands - 1, bands, dtype=F32)[None]
    w = 2.0 * math.pi * jnp.arange(seq, dtype=F32)[:, None] / seq
    z = jnp.concatenate([t, jnp.cos(fb * w), -jnp.sin(fb * w)], -1)
    return jnp.pad(z, ((0, 0), (0, LANES - HY_EMB)))


def _mixer(x, w_in, b_in, hy_conv_w, hy_conv_b, hy_filt_w1, hy_filt_b1, hy_filt_w2, hy_filt_b2,
           hy_filt_w3, hy_filt_freq, hy_skip, hy_norm_w, ml_conv_w, ml_conv_b, ml_norm_w):
    bsz, seq, _ = x.shape
    t = bsz * seq
    x2d = x.reshape(t, D_MODEL)
    n_main = w_in.shape[1] - N_GATE_COLS
    w_t = jnp.swapaxes(w_in, 0, 1)
    proj = _in_proj(x2d, w_t[:n_main].astype(BF16), b_in[None, :n_main])
    proj3 = proj.reshape(bsz, seq, n_main)
    wg = jnp.pad(w_t[n_main:], ((0, LANES - N_GATE_COLS), (0, 0)))
    bg = jnp.pad(b_in[None, n_main:], ((0, 0), (0, LANES - N_GATE_COLS)))
    gates = _gate_proj(x2d, wg, bg)[:, :N_GATE_COLS]
    g5 = gates.reshape(bsz, seq, 4, ML_HEADS)
    grow = g5.transpose(0, 3, 2, 1).reshape(bsz, ML_HEADS, 4, seq // CHUNK, CHUNK)
    gtr = grow.transpose(0, 1, 2, 4, 3)

    cmat, smat = _dft_tables(seq)
    zpad = _filter_features(seq)
    w1pad = jnp.pad(hy_filt_w1, ((0, LANES - HY_EMB), (0, 0)))
    deltas = jnp.abs(jnp.linspace(math.log(HY_DECAY_TARGET) / HY_SLOW_PCT,
                                  math.log(HY_DECAY_TARGET) / HY_FAST_PCT, D_HY, dtype=F32))[None]
    kr, ki, kn = _hyena_filters(zpad, w1pad, hy_filt_b1[None], hy_filt_w2, hy_filt_b2[None],
                                hy_filt_freq, hy_filt_w3, deltas, cmat, smat)
    y_hy = _hyena(proj3, hy_conv_w, hy_conv_b[None], cmat, smat, kr, ki, kn, hy_skip, hy_norm_w[None])
    y_ml = _mlstm(proj3, ml_conv_w, ml_conv_b[None], grow, gtr, ml_norm_w[None])
    return y_hy.reshape(t, D_HY), y_ml.reshape(t, D_ML), x2d


def _moe_tables(top_i, slot, counts):
    t = top_i.shape[0]
    n_tiles = N_EXPERTS + (t * TOP_K) // EXP_ROWS
    ntile = (counts + EXP_ROWS - 1) // EXP_ROWS
    ends = jnp.cumsum(ntile)
    starts = ends - ntile
    total = ends[-1]
    s_idx = jnp.arange(n_tiles, dtype=I32)
    valid = s_idx < total
    s_eff = jnp.where(valid, s_idx, jnp.maximum(total - 1, 0))
    tile_e = jnp.minimum(jnp.sum((s_eff[:, None] >= ends[None, :]).astype(I32), axis=1), N_EXPERTS - 1)
    local = s_eff - starts[tile_e]
    tile_rows = jnp.where(valid, jnp.clip(counts[tile_e] - local * EXP_ROWS, 0, EXP_ROWS), 0).astype(I32)
    tok = jnp.arange(t, dtype=I32)[:, None]
    row_tok = jnp.pad(jnp.sort((top_i * t + tok).reshape(-1)) % t, (0, EXP_ROWS))
    first = jnp.cumsum(counts) - counts
    tile_base = (first[tile_e] + local * EXP_ROWS).astype(I32)
    onehot = top_i[:, :, None] == jnp.arange(N_EXPERTS, dtype=I32)
    dest = jnp.sum(jnp.where(onehot, starts * EXP_ROWS, 0), axis=-1) + slot
    return tile_e.astype(I32), tile_rows, tile_base, row_tok.astype(I32), dest.astype(I32).reshape(-1)


def kernel(x, w_in, b_in, hy_conv_w, hy_conv_b, hy_filt_w1, hy_filt_b1, hy_filt_w2, hy_filt_b2, hy_filt_w3, hy_filt_freq, hy_skip, hy_norm_w, ml_conv_w, ml_conv_b, ml_norm_w, w_out, b_out, ln1_g, ln1_b, router_w, router_b, w_gu, b_gu, w_down, b_down, ln2_g, ln2_b):
    bsz, seq, _ = x.shape
    l = 0
    y_hy, y_ml, x2d = _mixer(x, w_in[l], b_in[l], hy_conv_w[l], hy_conv_b[l], hy_filt_w1[l], hy_filt_b1[l],
                             hy_filt_w2[l], hy_filt_b2[l], hy_filt_w3[l], hy_filt_freq[l], hy_skip[l],
                             hy_norm_w[l], ml_conv_w[l], ml_conv_b[l], ml_norm_w[l])
    x1, x1c = _out_proj_ln(y_hy, y_ml, x2d, w_out[l].astype(BF16), b_out[l][None], ln1_g[l][None],
                           ln1_b[l][None])
    rw = jnp.pad(router_w[l], ((0, 0), (0, LANES - N_EXPERTS)))
    rb = jnp.pad(router_b[l][None], ((0, 0), (0, LANES - N_EXPERTS)), constant_values=-1e30)
    top_i, top_g, slot, cnt = _router(x1, rw, rb)
    counts = cnt[0, :N_EXPERTS].astype(I32)
    tile_e, tile_rows, tile_base, row_tok, dest = _moe_tables(top_i[:, :TOP_K], slot[:, :TOP_K], counts)
    y_buf = _experts(tile_e, tile_rows, tile_base, row_tok, x1c, w_gu[l], b_gu[l][:, None, :], w_down[l],
                     b_down[l][:, None, :])
    out = _combine_ln(dest, y_buf, x1, top_g, ln2_g[l][None], ln2_b[l][None])
    return out.reshape(bsz, seq, D_MODEL)
```

```python
import functools
import math

import jax
import jax.numpy as jnp
from jax import lax
from jax.experimental import pallas as pl
from jax.experimental.pallas import tpu as pltpu

F32 = jnp.float32
BF16 = jnp.bfloat16
I32 = jnp.int32
HP = lax.Precision.HIGHEST

D_MODEL = 2048
D_HY = 1024
D_ML = 1024
ML_HEADS = 8
HEAD_DIM = 128
CHUNK = 128
N_GATE_COLS = 32
HY_EMB = 33
N_EXPERTS = 32
TOP_K = 4
D_FF = 2048
SWIGLU_LIMIT = 7.0
SWIGLU_ALPHA = 1.702
LN_EPS = 1e-5
DN_ALPHA = 2.0 ** 0.25
HY_DECAY_TARGET = 1e-2
HY_FAST_PCT = 0.3
HY_SLOW_PCT = 1.5

LANES = 128
ROW_CHUNKS = D_MODEL // LANES
VMEM_LIMIT = 56 * 1024 * 1024

PROJ_TM = 2048
PROJ_TN = 512
HY_CW = 256
HY_FC = 512
OUT_TM = 512
RT_TM = 256
EXP_ROWS = 1152
EXP_M_SIZES = (1024, 1088, 1152)
EXP_CHUNK = 128
EXP_TF = 256
GATHER_UNROLL = 8
CMB_TM = 128
DFT_TB = 128


def _cparams(sem):
    return pltpu.CompilerParams(dimension_semantics=sem, vmem_limit_bytes=VMEM_LIMIT)


def _split(a):
    hi = a.astype(BF16)
    return hi, (a - hi.astype(F32)).astype(BF16)


def _dot3(a, b, dims):
    a_hi, a_lo = _split(a)
    b_hi, b_lo = _split(b)
    mm = functools.partial(lax.dot_general, dimension_numbers=dims, preferred_element_type=F32)
    return mm(a_hi, b_hi) + (mm(a_hi, b_lo) + mm(a_lo, b_hi))


NN = (((1,), (0,)), ((), ()))
NT = (((1,), (1,)), ((), ()))


def _const_spec(shape, index_map):
    return pl.BlockSpec(shape, index_map, pipeline_mode=pl.Buffered(1))


def _proj_kernel(x_ref, w_ref, b_ref, o_ref, xb_ref):
    @pl.when(pl.program_id(1) == 0)
    def _():
        xb_ref[...] = x_ref[...].astype(BF16)

    acc = lax.dot_general(xb_ref[...], w_ref[...], (((1,), (1,)), ((), ())), preferred_element_type=F32)
    o_ref[...] = (acc + b_ref[...]).astype(o_ref.dtype)


def _in_proj(x2d, wt_bf, b_row):
    m, k = x2d.shape
    n = wt_bf.shape[0]
    return pl.pallas_call(
        _proj_kernel,
        grid=(m // PROJ_TM, n // PROJ_TN),
        in_specs=[pl.BlockSpec((PROJ_TM, k), lambda i, j: (i, 0)),
                  pl.BlockSpec((PROJ_TN, k), lambda i, j: (j, 0)),
                  pl.BlockSpec((1, PROJ_TN), lambda i, j: (0, j))],
        out_specs=pl.BlockSpec((PROJ_TM, PROJ_TN), lambda i, j: (i, j)),
        out_shape=jax.ShapeDtypeStruct((m, n), BF16),
        scratch_shapes=[pltpu.VMEM((PROJ_TM, k), BF16)],
        compiler_params=_cparams(("arbitrary", "arbitrary")),
        name="in_proj")(x2d, wt_bf, b_row)


def _gate_kernel(x_ref, w_ref, b_ref, o_ref):
    o_ref[...] = _dot3(x_ref[...], w_ref[...], NT) + b_ref[...]


def _gate_proj(x2d, wt_pad, b_pad):
    m, k = x2d.shape
    tm = 512
    return pl.pallas_call(
        _gate_kernel,
        grid=(m // tm,),
        in_specs=[pl.BlockSpec((tm, k), lambda i: (i, 0)),
                  pl.BlockSpec((LANES, k), lambda i: (0, 0)),
                  pl.BlockSpec((1, LANES), lambda i: (0, 0))],
        out_specs=pl.BlockSpec((tm, LANES), lambda i: (i, 0)),
        out_shape=jax.ShapeDtypeStruct((m, LANES), F32),
        compiler_params=_cparams(("arbitrary",)),
        name="gate_proj")(x2d, wt_pad, b_pad)


def _filter_kernel(z_ref, w1_ref, b1_ref, w2_ref, b2_ref, fq_ref, w3f_ref, w3b_ref, dl_ref,
                   c_ref, s_ref, kr_ref, ki_ref, km_ref, h_ref):
    seq = z_ref.shape[0]
    inv_n = 1.0 / (2 * seq)
    z = z_ref[...]

    @pl.when(jnp.logical_and(pl.program_id(0) == 0, pl.program_id(1) == 0))
    def _():
        h1 = jnp.sin(fq_ref[0:1, :] * (jnp.dot(z, w1_ref[...], precision=HP, preferred_element_type=F32)
                                       + b1_ref[...]))
        h_ref[...] = jnp.sin(fq_ref[1:2, :] * (jnp.dot(h1, w2_ref[...], precision=HP, preferred_element_type=F32)
                                               + b2_ref[...]))

    half = seq // 2
    h = h_ref[...]
    win = jnp.exp(-z[:, 0:1] * dl_ref[...])
    fwd = _dot3(h, w3f_ref[...], NN) * win
    bwd = _dot3(h, w3b_ref[...], NN) * win
    row = lax.broadcasted_iota(I32, fwd.shape, 0)
    bwd = jnp.where(row == 0, 0.0, bwd)
    inv = 1.0 / jnp.sum(jnp.abs(fwd) + jnp.abs(bwd), axis=0, keepdims=True)
    ks = ((fwd + bwd) * inv)
    kd = ((fwd - bwd) * inv)
    ksb = ks.astype(BF16)
    kdb = kd.astype(BF16)
    ec = jnp.dot(c_ref[0], ksb[:half], preferred_element_type=F32)
    oc = jnp.dot(c_ref[1], ksb[half:], preferred_element_type=F32)
    es = jnp.dot(s_ref[0], kdb[:half], preferred_element_type=F32)
    os_ = jnp.dot(s_ref[1], kdb[half:], preferred_element_type=F32)
    rowh = lax.broadcasted_iota(I32, ec.shape, 0)
    wf = jnp.where(rowh == 0, inv_n, 2.0 * inv_n)
    kr_ref[0, :half, :] = (ec + oc) * wf
    kr_ref[0, half:, :] = (ec - oc) * wf
    ki_ref[0, :half, :] = -(es + os_) * wf
    ki_ref[0, half:, :] = (es - os_) * wf
    alt = jnp.where((rowh & 1) == 0, 1.0, -1.0)
    km_ref[0, 0:1, :] = jnp.sum(ks[:half] * alt, axis=0, keepdims=True) * (2.0 * inv_n)
    km_ref[0, 1:2, :] = -jnp.sum(kd[half:] * alt, axis=0, keepdims=True) * (2.0 * inv_n)


def _hyena_filters(zpad, w1pad, b1, w2, b2, freq, w3, deltas, cmat, smat):
    seq = zpad.shape[0]
    nb = D_HY // HY_CW
    hid = w2.shape[0]
    full = lambda shape: pl.BlockSpec(shape, lambda o, c: (0,) * len(shape))
    out_sds = jax.ShapeDtypeStruct((2, seq, D_HY), F32)
    return pl.pallas_call(
        _filter_kernel,
        grid=(2, nb),
        in_specs=[full(zpad.shape), full(w1pad.shape), full(b1.shape), full(w2.shape), full(b2.shape),
                  full(freq.shape),
                  pl.BlockSpec((hid, HY_CW), lambda o, c: (0, o * 2 * nb + c)),
                  pl.BlockSpec((hid, HY_CW), lambda o, c: (0, o * 2 * nb + nb + c)),
                  pl.BlockSpec((1, HY_CW), lambda o, c: (0, c)),
                  _const_spec(cmat.shape, lambda o, c: (0, 0, 0)),
                  _const_spec(smat.shape, lambda o, c: (0, 0, 0))],
        out_specs=[pl.BlockSpec((1, seq, HY_CW), lambda o, c: (o, 0, c)),
                   pl.BlockSpec((1, seq, HY_CW), lambda o, c: (o, 0, c)),
                   pl.BlockSpec((1, 2, HY_CW), lambda o, c: (o, 0, c))],
        out_shape=[out_sds, out_sds, jax.ShapeDtypeStruct((2, 2, D_HY), F32)],
        scratch_shapes=[pltpu.VMEM((seq, hid), F32)],
        compiler_params=_cparams(("arbitrary", "arbitrary")),
        name="hyena_filters")(zpad, w1pad, b1, w2, b2, freq, w3, w3, deltas, cmat, smat)


def _short_conv(u, w_ref, b_ref, row, seq):
    prev = jnp.where(row == 0, 0.0, pltpu.roll(u, 1, 0))
    nxt = jnp.where(row == seq - 1, 0.0, pltpu.roll(u, seq - 1, 0))
    return w_ref[0:1, :] * prev + w_ref[1:2, :] * u + w_ref[2:3, :] * nxt + b_ref[...]


def _hyena_kernel(uv_ref, u1_ref, u2_ref, wv_ref, w1_ref, w2_ref, bv_ref, b1_ref, b2_ref,
                  c_ref, s_ref, kr_ref, ki_ref, km_ref, skip_ref, nw_ref, o_ref,
                  a_ref, b_ref, t_ref, zb_ref, s1_ref, s2_ref, s3_ref, s4_ref, md_ref):
    seq = uv_ref.shape[1]
    cw = uv_ref.shape[2]
    half = seq // 2
    nblk = half // HY_FC
    row = lax.broadcasted_iota(I32, (seq, LANES), 0)
    alt_half = jnp.where((lax.broadcasted_iota(I32, (half, cw), 0) & 1) == 0, 1.0, -1.0)
    alt_blk = jnp.where((lax.broadcasted_iota(I32, (HY_FC, cw), 0) & 1) == 0, 1.0, -1.0)
    groups = [slice(g * LANES, (g + 1) * LANES) for g in range(cw // LANES)]

    def conv_to(dst_ref, u_ref, w_ref, bias_ref):
        for gi, gs in enumerate(groups):
            t_ref[gi] = _short_conv(u_ref[0, :, gs].astype(F32), w_ref.at[:, gs], bias_ref.at[:, gs], row, seq)
            dst_ref[0:half, gs] = t_ref[gi, pl.ds(0, half, stride=2), :]
            dst_ref[half:seq, gs] = t_ref[gi, pl.ds(1, half, stride=2), :]

    def spectrum(zin_ref, o):
        z = zin_ref[...]
        zb_ref[...] = z.astype(BF16)
        am = jnp.sum(z[:half] * alt_half, axis=0, keepdims=True)
        bm = jnp.sum(z[half:] * alt_half, axis=0, keepdims=True)
        krm, kim = km_ref[o, 0:1, :], km_ref[o, 1:2, :]
        md_ref[0:1, :] = am * krm + bm * kim
        md_ref[1:2, :] = am * kim - bm * krm
        ze = zb_ref[0:half, :]
        zo = zb_ref[half:seq, :]
        for gb in range(nblk):
            lo = slice(gb * HY_FC, (gb + 1) * HY_FC)
            hi = slice(half + gb * HY_FC, half + (gb + 1) * HY_FC)
            ec = jnp.dot(c_ref[0, lo, :], ze, preferred_element_type=F32)
            oc = jnp.dot(c_ref[1, lo, :], zo, preferred_element_type=F32)
            es = jnp.dot(s_ref[0, lo, :], ze, preferred_element_type=F32)
            os_ = jnp.dot(s_ref[1, lo, :], zo, preferred_element_type=F32)
            a_lo, b_lo = ec + oc, es + os_
            a_hi, b_hi = ec - oc, os_ - es
            krl, kil = kr_ref[o, lo, :], ki_ref[o, lo, :]
            krh, kih = kr_ref[o, hi, :], ki_ref[o, hi, :]
            pr = a_lo * krl + b_lo * kil
            pi = a_lo * kil - b_lo * krl
            qr = a_hi * krh + b_hi * kih
            qi = a_hi * kih - b_hi * krh
            s1_ref[lo, :] = (pr + qr).astype(BF16)
            s2_ref[lo, :] = (pi - qi).astype(BF16)
            s3_ref[lo, :] = (pr - qr).astype(BF16)
            s4_ref[lo, :] = (pi + qi).astype(BF16)

    def conv_rows(ub, parity):
        us = slice(ub * HY_FC, (ub + 1) * HY_FC)
        if parity == 0:
            y = jnp.dot(c_ref[0, us, :], s1_ref[...], preferred_element_type=F32)
            y = y - jnp.dot(s_ref[0, us, :], s2_ref[...], preferred_element_type=F32)
            y = y + md_ref[0:1, :] * alt_blk
        else:
            y = jnp.dot(c_ref[2, us, :], s3_ref[...], preferred_element_type=F32)
            y = y - jnp.dot(s_ref[2, us, :], s4_ref[...], preferred_element_type=F32)
            y = y - md_ref[1:2, :] * alt_blk
        return slice(parity * half + ub * HY_FC, parity * half + (ub + 1) * HY_FC), y

    blocks = [(ub, parity) for parity in range(2) for ub in range(nblk)]
    conv_to(a_ref, uv_ref, wv_ref, bv_ref)
    conv_to(b_ref, u1_ref, w1_ref, b1_ref)
    spectrum(a_ref, 0)
    for ub, parity in blocks:
        rows, y = conv_rows(ub, parity)
        b_ref[rows, :] = b_ref[rows, :] * (y + skip_ref[0:1, :] * a_ref[rows, :])
    conv_to(a_ref, u2_ref, w2_ref, b2_ref)
    spectrum(b_ref, 1)
    for ub, parity in blocks:
        rows, y = conv_rows(ub, parity)
        z = a_ref[rows, :] * (y + skip_ref[1:2, :] * b_ref[rows, :])
        for gi, gs in enumerate(groups):
            zg = z[:, gs]
            mu = jnp.mean(zg, axis=-1, keepdims=True)
            zc = zg - mu
            var = jnp.mean(zc * zc, axis=-1, keepdims=True)
            t_ref[gi, pl.ds(2 * ub * HY_FC + parity, HY_FC, stride=2), :] = (
                zc * lax.rsqrt(var + LN_EPS) * nw_ref[:, gs])
    for gi, gs in enumerate(groups):
        o_ref[0, :, gs] = t_ref[gi].astype(o_ref.dtype)


def _hyena(proj3, conv_w, conv_b, cmat, smat, kr, ki, km, skip, norm_w):
    bsz, seq, _ = proj3.shape
    half = seq // 2
    nb = D_HY // HY_CW
    u_spec = lambda off: pl.BlockSpec((1, seq, HY_CW), lambda c, b: (b, 0, off + c))
    w_spec = lambda off: pl.BlockSpec((3, HY_CW), lambda c, b: (0, off + c))
    b_spec = lambda off: pl.BlockSpec((1, HY_CW), lambda c, b: (0, off + c))
    return pl.pallas_call(
        _hyena_kernel,
        grid=(nb, bsz),
        in_specs=[u_spec(0), u_spec(nb), u_spec(2 * nb),
                  w_spec(0), w_spec(nb), w_spec(2 * nb),
                  b_spec(0), b_spec(nb), b_spec(2 * nb),
                  _const_spec(cmat.shape, lambda c, b: (0, 0, 0)),
                  _const_spec(smat.shape, lambda c, b: (0, 0, 0)),
                  _const_spec((2, seq, HY_CW), lambda c, b: (0, 0, c)),
                  _const_spec((2, seq, HY_CW), lambda c, b: (0, 0, c)),
                  pl.BlockSpec((2, 2, HY_CW), lambda c, b: (0, 0, c)),
                  pl.BlockSpec((2, HY_CW), lambda c, b: (0, c)),
                  pl.BlockSpec((1, HY_CW), lambda c, b: (0, c))],
        out_specs=pl.BlockSpec((1, seq, HY_CW), lambda c, b: (b, 0, c)),
        out_shape=jax.ShapeDtypeStruct((bsz, seq, D_HY), BF16),
        scratch_shapes=[pltpu.VMEM((seq, HY_CW), F32), pltpu.VMEM((seq, HY_CW), F32),
                        pltpu.VMEM((HY_CW // LANES, seq, LANES), F32), pltpu.VMEM((seq, HY_CW), BF16),
                        pltpu.VMEM((half, HY_CW), BF16), pltpu.VMEM((half, HY_CW), BF16),
                        pltpu.VMEM((half, HY_CW), BF16), pltpu.VMEM((half, HY_CW), BF16),
                        pltpu.VMEM((2, HY_CW), F32)],
        compiler_params=_cparams(("arbitrary", "arbitrary")),
        name="hyena")(proj3, proj3, proj3, conv_w, conv_w, conv_w, conv_b, conv_b, conv_b,
                      cmat, smat, kr, ki, km, skip, norm_w)


def _mlstm_kernel(qp_ref, kp_ref, v_ref, og_ref, wq_ref, wk_ref, bq_ref, bk_ref, gr_ref, gt_ref,
                  nw_ref, o_ref, qb_ref, kb_ref, hacc_ref, cst_ref, nst_ref):
    seq = qp_ref.shape[1]
    d = qp_ref.shape[2]
    nchunk = seq // CHUNK
    row = lax.broadcasted_iota(I32, (seq, d), 0)
    q = _short_conv(qp_ref[0].astype(F32), wq_ref, bq_ref, row, seq)
    k = _short_conv(kp_ref[0].astype(F32), wk_ref, bk_ref, row, seq)
    qb_ref[...] = (q * jax.nn.sigmoid(q)).astype(BF16)
    kb_ref[...] = ((k * jax.nn.sigmoid(k)) * (d ** -0.5)).astype(BF16)

    ti = lax.broadcasted_iota(I32, (CHUNK, CHUNK), 0)
    si = lax.broadcasted_iota(I32, (CHUNK, CHUNK), 1)
    lower = ti >= si
    upper = ti <= si
    lower_f = lower.astype(F32)
    upper_f = upper.astype(F32)
    nt = (((1,), (1,)), ((), ()))
    tn = (((0,), (0,)), ((), ()))
    chunk_rows = [slice(c * CHUNK, (c + 1) * CHUNK) for c in range(nchunk)]

    for direction in range(2):
        f_idx, i_idx = 2 * direction + 1, 2 * direction
        mask = lower if direction == 0 else upper
        order = list(range(nchunk)) if direction == 0 else list(range(nchunk - 1, -1, -1))
        lf_r = jax.nn.log_sigmoid(gr_ref[0, 0, f_idx])
        b_r = jnp.dot(lf_r, upper_f if direction == 0 else lower_f, precision=HP, preferred_element_type=F32)
        rterm = b_r - gr_ref[0, 0, i_idx]
        b_last = jnp.sum(lf_r, axis=-1, keepdims=True)
        lf_c = jax.nn.log_sigmoid(gt_ref[0, 0, f_idx])
        b_c = jnp.dot(lower_f if direction == 0 else upper_f, lf_c, precision=HP, preferred_element_type=F32)
        i_c = gt_ref[0, 0, i_idx]

        bcol, gcol, gmax, rowmax, blast = [], [], [], [], []
        for c in range(nchunk):
            bc = jnp.broadcast_to(b_c[:, c:c + 1], (CHUNK, CHUNK))
            ic = jnp.broadcast_to(i_c[:, c:c + 1], (CHUNK, CHUNK))
            bl = jnp.broadcast_to(b_last[c:c + 1, :], (1, CHUNK))
            dmat = jnp.where(mask, bc - rterm[c:c + 1, :], -jnp.inf)
            g = bl - bc + ic
            bcol.append(bc)
            gcol.append(g)
            blast.append(bl)
            rowmax.append(jnp.max(dmat, axis=-1, keepdims=True))
            gmax.append(jnp.max(g, axis=0, keepdims=True))

        m = jnp.zeros((1, CHUNK), F32)
        m_in, m_out = [None] * nchunk, [None] * nchunk
        for c in order:
            m_in[c] = m
            m = jnp.maximum(blast[c] + m, gmax[c])
            m_out[c] = m

        cmat = jnp.zeros((d, d), F32)
        nvec = jnp.zeros((1, d), F32)
        for c in order:
            kc = kb_ref[chunk_rows[c], :]
            vc = v_ref[0, chunk_rows[c], :]
            cst_ref[c] = cmat.astype(BF16)
            nst_ref[c] = nvec
            wg = jnp.exp(gcol[c] - m_out[c])
            decay = jnp.exp(blast[c] + m_in[c] - m_out[c])
            upd = lax.dot_general((wg * vc.astype(F32)).astype(BF16), kc, tn, preferred_element_type=F32)
            cmat = decay * cmat + upd
            nvec = decay * nvec + jnp.sum(wg * kc.astype(F32), axis=0, keepdims=True)

        for c in range(nchunk):
            rs = chunk_rows[c]
            qc = qb_ref[rs, :]
            kc = kb_ref[rs, :]
            vc = v_ref[0, rs, :]
            dmat = jnp.where(mask, bcol[c] - rterm[c:c + 1, :], -jnp.inf)
            inter = bcol[c] + m_in[c]
            m_t = jnp.maximum(inter, rowmax[c])
            p = jnp.exp(dmat - m_t)
            inter_w = jnp.exp(inter - m_t)
            s = lax.dot_general(qc, kc, nt, preferred_element_type=F32) * p
            cq = lax.dot_general(qc, cst_ref[c], nt, preferred_element_type=F32)
            num = jnp.dot(s.astype(BF16), vc, preferred_element_type=F32) + inter_w * cq
            nq = jnp.sum(qc.astype(F32) * nst_ref[c], axis=-1, keepdims=True)
            den = jnp.sum(s, axis=-1, keepdims=True) + inter_w * nq
            h = num / jnp.maximum(jnp.abs(den), jnp.exp(-m_t))
            if direction == 0:
                hacc_ref[rs, :] = h
            else:
                hacc_ref[rs, :] += h

    h = hacc_ref[...]
    mu = jnp.mean(h, axis=-1, keepdims=True)
    hc = h - mu
    var = jnp.mean(hc * hc, axis=-1, keepdims=True)
    y = hc * lax.rsqrt(var + LN_EPS) * nw_ref[...] * jax.nn.sigmoid(og_ref[0].astype(F32))
    o_ref[0] = y.astype(o_ref.dtype)


def _mlstm(proj3, conv_w, conv_b, grow, gtr, norm_w):
    bsz, seq, _ = proj3.shape
    d = HEAD_DIM
    nchunk = seq // CHUNK
    hy_blocks = 3 * D_HY // d
    qoff, koff, voff, ooff = hy_blocks, hy_blocks + ML_HEADS, hy_blocks + 2 * ML_HEADS, hy_blocks + 3 * ML_HEADS
    p_spec = lambda off: pl.BlockSpec((1, seq, d), lambda b, h: (b, 0, off + h))
    return pl.pallas_call(
        _mlstm_kernel,
        grid=(bsz, ML_HEADS),
        in_specs=[p_spec(qoff), p_spec(koff), p_spec(voff), p_spec(ooff),
                  pl.BlockSpec((3, d), lambda b, h: (0, h)),
                  pl.BlockSpec((3, d), lambda b, h: (0, ML_HEADS + h)),
                  pl.BlockSpec((1, d), lambda b, h: (0, h)),
                  pl.BlockSpec((1, d), lambda b, h: (0, ML_HEADS + h)),
                  pl.BlockSpec((1, 1, 4, nchunk, CHUNK), lambda b, h: (b, h, 0, 0, 0)),
                  pl.BlockSpec((1, 1, 4, CHUNK, nchunk), lambda b, h: (b, h, 0, 0, 0)),
                  pl.BlockSpec((1, d), lambda b, h: (0, h))],
        out_specs=pl.BlockSpec((1, seq, d), lambda b, h: (b, 0, h)),
        out_shape=jax.ShapeDtypeStruct((bsz, seq, D_ML), BF16),
        scratch_shapes=[pltpu.VMEM((seq, d), BF16), pltpu.VMEM((seq, d), BF16),
                        pltpu.VMEM((seq, d), F32),
                        pltpu.VMEM((nchunk, d, d), BF16), pltpu.VMEM((nchunk, 1, d), F32)],
        compiler_params=_cparams(("arbitrary", "arbitrary")),
        name="mlstm")(proj3, proj3, proj3, proj3, conv_w, conv_w, conv_b, conv_b, grow, gtr, norm_w)


def _to_slabs(y):
    n = y.shape[0]
    parts = jnp.stack([y[:, c * LANES:(c + 1) * LANES] for c in range(ROW_CHUNKS)], axis=0)
    return pltpu.einshape("crl->rcl", parts).reshape(n * ROW_CHUNKS, LANES)


def _from_slabs(v):
    n = v.shape[0] // ROW_CHUNKS
    parts = pltpu.einshape("rcl->crl", v.reshape(n, ROW_CHUNKS, LANES))
    return jnp.concatenate([parts[c] for c in range(ROW_CHUNKS)], axis=-1)


def _layer_norm(u, g, b):
    mu = jnp.mean(u, axis=-1, keepdims=True)
    uc = u - mu
    var = jnp.mean(uc * uc, axis=-1, keepdims=True)
    return uc * lax.rsqrt(var + LN_EPS) * g + b


def _outproj_kernel(yh_ref, ym_ref, x_ref, wa_ref, wb_ref, b_ref, g_ref, be_ref, o_ref, oc_ref):
    mix = (jnp.dot(yh_ref[...], wa_ref[...], preferred_element_type=F32)
           + jnp.dot(ym_ref[...], wb_ref[...], preferred_element_type=F32) + b_ref[...])
    y = _layer_norm(DN_ALPHA * x_ref[...] + mix, g_ref[...], be_ref[...])
    o_ref[...] = y
    oc_ref[...] = _to_slabs(y)


def _out_proj_ln(y_hy, y_ml, x2d, w_out_bf, b_out, g, be):
    t = x2d.shape[0]
    tm = OUT_TM
    vec = lambda: pl.BlockSpec((1, D_MODEL), lambda i: (0, 0))
    return pl.pallas_call(
        _outproj_kernel,
        grid=(t // tm,),
        in_specs=[pl.BlockSpec((tm, D_HY), lambda i: (i, 0)),
                  pl.BlockSpec((tm, D_ML), lambda i: (i, 0)),
                  pl.BlockSpec((tm, D_MODEL), lambda i: (i, 0)),
                  _const_spec((D_HY, D_MODEL), lambda i: (0, 0)),
                  _const_spec((D_ML, D_MODEL), lambda i: (1, 0)),
                  vec(), vec(), vec()],
        out_specs=[pl.BlockSpec((tm, D_MODEL), lambda i: (i, 0)),
                   pl.BlockSpec((tm * ROW_CHUNKS, LANES), lambda i: (i, 0))],
        out_shape=[jax.ShapeDtypeStruct((t, D_MODEL), F32),
                   jax.ShapeDtypeStruct((t * ROW_CHUNKS, LANES), F32)],
        compiler_params=_cparams(("arbitrary",)),
        name="out_proj_ln1")(y_hy, y_ml, x2d, w_out_bf, w_out_bf, b_out, g, be)


def _router_kernel(x_ref, w_ref, b_ref, ti_ref, tg_ref, tp_ref, cnt_ref):
    tm = ti_ref.shape[0]

    @pl.when(pl.program_id(0) == 0)
    def _():
        cnt_ref[...] = jnp.zeros_like(cnt_ref)

    logits = _dot3(x_ref[...], w_ref[...], NN) + b_ref[...]
    lane = lax.broadcasted_iota(I32, (tm, LANES), 1)
    work = logits
    vals, idxs = [], []
    chosen = jnp.zeros((tm, LANES), F32)
    for _ in range(TOP_K):
        mx = jnp.max(work, axis=-1, keepdims=True)
        idx = jnp.min(jnp.where(work == mx, lane, LANES), axis=-1, keepdims=True)
        hit = lane == idx
        vals.append(mx)
        idxs.append(idx)
        chosen = jnp.where(hit, 1.0, chosen)
        work = jnp.where(hit, -jnp.inf, work)
    exps = [jnp.exp(v - vals[0]) for v in vals]
    den = exps[0] + exps[1] + exps[2] + exps[3]
    ri = lax.broadcasted_iota(I32, (tm, tm), 0)
    ci = lax.broadcasted_iota(I32, (tm, tm), 1)
    strict_lower = (ri > ci).astype(BF16)
    carry = cnt_ref[...]
    slot = carry + jnp.dot(strict_lower, chosen.astype(BF16), preferred_element_type=F32)
    ti = jnp.zeros((tm, LANES), I32)
    tg = jnp.zeros((tm, LANES), F32)
    tp = jnp.zeros((tm, LANES), F32)
    for k in range(TOP_K):
        sk = jnp.sum(jnp.where(lane == idxs[k], slot, 0.0), axis=-1, keepdims=True)
        ti = jnp.where(lane == k, idxs[k], ti)
        tg = jnp.where(lane == k, exps[k] / den, tg)
        tp = jnp.where(lane == k, sk, tp)
    ti_ref[...] = ti
    tg_ref[...] = tg
    tp_ref[...] = tp.astype(I32)
    cnt_ref[...] = carry + jnp.sum(chosen, axis=0, keepdims=True)


def _router(x1, w_pad, b_pad):
    t = x1.shape[0]
    tm = RT_TM
    o_spec = lambda: pl.BlockSpec((tm, LANES), lambda i: (i, 0))
    return pl.pallas_call(
        _router_kernel,
        grid=(t // tm,),
        in_specs=[pl.BlockSpec((tm, D_MODEL), lambda i: (i, 0)),
                  pl.BlockSpec((D_MODEL, LANES), lambda i: (0, 0)),
                  pl.BlockSpec((1, LANES), lambda i: (0, 0))],
        out_specs=[o_spec(), o_spec(), o_spec(), pl.BlockSpec((1, LANES), lambda i: (0, 0))],
        out_shape=[jax.ShapeDtypeStruct((t, LANES), I32), jax.ShapeDtypeStruct((t, LANES), F32),
                   jax.ShapeDtypeStruct((t, LANES), I32), jax.ShapeDtypeStruct((1, LANES), F32)],
        compiler_params=_cparams(("arbitrary",)),
        name="router")(x1, w_pad, b_pad)


def _expert_kernel(te_ref, tr_ref, tb_ref, rt_ref, x_hbm, wg_ref, wu_ref, wd_ref, bg_ref, bu_ref, bd_ref, y_hbm,
                   stage_ref, xb_ref, acc_ref, ring_ref, wgb_ref, wub_ref, wdb_ref, gsem, osem):
    g = pl.program_id(0)
    n_items = pl.num_programs(0) - 1
    nf = D_FF // EXP_TF
    n_tiles = n_items // nf
    item = jnp.maximum(g - 1, 0)
    s = item // nf
    j = item % nf
    rows = tr_ref[s]
    active = jnp.logical_and(g >= 1, rows > 0)
    cast_slot = g % 2
    use_slot = (g + 1) % 2
    slab = EXP_CHUNK * ROW_CHUNKS
    per_step = EXP_ROWS // nf

    def row_copy(tok, r):
        return pltpu.make_async_copy(
            x_hbm.at[pl.ds(pl.multiple_of(tok * ROW_CHUNKS, ROW_CHUNKS), ROW_CHUNKS), :],
            stage_ref.at[pl.ds(pl.multiple_of(r * ROW_CHUNKS, ROW_CHUNKS), ROW_CHUNKS), :], gsem)

    def wait_gather():
        pltpu.make_async_copy(x_hbm.at[pl.ds(0, EXP_ROWS * ROW_CHUNKS), :], stage_ref, gsem).wait()

    def cast_weights():
        wgb_ref[cast_slot] = wg_ref[0].astype(BF16)
        wub_ref[cast_slot] = wu_ref[0].astype(BF16)
        wdb_ref[cast_slot] = wd_ref[0].astype(BF16)

    @pl.when(g == 0)
    def _():
        stage_ref[...] = jnp.zeros_like(stage_ref)
        acc_ref[...] = jnp.zeros_like(acc_ref)
        base = tb_ref[0]

        def group(q, carry):
            for u in range(GATHER_UNROLL):
                r = q * GATHER_UNROLL + u
                row_copy(rt_ref[base + r], r).start()
            return carry

        lax.fori_loop(0, EXP_ROWS // GATHER_UNROLL, group, 0)

    @pl.when(jnp.logical_not(active))
    def _():
        cast_weights()

    @pl.when(active)
    def _():
        @pl.when(j == 0)
        def _():
            wait_gather()
            for i in range(EXP_ROWS // EXP_CHUNK):
                xb_ref[i * EXP_CHUNK:(i + 1) * EXP_CHUNK, :] = _from_slabs(
                    stage_ref[i * slab:(i + 1) * slab, :]).astype(BF16)

        def step_body(m):
            cast_weights()
            nxt_base = tb_ref[jnp.minimum(s + 1, n_tiles - 1)]
            for u in range(per_step):
                r = j * per_step + u
                row_copy(rt_ref[nxt_base + r], r).start()

            xb = xb_ref[0:m, :]
            gate = jnp.dot(xb, wgb_ref[use_slot], preferred_element_type=F32) + bg_ref[0]
            up = jnp.dot(xb, wub_ref[use_slot], preferred_element_type=F32) + bu_ref[0]
            gate = jnp.minimum(gate, SWIGLU_LIMIT)
            up = jnp.clip(up, -SWIGLU_LIMIT, SWIGLU_LIMIT)
            act = (up + 1.0) * (gate * jax.nn.sigmoid(SWIGLU_ALPHA * gate))
            part = jnp.dot(act.astype(BF16), wdb_ref[use_slot], preferred_element_type=F32)
            acc_ref[0:m, :] = jnp.where(j == 0, jnp.broadcast_to(bd_ref[0], part.shape), acc_ref[0:m, :]) + part

        lo = 0
        for m in EXP_M_SIZES:
            @pl.when(jnp.logical_and(rows > lo, rows <= m))
            def _(m=m):
                step_body(m)
            lo = m

        @pl.when(j == nf - 1)
        def _():
            nchunk = (rows + EXP_CHUNK - 1) // EXP_CHUNK

            def chunk_copy(i, slot):
                dst0 = pl.multiple_of((s * EXP_ROWS + i * EXP_CHUNK) * ROW_CHUNKS, slab)
                return pltpu.make_async_copy(ring_ref.at[slot], y_hbm.at[pl.ds(dst0, slab), :], osem.at[slot])

            def emit(i, carry):
                slot = i % 2

                @pl.when(i >= 2)
                def _():
                    chunk_copy(i - 2, slot).wait()

                r0 = pl.multiple_of(i * EXP_CHUNK, EXP_CHUNK)
                ring_ref[slot] = _to_slabs(acc_ref[pl.ds(r0, EXP_CHUNK), :])
                chunk_copy(i, slot).start()
                return carry

            lax.fori_loop(0, nchunk, emit, 0)
            for back in range(2):
                @pl.when(nchunk > back)
                def _():
                    last = nchunk - 1 - back
                    chunk_copy(last, last % 2).wait()

    @pl.when(g == n_items)
    def _():
        wait_gather()


def _experts(tile_e, tile_rows, tile_base, row_tok, x1c, w_gu, b_gu, w_down, b_down):
    n_tiles = tile_e.shape[0]
    nf = D_FF // EXP_TF
    n_items = n_tiles * nf

    def item_block(item, te, tr):
        s = item // nf
        return te[s], jnp.where(tr[s] > 0, item % nf, nf - 1)

    def cast_item(g, te, tr):
        return item_block(jnp.minimum(g, n_items - 1), te, tr)

    def use_item(g, te, tr):
        return item_block(jnp.maximum(g - 1, 0), te, tr)

    def w_gate(g, te, tr, tb, rt):
        e, j = cast_item(g, te, tr)
        return e, 0, j

    def w_up(g, te, tr, tb, rt):
        e, j = cast_item(g, te, tr)
        return e, 0, nf + j

    def w_down_map(g, te, tr, tb, rt):
        e, j = cast_item(g, te, tr)
        return e, j, 0

    def b_gate(g, te, tr, tb, rt):
        e, j = use_item(g, te, tr)
        return e, 0, j

    def b_up(g, te, tr, tb, rt):
        e, j = use_item(g, te, tr)
        return e, 0, nf + j

    def b_down_map(g, te, tr, tb, rt):
        e, _ = use_item(g, te, tr)
        return e, 0, 0

    grid_spec = pltpu.PrefetchScalarGridSpec(
        num_scalar_prefetch=4,
        grid=(n_items + 1,),
        in_specs=[pl.BlockSpec(memory_space=pl.ANY),
                  pl.BlockSpec((1, D_MODEL, EXP_TF), w_gate),
                  pl.BlockSpec((1, D_MODEL, EXP_TF), w_up),
                  pl.BlockSpec((1, EXP_TF, D_MODEL), w_down_map),
                  pl.BlockSpec((1, 1, EXP_TF), b_gate),
                  pl.BlockSpec((1, 1, EXP_TF), b_up),
                  pl.BlockSpec((1, 1, D_MODEL), b_down_map)],
        out_specs=pl.BlockSpec(memory_space=pl.ANY),
        scratch_shapes=[pltpu.VMEM((EXP_ROWS * ROW_CHUNKS, LANES), F32),
                        pltpu.VMEM((EXP_ROWS, D_MODEL), BF16),
                        pltpu.VMEM((EXP_ROWS, D_MODEL), F32),
                        pltpu.VMEM((2, EXP_CHUNK * ROW_CHUNKS, LANES), F32),
                        pltpu.VMEM((2, D_MODEL, EXP_TF), BF16),
                        pltpu.VMEM((2, D_MODEL, EXP_TF), BF16),
                        pltpu.VMEM((2, EXP_TF, D_MODEL), BF16),
                        pltpu.SemaphoreType.DMA(()),
                        pltpu.SemaphoreType.DMA((2,))])
    return pl.pallas_call(
        _expert_kernel,
        grid_spec=grid_spec,
        out_shape=jax.ShapeDtypeStruct((n_tiles * EXP_ROWS * ROW_CHUNKS, LANES), F32),
        compiler_params=_cparams(("arbitrary",)),
        name="experts")(tile_e, tile_rows, tile_base, row_tok, x1c, w_gu, w_gu, w_down, b_gu, b_gu, b_down)


def _combine_kernel(dest_ref, y_hbm, x_ref, tg_ref, g_ref, be_ref, o_ref, buf_ref, sem):
    tm = o_ref.shape[0]
    i = pl.program_id(0)
    n = pl.num_programs(0)

    def row_copy(src_row, slot, k, t):
        return pltpu.make_async_copy(
            y_hbm.at[pl.ds(pl.multiple_of(src_row * ROW_CHUNKS, ROW_CHUNKS), ROW_CHUNKS), :],
            buf_ref.at[slot, k, pl.ds(pl.multiple_of(t * ROW_CHUNKS, ROW_CHUNKS), ROW_CHUNKS), :],
            sem.at[slot])

    def start_tile(tile, slot):
        base = tile * tm * TOP_K

        def body(t2, carry):
            for u in range(2):
                t = t2 * 2 + u
                for k in range(TOP_K):
                    row_copy(dest_ref[base + t * TOP_K + k], slot, k, t).start()
            return carry

        lax.fori_loop(0, tm // 2, body, 0)

    def wait_tile(slot):
        for k in range(TOP_K):
            pltpu.make_async_copy(y_hbm.at[pl.ds(0, tm * ROW_CHUNKS), :], buf_ref.at[slot, k], sem.at[slot]).wait()

    @pl.when(i == 0)
    def _():
        start_tile(0, 0)

    @pl.when(i + 1 < n)
    def _():
        start_tile(jnp.minimum(i + 1, n - 1), (i + 1) % 2)

    slot = i % 2
    wait_tile(slot)
    tg = tg_ref[...]
    ff = jnp.zeros((tm, D_MODEL), F32)
    for k in range(TOP_K):
        ff = ff + tg[:, k:k + 1] * _from_slabs(buf_ref[slot, k])
    o_ref[...] = _layer_norm(DN_ALPHA * x_ref[...] + ff, g_ref[...], be_ref[...])


def _combine_ln(dest_flat, y_buf, x1, tg, g, be):
    t = tg.shape[0]
    tm = CMB_TM
    grid_spec = pltpu.PrefetchScalarGridSpec(
        num_scalar_prefetch=1,
        grid=(t // tm,),
        in_specs=[pl.BlockSpec(memory_space=pl.ANY),
                  pl.BlockSpec((tm, D_MODEL), lambda i, d: (i, 0)),
                  pl.BlockSpec((tm, LANES), lambda i, d: (i, 0)),
                  pl.BlockSpec((1, D_MODEL), lambda i, d: (0, 0)),
                  pl.BlockSpec((1, D_MODEL), lambda i, d: (0, 0))],
        out_specs=pl.BlockSpec((tm, D_MODEL), lambda i, d: (i, 0)),
        scratch_shapes=[pltpu.VMEM((2, TOP_K, tm * ROW_CHUNKS, LANES), F32), pltpu.SemaphoreType.DMA((2,))])
    return pl.pallas_call(
        _combine_kernel,
        grid_spec=grid_spec,
        out_shape=jax.ShapeDtypeStruct((t, D_MODEL), F32),
        compiler_params=_cparams(("arbitrary",)),
        name="combine_ln2")(dest_flat, y_buf, x1, tg, g, be)


def _dft_kernel(cd_ref, sd_ref, ca_ref, sa_ref, c_ref, s_ref):
    cd, sd = cd_ref[0], sd_ref[0]
    ca, sa = ca_ref[0], sa_ref[0]
    c_ref[0] = (cd * ca - sd * sa).astype(c_ref.dtype)
    s_ref[0] = (sd * ca + cd * sa).astype(s_ref.dtype)


def _dft_tables(seq):
    n = 2 * seq
    half = seq // 2
    nblk = half // DFT_TB
    idx = jnp.arange(half, dtype=I32)
    off = jnp.arange(DFT_TB, dtype=I32)
    start = jnp.arange(nblk, dtype=I32) * DFT_TB

    def angle(prod):
        return (prod % n).astype(F32) * (2.0 * math.pi / n)

    ang_d = jnp.stack([angle(off[:, None] * (2 * idx)[None, :]),
                       angle(off[:, None] * (2 * idx + 1)[None, :]),
                       angle((2 * off + 1)[:, None] * idx[None, :])])
    ang_a = jnp.stack([angle(start[:, None] * (2 * idx)[None, :]),
                       angle(start[:, None] * (2 * idx + 1)[None, :]),
                       angle((2 * start)[:, None] * idx[None, :])]).reshape(3 * nblk, 1, half)
    small = pl.BlockSpec((1, DFT_TB, half), lambda k, a: (k, 0, 0))
    base = pl.BlockSpec((1, 1, half), lambda k, a: (k * nblk + a, 0, 0))
    out = pl.BlockSpec((1, DFT_TB, half), lambda k, a: (k, a, 0))
    sds = jax.ShapeDtypeStruct((3, half, half), BF16)
    return pl.pallas_call(
        _dft_kernel, grid=(3, nblk), in_specs=[small, small, base, base], out_specs=[out, out],
        out_shape=[sds, sds], compiler_params=_cparams(("arbitrary", "arbitrary")),
        name="dft_tables")(jnp.cos(ang_d), jnp.sin(ang_d), jnp.cos(ang_a), jnp.sin(ang_a))


def _filter_features(seq):
    t = jnp.linspace(0.0, 1.0, seq, dtype=F32)[:, None]
    bands = (HY_EMB - 1) // 2
    fb = jnp.linspace(1e-4, bands - 1, bands, dtype=F32)[None]
    w = 2.0 * math.pi * jnp.arange(seq, dtype=F32)[:, None] / seq
    z = jnp.concatenate([t, jnp.cos(fb * w), -jnp.sin(fb * w)], -1)
    z = jnp.concatenate([z[0::2], z[1::2]], axis=0)
    return jnp.pad(z, ((0, 0), (0, LANES - HY_EMB)))


def _mixer(x, w_in, b_in, hy_conv_w, hy_conv_b, hy_filt_w1, hy_filt_b1, hy_filt_w2, hy_filt_b2,
           hy_filt_w3, hy_filt_freq, hy_skip, hy_norm_w, ml_conv_w, ml_conv_b, ml_norm_w):
    bsz, seq, _ = x.shape
    t = bsz * seq
    x2d = x.reshape(t, D_MODEL)
    n_main = w_in.shape[1] - N_GATE_COLS
    w_t = jnp.swapaxes(w_in, 0, 1)
    proj = _in_proj(x2d, w_t[:n_main].astype(BF16), b_in[None, :n_main])
    proj3 = proj.reshape(bsz, seq, n_main)
    wg = jnp.pad(w_t[n_main:], ((0, LANES - N_GATE_COLS), (0, 0)))
    bg = jnp.pad(b_in[None, n_main:], ((0, 0), (0, LANES - N_GATE_COLS)))
    gates = _gate_proj(x2d, wg, bg)[:, :N_GATE_COLS]
    g5 = gates.reshape(bsz, seq, 4, ML_HEADS)
    grow = g5.transpose(0, 3, 2, 1).reshape(bsz, ML_HEADS, 4, seq // CHUNK, CHUNK)
    gtr = grow.transpose(0, 1, 2, 4, 3)

    cmat, smat = _dft_tables(seq)
    zpad = _filter_features(seq)
    w1pad = jnp.pad(hy_filt_w1, ((0, LANES - HY_EMB), (0, 0)))
    deltas = jnp.abs(jnp.linspace(math.log(HY_DECAY_TARGET) / HY_SLOW_PCT,
                                  math.log(HY_DECAY_TARGET) / HY_FAST_PCT, D_HY, dtype=F32))[None]
    kr, ki, km = _hyena_filters(zpad, w1pad, hy_filt_b1[None], hy_filt_w2, hy_filt_b2[None],
                                hy_filt_freq, hy_filt_w3, deltas, cmat, smat)
    y_hy = _hyena(proj3, hy_conv_w, hy_conv_b[None], cmat, smat, kr, ki, km, hy_skip, hy_norm_w[None])
    y_ml = _mlstm(proj3, ml_conv_w, ml_conv_b[None], grow, gtr, ml_norm_w[None])
    return y_hy.reshape(t, D_HY), y_ml.reshape(t, D_ML), x2d


def _moe_tables(top_i, slot, counts):
    t = top_i.shape[0]
    n_tiles = N_EXPERTS + (t * TOP_K) // EXP_ROWS
    ntile = (counts + EXP_ROWS - 1) // EXP_ROWS
    ends = jnp.cumsum(ntile)
    starts = ends - ntile
    total = ends[-1]
    s_idx = jnp.arange(n_tiles, dtype=I32)
    valid = s_idx < total
    s_eff = jnp.where(valid, s_idx, jnp.maximum(total - 1, 0))
    tile_e = jnp.minimum(jnp.sum((s_eff[:, None] >= ends[None, :]).astype(I32), axis=1), N_EXPERTS - 1)
    local = s_eff - starts[tile_e]
    tile_rows = jnp.where(valid, jnp.clip(counts[tile_e] - local * EXP_ROWS, 0, EXP_ROWS), 0).astype(I32)
    tok = jnp.arange(t, dtype=I32)[:, None]
    row_tok = jnp.pad(jnp.sort((top_i * t + tok).reshape(-1)) % t, (0, EXP_ROWS))
    first = jnp.cumsum(counts) - counts
    tile_base = (first[tile_e] + local * EXP_ROWS).astype(I32)
    onehot = top_i[:, :, None] == jnp.arange(N_EXPERTS, dtype=I32)
    dest = jnp.sum(jnp.where(onehot, starts * EXP_ROWS, 0), axis=-1) + slot
    return tile_e.astype(I32), tile_rows, tile_base, row_tok.astype(I32), dest.astype(I32).reshape(-1)


def kernel(x, w_in, b_in, hy_conv_w, hy_conv_b, hy_filt_w1, hy_filt_b1, hy_filt_w2, hy_filt_b2, hy_filt_w3, hy_filt_freq, hy_skip, hy_norm_w, ml_conv_w, ml_conv_b, ml_norm_w, w_out, b_out, ln1_g, ln1_b, router_w, router_b, w_gu, b_gu, w_down, b_down, ln2_g, ln2_b):
    bsz, seq, _ = x.shape
    l = 0
    y_hy, y_ml, x2d = _mixer(x, w_in[l], b_in[l], hy_conv_w[l], hy_conv_b[l], hy_filt_w1[l], hy_filt_b1[l],
                             hy_filt_w2[l], hy_filt_b2[l], hy_filt_w3[l], hy_filt_freq[l], hy_skip[l],
                             hy_norm_w[l], ml_conv_w[l], ml_conv_b[l], ml_norm_w[l])
    x1, x1c = _out_proj_ln(y_hy, y_ml, x2d, w_out[l].astype(BF16), b_out[l][None], ln1_g[l][None],
                           ln1_b[l][None])
    rw = jnp.pad(router_w[l], ((0, 0), (0, LANES - N_EXPERTS)))
    rb = jnp.pad(router_b[l][None], ((0, 0), (0, LANES - N_EXPERTS)), constant_values=-1e30)
    top_i, top_g, slot, cnt = _router(x1, rw, rb)
    counts = cnt[0, :N_EXPERTS].astype(I32)
    tile_e, tile_rows, tile_base, row_tok, dest = _moe_tables(top_i[:, :TOP_K], slot[:, :TOP_K], counts)
    y_buf = _experts(tile_e, tile_rows, tile_base, row_tok, x1c, w_gu[l], b_gu[l][:, None, :], w_down[l],
                     b_down[l][:, None, :])
    out = _combine_ln(dest, y_buf, x1, top_g, ln2_g[l][None], ln2_b[l][None])
    return out.reshape(bsz, seq, D_MODEL)
```

```python
import functools
import math

import jax
import jax.numpy as jnp
from jax import lax
from jax.experimental import pallas as pl
from jax.experimental.pallas import tpu as pltpu

F32 = jnp.float32
BF16 = jnp.bfloat16
I32 = jnp.int32
HP = lax.Precision.HIGHEST

D_MODEL = 2048
D_HY = 1024
D_ML = 1024
ML_HEADS = 8
HEAD_DIM = 128
CHUNK = 128
N_GATE_COLS = 32
HY_EMB = 33
N_EXPERTS = 32
TOP_K = 4
D_FF = 2048
SWIGLU_LIMIT = 7.0
SWIGLU_ALPHA = 1.702
LN_EPS = 1e-5
DN_ALPHA = 2.0 ** 0.25
HY_DECAY_TARGET = 1e-2
HY_FAST_PCT = 0.3
HY_SLOW_PCT = 1.5

LANES = 128
ROW_CHUNKS = D_MODEL // LANES
VMEM_LIMIT = 56 * 1024 * 1024

PROJ_TM = 2048
PROJ_TN = 512
HY_CW = 256
HY_FC = 1024
OUT_TM = 512
RT_TM = 256
EXP_ROWS = 1152
EXP_M_SIZES = (1024, 1088, 1152)
EXP_CHUNK = 128
EXP_TF = 256
GATHER_UNROLL = 8
CMB_TM = 128
DFT_TB = 128


def _cparams(sem):
    return pltpu.CompilerParams(dimension_semantics=sem, vmem_limit_bytes=VMEM_LIMIT)


def _split(a):
    hi = a.astype(BF16)
    return hi, (a - hi.astype(F32)).astype(BF16)


def _dot3(a, b, dims):
    a_hi, a_lo = _split(a)
    b_hi, b_lo = _split(b)
    mm = functools.partial(lax.dot_general, dimension_numbers=dims, preferred_element_type=F32)
    return mm(a_hi, b_hi) + (mm(a_hi, b_lo) + mm(a_lo, b_hi))


NN = (((1,), (0,)), ((), ()))
NT = (((1,), (1,)), ((), ()))


def _const_spec(shape, index_map):
    return pl.BlockSpec(shape, index_map, pipeline_mode=pl.Buffered(1))


def _proj_kernel(x_ref, w_ref, b_ref, o_ref, xb_ref):
    @pl.when(pl.program_id(1) == 0)
    def _():
        xb_ref[...] = x_ref[...].astype(BF16)

    acc = lax.dot_general(xb_ref[...], w_ref[...], (((1,), (1,)), ((), ())), preferred_element_type=F32)
    o_ref[...] = (acc + b_ref[...]).astype(o_ref.dtype)


def _in_proj(x2d, wt_bf, b_row):
    m, k = x2d.shape
    n = wt_bf.shape[0]
    return pl.pallas_call(
        _proj_kernel,
        grid=(m // PROJ_TM, n // PROJ_TN),
        in_specs=[pl.BlockSpec((PROJ_TM, k), lambda i, j: (i, 0)),
                  pl.BlockSpec((PROJ_TN, k), lambda i, j: (j, 0)),
                  pl.BlockSpec((1, PROJ_TN), lambda i, j: (0, j))],
        out_specs=pl.BlockSpec((PROJ_TM, PROJ_TN), lambda i, j: (i, j)),
        out_shape=jax.ShapeDtypeStruct((m, n), BF16),
        scratch_shapes=[pltpu.VMEM((PROJ_TM, k), BF16)],
        compiler_params=_cparams(("arbitrary", "arbitrary")),
        name="in_proj")(x2d, wt_bf, b_row)


def _gate_kernel(x_ref, w_ref, b_ref, o_ref):
    o_ref[...] = _dot3(x_ref[...], w_ref[...], NT) + b_ref[...]


def _gate_proj(x2d, wt_pad, b_pad):
    m, k = x2d.shape
    tm = 512
    return pl.pallas_call(
        _gate_kernel,
        grid=(m // tm,),
        in_specs=[pl.BlockSpec((tm, k), lambda i: (i, 0)),
                  pl.BlockSpec((LANES, k), lambda i: (0, 0)),
                  pl.BlockSpec((1, LANES), lambda i: (0, 0))],
        out_specs=pl.BlockSpec((tm, LANES), lambda i: (i, 0)),
        out_shape=jax.ShapeDtypeStruct((m, LANES), F32),
        compiler_params=_cparams(("arbitrary",)),
        name="gate_proj")(x2d, wt_pad, b_pad)


def _filter_kernel(z_ref, w1_ref, b1_ref, w2_ref, b2_ref, fq_ref, w3f_ref, w3b_ref, dl_ref,
                   c_ref, s_ref, kr_ref, ki_ref, km_ref, h_ref):
    seq = z_ref.shape[0]
    inv_n = 1.0 / (2 * seq)
    z = z_ref[...]

    @pl.when(jnp.logical_and(pl.program_id(0) == 0, pl.program_id(1) == 0))
    def _():
        h1 = jnp.sin(fq_ref[0:1, :] * (jnp.dot(z, w1_ref[...], precision=HP, preferred_element_type=F32)
                                       + b1_ref[...]))
        h_ref[...] = jnp.sin(fq_ref[1:2, :] * (jnp.dot(h1, w2_ref[...], precision=HP, preferred_element_type=F32)
                                               + b2_ref[...]))

    half = seq // 2
    h = h_ref[...]
    win = jnp.exp(-z[:, 0:1] * dl_ref[...])
    fwd = _dot3(h, w3f_ref[...], NN) * win
    bwd = _dot3(h, w3b_ref[...], NN) * win
    row = lax.broadcasted_iota(I32, fwd.shape, 0)
    bwd = jnp.where(row == 0, 0.0, bwd)
    inv = 1.0 / jnp.sum(jnp.abs(fwd) + jnp.abs(bwd), axis=0, keepdims=True)
    ks = ((fwd + bwd) * inv)
    kd = ((fwd - bwd) * inv)
    ksb = ks.astype(BF16)
    kdb = kd.astype(BF16)
    ec = jnp.dot(c_ref[0], ksb[:half], preferred_element_type=F32)
    oc = jnp.dot(c_ref[1], ksb[half:], preferred_element_type=F32)
    es = jnp.dot(s_ref[0], kdb[:half], preferred_element_type=F32)
    os_ = jnp.dot(s_ref[1], kdb[half:], preferred_element_type=F32)
    rowh = lax.broadcasted_iota(I32, ec.shape, 0)
    wf = jnp.where(rowh == 0, inv_n, 2.0 * inv_n)
    kr_ref[0, :half, :] = (ec + oc) * wf
    kr_ref[0, half:, :] = (ec - oc) * wf
    ki_ref[0, :half, :] = -(es + os_) * wf
    ki_ref[0, half:, :] = (es - os_) * wf
    alt = jnp.where((rowh & 1) == 0, 1.0, -1.0)
    km_ref[0, 0:1, :] = jnp.sum(ks[:half] * alt, axis=0, keepdims=True) * (2.0 * inv_n)
    km_ref[0, 1:2, :] = -jnp.sum(kd[half:] * alt, axis=0, keepdims=True) * (2.0 * inv_n)


def _hyena_filters(zpad, w1pad, b1, w2, b2, freq, w3, deltas, cmat, smat):
    seq = zpad.shape[0]
    nb = D_HY // HY_CW
    hid = w2.shape[0]
    full = lambda shape: pl.BlockSpec(shape, lambda o, c: (0,) * len(shape))
    out_sds = jax.ShapeDtypeStruct((2, seq, D_HY), F32)
    return pl.pallas_call(
        _filter_kernel,
        grid=(2, nb),
        in_specs=[full(zpad.shape), full(w1pad.shape), full(b1.shape), full(w2.shape), full(b2.shape),
                  full(freq.shape),
                  pl.BlockSpec((hid, HY_CW), lambda o, c: (0, o * 2 * nb + c)),
                  pl.BlockSpec((hid, HY_CW), lambda o, c: (0, o * 2 * nb + nb + c)),
                  pl.BlockSpec((1, HY_CW), lambda o, c: (0, c)),
                  _const_spec(cmat.shape, lambda o, c: (0, 0, 0)),
                  _const_spec(smat.shape, lambda o, c: (0, 0, 0))],
        out_specs=[pl.BlockSpec((1, seq, HY_CW), lambda o, c: (o, 0, c)),
                   pl.BlockSpec((1, seq, HY_CW), lambda o, c: (o, 0, c)),
                   pl.BlockSpec((1, 2, HY_CW), lambda o, c: (o, 0, c))],
        out_shape=[out_sds, out_sds, jax.ShapeDtypeStruct((2, 2, D_HY), F32)],
        scratch_shapes=[pltpu.VMEM((seq, hid), F32)],
        compiler_params=_cparams(("arbitrary", "arbitrary")),
        name="hyena_filters")(zpad, w1pad, b1, w2, b2, freq, w3, w3, deltas, cmat, smat)


def _short_conv(u, w_ref, b_ref, row, seq):
    prev = jnp.where(row == 0, 0.0, pltpu.roll(u, 1, 0))
    nxt = jnp.where(row == seq - 1, 0.0, pltpu.roll(u, seq - 1, 0))
    return w_ref[0:1, :] * prev + w_ref[1:2, :] * u + w_ref[2:3, :] * nxt + b_ref[...]


def _hyena_kernel(uv_ref, u1_ref, u2_ref, wv_ref, w1_ref, w2_ref, bv_ref, b1_ref, b2_ref,
                  c_ref, s_ref, kr_ref, ki_ref, km_ref, skip_ref, nw_ref, o_ref,
                  a_ref, b_ref, t_ref, zb_ref, s1_ref, s2_ref, s3_ref, s4_ref, md_ref):
    seq = uv_ref.shape[1]
    cw = uv_ref.shape[2]
    half = seq // 2
    nblk = half // HY_FC
    row = lax.broadcasted_iota(I32, (seq, LANES), 0)
    alt_half = jnp.where((lax.broadcasted_iota(I32, (half, cw), 0) & 1) == 0, 1.0, -1.0)
    alt_blk = jnp.where((lax.broadcasted_iota(I32, (HY_FC, cw), 0) & 1) == 0, 1.0, -1.0)
    groups = [slice(g * LANES, (g + 1) * LANES) for g in range(cw // LANES)]

    def conv_to(dst_ref, u_ref, w_ref, bias_ref):
        for gi, gs in enumerate(groups):
            t_ref[gi] = _short_conv(u_ref[0, :, gs].astype(F32), w_ref.at[:, gs], bias_ref.at[:, gs], row, seq)
            dst_ref[0:half, gs] = t_ref[gi, pl.ds(0, half, stride=2), :]
            dst_ref[half:seq, gs] = t_ref[gi, pl.ds(1, half, stride=2), :]

    def spectrum(zin_ref, o):
        z = zin_ref[...]
        zb_ref[...] = z.astype(BF16)
        am = jnp.sum(z[:half] * alt_half, axis=0, keepdims=True)
        bm = jnp.sum(z[half:] * alt_half, axis=0, keepdims=True)
        krm, kim = km_ref[o, 0:1, :], km_ref[o, 1:2, :]
        md_ref[0:1, :] = am * krm + bm * kim
        md_ref[1:2, :] = am * kim - bm * krm
        ze = zb_ref[0:half, :]
        zo = zb_ref[half:seq, :]
        for gb in range(nblk):
            lo = slice(gb * HY_FC, (gb + 1) * HY_FC)
            hi = slice(half + gb * HY_FC, half + (gb + 1) * HY_FC)
            ec = jnp.dot(c_ref[0, lo, :], ze, preferred_element_type=F32)
            oc = jnp.dot(c_ref[1, lo, :], zo, preferred_element_type=F32)
            es = jnp.dot(s_ref[0, lo, :], ze, preferred_element_type=F32)
            os_ = jnp.dot(s_ref[1, lo, :], zo, preferred_element_type=F32)
            a_lo, b_lo = ec + oc, es + os_
            a_hi, b_hi = ec - oc, os_ - es
            krl, kil = kr_ref[o, lo, :], ki_ref[o, lo, :]
            krh, kih = kr_ref[o, hi, :], ki_ref[o, hi, :]
            pr = a_lo * krl + b_lo * kil
            pi = a_lo * kil - b_lo * krl
            qr = a_hi * krh + b_hi * kih
            qi = a_hi * kih - b_hi * krh
            s1_ref[lo, :] = (pr + qr).astype(BF16)
            s2_ref[lo, :] = (pi - qi).astype(BF16)
            s3_ref[lo, :] = (pr - qr).astype(BF16)
            s4_ref[lo, :] = (pi + qi).astype(BF16)

    def conv_rows(ub, parity):
        us = slice(ub * HY_FC, (ub + 1) * HY_FC)
        if parity == 0:
            y = jnp.dot(c_ref[0, us, :], s1_ref[...], preferred_element_type=F32)
            y = y - jnp.dot(s_ref[0, us, :], s2_ref[...], preferred_element_type=F32)
            y = y + md_ref[0:1, :] * alt_blk
        else:
            y = jnp.dot(c_ref[2, us, :], s3_ref[...], preferred_element_type=F32)
            y = y - jnp.dot(s_ref[2, us, :], s4_ref[...], preferred_element_type=F32)
            y = y - md_ref[1:2, :] * alt_blk
        return slice(parity * half + ub * HY_FC, parity * half + (ub + 1) * HY_FC), y

    blocks = [(ub, parity) for parity in range(2) for ub in range(nblk)]
    conv_to(a_ref, uv_ref, wv_ref, bv_ref)
    conv_to(b_ref, u1_ref, w1_ref, b1_ref)
    spectrum(a_ref, 0)
    for ub, parity in blocks:
        rows, y = conv_rows(ub, parity)
        b_ref[rows, :] = b_ref[rows, :] * (y + skip_ref[0:1, :] * a_ref[rows, :])
    conv_to(a_ref, u2_ref, w2_ref, b2_ref)
    spectrum(b_ref, 1)
    for ub, parity in blocks:
        rows, y = conv_rows(ub, parity)
        z = a_ref[rows, :] * (y + skip_ref[1:2, :] * b_ref[rows, :])
        for gi, gs in enumerate(groups):
            zg = z[:, gs]
            mu = jnp.mean(zg, axis=-1, keepdims=True)
            zc = zg - mu
            var = jnp.mean(zc * zc, axis=-1, keepdims=True)
            t_ref[gi, pl.ds(2 * ub * HY_FC + parity, HY_FC, stride=2), :] = (
                zc * lax.rsqrt(var + LN_EPS) * nw_ref[:, gs])
    for gi, gs in enumerate(groups):
        o_ref[0, :, gs] = t_ref[gi].astype(o_ref.dtype)


def _hyena(proj3, conv_w, conv_b, cmat, smat, kr, ki, km, skip, norm_w):
    bsz, seq, _ = proj3.shape
    half = seq // 2
    nb = D_HY // HY_CW
    u_spec = lambda off: pl.BlockSpec((1, seq, HY_CW), lambda c, b: (b, 0, off + c))
    w_spec = lambda off: pl.BlockSpec((3, HY_CW), lambda c, b: (0, off + c))
    b_spec = lambda off: pl.BlockSpec((1, HY_CW), lambda c, b: (0, off + c))
    return pl.pallas_call(
        _hyena_kernel,
        grid=(nb, bsz),
        in_specs=[u_spec(0), u_spec(nb), u_spec(2 * nb),
                  w_spec(0), w_spec(nb), w_spec(2 * nb),
                  b_spec(0), b_spec(nb), b_spec(2 * nb),
                  _const_spec(cmat.shape, lambda c, b: (0, 0, 0)),
                  _const_spec(smat.shape, lambda c, b: (0, 0, 0)),
                  _const_spec((2, seq, HY_CW), lambda c, b: (0, 0, c)),
                  _const_spec((2, seq, HY_CW), lambda c, b: (0, 0, c)),
                  pl.BlockSpec((2, 2, HY_CW), lambda c, b: (0, 0, c)),
                  pl.BlockSpec((2, HY_CW), lambda c, b: (0, c)),
                  pl.BlockSpec((1, HY_CW), lambda c, b: (0, c))],
        out_specs=pl.BlockSpec((1, seq, HY_CW), lambda c, b: (b, 0, c)),
        out_shape=jax.ShapeDtypeStruct((bsz, seq, D_HY), BF16),
        scratch_shapes=[pltpu.VMEM((seq, HY_CW), F32), pltpu.VMEM((seq, HY_CW), F32),
                        pltpu.VMEM((HY_CW // LANES, seq, LANES), F32), pltpu.VMEM((seq, HY_CW), BF16),
                        pltpu.VMEM((half, HY_CW), BF16), pltpu.VMEM((half, HY_CW), BF16),
                        pltpu.VMEM((half, HY_CW), BF16), pltpu.VMEM((half, HY_CW), BF16),
                        pltpu.VMEM((2, HY_CW), F32)],
        compiler_params=_cparams(("arbitrary", "arbitrary")),
        name="hyena")(proj3, proj3, proj3, conv_w, conv_w, conv_w, conv_b, conv_b, conv_b,
                      cmat, smat, kr, ki, km, skip, norm_w)


def _mlstm_kernel(qp_ref, kp_ref, v_ref, og_ref, wq_ref, wk_ref, bq_ref, bk_ref, gr_ref, gt_ref,
                  nw_ref, o_ref, qb_ref, kb_ref, hacc_ref, cst_ref, nst_ref):
    seq = qp_ref.shape[1]
    d = qp_ref.shape[2]
    nchunk = seq // CHUNK
    row = lax.broadcasted_iota(I32, (seq, d), 0)
    q = _short_conv(qp_ref[0].astype(F32), wq_ref, bq_ref, row, seq)
    k = _short_conv(kp_ref[0].astype(F32), wk_ref, bk_ref, row, seq)
    qb_ref[...] = (q * jax.nn.sigmoid(q)).astype(BF16)
    kb_ref[...] = ((k * jax.nn.sigmoid(k)) * (d ** -0.5)).astype(BF16)

    ti = lax.broadcasted_iota(I32, (CHUNK, CHUNK), 0)
    si = lax.broadcasted_iota(I32, (CHUNK, CHUNK), 1)
    lower = ti >= si
    upper = ti <= si
    lower_f = lower.astype(F32)
    upper_f = upper.astype(F32)
    nt = (((1,), (1,)), ((), ()))
    tn = (((0,), (0,)), ((), ()))
    chunk_rows = [slice(c * CHUNK, (c + 1) * CHUNK) for c in range(nchunk)]

    for direction in range(2):
        f_idx, i_idx = 2 * direction + 1, 2 * direction
        mask = lower if direction == 0 else upper
        order = list(range(nchunk)) if direction == 0 else list(range(nchunk - 1, -1, -1))
        lf_r = jax.nn.log_sigmoid(gr_ref[0, 0, f_idx])
        b_r = jnp.dot(lf_r, upper_f if direction == 0 else lower_f, precision=HP, preferred_element_type=F32)
        rterm = b_r - gr_ref[0, 0, i_idx]
        b_last = jnp.sum(lf_r, axis=-1, keepdims=True)
        lf_c = jax.nn.log_sigmoid(gt_ref[0, 0, f_idx])
        b_c = jnp.dot(lower_f if direction == 0 else upper_f, lf_c, precision=HP, preferred_element_type=F32)
        i_c = gt_ref[0, 0, i_idx]

        bcol, gcol, gmax, rowmax, blast = [], [], [], [], []
        for c in range(nchunk):
            bc = jnp.broadcast_to(b_c[:, c:c + 1], (CHUNK, CHUNK))
            ic = jnp.broadcast_to(i_c[:, c:c + 1], (CHUNK, CHUNK))
            bl = jnp.broadcast_to(b_last[c:c + 1, :], (1, CHUNK))
            dmat = jnp.where(mask, bc - rterm[c:c + 1, :], -jnp.inf)
            g = bl - bc + ic
            bcol.append(bc)
            gcol.append(g)
            blast.append(bl)
            rowmax.append(jnp.max(dmat, axis=-1, keepdims=True))
            gmax.append(jnp.max(g, axis=0, keepdims=True))

        m = jnp.zeros((1, CHUNK), F32)
        m_in, m_out = [None] * nchunk, [None] * nchunk
        for c in order:
            m_in[c] = m
            m = jnp.maximum(blast[c] + m, gmax[c])
            m_out[c] = m

        cmat = jnp.zeros((d, d), F32)
        nvec = jnp.zeros((1, d), F32)
        for c in order:
            kc = kb_ref[chunk_rows[c], :]
            vc = v_ref[0, chunk_rows[c], :]
            cst_ref[c] = cmat.astype(BF16)
            nst_ref[c] = nvec
            wg = jnp.exp(gcol[c] - m_out[c])
            decay = jnp.exp(blast[c] + m_in[c] - m_out[c])
            upd = lax.dot_general((wg * vc.astype(F32)).astype(BF16), kc, tn, preferred_element_type=F32)
            cmat = decay * cmat + upd
            nvec = decay * nvec + jnp.sum(wg * kc.astype(F32), axis=0, keepdims=True)

        for c in range(nchunk):
            rs = chunk_rows[c]
            qc = qb_ref[rs, :]
            kc = kb_ref[rs, :]
            vc = v_ref[0, rs, :]
            dmat = jnp.where(mask, bcol[c] - rterm[c:c + 1, :], -jnp.inf)
            inter = bcol[c] + m_in[c]
            m_t = jnp.maximum(inter, rowmax[c])
            p = jnp.exp(dmat - m_t)
            inter_w = jnp.exp(inter - m_t)
            s = lax.dot_general(qc, kc, nt, preferred_element_type=F32) * p
            cq = lax.dot_general(qc, cst_ref[c], nt, preferred_element_type=F32)
            num = jnp.dot(s.astype(BF16), vc, preferred_element_type=F32) + inter_w * cq
            nq = jnp.sum(qc.astype(F32) * nst_ref[c], axis=-1, keepdims=True)
            den = jnp.sum(s, axis=-1, keepdims=True) + inter_w * nq
            h = num / jnp.maximum(jnp.abs(den), jnp.exp(-m_t))
            if direction == 0:
                hacc_ref[rs, :] = h
            else:
                hacc_ref[rs, :] += h

    h = hacc_ref[...]
    mu = jnp.mean(h, axis=-1, keepdims=True)
    hc = h - mu
    var = jnp.mean(hc * hc, axis=-1, keepdims=True)
    y = hc * lax.rsqrt(var + LN_EPS) * nw_ref[...] * jax.nn.sigmoid(og_ref[0].astype(F32))
    o_ref[0] = y.astype(o_ref.dtype)


def _mlstm(proj3, conv_w, conv_b, grow, gtr, norm_w):
    bsz, seq, _ = proj3.shape
    d = HEAD_DIM
    nchunk = seq // CHUNK
    hy_blocks = 3 * D_HY // d
    qoff, koff, voff, ooff = hy_blocks, hy_blocks + ML_HEADS, hy_blocks + 2 * ML_HEADS, hy_blocks + 3 * ML_HEADS
    p_spec = lambda off: pl.BlockSpec((1, seq, d), lambda b, h: (b, 0, off + h))
    return pl.pallas_call(
        _mlstm_kernel,
        grid=(bsz, ML_HEADS),
        in_specs=[p_spec(qoff), p_spec(koff), p_spec(voff), p_spec(ooff),
                  pl.BlockSpec((3, d), lambda b, h: (0, h)),
                  pl.BlockSpec((3, d), lambda b, h: (0, ML_HEADS + h)),
                  pl.BlockSpec((1, d), lambda b, h: (0, h)),
                  pl.BlockSpec((1, d), lambda b, h: (0, ML_HEADS + h)),
                  pl.BlockSpec((1, 1, 4, nchunk, CHUNK), lambda b, h: (b, h, 0, 0, 0)),
                  pl.BlockSpec((1, 1, 4, CHUNK, nchunk), lambda b, h: (b, h, 0, 0, 0)),
                  pl.BlockSpec((1, d), lambda b, h: (0, h))],
        out_specs=pl.BlockSpec((1, seq, d), lambda b, h: (b, 0, h)),
        out_shape=jax.ShapeDtypeStruct((bsz, seq, D_ML), BF16),
        scratch_shapes=[pltpu.VMEM((seq, d), BF16), pltpu.VMEM((seq, d), BF16),
                        pltpu.VMEM((seq, d), F32),
                        pltpu.VMEM((nchunk, d, d), BF16), pltpu.VMEM((nchunk, 1, d), F32)],
        compiler_params=_cparams(("arbitrary", "arbitrary")),
        name="mlstm")(proj3, proj3, proj3, proj3, conv_w, conv_w, conv_b, conv_b, grow, gtr, norm_w)


def _to_slabs(y):
    n = y.shape[0]
    parts = jnp.stack([y[:, c * LANES:(c + 1) * LANES] for c in range(ROW_CHUNKS)], axis=0)
    return pltpu.einshape("crl->rcl", parts).reshape(n * ROW_CHUNKS, LANES)


def _from_slabs(v):
    n = v.shape[0] // ROW_CHUNKS
    parts = pltpu.einshape("rcl->crl", v.reshape(n, ROW_CHUNKS, LANES))
    return jnp.concatenate([parts[c] for c in range(ROW_CHUNKS)], axis=-1)


def _layer_norm(u, g, b):
    mu = jnp.mean(u, axis=-1, keepdims=True)
    uc = u - mu
    var = jnp.mean(uc * uc, axis=-1, keepdims=True)
    return uc * lax.rsqrt(var + LN_EPS) * g + b


def _outproj_kernel(yh_ref, ym_ref, x_ref, wa_ref, wb_ref, b_ref, g_ref, be_ref, o_ref, oc_ref):
    mix = (jnp.dot(yh_ref[...], wa_ref[...], preferred_element_type=F32)
           + jnp.dot(ym_ref[...], wb_ref[...], preferred_element_type=F32) + b_ref[...])
    y = _layer_norm(DN_ALPHA * x_ref[...] + mix, g_ref[...], be_ref[...])
    o_ref[...] = y
    oc_ref[...] = _to_slabs(y)


def _out_proj_ln(y_hy, y_ml, x2d, w_out_bf, b_out, g, be):
    t = x2d.shape[0]
    tm = OUT_TM
    vec = lambda: pl.BlockSpec((1, D_MODEL), lambda i: (0, 0))
    return pl.pallas_call(
        _outproj_kernel,
        grid=(t // tm,),
        in_specs=[pl.BlockSpec((tm, D_HY), lambda i: (i, 0)),
                  pl.BlockSpec((tm, D_ML), lambda i: (i, 0)),
                  pl.BlockSpec((tm, D_MODEL), lambda i: (i, 0)),
                  _const_spec((D_HY, D_MODEL), lambda i: (0, 0)),
                  _const_spec((D_ML, D_MODEL), lambda i: (1, 0)),
                  vec(), vec(), vec()],
        out_specs=[pl.BlockSpec((tm, D_MODEL), lambda i: (i, 0)),
                   pl.BlockSpec((tm * ROW_CHUNKS, LANES), lambda i: (i, 0))],
        out_shape=[jax.ShapeDtypeStruct((t, D_MODEL), F32),
                   jax.ShapeDtypeStruct((t * ROW_CHUNKS, LANES), F32)],
        compiler_params=_cparams(("arbitrary",)),
        name="out_proj_ln1")(y_hy, y_ml, x2d, w_out_bf, w_out_bf, b_out, g, be)


def _router_kernel(x_ref, w_ref, b_ref, ti_ref, tg_ref, tp_ref, cnt_ref):
    tm = ti_ref.shape[0]

    @pl.when(pl.program_id(0) == 0)
    def _():
        cnt_ref[...] = jnp.zeros_like(cnt_ref)

    logits = _dot3(x_ref[...], w_ref[...], NN) + b_ref[...]
    lane = lax.broadcasted_iota(I32, (tm, LANES), 1)
    work = logits
    vals, idxs = [], []
    chosen = jnp.zeros((tm, LANES), F32)
    for _ in range(TOP_K):
        mx = jnp.max(work, axis=-1, keepdims=True)
        idx = jnp.min(jnp.where(work == mx, lane, LANES), axis=-1, keepdims=True)
        hit = lane == idx
        vals.append(mx)
        idxs.append(idx)
        chosen = jnp.where(hit, 1.0, chosen)
        work = jnp.where(hit, -jnp.inf, work)
    exps = [jnp.exp(v - vals[0]) for v in vals]
    den = exps[0] + exps[1] + exps[2] + exps[3]
    ri = lax.broadcasted_iota(I32, (tm, tm), 0)
    ci = lax.broadcasted_iota(I32, (tm, tm), 1)
    strict_lower = (ri > ci).astype(BF16)
    carry = cnt_ref[...]
    slot = carry + jnp.dot(strict_lower, chosen.astype(BF16), preferred_element_type=F32)
    ti = jnp.zeros((tm, LANES), I32)
    tg = jnp.zeros((tm, LANES), F32)
    tp = jnp.zeros((tm, LANES), F32)
    for k in range(TOP_K):
        sk = jnp.sum(jnp.where(lane == idxs[k], slot, 0.0), axis=-1, keepdims=True)
        ti = jnp.where(lane == k, idxs[k], ti)
        tg = jnp.where(lane == k, exps[k] / den, tg)
        tp = jnp.where(lane == k, sk, tp)
    ti_ref[...] = ti
    tg_ref[...] = tg
    tp_ref[...] = tp.astype(I32)
    cnt_ref[...] = carry + jnp.sum(chosen, axis=0, keepdims=True)


def _router(x1, w_pad, b_pad):
    t = x1.shape[0]
    tm = RT_TM
    o_spec = lambda: pl.BlockSpec((tm, LANES), lambda i: (i, 0))
    return pl.pallas_call(
        _router_kernel,
        grid=(t // tm,),
        in_specs=[pl.BlockSpec((tm, D_MODEL), lambda i: (i, 0)),
                  pl.BlockSpec((D_MODEL, LANES), lambda i: (0, 0)),
                  pl.BlockSpec((1, LANES), lambda i: (0, 0))],
        out_specs=[o_spec(), o_spec(), o_spec(), pl.BlockSpec((1, LANES), lambda i: (0, 0))],
        out_shape=[jax.ShapeDtypeStruct((t, LANES), I32), jax.ShapeDtypeStruct((t, LANES), F32),
                   jax.ShapeDtypeStruct((t, LANES), I32), jax.ShapeDtypeStruct((1, LANES), F32)],
        compiler_params=_cparams(("arbitrary",)),
        name="router")(x1, w_pad, b_pad)


def _expert_kernel(te_ref, tr_ref, tb_ref, rt_ref, x_hbm, wg_ref, wu_ref, wd_ref, bg_ref, bu_ref, bd_ref, y_hbm,
                   stage_ref, xb_ref, acc_ref, ring_ref, wgb_ref, wub_ref, wdb_ref, gsem, osem):
    g = pl.program_id(0)
    n_items = pl.num_programs(0) - 1
    nf = D_FF // EXP_TF
    n_tiles = n_items // nf
    item = jnp.maximum(g - 1, 0)
    s = item // nf
    j = item % nf
    rows = tr_ref[s]
    active = jnp.logical_and(g >= 1, rows > 0)
    cast_slot = g % 2
    use_slot = (g + 1) % 2
    slab = EXP_CHUNK * ROW_CHUNKS
    per_step = EXP_ROWS // nf

    def row_copy(tok, r):
        return pltpu.make_async_copy(
            x_hbm.at[pl.ds(pl.multiple_of(tok * ROW_CHUNKS, ROW_CHUNKS), ROW_CHUNKS), :],
            stage_ref.at[pl.ds(pl.multiple_of(r * ROW_CHUNKS, ROW_CHUNKS), ROW_CHUNKS), :], gsem)

    def wait_gather():
        pltpu.make_async_copy(x_hbm.at[pl.ds(0, EXP_ROWS * ROW_CHUNKS), :], stage_ref, gsem).wait()

    def cast_weights():
        wgb_ref[cast_slot] = wg_ref[0].astype(BF16)
        wub_ref[cast_slot] = wu_ref[0].astype(BF16)
        wdb_ref[cast_slot] = wd_ref[0].astype(BF16)

    @pl.when(g == 0)
    def _():
        stage_ref[...] = jnp.zeros_like(stage_ref)
        acc_ref[...] = jnp.zeros_like(acc_ref)
        base = tb_ref[0]

        def group(q, carry):
            for u in range(GATHER_UNROLL):
                r = q * GATHER_UNROLL + u
                row_copy(rt_ref[base + r], r).start()
            return carry

        lax.fori_loop(0, EXP_ROWS // GATHER_UNROLL, group, 0)

        cast_weights()

    @pl.when(active)
    def _():
        @pl.when(j == 0)
        def _():
            wait_gather()
            for i in range(EXP_ROWS // EXP_CHUNK):
                xb_ref[i * EXP_CHUNK:(i + 1) * EXP_CHUNK, :] = _from_slabs(
                    stage_ref[i * slab:(i + 1) * slab, :]).astype(BF16)

        def step_body(m):
            cast_weights()
            nxt_base = tb_ref[jnp.minimum(s + 1, n_tiles - 1)]
            for u in range(per_step):
                r = j * per_step + u
                row_copy(rt_ref[nxt_base + r], r).start()

            xb = xb_ref[0:m, :]
            gate = jnp.dot(xb, wgb_ref[use_slot], preferred_element_type=F32) + bg_ref[0]
            up = jnp.dot(xb, wub_ref[use_slot], preferred_element_type=F32) + bu_ref[0]
            gate = jnp.minimum(gate, SWIGLU_LIMIT)
            up = jnp.clip(up, -SWIGLU_LIMIT, SWIGLU_LIMIT)
            act = (up + 1.0) * (gate * jax.nn.sigmoid(SWIGLU_ALPHA * gate))
            part = jnp.dot(act.astype(BF16), wdb_ref[use_slot], preferred_element_type=F32)
            acc_ref[0:m, :] = jnp.where(j == 0, jnp.broadcast_to(bd_ref[0], part.shape), acc_ref[0:m, :]) + part

        lo = 0
        for m in EXP_M_SIZES:
            @pl.when(jnp.logical_and(rows > lo, rows <= m))
            def _(m=m):
                step_body(m)
            lo = m

        @pl.when(j == nf - 1)
        def _():
            nchunk = (rows + EXP_CHUNK - 1) // EXP_CHUNK

            def chunk_copy(i, slot):
                dst0 = pl.multiple_of((s * EXP_ROWS + i * EXP_CHUNK) * ROW_CHUNKS, slab)
                return pltpu.make_async_copy(ring_ref.at[slot], y_hbm.at[pl.ds(dst0, slab), :], osem.at[slot])

            def emit(i, carry):
                slot = i % 2

                @pl.when(i >= 2)
                def _():
                    chunk_copy(i - 2, slot).wait()

                r0 = pl.multiple_of(i * EXP_CHUNK, EXP_CHUNK)
                ring_ref[slot] = _to_slabs(acc_ref[pl.ds(r0, EXP_CHUNK), :])
                chunk_copy(i, slot).start()
                return carry

            lax.fori_loop(0, nchunk, emit, 0)
            for back in range(2):
                @pl.when(nchunk > back)
                def _():
                    last = nchunk - 1 - back
                    chunk_copy(last, last % 2).wait()

    @pl.when(g == n_items)
    def _():
        wait_gather()


def _experts(tile_e, tile_rows, tile_base, row_tok, x1c, w_gu, b_gu, w_down, b_down):
    n_tiles = tile_e.shape[0]
    nf = D_FF // EXP_TF
    n_items = n_tiles * nf

    def item_block(item, te, tr):
        s = item // nf
        return te[s], jnp.where(tr[s] > 0, item % nf, nf - 1)

    def cast_item(g, te, tr):
        return item_block(jnp.minimum(g, n_items - 1), te, tr)

    def use_item(g, te, tr):
        return item_block(jnp.maximum(g - 1, 0), te, tr)

    def w_gate(g, te, tr, tb, rt):
        e, j = cast_item(g, te, tr)
        return e, 0, j

    def w_up(g, te, tr, tb, rt):
        e, j = cast_item(g, te, tr)
        return e, 0, nf + j

    def w_down_map(g, te, tr, tb, rt):
        e, j = cast_item(g, te, tr)
        return e, j, 0

    def b_gate(g, te, tr, tb, rt):
        e, j = use_item(g, te, tr)
        return e, 0, j

    def b_up(g, te, tr, tb, rt):
        e, j = use_item(g, te, tr)
        return e, 0, nf + j

    def b_down_map(g, te, tr, tb, rt):
        e, _ = use_item(g, te, tr)
        return e, 0, 0

    grid_spec = pltpu.PrefetchScalarGridSpec(
        num_scalar_prefetch=4,
        grid=(n_items + 1,),
        in_specs=[pl.BlockSpec(memory_space=pl.ANY),
                  pl.BlockSpec((1, D_MODEL, EXP_TF), w_gate),
                  pl.BlockSpec((1, D_MODEL, EXP_TF), w_up),
                  pl.BlockSpec((1, EXP_TF, D_MODEL), w_down_map),
                  pl.BlockSpec((1, 1, EXP_TF), b_gate),
                  pl.BlockSpec((1, 1, EXP_TF), b_up),
                  pl.BlockSpec((1, 1, D_MODEL), b_down_map)],
        out_specs=pl.BlockSpec(memory_space=pl.ANY),
        scratch_shapes=[pltpu.VMEM((EXP_ROWS * ROW_CHUNKS, LANES), F32),
                        pltpu.VMEM((EXP_ROWS, D_MODEL), BF16),
                        pltpu.VMEM((EXP_ROWS, D_MODEL), F32),
                        pltpu.VMEM((2, EXP_CHUNK * ROW_CHUNKS, LANES), F32),
                        pltpu.VMEM((2, D_MODEL, EXP_TF), BF16),
                        pltpu.VMEM((2, D_MODEL, EXP_TF), BF16),
                        pltpu.VMEM((2, EXP_TF, D_MODEL), BF16),
                        pltpu.SemaphoreType.DMA(()),
                        pltpu.SemaphoreType.DMA((2,))])
    return pl.pallas_call(
        _expert_kernel,
        grid_spec=grid_spec,
        out_shape=jax.ShapeDtypeStruct((n_tiles * EXP_ROWS * ROW_CHUNKS, LANES), F32),
        compiler_params=_cparams(("arbitrary",)),
        name="experts")(tile_e, tile_rows, tile_base, row_tok, x1c, w_gu, w_gu, w_down, b_gu, b_gu, b_down)


def _combine_kernel(dest_ref, y_hbm, x_ref, tg_ref, g_ref, be_ref, o_ref, buf_a, buf_b, sem_a, sem_b):
    tm = o_ref.shape[0] // 2
    i = pl.program_id(0)
    n = pl.num_programs(0)
    last_tile = 2 * n - 1

    def start_tile(tile, buf, sem):
        base = tile * tm * TOP_K
        for t in range(tm):
            for k in range(TOP_K):
                src_row = dest_ref[base + t * TOP_K + k]
                pltpu.make_async_copy(
                    y_hbm.at[pl.ds(pl.multiple_of(src_row * ROW_CHUNKS, ROW_CHUNKS), ROW_CHUNKS), :],
                    buf.at[k, t * ROW_CHUNKS:(t + 1) * ROW_CHUNKS, :], sem).start()

    def wait_tile(buf, sem):
        for k in range(TOP_K):
            pltpu.make_async_copy(y_hbm.at[pl.ds(0, tm * ROW_CHUNKS), :], buf.at[k], sem).wait()

    def combine(buf, rows):
        tg = tg_ref[rows, :]
        ff = jnp.zeros((tm, D_MODEL), F32)
        for k in range(TOP_K):
            ff = ff + tg[:, k:k + 1] * _from_slabs(buf[k])
        o_ref[rows, :] = _layer_norm(DN_ALPHA * x_ref[rows, :] + ff, g_ref[...], be_ref[...])

    @pl.when(i == 0)
    def _():
        start_tile(0, buf_a, sem_a)

    wait_tile(buf_a, sem_a)
    start_tile(2 * i + 1, buf_b, sem_b)
    combine(buf_a, slice(0, tm))
    wait_tile(buf_b, sem_b)
    start_tile(jnp.minimum(2 * i + 2, last_tile), buf_a, sem_a)
    combine(buf_b, slice(tm, 2 * tm))

    @pl.when(i == n - 1)
    def _():
        wait_tile(buf_a, sem_a)


def _combine_ln(dest_flat, y_buf, x1, tg, g, be):
    t = tg.shape[0]
    tm = 2 * CMB_TM
    buf = pltpu.VMEM((TOP_K, CMB_TM * ROW_CHUNKS, LANES), F32)
    grid_spec = pltpu.PrefetchScalarGridSpec(
        num_scalar_prefetch=1,
        grid=(t // tm,),
        in_specs=[pl.BlockSpec(memory_space=pl.ANY),
                  pl.BlockSpec((tm, D_MODEL), lambda i, d: (i, 0)),
                  pl.BlockSpec((tm, LANES), lambda i, d: (i, 0)),
                  pl.BlockSpec((1, D_MODEL), lambda i, d: (0, 0)),
                  pl.BlockSpec((1, D_MODEL), lambda i, d: (0, 0))],
        out_specs=pl.BlockSpec((tm, D_MODEL), lambda i, d: (i, 0)),
        scratch_shapes=[buf, buf, pltpu.SemaphoreType.DMA(()), pltpu.SemaphoreType.DMA(())])
    return pl.pallas_call(
        _combine_kernel,
        grid_spec=grid_spec,
        out_shape=jax.ShapeDtypeStruct((t, D_MODEL), F32),
        compiler_params=_cparams(("arbitrary",)),
        name="combine_ln2")(dest_flat, y_buf, x1, tg, g, be)


def _dft_kernel(cd_ref, sd_ref, ca_ref, sa_ref, c_ref, s_ref):
    cd, sd = cd_ref[0], sd_ref[0]
    ca, sa = ca_ref[0], sa_ref[0]
    c_ref[0] = (cd * ca - sd * sa).astype(c_ref.dtype)
    s_ref[0] = (sd * ca + cd * sa).astype(s_ref.dtype)


def _dft_tables(seq):
    n = 2 * seq
    half = seq // 2
    nblk = half // DFT_TB
    idx = jnp.arange(half, dtype=I32)
    off = jnp.arange(DFT_TB, dtype=I32)
    start = jnp.arange(nblk, dtype=I32) * DFT_TB

    def angle(prod):
        return (prod % n).astype(F32) * (2.0 * math.pi / n)

    ang_d = jnp.stack([angle(off[:, None] * (2 * idx)[None, :]),
                       angle(off[:, None] * (2 * idx + 1)[None, :]),
                       angle((2 * off + 1)[:, None] * idx[None, :])])
    ang_a = jnp.stack([angle(start[:, None] * (2 * idx)[None, :]),
                       angle(start[:, None] * (2 * idx + 1)[None, :]),
                       angle((2 * start)[:, None] * idx[None, :])]).reshape(3 * nblk, 1, half)
    small = pl.BlockSpec((1, DFT_TB, half), lambda k, a: (k, 0, 0))
    base = pl.BlockSpec((1, 1, half), lambda k, a: (k * nblk + a, 0, 0))
    out = pl.BlockSpec((1, DFT_TB, half), lambda k, a: (k, a, 0))
    sds = jax.ShapeDtypeStruct((3, half, half), BF16)
    return pl.pallas_call(
        _dft_kernel, grid=(3, nblk), in_specs=[small, small, base, base], out_specs=[out, out],
        out_shape=[sds, sds], compiler_params=_cparams(("arbitrary", "arbitrary")),
        name="dft_tables")(jnp.cos(ang_d), jnp.sin(ang_d), jnp.cos(ang_a), jnp.sin(ang_a))


def _filter_features(seq):
    t = jnp.linspace(0.0, 1.0, seq, dtype=F32)[:, None]
    bands = (HY_EMB - 1) // 2
    fb = jnp.linspace(1e-4, bands - 1, bands, dtype=F32)[None]
    w = 2.0 * math.pi * jnp.arange(seq, dtype=F32)[:, None] / seq
    z = jnp.concatenate([t, jnp.cos(fb * w), -jnp.sin(fb * w)], -1)
    z = jnp.concatenate([z[0::2], z[1::2]], axis=0)
    return jnp.pad(z, ((0, 0), (0, LANES - HY_EMB)))


def _mixer(x, w_in, b_in, hy_conv_w, hy_conv_b, hy_filt_w1, hy_filt_b1, hy_filt_w2, hy_filt_b2,
           hy_filt_w3, hy_filt_freq, hy_skip, hy_norm_w, ml_conv_w, ml_conv_b, ml_norm_w):
    bsz, seq, _ = x.shape
    t = bsz * seq
    x2d = x.reshape(t, D_MODEL)
    n_main = w_in.shape[1] - N_GATE_COLS
    w_t = jnp.swapaxes(w_in, 0, 1)
    proj = _in_proj(x2d, w_t[:n_main].astype(BF16), b_in[None, :n_main])
    proj3 = proj.reshape(bsz, seq, n_main)
    wg = jnp.pad(w_t[n_main:], ((0, LANES - N_GATE_COLS), (0, 0)))
    bg = jnp.pad(b_in[None, n_main:], ((0, 0), (0, LANES - N_GATE_COLS)))
    gates = _gate_proj(x2d, wg, bg)[:, :N_GATE_COLS]
    g5 = gates.reshape(bsz, seq, 4, ML_HEADS)
    grow = g5.transpose(0, 3, 2, 1).reshape(bsz, ML_HEADS, 4, seq // CHUNK, CHUNK)
    gtr = grow.transpose(0, 1, 2, 4, 3)

    cmat, smat = _dft_tables(seq)
    zpad = _filter_features(seq)
    w1pad = jnp.pad(hy_filt_w1, ((0, LANES - HY_EMB), (0, 0)))
    deltas = jnp.abs(jnp.linspace(math.log(HY_DECAY_TARGET) / HY_SLOW_PCT,
                                  math.log(HY_DECAY_TARGET) / HY_FAST_PCT, D_HY, dtype=F32))[None]
    kr, ki, km = _hyena_filters(zpad, w1pad, hy_filt_b1[None], hy_filt_w2, hy_filt_b2[None],
                                hy_filt_freq, hy_filt_w3, deltas, cmat, smat)
    y_hy = _hyena(proj3, hy_conv_w, hy_conv_b[None], cmat, smat, kr, ki, km, hy_skip, hy_norm_w[None])
    y_ml = _mlstm(proj3, ml_conv_w, ml_conv_b[None], grow, gtr, ml_norm_w[None])
    return y_hy.reshape(t, D_HY), y_ml.reshape(t, D_ML), x2d


def _moe_tables(top_i, slot, counts):
    t = top_i.shape[0]
    n_tiles = N_EXPERTS + (t * TOP_K) // EXP_ROWS
    ntile = (counts + EXP_ROWS - 1) // EXP_ROWS
    ends = jnp.cumsum(ntile)
    starts = ends - ntile
    total = ends[-1]
    s_idx = jnp.arange(n_tiles, dtype=I32)
    valid = s_idx < total
    s_eff = jnp.where(valid, s_idx, jnp.maximum(total - 1, 0))
    tile_e = jnp.minimum(jnp.sum((s_eff[:, None] >= ends[None, :]).astype(I32), axis=1), N_EXPERTS - 1)
    local = s_eff - starts[tile_e]
    tile_rows = jnp.where(valid, jnp.clip(counts[tile_e] - local * EXP_ROWS, 0, EXP_ROWS), 0).astype(I32)
    tok = jnp.arange(t, dtype=I32)[:, None]
    row_tok = jnp.pad(jnp.sort((top_i * t + tok).reshape(-1)) % t, (0, EXP_ROWS))
    first = jnp.cumsum(counts) - counts
    tile_base = (first[tile_e] + local * EXP_ROWS).astype(I32)
    onehot = top_i[:, :, None] == jnp.arange(N_EXPERTS, dtype=I32)
    dest = jnp.sum(jnp.where(onehot, starts * EXP_ROWS, 0), axis=-1) + slot
    return tile_e.astype(I32), tile_rows, tile_base, row_tok.astype(I32), dest.astype(I32).reshape(-1)


def kernel(x, w_in, b_in, hy_conv_w, hy_conv_b, hy_filt_w1, hy_filt_b1, hy_filt_w2, hy_filt_b2, hy_filt_w3, hy_filt_freq, hy_skip, hy_norm_w, ml_conv_w, ml_conv_b, ml_norm_w, w_out, b_out, ln1_g, ln1_b, router_w, router_b, w_gu, b_gu, w_down, b_down, ln2_g, ln2_b):
    bsz, seq, _ = x.shape
    l = 0
    y_hy, y_ml, x2d = _mixer(x, w_in[l], b_in[l], hy_conv_w[l], hy_conv_b[l], hy_filt_w1[l], hy_filt_b1[l],
                             hy_filt_w2[l], hy_filt_b2[l], hy_filt_w3[l], hy_filt_freq[l], hy_skip[l],
                             hy_norm_w[l], ml_conv_w[l], ml_conv_b[l], ml_norm_w[l])
    x1, x1c = _out_proj_ln(y_hy, y_ml, x2d, w_out[l].astype(BF16), b_out[l][None], ln1_g[l][None],
                           ln1_b[l][None])
    rw = jnp.pad(router_w[l], ((0, 0), (0, LANES - N_EXPERTS)))
    rb = jnp.pad(router_b[l][None], ((0, 0), (0, LANES - N_EXPERTS)), constant_values=-1e30)
    top_i, top_g, slot, cnt = _router(x1, rw, rb)
    counts = cnt[0, :N_EXPERTS].astype(I32)
    tile_e, tile_rows, tile_base, row_tok, dest = _moe_tables(top_i[:, :TOP_K], slot[:, :TOP_K], counts)
    y_buf = _experts(tile_e, tile_rows, tile_base, row_tok, x1c, w_gu[l], b_gu[l][:, None, :], w_down[l],
                     b_down[l][:, None, :])
    out = _combine_ln(dest, y_buf, x1, top_g, ln2_g[l][None], ln2_b[l][None])
    return out.reshape(bsz, seq, D_MODEL)
```

```python
import functools
import math

import jax
import jax.numpy as jnp
from jax import lax
from jax.experimental import pallas as pl
from jax.experimental.pallas import tpu as pltpu

F32 = jnp.float32
BF16 = jnp.bfloat16
I32 = jnp.int32
HP = lax.Precision.HIGHEST

D_MODEL = 2048
D_HY = 1024
D_ML = 1024
ML_HEADS = 8
HEAD_DIM = 128
CHUNK = 128
N_GATE_COLS = 32
HY_EMB = 33
N_EXPERTS = 32
TOP_K = 4
D_FF = 2048
SWIGLU_LIMIT = 7.0
SWIGLU_ALPHA = 1.702
LN_EPS = 1e-5
DN_ALPHA = 2.0 ** 0.25
HY_DECAY_TARGET = 1e-2
HY_FAST_PCT = 0.3
HY_SLOW_PCT = 1.5

LANES = 128
ROW_CHUNKS = D_MODEL // LANES
VMEM_LIMIT = 56 * 1024 * 1024

PROJ_TM = 2048
PROJ_TN = 512
HY_CW = 256
HY_FC = 512
OUT_TM = 512
RT_TM = 256
EXP_ROWS = 1152
EXP_M_SIZES = (1024, 1088, 1152)
EXP_CHUNK = 128
EXP_TF = 256
GATHER_UNROLL = 8
CMB_TM = 128
DFT_TB = 128


def _cparams(sem):
    return pltpu.CompilerParams(dimension_semantics=sem, vmem_limit_bytes=VMEM_LIMIT)


def _split(a):
    hi = a.astype(BF16)
    return hi, (a - hi.astype(F32)).astype(BF16)


def _dot3(a, b, dims):
    a_hi, a_lo = _split(a)
    b_hi, b_lo = _split(b)
    mm = functools.partial(lax.dot_general, dimension_numbers=dims, preferred_element_type=F32)
    return mm(a_hi, b_hi) + (mm(a_hi, b_lo) + mm(a_lo, b_hi))


NN = (((1,), (0,)), ((), ()))
NT = (((1,), (1,)), ((), ()))


def _const_spec(shape, index_map):
    return pl.BlockSpec(shape, index_map, pipeline_mode=pl.Buffered(1))


def _proj_kernel(x_ref, w_ref, b_ref, o_ref, xb_ref):
    @pl.when(pl.program_id(1) == 0)
    def _():
        xb_ref[...] = x_ref[...].astype(BF16)

    acc = lax.dot_general(xb_ref[...], w_ref[...], (((1,), (1,)), ((), ())), preferred_element_type=F32)
    o_ref[...] = (acc + b_ref[...]).astype(o_ref.dtype)


def _in_proj(x2d, wt_bf, b_row):
    m, k = x2d.shape
    n = wt_bf.shape[0]
    return pl.pallas_call(
        _proj_kernel,
        grid=(m // PROJ_TM, n // PROJ_TN),
        in_specs=[pl.BlockSpec((PROJ_TM, k), lambda i, j: (i, 0)),
                  pl.BlockSpec((PROJ_TN, k), lambda i, j: (j, 0)),
                  pl.BlockSpec((1, PROJ_TN), lambda i, j: (0, j))],
        out_specs=pl.BlockSpec((PROJ_TM, PROJ_TN), lambda i, j: (i, j)),
        out_shape=jax.ShapeDtypeStruct((m, n), BF16),
        scratch_shapes=[pltpu.VMEM((PROJ_TM, k), BF16)],
        compiler_params=_cparams(("arbitrary", "arbitrary")),
        name="in_proj")(x2d, wt_bf, b_row)


def _gate_kernel(x_ref, w_ref, b_ref, o_ref):
    o_ref[...] = _dot3(x_ref[...], w_ref[...], NT) + b_ref[...]


def _gate_proj(x2d, wt_pad, b_pad):
    m, k = x2d.shape
    tm = 512
    return pl.pallas_call(
        _gate_kernel,
        grid=(m // tm,),
        in_specs=[pl.BlockSpec((tm, k), lambda i: (i, 0)),
                  pl.BlockSpec((LANES, k), lambda i: (0, 0)),
                  pl.BlockSpec((1, LANES), lambda i: (0, 0))],
        out_specs=pl.BlockSpec((tm, LANES), lambda i: (i, 0)),
        out_shape=jax.ShapeDtypeStruct((m, LANES), F32),
        compiler_params=_cparams(("arbitrary",)),
        name="gate_proj")(x2d, wt_pad, b_pad)


def _filter_kernel(z_ref, w1_ref, b1_ref, w2_ref, b2_ref, fq_ref, w3f_ref, w3b_ref, dl_ref,
                   c_ref, s_ref, kr_ref, ki_ref, km_ref, h_ref):
    seq = z_ref.shape[0]
    inv_n = 1.0 / (2 * seq)
    z = z_ref[...]

    @pl.when(jnp.logical_and(pl.program_id(0) == 0, pl.program_id(1) == 0))
    def _():
        h1 = jnp.sin(fq_ref[0:1, :] * (jnp.dot(z, w1_ref[...], precision=HP, preferred_element_type=F32)
                                       + b1_ref[...]))
        h_ref[...] = jnp.sin(fq_ref[1:2, :] * (jnp.dot(h1, w2_ref[...], precision=HP, preferred_element_type=F32)
                                               + b2_ref[...]))

    half = seq // 2
    h = h_ref[...]
    win = jnp.exp(-z[:, 0:1] * dl_ref[...])
    fwd = _dot3(h, w3f_ref[...], NN) * win
    bwd = _dot3(h, w3b_ref[...], NN) * win
    row = lax.broadcasted_iota(I32, fwd.shape, 0)
    bwd = jnp.where(row == 0, 0.0, bwd)
    inv = 1.0 / jnp.sum(jnp.abs(fwd) + jnp.abs(bwd), axis=0, keepdims=True)
    ks = ((fwd + bwd) * inv)
    kd = ((fwd - bwd) * inv)
    ksb = ks.astype(BF16)
    kdb = kd.astype(BF16)
    ec = jnp.dot(c_ref[0], ksb[:half], preferred_element_type=F32)
    oc = jnp.dot(c_ref[1], ksb[half:], preferred_element_type=F32)
    es = jnp.dot(s_ref[0], kdb[:half], preferred_element_type=F32)
    os_ = jnp.dot(s_ref[1], kdb[half:], preferred_element_type=F32)
    rowh = lax.broadcasted_iota(I32, ec.shape, 0)
    wf = jnp.where(rowh == 0, inv_n, 2.0 * inv_n)
    kr_ref[0, :half, :] = (ec + oc) * wf
    kr_ref[0, half:, :] = (ec - oc) * wf
    ki_ref[0, :half, :] = -(es + os_) * wf
    ki_ref[0, half:, :] = (es - os_) * wf
    alt = jnp.where((rowh & 1) == 0, 1.0, -1.0)
    km_ref[0, 0:1, :] = jnp.sum(ks[:half] * alt, axis=0, keepdims=True) * (2.0 * inv_n)
    km_ref[0, 1:2, :] = -jnp.sum(kd[half:] * alt, axis=0, keepdims=True) * (2.0 * inv_n)


def _hyena_filters(zpad, w1pad, b1, w2, b2, freq, w3, deltas, cmat, smat):
    seq = zpad.shape[0]
    nb = D_HY // HY_CW
    hid = w2.shape[0]
    full = lambda shape: pl.BlockSpec(shape, lambda o, c: (0,) * len(shape))
    out_sds = jax.ShapeDtypeStruct((2, seq, D_HY), F32)
    return pl.pallas_call(
        _filter_kernel,
        grid=(2, nb),
        in_specs=[full(zpad.shape), full(w1pad.shape), full(b1.shape), full(w2.shape), full(b2.shape),
                  full(freq.shape),
                  pl.BlockSpec((hid, HY_CW), lambda o, c: (0, o * 2 * nb + c)),
                  pl.BlockSpec((hid, HY_CW), lambda o, c: (0, o * 2 * nb + nb + c)),
                  pl.BlockSpec((1, HY_CW), lambda o, c: (0, c)),
                  _const_spec(cmat.shape, lambda o, c: (0, 0, 0)),
                  _const_spec(smat.shape, lambda o, c: (0, 0, 0))],
        out_specs=[pl.BlockSpec((1, seq, HY_CW), lambda o, c: (o, 0, c)),
                   pl.BlockSpec((1, seq, HY_CW), lambda o, c: (o, 0, c)),
                   pl.BlockSpec((1, 2, HY_CW), lambda o, c: (o, 0, c))],
        out_shape=[out_sds, out_sds, jax.ShapeDtypeStruct((2, 2, D_HY), F32)],
        scratch_shapes=[pltpu.VMEM((seq, hid), F32)],
        compiler_params=_cparams(("arbitrary", "arbitrary")),
        name="hyena_filters")(zpad, w1pad, b1, w2, b2, freq, w3, w3, deltas, cmat, smat)


def _short_conv(u, w_ref, b_ref, row, seq):
    prev = jnp.where(row == 0, 0.0, pltpu.roll(u, 1, 0))
    nxt = jnp.where(row == seq - 1, 0.0, pltpu.roll(u, seq - 1, 0))
    return w_ref[0:1, :] * prev + w_ref[1:2, :] * u + w_ref[2:3, :] * nxt + b_ref[...]


def _hyena_kernel(uv_ref, u1_ref, u2_ref, wv_ref, w1_ref, w2_ref, bv_ref, b1_ref, b2_ref,
                  c_ref, s_ref, kr_ref, ki_ref, km_ref, skip_ref, nw_ref, o_ref,
                  a_ref, b_ref, t_ref, zb_ref, s1_ref, s2_ref, s3_ref, s4_ref, md_ref):
    seq = uv_ref.shape[1]
    cw = uv_ref.shape[2]
    half = seq // 2
    nblk = half // HY_FC
    row = lax.broadcasted_iota(I32, (seq, LANES), 0)
    alt_half = jnp.where((lax.broadcasted_iota(I32, (half, cw), 0) & 1) == 0, 1.0, -1.0)
    alt_blk = jnp.where((lax.broadcasted_iota(I32, (HY_FC, cw), 0) & 1) == 0, 1.0, -1.0)
    groups = [slice(g * LANES, (g + 1) * LANES) for g in range(cw // LANES)]

    def conv_to(dst_ref, u_ref, w_ref, bias_ref):
        for gi, gs in enumerate(groups):
            t_ref[gi] = _short_conv(u_ref[0, :, gs].astype(F32), w_ref.at[:, gs], bias_ref.at[:, gs], row, seq)
            dst_ref[0:half, gs] = t_ref[gi, pl.ds(0, half, stride=2), :]
            dst_ref[half:seq, gs] = t_ref[gi, pl.ds(1, half, stride=2), :]

    def spectrum(zin_ref, o):
        z = zin_ref[...]
        zb_ref[...] = z.astype(BF16)
        am = jnp.sum(z[:half] * alt_half, axis=0, keepdims=True)
        bm = jnp.sum(z[half:] * alt_half, axis=0, keepdims=True)
        krm, kim = km_ref[o, 0:1, :], km_ref[o, 1:2, :]
        md_ref[0:1, :] = am * krm + bm * kim
        md_ref[1:2, :] = am * kim - bm * krm
        ze = zb_ref[0:half, :]
        zo = zb_ref[half:seq, :]
        for gb in range(nblk):
            lo = slice(gb * HY_FC, (gb + 1) * HY_FC)
            hi = slice(half + gb * HY_FC, half + (gb + 1) * HY_FC)
            ec = jnp.dot(c_ref[0, lo, :], ze, preferred_element_type=F32)
            oc = jnp.dot(c_ref[1, lo, :], zo, preferred_element_type=F32)
            es = jnp.dot(s_ref[0, lo, :], ze, preferred_element_type=F32)
            os_ = jnp.dot(s_ref[1, lo, :], zo, preferred_element_type=F32)
            a_lo, b_lo = ec + oc, es + os_
            a_hi, b_hi = ec - oc, os_ - es
            krl, kil = kr_ref[o, lo, :], ki_ref[o, lo, :]
            krh, kih = kr_ref[o, hi, :], ki_ref[o, hi, :]
            pr = a_lo * krl + b_lo * kil
            pi = a_lo * kil - b_lo * krl
            qr = a_hi * krh + b_hi * kih
            qi = a_hi * kih - b_hi * krh
            s1_ref[lo, :] = (pr + qr).astype(BF16)
            s2_ref[lo, :] = (pi - qi).astype(BF16)
            s3_ref[lo, :] = (pr - qr).astype(BF16)
            s4_ref[lo, :] = (pi + qi).astype(BF16)

    def conv_rows(ub, parity):
        us = slice(ub * HY_FC, (ub + 1) * HY_FC)
        if parity == 0:
            y = jnp.dot(c_ref[0, us, :], s1_ref[...], preferred_element_type=F32)
            y = y - jnp.dot(s_ref[0, us, :], s2_ref[...], preferred_element_type=F32)
            y = y + md_ref[0:1, :] * alt_blk
        else:
            y = jnp.dot(c_ref[2, us, :], s3_ref[...], preferred_element_type=F32)
            y = y - jnp.dot(s_ref[2, us, :], s4_ref[...], preferred_element_type=F32)
            y = y - md_ref[1:2, :] * alt_blk
        return slice(parity * half + ub * HY_FC, parity * half + (ub + 1) * HY_FC), y

    blocks = [(ub, parity) for parity in range(2) for ub in range(nblk)]
    conv_to(a_ref, uv_ref, wv_ref, bv_ref)
    conv_to(b_ref, u1_ref, w1_ref, b1_ref)
    spectrum(a_ref, 0)
    for ub, parity in blocks:
        rows, y = conv_rows(ub, parity)
        b_ref[rows, :] = b_ref[rows, :] * (y + skip_ref[0:1, :] * a_ref[rows, :])
    conv_to(a_ref, u2_ref, w2_ref, b2_ref)
    spectrum(b_ref, 1)
    for ub, parity in blocks:
        rows, y = conv_rows(ub, parity)
        z = a_ref[rows, :] * (y + skip_ref[1:2, :] * b_ref[rows, :])
        for gi, gs in enumerate(groups):
            zg = z[:, gs]
            mu = jnp.mean(zg, axis=-1, keepdims=True)
            zc = zg - mu
            var = jnp.mean(zc * zc, axis=-1, keepdims=True)
            t_ref[gi, pl.ds(2 * ub * HY_FC + parity, HY_FC, stride=2), :] = (
                zc * lax.rsqrt(var + LN_EPS) * nw_ref[:, gs])
    for gi, gs in enumerate(groups):
        o_ref[0, :, gs] = t_ref[gi].astype(o_ref.dtype)


def _hyena(proj3, conv_w, conv_b, cmat, smat, kr, ki, km, skip, norm_w):
    bsz, seq, _ = proj3.shape
    half = seq // 2
    nb = D_HY // HY_CW
    u_spec = lambda off: pl.BlockSpec((1, seq, HY_CW), lambda c, b: (b, 0, off + c))
    w_spec = lambda off: pl.BlockSpec((3, HY_CW), lambda c, b: (0, off + c))
    b_spec = lambda off: pl.BlockSpec((1, HY_CW), lambda c, b: (0, off + c))
    return pl.pallas_call(
        _hyena_kernel,
        grid=(nb, bsz),
        in_specs=[u_spec(0), u_spec(nb), u_spec(2 * nb),
                  w_spec(0), w_spec(nb), w_spec(2 * nb),
                  b_spec(0), b_spec(nb), b_spec(2 * nb),
                  _const_spec(cmat.shape, lambda c, b: (0, 0, 0)),
                  _const_spec(smat.shape, lambda c, b: (0, 0, 0)),
                  _const_spec((2, seq, HY_CW), lambda c, b: (0, 0, c)),
                  _const_spec((2, seq, HY_CW), lambda c, b: (0, 0, c)),
                  pl.BlockSpec((2, 2, HY_CW), lambda c, b: (0, 0, c)),
                  pl.BlockSpec((2, HY_CW), lambda c, b: (0, c)),
                  pl.BlockSpec((1, HY_CW), lambda c, b: (0, c))],
        out_specs=pl.BlockSpec((1, seq, HY_CW), lambda c, b: (b, 0, c)),
        out_shape=jax.ShapeDtypeStruct((bsz, seq, D_HY), BF16),
        scratch_shapes=[pltpu.VMEM((seq, HY_CW), F32), pltpu.VMEM((seq, HY_CW), F32),
                        pltpu.VMEM((HY_CW // LANES, seq, LANES), F32), pltpu.VMEM((seq, HY_CW), BF16),
                        pltpu.VMEM((half, HY_CW), BF16), pltpu.VMEM((half, HY_CW), BF16),
                        pltpu.VMEM((half, HY_CW), BF16), pltpu.VMEM((half, HY_CW), BF16),
                        pltpu.VMEM((2, HY_CW), F32)],
        compiler_params=_cparams(("arbitrary", "arbitrary")),
        name="hyena")(proj3, proj3, proj3, conv_w, conv_w, conv_w, conv_b, conv_b, conv_b,
                      cmat, smat, kr, ki, km, skip, norm_w)


def _mlstm_kernel(qp_ref, kp_ref, v_ref, og_ref, wq_ref, wk_ref, bq_ref, bk_ref, gr_ref, gt_ref,
                  nw_ref, o_ref, qb_ref, kb_ref, hacc_ref, cst_ref):
    seq = qp_ref.shape[1]
    d = qp_ref.shape[2]
    nchunk = seq // CHUNK
    row = lax.broadcasted_iota(I32, (seq, d), 0)
    q = _short_conv(qp_ref[0].astype(F32), wq_ref, bq_ref, row, seq)
    k = _short_conv(kp_ref[0].astype(F32), wk_ref, bk_ref, row, seq)
    qb_ref[...] = (q * jax.nn.sigmoid(q)).astype(BF16)
    kb_ref[...] = ((k * jax.nn.sigmoid(k)) * (d ** -0.5)).astype(BF16)

    ti = lax.broadcasted_iota(I32, (CHUNK, CHUNK), 0)
    si = lax.broadcasted_iota(I32, (CHUNK, CHUNK), 1)
    lower = ti >= si
    upper = ti <= si
    lower_f = lower.astype(F32)
    upper_f = upper.astype(F32)
    nt = (((1,), (1,)), ((), ()))
    tn = (((0,), (0,)), ((), ()))
    chunk_rows = [slice(c * CHUNK, (c + 1) * CHUNK) for c in range(nchunk)]

    for direction in range(2):
        f_idx, i_idx = 2 * direction + 1, 2 * direction
        mask = lower if direction == 0 else upper
        order = list(range(nchunk)) if direction == 0 else list(range(nchunk - 1, -1, -1))
        lf_r = jax.nn.log_sigmoid(gr_ref[0, 0, f_idx])
        b_r = jnp.dot(lf_r, upper_f if direction == 0 else lower_f, precision=HP, preferred_element_type=F32)
        rterm = b_r - gr_ref[0, 0, i_idx]
        b_last = jnp.sum(lf_r, axis=-1, keepdims=True)
        lf_c = jax.nn.log_sigmoid(gt_ref[0, 0, f_idx])
        b_c = jnp.dot(lower_f if direction == 0 else upper_f, lf_c, precision=HP, preferred_element_type=F32)
        i_c = gt_ref[0, 0, i_idx]

        bcol, gcol, gmax, rowmax, blast = [], [], [], [], []
        for c in range(nchunk):
            bc = jnp.broadcast_to(b_c[:, c:c + 1], (CHUNK, CHUNK))
            ic = jnp.broadcast_to(i_c[:, c:c + 1], (CHUNK, CHUNK))
            bl = jnp.broadcast_to(b_last[c:c + 1, :], (1, CHUNK))
            dmat = jnp.where(mask, bc - rterm[c:c + 1, :], -jnp.inf)
            g = bl - bc + ic
            bcol.append(bc)
            gcol.append(g)
            blast.append(bl)
            rowmax.append(jnp.max(dmat, axis=-1, keepdims=True))
            gmax.append(jnp.max(g, axis=0, keepdims=True))

        m = jnp.zeros((1, CHUNK), F32)
        m_in, m_out = [None] * nchunk, [None] * nchunk
        for c in order:
            m_in[c] = m
            m = jnp.maximum(blast[c] + m, gmax[c])
            m_out[c] = m

        g_row = b_last - rterm
        cmat = jnp.zeros((d, d), F32)
        nvec = jnp.zeros((1, d), F32)
        for c in order:
            kc = kb_ref[chunk_rows[c], :]
            vc = v_ref[0, chunk_rows[c], :]
            cst_ref[c, 0:d, :] = cmat.astype(BF16)
            cst_ref[c, d:2 * d, :] = jnp.broadcast_to(nvec, (d, d)).astype(BF16)
            wg = jnp.exp(gcol[c] - m_out[c]).astype(BF16)
            wg_row = jnp.broadcast_to(jnp.exp(g_row[c:c + 1, :] - m_out[c]), (8, CHUNK)).astype(BF16)
            decay = jnp.exp(blast[c] + m_in[c] - m_out[c])
            upd = lax.dot_general(wg * vc, kc, tn, preferred_element_type=F32)
            cmat = decay * cmat + upd
            nvec = decay * nvec + jnp.dot(wg_row, kc, preferred_element_type=F32)[0:1, :]

        ones_blk = jnp.ones((CHUNK, d), BF16)
        for c in range(nchunk):
            rs = chunk_rows[c]
            qc = qb_ref[rs, :]
            kc = kb_ref[rs, :]
            vone = jnp.concatenate([v_ref[0, rs, :], ones_blk], axis=-1)
            dmat = jnp.where(mask, bcol[c] - rterm[c:c + 1, :], -jnp.inf)
            inter = bcol[c] + m_in[c]
            m_t = jnp.maximum(inter, rowmax[c])
            p = jnp.exp(dmat - m_t)
            inter_w = jnp.exp(inter - m_t)
            s = lax.dot_general(qc, kc, nt, preferred_element_type=F32) * p
            intra = jnp.dot(s.astype(BF16), vone, preferred_element_type=F32)
            cross = lax.dot_general(qc, cst_ref[c], nt, preferred_element_type=F32)
            num = intra[:, :d] + inter_w * cross[:, :d]
            den = intra[:, d:] + inter_w * cross[:, d:]
            h = num / jnp.maximum(jnp.abs(den), jnp.exp(-m_t))
            if direction == 0:
                hacc_ref[rs, :] = h
            else:
                hacc_ref[rs, :] += h

    h = hacc_ref[...]
    mu = jnp.mean(h, axis=-1, keepdims=True)
    hc = h - mu
    var = jnp.mean(hc * hc, axis=-1, keepdims=True)
    y = hc * lax.rsqrt(var + LN_EPS) * nw_ref[...] * jax.nn.sigmoid(og_ref[0].astype(F32))
    o_ref[0] = y.astype(o_ref.dtype)


def _mlstm(proj3, conv_w, conv_b, grow, gtr, norm_w):
    bsz, seq, _ = proj3.shape
    d = HEAD_DIM
    nchunk = seq // CHUNK
    hy_blocks = 3 * D_HY // d
    qoff, koff, voff, ooff = hy_blocks, hy_blocks + ML_HEADS, hy_blocks + 2 * ML_HEADS, hy_blocks + 3 * ML_HEADS
    p_spec = lambda off: pl.BlockSpec((1, seq, d), lambda b, h: (b, 0, off + h))
    return pl.pallas_call(
        _mlstm_kernel,
        grid=(bsz, ML_HEADS),
        in_specs=[p_spec(qoff), p_spec(koff), p_spec(voff), p_spec(ooff),
                  pl.BlockSpec((3, d), lambda b, h: (0, h)),
                  pl.BlockSpec((3, d), lambda b, h: (0, ML_HEADS + h)),
                  pl.BlockSpec((1, d), lambda b, h: (0, h)),
                  pl.BlockSpec((1, d), lambda b, h: (0, ML_HEADS + h)),
                  pl.BlockSpec((1, 1, 4, nchunk, CHUNK), lambda b, h: (b, h, 0, 0, 0)),
                  pl.BlockSpec((1, 1, 4, CHUNK, nchunk), lambda b, h: (b, h, 0, 0, 0)),
                  pl.BlockSpec((1, d), lambda b, h: (0, h))],
        out_specs=pl.BlockSpec((1, seq, d), lambda b, h: (b, 0, h)),
        out_shape=jax.ShapeDtypeStruct((bsz, seq, D_ML), BF16),
        scratch_shapes=[pltpu.VMEM((seq, d), BF16), pltpu.VMEM((seq, d), BF16),
                        pltpu.VMEM((seq, d), F32),
                        pltpu.VMEM((nchunk, 2 * d, d), BF16)],
        compiler_params=_cparams(("arbitrary", "arbitrary")),
        name="mlstm")(proj3, proj3, proj3, proj3, conv_w, conv_w, conv_b, conv_b, grow, gtr, norm_w)


def _to_slabs(y):
    n = y.shape[0]
    parts = jnp.stack([y[:, c * LANES:(c + 1) * LANES] for c in range(ROW_CHUNKS)], axis=0)
    return pltpu.einshape("crl->rcl", parts).reshape(n * ROW_CHUNKS, LANES)


def _from_slabs(v):
    n = v.shape[0] // ROW_CHUNKS
    parts = pltpu.einshape("rcl->crl", v.reshape(n, ROW_CHUNKS, LANES))
    return jnp.concatenate([parts[c] for c in range(ROW_CHUNKS)], axis=-1)


def _layer_norm(u, g, b):
    mu = jnp.mean(u, axis=-1, keepdims=True)
    uc = u - mu
    var = jnp.mean(uc * uc, axis=-1, keepdims=True)
    return uc * lax.rsqrt(var + LN_EPS) * g + b


def _outproj_kernel(yh_ref, ym_ref, x_ref, wa_ref, wb_ref, b_ref, g_ref, be_ref, o_ref, oc_ref):
    mix = (jnp.dot(yh_ref[...], wa_ref[...], preferred_element_type=F32)
           + jnp.dot(ym_ref[...], wb_ref[...], preferred_element_type=F32) + b_ref[...])
    y = _layer_norm(DN_ALPHA * x_ref[...] + mix, g_ref[...], be_ref[...])
    o_ref[...] = y
    oc_ref[...] = _to_slabs(y)


def _out_proj_ln(y_hy, y_ml, x2d, w_out_bf, b_out, g, be):
    t = x2d.shape[0]
    tm = OUT_TM
    vec = lambda: pl.BlockSpec((1, D_MODEL), lambda i: (0, 0))
    return pl.pallas_call(
        _outproj_kernel,
        grid=(t // tm,),
        in_specs=[pl.BlockSpec((tm, D_HY), lambda i: (i, 0)),
                  pl.BlockSpec((tm, D_ML), lambda i: (i, 0)),
                  pl.BlockSpec((tm, D_MODEL), lambda i: (i, 0)),
                  _const_spec((D_HY, D_MODEL), lambda i: (0, 0)),
                  _const_spec((D_ML, D_MODEL), lambda i: (1, 0)),
                  vec(), vec(), vec()],
        out_specs=[pl.BlockSpec((tm, D_MODEL), lambda i: (i, 0)),
                   pl.BlockSpec((tm * ROW_CHUNKS, LANES), lambda i: (i, 0))],
        out_shape=[jax.ShapeDtypeStruct((t, D_MODEL), F32),
                   jax.ShapeDtypeStruct((t * ROW_CHUNKS, LANES), F32)],
        compiler_params=_cparams(("arbitrary",)),
        name="out_proj_ln1")(y_hy, y_ml, x2d, w_out_bf, w_out_bf, b_out, g, be)


def _router_kernel(x_ref, w_ref, b_ref, ti_ref, tg_ref, tp_ref, cnt_ref):
    tm = ti_ref.shape[0]

    @pl.when(pl.program_id(0) == 0)
    def _():
        cnt_ref[...] = jnp.zeros_like(cnt_ref)

    logits = _dot3(x_ref[...], w_ref[...], NN) + b_ref[...]
    lane = lax.broadcasted_iota(I32, (tm, LANES), 1)
    work = logits
    vals, idxs = [], []
    chosen = jnp.zeros((tm, LANES), F32)
    for _ in range(TOP_K):
        mx = jnp.max(work, axis=-1, keepdims=True)
        idx = jnp.min(jnp.where(work == mx, lane, LANES), axis=-1, keepdims=True)
        hit = lane == idx
        vals.append(mx)
        idxs.append(idx)
        chosen = jnp.where(hit, 1.0, chosen)
        work = jnp.where(hit, -jnp.inf, work)
    exps = [jnp.exp(v - vals[0]) for v in vals]
    den = exps[0] + exps[1] + exps[2] + exps[3]
    ri = lax.broadcasted_iota(I32, (tm, tm), 0)
    ci = lax.broadcasted_iota(I32, (tm, tm), 1)
    strict_lower = (ri > ci).astype(BF16)
    carry = cnt_ref[...]
    slot = carry + jnp.dot(strict_lower, chosen.astype(BF16), preferred_element_type=F32)
    ti = jnp.zeros((tm, LANES), I32)
    tg = jnp.zeros((tm, LANES), F32)
    tp = jnp.zeros((tm, LANES), F32)
    for k in range(TOP_K):
        sk = jnp.sum(jnp.where(lane == idxs[k], slot, 0.0), axis=-1, keepdims=True)
        ti = jnp.where(lane == k, idxs[k], ti)
        tg = jnp.where(lane == k, exps[k] / den, tg)
        tp = jnp.where(lane == k, sk, tp)
    ti_ref[...] = ti
    tg_ref[...] = tg
    tp_ref[...] = tp.astype(I32)
    cnt_ref[...] = carry + jnp.sum(chosen, axis=0, keepdims=True)


def _router(x1, w_pad, b_pad):
    t = x1.shape[0]
    tm = RT_TM
    o_spec = lambda: pl.BlockSpec((tm, LANES), lambda i: (i, 0))
    return pl.pallas_call(
        _router_kernel,
        grid=(t // tm,),
        in_specs=[pl.BlockSpec((tm, D_MODEL), lambda i: (i, 0)),
                  pl.BlockSpec((D_MODEL, LANES), lambda i: (0, 0)),
                  pl.BlockSpec((1, LANES), lambda i: (0, 0))],
        out_specs=[o_spec(), o_spec(), o_spec(), pl.BlockSpec((1, LANES), lambda i: (0, 0))],
        out_shape=[jax.ShapeDtypeStruct((t, LANES), I32), jax.ShapeDtypeStruct((t, LANES), F32),
                   jax.ShapeDtypeStruct((t, LANES), I32), jax.ShapeDtypeStruct((1, LANES), F32)],
        compiler_params=_cparams(("arbitrary",)),
        name="router")(x1, w_pad, b_pad)


def _expert_kernel(te_ref, tr_ref, tb_ref, rt_ref, x_hbm, wg_ref, wu_ref, wd_ref, bg_ref, bu_ref, bd_ref, y_hbm,
                   stage_ref, xb_ref, acc_ref, ring_ref, wgb_ref, wub_ref, wdb_ref, gsem, osem):
    g = pl.program_id(0)
    n_items = pl.num_programs(0) - 1
    nf = D_FF // EXP_TF
    n_tiles = n_items // nf
    item = jnp.maximum(g - 1, 0)
    s = item // nf
    j = item % nf
    rows = tr_ref[s]
    active = jnp.logical_and(g >= 1, rows > 0)
    cast_slot = g % 2
    use_slot = (g + 1) % 2
    slab = EXP_CHUNK * ROW_CHUNKS
    per_step = EXP_ROWS // nf

    def row_copy(tok, r):
        return pltpu.make_async_copy(
            x_hbm.at[pl.ds(pl.multiple_of(tok * ROW_CHUNKS, ROW_CHUNKS), ROW_CHUNKS), :],
            stage_ref.at[pl.ds(pl.multiple_of(r * ROW_CHUNKS, ROW_CHUNKS), ROW_CHUNKS), :], gsem)

    def wait_gather():
        pltpu.make_async_copy(x_hbm.at[pl.ds(0, EXP_ROWS * ROW_CHUNKS), :], stage_ref, gsem).wait()

    def cast_weights():
        wgb_ref[cast_slot] = wg_ref[0].astype(BF16)
        wub_ref[cast_slot] = wu_ref[0].astype(BF16)
        wdb_ref[cast_slot] = wd_ref[0].astype(BF16)

    @pl.when(g == 0)
    def _():
        stage_ref[...] = jnp.zeros_like(stage_ref)
        acc_ref[...] = jnp.zeros_like(acc_ref)
        base = tb_ref[0]

        def group(q, carry):
            for u in range(GATHER_UNROLL):
                r = q * GATHER_UNROLL + u
                row_copy(rt_ref[base + r], r).start()
            return carry

        lax.fori_loop(0, EXP_ROWS // GATHER_UNROLL, group, 0)

        cast_weights()

    @pl.when(active)
    def _():
        @pl.when(j == 0)
        def _():
            wait_gather()
            for i in range(EXP_ROWS // EXP_CHUNK):
                xb_ref[i * EXP_CHUNK:(i + 1) * EXP_CHUNK, :] = _from_slabs(
                    stage_ref[i * slab:(i + 1) * slab, :]).astype(BF16)

        def step_body(m):
            cast_weights()
            nxt_base = tb_ref[jnp.minimum(s + 1, n_tiles - 1)]
            for u in range(per_step):
                r = j * per_step + u
                row_copy(rt_ref[nxt_base + r], r).start()

            xb = xb_ref[0:m, :]
            gate = jnp.dot(xb, wgb_ref[use_slot], preferred_element_type=F32) + bg_ref[0]
            up = jnp.dot(xb, wub_ref[use_slot], preferred_element_type=F32) + bu_ref[0]
            gate = jnp.minimum(gate, SWIGLU_LIMIT)
            up = jnp.clip(up, -SWIGLU_LIMIT, SWIGLU_LIMIT)
            act = (up + 1.0) * (gate * jax.nn.sigmoid(SWIGLU_ALPHA * gate))
            part = jnp.dot(act.astype(BF16), wdb_ref[use_slot], preferred_element_type=F32)
            acc_ref[0:m, :] = jnp.where(j == 0, jnp.broadcast_to(bd_ref[0], part.shape), acc_ref[0:m, :]) + part

        lo = 0
        for m in EXP_M_SIZES:
            @pl.when(jnp.logical_and(rows > lo, rows <= m))
            def _(m=m):
                step_body(m)
            lo = m

        @pl.when(j == nf - 1)
        def _():
            nchunk = (rows + EXP_CHUNK - 1) // EXP_CHUNK

            def chunk_copy(i, slot):
                dst0 = pl.multiple_of((s * EXP_ROWS + i * EXP_CHUNK) * ROW_CHUNKS, slab)
                return pltpu.make_async_copy(ring_ref.at[slot], y_hbm.at[pl.ds(dst0, slab), :], osem.at[slot])

            def emit(i, carry):
                slot = i % 2

                @pl.when(i >= 2)
                def _():
                    chunk_copy(i - 2, slot).wait()

                r0 = pl.multiple_of(i * EXP_CHUNK, EXP_CHUNK)
                ring_ref[slot] = _to_slabs(acc_ref[pl.ds(r0, EXP_CHUNK), :])
                chunk_copy(i, slot).start()
                return carry

            lax.fori_loop(0, nchunk, emit, 0)
            for back in range(2):
                @pl.when(nchunk > back)
                def _():
                    last = nchunk - 1 - back
                    chunk_copy(last, last % 2).wait()

    @pl.when(g == n_items)
    def _():
        wait_gather()


def _experts(tile_e, tile_rows, tile_base, row_tok, x1c, w_gu, b_gu, w_down, b_down):
    n_tiles = tile_e.shape[0]
    nf = D_FF // EXP_TF
    n_items = n_tiles * nf

    def item_block(item, te, tr):
        s = item // nf
        return te[s], jnp.where(tr[s] > 0, item % nf, nf - 1)

    def cast_item(g, te, tr):
        return item_block(jnp.minimum(g, n_items - 1), te, tr)

    def use_item(g, te, tr):
        return item_block(jnp.maximum(g - 1, 0), te, tr)

    def w_gate(g, te, tr, tb, rt):
        e, j = cast_item(g, te, tr)
        return e, 0, j

    def w_up(g, te, tr, tb, rt):
        e, j = cast_item(g, te, tr)
        return e, 0, nf + j

    def w_down_map(g, te, tr, tb, rt):
        e, j = cast_item(g, te, tr)
        return e, j, 0

    def b_gate(g, te, tr, tb, rt):
        e, j = use_item(g, te, tr)
        return e, 0, j

    def b_up(g, te, tr, tb, rt):
        e, j = use_item(g, te, tr)
        return e, 0, nf + j

    def b_down_map(g, te, tr, tb, rt):
        e, _ = use_item(g, te, tr)
        return e, 0, 0

    grid_spec = pltpu.PrefetchScalarGridSpec(
        num_scalar_prefetch=4,
        grid=(n_items + 1,),
        in_specs=[pl.BlockSpec(memory_space=pl.ANY),
                  pl.BlockSpec((1, D_MODEL, EXP_TF), w_gate),
                  pl.BlockSpec((1, D_MODEL, EXP_TF), w_up),
                  pl.BlockSpec((1, EXP_TF, D_MODEL), w_down_map),
                  pl.BlockSpec((1, 1, EXP_TF), b_gate),
                  pl.BlockSpec((1, 1, EXP_TF), b_up),
                  pl.BlockSpec((1, 1, D_MODEL), b_down_map)],
        out_specs=pl.BlockSpec(memory_space=pl.ANY),
        scratch_shapes=[pltpu.VMEM((EXP_ROWS * ROW_CHUNKS, LANES), F32),
                        pltpu.VMEM((EXP_ROWS, D_MODEL), BF16),
                        pltpu.VMEM((EXP_ROWS, D_MODEL), F32),
                        pltpu.VMEM((2, EXP_CHUNK * ROW_CHUNKS, LANES), F32),
                        pltpu.VMEM((2, D_MODEL, EXP_TF), BF16),
                        pltpu.VMEM((2, D_MODEL, EXP_TF), BF16),
                        pltpu.VMEM((2, EXP_TF, D_MODEL), BF16),
                        pltpu.SemaphoreType.DMA(()),
                        pltpu.SemaphoreType.DMA((2,))])
    return pl.pallas_call(
        _expert_kernel,
        grid_spec=grid_spec,
        out_shape=jax.ShapeDtypeStruct((n_tiles * EXP_ROWS * ROW_CHUNKS, LANES), F32),
        compiler_params=_cparams(("arbitrary",)),
        name="experts")(tile_e, tile_rows, tile_base, row_tok, x1c, w_gu, w_gu, w_down, b_gu, b_gu, b_down)


def _combine_kernel(dest_ref, y_hbm, x_ref, tg_ref, g_ref, be_ref, o_ref, buf_ref, sem):
    tm = o_ref.shape[0]
    i = pl.program_id(0)
    n = pl.num_programs(0)

    def row_copy(src_row, slot, k, t):
        return pltpu.make_async_copy(
            y_hbm.at[pl.ds(pl.multiple_of(src_row * ROW_CHUNKS, ROW_CHUNKS), ROW_CHUNKS), :],
            buf_ref.at[slot, k, pl.ds(pl.multiple_of(t * ROW_CHUNKS, ROW_CHUNKS), ROW_CHUNKS), :],
            sem.at[slot])

    def start_tile(tile, slot):
        base = tile * tm * TOP_K

        def body(t2, carry):
            for u in range(2):
                t = t2 * 2 + u
                for k in range(TOP_K):
                    row_copy(dest_ref[base + t * TOP_K + k], slot, k, t).start()
            return carry

        lax.fori_loop(0, tm // 2, body, 0)

    def wait_tile(slot):
        for k in range(TOP_K):
            pltpu.make_async_copy(y_hbm.at[pl.ds(0, tm * ROW_CHUNKS), :], buf_ref.at[slot, k], sem.at[slot]).wait()

    @pl.when(i == 0)
    def _():
        start_tile(0, 0)

    @pl.when(i + 1 < n)
    def _():
        start_tile(jnp.minimum(i + 1, n - 1), (i + 1) % 2)

    slot = i % 2
    wait_tile(slot)
    tg = tg_ref[...]
    ff = jnp.zeros((tm, D_MODEL), F32)
    for k in range(TOP_K):
        ff = ff + tg[:, k:k + 1] * _from_slabs(buf_ref[slot, k])
    o_ref[...] = _layer_norm(DN_ALPHA * x_ref[...] + ff, g_ref[...], be_ref[...])


def _combine_ln(dest_flat, y_buf, x1, tg, g, be):
    t = tg.shape[0]
    tm = CMB_TM
    grid_spec = pltpu.PrefetchScalarGridSpec(
        num_scalar_prefetch=1,
        grid=(t // tm,),
        in_specs=[pl.BlockSpec(memory_space=pl.ANY),
                  pl.BlockSpec((tm, D_MODEL), lambda i, d: (i, 0)),
                  pl.BlockSpec((tm, LANES), lambda i, d: (i, 0)),
                  pl.BlockSpec((1, D_MODEL), lambda i, d: (0, 0)),
                  pl.BlockSpec((1, D_MODEL), lambda i, d: (0, 0))],
        out_specs=pl.BlockSpec((tm, D_MODEL), lambda i, d: (i, 0)),
        scratch_shapes=[pltpu.VMEM((2, TOP_K, tm * ROW_CHUNKS, LANES), F32), pltpu.SemaphoreType.DMA((2,))])
    return pl.pallas_call(
        _combine_kernel,
        grid_spec=grid_spec,
        out_shape=jax.ShapeDtypeStruct((t, D_MODEL), F32),
        compiler_params=_cparams(("arbitrary",)),
        name="combine_ln2")(dest_flat, y_buf, x1, tg, g, be)


def _dft_kernel(cd_ref, sd_ref, ca_ref, sa_ref, c_ref, s_ref):
    cd, sd = cd_ref[0], sd_ref[0]
    ca, sa = ca_ref[0], sa_ref[0]
    c_ref[0] = (cd * ca - sd * sa).astype(c_ref.dtype)
    s_ref[0] = (sd * ca + cd * sa).astype(s_ref.dtype)


def _dft_tables(seq):
    n = 2 * seq
    half = seq // 2
    nblk = half // DFT_TB
    idx = jnp.arange(half, dtype=I32)
    off = jnp.arange(DFT_TB, dtype=I32)
    start = jnp.arange(nblk, dtype=I32) * DFT_TB

    def angle(prod):
        return (prod % n).astype(F32) * (2.0 * math.pi / n)

    ang_d = jnp.stack([angle(off[:, None] * (2 * idx)[None, :]),
                       angle(off[:, None] * (2 * idx + 1)[None, :]),
                       angle((2 * off + 1)[:, None] * idx[None, :])])
    ang_a = jnp.stack([angle(start[:, None] * (2 * idx)[None, :]),
                       angle(start[:, None] * (2 * idx + 1)[None, :]),
                       angle((2 * start)[:, None] * idx[None, :])]).reshape(3 * nblk, 1, half)
    small = pl.BlockSpec((1, DFT_TB, half), lambda k, a: (k, 0, 0))
    base = pl.BlockSpec((1, 1, half), lambda k, a: (k * nblk + a, 0, 0))
    out = pl.BlockSpec((1, DFT_TB, half), lambda k, a: (k, a, 0))
    sds = jax.ShapeDtypeStruct((3, half, half), BF16)
    return pl.pallas_call(
        _dft_kernel, grid=(3, nblk), in_specs=[small, small, base, base], out_specs=[out, out],
        out_shape=[sds, sds], compiler_params=_cparams(("arbitrary", "arbitrary")),
        name="dft_tables")(jnp.cos(ang_d), jnp.sin(ang_d), jnp.cos(ang_a), jnp.sin(ang_a))


def _filter_features(seq):
    t = jnp.linspace(0.0, 1.0, seq, dtype=F32)[:, None]
    bands = (HY_EMB - 1) // 2
    fb = jnp.linspace(1e-4, bands - 1, bands, dtype=F32)[None]
    w = 2.0 * math.pi * jnp.arange(seq, dtype=F32)[:, None] / seq
    z = jnp.concatenate([t, jnp.cos(fb * w), -jnp.sin(fb * w)], -1)
    z = jnp.concatenate([z[0::2], z[1::2]], axis=0)
    return jnp.pad(z, ((0, 0), (0, LANES - HY_EMB)))


def _mixer(x, w_in, b_in, hy_conv_w, hy_conv_b, hy_filt_w1, hy_filt_b1, hy_filt_w2, hy_filt_b2,
           hy_filt_w3, hy_filt_freq, hy_skip, hy_norm_w, ml_conv_w, ml_conv_b, ml_norm_w):
    bsz, seq, _ = x.shape
    t = bsz * seq
    x2d = x.reshape(t, D_MODEL)
    n_main = w_in.shape[1] - N_GATE_COLS
    w_t = jnp.swapaxes(w_in, 0, 1)
    proj = _in_proj(x2d, w_t[:n_main].astype(BF16), b_in[None, :n_main])
    proj3 = proj.reshape(bsz, seq, n_main)
    wg = jnp.pad(w_t[n_main:], ((0, LANES - N_GATE_COLS), (0, 0)))
    bg = jnp.pad(b_in[None, n_main:], ((0, 0), (0, LANES - N_GATE_COLS)))
    gates = _gate_proj(x2d, wg, bg)[:, :N_GATE_COLS]
    g5 = gates.reshape(bsz, seq, 4, ML_HEADS)
    grow = g5.transpose(0, 3, 2, 1).reshape(bsz, ML_HEADS, 4, seq // CHUNK, CHUNK)
    gtr = grow.transpose(0, 1, 2, 4, 3)

    cmat, smat = _dft_tables(seq)
    zpad = _filter_features(seq)
    w1pad = jnp.pad(hy_filt_w1, ((0, LANES - HY_EMB), (0, 0)))
    deltas = jnp.abs(jnp.linspace(math.log(HY_DECAY_TARGET) / HY_SLOW_PCT,
                                  math.log(HY_DECAY_TARGET) / HY_FAST_PCT, D_HY, dtype=F32))[None]
    kr, ki, km = _hyena_filters(zpad, w1pad, hy_filt_b1[None], hy_filt_w2, hy_filt_b2[None],
                                hy_filt_freq, hy_filt_w3, deltas, cmat, smat)
    y_hy = _hyena(proj3, hy_conv_w, hy_conv_b[None], cmat, smat, kr, ki, km, hy_skip, hy_norm_w[None])
    y_ml = _mlstm(proj3, ml_conv_w, ml_conv_b[None], grow, gtr, ml_norm_w[None])
    return y_hy.reshape(t, D_HY), y_ml.reshape(t, D_ML), x2d


def _moe_tables(top_i, slot, counts):
    t = top_i.shape[0]
    n_tiles = N_EXPERTS + (t * TOP_K) // EXP_ROWS
    ntile = (counts + EXP_ROWS - 1) // EXP_ROWS
    ends = jnp.cumsum(ntile)
    starts = ends - ntile
    total = ends[-1]
    s_idx = jnp.arange(n_tiles, dtype=I32)
    valid = s_idx < total
    s_eff = jnp.where(valid, s_idx, jnp.maximum(total - 1, 0))
    tile_e = jnp.minimum(jnp.sum((s_eff[:, None] >= ends[None, :]).astype(I32), axis=1), N_EXPERTS - 1)
    local = s_eff - starts[tile_e]
    tile_rows = jnp.where(valid, jnp.clip(counts[tile_e] - local * EXP_ROWS, 0, EXP_ROWS), 0).astype(I32)
    tok = jnp.arange(t, dtype=I32)[:, None]
    row_tok = jnp.pad(jnp.sort((top_i * t + tok).reshape(-1)) % t, (0, EXP_ROWS))
    first = jnp.cumsum(counts) - counts
    tile_base = (first[tile_e] + local * EXP_ROWS).astype(I32)
    onehot = top_i[:, :, None] == jnp.arange(N_EXPERTS, dtype=I32)
    dest = jnp.sum(jnp.where(onehot, starts * EXP_ROWS, 0), axis=-1) + slot
    return tile_e.astype(I32), tile_rows, tile_base, row_tok.astype(I32), dest.astype(I32).reshape(-1)


def kernel(x, w_in, b_in, hy_conv_w, hy_conv_b, hy_filt_w1, hy_filt_b1, hy_filt_w2, hy_filt_b2, hy_filt_w3, hy_filt_freq, hy_skip, hy_norm_w, ml_conv_w, ml_conv_b, ml_norm_w, w_out, b_out, ln1_g, ln1_b, router_w, router_b, w_gu, b_gu, w_down, b_down, ln2_g, ln2_b):
    bsz, seq, _ = x.shape
    l = 0
    y_hy, y_ml, x2d = _mixer(x, w_in[l], b_in[l], hy_conv_w[l], hy_conv_b[l], hy_filt_w1[l], hy_filt_b1[l],
                             hy_filt_w2[l], hy_filt_b2[l], hy_filt_w3[l], hy_filt_freq[l], hy_skip[l],
                             hy_norm_w[l], ml_conv_w[l], ml_conv_b[l], ml_norm_w[l])
    x1, x1c = _out_proj_ln(y_hy, y_ml, x2d, w_out[l].astype(BF16), b_out[l][None], ln1_g[l][None],
                           ln1_b[l][None])
    rw = jnp.pad(router_w[l], ((0, 0), (0, LANES - N_EXPERTS)))
    rb = jnp.pad(router_b[l][None], ((0, 0), (0, LANES - N_EXPERTS)), constant_values=-1e30)
    top_i, top_g, slot, cnt = _router(x1, rw, rb)
    counts = cnt[0, :N_EXPERTS].astype(I32)
    tile_e, tile_rows, tile_base, row_tok, dest = _moe_tables(top_i[:, :TOP_K], slot[:, :TOP_K], counts)
    y_buf = _experts(tile_e, tile_rows, tile_base, row_tok, x1c, w_gu[l], b_gu[l][:, None, :], w_down[l],
                     b_down[l][:, None, :])
    out = _combine_ln(dest, y_buf, x1, top_g, ln2_g[l][None], ln2_b[l][None])
    return out.reshape(bsz, seq, D_MODEL)
```

```python
import functools
import math

import jax
import jax.numpy as jnp
from jax import lax
from jax.experimental import pallas as pl
from jax.experimental.pallas import tpu as pltpu

F32 = jnp.float32
BF16 = jnp.bfloat16
I32 = jnp.int32
HP = lax.Precision.HIGHEST

D_MODEL = 2048
D_HY = 1024
D_ML = 1024
ML_HEADS = 8
HEAD_DIM = 128
CHUNK = 128
N_GATE_COLS = 32
HY_EMB = 33
N_EXPERTS = 32
TOP_K = 4
D_FF = 2048
SWIGLU_LIMIT = 7.0
SWIGLU_ALPHA = 1.702
LN_EPS = 1e-5
DN_ALPHA = 2.0 ** 0.25
HY_DECAY_TARGET = 1e-2
HY_FAST_PCT = 0.3
HY_SLOW_PCT = 1.5

LANES = 128
ROW_CHUNKS = D_MODEL // LANES
VMEM_LIMIT = 60 * 1024 * 1024

PROJ_TM = 2048
PROJ_TN = 512
PROJ_GATE_ROWS = 256
HY_CW = 256
HY_FC = 512
OUT_TM = 512
RT_TM = 256
EXP_ROWS = 1152
EXP_M_SIZES = (1024, 1088, 1152)
EXP_CHUNK = 128
EXP_TF = 256
GATHER_UNROLL = 8
CMB_TM = 128
DFT_TB = 128


def _cparams(sem):
    return pltpu.CompilerParams(dimension_semantics=sem, vmem_limit_bytes=VMEM_LIMIT)


def _split(a):
    hi = a.astype(BF16)
    return hi, (a - hi.astype(F32)).astype(BF16)


def _dot3(a, b, dims):
    a_hi, a_lo = _split(a)
    b_hi, b_lo = _split(b)
    mm = functools.partial(lax.dot_general, dimension_numbers=dims, preferred_element_type=F32)
    return mm(a_hi, b_hi) + (mm(a_hi, b_lo) + mm(a_lo, b_hi))


NN = (((1,), (0,)), ((), ()))
NT = (((1,), (1,)), ((), ()))


def _const_spec(shape, index_map):
    return pl.BlockSpec(shape, index_map, pipeline_mode=pl.Buffered(1))


def _proj_kernel(x_ref, w_ref, b_ref, wg_ref, bg_ref, o_ref, og_ref, xb_ref):
    @pl.when(pl.program_id(1) == 0)
    def _():
        xb_ref[...] = x_ref[...].astype(BF16)
        wg_hi, wg_lo = _split(wg_ref[...])
        mm = functools.partial(lax.dot_general, dimension_numbers=NT, preferred_element_type=F32)
        for r in range(x_ref.shape[0] // PROJ_GATE_ROWS):
            rows = slice(r * PROJ_GATE_ROWS, (r + 1) * PROJ_GATE_ROWS)
            x_hi = xb_ref[rows, :]
            x_lo = (x_ref[rows, :] - x_hi.astype(F32)).astype(BF16)
            og_ref[rows, :] = mm(x_hi, wg_hi) + (mm(x_hi, wg_lo) + mm(x_lo, wg_hi)) + bg_ref[...]

    acc = lax.dot_general(xb_ref[...], w_ref[...], NT, preferred_element_type=F32)
    o_ref[...] = (acc + b_ref[...]).astype(o_ref.dtype)


def _in_proj(x2d, wt_bf, b_row, wt_gate, b_gate):
    m, k = x2d.shape
    n = wt_bf.shape[0]
    return pl.pallas_call(
        _proj_kernel,
        grid=(m // PROJ_TM, n // PROJ_TN),
        in_specs=[pl.BlockSpec((PROJ_TM, k), lambda i, j: (i, 0)),
                  pl.BlockSpec((PROJ_TN, k), lambda i, j: (j, 0)),
                  pl.BlockSpec((1, PROJ_TN), lambda i, j: (0, j)),
                  _const_spec((LANES, k), lambda i, j: (0, 0)),
                  _const_spec((1, LANES), lambda i, j: (0, 0))],
        out_specs=[pl.BlockSpec((PROJ_TM, PROJ_TN), lambda i, j: (i, j)),
                   pl.BlockSpec((PROJ_TM, LANES), lambda i, j: (i, 0))],
        out_shape=[jax.ShapeDtypeStruct((m, n), BF16), jax.ShapeDtypeStruct((m, LANES), F32)],
        scratch_shapes=[pltpu.VMEM((PROJ_TM, k), BF16)],
        compiler_params=_cparams(("arbitrary", "arbitrary")),
        name="in_proj")(x2d, wt_bf, b_row, wt_gate, b_gate)


def _filter_kernel(z_ref, w1_ref, b1_ref, w2_ref, b2_ref, fq_ref, w3f_ref, w3b_ref, dl_ref,
                   c_ref, s_ref, kr_ref, ki_ref, km_ref, h_ref):
    seq = z_ref.shape[0]
    inv_n = 1.0 / (2 * seq)
    z = z_ref[...]

    @pl.when(jnp.logical_and(pl.program_id(0) == 0, pl.program_id(1) == 0))
    def _():
        h1 = jnp.sin(fq_ref[0:1, :] * (jnp.dot(z, w1_ref[...], precision=HP, preferred_element_type=F32)
                                       + b1_ref[...]))
        h_ref[...] = jnp.sin(fq_ref[1:2, :] * (jnp.dot(h1, w2_ref[...], precision=HP, preferred_element_type=F32)
                                               + b2_ref[...]))

    half = seq // 2
    h = h_ref[...]
    win = jnp.exp(-z[:, 0:1] * dl_ref[...])
    fwd = _dot3(h, w3f_ref[...], NN) * win
    bwd = _dot3(h, w3b_ref[...], NN) * win
    row = lax.broadcasted_iota(I32, fwd.shape, 0)
    bwd = jnp.where(row == 0, 0.0, bwd)
    inv = 1.0 / jnp.sum(jnp.abs(fwd) + jnp.abs(bwd), axis=0, keepdims=True)
    ks = ((fwd + bwd) * inv)
    kd = ((fwd - bwd) * inv)
    ksb = ks.astype(BF16)
    kdb = kd.astype(BF16)
    ec = jnp.dot(c_ref[0], ksb[:half], preferred_element_type=F32)
    oc = jnp.dot(c_ref[1], ksb[half:], preferred_element_type=F32)
    es = jnp.dot(s_ref[0], kdb[:half], preferred_element_type=F32)
    os_ = jnp.dot(s_ref[1], kdb[half:], preferred_element_type=F32)
    rowh = lax.broadcasted_iota(I32, ec.shape, 0)
    wf = jnp.where(rowh == 0, inv_n, 2.0 * inv_n)
    kr_ref[0, :half, :] = (ec + oc) * wf
    kr_ref[0, half:, :] = (ec - oc) * wf
    ki_ref[0, :half, :] = -(es + os_) * wf
    ki_ref[0, half:, :] = (es - os_) * wf
    alt = jnp.where((rowh & 1) == 0, 1.0, -1.0)
    km_ref[0, 0:1, :] = jnp.sum(ks[:half] * alt, axis=0, keepdims=True) * (2.0 * inv_n)
    km_ref[0, 1:2, :] = -jnp.sum(kd[half:] * alt, axis=0, keepdims=True) * (2.0 * inv_n)


def _hyena_filters(zpad, w1pad, b1, w2, b2, freq, w3, deltas, cmat, smat):
    seq = zpad.shape[0]
    nb = D_HY // HY_CW
    hid = w2.shape[0]
    full = lambda shape: pl.BlockSpec(shape, lambda o, c: (0,) * len(shape))
    out_sds = jax.ShapeDtypeStruct((2, seq, D_HY), F32)
    return pl.pallas_call(
        _filter_kernel,
        grid=(2, nb),
        in_specs=[full(zpad.shape), full(w1pad.shape), full(b1.shape), full(w2.shape), full(b2.shape),
                  full(freq.shape),
                  pl.BlockSpec((hid, HY_CW), lambda o, c: (0, o * 2 * nb + c)),
                  pl.BlockSpec((hid, HY_CW), lambda o, c: (0, o * 2 * nb + nb + c)),
                  pl.BlockSpec((1, HY_CW), lambda o, c: (0, c)),
                  _const_spec(cmat.shape, lambda o, c: (0, 0, 0)),
                  _const_spec(smat.shape, lambda o, c: (0, 0, 0))],
        out_specs=[pl.BlockSpec((1, seq, HY_CW), lambda o, c: (o, 0, c)),
                   pl.BlockSpec((1, seq, HY_CW), lambda o, c: (o, 0, c)),
                   pl.BlockSpec((1, 2, HY_CW), lambda o, c: (o, 0, c))],
        out_shape=[out_sds, out_sds, jax.ShapeDtypeStruct((2, 2, D_HY), F32)],
        scratch_shapes=[pltpu.VMEM((seq, hid), F32)],
        compiler_params=_cparams(("arbitrary", "arbitrary")),
        name="hyena_filters")(zpad, w1pad, b1, w2, b2, freq, w3, w3, deltas, cmat, smat)


def _short_conv(u, w_ref, b_ref, row, seq):
    prev = jnp.where(row == 0, 0.0, pltpu.roll(u, 1, 0))
    nxt = jnp.where(row == seq - 1, 0.0, pltpu.roll(u, seq - 1, 0))
    return w_ref[0:1, :] * prev + w_ref[1:2, :] * u + w_ref[2:3, :] * nxt + b_ref[...]


def _hyena_kernel(uv_ref, u1_ref, u2_ref, wv_ref, w1_ref, w2_ref, bv_ref, b1_ref, b2_ref,
                  c_ref, s_ref, kr_ref, ki_ref, km_ref, skip_ref, nw_ref, o_ref,
                  a_ref, b_ref, t_ref, zb_ref, s1_ref, s2_ref, s3_ref, s4_ref, md_ref):
    seq = uv_ref.shape[1]
    cw = uv_ref.shape[2]
    half = seq // 2
    nblk = half // HY_FC
    row = lax.broadcasted_iota(I32, (seq, LANES), 0)
    alt_half = jnp.where((lax.broadcasted_iota(I32, (half, cw), 0) & 1) == 0, 1.0, -1.0)
    alt_blk = jnp.where((lax.broadcasted_iota(I32, (HY_FC, cw), 0) & 1) == 0, 1.0, -1.0)
    groups = [slice(g * LANES, (g + 1) * LANES) for g in range(cw // LANES)]

    def conv_to(dst_ref, u_ref, w_ref, bias_ref):
        for gi, gs in enumerate(groups):
            t_ref[gi] = _short_conv(u_ref[0, :, gs].astype(F32), w_ref.at[:, gs], bias_ref.at[:, gs], row, seq)
            dst_ref[0:half, gs] = t_ref[gi, pl.ds(0, half, stride=2), :]
            dst_ref[half:seq, gs] = t_ref[gi, pl.ds(1, half, stride=2), :]

    def spectrum(zin_ref, o):
        z = zin_ref[...]
        zb_ref[...] = z.astype(BF16)
        am = jnp.sum(z[:half] * alt_half, axis=0, keepdims=True)
        bm = jnp.sum(z[half:] * alt_half, axis=0, keepdims=True)
        krm, kim = km_ref[o, 0:1, :], km_ref[o, 1:2, :]
        md_ref[0:1, :] = am * krm + bm * kim
        md_ref[1:2, :] = am * kim - bm * krm
        ze = zb_ref[0:half, :]
        zo = zb_ref[half:seq, :]
        for gb in range(nblk):
            lo = slice(gb * HY_FC, (gb + 1) * HY_FC)
            hi = slice(half + gb * HY_FC, half + (gb + 1) * HY_FC)
            ec = jnp.dot(c_ref[0, lo, :], ze, preferred_element_type=F32)
            oc = jnp.dot(c_ref[1, lo, :], zo, preferred_element_type=F32)
            es = jnp.dot(s_ref[0, lo, :], ze, preferred_element_type=F32)
            os_ = jnp.dot(s_ref[1, lo, :], zo, preferred_element_type=F32)
            a_lo, b_lo = ec + oc, es + os_
            a_hi, b_hi = ec - oc, os_ - es
            krl, kil = kr_ref[o, lo, :], ki_ref[o, lo, :]
            krh, kih = kr_ref[o, hi, :], ki_ref[o, hi, :]
            pr = a_lo * krl + b_lo * kil
            pi = a_lo * kil - b_lo * krl
            qr = a_hi * krh + b_hi * kih
            qi = a_hi * kih - b_hi * krh
            s1_ref[lo, :] = (pr + qr).astype(BF16)
            s2_ref[lo, :] = (pi - qi).astype(BF16)
            s3_ref[lo, :] = (pr - qr).astype(BF16)
            s4_ref[lo, :] = (pi + qi).astype(BF16)

    def conv_rows(ub, parity):
        us = slice(ub * HY_FC, (ub + 1) * HY_FC)
        if parity == 0:
            y = jnp.dot(c_ref[0, us, :], s1_ref[...], preferred_element_type=F32)
            y = y - jnp.dot(s_ref[0, us, :], s2_ref[...], preferred_element_type=F32)
            y = y + md_ref[0:1, :] * alt_blk
        else:
            y = jnp.dot(c_ref[2, us, :], s3_ref[...], preferred_element_type=F32)
            y = y - jnp.dot(s_ref[2, us, :], s4_ref[...], preferred_element_type=F32)
            y = y - md_ref[1:2, :] * alt_blk
        return slice(parity * half + ub * HY_FC, parity * half + (ub + 1) * HY_FC), y

    blocks = [(ub, parity) for parity in range(2) for ub in range(nblk)]
    conv_to(a_ref, uv_ref, wv_ref, bv_ref)
    conv_to(b_ref, u1_ref, w1_ref, b1_ref)
    spectrum(a_ref, 0)
    for ub, parity in blocks:
        rows, y = conv_rows(ub, parity)
        b_ref[rows, :] = b_ref[rows, :] * (y + skip_ref[0:1, :] * a_ref[rows, :])
    conv_to(a_ref, u2_ref, w2_ref, b2_ref)
    spectrum(b_ref, 1)
    for ub, parity in blocks:
        rows, y = conv_rows(ub, parity)
        z = a_ref[rows, :] * (y + skip_ref[1:2, :] * b_ref[rows, :])
        for gi, gs in enumerate(groups):
            zg = z[:, gs]
            mu = jnp.mean(zg, axis=-1, keepdims=True)
            zc = zg - mu
            var = jnp.mean(zc * zc, axis=-1, keepdims=True)
            t_ref[gi, pl.ds(2 * ub * HY_FC + parity, HY_FC, stride=2), :] = (
                zc * lax.rsqrt(var + LN_EPS) * nw_ref[:, gs])
    for gi, gs in enumerate(groups):
        o_ref[0, :, gs] = t_ref[gi].astype(o_ref.dtype)


def _hyena(proj3, conv_w, conv_b, cmat, smat, kr, ki, km, skip, norm_w):
    bsz, seq, _ = proj3.shape
    half = seq // 2
    nb = D_HY // HY_CW
    u_spec = lambda off: pl.BlockSpec((1, seq, HY_CW), lambda c, b: (b, 0, off + c))
    w_spec = lambda off: pl.BlockSpec((3, HY_CW), lambda c, b: (0, off + c))
    b_spec = lambda off: pl.BlockSpec((1, HY_CW), lambda c, b: (0, off + c))
    return pl.pallas_call(
        _hyena_kernel,
        grid=(nb, bsz),
        in_specs=[u_spec(0), u_spec(nb), u_spec(2 * nb),
                  w_spec(0), w_spec(nb), w_spec(2 * nb),
                  b_spec(0), b_spec(nb), b_spec(2 * nb),
                  _const_spec(cmat.shape, lambda c, b: (0, 0, 0)),
                  _const_spec(smat.shape, lambda c, b: (0, 0, 0)),
                  _const_spec((2, seq, HY_CW), lambda c, b: (0, 0, c)),
                  _const_spec((2, seq, HY_CW), lambda c, b: (0, 0, c)),
                  pl.BlockSpec((2, 2, HY_CW), lambda c, b: (0, 0, c)),
                  pl.BlockSpec((2, HY_CW), lambda c, b: (0, c)),
                  pl.BlockSpec((1, HY_CW), lambda c, b: (0, c))],
        out_specs=pl.BlockSpec((1, seq, HY_CW), lambda c, b: (b, 0, c)),
        out_shape=jax.ShapeDtypeStruct((bsz, seq, D_HY), BF16),
        scratch_shapes=[pltpu.VMEM((seq, HY_CW), F32), pltpu.VMEM((seq, HY_CW), F32),
                        pltpu.VMEM((HY_CW // LANES, seq, LANES), F32), pltpu.VMEM((seq, HY_CW), BF16),
                        pltpu.VMEM((half, HY_CW), BF16), pltpu.VMEM((half, HY_CW), BF16),
                        pltpu.VMEM((half, HY_CW), BF16), pltpu.VMEM((half, HY_CW), BF16),
                        pltpu.VMEM((2, HY_CW), F32)],
        compiler_params=_cparams(("arbitrary", "arbitrary")),
        name="hyena")(proj3, proj3, proj3, conv_w, conv_w, conv_w, conv_b, conv_b, conv_b,
                      cmat, smat, kr, ki, km, skip, norm_w)


def _mlstm_kernel(qp_ref, kp_ref, v_ref, og_ref, wq_ref, wk_ref, bq_ref, bk_ref, gr_ref, gt_ref,
                  nw_ref, o_ref, qb_ref, kb_ref, hacc_ref, cst_ref, nst_ref):
    seq = qp_ref.shape[1]
    d = qp_ref.shape[2]
    nchunk = seq // CHUNK
    row = lax.broadcasted_iota(I32, (seq, d), 0)
    q = _short_conv(qp_ref[0].astype(F32), wq_ref, bq_ref, row, seq)
    k = _short_conv(kp_ref[0].astype(F32), wk_ref, bk_ref, row, seq)
    qb_ref[...] = (q * jax.nn.sigmoid(q)).astype(BF16)
    kb_ref[...] = ((k * jax.nn.sigmoid(k)) * (d ** -0.5)).astype(BF16)

    ti = lax.broadcasted_iota(I32, (CHUNK, CHUNK), 0)
    si = lax.broadcasted_iota(I32, (CHUNK, CHUNK), 1)
    lower = ti >= si
    upper = ti <= si
    lower_f = lower.astype(F32)
    upper_f = upper.astype(F32)
    nt = (((1,), (1,)), ((), ()))
    tn = (((0,), (0,)), ((), ()))
    chunk_rows = [slice(c * CHUNK, (c + 1) * CHUNK) for c in range(nchunk)]

    for direction in range(2):
        f_idx, i_idx = 2 * direction + 1, 2 * direction
        mask = lower if direction == 0 else upper
        order = list(range(nchunk)) if direction == 0 else list(range(nchunk - 1, -1, -1))
        lf_r = jax.nn.log_sigmoid(gr_ref[0, 0, f_idx])
        b_r = jnp.dot(lf_r, upper_f if direction == 0 else lower_f, precision=HP, preferred_element_type=F32)
        rterm = b_r - gr_ref[0, 0, i_idx]
        b_last = jnp.sum(lf_r, axis=-1, keepdims=True)
        lf_c = jax.nn.log_sigmoid(gt_ref[0, 0, f_idx])
        b_c = jnp.dot(lower_f if direction == 0 else upper_f, lf_c, precision=HP, preferred_element_type=F32)
        i_c = gt_ref[0, 0, i_idx]

        bcol, gcol, gmax, rowmax, blast = [], [], [], [], []
        for c in range(nchunk):
            bc = jnp.broadcast_to(b_c[:, c:c + 1], (CHUNK, CHUNK))
            ic = jnp.broadcast_to(i_c[:, c:c + 1], (CHUNK, CHUNK))
            bl = jnp.broadcast_to(b_last[c:c + 1, :], (1, CHUNK))
            dmat = jnp.where(mask, bc - rterm[c:c + 1, :], -jnp.inf)
            g = bl - bc + ic
            bcol.append(bc)
            gcol.append(g)
            blast.append(bl)
            rowmax.append(jnp.max(dmat, axis=-1, keepdims=True))
            gmax.append(jnp.max(g, axis=0, keepdims=True))

        m = jnp.zeros((1, CHUNK), F32)
        m_in, m_out = [None] * nchunk, [None] * nchunk
        for c in order:
            m_in[c] = m
            m = jnp.maximum(blast[c] + m, gmax[c])
            m_out[c] = m

        cmat = jnp.zeros((d, d), F32)
        nvec = jnp.zeros((1, d), F32)
        for c in order:
            kc = kb_ref[chunk_rows[c], :]
            vc = v_ref[0, chunk_rows[c], :]
            cst_ref[c] = cmat.astype(BF16)
            nst_ref[c] = nvec
            wg = jnp.exp(gcol[c] - m_out[c])
            decay = jnp.exp(blast[c] + m_in[c] - m_out[c])
            upd = lax.dot_general((wg * vc.astype(F32)).astype(BF16), kc, tn, preferred_element_type=F32)
            cmat = decay * cmat + upd
            nvec = decay * nvec + jnp.sum(wg * kc.astype(F32), axis=0, keepdims=True)

        for c in range(nchunk):
            rs = chunk_rows[c]
            qc = qb_ref[rs, :]
            kc = kb_ref[rs, :]
            vc = v_ref[0, rs, :]
            dmat = jnp.where(mask, bcol[c] - rterm[c:c + 1, :], -jnp.inf)
            inter = bcol[c] + m_in[c]
            m_t = jnp.maximum(inter, rowmax[c])
            p = jnp.exp(dmat - m_t)
            inter_w = jnp.exp(inter - m_t)
            s = lax.dot_general(qc, kc, nt, preferred_element_type=F32) * p
            cq = lax.dot_general(qc, cst_ref[c], nt, preferred_element_type=F32)
            num = jnp.dot(s.astype(BF16), vc, preferred_element_type=F32) + inter_w * cq
            nq = jnp.sum(qc.astype(F32) * nst_ref[c], axis=-1, keepdims=True)
            den = jnp.sum(s, axis=-1, keepdims=True) + inter_w * nq
            h = num / jnp.maximum(jnp.abs(den), jnp.exp(-m_t))
            if direction == 0:
                hacc_ref[rs, :] = h
            else:
                hacc_ref[rs, :] += h

    h = hacc_ref[...]
    mu = jnp.mean(h, axis=-1, keepdims=True)
    hc = h - mu
    var = jnp.mean(hc * hc, axis=-1, keepdims=True)
    y = hc * lax.rsqrt(var + LN_EPS) * nw_ref[...] * jax.nn.sigmoid(og_ref[0].astype(F32))
    o_ref[0] = y.astype(o_ref.dtype)


def _mlstm(proj3, conv_w, conv_b, grow, gtr, norm_w):
    bsz, seq, _ = proj3.shape
    d = HEAD_DIM
    nchunk = seq // CHUNK
    hy_blocks = 3 * D_HY // d
    qoff, koff, voff, ooff = hy_blocks, hy_blocks + ML_HEADS, hy_blocks + 2 * ML_HEADS, hy_blocks + 3 * ML_HEADS
    p_spec = lambda off: pl.BlockSpec((1, seq, d), lambda b, h: (b, 0, off + h))
    return pl.pallas_call(
        _mlstm_kernel,
        grid=(bsz, ML_HEADS),
        in_specs=[p_spec(qoff), p_spec(koff), p_spec(voff), p_spec(ooff),
                  pl.BlockSpec((3, d), lambda b, h: (0, h)),
                  pl.BlockSpec((3, d), lambda b, h: (0, ML_HEADS + h)),
                  pl.BlockSpec((1, d), lambda b, h: (0, h)),
                  pl.BlockSpec((1, d), lambda b, h: (0, ML_HEADS + h)),
                  pl.BlockSpec((1, 1, 4, nchunk, CHUNK), lambda b, h: (b, h, 0, 0, 0)),
                  pl.BlockSpec((1, 1, 4, CHUNK, nchunk), lambda b, h: (b, h, 0, 0, 0)),
                  pl.BlockSpec((1, d), lambda b, h: (0, h))],
        out_specs=pl.BlockSpec((1, seq, d), lambda b, h: (b, 0, h)),
        out_shape=jax.ShapeDtypeStruct((bsz, seq, D_ML), BF16),
        scratch_shapes=[pltpu.VMEM((seq, d), BF16), pltpu.VMEM((seq, d), BF16),
                        pltpu.VMEM((seq, d), F32),
                        pltpu.VMEM((nchunk, d, d), BF16), pltpu.VMEM((nchunk, 1, d), F32)],
        compiler_params=_cparams(("arbitrary", "arbitrary")),
        name="mlstm")(proj3, proj3, proj3, proj3, conv_w, conv_w, conv_b, conv_b, grow, gtr, norm_w)


def _to_slabs(y):
    n = y.shape[0]
    parts = jnp.stack([y[:, c * LANES:(c + 1) * LANES] for c in range(ROW_CHUNKS)], axis=0)
    return pltpu.einshape("crl->rcl", parts).reshape(n * ROW_CHUNKS, LANES)


def _from_slabs(v):
    n = v.shape[0] // ROW_CHUNKS
    parts = pltpu.einshape("rcl->crl", v.reshape(n, ROW_CHUNKS, LANES))
    return jnp.concatenate([parts[c] for c in range(ROW_CHUNKS)], axis=-1)


def _layer_norm(u, g, b):
    mu = jnp.mean(u, axis=-1, keepdims=True)
    uc = u - mu
    var = jnp.mean(uc * uc, axis=-1, keepdims=True)
    return uc * lax.rsqrt(var + LN_EPS) * g + b


def _outproj_kernel(yh_ref, ym_ref, x_ref, wa_ref, wb_ref, b_ref, g_ref, be_ref, o_ref, oc_ref):
    mix = (jnp.dot(yh_ref[...], wa_ref[...], preferred_element_type=F32)
           + jnp.dot(ym_ref[...], wb_ref[...], preferred_element_type=F32) + b_ref[...])
    y = _layer_norm(DN_ALPHA * x_ref[...] + mix, g_ref[...], be_ref[...])
    o_ref[...] = y
    oc_ref[...] = _to_slabs(y)


def _out_proj_ln(y_hy, y_ml, x2d, w_out_bf, b_out, g, be):
    t = x2d.shape[0]
    tm = OUT_TM
    vec = lambda: pl.BlockSpec((1, D_MODEL), lambda i: (0, 0))
    return pl.pallas_call(
        _outproj_kernel,
        grid=(t // tm,),
        in_specs=[pl.BlockSpec((tm, D_HY), lambda i: (i, 0)),
                  pl.BlockSpec((tm, D_ML), lambda i: (i, 0)),
                  pl.BlockSpec((tm, D_MODEL), lambda i: (i, 0)),
                  _const_spec((D_HY, D_MODEL), lambda i: (0, 0)),
                  _const_spec((D_ML, D_MODEL), lambda i: (1, 0)),
                  vec(), vec(), vec()],
        out_specs=[pl.BlockSpec((tm, D_MODEL), lambda i: (i, 0)),
                   pl.BlockSpec((tm * ROW_CHUNKS, LANES), lambda i: (i, 0))],
        out_shape=[jax.ShapeDtypeStruct((t, D_MODEL), F32),
                   jax.ShapeDtypeStruct((t * ROW_CHUNKS, LANES), F32)],
        compiler_params=_cparams(("arbitrary",)),
        name="out_proj_ln1")(y_hy, y_ml, x2d, w_out_bf, w_out_bf, b_out, g, be)


def _router_kernel(x_ref, w_ref, b_ref, ti_ref, tg_ref, tp_ref, cnt_ref):
    tm = ti_ref.shape[0]

    @pl.when(pl.program_id(0) == 0)
    def _():
        cnt_ref[...] = jnp.zeros_like(cnt_ref)

    logits = _dot3(x_ref[...], w_ref[...], NN) + b_ref[...]
    lane = lax.broadcasted_iota(I32, (tm, LANES), 1)
    work = logits
    vals, idxs = [], []
    chosen = jnp.zeros((tm, LANES), F32)
    for _ in range(TOP_K):
        mx = jnp.max(work, axis=-1, keepdims=True)
        idx = jnp.min(jnp.where(work == mx, lane, LANES), axis=-1, keepdims=True)
        hit = lane == idx
        vals.append(mx)
        idxs.append(idx)
        chosen = jnp.where(hit, 1.0, chosen)
        work = jnp.where(hit, -jnp.inf, work)
    exps = [jnp.exp(v - vals[0]) for v in vals]
    den = exps[0] + exps[1] + exps[2] + exps[3]
    ri = lax.broadcasted_iota(I32, (tm, tm), 0)
    ci = lax.broadcasted_iota(I32, (tm, tm), 1)
    strict_lower = (ri > ci).astype(BF16)
    carry = cnt_ref[...]
    slot = carry + jnp.dot(strict_lower, chosen.astype(BF16), preferred_element_type=F32)
    ti = jnp.zeros((tm, LANES), I32)
    tg = jnp.zeros((tm, LANES), F32)
    tp = jnp.zeros((tm, LANES), F32)
    for k in range(TOP_K):
        sk = jnp.sum(jnp.where(lane == idxs[k], slot, 0.0), axis=-1, keepdims=True)
        ti = jnp.where(lane == k, idxs[k], ti)
        tg = jnp.where(lane == k, exps[k] / den, tg)
        tp = jnp.where(lane == k, sk, tp)
    ti_ref[...] = ti
    tg_ref[...] = tg
    tp_ref[...] = tp.astype(I32)
    cnt_ref[...] = carry + jnp.sum(chosen, axis=0, keepdims=True)


def _router(x1, w_pad, b_pad):
    t = x1.shape[0]
    tm = RT_TM
    o_spec = lambda: pl.BlockSpec((tm, LANES), lambda i: (i, 0))
    return pl.pallas_call(
        _router_kernel,
        grid=(t // tm,),
        in_specs=[pl.BlockSpec((tm, D_MODEL), lambda i: (i, 0)),
                  pl.BlockSpec((D_MODEL, LANES), lambda i: (0, 0)),
                  pl.BlockSpec((1, LANES), lambda i: (0, 0))],
        out_specs=[o_spec(), o_spec(), o_spec(), pl.BlockSpec((1, LANES), lambda i: (0, 0))],
        out_shape=[jax.ShapeDtypeStruct((t, LANES), I32), jax.ShapeDtypeStruct((t, LANES), F32),
                   jax.ShapeDtypeStruct((t, LANES), I32), jax.ShapeDtypeStruct((1, LANES), F32)],
        compiler_params=_cparams(("arbitrary",)),
        name="router")(x1, w_pad, b_pad)


def _expert_kernel(te_ref, tr_ref, tb_ref, rt_ref, x_hbm, wg_ref, wu_ref, wd_ref, bg_ref, bu_ref, bd_ref, y_hbm,
                   stage_ref, xb_ref, acc_ref, ring_ref, wgb_ref, wub_ref, wdb_ref, gsem, osem):
    g = pl.program_id(0)
    n_items = pl.num_programs(0) - 1
    nf = D_FF // EXP_TF
    n_tiles = n_items // nf
    item = jnp.maximum(g - 1, 0)
    s = item // nf
    j = item % nf
    rows = tr_ref[s]
    active = jnp.logical_and(g >= 1, rows > 0)
    cast_slot = g % 2
    use_slot = (g + 1) % 2
    slab = EXP_CHUNK * ROW_CHUNKS
    per_step = EXP_ROWS // nf

    def row_copy(tok, r):
        return pltpu.make_async_copy(
            x_hbm.at[pl.ds(pl.multiple_of(tok * ROW_CHUNKS, ROW_CHUNKS), ROW_CHUNKS), :],
            stage_ref.at[pl.ds(pl.multiple_of(r * ROW_CHUNKS, ROW_CHUNKS), ROW_CHUNKS), :], gsem)

    def wait_gather():
        pltpu.make_async_copy(x_hbm.at[pl.ds(0, EXP_ROWS * ROW_CHUNKS), :], stage_ref, gsem).wait()

    def cast_weights():
        wgb_ref[cast_slot] = wg_ref[0].astype(BF16)
        wub_ref[cast_slot] = wu_ref[0].astype(BF16)
        wdb_ref[cast_slot] = wd_ref[0].astype(BF16)

    @pl.when(g == 0)
    def _():
        stage_ref[...] = jnp.zeros_like(stage_ref)
        acc_ref[...] = jnp.zeros_like(acc_ref)
        base = tb_ref[0]

        def group(q, carry):
            for u in range(GATHER_UNROLL):
                r = q * GATHER_UNROLL + u
                row_copy(rt_ref[base + r], r).start()
            return carry

        lax.fori_loop(0, EXP_ROWS // GATHER_UNROLL, group, 0)

        cast_weights()

    @pl.when(active)
    def _():
        @pl.when(j == 0)
        def _():
            wait_gather()
            for i in range(EXP_ROWS // EXP_CHUNK):
                xb_ref[i * EXP_CHUNK:(i + 1) * EXP_CHUNK, :] = _from_slabs(
                    stage_ref[i * slab:(i + 1) * slab, :]).astype(BF16)

        def step_body(m):
            cast_weights()
            nxt_base = tb_ref[jnp.minimum(s + 1, n_tiles - 1)]
            for u in range(per_step):
                r = j * per_step + u
                row_copy(rt_ref[nxt_base + r], r).start()

            xb = xb_ref[0:m, :]
            gate = jnp.dot(xb, wgb_ref[use_slot], preferred_element_type=F32) + bg_ref[0]
            up = jnp.dot(xb, wub_ref[use_slot], preferred_element_type=F32) + bu_ref[0]
            gate = jnp.minimum(gate, SWIGLU_LIMIT)
            up = jnp.clip(up, -SWIGLU_LIMIT, SWIGLU_LIMIT)
            act = (up + 1.0) * (gate * jax.nn.sigmoid(SWIGLU_ALPHA * gate))
            part = jnp.dot(act.astype(BF16), wdb_ref[use_slot], preferred_element_type=F32)
            acc_ref[0:m, :] = jnp.where(j == 0, jnp.broadcast_to(bd_ref[0], part.shape), acc_ref[0:m, :]) + part

        lo = 0
        for m in EXP_M_SIZES:
            @pl.when(jnp.logical_and(rows > lo, rows <= m))
            def _(m=m):
                step_body(m)
            lo = m

        @pl.when(j == nf - 1)
        def _():
            nchunk = (rows + EXP_CHUNK - 1) // EXP_CHUNK

            def chunk_copy(i, slot):
                dst0 = pl.multiple_of((s * EXP_ROWS + i * EXP_CHUNK) * ROW_CHUNKS, slab)
                return pltpu.make_async_copy(ring_ref.at[slot], y_hbm.at[pl.ds(dst0, slab), :], osem.at[slot])

            def emit(i, carry):
                slot = i % 2

                @pl.when(i >= 2)
                def _():
                    chunk_copy(i - 2, slot).wait()

                r0 = pl.multiple_of(i * EXP_CHUNK, EXP_CHUNK)
                ring_ref[slot] = _to_slabs(acc_ref[pl.ds(r0, EXP_CHUNK), :])
                chunk_copy(i, slot).start()
                return carry

            lax.fori_loop(0, nchunk, emit, 0)
            for back in range(2):
                @pl.when(nchunk > back)
                def _():
                    last = nchunk - 1 - back
                    chunk_copy(last, last % 2).wait()

    @pl.when(g == n_items)
    def _():
        wait_gather()


def _experts(tile_e, tile_rows, tile_base, row_tok, used_tiles, x1c, w_gu, b_gu, w_down, b_down):
    n_tiles = tile_e.shape[0]
    nf = D_FF // EXP_TF
    n_items = n_tiles * nf

    def item_block(item, te, tr):
        s = item // nf
        return te[s], jnp.where(tr[s] > 0, item % nf, nf - 1)

    def cast_item(g, te, tr):
        return item_block(jnp.minimum(g, n_items - 1), te, tr)

    def use_item(g, te, tr):
        return item_block(jnp.maximum(g - 1, 0), te, tr)

    def w_gate(g, te, tr, tb, rt):
        e, j = cast_item(g, te, tr)
        return e, 0, j

    def w_up(g, te, tr, tb, rt):
        e, j = cast_item(g, te, tr)
        return e, 0, nf + j

    def w_down_map(g, te, tr, tb, rt):
        e, j = cast_item(g, te, tr)
        return e, j, 0

    def b_gate(g, te, tr, tb, rt):
        e, j = use_item(g, te, tr)
        return e, 0, j

    def b_up(g, te, tr, tb, rt):
        e, j = use_item(g, te, tr)
        return e, 0, nf + j

    def b_down_map(g, te, tr, tb, rt):
        e, _ = use_item(g, te, tr)
        return e, 0, 0

    grid_spec = pltpu.PrefetchScalarGridSpec(
        num_scalar_prefetch=4,
        grid=(used_tiles * nf + 1,),
        in_specs=[pl.BlockSpec(memory_space=pl.ANY),
                  pl.BlockSpec((1, D_MODEL, EXP_TF), w_gate),
                  pl.BlockSpec((1, D_MODEL, EXP_TF), w_up),
                  pl.BlockSpec((1, EXP_TF, D_MODEL), w_down_map),
                  pl.BlockSpec((1, 1, EXP_TF), b_gate),
                  pl.BlockSpec((1, 1, EXP_TF), b_up),
                  pl.BlockSpec((1, 1, D_MODEL), b_down_map)],
        out_specs=pl.BlockSpec(memory_space=pl.ANY),
        scratch_shapes=[pltpu.VMEM((EXP_ROWS * ROW_CHUNKS, LANES), F32),
                        pltpu.VMEM((EXP_ROWS, D_MODEL), BF16),
                        pltpu.VMEM((EXP_ROWS, D_MODEL), F32),
                        pltpu.VMEM((2, EXP_CHUNK * ROW_CHUNKS, LANES), F32),
                        pltpu.VMEM((2, D_MODEL, EXP_TF), BF16),
                        pltpu.VMEM((2, D_MODEL, EXP_TF), BF16),
                        pltpu.VMEM((2, EXP_TF, D_MODEL), BF16),
                        pltpu.SemaphoreType.DMA(()),
                        pltpu.SemaphoreType.DMA((2,))])
    return pl.pallas_call(
        _expert_kernel,
        grid_spec=grid_spec,
        out_shape=jax.ShapeDtypeStruct((n_tiles * EXP_ROWS * ROW_CHUNKS, LANES), F32),
        compiler_params=_cparams(("arbitrary",)),
        name="experts")(tile_e, tile_rows, tile_base, row_tok, x1c, w_gu, w_gu, w_down, b_gu, b_gu, b_down)


def _combine_kernel(dest_ref, y_hbm, x_ref, tg_ref, g_ref, be_ref, o_ref, buf_ref, sem):
    tm = o_ref.shape[0]
    i = pl.program_id(0)
    n = pl.num_programs(0)

    def row_copy(src_row, slot, k, t):
        return pltpu.make_async_copy(
            y_hbm.at[pl.ds(pl.multiple_of(src_row * ROW_CHUNKS, ROW_CHUNKS), ROW_CHUNKS), :],
            buf_ref.at[slot, k, pl.ds(pl.multiple_of(t * ROW_CHUNKS, ROW_CHUNKS), ROW_CHUNKS), :],
            sem.at[slot])

    def start_tile(tile, slot):
        base = tile * tm * TOP_K

        def body(t2, carry):
            for u in range(2):
                t = t2 * 2 + u
                for k in range(TOP_K):
                    row_copy(dest_ref[base + t * TOP_K + k], slot, k, t).start()
            return carry

        lax.fori_loop(0, tm // 2, body, 0)

    def wait_tile(slot):
        for k in range(TOP_K):
            pltpu.make_async_copy(y_hbm.at[pl.ds(0, tm * ROW_CHUNKS), :], buf_ref.at[slot, k], sem.at[slot]).wait()

    @pl.when(i == 0)
    def _():
        start_tile(0, 0)

    @pl.when(i + 1 < n)
    def _():
        start_tile(jnp.minimum(i + 1, n - 1), (i + 1) % 2)

    slot = i % 2
    wait_tile(slot)
    tg = tg_ref[...]
    ff = jnp.zeros((tm, D_MODEL), F32)
    for k in range(TOP_K):
        ff = ff + tg[:, k:k + 1] * _from_slabs(buf_ref[slot, k])
    o_ref[...] = _layer_norm(DN_ALPHA * x_ref[...] + ff, g_ref[...], be_ref[...])


def _combine_ln(dest_flat, y_buf, x1, tg, g, be):
    t = tg.shape[0]
    tm = CMB_TM
    grid_spec = pltpu.PrefetchScalarGridSpec(
        num_scalar_prefetch=1,
        grid=(t // tm,),
        in_specs=[pl.BlockSpec(memory_space=pl.ANY),
                  pl.BlockSpec((tm, D_MODEL), lambda i, d: (i, 0)),
                  pl.BlockSpec((tm, LANES), lambda i, d: (i, 0)),
                  pl.BlockSpec((1, D_MODEL), lambda i, d: (0, 0)),
                  pl.BlockSpec((1, D_MODEL), lambda i, d: (0, 0))],
        out_specs=pl.BlockSpec((tm, D_MODEL), lambda i, d: (i, 0)),
        scratch_shapes=[pltpu.VMEM((2, TOP_K, tm * ROW_CHUNKS, LANES), F32), pltpu.SemaphoreType.DMA((2,))])
    return pl.pallas_call(
        _combine_kernel,
        grid_spec=grid_spec,
        out_shape=jax.ShapeDtypeStruct((t, D_MODEL), F32),
        compiler_params=_cparams(("arbitrary",)),
        name="combine_ln2")(dest_flat, y_buf, x1, tg, g, be)


def _dft_kernel(cd_ref, sd_ref, ca_ref, sa_ref, c_ref, s_ref):
    cd, sd = cd_ref[0], sd_ref[0]
    ca, sa = ca_ref[0], sa_ref[0]
    c_ref[0] = (cd * ca - sd * sa).astype(c_ref.dtype)
    s_ref[0] = (sd * ca + cd * sa).astype(s_ref.dtype)


def _dft_tables(seq):
    n = 2 * seq
    half = seq // 2
    nblk = half // DFT_TB
    idx = jnp.arange(half, dtype=I32)
    off = jnp.arange(DFT_TB, dtype=I32)
    start = jnp.arange(nblk, dtype=I32) * DFT_TB

    def angle(prod):
        return (prod % n).astype(F32) * (2.0 * math.pi / n)

    ang_d = jnp.stack([angle(off[:, None] * (2 * idx)[None, :]),
                       angle(off[:, None] * (2 * idx + 1)[None, :]),
                       angle((2 * off + 1)[:, None] * idx[None, :])])
    ang_a = jnp.stack([angle(start[:, None] * (2 * idx)[None, :]),
                       angle(start[:, None] * (2 * idx + 1)[None, :]),
                       angle((2 * start)[:, None] * idx[None, :])]).reshape(3 * nblk, 1, half)
    small = pl.BlockSpec((1, DFT_TB, half), lambda k, a: (k, 0, 0))
    base = pl.BlockSpec((1, 1, half), lambda k, a: (k * nblk + a, 0, 0))
    out = pl.BlockSpec((1, DFT_TB, half), lambda k, a: (k, a, 0))
    sds = jax.ShapeDtypeStruct((3, half, half), BF16)
    return pl.pallas_call(
        _dft_kernel, grid=(3, nblk), in_specs=[small, small, base, base], out_specs=[out, out],
        out_shape=[sds, sds], compiler_params=_cparams(("arbitrary", "arbitrary")),
        name="dft_tables")(jnp.cos(ang_d), jnp.sin(ang_d), jnp.cos(ang_a), jnp.sin(ang_a))


def _filter_features(seq):
    t = jnp.linspace(0.0, 1.0, seq, dtype=F32)[:, None]
    bands = (HY_EMB - 1) // 2
    fb = jnp.linspace(1e-4, bands - 1, bands, dtype=F32)[None]
    w = 2.0 * math.pi * jnp.arange(seq, dtype=F32)[:, None] / seq
    z = jnp.concatenate([t, jnp.cos(fb * w), -jnp.sin(fb * w)], -1)
    z = jnp.concatenate([z[0::2], z[1::2]], axis=0)
    return jnp.pad(z, ((0, 0), (0, LANES - HY_EMB)))


def _mixer(x, w_in, b_in, hy_conv_w, hy_conv_b, hy_filt_w1, hy_filt_b1, hy_filt_w2, hy_filt_b2,
           hy_filt_w3, hy_filt_freq, hy_skip, hy_norm_w, ml_conv_w, ml_conv_b, ml_norm_w):
    bsz, seq, _ = x.shape
    t = bsz * seq
    x2d = x.reshape(t, D_MODEL)
    n_main = w_in.shape[1] - N_GATE_COLS
    w_t = jnp.swapaxes(w_in, 0, 1)
    wg = jnp.pad(w_t[n_main:], ((0, LANES - N_GATE_COLS), (0, 0)))
    bg = jnp.pad(b_in[None, n_main:], ((0, 0), (0, LANES - N_GATE_COLS)))
    proj, gates = _in_proj(x2d, w_t[:n_main].astype(BF16), b_in[None, :n_main], wg, bg)
    proj3 = proj.reshape(bsz, seq, n_main)
    gates = gates[:, :N_GATE_COLS]
    g5 = gates.reshape(bsz, seq, 4, ML_HEADS)
    grow = g5.transpose(0, 3, 2, 1).reshape(bsz, ML_HEADS, 4, seq // CHUNK, CHUNK)
    gtr = grow.transpose(0, 1, 2, 4, 3)

    cmat, smat = _dft_tables(seq)
    zpad = _filter_features(seq)
    w1pad = jnp.pad(hy_filt_w1, ((0, LANES - HY_EMB), (0, 0)))
    deltas = jnp.abs(jnp.linspace(math.log(HY_DECAY_TARGET) / HY_SLOW_PCT,
                                  math.log(HY_DECAY_TARGET) / HY_FAST_PCT, D_HY, dtype=F32))[None]
    kr, ki, km = _hyena_filters(zpad, w1pad, hy_filt_b1[None], hy_filt_w2, hy_filt_b2[None],
                                hy_filt_freq, hy_filt_w3, deltas, cmat, smat)
    y_hy = _hyena(proj3, hy_conv_w, hy_conv_b[None], cmat, smat, kr, ki, km, hy_skip, hy_norm_w[None])
    y_ml = _mlstm(proj3, ml_conv_w, ml_conv_b[None], grow, gtr, ml_norm_w[None])
    return y_hy.reshape(t, D_HY), y_ml.reshape(t, D_ML), x2d


def _moe_tables(top_i, slot, counts):
    t = top_i.shape[0]
    n_tiles = N_EXPERTS + (t * TOP_K) // EXP_ROWS
    ntile = (counts + EXP_ROWS - 1) // EXP_ROWS
    ends = jnp.cumsum(ntile)
    starts = ends - ntile
    total = ends[-1]
    s_idx = jnp.arange(n_tiles, dtype=I32)
    valid = s_idx < total
    s_eff = jnp.where(valid, s_idx, jnp.maximum(total - 1, 0))
    tile_e = jnp.minimum(jnp.sum((s_eff[:, None] >= ends[None, :]).astype(I32), axis=1), N_EXPERTS - 1)
    local = s_eff - starts[tile_e]
    tile_rows = jnp.where(valid, jnp.clip(counts[tile_e] - local * EXP_ROWS, 0, EXP_ROWS), 0).astype(I32)
    tok = jnp.arange(t, dtype=I32)[:, None]
    row_tok = jnp.pad(jnp.sort((top_i * t + tok).reshape(-1)) % t, (0, EXP_ROWS))
    first = jnp.cumsum(counts) - counts
    tile_base = (first[tile_e] + local * EXP_ROWS).astype(I32)
    onehot = top_i[:, :, None] == jnp.arange(N_EXPERTS, dtype=I32)
    dest = jnp.sum(jnp.where(onehot, starts * EXP_ROWS, 0), axis=-1) + slot
    return (tile_e.astype(I32), tile_rows, tile_base, row_tok.astype(I32), total.astype(I32),
            dest.astype(I32).reshape(-1))


def kernel(x, w_in, b_in, hy_conv_w, hy_conv_b, hy_filt_w1, hy_filt_b1, hy_filt_w2, hy_filt_b2, hy_filt_w3, hy_filt_freq, hy_skip, hy_norm_w, ml_conv_w, ml_conv_b, ml_norm_w, w_out, b_out, ln1_g, ln1_b, router_w, router_b, w_gu, b_gu, w_down, b_down, ln2_g, ln2_b):
    bsz, seq, _ = x.shape
    l = 0
    y_hy, y_ml, x2d = _mixer(x, w_in[l], b_in[l], hy_conv_w[l], hy_conv_b[l], hy_filt_w1[l], hy_filt_b1[l],
                             hy_filt_w2[l], hy_filt_b2[l], hy_filt_w3[l], hy_filt_freq[l], hy_skip[l],
                             hy_norm_w[l], ml_conv_w[l], ml_conv_b[l], ml_norm_w[l])
    x1, x1c = _out_proj_ln(y_hy, y_ml, x2d, w_out[l].astype(BF16), b_out[l][None], ln1_g[l][None],
                           ln1_b[l][None])
    rw = jnp.pad(router_w[l], ((0, 0), (0, LANES - N_EXPERTS)))
    rb = jnp.pad(router_b[l][None], ((0, 0), (0, LANES - N_EXPERTS)), constant_values=-1e30)
    top_i, top_g, slot, cnt = _router(x1, rw, rb)
    counts = cnt[0, :N_EXPERTS].astype(I32)
    tile_e, tile_rows, tile_base, row_tok, used, dest = _moe_tables(top_i[:, :TOP_K], slot[:, :TOP_K], counts)
    y_buf = _experts(tile_e, tile_rows, tile_base, row_tok, used, x1c, w_gu[l], b_gu[l][:, None, :], w_down[l],
                     b_down[l][:, None, :])
    out = _combine_ln(dest, y_buf, x1, top_g, ln2_g[l][None], ln2_b[l][None])
    return out.reshape(bsz, seq, D_MODEL)
```

```python
import functools
import math

import jax
import jax.numpy as jnp
from jax import lax
from jax.experimental import pallas as pl
from jax.experimental.pallas import tpu as pltpu

F32 = jnp.float32
BF16 = jnp.bfloat16
I32 = jnp.int32
U32 = jnp.uint32
HP = lax.Precision.HIGHEST

D_MODEL = 2048
D_HY = 1024
D_ML = 1024
ML_HEADS = 8
HEAD_DIM = 128
CHUNK = 128
N_GATE_COLS = 32
HY_EMB = 33
N_EXPERTS = 32
TOP_K = 4
D_FF = 2048
SWIGLU_LIMIT = 7.0
SWIGLU_ALPHA = 1.702
LN_EPS = 1e-5
DN_ALPHA = 2.0 ** 0.25
HY_DECAY_TARGET = 1e-2
HY_FAST_PCT = 0.3
HY_SLOW_PCT = 1.5

LANES = 128
ROW_CHUNKS = D_MODEL // (2 * LANES)
VMEM_LIMIT = 60 * 1024 * 1024

PROJ_TM = 2048
PROJ_TN = 512
PROJ_GATE_ROWS = 256
HY_CW = 256
HY_FC = 512
OUT_TM = 512
RT_TM = 256
EXP_ROWS = 1152
EXP_M_SIZES = (1024, 1088, 1152)
EXP_CHUNK = 128
EXP_TF = 256
GATHER_UNROLL = 8
CMB_TM = 128
DFT_TB = 128


def _cparams(sem):
    return pltpu.CompilerParams(dimension_semantics=sem, vmem_limit_bytes=VMEM_LIMIT)


def _split(a):
    hi = a.astype(BF16)
    return hi, (a - hi.astype(F32)).astype(BF16)


def _dot3(a, b, dims):
    a_hi, a_lo = _split(a)
    b_hi, b_lo = _split(b)
    mm = functools.partial(lax.dot_general, dimension_numbers=dims, preferred_element_type=F32)
    return mm(a_hi, b_hi) + (mm(a_hi, b_lo) + mm(a_lo, b_hi))


NN = (((1,), (0,)), ((), ()))
NT = (((1,), (1,)), ((), ()))


def _const_spec(shape, index_map):
    return pl.BlockSpec(shape, index_map, pipeline_mode=pl.Buffered(1))


def _proj_kernel(x_ref, w_ref, b_ref, wg_ref, bg_ref, o_ref, og_ref, xb_ref):
    @pl.when(pl.program_id(1) == 0)
    def _():
        xb_ref[...] = x_ref[...].astype(BF16)
        wg_hi, wg_lo = _split(wg_ref[...])
        mm = functools.partial(lax.dot_general, dimension_numbers=NT, preferred_element_type=F32)
        for r in range(x_ref.shape[0] // PROJ_GATE_ROWS):
            rows = slice(r * PROJ_GATE_ROWS, (r + 1) * PROJ_GATE_ROWS)
            x_hi = xb_ref[rows, :]
            x_lo = (x_ref[rows, :] - x_hi.astype(F32)).astype(BF16)
            og_ref[rows, :] = mm(x_hi, wg_hi) + (mm(x_hi, wg_lo) + mm(x_lo, wg_hi)) + bg_ref[...]

    acc = lax.dot_general(xb_ref[...], w_ref[...], NT, preferred_element_type=F32)
    o_ref[...] = (acc + b_ref[...]).astype(o_ref.dtype)


def _in_proj(x2d, wt_bf, b_row, wt_gate, b_gate):
    m, k = x2d.shape
    n = wt_bf.shape[0]
    return pl.pallas_call(
        _proj_kernel,
        grid=(m // PROJ_TM, n // PROJ_TN),
        in_specs=[pl.BlockSpec((PROJ_TM, k), lambda i, j: (i, 0)),
                  pl.BlockSpec((PROJ_TN, k), lambda i, j: (j, 0)),
                  pl.BlockSpec((1, PROJ_TN), lambda i, j: (0, j)),
                  _const_spec((LANES, k), lambda i, j: (0, 0)),
                  _const_spec((1, LANES), lambda i, j: (0, 0))],
        out_specs=[pl.BlockSpec((PROJ_TM, PROJ_TN), lambda i, j: (i, j)),
                   pl.BlockSpec((PROJ_TM, LANES), lambda i, j: (i, 0))],
        out_shape=[jax.ShapeDtypeStruct((m, n), BF16), jax.ShapeDtypeStruct((m, LANES), F32)],
        scratch_shapes=[pltpu.VMEM((PROJ_TM, k), BF16)],
        compiler_params=_cparams(("arbitrary", "arbitrary")),
        name="in_proj")(x2d, wt_bf, b_row, wt_gate, b_gate)


def _filter_kernel(z_ref, w1_ref, b1_ref, w2_ref, b2_ref, fq_ref, w3f_ref, w3b_ref, dl_ref,
                   c_ref, s_ref, kr_ref, ki_ref, km_ref, h_ref):
    seq = z_ref.shape[0]
    inv_n = 1.0 / (2 * seq)
    z = z_ref[...]

    @pl.when(jnp.logical_and(pl.program_id(0) == 0, pl.program_id(1) == 0))
    def _():
        h1 = jnp.sin(fq_ref[0:1, :] * (jnp.dot(z, w1_ref[...], precision=HP, preferred_element_type=F32)
                                       + b1_ref[...]))
        h_ref[...] = jnp.sin(fq_ref[1:2, :] * (jnp.dot(h1, w2_ref[...], precision=HP, preferred_element_type=F32)
                                               + b2_ref[...]))

    half = seq // 2
    h = h_ref[...]
    win = jnp.exp(-z[:, 0:1] * dl_ref[...])
    fwd = _dot3(h, w3f_ref[...], NN) * win
    bwd = _dot3(h, w3b_ref[...], NN) * win
    row = lax.broadcasted_iota(I32, fwd.shape, 0)
    bwd = jnp.where(row == 0, 0.0, bwd)
    inv = 1.0 / jnp.sum(jnp.abs(fwd) + jnp.abs(bwd), axis=0, keepdims=True)
    ks = ((fwd + bwd) * inv)
    kd = ((fwd - bwd) * inv)
    ksb = ks.astype(BF16)
    kdb = kd.astype(BF16)
    ec = jnp.dot(c_ref[0], ksb[:half], preferred_element_type=F32)
    oc = jnp.dot(c_ref[1], ksb[half:], preferred_element_type=F32)
    es = jnp.dot(s_ref[0], kdb[:half], preferred_element_type=F32)
    os_ = jnp.dot(s_ref[1], kdb[half:], preferred_element_type=F32)
    rowh = lax.broadcasted_iota(I32, ec.shape, 0)
    wf = jnp.where(rowh == 0, inv_n, 2.0 * inv_n)
    kr_ref[0, :half, :] = (ec + oc) * wf
    kr_ref[0, half:, :] = (ec - oc) * wf
    ki_ref[0, :half, :] = -(es + os_) * wf
    ki_ref[0, half:, :] = (es - os_) * wf
    alt = jnp.where((rowh & 1) == 0, 1.0, -1.0)
    km_ref[0, 0:1, :] = jnp.sum(ks[:half] * alt, axis=0, keepdims=True) * (2.0 * inv_n)
    km_ref[0, 1:2, :] = -jnp.sum(kd[half:] * alt, axis=0, keepdims=True) * (2.0 * inv_n)


def _hyena_filters(zpad, w1pad, b1, w2, b2, freq, w3, deltas, cmat, smat):
    seq = zpad.shape[0]
    nb = D_HY // HY_CW
    hid = w2.shape[0]
    full = lambda shape: pl.BlockSpec(shape, lambda o, c: (0,) * len(shape))
    out_sds = jax.ShapeDtypeStruct((2, seq, D_HY), F32)
    return pl.pallas_call(
        _filter_kernel,
        grid=(2, nb),
        in_specs=[full(zpad.shape), full(w1pad.shape), full(b1.shape), full(w2.shape), full(b2.shape),
                  full(freq.shape),
                  pl.BlockSpec((hid, HY_CW), lambda o, c: (0, o * 2 * nb + c)),
                  pl.BlockSpec((hid, HY_CW), lambda o, c: (0, o * 2 * nb + nb + c)),
                  pl.BlockSpec((1, HY_CW), lambda o, c: (0, c)),
                  _const_spec(cmat.shape, lambda o, c: (0, 0, 0)),
                  _const_spec(smat.shape, lambda o, c: (0, 0, 0))],
        out_specs=[pl.BlockSpec((1, seq, HY_CW), lambda o, c: (o, 0, c)),
                   pl.BlockSpec((1, seq, HY_CW), lambda o, c: (o, 0, c)),
                   pl.BlockSpec((1, 2, HY_CW), lambda o, c: (o, 0, c))],
        out_shape=[out_sds, out_sds, jax.ShapeDtypeStruct((2, 2, D_HY), F32)],
        scratch_shapes=[pltpu.VMEM((seq, hid), F32)],
        compiler_params=_cparams(("arbitrary", "arbitrary")),
        name="hyena_filters")(zpad, w1pad, b1, w2, b2, freq, w3, w3, deltas, cmat, smat)


def _short_conv(u, w_ref, b_ref, row, seq):
    prev = jnp.where(row == 0, 0.0, pltpu.roll(u, 1, 0))
    nxt = jnp.where(row == seq - 1, 0.0, pltpu.roll(u, seq - 1, 0))
    return w_ref[0:1, :] * prev + w_ref[1:2, :] * u + w_ref[2:3, :] * nxt + b_ref[...]


def _hyena_kernel(uv_ref, u1_ref, u2_ref, wv_ref, w1_ref, w2_ref, bv_ref, b1_ref, b2_ref,
                  c_ref, s_ref, kr_ref, ki_ref, km_ref, skip_ref, nw_ref, o_ref,
                  a_ref, b_ref, t_ref, zb_ref, s1_ref, s2_ref, s3_ref, s4_ref, md_ref):
    seq = uv_ref.shape[1]
    cw = uv_ref.shape[2]
    half = seq // 2
    nblk = half // HY_FC
    row = lax.broadcasted_iota(I32, (seq, LANES), 0)
    alt_half = jnp.where((lax.broadcasted_iota(I32, (half, cw), 0) & 1) == 0, 1.0, -1.0)
    alt_blk = jnp.where((lax.broadcasted_iota(I32, (HY_FC, cw), 0) & 1) == 0, 1.0, -1.0)
    groups = [slice(g * LANES, (g + 1) * LANES) for g in range(cw // LANES)]

    def conv_to(dst_ref, u_ref, w_ref, bias_ref):
        for gi, gs in enumerate(groups):
            t_ref[gi] = _short_conv(u_ref[0, :, gs].astype(F32), w_ref.at[:, gs], bias_ref.at[:, gs], row, seq)
            dst_ref[0:half, gs] = t_ref[gi, pl.ds(0, half, stride=2), :]
            dst_ref[half:seq, gs] = t_ref[gi, pl.ds(1, half, stride=2), :]

    def spectrum(zin_ref, o):
        z = zin_ref[...]
        zb_ref[...] = z.astype(BF16)
        am = jnp.sum(z[:half] * alt_half, axis=0, keepdims=True)
        bm = jnp.sum(z[half:] * alt_half, axis=0, keepdims=True)
        krm, kim = km_ref[o, 0:1, :], km_ref[o, 1:2, :]
        md_ref[0:1, :] = am * krm + bm * kim
        md_ref[1:2, :] = am * kim - bm * krm
        ze = zb_ref[0:half, :]
        zo = zb_ref[half:seq, :]
        for gb in range(nblk):
            lo = slice(gb * HY_FC, (gb + 1) * HY_FC)
            hi = slice(half + gb * HY_FC, half + (gb + 1) * HY_FC)
            ec = jnp.dot(c_ref[0, lo, :], ze, preferred_element_type=F32)
            oc = jnp.dot(c_ref[1, lo, :], zo, preferred_element_type=F32)
            es = jnp.dot(s_ref[0, lo, :], ze, preferred_element_type=F32)
            os_ = jnp.dot(s_ref[1, lo, :], zo, preferred_element_type=F32)
            a_lo, b_lo = ec + oc, es + os_
            a_hi, b_hi = ec - oc, os_ - es
            krl, kil = kr_ref[o, lo, :], ki_ref[o, lo, :]
            krh, kih = kr_ref[o, hi, :], ki_ref[o, hi, :]
            pr = a_lo * krl + b_lo * kil
            pi = a_lo * kil - b_lo * krl
            qr = a_hi * krh + b_hi * kih
            qi = a_hi * kih - b_hi * krh
            s1_ref[lo, :] = (pr + qr).astype(BF16)
            s2_ref[lo, :] = (pi - qi).astype(BF16)
            s3_ref[lo, :] = (pr - qr).astype(BF16)
            s4_ref[lo, :] = (pi + qi).astype(BF16)

    def conv_rows(ub, parity):
        us = slice(ub * HY_FC, (ub + 1) * HY_FC)
        if parity == 0:
            y = jnp.dot(c_ref[0, us, :], s1_ref[...], preferred_element_type=F32)
            y = y - jnp.dot(s_ref[0, us, :], s2_ref[...], preferred_element_type=F32)
            y = y + md_ref[0:1, :] * alt_blk
        else:
            y = jnp.dot(c_ref[2, us, :], s3_ref[...], preferred_element_type=F32)
            y = y - jnp.dot(s_ref[2, us, :], s4_ref[...], preferred_element_type=F32)
            y = y - md_ref[1:2, :] * alt_blk
        return slice(parity * half + ub * HY_FC, parity * half + (ub + 1) * HY_FC), y

    blocks = [(ub, parity) for parity in range(2) for ub in range(nblk)]
    conv_to(a_ref, uv_ref, wv_ref, bv_ref)
    conv_to(b_ref, u1_ref, w1_ref, b1_ref)
    spectrum(a_ref, 0)
    for ub, parity in blocks:
        rows, y = conv_rows(ub, parity)
        b_ref[rows, :] = b_ref[rows, :] * (y + skip_ref[0:1, :] * a_ref[rows, :])
    conv_to(a_ref, u2_ref, w2_ref, b2_ref)
    spectrum(b_ref, 1)
    for ub, parity in blocks:
        rows, y = conv_rows(ub, parity)
        z = a_ref[rows, :] * (y + skip_ref[1:2, :] * b_ref[rows, :])
        for gi, gs in enumerate(groups):
            zg = z[:, gs]
            mu = jnp.mean(zg, axis=-1, keepdims=True)
            zc = zg - mu
            var = jnp.mean(zc * zc, axis=-1, keepdims=True)
            t_ref[gi, pl.ds(2 * ub * HY_FC + parity, HY_FC, stride=2), :] = (
                zc * lax.rsqrt(var + LN_EPS) * nw_ref[:, gs])
    for gi, gs in enumerate(groups):
        o_ref[0, :, gs] = t_ref[gi].astype(o_ref.dtype)


def _hyena(proj3, conv_w, conv_b, cmat, smat, kr, ki, km, skip, norm_w):
    bsz, seq, _ = proj3.shape
    half = seq // 2
    nb = D_HY // HY_CW
    u_spec = lambda off: pl.BlockSpec((1, seq, HY_CW), lambda c, b: (b, 0, off + c))
    w_spec = lambda off: pl.BlockSpec((3, HY_CW), lambda c, b: (0, off + c))
    b_spec = lambda off: pl.BlockSpec((1, HY_CW), lambda c, b: (0, off + c))
    return pl.pallas_call(
        _hyena_kernel,
        grid=(nb, bsz),
        in_specs=[u_spec(0), u_spec(nb), u_spec(2 * nb),
                  w_spec(0), w_spec(nb), w_spec(2 * nb),
                  b_spec(0), b_spec(nb), b_spec(2 * nb),
                  _const_spec(cmat.shape, lambda c, b: (0, 0, 0)),
                  _const_spec(smat.shape, lambda c, b: (0, 0, 0)),
                  _const_spec((2, seq, HY_CW), lambda c, b: (0, 0, c)),
                  _const_spec((2, seq, HY_CW), lambda c, b: (0, 0, c)),
                  pl.BlockSpec((2, 2, HY_CW), lambda c, b: (0, 0, c)),
                  pl.BlockSpec((2, HY_CW), lambda c, b: (0, c)),
                  pl.BlockSpec((1, HY_CW), lambda c, b: (0, c))],
        out_specs=pl.BlockSpec((1, seq, HY_CW), lambda c, b: (b, 0, c)),
        out_shape=jax.ShapeDtypeStruct((bsz, seq, D_HY), BF16),
        scratch_shapes=[pltpu.VMEM((seq, HY_CW), F32), pltpu.VMEM((seq, HY_CW), F32),
                        pltpu.VMEM((HY_CW // LANES, seq, LANES), F32), pltpu.VMEM((seq, HY_CW), BF16),
                        pltpu.VMEM((half, HY_CW), BF16), pltpu.VMEM((half, HY_CW), BF16),
                        pltpu.VMEM((half, HY_CW), BF16), pltpu.VMEM((half, HY_CW), BF16),
                        pltpu.VMEM((2, HY_CW), F32)],
        compiler_params=_cparams(("arbitrary", "arbitrary")),
        name="hyena")(proj3, proj3, proj3, conv_w, conv_w, conv_w, conv_b, conv_b, conv_b,
                      cmat, smat, kr, ki, km, skip, norm_w)


def _mlstm_kernel(qp_ref, kp_ref, v_ref, og_ref, wq_ref, wk_ref, bq_ref, bk_ref, gr_ref, gt_ref,
                  nw_ref, o_ref, qb_ref, kb_ref, hacc_ref, cst_ref, nst_ref):
    seq = qp_ref.shape[1]
    d = qp_ref.shape[2]
    nchunk = seq // CHUNK
    row = lax.broadcasted_iota(I32, (seq, d), 0)
    q = _short_conv(qp_ref[0].astype(F32), wq_ref, bq_ref, row, seq)
    k = _short_conv(kp_ref[0].astype(F32), wk_ref, bk_ref, row, seq)
    qb_ref[...] = (q * jax.nn.sigmoid(q)).astype(BF16)
    kb_ref[...] = ((k * jax.nn.sigmoid(k)) * (d ** -0.5)).astype(BF16)

    ti = lax.broadcasted_iota(I32, (CHUNK, CHUNK), 0)
    si = lax.broadcasted_iota(I32, (CHUNK, CHUNK), 1)
    lower = ti >= si
    upper = ti <= si
    lower_f = lower.astype(F32)
    upper_f = upper.astype(F32)
    nt = (((1,), (1,)), ((), ()))
    tn = (((0,), (0,)), ((), ()))
    chunk_rows = [slice(c * CHUNK, (c + 1) * CHUNK) for c in range(nchunk)]

    for direction in range(2):
        f_idx, i_idx = 2 * direction + 1, 2 * direction
        mask = lower if direction == 0 else upper
        order = list(range(nchunk)) if direction == 0 else list(range(nchunk - 1, -1, -1))
        lf_r = jax.nn.log_sigmoid(gr_ref[0, 0, f_idx])
        b_r = jnp.dot(lf_r, upper_f if direction == 0 else lower_f, precision=HP, preferred_element_type=F32)
        rterm = b_r - gr_ref[0, 0, i_idx]
        b_last = jnp.sum(lf_r, axis=-1, keepdims=True)
        lf_c = jax.nn.log_sigmoid(gt_ref[0, 0, f_idx])
        b_c = jnp.dot(lower_f if direction == 0 else upper_f, lf_c, precision=HP, preferred_element_type=F32)
        i_c = gt_ref[0, 0, i_idx]

        bcol, gcol, gmax, rowmax, blast = [], [], [], [], []
        for c in range(nchunk):
            bc = jnp.broadcast_to(b_c[:, c:c + 1], (CHUNK, CHUNK))
            ic = jnp.broadcast_to(i_c[:, c:c + 1], (CHUNK, CHUNK))
            bl = jnp.broadcast_to(b_last[c:c + 1, :], (1, CHUNK))
            dmat = jnp.where(mask, bc - rterm[c:c + 1, :], -jnp.inf)
            g = bl - bc + ic
            bcol.append(bc)
            gcol.append(g)
            blast.append(bl)
            rowmax.append(jnp.max(dmat, axis=-1, keepdims=True))
            gmax.append(jnp.max(g, axis=0, keepdims=True))

        m = jnp.zeros((1, CHUNK), F32)
        m_in, m_out = [None] * nchunk, [None] * nchunk
        for c in order:
            m_in[c] = m
            m = jnp.maximum(blast[c] + m, gmax[c])
            m_out[c] = m

        cmat = jnp.zeros((d, d), F32)
        nvec = jnp.zeros((1, d), F32)
        for c in order:
            kc = kb_ref[chunk_rows[c], :]
            vc = v_ref[0, chunk_rows[c], :]
            cst_ref[c] = cmat.astype(BF16)
            nst_ref[c] = nvec
            wg = jnp.exp(gcol[c] - m_out[c])
            decay = jnp.exp(blast[c] + m_in[c] - m_out[c])
            upd = lax.dot_general((wg * vc.astype(F32)).astype(BF16), kc, tn, preferred_element_type=F32)
            cmat = decay * cmat + upd
            nvec = decay * nvec + jnp.sum(wg * kc.astype(F32), axis=0, keepdims=True)

        for c in range(nchunk):
            rs = chunk_rows[c]
            qc = qb_ref[rs, :]
            kc = kb_ref[rs, :]
            vc = v_ref[0, rs, :]
            dmat = jnp.where(mask, bcol[c] - rterm[c:c + 1, :], -jnp.inf)
            inter = bcol[c] + m_in[c]
            m_t = jnp.maximum(inter, rowmax[c])
            p = jnp.exp(dmat - m_t)
            inter_w = jnp.exp(inter - m_t)
            s = lax.dot_general(qc, kc, nt, preferred_element_type=F32) * p
            cq = lax.dot_general(qc, cst_ref[c], nt, preferred_element_type=F32)
            num = jnp.dot(s.astype(BF16), vc, preferred_element_type=F32) + inter_w * cq
            nq = jnp.sum(qc.astype(F32) * nst_ref[c], axis=-1, keepdims=True)
            den = jnp.sum(s, axis=-1, keepdims=True) + inter_w * nq
            h = num / jnp.maximum(jnp.abs(den), jnp.exp(-m_t))
            if direction == 0:
                hacc_ref[rs, :] = h
            else:
                hacc_ref[rs, :] += h

    h = hacc_ref[...]
    mu = jnp.mean(h, axis=-1, keepdims=True)
    hc = h - mu
    var = jnp.mean(hc * hc, axis=-1, keepdims=True)
    y = hc * lax.rsqrt(var + LN_EPS) * nw_ref[...] * jax.nn.sigmoid(og_ref[0].astype(F32))
    o_ref[0] = y.astype(o_ref.dtype)


def _mlstm(proj3, conv_w, conv_b, grow, gtr, norm_w):
    bsz, seq, _ = proj3.shape
    d = HEAD_DIM
    nchunk = seq // CHUNK
    hy_blocks = 3 * D_HY // d
    qoff, koff, voff, ooff = hy_blocks, hy_blocks + ML_HEADS, hy_blocks + 2 * ML_HEADS, hy_blocks + 3 * ML_HEADS
    p_spec = lambda off: pl.BlockSpec((1, seq, d), lambda b, h: (b, 0, off + h))
    return pl.pallas_call(
        _mlstm_kernel,
        grid=(bsz, ML_HEADS),
        in_specs=[p_spec(qoff), p_spec(koff), p_spec(voff), p_spec(ooff),
                  pl.BlockSpec((3, d), lambda b, h: (0, h)),
                  pl.BlockSpec((3, d), lambda b, h: (0, ML_HEADS + h)),
                  pl.BlockSpec((1, d), lambda b, h: (0, h)),
                  pl.BlockSpec((1, d), lambda b, h: (0, ML_HEADS + h)),
                  pl.BlockSpec((1, 1, 4, nchunk, CHUNK), lambda b, h: (b, h, 0, 0, 0)),
                  pl.BlockSpec((1, 1, 4, CHUNK, nchunk), lambda b, h: (b, h, 0, 0, 0)),
                  pl.BlockSpec((1, d), lambda b, h: (0, h))],
        out_specs=pl.BlockSpec((1, seq, d), lambda b, h: (b, 0, h)),
        out_shape=jax.ShapeDtypeStruct((bsz, seq, D_ML), BF16),
        scratch_shapes=[pltpu.VMEM((seq, d), BF16), pltpu.VMEM((seq, d), BF16),
                        pltpu.VMEM((seq, d), F32),
                        pltpu.VMEM((nchunk, d, d), BF16), pltpu.VMEM((nchunk, 1, d), F32)],
        compiler_params=_cparams(("arbitrary", "arbitrary")),
        name="mlstm")(proj3, proj3, proj3, proj3, conv_w, conv_w, conv_b, conv_b, grow, gtr, norm_w)


def _to_slabs(y):
    n = y.shape[0]
    half = D_MODEL // 2
    lo = lax.bitcast_convert_type(y[:, :half].astype(BF16).astype(F32), U32) >> 16
    hi = lax.bitcast_convert_type(y[:, half:].astype(BF16).astype(F32), U32) & jnp.uint32(0xFFFF0000)
    words = hi | lo
    parts = jnp.stack([words[:, c * LANES:(c + 1) * LANES] for c in range(ROW_CHUNKS)], axis=0)
    return pltpu.einshape("crl->rcl", parts).reshape(n * ROW_CHUNKS, LANES)


def _from_slabs(v):
    n = v.shape[0] // ROW_CHUNKS
    parts = pltpu.einshape("rcl->crl", v.reshape(n, ROW_CHUNKS, LANES))
    lo = [lax.bitcast_convert_type(parts[c] << 16, F32) for c in range(ROW_CHUNKS)]
    hi = [lax.bitcast_convert_type(parts[c] & jnp.uint32(0xFFFF0000), F32) for c in range(ROW_CHUNKS)]
    return jnp.concatenate(lo + hi, axis=-1)


def _layer_norm(u, g, b):
    mu = jnp.mean(u, axis=-1, keepdims=True)
    uc = u - mu
    var = jnp.mean(uc * uc, axis=-1, keepdims=True)
    return uc * lax.rsqrt(var + LN_EPS) * g + b


def _outproj_kernel(yh_ref, ym_ref, x_ref, wa_ref, wb_ref, b_ref, g_ref, be_ref, o_ref, oc_ref):
    mix = (jnp.dot(yh_ref[...], wa_ref[...], preferred_element_type=F32)
           + jnp.dot(ym_ref[...], wb_ref[...], preferred_element_type=F32) + b_ref[...])
    y = _layer_norm(DN_ALPHA * x_ref[...] + mix, g_ref[...], be_ref[...])
    o_ref[...] = y
    oc_ref[...] = _to_slabs(y)


def _out_proj_ln(y_hy, y_ml, x2d, w_out_bf, b_out, g, be):
    t = x2d.shape[0]
    tm = OUT_TM
    vec = lambda: pl.BlockSpec((1, D_MODEL), lambda i: (0, 0))
    return pl.pallas_call(
        _outproj_kernel,
        grid=(t // tm,),
        in_specs=[pl.BlockSpec((tm, D_HY), lambda i: (i, 0)),
                  pl.BlockSpec((tm, D_ML), lambda i: (i, 0)),
                  pl.BlockSpec((tm, D_MODEL), lambda i: (i, 0)),
                  _const_spec((D_HY, D_MODEL), lambda i: (0, 0)),
                  _const_spec((D_ML, D_MODEL), lambda i: (1, 0)),
                  vec(), vec(), vec()],
        out_specs=[pl.BlockSpec((tm, D_MODEL), lambda i: (i, 0)),
                   pl.BlockSpec((tm * ROW_CHUNKS, LANES), lambda i: (i, 0))],
        out_shape=[jax.ShapeDtypeStruct((t, D_MODEL), F32),
                   jax.ShapeDtypeStruct((t * ROW_CHUNKS, LANES), U32)],
        compiler_params=_cparams(("arbitrary",)),
        name="out_proj_ln1")(y_hy, y_ml, x2d, w_out_bf, w_out_bf, b_out, g, be)


def _router_kernel(x_ref, w_ref, b_ref, ti_ref, tg_ref, tp_ref, cnt_ref):
    tm = ti_ref.shape[0]

    @pl.when(pl.program_id(0) == 0)
    def _():
        cnt_ref[...] = jnp.zeros_like(cnt_ref)

    logits = _dot3(x_ref[...], w_ref[...], NN) + b_ref[...]
    lane = lax.broadcasted_iota(I32, (tm, LANES), 1)
    work = logits
    vals, idxs = [], []
    chosen = jnp.zeros((tm, LANES), F32)
    for _ in range(TOP_K):
        mx = jnp.max(work, axis=-1, keepdims=True)
        idx = jnp.min(jnp.where(work == mx, lane, LANES), axis=-1, keepdims=True)
        hit = lane == idx
        vals.append(mx)
        idxs.append(idx)
        chosen = jnp.where(hit, 1.0, chosen)
        work = jnp.where(hit, -jnp.inf, work)
    exps = [jnp.exp(v - vals[0]) for v in vals]
    den = exps[0] + exps[1] + exps[2] + exps[3]
    ri = lax.broadcasted_iota(I32, (tm, tm), 0)
    ci = lax.broadcasted_iota(I32, (tm, tm), 1)
    strict_lower = (ri > ci).astype(BF16)
    carry = cnt_ref[...]
    slot = carry + jnp.dot(strict_lower, chosen.astype(BF16), preferred_element_type=F32)
    ti = jnp.zeros((tm, LANES), I32)
    tg = jnp.zeros((tm, LANES), F32)
    tp = jnp.zeros((tm, LANES), F32)
    for k in range(TOP_K):
        sk = jnp.sum(jnp.where(lane == idxs[k], slot, 0.0), axis=-1, keepdims=True)
        ti = jnp.where(lane == k, idxs[k], ti)
        tg = jnp.where(lane == k, exps[k] / den, tg)
        tp = jnp.where(lane == k, sk, tp)
    ti_ref[...] = ti
    tg_ref[...] = tg
    tp_ref[...] = tp.astype(I32)
    cnt_ref[...] = carry + jnp.sum(chosen, axis=0, keepdims=True)


def _router(x1, w_pad, b_pad):
    t = x1.shape[0]
    tm = RT_TM
    o_spec = lambda: pl.BlockSpec((tm, LANES), lambda i: (i, 0))
    return pl.pallas_call(
        _router_kernel,
        grid=(t // tm,),
        in_specs=[pl.BlockSpec((tm, D_MODEL), lambda i: (i, 0)),
                  pl.BlockSpec((D_MODEL, LANES), lambda i: (0, 0)),
                  pl.BlockSpec((1, LANES), lambda i: (0, 0))],
        out_specs=[o_spec(), o_spec(), o_spec(), pl.BlockSpec((1, LANES), lambda i: (0, 0))],
        out_shape=[jax.ShapeDtypeStruct((t, LANES), I32), jax.ShapeDtypeStruct((t, LANES), F32),
                   jax.ShapeDtypeStruct((t, LANES), I32), jax.ShapeDtypeStruct((1, LANES), F32)],
        compiler_params=_cparams(("arbitrary",)),
        name="router")(x1, w_pad, b_pad)


def _expert_kernel(te_ref, tr_ref, tb_ref, rt_ref, x_hbm, wg_ref, wu_ref, wd_ref, bg_ref, bu_ref, bd_ref, y_hbm,
                   stage_ref, xb_ref, acc_ref, ring_ref, wgb_ref, wub_ref, wdb_ref, gsem, osem):
    g = pl.program_id(0)
    n_items = pl.num_programs(0) - 1
    nf = D_FF // EXP_TF
    n_tiles = n_items // nf
    item = jnp.maximum(g - 1, 0)
    s = item // nf
    j = item % nf
    rows = tr_ref[s]
    active = jnp.logical_and(g >= 1, rows > 0)
    cast_slot = g % 2
    use_slot = (g + 1) % 2
    slab = EXP_CHUNK * ROW_CHUNKS
    per_step = EXP_ROWS // nf

    def row_copy(tok, r):
        return pltpu.make_async_copy(
            x_hbm.at[pl.ds(pl.multiple_of(tok * ROW_CHUNKS, ROW_CHUNKS), ROW_CHUNKS), :],
            stage_ref.at[pl.ds(pl.multiple_of(r * ROW_CHUNKS, ROW_CHUNKS), ROW_CHUNKS), :], gsem)

    def wait_gather():
        pltpu.make_async_copy(x_hbm.at[pl.ds(0, EXP_ROWS * ROW_CHUNKS), :], stage_ref, gsem).wait()

    def cast_weights():
        wgb_ref[cast_slot] = wg_ref[0].astype(BF16)
        wub_ref[cast_slot] = wu_ref[0].astype(BF16)
        wdb_ref[cast_slot] = wd_ref[0].astype(BF16)

    @pl.when(g == 0)
    def _():
        stage_ref[...] = jnp.zeros_like(stage_ref)
        acc_ref[...] = jnp.zeros_like(acc_ref)
        base = tb_ref[0]

        def group(q, carry):
            for u in range(GATHER_UNROLL):
                r = q * GATHER_UNROLL + u
                row_copy(rt_ref[base + r], r).start()
            return carry

        lax.fori_loop(0, EXP_ROWS // GATHER_UNROLL, group, 0)

        cast_weights()

    @pl.when(active)
    def _():
        @pl.when(j == 0)
        def _():
            wait_gather()
            for i in range(EXP_ROWS // EXP_CHUNK):
                xb_ref[i * EXP_CHUNK:(i + 1) * EXP_CHUNK, :] = _from_slabs(
                    stage_ref[i * slab:(i + 1) * slab, :]).astype(BF16)

        def step_body(m):
            cast_weights()
            nxt_base = tb_ref[jnp.minimum(s + 1, n_tiles - 1)]
            for u in range(per_step):
                r = j * per_step + u
                row_copy(rt_ref[nxt_base + r], r).start()

            xb = xb_ref[0:m, :]
            gate = jnp.dot(xb, wgb_ref[use_slot], preferred_element_type=F32) + bg_ref[0]
            up = jnp.dot(xb, wub_ref[use_slot], preferred_element_type=F32) + bu_ref[0]
            gate = jnp.minimum(gate, SWIGLU_LIMIT)
            up = jnp.clip(up, -SWIGLU_LIMIT, SWIGLU_LIMIT)
            act = (up + 1.0) * (gate * jax.nn.sigmoid(SWIGLU_ALPHA * gate))
            part = jnp.dot(act.astype(BF16), wdb_ref[use_slot], preferred_element_type=F32)
            acc_ref[0:m, :] = jnp.where(j == 0, jnp.broadcast_to(bd_ref[0], part.shape), acc_ref[0:m, :]) + part

        lo = 0
        for m in EXP_M_SIZES:
            @pl.when(jnp.logical_and(rows > lo, rows <= m))
            def _(m=m):
                step_body(m)
            lo = m

        @pl.when(j == nf - 1)
        def _():
            nchunk = (rows + EXP_CHUNK - 1) // EXP_CHUNK

            def chunk_copy(i, slot):
                dst0 = pl.multiple_of((s * EXP_ROWS + i * EXP_CHUNK) * ROW_CHUNKS, slab)
                return pltpu.make_async_copy(ring_ref.at[slot], y_hbm.at[pl.ds(dst0, slab), :], osem.at[slot])

            def emit(i, carry):
                slot = i % 2

                @pl.when(i >= 2)
                def _():
                    chunk_copy(i - 2, slot).wait()

                r0 = pl.multiple_of(i * EXP_CHUNK, EXP_CHUNK)
                ring_ref[slot] = _to_slabs(acc_ref[pl.ds(r0, EXP_CHUNK), :])
                chunk_copy(i, slot).start()
                return carry

            lax.fori_loop(0, nchunk, emit, 0)
            for back in range(2):
                @pl.when(nchunk > back)
                def _():
                    last = nchunk - 1 - back
                    chunk_copy(last, last % 2).wait()

    @pl.when(g == n_items)
    def _():
        wait_gather()


def _experts(tile_e, tile_rows, tile_base, row_tok, used_tiles, x1c, w_gu, b_gu, w_down, b_down):
    n_tiles = tile_e.shape[0]
    nf = D_FF // EXP_TF
    n_items = n_tiles * nf

    def item_block(item, te, tr):
        s = item // nf
        return te[s], jnp.where(tr[s] > 0, item % nf, nf - 1)

    def cast_item(g, te, tr):
        return item_block(jnp.minimum(g, n_items - 1), te, tr)

    def use_item(g, te, tr):
        return item_block(jnp.maximum(g - 1, 0), te, tr)

    def w_gate(g, te, tr, tb, rt):
        e, j = cast_item(g, te, tr)
        return e, 0, j

    def w_up(g, te, tr, tb, rt):
        e, j = cast_item(g, te, tr)
        return e, 0, nf + j

    def w_down_map(g, te, tr, tb, rt):
        e, j = cast_item(g, te, tr)
        return e, j, 0

    def b_gate(g, te, tr, tb, rt):
        e, j = use_item(g, te, tr)
        return e, 0, j

    def b_up(g, te, tr, tb, rt):
        e, j = use_item(g, te, tr)
        return e, 0, nf + j

    def b_down_map(g, te, tr, tb, rt):
        e, _ = use_item(g, te, tr)
        return e, 0, 0

    grid_spec = pltpu.PrefetchScalarGridSpec(
        num_scalar_prefetch=4,
        grid=(used_tiles * nf + 1,),
        in_specs=[pl.BlockSpec(memory_space=pl.ANY),
                  pl.BlockSpec((1, D_MODEL, EXP_TF), w_gate),
                  pl.BlockSpec((1, D_MODEL, EXP_TF), w_up),
                  pl.BlockSpec((1, EXP_TF, D_MODEL), w_down_map),
                  pl.BlockSpec((1, 1, EXP_TF), b_gate),
                  pl.BlockSpec((1, 1, EXP_TF), b_up),
                  pl.BlockSpec((1, 1, D_MODEL), b_down_map)],
        out_specs=pl.BlockSpec(memory_space=pl.ANY),
        scratch_shapes=[pltpu.VMEM((EXP_ROWS * ROW_CHUNKS, LANES), U32),
                        pltpu.VMEM((EXP_ROWS, D_MODEL), BF16),
                        pltpu.VMEM((EXP_ROWS, D_MODEL), F32),
                        pltpu.VMEM((2, EXP_CHUNK * ROW_CHUNKS, LANES), U32),
                        pltpu.VMEM((2, D_MODEL, EXP_TF), BF16),
                        pltpu.VMEM((2, D_MODEL, EXP_TF), BF16),
                        pltpu.VMEM((2, EXP_TF, D_MODEL), BF16),
                        pltpu.SemaphoreType.DMA(()),
                        pltpu.SemaphoreType.DMA((2,))])
    return pl.pallas_call(
        _expert_kernel,
        grid_spec=grid_spec,
        out_shape=jax.ShapeDtypeStruct((n_tiles * EXP_ROWS * ROW_CHUNKS, LANES), U32),
        compiler_params=_cparams(("arbitrary",)),
        name="experts")(tile_e, tile_rows, tile_base, row_tok, x1c, w_gu, w_gu, w_down, b_gu, b_gu, b_down)


def _combine_kernel(dest_ref, y_hbm, x_ref, tg_ref, g_ref, be_ref, o_ref, buf_ref, sem):
    tm = o_ref.shape[0]
    i = pl.program_id(0)
    n = pl.num_programs(0)

    def row_copy(src_row, slot, k, t):
        return pltpu.make_async_copy(
            y_hbm.at[pl.ds(pl.multiple_of(src_row * ROW_CHUNKS, ROW_CHUNKS), ROW_CHUNKS), :],
            buf_ref.at[slot, k, pl.ds(pl.multiple_of(t * ROW_CHUNKS, ROW_CHUNKS), ROW_CHUNKS), :],
            sem.at[slot])

    def start_tile(tile, slot):
        base = tile * tm * TOP_K

        def body(t2, carry):
            for u in range(2):
                t = t2 * 2 + u
                for k in range(TOP_K):
                    row_copy(dest_ref[base + t * TOP_K + k], slot, k, t).start()
            return carry

        lax.fori_loop(0, tm // 2, body, 0)

    def wait_tile(slot):
        for k in range(TOP_K):
            pltpu.make_async_copy(y_hbm.at[pl.ds(0, tm * ROW_CHUNKS), :], buf_ref.at[slot, k], sem.at[slot]).wait()

    @pl.when(i == 0)
    def _():
        start_tile(0, 0)

    @pl.when(i + 1 < n)
    def _():
        start_tile(jnp.minimum(i + 1, n - 1), (i + 1) % 2)

    slot = i % 2
    wait_tile(slot)
    tg = tg_ref[...]
    ff = jnp.zeros((tm, D_MODEL), F32)
    for k in range(TOP_K):
        ff = ff + tg[:, k:k + 1] * _from_slabs(buf_ref[slot, k])
    o_ref[...] = _layer_norm(DN_ALPHA * x_ref[...] + ff, g_ref[...], be_ref[...])


def _combine_ln(dest_flat, y_buf, x1, tg, g, be):
    t = tg.shape[0]
    tm = CMB_TM
    grid_spec = pltpu.PrefetchScalarGridSpec(
        num_scalar_prefetch=1,
        grid=(t // tm,),
        in_specs=[pl.BlockSpec(memory_space=pl.ANY),
                  pl.BlockSpec((tm, D_MODEL), lambda i, d: (i, 0)),
                  pl.BlockSpec((tm, LANES), lambda i, d: (i, 0)),
                  pl.BlockSpec((1, D_MODEL), lambda i, d: (0, 0)),
                  pl.BlockSpec((1, D_MODEL), lambda i, d: (0, 0))],
        out_specs=pl.BlockSpec((tm, D_MODEL), lambda i, d: (i, 0)),
        scratch_shapes=[pltpu.VMEM((2, TOP_K, tm * ROW_CHUNKS, LANES), U32), pltpu.SemaphoreType.DMA((2,))])
    return pl.pallas_call(
        _combine_kernel,
        grid_spec=grid_spec,
        out_shape=jax.ShapeDtypeStruct((t, D_MODEL), F32),
        compiler_params=_cparams(("arbitrary",)),
        name="combine_ln2")(dest_flat, y_buf, x1, tg, g, be)


def _dft_kernel(cd_ref, sd_ref, ca_ref, sa_ref, c_ref, s_ref):
    cd, sd = cd_ref[0], sd_ref[0]
    ca, sa = ca_ref[0], sa_ref[0]
    c_ref[0] = (cd * ca - sd * sa).astype(c_ref.dtype)
    s_ref[0] = (sd * ca + cd * sa).astype(s_ref.dtype)


def _dft_tables(seq):
    n = 2 * seq
    half = seq // 2
    nblk = half // DFT_TB
    idx = jnp.arange(half, dtype=I32)
    off = jnp.arange(DFT_TB, dtype=I32)
    start = jnp.arange(nblk, dtype=I32) * DFT_TB

    def angle(prod):
        return (prod % n).astype(F32) * (2.0 * math.pi / n)

    ang_d = jnp.stack([angle(off[:, None] * (2 * idx)[None, :]),
                       angle(off[:, None] * (2 * idx + 1)[None, :]),
                       angle((2 * off + 1)[:, None] * idx[None, :])])
    ang_a = jnp.stack([angle(start[:, None] * (2 * idx)[None, :]),
                       angle(start[:, None] * (2 * idx + 1)[None, :]),
                       angle((2 * start)[:, None] * idx[None, :])]).reshape(3 * nblk, 1, half)
    small = pl.BlockSpec((1, DFT_TB, half), lambda k, a: (k, 0, 0))
    base = pl.BlockSpec((1, 1, half), lambda k, a: (k * nblk + a, 0, 0))
    out = pl.BlockSpec((1, DFT_TB, half), lambda k, a: (k, a, 0))
    sds = jax.ShapeDtypeStruct((3, half, half), BF16)
    return pl.pallas_call(
        _dft_kernel, grid=(3, nblk), in_specs=[small, small, base, base], out_specs=[out, out],
        out_shape=[sds, sds], compiler_params=_cparams(("arbitrary", "arbitrary")),
        name="dft_tables")(jnp.cos(ang_d), jnp.sin(ang_d), jnp.cos(ang_a), jnp.sin(ang_a))


def _filter_features(seq):
    t = jnp.linspace(0.0, 1.0, seq, dtype=F32)[:, None]
    bands = (HY_EMB - 1) // 2
    fb = jnp.linspace(1e-4, bands - 1, bands, dtype=F32)[None]
    w = 2.0 * math.pi * jnp.arange(seq, dtype=F32)[:, None] / seq
    z = jnp.concatenate([t, jnp.cos(fb * w), -jnp.sin(fb * w)], -1)
    z = jnp.concatenate([z[0::2], z[1::2]], axis=0)
    return jnp.pad(z, ((0, 0), (0, LANES - HY_EMB)))


def _mixer(x, w_in, b_in, hy_conv_w, hy_conv_b, hy_filt_w1, hy_filt_b1, hy_filt_w2, hy_filt_b2,
           hy_filt_w3, hy_filt_freq, hy_skip, hy_norm_w, ml_conv_w, ml_conv_b, ml_norm_w):
    bsz, seq, _ = x.shape
    t = bsz * seq
    x2d = x.reshape(t, D_MODEL)
    n_main = w_in.shape[1] - N_GATE_COLS
    w_t = jnp.swapaxes(w_in, 0, 1)
    wg = jnp.pad(w_t[n_main:], ((0, LANES - N_GATE_COLS), (0, 0)))
    bg = jnp.pad(b_in[None, n_main:], ((0, 0), (0, LANES - N_GATE_COLS)))
    proj, gates = _in_proj(x2d, w_t[:n_main].astype(BF16), b_in[None, :n_main], wg, bg)
    proj3 = proj.reshape(bsz, seq, n_main)
    gates = gates[:, :N_GATE_COLS]
    g5 = gates.reshape(bsz, seq, 4, ML_HEADS)
    grow = g5.transpose(0, 3, 2, 1).reshape(bsz, ML_HEADS, 4, seq // CHUNK, CHUNK)
    gtr = grow.transpose(0, 1, 2, 4, 3)

    cmat, smat = _dft_tables(seq)
    zpad = _filter_features(seq)
    w1pad = jnp.pad(hy_filt_w1, ((0, LANES - HY_EMB), (0, 0)))
    deltas = jnp.abs(jnp.linspace(math.log(HY_DECAY_TARGET) / HY_SLOW_PCT,
                                  math.log(HY_DECAY_TARGET) / HY_FAST_PCT, D_HY, dtype=F32))[None]
    kr, ki, km = _hyena_filters(zpad, w1pad, hy_filt_b1[None], hy_filt_w2, hy_filt_b2[None],
                                hy_filt_freq, hy_filt_w3, deltas, cmat, smat)
    y_hy = _hyena(proj3, hy_conv_w, hy_conv_b[None], cmat, smat, kr, ki, km, hy_skip, hy_norm_w[None])
    y_ml = _mlstm(proj3, ml_conv_w, ml_conv_b[None], grow, gtr, ml_norm_w[None])
    return y_hy.reshape(t, D_HY), y_ml.reshape(t, D_ML), x2d


def _moe_tables(top_i, slot, counts):
    t = top_i.shape[0]
    n_tiles = N_EXPERTS + (t * TOP_K) // EXP_ROWS
    ntile = (counts + EXP_ROWS - 1) // EXP_ROWS
    ends = jnp.cumsum(ntile)
    starts = ends - ntile
    total = ends[-1]
    s_idx = jnp.arange(n_tiles, dtype=I32)
    valid = s_idx < total
    s_eff = jnp.where(valid, s_idx, jnp.maximum(total - 1, 0))
    tile_e = jnp.minimum(jnp.sum((s_eff[:, None] >= ends[None, :]).astype(I32), axis=1), N_EXPERTS - 1)
    local = s_eff - starts[tile_e]
    tile_rows = jnp.where(valid, jnp.clip(counts[tile_e] - local * EXP_ROWS, 0, EXP_ROWS), 0).astype(I32)
    tok = jnp.arange(t, dtype=I32)[:, None]
    row_tok = jnp.pad(jnp.sort((top_i * t + tok).reshape(-1)) % t, (0, EXP_ROWS))
    first = jnp.cumsum(counts) - counts
    tile_base = (first[tile_e] + local * EXP_ROWS).astype(I32)
    onehot = top_i[:, :, None] == jnp.arange(N_EXPERTS, dtype=I32)
    dest = jnp.sum(jnp.where(onehot, starts * EXP_ROWS, 0), axis=-1) + slot
    return (tile_e.astype(I32), tile_rows, tile_base, row_tok.astype(I32), total.astype(I32),
            dest.astype(I32).reshape(-1))


def kernel(x, w_in, b_in, hy_conv_w, hy_conv_b, hy_filt_w1, hy_filt_b1, hy_filt_w2, hy_filt_b2, hy_filt_w3, hy_filt_freq, hy_skip, hy_norm_w, ml_conv_w, ml_conv_b, ml_norm_w, w_out, b_out, ln1_g, ln1_b, router_w, router_b, w_gu, b_gu, w_down, b_down, ln2_g, ln2_b):
    bsz, seq, _ = x.shape
    l = 0
    y_hy, y_ml, x2d = _mixer(x, w_in[l], b_in[l], hy_conv_w[l], hy_conv_b[l], hy_filt_w1[l], hy_filt_b1[l],
                             hy_filt_w2[l], hy_filt_b2[l], hy_filt_w3[l], hy_filt_freq[l], hy_skip[l],
                             hy_norm_w[l], ml_conv_w[l], ml_conv_b[l], ml_norm_w[l])
    x1, x1c = _out_proj_ln(y_hy, y_ml, x2d, w_out[l].astype(BF16), b_out[l][None], ln1_g[l][None],
                           ln1_b[l][None])
    rw = jnp.pad(router_w[l], ((0, 0), (0, LANES - N_EXPERTS)))
    rb = jnp.pad(router_b[l][None], ((0, 0), (0, LANES - N_EXPERTS)), constant_values=-1e30)
    top_i, top_g, slot, cnt = _router(x1, rw, rb)
    counts = cnt[0, :N_EXPERTS].astype(I32)
    tile_e, tile_rows, tile_base, row_tok, used, dest = _moe_tables(top_i[:, :TOP_K], slot[:, :TOP_K], counts)
    y_buf = _experts(tile_e, tile_rows, tile_base, row_tok, used, x1c, w_gu[l], b_gu[l][:, None, :], w_down[l],
                     b_down[l][:, None, :])
    out = _combine_ln(dest, y_buf, x1, top_g, ln2_g[l][None], ln2_b[l][None])
    return out.reshape(bsz, seq, D_MODEL)
```

```python
import functools
import math

import jax
import jax.numpy as jnp
from jax import lax
from jax.experimental import pallas as pl
from jax.experimental.pallas import tpu as pltpu

F32 = jnp.float32
BF16 = jnp.bfloat16
I32 = jnp.int32
U32 = jnp.uint32
HP = lax.Precision.HIGHEST

D_MODEL = 2048
D_HY = 1024
D_ML = 1024
ML_HEADS = 8
HEAD_DIM = 128
CHUNK = 128
N_GATE_COLS = 32
HY_EMB = 33
N_EXPERTS = 32
TOP_K = 4
D_FF = 2048
SWIGLU_LIMIT = 7.0
SWIGLU_ALPHA = 1.702
LN_EPS = 1e-5
DN_ALPHA = 2.0 ** 0.25
HY_DECAY_TARGET = 1e-2
HY_FAST_PCT = 0.3
HY_SLOW_PCT = 1.5

LANES = 128
ROW_CHUNKS = D_MODEL // (2 * LANES)
VMEM_LIMIT = 60 * 1024 * 1024

PROJ_TM = 2048
PROJ_TN = 512
PROJ_GATE_ROWS = 256
HY_CW = 256
HY_FC = 512
OUT_TM = 512
EXP_ROWS = 1152
EXP_M_SIZES = (1024, 1088, 1152)
EXP_CHUNK = 128
EXP_TF = 256
GATHER_UNROLL = 8
CMB_TM = 128
DFT_TB = 128


def _cparams(sem):
    return pltpu.CompilerParams(dimension_semantics=sem, vmem_limit_bytes=VMEM_LIMIT)


def _split(a):
    hi = a.astype(BF16)
    return hi, (a - hi.astype(F32)).astype(BF16)


def _dot3(a, b, dims):
    a_hi, a_lo = _split(a)
    b_hi, b_lo = _split(b)
    mm = functools.partial(lax.dot_general, dimension_numbers=dims, preferred_element_type=F32)
    return mm(a_hi, b_hi) + (mm(a_hi, b_lo) + mm(a_lo, b_hi))


NN = (((1,), (0,)), ((), ()))
NT = (((1,), (1,)), ((), ()))


def _const_spec(shape, index_map):
    return pl.BlockSpec(shape, index_map, pipeline_mode=pl.Buffered(1))


def _proj_kernel(x_ref, w_ref, b_ref, wg_ref, bg_ref, o_ref, og_ref, xb_ref):
    @pl.when(pl.program_id(1) == 0)
    def _():
        xb_ref[...] = x_ref[...].astype(BF16)
        wg_hi, wg_lo = _split(wg_ref[...])
        mm = functools.partial(lax.dot_general, dimension_numbers=NT, preferred_element_type=F32)
        for r in range(x_ref.shape[0] // PROJ_GATE_ROWS):
            rows = slice(r * PROJ_GATE_ROWS, (r + 1) * PROJ_GATE_ROWS)
            x_hi = xb_ref[rows, :]
            x_lo = (x_ref[rows, :] - x_hi.astype(F32)).astype(BF16)
            og_ref[rows, :] = mm(x_hi, wg_hi) + (mm(x_hi, wg_lo) + mm(x_lo, wg_hi)) + bg_ref[...]

    acc = lax.dot_general(xb_ref[...], w_ref[...], NT, preferred_element_type=F32)
    o_ref[...] = (acc + b_ref[...]).astype(o_ref.dtype)


def _in_proj(x2d, wt_bf, b_row, wt_gate, b_gate):
    m, k = x2d.shape
    n = wt_bf.shape[0]
    return pl.pallas_call(
        _proj_kernel,
        grid=(m // PROJ_TM, n // PROJ_TN),
        in_specs=[pl.BlockSpec((PROJ_TM, k), lambda i, j: (i, 0)),
                  pl.BlockSpec((PROJ_TN, k), lambda i, j: (j, 0)),
                  pl.BlockSpec((1, PROJ_TN), lambda i, j: (0, j)),
                  _const_spec((LANES, k), lambda i, j: (0, 0)),
                  _const_spec((1, LANES), lambda i, j: (0, 0))],
        out_specs=[pl.BlockSpec((PROJ_TM, PROJ_TN), lambda i, j: (i, j)),
                   pl.BlockSpec((PROJ_TM, LANES), lambda i, j: (i, 0))],
        out_shape=[jax.ShapeDtypeStruct((m, n), BF16), jax.ShapeDtypeStruct((m, LANES), F32)],
        scratch_shapes=[pltpu.VMEM((PROJ_TM, k), BF16)],
        compiler_params=_cparams(("arbitrary", "arbitrary")),
        name="in_proj")(x2d, wt_bf, b_row, wt_gate, b_gate)


def _filter_kernel(z_ref, w1_ref, b1_ref, w2_ref, b2_ref, fq_ref, w3f_ref, w3b_ref, dl_ref,
                   c_ref, s_ref, kr_ref, ki_ref, km_ref, h_ref):
    seq = z_ref.shape[0]
    inv_n = 1.0 / (2 * seq)
    z = z_ref[...]

    @pl.when(jnp.logical_and(pl.program_id(0) == 0, pl.program_id(1) == 0))
    def _():
        h1 = jnp.sin(fq_ref[0:1, :] * (jnp.dot(z, w1_ref[...], precision=HP, preferred_element_type=F32)
                                       + b1_ref[...]))
        h_ref[...] = jnp.sin(fq_ref[1:2, :] * (jnp.dot(h1, w2_ref[...], precision=HP, preferred_element_type=F32)
                                               + b2_ref[...]))

    half = seq // 2
    h = h_ref[...]
    win = jnp.exp(-z[:, 0:1] * dl_ref[...])
    fwd = _dot3(h, w3f_ref[...], NN) * win
    bwd = _dot3(h, w3b_ref[...], NN) * win
    row = lax.broadcasted_iota(I32, fwd.shape, 0)
    bwd = jnp.where(row == 0, 0.0, bwd)
    inv = 1.0 / jnp.sum(jnp.abs(fwd) + jnp.abs(bwd), axis=0, keepdims=True)
    ks = ((fwd + bwd) * inv)
    kd = ((fwd - bwd) * inv)
    ksb = ks.astype(BF16)
    kdb = kd.astype(BF16)
    ec = jnp.dot(c_ref[0], ksb[:half], preferred_element_type=F32)
    oc = jnp.dot(c_ref[1], ksb[half:], preferred_element_type=F32)
    es = jnp.dot(s_ref[0], kdb[:half], preferred_element_type=F32)
    os_ = jnp.dot(s_ref[1], kdb[half:], preferred_element_type=F32)
    rowh = lax.broadcasted_iota(I32, ec.shape, 0)
    wf = jnp.where(rowh == 0, inv_n, 2.0 * inv_n)
    kr_ref[0, :half, :] = (ec + oc) * wf
    kr_ref[0, half:, :] = (ec - oc) * wf
    ki_ref[0, :half, :] = -(es + os_) * wf
    ki_ref[0, half:, :] = (es - os_) * wf
    alt = jnp.where((rowh & 1) == 0, 1.0, -1.0)
    km_ref[0, 0:1, :] = jnp.sum(ks[:half] * alt, axis=0, keepdims=True) * (2.0 * inv_n)
    km_ref[0, 1:2, :] = -jnp.sum(kd[half:] * alt, axis=0, keepdims=True) * (2.0 * inv_n)


def _hyena_filters(zpad, w1pad, b1, w2, b2, freq, w3, deltas, cmat, smat):
    seq = zpad.shape[0]
    nb = D_HY // HY_CW
    hid = w2.shape[0]
    full = lambda shape: pl.BlockSpec(shape, lambda o, c: (0,) * len(shape))
    out_sds = jax.ShapeDtypeStruct((2, seq, D_HY), F32)
    return pl.pallas_call(
        _filter_kernel,
        grid=(2, nb),
        in_specs=[full(zpad.shape), full(w1pad.shape), full(b1.shape), full(w2.shape), full(b2.shape),
                  full(freq.shape),
                  pl.BlockSpec((hid, HY_CW), lambda o, c: (0, o * 2 * nb + c)),
                  pl.BlockSpec((hid, HY_CW), lambda o, c: (0, o * 2 * nb + nb + c)),
                  pl.BlockSpec((1, HY_CW), lambda o, c: (0, c)),
                  _const_spec(cmat.shape, lambda o, c: (0, 0, 0)),
                  _const_spec(smat.shape, lambda o, c: (0, 0, 0))],
        out_specs=[pl.BlockSpec((1, seq, HY_CW), lambda o, c: (o, 0, c)),
                   pl.BlockSpec((1, seq, HY_CW), lambda o, c: (o, 0, c)),
                   pl.BlockSpec((1, 2, HY_CW), lambda o, c: (o, 0, c))],
        out_shape=[out_sds, out_sds, jax.ShapeDtypeStruct((2, 2, D_HY), F32)],
        scratch_shapes=[pltpu.VMEM((seq, hid), F32)],
        compiler_params=_cparams(("arbitrary", "arbitrary")),
        name="hyena_filters")(zpad, w1pad, b1, w2, b2, freq, w3, w3, deltas, cmat, smat)


def _short_conv(u, w_ref, b_ref, row, seq):
    prev = jnp.where(row == 0, 0.0, pltpu.roll(u, 1, 0))
    nxt = jnp.where(row == seq - 1, 0.0, pltpu.roll(u, seq - 1, 0))
    return w_ref[0:1, :] * prev + w_ref[1:2, :] * u + w_ref[2:3, :] * nxt + b_ref[...]


def _hyena_kernel(uv_ref, u1_ref, u2_ref, wv_ref, w1_ref, w2_ref, bv_ref, b1_ref, b2_ref,
                  c_ref, s_ref, kr_ref, ki_ref, km_ref, skip_ref, nw_ref, o_ref,
                  a_ref, b_ref, t_ref, zb_ref, s1_ref, s2_ref, s3_ref, s4_ref, md_ref):
    seq = uv_ref.shape[1]
    cw = uv_ref.shape[2]
    half = seq // 2
    nblk = half // HY_FC
    row = lax.broadcasted_iota(I32, (seq, LANES), 0)
    alt_half = jnp.where((lax.broadcasted_iota(I32, (half, cw), 0) & 1) == 0, 1.0, -1.0)
    alt_blk = jnp.where((lax.broadcasted_iota(I32, (HY_FC, cw), 0) & 1) == 0, 1.0, -1.0)
    groups = [slice(g * LANES, (g + 1) * LANES) for g in range(cw // LANES)]

    def conv_to(dst_ref, u_ref, w_ref, bias_ref):
        for gi, gs in enumerate(groups):
            t_ref[gi] = _short_conv(u_ref[0, :, gs].astype(F32), w_ref.at[:, gs], bias_ref.at[:, gs], row, seq)
            dst_ref[0:half, gs] = t_ref[gi, pl.ds(0, half, stride=2), :]
            dst_ref[half:seq, gs] = t_ref[gi, pl.ds(1, half, stride=2), :]

    def spectrum(zin_ref, o):
        z = zin_ref[...]
        zb_ref[...] = z.astype(BF16)
        am = jnp.sum(z[:half] * alt_half, axis=0, keepdims=True)
        bm = jnp.sum(z[half:] * alt_half, axis=0, keepdims=True)
        krm, kim = km_ref[o, 0:1, :], km_ref[o, 1:2, :]
        md_ref[0:1, :] = am * krm + bm * kim
        md_ref[1:2, :] = am * kim - bm * krm
        ze = zb_ref[0:half, :]
        zo = zb_ref[half:seq, :]
        for gb in range(nblk):
            lo = slice(gb * HY_FC, (gb + 1) * HY_FC)
            hi = slice(half + gb * HY_FC, half + (gb + 1) * HY_FC)
            ec = jnp.dot(c_ref[0, lo, :], ze, preferred_element_type=F32)
            oc = jnp.dot(c_ref[1, lo, :], zo, preferred_element_type=F32)
            es = jnp.dot(s_ref[0, lo, :], ze, preferred_element_type=F32)
            os_ = jnp.dot(s_ref[1, lo, :], zo, preferred_element_type=F32)
            a_lo, b_lo = ec + oc, es + os_
            a_hi, b_hi = ec - oc, os_ - es
            krl, kil = kr_ref[o, lo, :], ki_ref[o, lo, :]
            krh, kih = kr_ref[o, hi, :], ki_ref[o, hi, :]
            pr = a_lo * krl + b_lo * kil
            pi = a_lo * kil - b_lo * krl
            qr = a_hi * krh + b_hi * kih
            qi = a_hi * kih - b_hi * krh
            s1_ref[lo, :] = (pr + qr).astype(BF16)
            s2_ref[lo, :] = (pi - qi).astype(BF16)
            s3_ref[lo, :] = (pr - qr).astype(BF16)
            s4_ref[lo, :] = (pi + qi).astype(BF16)

    def conv_rows(ub, parity):
        us = slice(ub * HY_FC, (ub + 1) * HY_FC)
        if parity == 0:
            y = jnp.dot(c_ref[0, us, :], s1_ref[...], preferred_element_type=F32)
            y = y - jnp.dot(s_ref[0, us, :], s2_ref[...], preferred_element_type=F32)
            y = y + md_ref[0:1, :] * alt_blk
        else:
            y = jnp.dot(c_ref[2, us, :], s3_ref[...], preferred_element_type=F32)
            y = y - jnp.dot(s_ref[2, us, :], s4_ref[...], preferred_element_type=F32)
            y = y - md_ref[1:2, :] * alt_blk
        return slice(parity * half + ub * HY_FC, parity * half + (ub + 1) * HY_FC), y

    blocks = [(ub, parity) for parity in range(2) for ub in range(nblk)]
    conv_to(a_ref, uv_ref, wv_ref, bv_ref)
    conv_to(b_ref, u1_ref, w1_ref, b1_ref)
    spectrum(a_ref, 0)
    for ub, parity in blocks:
        rows, y = conv_rows(ub, parity)
        b_ref[rows, :] = b_ref[rows, :] * (y + skip_ref[0:1, :] * a_ref[rows, :])
    conv_to(a_ref, u2_ref, w2_ref, b2_ref)
    spectrum(b_ref, 1)
    for ub, parity in blocks:
        rows, y = conv_rows(ub, parity)
        z = a_ref[rows, :] * (y + skip_ref[1:2, :] * b_ref[rows, :])
        for gi, gs in enumerate(groups):
            zg = z[:, gs]
            mu = jnp.mean(zg, axis=-1, keepdims=True)
            zc = zg - mu
            var = jnp.mean(zc * zc, axis=-1, keepdims=True)
            t_ref[gi, pl.ds(2 * ub * HY_FC + parity, HY_FC, stride=2), :] = (
                zc * lax.rsqrt(var + LN_EPS) * nw_ref[:, gs])
    for gi, gs in enumerate(groups):
        o_ref[0, :, gs] = t_ref[gi].astype(o_ref.dtype)


def _hyena(proj3, conv_w, conv_b, cmat, smat, kr, ki, km, skip, norm_w):
    bsz, seq, _ = proj3.shape
    half = seq // 2
    nb = D_HY // HY_CW
    u_spec = lambda off: pl.BlockSpec((1, seq, HY_CW), lambda c, b: (b, 0, off + c))
    w_spec = lambda off: pl.BlockSpec((3, HY_CW), lambda c, b: (0, off + c))
    b_spec = lambda off: pl.BlockSpec((1, HY_CW), lambda c, b: (0, off + c))
    return pl.pallas_call(
        _hyena_kernel,
        grid=(nb, bsz),
        in_specs=[u_spec(0), u_spec(nb), u_spec(2 * nb),
                  w_spec(0), w_spec(nb), w_spec(2 * nb),
                  b_spec(0), b_spec(nb), b_spec(2 * nb),
                  _const_spec(cmat.shape, lambda c, b: (0, 0, 0)),
                  _const_spec(smat.shape, lambda c, b: (0, 0, 0)),
                  _const_spec((2, seq, HY_CW), lambda c, b: (0, 0, c)),
                  _const_spec((2, seq, HY_CW), lambda c, b: (0, 0, c)),
                  pl.BlockSpec((2, 2, HY_CW), lambda c, b: (0, 0, c)),
                  pl.BlockSpec((2, HY_CW), lambda c, b: (0, c)),
                  pl.BlockSpec((1, HY_CW), lambda c, b: (0, c))],
        out_specs=pl.BlockSpec((1, seq, HY_CW), lambda c, b: (b, 0, c)),
        out_shape=jax.ShapeDtypeStruct((bsz, seq, D_HY), BF16),
        scratch_shapes=[pltpu.VMEM((seq, HY_CW), F32), pltpu.VMEM((seq, HY_CW), F32),
                        pltpu.VMEM((HY_CW // LANES, seq, LANES), F32), pltpu.VMEM((seq, HY_CW), BF16),
                        pltpu.VMEM((half, HY_CW), BF16), pltpu.VMEM((half, HY_CW), BF16),
                        pltpu.VMEM((half, HY_CW), BF16), pltpu.VMEM((half, HY_CW), BF16),
                        pltpu.VMEM((2, HY_CW), F32)],
        compiler_params=_cparams(("arbitrary", "arbitrary")),
        name="hyena")(proj3, proj3, proj3, conv_w, conv_w, conv_w, conv_b, conv_b, conv_b,
                      cmat, smat, kr, ki, km, skip, norm_w)


def _mlstm_kernel(qp_ref, kp_ref, v_ref, og_ref, wq_ref, wk_ref, bq_ref, bk_ref, gr_ref, gt_ref,
                  nw_ref, o_ref, qb_ref, kb_ref, hacc_ref, cst_ref, nst_ref):
    seq = qp_ref.shape[1]
    d = qp_ref.shape[2]
    nchunk = seq // CHUNK
    row = lax.broadcasted_iota(I32, (seq, d), 0)
    q = _short_conv(qp_ref[0].astype(F32), wq_ref, bq_ref, row, seq)
    k = _short_conv(kp_ref[0].astype(F32), wk_ref, bk_ref, row, seq)
    qb_ref[...] = (q * jax.nn.sigmoid(q)).astype(BF16)
    kb_ref[...] = ((k * jax.nn.sigmoid(k)) * (d ** -0.5)).astype(BF16)

    ti = lax.broadcasted_iota(I32, (CHUNK, CHUNK), 0)
    si = lax.broadcasted_iota(I32, (CHUNK, CHUNK), 1)
    lower = ti >= si
    upper = ti <= si
    lower_f = lower.astype(F32)
    upper_f = upper.astype(F32)
    nt = (((1,), (1,)), ((), ()))
    tn = (((0,), (0,)), ((), ()))
    chunk_rows = [slice(c * CHUNK, (c + 1) * CHUNK) for c in range(nchunk)]

    for direction in range(2):
        f_idx, i_idx = 2 * direction + 1, 2 * direction
        mask = lower if direction == 0 else upper
        order = list(range(nchunk)) if direction == 0 else list(range(nchunk - 1, -1, -1))
        lf_r = jax.nn.log_sigmoid(gr_ref[0, 0, f_idx])
        b_r = jnp.dot(lf_r, upper_f if direction == 0 else lower_f, precision=HP, preferred_element_type=F32)
        rterm = b_r - gr_ref[0, 0, i_idx]
        b_last = jnp.sum(lf_r, axis=-1, keepdims=True)
        lf_c = jax.nn.log_sigmoid(gt_ref[0, 0, f_idx])
        b_c = jnp.dot(lower_f if direction == 0 else upper_f, lf_c, precision=HP, preferred_element_type=F32)
        i_c = gt_ref[0, 0, i_idx]

        bcol, gcol, gmax, rowmax, blast = [], [], [], [], []
        for c in range(nchunk):
            bc = jnp.broadcast_to(b_c[:, c:c + 1], (CHUNK, CHUNK))
            ic = jnp.broadcast_to(i_c[:, c:c + 1], (CHUNK, CHUNK))
            bl = jnp.broadcast_to(b_last[c:c + 1, :], (1, CHUNK))
            dmat = jnp.where(mask, bc - rterm[c:c + 1, :], -jnp.inf)
            g = bl - bc + ic
            bcol.append(bc)
            gcol.append(g)
            blast.append(bl)
            rowmax.append(jnp.max(dmat, axis=-1, keepdims=True))
            gmax.append(jnp.max(g, axis=0, keepdims=True))

        m = jnp.zeros((1, CHUNK), F32)
        m_in, m_out = [None] * nchunk, [None] * nchunk
        for c in order:
            m_in[c] = m
            m = jnp.maximum(blast[c] + m, gmax[c])
            m_out[c] = m

        cmat = jnp.zeros((d, d), F32)
        nvec = jnp.zeros((1, d), F32)
        for c in order:
            kc = kb_ref[chunk_rows[c], :]
            vc = v_ref[0, chunk_rows[c], :]
            cst_ref[c] = cmat.astype(BF16)
            nst_ref[c] = nvec
            wg = jnp.exp(gcol[c] - m_out[c])
            decay = jnp.exp(blast[c] + m_in[c] - m_out[c])
            upd = lax.dot_general((wg * vc.astype(F32)).astype(BF16), kc, tn, preferred_element_type=F32)
            cmat = decay * cmat + upd
            nvec = decay * nvec + jnp.sum(wg * kc.astype(F32), axis=0, keepdims=True)

        for c in range(nchunk):
            rs = chunk_rows[c]
            qc = qb_ref[rs, :]
            kc = kb_ref[rs, :]
            vc = v_ref[0, rs, :]
            dmat = jnp.where(mask, bcol[c] - rterm[c:c + 1, :], -jnp.inf)
            inter = bcol[c] + m_in[c]
            m_t = jnp.maximum(inter, rowmax[c])
            p = jnp.exp(dmat - m_t)
            inter_w = jnp.exp(inter - m_t)
            s = lax.dot_general(qc, kc, nt, preferred_element_type=F32) * p
            cq = lax.dot_general(qc, cst_ref[c], nt, preferred_element_type=F32)
            num = jnp.dot(s.astype(BF16), vc, preferred_element_type=F32) + inter_w * cq
            nq = jnp.sum(qc.astype(F32) * nst_ref[c], axis=-1, keepdims=True)
            den = jnp.sum(s, axis=-1, keepdims=True) + inter_w * nq
            h = num / jnp.maximum(jnp.abs(den), jnp.exp(-m_t))
            if direction == 0:
                hacc_ref[rs, :] = h
            else:
                hacc_ref[rs, :] += h

    h = hacc_ref[...]
    mu = jnp.mean(h, axis=-1, keepdims=True)
    hc = h - mu
    var = jnp.mean(hc * hc, axis=-1, keepdims=True)
    y = hc * lax.rsqrt(var + LN_EPS) * nw_ref[...] * jax.nn.sigmoid(og_ref[0].astype(F32))
    o_ref[0] = y.astype(o_ref.dtype)


def _mlstm(proj3, conv_w, conv_b, grow, gtr, norm_w):
    bsz, seq, _ = proj3.shape
    d = HEAD_DIM
    nchunk = seq // CHUNK
    hy_blocks = 3 * D_HY // d
    qoff, koff, voff, ooff = hy_blocks, hy_blocks + ML_HEADS, hy_blocks + 2 * ML_HEADS, hy_blocks + 3 * ML_HEADS
    p_spec = lambda off: pl.BlockSpec((1, seq, d), lambda b, h: (b, 0, off + h))
    return pl.pallas_call(
        _mlstm_kernel,
        grid=(bsz, ML_HEADS),
        in_specs=[p_spec(qoff), p_spec(koff), p_spec(voff), p_spec(ooff),
                  pl.BlockSpec((3, d), lambda b, h: (0, h)),
                  pl.BlockSpec((3, d), lambda b, h: (0, ML_HEADS + h)),
                  pl.BlockSpec((1, d), lambda b, h: (0, h)),
                  pl.BlockSpec((1, d), lambda b, h: (0, ML_HEADS + h)),
                  pl.BlockSpec((1, 1, 4, nchunk, CHUNK), lambda b, h: (b, h, 0, 0, 0)),
                  pl.BlockSpec((1, 1, 4, CHUNK, nchunk), lambda b, h: (b, h, 0, 0, 0)),
                  pl.BlockSpec((1, d), lambda b, h: (0, h))],
        out_specs=pl.BlockSpec((1, seq, d), lambda b, h: (b, 0, h)),
        out_shape=jax.ShapeDtypeStruct((bsz, seq, D_ML), BF16),
        scratch_shapes=[pltpu.VMEM((seq, d), BF16), pltpu.VMEM((seq, d), BF16),
                        pltpu.VMEM((seq, d), F32),
                        pltpu.VMEM((nchunk, d, d), BF16), pltpu.VMEM((nchunk, 1, d), F32)],
        compiler_params=_cparams(("arbitrary", "arbitrary")),
        name="mlstm")(proj3, proj3, proj3, proj3, conv_w, conv_w, conv_b, conv_b, grow, gtr, norm_w)


def _to_slabs(y):
    n = y.shape[0]
    half = D_MODEL // 2
    lo = lax.bitcast_convert_type(y[:, :half].astype(BF16).astype(F32), U32) >> 16
    hi = lax.bitcast_convert_type(y[:, half:].astype(BF16).astype(F32), U32) & jnp.uint32(0xFFFF0000)
    words = hi | lo
    parts = jnp.stack([words[:, c * LANES:(c + 1) * LANES] for c in range(ROW_CHUNKS)], axis=0)
    return pltpu.einshape("crl->rcl", parts).reshape(n * ROW_CHUNKS, LANES)


def _from_slabs(v):
    n = v.shape[0] // ROW_CHUNKS
    parts = pltpu.einshape("rcl->crl", v.reshape(n, ROW_CHUNKS, LANES))
    lo = [lax.bitcast_convert_type(parts[c] << 16, F32) for c in range(ROW_CHUNKS)]
    hi = [lax.bitcast_convert_type(parts[c] & jnp.uint32(0xFFFF0000), F32) for c in range(ROW_CHUNKS)]
    return jnp.concatenate(lo + hi, axis=-1)


def _layer_norm(u, g, b):
    mu = jnp.mean(u, axis=-1, keepdims=True)
    uc = u - mu
    var = jnp.mean(uc * uc, axis=-1, keepdims=True)
    return uc * lax.rsqrt(var + LN_EPS) * g + b


def _route(x, w_ref, b_ref, ti_ref, tg_ref, tp_ref, cnt_ref):
    tm = ti_ref.shape[0]

    @pl.when(pl.program_id(0) == 0)
    def _():
        cnt_ref[...] = jnp.zeros_like(cnt_ref)

    logits = _dot3(x, w_ref[...], NN) + b_ref[...]
    lane = lax.broadcasted_iota(I32, (tm, LANES), 1)
    work = logits
    vals, idxs = [], []
    chosen = jnp.zeros((tm, LANES), F32)
    for _ in range(TOP_K):
        mx = jnp.max(work, axis=-1, keepdims=True)
        idx = jnp.min(jnp.where(work == mx, lane, LANES), axis=-1, keepdims=True)
        hit = lane == idx
        vals.append(mx)
        idxs.append(idx)
        chosen = jnp.where(hit, 1.0, chosen)
        work = jnp.where(hit, -jnp.inf, work)
    exps = [jnp.exp(v - vals[0]) for v in vals]
    den = exps[0] + exps[1] + exps[2] + exps[3]
    ri = lax.broadcasted_iota(I32, (tm, tm), 0)
    ci = lax.broadcasted_iota(I32, (tm, tm), 1)
    strict_lower = (ri > ci).astype(BF16)
    carry = cnt_ref[...]
    slot = carry + jnp.dot(strict_lower, chosen.astype(BF16), preferred_element_type=F32)
    ti = jnp.zeros((tm, LANES), I32)
    tg = jnp.zeros((tm, LANES), F32)
    tp = jnp.zeros((tm, LANES), F32)
    for k in range(TOP_K):
        sk = jnp.sum(jnp.where(lane == idxs[k], slot, 0.0), axis=-1, keepdims=True)
        ti = jnp.where(lane == k, idxs[k], ti)
        tg = jnp.where(lane == k, exps[k] / den, tg)
        tp = jnp.where(lane == k, sk, tp)
    ti_ref[...] = ti
    tg_ref[...] = tg
    tp_ref[...] = tp.astype(I32)
    cnt_ref[...] = carry + jnp.sum(chosen, axis=0, keepdims=True)


def _outproj_kernel(yh_ref, ym_ref, x_ref, wa_ref, wb_ref, b_ref, g_ref, be_ref, rw_ref, rb_ref,
                    o_ref, oc_ref, ti_ref, tg_ref, tp_ref, cnt_ref):
    mix = (jnp.dot(yh_ref[...], wa_ref[...], preferred_element_type=F32)
           + jnp.dot(ym_ref[...], wb_ref[...], preferred_element_type=F32) + b_ref[...])
    y = _layer_norm(DN_ALPHA * x_ref[...] + mix, g_ref[...], be_ref[...])
    o_ref[...] = y
    oc_ref[...] = _to_slabs(y)
    _route(y, rw_ref, rb_ref, ti_ref, tg_ref, tp_ref, cnt_ref)


def _out_proj_ln_route(y_hy, y_ml, x2d, w_out_bf, b_out, g, be, rw_pad, rb_pad):
    t = x2d.shape[0]
    tm = OUT_TM
    vec = lambda: pl.BlockSpec((1, D_MODEL), lambda i: (0, 0))
    lane_blk = lambda: pl.BlockSpec((tm, LANES), lambda i: (i, 0))
    return pl.pallas_call(
        _outproj_kernel,
        grid=(t // tm,),
        in_specs=[pl.BlockSpec((tm, D_HY), lambda i: (i, 0)),
                  pl.BlockSpec((tm, D_ML), lambda i: (i, 0)),
                  pl.BlockSpec((tm, D_MODEL), lambda i: (i, 0)),
                  _const_spec((D_HY, D_MODEL), lambda i: (0, 0)),
                  _const_spec((D_ML, D_MODEL), lambda i: (1, 0)),
                  vec(), vec(), vec(),
                  _const_spec((D_MODEL, LANES), lambda i: (0, 0)),
                  pl.BlockSpec((1, LANES), lambda i: (0, 0))],
        out_specs=[pl.BlockSpec((tm, D_MODEL), lambda i: (i, 0)),
                   pl.BlockSpec((tm * ROW_CHUNKS, LANES), lambda i: (i, 0)),
                   lane_blk(), lane_blk(), lane_blk(), pl.BlockSpec((1, LANES), lambda i: (0, 0))],
        out_shape=[jax.ShapeDtypeStruct((t, D_MODEL), F32),
                   jax.ShapeDtypeStruct((t * ROW_CHUNKS, LANES), U32),
                   jax.ShapeDtypeStruct((t, LANES), I32), jax.ShapeDtypeStruct((t, LANES), F32),
                   jax.ShapeDtypeStruct((t, LANES), I32), jax.ShapeDtypeStruct((1, LANES), F32)],
        compiler_params=_cparams(("arbitrary",)),
        name="out_proj_ln1_route")(y_hy, y_ml, x2d, w_out_bf, w_out_bf, b_out, g, be, rw_pad, rb_pad)


def _expert_kernel(te_ref, tr_ref, tb_ref, rt_ref, x_hbm, wg_ref, wu_ref, wd_ref, bg_ref, bu_ref, bd_ref, y_hbm,
                   stage_ref, xb_ref, acc_ref, ring_ref, wgb_ref, wub_ref, wdb_ref, gsem, osem):
    g = pl.program_id(0)
    n_items = pl.num_programs(0) - 1
    nf = D_FF // EXP_TF
    n_tiles = n_items // nf
    item = jnp.maximum(g - 1, 0)
    s = item // nf
    j = item % nf
    rows = tr_ref[s]
    active = jnp.logical_and(g >= 1, rows > 0)
    cast_slot = g % 2
    use_slot = (g + 1) % 2
    slab = EXP_CHUNK * ROW_CHUNKS
    per_step = EXP_ROWS // nf

    def row_copy(tok, r):
        return pltpu.make_async_copy(
            x_hbm.at[pl.ds(pl.multiple_of(tok * ROW_CHUNKS, ROW_CHUNKS), ROW_CHUNKS), :],
            stage_ref.at[pl.ds(pl.multiple_of(r * ROW_CHUNKS, ROW_CHUNKS), ROW_CHUNKS), :], gsem)

    def wait_gather():
        pltpu.make_async_copy(x_hbm.at[pl.ds(0, EXP_ROWS * ROW_CHUNKS), :], stage_ref, gsem).wait()

    def cast_weights():
        wgb_ref[cast_slot] = wg_ref[0].astype(BF16)
        wub_ref[cast_slot] = wu_ref[0].astype(BF16)
        wdb_ref[cast_slot] = wd_ref[0].astype(BF16)

    @pl.when(g == 0)
    def _():
        stage_ref[...] = jnp.zeros_like(stage_ref)
        acc_ref[...] = jnp.zeros_like(acc_ref)
        base = tb_ref[0]

        def group(q, carry):
            for u in range(GATHER_UNROLL):
                r = q * GATHER_UNROLL + u
                row_copy(rt_ref[base + r], r).start()
            return carry

        lax.fori_loop(0, EXP_ROWS // GATHER_UNROLL, group, 0)

        cast_weights()

    @pl.when(active)
    def _():
        @pl.when(j == 0)
        def _():
            wait_gather()
            for i in range(EXP_ROWS // EXP_CHUNK):
                xb_ref[i * EXP_CHUNK:(i + 1) * EXP_CHUNK, :] = _from_slabs(
                    stage_ref[i * slab:(i + 1) * slab, :]).astype(BF16)

        def step_body(m):
            cast_weights()
            nxt_base = tb_ref[jnp.minimum(s + 1, n_tiles - 1)]
            for u in range(per_step):
                r = j * per_step + u
                row_copy(rt_ref[nxt_base + r], r).start()

            xb = xb_ref[0:m, :]
            gate = jnp.dot(xb, wgb_ref[use_slot], preferred_element_type=F32) + bg_ref[0]
            up = jnp.dot(xb, wub_ref[use_slot], preferred_element_type=F32) + bu_ref[0]
            gate = jnp.minimum(gate, SWIGLU_LIMIT)
            up = jnp.clip(up, -SWIGLU_LIMIT, SWIGLU_LIMIT)
            act = (up + 1.0) * (gate * jax.nn.sigmoid(SWIGLU_ALPHA * gate))
            part = jnp.dot(act.astype(BF16), wdb_ref[use_slot], preferred_element_type=F32)
            acc_ref[0:m, :] = jnp.where(j == 0, jnp.broadcast_to(bd_ref[0], part.shape), acc_ref[0:m, :]) + part

        lo = 0
        for m in EXP_M_SIZES:
            @pl.when(jnp.logical_and(rows > lo, rows <= m))
            def _(m=m):
                step_body(m)
            lo = m

        @pl.when(j == nf - 1)
        def _():
            nchunk = (rows + EXP_CHUNK - 1) // EXP_CHUNK

            def chunk_copy(i, slot):
                dst0 = pl.multiple_of((s * EXP_ROWS + i * EXP_CHUNK) * ROW_CHUNKS, slab)
                return pltpu.make_async_copy(ring_ref.at[slot], y_hbm.at[pl.ds(dst0, slab), :], osem.at[slot])

            def emit(i, carry):
                slot = i % 2

                @pl.when(i >= 2)
                def _():
                    chunk_copy(i - 2, slot).wait()

                r0 = pl.multiple_of(i * EXP_CHUNK, EXP_CHUNK)
                ring_ref[slot] = _to_slabs(acc_ref[pl.ds(r0, EXP_CHUNK), :])
                chunk_copy(i, slot).start()
                return carry

            lax.fori_loop(0, nchunk, emit, 0)
            for back in range(2):
                @pl.when(nchunk > back)
                def _():
                    last = nchunk - 1 - back
                    chunk_copy(last, last % 2).wait()

    @pl.when(g == n_items)
    def _():
        wait_gather()


def _experts(tile_e, tile_rows, tile_base, row_tok, used_tiles, x1c, w_gu, b_gu, w_down, b_down):
    n_tiles = tile_e.shape[0]
    nf = D_FF // EXP_TF
    n_items = n_tiles * nf

    def item_block(item, te, tr):
        s = item // nf
        return te[s], jnp.where(tr[s] > 0, item % nf, nf - 1)

    def cast_item(g, te, tr):
        return item_block(jnp.minimum(g, n_items - 1), te, tr)

    def use_item(g, te, tr):
        return item_block(jnp.maximum(g - 1, 0), te, tr)

    def w_gate(g, te, tr, tb, rt):
        e, j = cast_item(g, te, tr)
        return e, 0, j

    def w_up(g, te, tr, tb, rt):
        e, j = cast_item(g, te, tr)
        return e, 0, nf + j

    def w_down_map(g, te, tr, tb, rt):
        e, j = cast_item(g, te, tr)
        return e, j, 0

    def b_gate(g, te, tr, tb, rt):
        e, j = use_item(g, te, tr)
        return e, 0, j

    def b_up(g, te, tr, tb, rt):
        e, j = use_item(g, te, tr)
        return e, 0, nf + j

    def b_down_map(g, te, tr, tb, rt):
        e, _ = use_item(g, te, tr)
        return e, 0, 0

    grid_spec = pltpu.PrefetchScalarGridSpec(
        num_scalar_prefetch=4,
        grid=(used_tiles * nf + 1,),
        in_specs=[pl.BlockSpec(memory_space=pl.ANY),
                  pl.BlockSpec((1, D_MODEL, EXP_TF), w_gate),
                  pl.BlockSpec((1, D_MODEL, EXP_TF), w_up),
                  pl.BlockSpec((1, EXP_TF, D_MODEL), w_down_map),
                  pl.BlockSpec((1, 1, EXP_TF), b_gate),
                  pl.BlockSpec((1, 1, EXP_TF), b_up),
                  pl.BlockSpec((1, 1, D_MODEL), b_down_map)],
        out_specs=pl.BlockSpec(memory_space=pl.ANY),
        scratch_shapes=[pltpu.VMEM((EXP_ROWS * ROW_CHUNKS, LANES), U32),
                        pltpu.VMEM((EXP_ROWS, D_MODEL), BF16),
                        pltpu.VMEM((EXP_ROWS, D_MODEL), F32),
                        pltpu.VMEM((2, EXP_CHUNK * ROW_CHUNKS, LANES), U32),
                        pltpu.VMEM((2, D_MODEL, EXP_TF), BF16),
                        pltpu.VMEM((2, D_MODEL, EXP_TF), BF16),
                        pltpu.VMEM((2, EXP_TF, D_MODEL), BF16),
                        pltpu.SemaphoreType.DMA(()),
                        pltpu.SemaphoreType.DMA((2,))])
    return pl.pallas_call(
        _expert_kernel,
        grid_spec=grid_spec,
        out_shape=jax.ShapeDtypeStruct((n_tiles * EXP_ROWS * ROW_CHUNKS, LANES), U32),
        compiler_params=_cparams(("arbitrary",)),
        name="experts")(tile_e, tile_rows, tile_base, row_tok, x1c, w_gu, w_gu, w_down, b_gu, b_gu, b_down)


def _combine_kernel(dest_ref, y_hbm, x_ref, tg_ref, g_ref, be_ref, o_ref, buf_ref, sem):
    tm = o_ref.shape[0]
    i = pl.program_id(0)
    n = pl.num_programs(0)

    def row_copy(src_row, slot, k, t):
        return pltpu.make_async_copy(
            y_hbm.at[pl.ds(pl.multiple_of(src_row * ROW_CHUNKS, ROW_CHUNKS), ROW_CHUNKS), :],
            buf_ref.at[slot, k, pl.ds(pl.multiple_of(t * ROW_CHUNKS, ROW_CHUNKS), ROW_CHUNKS), :],
            sem.at[slot])

    def start_tile(tile, slot):
        base = tile * tm * TOP_K

        def body(t2, carry):
            for u in range(2):
                t = t2 * 2 + u
                for k in range(TOP_K):
                    row_copy(dest_ref[base + t * TOP_K + k], slot, k, t).start(priority=k % 2)
            return carry

        lax.fori_loop(0, tm // 2, body, 0)

    def wait_tile(slot):
        for k in range(TOP_K):
            pltpu.make_async_copy(y_hbm.at[pl.ds(0, tm * ROW_CHUNKS), :], buf_ref.at[slot, k], sem.at[slot]).wait()

    @pl.when(i == 0)
    def _():
        start_tile(0, 0)

    @pl.when(i + 1 < n)
    def _():
        start_tile(jnp.minimum(i + 1, n - 1), (i + 1) % 2)

    slot = i % 2
    wait_tile(slot)
    tg = tg_ref[...]
    ff = jnp.zeros((tm, D_MODEL), F32)
    for k in range(TOP_K):
        ff = ff + tg[:, k:k + 1] * _from_slabs(buf_ref[slot, k])
    o_ref[...] = _layer_norm(DN_ALPHA * x_ref[...] + ff, g_ref[...], be_ref[...])


def _combine_ln(dest_flat, y_buf, x1, tg, g, be):
    t = tg.shape[0]
    tm = CMB_TM
    grid_spec = pltpu.PrefetchScalarGridSpec(
        num_scalar_prefetch=1,
        grid=(t // tm,),
        in_specs=[pl.BlockSpec(memory_space=pl.ANY),
                  pl.BlockSpec((tm, D_MODEL), lambda i, d: (i, 0)),
                  pl.BlockSpec((tm, LANES), lambda i, d: (i, 0)),
                  pl.BlockSpec((1, D_MODEL), lambda i, d: (0, 0)),
                  pl.BlockSpec((1, D_MODEL), lambda i, d: (0, 0))],
        out_specs=pl.BlockSpec((tm, D_MODEL), lambda i, d: (i, 0)),
        scratch_shapes=[pltpu.VMEM((2, TOP_K, tm * ROW_CHUNKS, LANES), U32), pltpu.SemaphoreType.DMA((2,))])
    return pl.pallas_call(
        _combine_kernel,
        grid_spec=grid_spec,
        out_shape=jax.ShapeDtypeStruct((t, D_MODEL), F32),
        compiler_params=_cparams(("arbitrary",)),
        name="combine_ln2")(dest_flat, y_buf, x1, tg, g, be)


def _dft_kernel(cd_ref, sd_ref, ca_ref, sa_ref, c_ref, s_ref):
    cd, sd = cd_ref[0], sd_ref[0]
    ca, sa = ca_ref[0], sa_ref[0]
    c_ref[0] = (cd * ca - sd * sa).astype(c_ref.dtype)
    s_ref[0] = (sd * ca + cd * sa).astype(s_ref.dtype)


def _dft_tables(seq):
    n = 2 * seq
    half = seq // 2
    nblk = half // DFT_TB
    idx = jnp.arange(half, dtype=I32)
    off = jnp.arange(DFT_TB, dtype=I32)
    start = jnp.arange(nblk, dtype=I32) * DFT_TB

    def angle(prod):
        return (prod % n).astype(F32) * (2.0 * math.pi / n)

    ang_d = jnp.stack([angle(off[:, None] * (2 * idx)[None, :]),
                       angle(off[:, None] * (2 * idx + 1)[None, :]),
                       angle((2 * off + 1)[:, None] * idx[None, :])])
    ang_a = jnp.stack([angle(start[:, None] * (2 * idx)[None, :]),
                       angle(start[:, None] * (2 * idx + 1)[None, :]),
                       angle((2 * start)[:, None] * idx[None, :])]).reshape(3 * nblk, 1, half)
    small = pl.BlockSpec((1, DFT_TB, half), lambda k, a: (k, 0, 0))
    base = pl.BlockSpec((1, 1, half), lambda k, a: (k * nblk + a, 0, 0))
    out = pl.BlockSpec((1, DFT_TB, half), lambda k, a: (k, a, 0))
    sds = jax.ShapeDtypeStruct((3, half, half), BF16)
    return pl.pallas_call(
        _dft_kernel, grid=(3, nblk), in_specs=[small, small, base, base], out_specs=[out, out],
        out_shape=[sds, sds], compiler_params=_cparams(("arbitrary", "arbitrary")),
        name="dft_tables")(jnp.cos(ang_d), jnp.sin(ang_d), jnp.cos(ang_a), jnp.sin(ang_a))


def _filter_features(seq):
    t = jnp.linspace(0.0, 1.0, seq, dtype=F32)[:, None]
    bands = (HY_EMB - 1) // 2
    fb = jnp.linspace(1e-4, bands - 1, bands, dtype=F32)[None]
    w = 2.0 * math.pi * jnp.arange(seq, dtype=F32)[:, None] / seq
    z = jnp.concatenate([t, jnp.cos(fb * w), -jnp.sin(fb * w)], -1)
    z = jnp.concatenate([z[0::2], z[1::2]], axis=0)
    return jnp.pad(z, ((0, 0), (0, LANES - HY_EMB)))


def _mixer(x, w_in, b_in, hy_conv_w, hy_conv_b, hy_filt_w1, hy_filt_b1, hy_filt_w2, hy_filt_b2,
           hy_filt_w3, hy_filt_freq, hy_skip, hy_norm_w, ml_conv_w, ml_conv_b, ml_norm_w):
    bsz, seq, _ = x.shape
    t = bsz * seq
    x2d = x.reshape(t, D_MODEL)
    n_main = w_in.shape[1] - N_GATE_COLS
    w_t = jnp.swapaxes(w_in, 0, 1)
    wg = jnp.pad(w_t[n_main:], ((0, LANES - N_GATE_COLS), (0, 0)))
    bg = jnp.pad(b_in[None, n_main:], ((0, 0), (0, LANES - N_GATE_COLS)))
    proj, gates = _in_proj(x2d, w_t[:n_main].astype(BF16), b_in[None, :n_main], wg, bg)
    proj3 = proj.reshape(bsz, seq, n_main)
    gates = gates[:, :N_GATE_COLS]
    g5 = gates.reshape(bsz, seq, 4, ML_HEADS)
    grow = g5.transpose(0, 3, 2, 1).reshape(bsz, ML_HEADS, 4, seq // CHUNK, CHUNK)
    gtr = grow.transpose(0, 1, 2, 4, 3)

    cmat, smat = _dft_tables(seq)
    zpad = _filter_features(seq)
    w1pad = jnp.pad(hy_filt_w1, ((0, LANES - HY_EMB), (0, 0)))
    deltas = jnp.abs(jnp.linspace(math.log(HY_DECAY_TARGET) / HY_SLOW_PCT,
                                  math.log(HY_DECAY_TARGET) / HY_FAST_PCT, D_HY, dtype=F32))[None]
    kr, ki, km = _hyena_filters(zpad, w1pad, hy_filt_b1[None], hy_filt_w2, hy_filt_b2[None],
                                hy_filt_freq, hy_filt_w3, deltas, cmat, smat)
    y_hy = _hyena(proj3, hy_conv_w, hy_conv_b[None], cmat, smat, kr, ki, km, hy_skip, hy_norm_w[None])
    y_ml = _mlstm(proj3, ml_conv_w, ml_conv_b[None], grow, gtr, ml_norm_w[None])
    return y_hy.reshape(t, D_HY), y_ml.reshape(t, D_ML), x2d


def _moe_tables(top_i, slot, counts):
    t = top_i.shape[0]
    n_tiles = N_EXPERTS + (t * TOP_K) // EXP_ROWS
    ntile = (counts + EXP_ROWS - 1) // EXP_ROWS
    ends = jnp.cumsum(ntile)
    starts = ends - ntile
    total = ends[-1]
    s_idx = jnp.arange(n_tiles, dtype=I32)
    valid = s_idx < total
    s_eff = jnp.where(valid, s_idx, jnp.maximum(total - 1, 0))
    tile_e = jnp.minimum(jnp.sum((s_eff[:, None] >= ends[None, :]).astype(I32), axis=1), N_EXPERTS - 1)
    local = s_eff - starts[tile_e]
    tile_rows = jnp.where(valid, jnp.clip(counts[tile_e] - local * EXP_ROWS, 0, EXP_ROWS), 0).astype(I32)
    tok = jnp.arange(t, dtype=I32)[:, None]
    row_tok = jnp.pad(jnp.sort((top_i * t + tok).reshape(-1)) % t, (0, EXP_ROWS))
    first = jnp.cumsum(counts) - counts
    tile_base = (first[tile_e] + local * EXP_ROWS).astype(I32)
    onehot = top_i[:, :, None] == jnp.arange(N_EXPERTS, dtype=I32)
    dest = jnp.sum(jnp.where(onehot, starts * EXP_ROWS, 0), axis=-1) + slot
    return (tile_e.astype(I32), tile_rows, tile_base, row_tok.astype(I32), total.astype(I32),
            dest.astype(I32).reshape(-1))


def kernel(x, w_in, b_in, hy_conv_w, hy_conv_b, hy_filt_w1, hy_filt_b1, hy_filt_w2, hy_filt_b2, hy_filt_w3, hy_filt_freq, hy_skip, hy_norm_w, ml_conv_w, ml_conv_b, ml_norm_w, w_out, b_out, ln1_g, ln1_b, router_w, router_b, w_gu, b_gu, w_down, b_down, ln2_g, ln2_b):
    bsz, seq, _ = x.shape
    l = 0
    y_hy, y_ml, x2d = _mixer(x, w_in[l], b_in[l], hy_conv_w[l], hy_conv_b[l], hy_filt_w1[l], hy_filt_b1[l],
                             hy_filt_w2[l], hy_filt_b2[l], hy_filt_w3[l], hy_filt_freq[l], hy_skip[l],
                             hy_norm_w[l], ml_conv_w[l], ml_conv_b[l], ml_norm_w[l])
    rw = jnp.pad(router_w[l], ((0, 0), (0, LANES - N_EXPERTS)))
    rb = jnp.pad(router_b[l][None], ((0, 0), (0, LANES - N_EXPERTS)), constant_values=-1e30)
    x1, x1c, top_i, top_g, slot, cnt = _out_proj_ln_route(
        y_hy, y_ml, x2d, w_out[l].astype(BF16), b_out[l][None], ln1_g[l][None], ln1_b[l][None], rw, rb)
    counts = cnt[0, :N_EXPERTS].astype(I32)
    tile_e, tile_rows, tile_base, row_tok, used, dest = _moe_tables(top_i[:, :TOP_K], slot[:, :TOP_K], counts)
    y_buf = _experts(tile_e, tile_rows, tile_base, row_tok, used, x1c, w_gu[l], b_gu[l][:, None, :], w_down[l],
                     b_down[l][:, None, :])
    out = _combine_ln(dest, y_buf, x1, top_g, ln2_g[l][None], ln2_b[l][None])
    return out.reshape(bsz, seq, D_MODEL)
```

```python
import functools
import math

import jax
import jax.numpy as jnp
from jax import lax
from jax.experimental import pallas as pl
from jax.experimental.pallas import tpu as pltpu

F32 = jnp.float32
BF16 = jnp.bfloat16
I32 = jnp.int32
U32 = jnp.uint32
HP = lax.Precision.HIGHEST

D_MODEL = 2048
D_HY = 1024
D_ML = 1024
ML_HEADS = 8
HEAD_DIM = 128
CHUNK = 128
N_GATE_COLS = 32
HY_EMB = 33
N_EXPERTS = 32
TOP_K = 4
D_FF = 2048
SWIGLU_LIMIT = 7.0
SWIGLU_ALPHA = 1.702
LN_EPS = 1e-5
DN_ALPHA = 2.0 ** 0.25
HY_DECAY_TARGET = 1e-2
HY_FAST_PCT = 0.3
HY_SLOW_PCT = 1.5

LANES = 128
ROW_CHUNKS = D_MODEL // (2 * LANES)
VMEM_LIMIT = 60 * 1024 * 1024

PROJ_TM = 2048
PROJ_TN = 512
PROJ_GATE_ROWS = 256
HY_CW = 256
HY_FC = 512
OUT_TM = 512
EXP_ROWS = 1152
EXP_M_SIZES = (1024, 1088, 1152)
EXP_CHUNK = 128
EXP_TF = 256
GATHER_UNROLL = 8
CMB_TM = 128
DFT_TB = 128


def _cparams(sem):
    return pltpu.CompilerParams(dimension_semantics=sem, vmem_limit_bytes=VMEM_LIMIT)


def _split(a):
    hi = a.astype(BF16)
    return hi, (a - hi.astype(F32)).astype(BF16)


def _dot3(a, b, dims):
    a_hi, a_lo = _split(a)
    b_hi, b_lo = _split(b)
    mm = functools.partial(lax.dot_general, dimension_numbers=dims, preferred_element_type=F32)
    return mm(a_hi, b_hi) + (mm(a_hi, b_lo) + mm(a_lo, b_hi))


NN = (((1,), (0,)), ((), ()))
NT = (((1,), (1,)), ((), ()))


def _const_spec(shape, index_map):
    return pl.BlockSpec(shape, index_map, pipeline_mode=pl.Buffered(1))


def _proj_kernel(x_ref, w_ref, b_ref, wg_ref, bg_ref, o_ref, og_ref, xb_ref):
    @pl.when(pl.program_id(1) == 0)
    def _():
        xb_ref[...] = x_ref[...].astype(BF16)
        wg_hi, wg_lo = _split(wg_ref[...])
        mm = functools.partial(lax.dot_general, dimension_numbers=NT, preferred_element_type=F32)
        for r in range(x_ref.shape[0] // PROJ_GATE_ROWS):
            rows = slice(r * PROJ_GATE_ROWS, (r + 1) * PROJ_GATE_ROWS)
            x_hi = xb_ref[rows, :]
            x_lo = (x_ref[rows, :] - x_hi.astype(F32)).astype(BF16)
            og_ref[rows, :] = mm(x_hi, wg_hi) + (mm(x_hi, wg_lo) + mm(x_lo, wg_hi)) + bg_ref[...]

    acc = lax.dot_general(xb_ref[...], w_ref[...], NT, preferred_element_type=F32)
    o_ref[...] = (acc + b_ref[...]).astype(o_ref.dtype)


def _in_proj(x2d, wt_bf, b_row, wt_gate, b_gate):
    m, k = x2d.shape
    n = wt_bf.shape[0]
    return pl.pallas_call(
        _proj_kernel,
        grid=(m // PROJ_TM, n // PROJ_TN),
        in_specs=[pl.BlockSpec((PROJ_TM, k), lambda i, j: (i, 0)),
                  pl.BlockSpec((PROJ_TN, k), lambda i, j: (j, 0)),
                  pl.BlockSpec((1, PROJ_TN), lambda i, j: (0, j)),
                  _const_spec((LANES, k), lambda i, j: (0, 0)),
                  _const_spec((1, LANES), lambda i, j: (0, 0))],
        out_specs=[pl.BlockSpec((PROJ_TM, PROJ_TN), lambda i, j: (i, j)),
                   pl.BlockSpec((PROJ_TM, LANES), lambda i, j: (i, 0))],
        out_shape=[jax.ShapeDtypeStruct((m, n), BF16), jax.ShapeDtypeStruct((m, LANES), F32)],
        scratch_shapes=[pltpu.VMEM((PROJ_TM, k), BF16)],
        compiler_params=_cparams(("arbitrary", "arbitrary")),
        name="in_proj")(x2d, wt_bf, b_row, wt_gate, b_gate)


def _filter_kernel(z_ref, w1_ref, b1_ref, w2_ref, b2_ref, fq_ref, w3f_ref, w3b_ref, dl_ref,
                   c_ref, s_ref, kr_ref, ki_ref, km_ref, h_ref):
    seq = z_ref.shape[0]
    inv_n = 1.0 / (2 * seq)
    z = z_ref[...]

    @pl.when(jnp.logical_and(pl.program_id(0) == 0, pl.program_id(1) == 0))
    def _():
        h1 = jnp.sin(fq_ref[0:1, :] * (jnp.dot(z, w1_ref[...], precision=HP, preferred_element_type=F32)
                                       + b1_ref[...]))
        h_ref[...] = jnp.sin(fq_ref[1:2, :] * (jnp.dot(h1, w2_ref[...], precision=HP, preferred_element_type=F32)
                                               + b2_ref[...]))

    half = seq // 2
    h = h_ref[...]
    win = jnp.exp(-z[:, 0:1] * dl_ref[...])
    fwd = _dot3(h, w3f_ref[...], NN) * win
    bwd = _dot3(h, w3b_ref[...], NN) * win
    row = lax.broadcasted_iota(I32, fwd.shape, 0)
    bwd = jnp.where(row == 0, 0.0, bwd)
    inv = 1.0 / jnp.sum(jnp.abs(fwd) + jnp.abs(bwd), axis=0, keepdims=True)
    ks = ((fwd + bwd) * inv)
    kd = ((fwd - bwd) * inv)
    ksb = ks.astype(BF16)
    kdb = kd.astype(BF16)
    ec = jnp.dot(c_ref[0], ksb[:half], preferred_element_type=F32)
    oc = jnp.dot(c_ref[1], ksb[half:], preferred_element_type=F32)
    es = jnp.dot(s_ref[0], kdb[:half], preferred_element_type=F32)
    os_ = jnp.dot(s_ref[1], kdb[half:], preferred_element_type=F32)
    rowh = lax.broadcasted_iota(I32, ec.shape, 0)
    wf = jnp.where(rowh == 0, inv_n, 2.0 * inv_n)
    kr_ref[0, :half, :] = (ec + oc) * wf
    kr_ref[0, half:, :] = (ec - oc) * wf
    ki_ref[0, :half, :] = -(es + os_) * wf
    ki_ref[0, half:, :] = (es - os_) * wf
    alt = jnp.where((rowh & 1) == 0, 1.0, -1.0)
    km_ref[0, 0:1, :] = jnp.sum(ks[:half] * alt, axis=0, keepdims=True) * (2.0 * inv_n)
    km_ref[0, 1:2, :] = -jnp.sum(kd[half:] * alt, axis=0, keepdims=True) * (2.0 * inv_n)


def _hyena_filters(zpad, w1pad, b1, w2, b2, freq, w3, deltas, cmat, smat):
    seq = zpad.shape[0]
    nb = D_HY // HY_CW
    hid = w2.shape[0]
    full = lambda shape: pl.BlockSpec(shape, lambda o, c: (0,) * len(shape))
    out_sds = jax.ShapeDtypeStruct((2, seq, D_HY), F32)
    return pl.pallas_call(
        _filter_kernel,
        grid=(2, nb),
        in_specs=[full(zpad.shape), full(w1pad.shape), full(b1.shape), full(w2.shape), full(b2.shape),
                  full(freq.shape),
                  pl.BlockSpec((hid, HY_CW), lambda o, c: (0, o * 2 * nb + c)),
                  pl.BlockSpec((hid, HY_CW), lambda o, c: (0, o * 2 * nb + nb + c)),
                  pl.BlockSpec((1, HY_CW), lambda o, c: (0, c)),
                  _const_spec(cmat.shape, lambda o, c: (0, 0, 0)),
                  _const_spec(smat.shape, lambda o, c: (0, 0, 0))],
        out_specs=[pl.BlockSpec((1, seq, HY_CW), lambda o, c: (o, 0, c)),
                   pl.BlockSpec((1, seq, HY_CW), lambda o, c: (o, 0, c)),
                   pl.BlockSpec((1, 2, HY_CW), lambda o, c: (o, 0, c))],
        out_shape=[out_sds, out_sds, jax.ShapeDtypeStruct((2, 2, D_HY), F32)],
        scratch_shapes=[pltpu.VMEM((seq, hid), F32)],
        compiler_params=_cparams(("arbitrary", "arbitrary")),
        name="hyena_filters")(zpad, w1pad, b1, w2, b2, freq, w3, w3, deltas, cmat, smat)


def _short_conv(u, w_ref, b_ref, row, seq):
    prev = jnp.where(row == 0, 0.0, pltpu.roll(u, 1, 0))
    nxt = jnp.where(row == seq - 1, 0.0, pltpu.roll(u, seq - 1, 0))
    return w_ref[0:1, :] * prev + w_ref[1:2, :] * u + w_ref[2:3, :] * nxt + b_ref[...]


def _hyena_kernel(uv_ref, u1_ref, u2_ref, wv_ref, w1_ref, w2_ref, bv_ref, b1_ref, b2_ref,
                  c_ref, s_ref, kr_ref, ki_ref, km_ref, skip_ref, nw_ref, o_ref,
                  a_ref, b_ref, t_ref, zb_ref, s1_ref, s2_ref, s3_ref, s4_ref, md_ref):
    seq = uv_ref.shape[1]
    cw = uv_ref.shape[2]
    half = seq // 2
    nblk = half // HY_FC
    row = lax.broadcasted_iota(I32, (seq, LANES), 0)
    alt_half = jnp.where((lax.broadcasted_iota(I32, (half, cw), 0) & 1) == 0, 1.0, -1.0)
    alt_blk = jnp.where((lax.broadcasted_iota(I32, (HY_FC, cw), 0) & 1) == 0, 1.0, -1.0)
    groups = [slice(g * LANES, (g + 1) * LANES) for g in range(cw // LANES)]

    def conv_to(dst_ref, u_ref, w_ref, bias_ref):
        for gi, gs in enumerate(groups):
            t_ref[gi] = _short_conv(u_ref[0, :, gs].astype(F32), w_ref.at[:, gs], bias_ref.at[:, gs], row, seq)
            dst_ref[0:half, gs] = t_ref[gi, pl.ds(0, half, stride=2), :]
            dst_ref[half:seq, gs] = t_ref[gi, pl.ds(1, half, stride=2), :]

    def spectrum(zin_ref, o):
        z = zin_ref[...]
        zb_ref[...] = z.astype(BF16)
        am = jnp.sum(z[:half] * alt_half, axis=0, keepdims=True)
        bm = jnp.sum(z[half:] * alt_half, axis=0, keepdims=True)
        krm, kim = km_ref[o, 0:1, :], km_ref[o, 1:2, :]
        md_ref[0:1, :] = am * krm + bm * kim
        md_ref[1:2, :] = am * kim - bm * krm
        ze = zb_ref[0:half, :]
        zo = zb_ref[half:seq, :]
        for gb in range(nblk):
            lo = slice(gb * HY_FC, (gb + 1) * HY_FC)
            hi = slice(half + gb * HY_FC, half + (gb + 1) * HY_FC)
            ec = jnp.dot(c_ref[0, lo, :], ze, preferred_element_type=F32)
            oc = jnp.dot(c_ref[1, lo, :], zo, preferred_element_type=F32)
            es = jnp.dot(s_ref[0, lo, :], ze, preferred_element_type=F32)
            os_ = jnp.dot(s_ref[1, lo, :], zo, preferred_element_type=F32)
            a_lo, b_lo = ec + oc, es + os_
            a_hi, b_hi = ec - oc, os_ - es
            krl, kil = kr_ref[o, lo, :], ki_ref[o, lo, :]
            krh, kih = kr_ref[o, hi, :], ki_ref[o, hi, :]
            pr = a_lo * krl + b_lo * kil
            pi = a_lo * kil - b_lo * krl
            qr = a_hi * krh + b_hi * kih
            qi = a_hi * kih - b_hi * krh
            s1_ref[lo, :] = (pr + qr).astype(BF16)
            s2_ref[lo, :] = (pi - qi).astype(BF16)
            s3_ref[lo, :] = (pr - qr).astype(BF16)
            s4_ref[lo, :] = (pi + qi).astype(BF16)

    def conv_rows(ub, parity):
        us = slice(ub * HY_FC, (ub + 1) * HY_FC)
        if parity == 0:
            y = jnp.dot(c_ref[0, us, :], s1_ref[...], preferred_element_type=F32)
            y = y - jnp.dot(s_ref[0, us, :], s2_ref[...], preferred_element_type=F32)
            y = y + md_ref[0:1, :] * alt_blk
        else:
            y = jnp.dot(c_ref[2, us, :], s3_ref[...], preferred_element_type=F32)
            y = y - jnp.dot(s_ref[2, us, :], s4_ref[...], preferred_element_type=F32)
            y = y - md_ref[1:2, :] * alt_blk
        return slice(parity * half + ub * HY_FC, parity * half + (ub + 1) * HY_FC), y

    blocks = [(ub, parity) for parity in range(2) for ub in range(nblk)]
    conv_to(a_ref, uv_ref, wv_ref, bv_ref)
    conv_to(b_ref, u1_ref, w1_ref, b1_ref)
    spectrum(a_ref, 0)
    for ub, parity in blocks:
        rows, y = conv_rows(ub, parity)
        b_ref[rows, :] = b_ref[rows, :] * (y + skip_ref[0:1, :] * a_ref[rows, :])
    conv_to(a_ref, u2_ref, w2_ref, b2_ref)
    spectrum(b_ref, 1)
    for ub, parity in blocks:
        rows, y = conv_rows(ub, parity)
        z = a_ref[rows, :] * (y + skip_ref[1:2, :] * b_ref[rows, :])
        for gi, gs in enumerate(groups):
            zg = z[:, gs]
            mu = jnp.mean(zg, axis=-1, keepdims=True)
            zc = zg - mu
            var = jnp.mean(zc * zc, axis=-1, keepdims=True)
            t_ref[gi, pl.ds(2 * ub * HY_FC + parity, HY_FC, stride=2), :] = (
                zc * lax.rsqrt(var + LN_EPS) * nw_ref[:, gs])
    for gi, gs in enumerate(groups):
        o_ref[0, :, gs] = t_ref[gi].astype(o_ref.dtype)


def _hyena(proj3, conv_w, conv_b, cmat, smat, kr, ki, km, skip, norm_w):
    bsz, seq, _ = proj3.shape
    half = seq // 2
    nb = D_HY // HY_CW
    u_spec = lambda off: pl.BlockSpec((1, seq, HY_CW), lambda c, b: (b, 0, off + c))
    w_spec = lambda off: pl.BlockSpec((3, HY_CW), lambda c, b: (0, off + c))
    b_spec = lambda off: pl.BlockSpec((1, HY_CW), lambda c, b: (0, off + c))
    return pl.pallas_call(
        _hyena_kernel,
        grid=(nb, bsz),
        in_specs=[u_spec(0), u_spec(nb), u_spec(2 * nb),
                  w_spec(0), w_spec(nb), w_spec(2 * nb),
                  b_spec(0), b_spec(nb), b_spec(2 * nb),
                  _const_spec(cmat.shape, lambda c, b: (0, 0, 0)),
                  _const_spec(smat.shape, lambda c, b: (0, 0, 0)),
                  _const_spec((2, seq, HY_CW), lambda c, b: (0, 0, c)),
                  _const_spec((2, seq, HY_CW), lambda c, b: (0, 0, c)),
                  pl.BlockSpec((2, 2, HY_CW), lambda c, b: (0, 0, c)),
                  pl.BlockSpec((2, HY_CW), lambda c, b: (0, c)),
                  pl.BlockSpec((1, HY_CW), lambda c, b: (0, c))],
        out_specs=pl.BlockSpec((1, seq, HY_CW), lambda c, b: (b, 0, c)),
        out_shape=jax.ShapeDtypeStruct((bsz, seq, D_HY), BF16),
        scratch_shapes=[pltpu.VMEM((seq, HY_CW), F32), pltpu.VMEM((seq, HY_CW), F32),
                        pltpu.VMEM((HY_CW // LANES, seq, LANES), F32), pltpu.VMEM((seq, HY_CW), BF16),
                        pltpu.VMEM((half, HY_CW), BF16), pltpu.VMEM((half, HY_CW), BF16),
                        pltpu.VMEM((half, HY_CW), BF16), pltpu.VMEM((half, HY_CW), BF16),
                        pltpu.VMEM((2, HY_CW), F32)],
        compiler_params=_cparams(("arbitrary", "arbitrary")),
        name="hyena")(proj3, proj3, proj3, conv_w, conv_w, conv_w, conv_b, conv_b, conv_b,
                      cmat, smat, kr, ki, km, skip, norm_w)


def _mlstm_kernel(qp_ref, kp_ref, v_ref, og_ref, wq_ref, wk_ref, bq_ref, bk_ref, gr_ref, gt_ref,
                  nw_ref, o_ref, qb_ref, kb_ref, hacc_ref, cst_ref, nst_ref):
    seq = qp_ref.shape[1]
    d = qp_ref.shape[2]
    nchunk = seq // CHUNK
    row = lax.broadcasted_iota(I32, (seq, d), 0)
    q = _short_conv(qp_ref[0].astype(F32), wq_ref, bq_ref, row, seq)
    k = _short_conv(kp_ref[0].astype(F32), wk_ref, bk_ref, row, seq)
    qb_ref[...] = (q * jax.nn.sigmoid(q)).astype(BF16)
    kb_ref[...] = ((k * jax.nn.sigmoid(k)) * (d ** -0.5)).astype(BF16)

    ti = lax.broadcasted_iota(I32, (CHUNK, CHUNK), 0)
    si = lax.broadcasted_iota(I32, (CHUNK, CHUNK), 1)
    lower = ti >= si
    upper = ti <= si
    lower_f = lower.astype(F32)
    upper_f = upper.astype(F32)
    nt = (((1,), (1,)), ((), ()))
    tn = (((0,), (0,)), ((), ()))
    chunk_rows = [slice(c * CHUNK, (c + 1) * CHUNK) for c in range(nchunk)]

    for direction in range(2):
        f_idx, i_idx = 2 * direction + 1, 2 * direction
        mask = lower if direction == 0 else upper
        order = list(range(nchunk)) if direction == 0 else list(range(nchunk - 1, -1, -1))
        lf_r = jax.nn.log_sigmoid(gr_ref[0, 0, f_idx])
        b_r = jnp.dot(lf_r, upper_f if direction == 0 else lower_f, precision=HP, preferred_element_type=F32)
        rterm = b_r - gr_ref[0, 0, i_idx]
        b_last = jnp.sum(lf_r, axis=-1, keepdims=True)
        lf_c = jax.nn.log_sigmoid(gt_ref[0, 0, f_idx])
        b_c = jnp.dot(lower_f if direction == 0 else upper_f, lf_c, precision=HP, preferred_element_type=F32)
        i_c = gt_ref[0, 0, i_idx]

        bcol, gcol, gmax, rowmax, blast = [], [], [], [], []
        for c in range(nchunk):
            bc = jnp.broadcast_to(b_c[:, c:c + 1], (CHUNK, CHUNK))
            ic = jnp.broadcast_to(i_c[:, c:c + 1], (CHUNK, CHUNK))
            bl = jnp.broadcast_to(b_last[c:c + 1, :], (1, CHUNK))
            dmat = jnp.where(mask, bc - rterm[c:c + 1, :], -jnp.inf)
            g = bl - bc + ic
            bcol.append(bc)
            gcol.append(g)
            blast.append(bl)
            rowmax.append(jnp.max(dmat, axis=-1, keepdims=True))
            gmax.append(jnp.max(g, axis=0, keepdims=True))

        m = jnp.zeros((1, CHUNK), F32)
        m_in, m_out = [None] * nchunk, [None] * nchunk
        for c in order:
            m_in[c] = m
            m = jnp.maximum(blast[c] + m, gmax[c])
            m_out[c] = m

        cmat = jnp.zeros((d, d), F32)
        nvec = jnp.zeros((1, d), F32)
        for c in order:
            kc = kb_ref[chunk_rows[c], :]
            vc = v_ref[0, chunk_rows[c], :]
            cst_ref[c] = cmat.astype(BF16)
            nst_ref[c] = nvec
            wg = jnp.exp(gcol[c] - m_out[c])
            decay = jnp.exp(blast[c] + m_in[c] - m_out[c])
            upd = lax.dot_general((wg * vc.astype(F32)).astype(BF16), kc, tn, preferred_element_type=F32)
            cmat = decay * cmat + upd
            nvec = decay * nvec + jnp.sum(wg * kc.astype(F32), axis=0, keepdims=True)

        for c in range(nchunk):
            rs = chunk_rows[c]
            qc = qb_ref[rs, :]
            kc = kb_ref[rs, :]
            vc = v_ref[0, rs, :]
            dmat = jnp.where(mask, bcol[c] - rterm[c:c + 1, :], -jnp.inf)
            inter = bcol[c] + m_in[c]
            m_t = jnp.maximum(inter, rowmax[c])
            p = jnp.exp(dmat - m_t)
            inter_w = jnp.exp(inter - m_t)
            s = lax.dot_general(qc, kc, nt, preferred_element_type=F32) * p
            cq = lax.dot_general(qc, cst_ref[c], nt, preferred_element_type=F32)
            num = jnp.dot(s.astype(BF16), vc, preferred_element_type=F32) + inter_w * cq
            nq = jnp.sum(qc.astype(F32) * nst_ref[c], axis=-1, keepdims=True)
            den = jnp.sum(s, axis=-1, keepdims=True) + inter_w * nq
            h = num / jnp.maximum(jnp.abs(den), jnp.exp(-m_t))
            if direction == 0:
                hacc_ref[rs, :] = h
            else:
                hacc_ref[rs, :] += h

    h = hacc_ref[...]
    mu = jnp.mean(h, axis=-1, keepdims=True)
    hc = h - mu
    var = jnp.mean(hc * hc, axis=-1, keepdims=True)
    y = hc * lax.rsqrt(var + LN_EPS) * nw_ref[...] * jax.nn.sigmoid(og_ref[0].astype(F32))
    o_ref[0] = y.astype(o_ref.dtype)


def _mlstm(proj3, conv_w, conv_b, grow, gtr, norm_w):
    bsz, seq, _ = proj3.shape
    d = HEAD_DIM
    nchunk = seq // CHUNK
    hy_blocks = 3 * D_HY // d
    qoff, koff, voff, ooff = hy_blocks, hy_blocks + ML_HEADS, hy_blocks + 2 * ML_HEADS, hy_blocks + 3 * ML_HEADS
    p_spec = lambda off: pl.BlockSpec((1, seq, d), lambda b, h: (b, 0, off + h))
    return pl.pallas_call(
        _mlstm_kernel,
        grid=(bsz, ML_HEADS),
        in_specs=[p_spec(qoff), p_spec(koff), p_spec(voff), p_spec(ooff),
                  pl.BlockSpec((3, d), lambda b, h: (0, h)),
                  pl.BlockSpec((3, d), lambda b, h: (0, ML_HEADS + h)),
                  pl.BlockSpec((1, d), lambda b, h: (0, h)),
                  pl.BlockSpec((1, d), lambda b, h: (0, ML_HEADS + h)),
                  pl.BlockSpec((1, 1, 4, nchunk, CHUNK), lambda b, h: (b, h, 0, 0, 0)),
                  pl.BlockSpec((1, 1, 4, CHUNK, nchunk), lambda b, h: (b, h, 0, 0, 0)),
                  pl.BlockSpec((1, d), lambda b, h: (0, h))],
        out_specs=pl.BlockSpec((1, seq, d), lambda b, h: (b, 0, h)),
        out_shape=jax.ShapeDtypeStruct((bsz, seq, D_ML), BF16),
        scratch_shapes=[pltpu.VMEM((seq, d), BF16), pltpu.VMEM((seq, d), BF16),
                        pltpu.VMEM((seq, d), F32),
                        pltpu.VMEM((nchunk, d, d), BF16), pltpu.VMEM((nchunk, 1, d), F32)],
        compiler_params=_cparams(("arbitrary", "arbitrary")),
        name="mlstm")(proj3, proj3, proj3, proj3, conv_w, conv_w, conv_b, conv_b, grow, gtr, norm_w)


def _to_slabs(y):
    n = y.shape[0]
    half = D_MODEL // 2
    lo = lax.bitcast_convert_type(y[:, :half].astype(BF16).astype(F32), U32) >> 16
    hi = lax.bitcast_convert_type(y[:, half:].astype(BF16).astype(F32), U32) & jnp.uint32(0xFFFF0000)
    words = hi | lo
    parts = jnp.stack([words[:, c * LANES:(c + 1) * LANES] for c in range(ROW_CHUNKS)], axis=0)
    return pltpu.einshape("crl->rcl", parts).reshape(n * ROW_CHUNKS, LANES)


def _from_slabs(v):
    n = v.shape[0] // ROW_CHUNKS
    parts = pltpu.einshape("rcl->crl", v.reshape(n, ROW_CHUNKS, LANES))
    lo = [lax.bitcast_convert_type(parts[c] << 16, F32) for c in range(ROW_CHUNKS)]
    hi = [lax.bitcast_convert_type(parts[c] & jnp.uint32(0xFFFF0000), F32) for c in range(ROW_CHUNKS)]
    return jnp.concatenate(lo + hi, axis=-1)


def _layer_norm(u, g, b):
    mu = jnp.mean(u, axis=-1, keepdims=True)
    uc = u - mu
    var = jnp.mean(uc * uc, axis=-1, keepdims=True)
    return uc * lax.rsqrt(var + LN_EPS) * g + b


def _route(x, w_ref, b_ref, ti_ref, tg_ref, tp_ref, cnt_ref):
    tm = ti_ref.shape[0]

    @pl.when(pl.program_id(0) == 0)
    def _():
        cnt_ref[...] = jnp.zeros_like(cnt_ref)

    logits = _dot3(x, w_ref[...], NN) + b_ref[...]
    lane = lax.broadcasted_iota(I32, (tm, LANES), 1)
    work = logits
    vals, idxs = [], []
    chosen = jnp.zeros((tm, LANES), F32)
    for _ in range(TOP_K):
        mx = jnp.max(work, axis=-1, keepdims=True)
        idx = jnp.min(jnp.where(work == mx, lane, LANES), axis=-1, keepdims=True)
        hit = lane == idx
        vals.append(mx)
        idxs.append(idx)
        chosen = jnp.where(hit, 1.0, chosen)
        work = jnp.where(hit, -jnp.inf, work)
    exps = [jnp.exp(v - vals[0]) for v in vals]
    den = exps[0] + exps[1] + exps[2] + exps[3]
    ri = lax.broadcasted_iota(I32, (tm, tm), 0)
    ci = lax.broadcasted_iota(I32, (tm, tm), 1)
    strict_lower = (ri > ci).astype(BF16)
    carry = cnt_ref[...]
    slot = carry + jnp.dot(strict_lower, chosen.astype(BF16), preferred_element_type=F32)
    ti = jnp.zeros((tm, LANES), I32)
    tg = jnp.zeros((tm, LANES), F32)
    tp = jnp.zeros((tm, LANES), F32)
    for k in range(TOP_K):
        sk = jnp.sum(jnp.where(lane == idxs[k], slot, 0.0), axis=-1, keepdims=True)
        ti = jnp.where(lane == k, idxs[k], ti)
        tg = jnp.where(lane == k, exps[k] / den, tg)
        tp = jnp.where(lane == k, sk, tp)
    ti_ref[...] = ti
    tg_ref[...] = tg
    tp_ref[...] = tp.astype(I32)
    cnt_ref[...] = carry + jnp.sum(chosen, axis=0, keepdims=True)


def _outproj_kernel(yh_ref, ym_ref, x_ref, wa_ref, wb_ref, b_ref, g_ref, be_ref, rw_ref, rb_ref,
                    o_ref, oc_ref, ti_ref, tg_ref, tp_ref, cnt_ref):
    mix = (jnp.dot(yh_ref[...], wa_ref[...], preferred_element_type=F32)
           + jnp.dot(ym_ref[...], wb_ref[...], preferred_element_type=F32) + b_ref[...])
    y = _layer_norm(DN_ALPHA * x_ref[...] + mix, g_ref[...], be_ref[...])
    o_ref[...] = y
    oc_ref[...] = _to_slabs(y)
    _route(y, rw_ref, rb_ref, ti_ref, tg_ref, tp_ref, cnt_ref)


def _out_proj_ln_route(y_hy, y_ml, x2d, w_out_bf, b_out, g, be, rw_pad, rb_pad):
    t = x2d.shape[0]
    tm = OUT_TM
    vec = lambda: pl.BlockSpec((1, D_MODEL), lambda i: (0, 0))
    lane_blk = lambda: pl.BlockSpec((tm, LANES), lambda i: (i, 0))
    return pl.pallas_call(
        _outproj_kernel,
        grid=(t // tm,),
        in_specs=[pl.BlockSpec((tm, D_HY), lambda i: (i, 0)),
                  pl.BlockSpec((tm, D_ML), lambda i: (i, 0)),
                  pl.BlockSpec((tm, D_MODEL), lambda i: (i, 0)),
                  _const_spec((D_HY, D_MODEL), lambda i: (0, 0)),
                  _const_spec((D_ML, D_MODEL), lambda i: (1, 0)),
                  vec(), vec(), vec(),
                  _const_spec((D_MODEL, LANES), lambda i: (0, 0)),
                  pl.BlockSpec((1, LANES), lambda i: (0, 0))],
        out_specs=[pl.BlockSpec((tm, D_MODEL), lambda i: (i, 0)),
                   pl.BlockSpec((tm * ROW_CHUNKS, LANES), lambda i: (i, 0)),
                   lane_blk(), lane_blk(), lane_blk(), pl.BlockSpec((1, LANES), lambda i: (0, 0))],
        out_shape=[jax.ShapeDtypeStruct((t, D_MODEL), F32),
                   jax.ShapeDtypeStruct((t * ROW_CHUNKS, LANES), U32),
                   jax.ShapeDtypeStruct((t, LANES), I32), jax.ShapeDtypeStruct((t, LANES), F32),
                   jax.ShapeDtypeStruct((t, LANES), I32), jax.ShapeDtypeStruct((1, LANES), F32)],
        compiler_params=_cparams(("arbitrary",)),
        name="out_proj_ln1_route")(y_hy, y_ml, x2d, w_out_bf, w_out_bf, b_out, g, be, rw_pad, rb_pad)


def _expert_kernel(te_ref, tr_ref, tb_ref, rt_ref, x_hbm, wg_ref, wu_ref, wd_ref, bg_ref, bu_ref, bd_ref, y_hbm,
                   stage_ref, xb_ref, acc_ref, ring_ref, wgb_ref, wub_ref, wdb_ref, gsem, osem):
    g = pl.program_id(0)
    n_items = pl.num_programs(0) - 1
    nf = D_FF // EXP_TF
    n_tiles = n_items // nf
    item = jnp.maximum(g - 1, 0)
    s = item // nf
    j = item % nf
    rows = tr_ref[s]
    active = jnp.logical_and(g >= 1, rows > 0)
    cast_slot = g % 2
    use_slot = (g + 1) % 2
    slab = EXP_CHUNK * ROW_CHUNKS
    per_step = EXP_ROWS // nf

    def row_copy(tok, r):
        return pltpu.make_async_copy(
            x_hbm.at[pl.ds(pl.multiple_of(tok * ROW_CHUNKS, ROW_CHUNKS), ROW_CHUNKS), :],
            stage_ref.at[pl.ds(pl.multiple_of(r * ROW_CHUNKS, ROW_CHUNKS), ROW_CHUNKS), :], gsem)

    def wait_gather():
        pltpu.make_async_copy(x_hbm.at[pl.ds(0, EXP_ROWS * ROW_CHUNKS), :], stage_ref, gsem).wait()

    def cast_weights():
        wgb_ref[cast_slot] = wg_ref[0].astype(BF16)
        wub_ref[cast_slot] = wu_ref[0].astype(BF16)
        wdb_ref[cast_slot] = wd_ref[0].astype(BF16)

    @pl.when(g == 0)
    def _():
        stage_ref[...] = jnp.zeros_like(stage_ref)
        acc_ref[...] = jnp.zeros_like(acc_ref)
        base = tb_ref[0]

        def group(q, carry):
            for u in range(GATHER_UNROLL):
                r = q * GATHER_UNROLL + u
                row_copy(rt_ref[base + r], r).start()
            return carry

        lax.fori_loop(0, EXP_ROWS // GATHER_UNROLL, group, 0)

        cast_weights()

    @pl.when(active)
    def _():
        @pl.when(j == 0)
        def _():
            wait_gather()
            for i in range(EXP_ROWS // EXP_CHUNK):
                xb_ref[i * EXP_CHUNK:(i + 1) * EXP_CHUNK, :] = _from_slabs(
                    stage_ref[i * slab:(i + 1) * slab, :]).astype(BF16)

        def step_body(m):
            cast_weights()
            nxt_base = tb_ref[jnp.minimum(s + 1, n_tiles - 1)]
            for u in range(per_step):
                r = j * per_step + u
                row_copy(rt_ref[nxt_base + r], r).start()

            xb = xb_ref[0:m, :]
            gate = jnp.dot(xb, wgb_ref[use_slot], preferred_element_type=F32) + bg_ref[0]
            up = jnp.dot(xb, wub_ref[use_slot], preferred_element_type=F32) + bu_ref[0]
            gate = jnp.minimum(gate, SWIGLU_LIMIT)
            up = jnp.clip(up, -SWIGLU_LIMIT, SWIGLU_LIMIT)
            act = (up + 1.0) * (gate * jax.nn.sigmoid(SWIGLU_ALPHA * gate))
            part = jnp.dot(act.astype(BF16), wdb_ref[use_slot], preferred_element_type=F32)
            acc_ref[0:m, :] = jnp.where(j == 0, jnp.broadcast_to(bd_ref[0], part.shape), acc_ref[0:m, :]) + part

        lo = 0
        for m in EXP_M_SIZES:
            @pl.when(jnp.logical_and(rows > lo, rows <= m))
            def _(m=m):
                step_body(m)
            lo = m

        @pl.when(j == nf - 1)
        def _():
            nchunk = (rows + EXP_CHUNK - 1) // EXP_CHUNK

            def chunk_copy(i, slot):
                dst0 = pl.multiple_of((s * EXP_ROWS + i * EXP_CHUNK) * ROW_CHUNKS, slab)
                return pltpu.make_async_copy(ring_ref.at[slot], y_hbm.at[pl.ds(dst0, slab), :], osem.at[slot])

            def emit(i, carry):
                slot = i % 2

                @pl.when(i >= 2)
                def _():
                    chunk_copy(i - 2, slot).wait()

                r0 = pl.multiple_of(i * EXP_CHUNK, EXP_CHUNK)
                ring_ref[slot] = _to_slabs(acc_ref[pl.ds(r0, EXP_CHUNK), :])
                chunk_copy(i, slot).start()
                return carry

            lax.fori_loop(0, nchunk, emit, 0)
            for back in range(2):
                @pl.when(nchunk > back)
                def _():
                    last = nchunk - 1 - back
                    chunk_copy(last, last % 2).wait()

    @pl.when(g == n_items)
    def _():
        wait_gather()


def _experts(tile_e, tile_rows, tile_base, row_tok, used_tiles, x1c, w_gu, b_gu, w_down, b_down):
    n_tiles = tile_e.shape[0]
    nf = D_FF // EXP_TF
    n_items = n_tiles * nf

    def item_block(item, te, tr):
        s = item // nf
        return te[s], jnp.where(tr[s] > 0, item % nf, nf - 1)

    def cast_item(g, te, tr):
        return item_block(jnp.minimum(g, n_items - 1), te, tr)

    def use_item(g, te, tr):
        return item_block(jnp.maximum(g - 1, 0), te, tr)

    def w_gate(g, te, tr, tb, rt):
        e, j = cast_item(g, te, tr)
        return e, 0, j

    def w_up(g, te, tr, tb, rt):
        e, j = cast_item(g, te, tr)
        return e, 0, nf + j

    def w_down_map(g, te, tr, tb, rt):
        e, j = cast_item(g, te, tr)
        return e, j, 0

    def b_gate(g, te, tr, tb, rt):
        e, j = use_item(g, te, tr)
        return e, 0, j

    def b_up(g, te, tr, tb, rt):
        e, j = use_item(g, te, tr)
        return e, 0, nf + j

    def b_down_map(g, te, tr, tb, rt):
        e, _ = use_item(g, te, tr)
        return e, 0, 0

    grid_spec = pltpu.PrefetchScalarGridSpec(
        num_scalar_prefetch=4,
        grid=(used_tiles * nf + 1,),
        in_specs=[pl.BlockSpec(memory_space=pl.ANY),
                  pl.BlockSpec((1, D_MODEL, EXP_TF), w_gate),
                  pl.BlockSpec((1, D_MODEL, EXP_TF), w_up),
                  pl.BlockSpec((1, EXP_TF, D_MODEL), w_down_map),
                  pl.BlockSpec((1, 1, EXP_TF), b_gate),
                  pl.BlockSpec((1, 1, EXP_TF), b_up),
                  pl.BlockSpec((1, 1, D_MODEL), b_down_map)],
        out_specs=pl.BlockSpec(memory_space=pl.ANY),
        scratch_shapes=[pltpu.VMEM((EXP_ROWS * ROW_CHUNKS, LANES), U32),
                        pltpu.VMEM((EXP_ROWS, D_MODEL), BF16),
                        pltpu.VMEM((EXP_ROWS, D_MODEL), F32),
                        pltpu.VMEM((2, EXP_CHUNK * ROW_CHUNKS, LANES), U32),
                        pltpu.VMEM((2, D_MODEL, EXP_TF), BF16),
                        pltpu.VMEM((2, D_MODEL, EXP_TF), BF16),
                        pltpu.VMEM((2, EXP_TF, D_MODEL), BF16),
                        pltpu.SemaphoreType.DMA(()),
                        pltpu.SemaphoreType.DMA((2,))])
    return pl.pallas_call(
        _expert_kernel,
        grid_spec=grid_spec,
        out_shape=jax.ShapeDtypeStruct((n_tiles * EXP_ROWS * ROW_CHUNKS, LANES), U32),
        compiler_params=_cparams(("arbitrary",)),
        name="experts")(tile_e, tile_rows, tile_base, row_tok, x1c, w_gu, w_gu, w_down, b_gu, b_gu, b_down)


def _combine_kernel(n_steps, dest_ref, y_hbm, x_ref, tg_ref, g_ref, be_ref, o_ref, buf_a, buf_b, sem_a, sem_b):
    tm = o_ref.shape[0]
    i = pl.program_id(0)

    def start_tile(tile, buf, sem):
        base = tile * tm * TOP_K
        for t in range(tm):
            for k in range(TOP_K):
                src_row = dest_ref[base + t * TOP_K + k]
                pltpu.make_async_copy(
                    y_hbm.at[pl.ds(pl.multiple_of(src_row * ROW_CHUNKS, ROW_CHUNKS), ROW_CHUNKS), :],
                    buf.at[k, t * ROW_CHUNKS:(t + 1) * ROW_CHUNKS, :], sem).start()

    def wait_tile(buf, sem):
        for k in range(TOP_K):
            pltpu.make_async_copy(y_hbm.at[pl.ds(0, tm * ROW_CHUNKS), :], buf.at[k], sem).wait()

    def step(cur, cur_sem, nxt, nxt_sem):
        wait_tile(cur, cur_sem)
        start_tile(jnp.minimum(i + 1, n_steps - 1), nxt, nxt_sem)
        tg = tg_ref[...]
        ff = jnp.zeros((tm, D_MODEL), F32)
        for k in range(TOP_K):
            ff = ff + tg[:, k:k + 1] * _from_slabs(cur[k])
        o_ref[...] = _layer_norm(DN_ALPHA * x_ref[...] + ff, g_ref[...], be_ref[...])

    @pl.when(i == 0)
    def _():
        start_tile(0, buf_a, sem_a)

    @pl.when(i % 2 == 0)
    def _():
        step(buf_a, sem_a, buf_b, sem_b)

    @pl.when(i % 2 == 1)
    def _():
        step(buf_b, sem_b, buf_a, sem_a)

    @pl.when(i == n_steps - 1)
    def _():
        if (n_steps - 1) % 2 == 0:
            wait_tile(buf_b, sem_b)
        else:
            wait_tile(buf_a, sem_a)


def _combine_ln(dest_flat, y_buf, x1, tg, g, be):
    t = tg.shape[0]
    tm = CMB_TM
    buf = pltpu.VMEM((TOP_K, tm * ROW_CHUNKS, LANES), U32)
    grid_spec = pltpu.PrefetchScalarGridSpec(
        num_scalar_prefetch=1,
        grid=(t // tm,),
        in_specs=[pl.BlockSpec(memory_space=pl.ANY),
                  pl.BlockSpec((tm, D_MODEL), lambda i, d: (i, 0)),
                  pl.BlockSpec((tm, LANES), lambda i, d: (i, 0)),
                  pl.BlockSpec((1, D_MODEL), lambda i, d: (0, 0)),
                  pl.BlockSpec((1, D_MODEL), lambda i, d: (0, 0))],
        out_specs=pl.BlockSpec((tm, D_MODEL), lambda i, d: (i, 0)),
        scratch_shapes=[buf, buf, pltpu.SemaphoreType.DMA(()), pltpu.SemaphoreType.DMA(())])
    return pl.pallas_call(
        functools.partial(_combine_kernel, t // tm),
        grid_spec=grid_spec,
        out_shape=jax.ShapeDtypeStruct((t, D_MODEL), F32),
        compiler_params=_cparams(("arbitrary",)),
        name="combine_ln2")(dest_flat, y_buf, x1, tg, g, be)


def _dft_kernel(cd_ref, sd_ref, ca_ref, sa_ref, c_ref, s_ref):
    cd, sd = cd_ref[0], sd_ref[0]
    ca, sa = ca_ref[0], sa_ref[0]
    c_ref[0] = (cd * ca - sd * sa).astype(c_ref.dtype)
    s_ref[0] = (sd * ca + cd * sa).astype(s_ref.dtype)


def _dft_tables(seq):
    n = 2 * seq
    half = seq // 2
    nblk = half // DFT_TB
    idx = jnp.arange(half, dtype=I32)
    off = jnp.arange(DFT_TB, dtype=I32)
    start = jnp.arange(nblk, dtype=I32) * DFT_TB

    def angle(prod):
        return (prod % n).astype(F32) * (2.0 * math.pi / n)

    ang_d = jnp.stack([angle(off[:, None] * (2 * idx)[None, :]),
                       angle(off[:, None] * (2 * idx + 1)[None, :]),
                       angle((2 * off + 1)[:, None] * idx[None, :])])
    ang_a = jnp.stack([angle(start[:, None] * (2 * idx)[None, :]),
                       angle(start[:, None] * (2 * idx + 1)[None, :]),
                       angle((2 * start)[:, None] * idx[None, :])]).reshape(3 * nblk, 1, half)
    small = pl.BlockSpec((1, DFT_TB, half), lambda k, a: (k, 0, 0))
    base = pl.BlockSpec((1, 1, half), lambda k, a: (k * nblk + a, 0, 0))
    out = pl.BlockSpec((1, DFT_TB, half), lambda k, a: (k, a, 0))
    sds = jax.ShapeDtypeStruct((3, half, half), BF16)
    return pl.pallas_call(
        _dft_kernel, grid=(3, nblk), in_specs=[small, small, base, base], out_specs=[out, out],
        out_shape=[sds, sds], compiler_params=_cparams(("arbitrary", "arbitrary")),
        name="dft_tables")(jnp.cos(ang_d), jnp.sin(ang_d), jnp.cos(ang_a), jnp.sin(ang_a))


def _filter_features(seq):
    t = jnp.linspace(0.0, 1.0, seq, dtype=F32)[:, None]
    bands = (HY_EMB - 1) // 2
    fb = jnp.linspace(1e-4, bands - 1, bands, dtype=F32)[None]
    w = 2.0 * math.pi * jnp.arange(seq, dtype=F32)[:, None] / seq
    z = jnp.concatenate([t, jnp.cos(fb * w), -jnp.sin(fb * w)], -1)
    z = jnp.concatenate([z[0::2], z[1::2]], axis=0)
    return jnp.pad(z, ((0, 0), (0, LANES - HY_EMB)))


def _mixer(x, w_in, b_in, hy_conv_w, hy_conv_b, hy_filt_w1, hy_filt_b1, hy_filt_w2, hy_filt_b2,
           hy_filt_w3, hy_filt_freq, hy_skip, hy_norm_w, ml_conv_w, ml_conv_b, ml_norm_w):
    bsz, seq, _ = x.shape
    t = bsz * seq
    x2d = x.reshape(t, D_MODEL)
    n_main = w_in.shape[1] - N_GATE_COLS
    w_t = jnp.swapaxes(w_in, 0, 1)
    wg = jnp.pad(w_t[n_main:], ((0, LANES - N_GATE_COLS), (0, 0)))
    bg = jnp.pad(b_in[None, n_main:], ((0, 0), (0, LANES - N_GATE_COLS)))
    proj, gates = _in_proj(x2d, w_t[:n_main].astype(BF16), b_in[None, :n_main], wg, bg)
    proj3 = proj.reshape(bsz, seq, n_main)
    gates = gates[:, :N_GATE_COLS]
    g5 = gates.reshape(bsz, seq, 4, ML_HEADS)
    grow = g5.transpose(0, 3, 2, 1).reshape(bsz, ML_HEADS, 4, seq // CHUNK, CHUNK)
    gtr = grow.transpose(0, 1, 2, 4, 3)

    cmat, smat = _dft_tables(seq)
    zpad = _filter_features(seq)
    w1pad = jnp.pad(hy_filt_w1, ((0, LANES - HY_EMB), (0, 0)))
    deltas = jnp.abs(jnp.linspace(math.log(HY_DECAY_TARGET) / HY_SLOW_PCT,
                                  math.log(HY_DECAY_TARGET) / HY_FAST_PCT, D_HY, dtype=F32))[None]
    kr, ki, km = _hyena_filters(zpad, w1pad, hy_filt_b1[None], hy_filt_w2, hy_filt_b2[None],
                                hy_filt_freq, hy_filt_w3, deltas, cmat, smat)
    y_hy = _hyena(proj3, hy_conv_w, hy_conv_b[None], cmat, smat, kr, ki, km, hy_skip, hy_norm_w[None])
    y_ml = _mlstm(proj3, ml_conv_w, ml_conv_b[None], grow, gtr, ml_norm_w[None])
    return y_hy.reshape(t, D_HY), y_ml.reshape(t, D_ML), x2d


def _moe_tables(top_i, slot, counts):
    t = top_i.shape[0]
    n_tiles = N_EXPERTS + (t * TOP_K) // EXP_ROWS
    ntile = (counts + EXP_ROWS - 1) // EXP_ROWS
    ends = jnp.cumsum(ntile)
    starts = ends - ntile
    total = ends[-1]
    s_idx = jnp.arange(n_tiles, dtype=I32)
    valid = s_idx < total
    s_eff = jnp.where(valid, s_idx, jnp.maximum(total - 1, 0))
    tile_e = jnp.minimum(jnp.sum((s_eff[:, None] >= ends[None, :]).astype(I32), axis=1), N_EXPERTS - 1)
    local = s_eff - starts[tile_e]
    tile_rows = jnp.where(valid, jnp.clip(counts[tile_e] - local * EXP_ROWS, 0, EXP_ROWS), 0).astype(I32)
    tok = jnp.arange(t, dtype=I32)[:, None]
    row_tok = jnp.pad(jnp.sort((top_i * t + tok).reshape(-1)) % t, (0, EXP_ROWS))
    first = jnp.cumsum(counts) - counts
    tile_base = (first[tile_e] + local * EXP_ROWS).astype(I32)
    onehot = top_i[:, :, None] == jnp.arange(N_EXPERTS, dtype=I32)
    dest = jnp.sum(jnp.where(onehot, starts * EXP_ROWS, 0), axis=-1) + slot
    return (tile_e.astype(I32), tile_rows, tile_base, row_tok.astype(I32), total.astype(I32),
            dest.astype(I32).reshape(-1))


def kernel(x, w_in, b_in, hy_conv_w, hy_conv_b, hy_filt_w1, hy_filt_b1, hy_filt_w2, hy_filt_b2, hy_filt_w3, hy_filt_freq, hy_skip, hy_norm_w, ml_conv_w, ml_conv_b, ml_norm_w, w_out, b_out, ln1_g, ln1_b, router_w, router_b, w_gu, b_gu, w_down, b_down, ln2_g, ln2_b):
    bsz, seq, _ = x.shape
    l = 0
    y_hy, y_ml, x2d = _mixer(x, w_in[l], b_in[l], hy_conv_w[l], hy_conv_b[l], hy_filt_w1[l], hy_filt_b1[l],
                             hy_filt_w2[l], hy_filt_b2[l], hy_filt_w3[l], hy_filt_freq[l], hy_skip[l],
                             hy_norm_w[l], ml_conv_w[l], ml_conv_b[l], ml_norm_w[l])
    rw = jnp.pad(router_w[l], ((0, 0), (0, LANES - N_EXPERTS)))
    rb = jnp.pad(router_b[l][None], ((0, 0), (0, LANES - N_EXPERTS)), constant_values=-1e30)
    x1, x1c, top_i, top_g, slot, cnt = _out_proj_ln_route(
        y_hy, y_ml, x2d, w_out[l].astype(BF16), b_out[l][None], ln1_g[l][None], ln1_b[l][None], rw, rb)
    counts = cnt[0, :N_EXPERTS].astype(I32)
    tile_e, tile_rows, tile_base, row_tok, used, dest = _moe_tables(top_i[:, :TOP_K], slot[:, :TOP_K], counts)
    y_buf = _experts(tile_e, tile_rows, tile_base, row_tok, used, x1c, w_gu[l], b_gu[l][:, None, :], w_down[l],
                     b_down[l][:, None, :])
    out = _combine_ln(dest, y_buf, x1, top_g, ln2_g[l][None], ln2_b[l][None])
    return out.reshape(bsz, seq, D_MODEL)
```

```python
import functools
import math

import jax
import jax.numpy as jnp
from jax import lax
from jax.experimental import pallas as pl
from jax.experimental.pallas import tpu as pltpu

F32 = jnp.float32
BF16 = jnp.bfloat16
I32 = jnp.int32
U32 = jnp.uint32
HP = lax.Precision.HIGHEST

D_MODEL = 2048
D_HY = 1024
D_ML = 1024
ML_HEADS = 8
HEAD_DIM = 128
CHUNK = 128
N_GATE_COLS = 32
HY_EMB = 33
N_EXPERTS = 32
TOP_K = 4
D_FF = 2048
SWIGLU_LIMIT = 7.0
SWIGLU_ALPHA = 1.702
LN_EPS = 1e-5
DN_ALPHA = 2.0 ** 0.25
HY_DECAY_TARGET = 1e-2
HY_FAST_PCT = 0.3
HY_SLOW_PCT = 1.5

LANES = 128
ROW_CHUNKS = D_MODEL // (2 * LANES)
VMEM_LIMIT = 60 * 1024 * 1024

PROJ_TM = 2048
PROJ_TN = 512
PROJ_GATE_ROWS = 512
HY_CW = 256
HY_FC = 512
OUT_TM = 512
EXP_ROWS = 1152
EXP_M_SIZES = (1024, 1088, 1152)
EXP_CHUNK = 128
EXP_TF = 256
GATHER_UNROLL = 8
CMB_TM = 128
DFT_TB = 128


def _cparams(sem):
    return pltpu.CompilerParams(dimension_semantics=sem, vmem_limit_bytes=VMEM_LIMIT)


def _split(a):
    hi = a.astype(BF16)
    return hi, (a - hi.astype(F32)).astype(BF16)


def _dot3(a, b, dims):
    a_hi, a_lo = _split(a)
    b_hi, b_lo = _split(b)
    mm = functools.partial(lax.dot_general, dimension_numbers=dims, preferred_element_type=F32)
    return mm(a_hi, b_hi) + (mm(a_hi, b_lo) + mm(a_lo, b_hi))


NN = (((1,), (0,)), ((), ()))
NT = (((1,), (1,)), ((), ()))


def _const_spec(shape, index_map):
    return pl.BlockSpec(shape, index_map, pipeline_mode=pl.Buffered(1))


def _proj_kernel(x_ref, w_ref, b_ref, wg_ref, bg_ref, o_ref, og_ref, xb_ref):
    @pl.when(pl.program_id(1) == 0)
    def _():
        xb_ref[...] = x_ref[...].astype(BF16)
        wg_hi, wg_lo = _split(wg_ref[...])
        mm = functools.partial(lax.dot_general, dimension_numbers=NT, preferred_element_type=F32)
        for r in range(x_ref.shape[0] // PROJ_GATE_ROWS):
            rows = slice(r * PROJ_GATE_ROWS, (r + 1) * PROJ_GATE_ROWS)
            x_hi = xb_ref[rows, :]
            x_lo = (x_ref[rows, :] - x_hi.astype(F32)).astype(BF16)
            og_ref[rows, :] = mm(x_hi, wg_hi) + (mm(x_hi, wg_lo) + mm(x_lo, wg_hi)) + bg_ref[...]

    acc = lax.dot_general(xb_ref[...], w_ref[...], NT, preferred_element_type=F32)
    o_ref[...] = (acc + b_ref[...]).astype(o_ref.dtype)


def _in_proj(x2d, wt_bf, b_row, wt_gate, b_gate):
    m, k = x2d.shape
    n = wt_bf.shape[0]
    return pl.pallas_call(
        _proj_kernel,
        grid=(m // PROJ_TM, n // PROJ_TN),
        in_specs=[pl.BlockSpec((PROJ_TM, k), lambda i, j: (i, 0)),
                  pl.BlockSpec((PROJ_TN, k), lambda i, j: (j, 0)),
                  pl.BlockSpec((1, PROJ_TN), lambda i, j: (0, j)),
                  _const_spec((LANES, k), lambda i, j: (0, 0)),
                  _const_spec((1, LANES), lambda i, j: (0, 0))],
        out_specs=[pl.BlockSpec((PROJ_TM, PROJ_TN), lambda i, j: (i, j)),
                   pl.BlockSpec((PROJ_TM, LANES), lambda i, j: (i, 0))],
        out_shape=[jax.ShapeDtypeStruct((m, n), BF16), jax.ShapeDtypeStruct((m, LANES), F32)],
        scratch_shapes=[pltpu.VMEM((PROJ_TM, k), BF16)],
        compiler_params=_cparams(("arbitrary", "arbitrary")),
        name="in_proj")(x2d, wt_bf, b_row, wt_gate, b_gate)


def _filter_kernel(z_ref, w1_ref, b1_ref, w2_ref, b2_ref, fq_ref, w3f_ref, w3b_ref, dl_ref,
                   c_ref, s_ref, kr_ref, ki_ref, km_ref, h_ref):
    seq = z_ref.shape[0]
    inv_n = 1.0 / (2 * seq)
    z = z_ref[...]

    @pl.when(jnp.logical_and(pl.program_id(0) == 0, pl.program_id(1) == 0))
    def _():
        h1 = jnp.sin(fq_ref[0:1, :] * (jnp.dot(z, w1_ref[...], precision=HP, preferred_element_type=F32)
                                       + b1_ref[...]))
        h_ref[...] = jnp.sin(fq_ref[1:2, :] * (jnp.dot(h1, w2_ref[...], precision=HP, preferred_element_type=F32)
                                               + b2_ref[...]))

    half = seq // 2
    h = h_ref[...]
    win = jnp.exp(-z[:, 0:1] * dl_ref[...])
    fwd = _dot3(h, w3f_ref[...], NN) * win
    bwd = _dot3(h, w3b_ref[...], NN) * win
    row = lax.broadcasted_iota(I32, fwd.shape, 0)
    bwd = jnp.where(row == 0, 0.0, bwd)
    inv = 1.0 / jnp.sum(jnp.abs(fwd) + jnp.abs(bwd), axis=0, keepdims=True)
    ks = ((fwd + bwd) * inv)
    kd = ((fwd - bwd) * inv)
    ksb = ks.astype(BF16)
    kdb = kd.astype(BF16)
    ec = jnp.dot(c_ref[0], ksb[:half], preferred_element_type=F32)
    oc = jnp.dot(c_ref[1], ksb[half:], preferred_element_type=F32)
    es = jnp.dot(s_ref[0], kdb[:half], preferred_element_type=F32)
    os_ = jnp.dot(s_ref[1], kdb[half:], preferred_element_type=F32)
    rowh = lax.broadcasted_iota(I32, ec.shape, 0)
    wf = jnp.where(rowh == 0, inv_n, 2.0 * inv_n)
    kr_ref[0, :half, :] = (ec + oc) * wf
    kr_ref[0, half:, :] = (ec - oc) * wf
    ki_ref[0, :half, :] = -(es + os_) * wf
    ki_ref[0, half:, :] = (es - os_) * wf
    alt = jnp.where((rowh & 1) == 0, 1.0, -1.0)
    km_ref[0, 0:1, :] = jnp.sum(ks[:half] * alt, axis=0, keepdims=True) * (2.0 * inv_n)
    km_ref[0, 1:2, :] = -jnp.sum(kd[half:] * alt, axis=0, keepdims=True) * (2.0 * inv_n)


def _hyena_filters(zpad, w1pad, b1, w2, b2, freq, w3, deltas, cmat, smat):
    seq = zpad.shape[0]
    nb = D_HY // HY_CW
    hid = w2.shape[0]
    full = lambda shape: pl.BlockSpec(shape, lambda o, c: (0,) * len(shape))
    out_sds = jax.ShapeDtypeStruct((2, seq, D_HY), F32)
    return pl.pallas_call(
        _filter_kernel,
        grid=(2, nb),
        in_specs=[full(zpad.shape), full(w1pad.shape), full(b1.shape), full(w2.shape), full(b2.shape),
                  full(freq.shape),
                  pl.BlockSpec((hid, HY_CW), lambda o, c: (0, o * 2 * nb + c)),
                  pl.BlockSpec((hid, HY_CW), lambda o, c: (0, o * 2 * nb + nb + c)),
                  pl.BlockSpec((1, HY_CW), lambda o, c: (0, c)),
                  _const_spec(cmat.shape, lambda o, c: (0, 0, 0)),
                  _const_spec(smat.shape, lambda o, c: (0, 0, 0))],
        out_specs=[pl.BlockSpec((1, seq, HY_CW), lambda o, c: (o, 0, c)),
                   pl.BlockSpec((1, seq, HY_CW), lambda o, c: (o, 0, c)),
                   pl.BlockSpec((1, 2, HY_CW), lambda o, c: (o, 0, c))],
        out_shape=[out_sds, out_sds, jax.ShapeDtypeStruct((2, 2, D_HY), F32)],
        scratch_shapes=[pltpu.VMEM((seq, hid), F32)],
        compiler_params=_cparams(("arbitrary", "arbitrary")),
        name="hyena_filters")(zpad, w1pad, b1, w2, b2, freq, w3, w3, deltas, cmat, smat)


def _short_conv(u, w_ref, b_ref, row, seq):
    prev = jnp.where(row == 0, 0.0, pltpu.roll(u, 1, 0))
    nxt = jnp.where(row == seq - 1, 0.0, pltpu.roll(u, seq - 1, 0))
    return w_ref[0:1, :] * prev + w_ref[1:2, :] * u + w_ref[2:3, :] * nxt + b_ref[...]


def _hyena_kernel(uv_ref, u1_ref, u2_ref, wv_ref, w1_ref, w2_ref, bv_ref, b1_ref, b2_ref,
                  c_ref, s_ref, kr_ref, ki_ref, km_ref, skip_ref, nw_ref, o_ref,
                  a_ref, b_ref, t_ref, zb_ref, s1_ref, s2_ref, s3_ref, s4_ref, md_ref):
    seq = uv_ref.shape[1]
    cw = uv_ref.shape[2]
    half = seq // 2
    nblk = half // HY_FC
    row = lax.broadcasted_iota(I32, (seq, LANES), 0)
    alt_half = jnp.where((lax.broadcasted_iota(I32, (half, cw), 0) & 1) == 0, 1.0, -1.0)
    alt_blk = jnp.where((lax.broadcasted_iota(I32, (HY_FC, cw), 0) & 1) == 0, 1.0, -1.0)
    groups = [slice(g * LANES, (g + 1) * LANES) for g in range(cw // LANES)]

    def conv_to(dst_ref, u_ref, w_ref, bias_ref):
        for gi, gs in enumerate(groups):
            t_ref[gi] = _short_conv(u_ref[0, :, gs].astype(F32), w_ref.at[:, gs], bias_ref.at[:, gs], row, seq)
            dst_ref[0:half, gs] = t_ref[gi, pl.ds(0, half, stride=2), :]
            dst_ref[half:seq, gs] = t_ref[gi, pl.ds(1, half, stride=2), :]

    def spectrum(zin_ref, o):
        z = zin_ref[...]
        zb_ref[...] = z.astype(BF16)
        am = jnp.sum(z[:half] * alt_half, axis=0, keepdims=True)
        bm = jnp.sum(z[half:] * alt_half, axis=0, keepdims=True)
        krm, kim = km_ref[o, 0:1, :], km_ref[o, 1:2, :]
        md_ref[0:1, :] = am * krm + bm * kim
        md_ref[1:2, :] = am * kim - bm * krm
        ze = zb_ref[0:half, :]
        zo = zb_ref[half:seq, :]
        for gb in range(nblk):
            lo = slice(gb * HY_FC, (gb + 1) * HY_FC)
            hi = slice(half + gb * HY_FC, half + (gb + 1) * HY_FC)
            ec = jnp.dot(c_ref[0, lo, :], ze, preferred_element_type=F32)
            oc = jnp.dot(c_ref[1, lo, :], zo, preferred_element_type=F32)
            es = jnp.dot(s_ref[0, lo, :], ze, preferred_element_type=F32)
            os_ = jnp.dot(s_ref[1, lo, :], zo, preferred_element_type=F32)
            a_lo, b_lo = ec + oc, es + os_
            a_hi, b_hi = ec - oc, os_ - es
            krl, kil = kr_ref[o, lo, :], ki_ref[o, lo, :]
            krh, kih = kr_ref[o, hi, :], ki_ref[o, hi, :]
            pr = a_lo * krl + b_lo * kil
            pi = a_lo * kil - b_lo * krl
            qr = a_hi * krh + b_hi * kih
            qi = a_hi * kih - b_hi * krh
            s1_ref[lo, :] = (pr + qr).astype(BF16)
            s2_ref[lo, :] = (pi - qi).astype(BF16)
            s3_ref[lo, :] = (pr - qr).astype(BF16)
            s4_ref[lo, :] = (pi + qi).astype(BF16)

    def conv_rows(ub, parity):
        us = slice(ub * HY_FC, (ub + 1) * HY_FC)
        if parity == 0:
            y = jnp.dot(c_ref[0, us, :], s1_ref[...], preferred_element_type=F32)
            y = y - jnp.dot(s_ref[0, us, :], s2_ref[...], preferred_element_type=F32)
            y = y + md_ref[0:1, :] * alt_blk
        else:
            y = jnp.dot(c_ref[2, us, :], s3_ref[...], preferred_element_type=F32)
            y = y - jnp.dot(s_ref[2, us, :], s4_ref[...], preferred_element_type=F32)
            y = y - md_ref[1:2, :] * alt_blk
        return slice(parity * half + ub * HY_FC, parity * half + (ub + 1) * HY_FC), y

    blocks = [(ub, parity) for parity in range(2) for ub in range(nblk)]
    conv_to(a_ref, uv_ref, wv_ref, bv_ref)
    conv_to(b_ref, u1_ref, w1_ref, b1_ref)
    spectrum(a_ref, 0)
    for ub, parity in blocks:
        rows, y = conv_rows(ub, parity)
        b_ref[rows, :] = b_ref[rows, :] * (y + skip_ref[0:1, :] * a_ref[rows, :])
    conv_to(a_ref, u2_ref, w2_ref, b2_ref)
    spectrum(b_ref, 1)
    for ub, parity in blocks:
        rows, y = conv_rows(ub, parity)
        z = a_ref[rows, :] * (y + skip_ref[1:2, :] * b_ref[rows, :])
        for gi, gs in enumerate(groups):
            zg = z[:, gs]
            mu = jnp.mean(zg, axis=-1, keepdims=True)
            zc = zg - mu
            var = jnp.mean(zc * zc, axis=-1, keepdims=True)
            t_ref[gi, pl.ds(2 * ub * HY_FC + parity, HY_FC, stride=2), :] = (
                zc * lax.rsqrt(var + LN_EPS) * nw_ref[:, gs])
    for gi, gs in enumerate(groups):
        o_ref[0, :, gs] = t_ref[gi].astype(o_ref.dtype)


def _hyena(proj3, conv_w, conv_b, cmat, smat, kr, ki, km, skip, norm_w):
    bsz, seq, _ = proj3.shape
    half = seq // 2
    nb = D_HY // HY_CW
    u_spec = lambda off: pl.BlockSpec((1, seq, HY_CW), lambda c, b: (b, 0, off + c))
    w_spec = lambda off: pl.BlockSpec((3, HY_CW), lambda c, b: (0, off + c))
    b_spec = lambda off: pl.BlockSpec((1, HY_CW), lambda c, b: (0, off + c))
    return pl.pallas_call(
        _hyena_kernel,
        grid=(nb, bsz),
        in_specs=[u_spec(0), u_spec(nb), u_spec(2 * nb),
                  w_spec(0), w_spec(nb), w_spec(2 * nb),
                  b_spec(0), b_spec(nb), b_spec(2 * nb),
                  _const_spec(cmat.shape, lambda c, b: (0, 0, 0)),
                  _const_spec(smat.shape, lambda c, b: (0, 0, 0)),
                  _const_spec((2, seq, HY_CW), lambda c, b: (0, 0, c)),
                  _const_spec((2, seq, HY_CW), lambda c, b: (0, 0, c)),
                  pl.BlockSpec((2, 2, HY_CW), lambda c, b: (0, 0, c)),
                  pl.BlockSpec((2, HY_CW), lambda c, b: (0, c)),
                  pl.BlockSpec((1, HY_CW), lambda c, b: (0, c))],
        out_specs=pl.BlockSpec((1, seq, HY_CW), lambda c, b: (b, 0, c)),
        out_shape=jax.ShapeDtypeStruct((bsz, seq, D_HY), BF16),
        scratch_shapes=[pltpu.VMEM((seq, HY_CW), F32), pltpu.VMEM((seq, HY_CW), F32),
                        pltpu.VMEM((HY_CW // LANES, seq, LANES), F32), pltpu.VMEM((seq, HY_CW), BF16),
                        pltpu.VMEM((half, HY_CW), BF16), pltpu.VMEM((half, HY_CW), BF16),
                        pltpu.VMEM((half, HY_CW), BF16), pltpu.VMEM((half, HY_CW), BF16),
                        pltpu.VMEM((2, HY_CW), F32)],
        compiler_params=_cparams(("arbitrary", "arbitrary")),
        name="hyena")(proj3, proj3, proj3, conv_w, conv_w, conv_w, conv_b, conv_b, conv_b,
                      cmat, smat, kr, ki, km, skip, norm_w)


def _mlstm_kernel(qp_ref, kp_ref, v_ref, og_ref, wq_ref, wk_ref, bq_ref, bk_ref, gr_ref, gt_ref,
                  nw_ref, o_ref, qb_ref, kb_ref, hacc_ref, cst_ref, nst_ref):
    seq = qp_ref.shape[1]
    d = qp_ref.shape[2]
    nchunk = seq // CHUNK
    row = lax.broadcasted_iota(I32, (seq, d), 0)
    q = _short_conv(qp_ref[0].astype(F32), wq_ref, bq_ref, row, seq)
    k = _short_conv(kp_ref[0].astype(F32), wk_ref, bk_ref, row, seq)
    qb_ref[...] = (q * jax.nn.sigmoid(q)).astype(BF16)
    kb_ref[...] = ((k * jax.nn.sigmoid(k)) * (d ** -0.5)).astype(BF16)

    ti = lax.broadcasted_iota(I32, (CHUNK, CHUNK), 0)
    si = lax.broadcasted_iota(I32, (CHUNK, CHUNK), 1)
    lower = ti >= si
    upper = ti <= si
    lower_f = lower.astype(F32)
    upper_f = upper.astype(F32)
    nt = (((1,), (1,)), ((), ()))
    tn = (((0,), (0,)), ((), ()))
    chunk_rows = [slice(c * CHUNK, (c + 1) * CHUNK) for c in range(nchunk)]

    for direction in range(2):
        f_idx, i_idx = 2 * direction + 1, 2 * direction
        mask = lower if direction == 0 else upper
        order = list(range(nchunk)) if direction == 0 else list(range(nchunk - 1, -1, -1))
        lf_r = jax.nn.log_sigmoid(gr_ref[0, 0, f_idx])
        b_r = jnp.dot(lf_r, upper_f if direction == 0 else lower_f, precision=HP, preferred_element_type=F32)
        rterm = b_r - gr_ref[0, 0, i_idx]
        b_last = jnp.sum(lf_r, axis=-1, keepdims=True)
        lf_c = jax.nn.log_sigmoid(gt_ref[0, 0, f_idx])
        b_c = jnp.dot(lower_f if direction == 0 else upper_f, lf_c, precision=HP, preferred_element_type=F32)
        i_c = gt_ref[0, 0, i_idx]

        bcol, gcol, gmax, rowmax, blast = [], [], [], [], []
        for c in range(nchunk):
            bc = jnp.broadcast_to(b_c[:, c:c + 1], (CHUNK, CHUNK))
            ic = jnp.broadcast_to(i_c[:, c:c + 1], (CHUNK, CHUNK))
            bl = jnp.broadcast_to(b_last[c:c + 1, :], (1, CHUNK))
            dmat = jnp.where(mask, bc - rterm[c:c + 1, :], -jnp.inf)
            g = bl - bc + ic
            bcol.append(bc)
            gcol.append(g)
            blast.append(bl)
            rowmax.append(jnp.max(dmat, axis=-1, keepdims=True))
            gmax.append(jnp.max(g, axis=0, keepdims=True))

        m = jnp.zeros((1, CHUNK), F32)
        m_in, m_out = [None] * nchunk, [None] * nchunk
        for c in order:
            m_in[c] = m
            m = jnp.maximum(blast[c] + m, gmax[c])
            m_out[c] = m

        cmat = jnp.zeros((d, d), F32)
        nvec = jnp.zeros((1, d), F32)
        for c in order:
            kc = kb_ref[chunk_rows[c], :]
            vc = v_ref[0, chunk_rows[c], :]
            cst_ref[c] = cmat.astype(BF16)
            nst_ref[c] = nvec
            wg = jnp.exp(gcol[c] - m_out[c])
            decay = jnp.exp(blast[c] + m_in[c] - m_out[c])
            upd = lax.dot_general((wg * vc.astype(F32)).astype(BF16), kc, tn, preferred_element_type=F32)
            cmat = decay * cmat + upd
            nvec = decay * nvec + jnp.sum(wg * kc.astype(F32), axis=0, keepdims=True)

        for c in range(nchunk):
            rs = chunk_rows[c]
            qc = qb_ref[rs, :]
            kc = kb_ref[rs, :]
            vc = v_ref[0, rs, :]
            dmat = jnp.where(mask, bcol[c] - rterm[c:c + 1, :], -jnp.inf)
            inter = bcol[c] + m_in[c]
            m_t = jnp.maximum(inter, rowmax[c])
            p = jnp.exp(dmat - m_t)
            inter_w = jnp.exp(inter - m_t)
            s = lax.dot_general(qc, kc, nt, preferred_element_type=F32) * p
            cq = lax.dot_general(qc, cst_ref[c], nt, preferred_element_type=F32)
            num = jnp.dot(s.astype(BF16), vc, preferred_element_type=F32) + inter_w * cq
            nq = jnp.sum(qc.astype(F32) * nst_ref[c], axis=-1, keepdims=True)
            den = jnp.sum(s, axis=-1, keepdims=True) + inter_w * nq
            h = num / jnp.maximum(jnp.abs(den), jnp.exp(-m_t))
            if direction == 0:
                hacc_ref[rs, :] = h
            else:
                hacc_ref[rs, :] += h

    h = hacc_ref[...]
    mu = jnp.mean(h, axis=-1, keepdims=True)
    hc = h - mu
    var = jnp.mean(hc * hc, axis=-1, keepdims=True)
    y = hc * lax.rsqrt(var + LN_EPS) * nw_ref[...] * jax.nn.sigmoid(og_ref[0].astype(F32))
    o_ref[0] = y.astype(o_ref.dtype)


def _mlstm(proj3, conv_w, conv_b, grow, gtr, norm_w):
    bsz, seq, _ = proj3.shape
    d = HEAD_DIM
    nchunk = seq // CHUNK
    hy_blocks = 3 * D_HY // d
    qoff, koff, voff, ooff = hy_blocks, hy_blocks + ML_HEADS, hy_blocks + 2 * ML_HEADS, hy_blocks + 3 * ML_HEADS
    p_spec = lambda off: pl.BlockSpec((1, seq, d), lambda b, h: (b, 0, off + h))
    return pl.pallas_call(
        _mlstm_kernel,
        grid=(bsz, ML_HEADS),
        in_specs=[p_spec(qoff), p_spec(koff), p_spec(voff), p_spec(ooff),
                  pl.BlockSpec((3, d), lambda b, h: (0, h)),
                  pl.BlockSpec((3, d), lambda b, h: (0, ML_HEADS + h)),
                  pl.BlockSpec((1, d), lambda b, h: (0, h)),
                  pl.BlockSpec((1, d), lambda b, h: (0, ML_HEADS + h)),
                  pl.BlockSpec((1, 1, 4, nchunk, CHUNK), lambda b, h: (b, h, 0, 0, 0)),
                  pl.BlockSpec((1, 1, 4, CHUNK, nchunk), lambda b, h: (b, h, 0, 0, 0)),
                  pl.BlockSpec((1, d), lambda b, h: (0, h))],
        out_specs=pl.BlockSpec((1, seq, d), lambda b, h: (b, 0, h)),
        out_shape=jax.ShapeDtypeStruct((bsz, seq, D_ML), BF16),
        scratch_shapes=[pltpu.VMEM((seq, d), BF16), pltpu.VMEM((seq, d), BF16),
                        pltpu.VMEM((seq, d), F32),
                        pltpu.VMEM((nchunk, d, d), BF16), pltpu.VMEM((nchunk, 1, d), F32)],
        compiler_params=_cparams(("arbitrary", "arbitrary")),
        name="mlstm")(proj3, proj3, proj3, proj3, conv_w, conv_w, conv_b, conv_b, grow, gtr, norm_w)


def _to_slabs(y):
    n = y.shape[0]
    half = D_MODEL // 2
    lo = lax.bitcast_convert_type(y[:, :half].astype(BF16).astype(F32), U32) >> 16
    hi = lax.bitcast_convert_type(y[:, half:].astype(BF16).astype(F32), U32) & jnp.uint32(0xFFFF0000)
    words = hi | lo
    parts = jnp.stack([words[:, c * LANES:(c + 1) * LANES] for c in range(ROW_CHUNKS)], axis=0)
    return pltpu.einshape("crl->rcl", parts).reshape(n * ROW_CHUNKS, LANES)


def _from_slabs(v):
    n = v.shape[0] // ROW_CHUNKS
    parts = pltpu.einshape("rcl->crl", v.reshape(n, ROW_CHUNKS, LANES))
    lo = [lax.bitcast_convert_type(parts[c] << 16, F32) for c in range(ROW_CHUNKS)]
    hi = [lax.bitcast_convert_type(parts[c] & jnp.uint32(0xFFFF0000), F32) for c in range(ROW_CHUNKS)]
    return jnp.concatenate(lo + hi, axis=-1)


def _layer_norm(u, g, b):
    mu = jnp.mean(u, axis=-1, keepdims=True)
    uc = u - mu
    var = jnp.mean(uc * uc, axis=-1, keepdims=True)
    return uc * lax.rsqrt(var + LN_EPS) * g + b


def _route(x, w_ref, b_ref, ti_ref, tg_ref, tp_ref, cnt_ref):
    tm = ti_ref.shape[0]

    @pl.when(pl.program_id(0) == 0)
    def _():
        cnt_ref[...] = jnp.zeros_like(cnt_ref)

    logits = _dot3(x, w_ref[...], NN) + b_ref[...]
    lane = lax.broadcasted_iota(I32, (tm, LANES), 1)
    work = logits
    vals, idxs = [], []
    chosen = jnp.zeros((tm, LANES), F32)
    for _ in range(TOP_K):
        mx = jnp.max(work, axis=-1, keepdims=True)
        idx = jnp.min(jnp.where(work == mx, lane, LANES), axis=-1, keepdims=True)
        hit = lane == idx
        vals.append(mx)
        idxs.append(idx)
        chosen = jnp.where(hit, 1.0, chosen)
        work = jnp.where(hit, -jnp.inf, work)
    exps = [jnp.exp(v - vals[0]) for v in vals]
    den = exps[0] + exps[1] + exps[2] + exps[3]
    ri = lax.broadcasted_iota(I32, (tm, tm), 0)
    ci = lax.broadcasted_iota(I32, (tm, tm), 1)
    strict_lower = (ri > ci).astype(BF16)
    carry = cnt_ref[...]
    slot = carry + jnp.dot(strict_lower, chosen.astype(BF16), preferred_element_type=F32)
    ti = jnp.zeros((tm, LANES), I32)
    tg = jnp.zeros((tm, LANES), F32)
    tp = jnp.zeros((tm, LANES), F32)
    for k in range(TOP_K):
        sk = jnp.sum(jnp.where(lane == idxs[k], slot, 0.0), axis=-1, keepdims=True)
        ti = jnp.where(lane == k, idxs[k], ti)
        tg = jnp.where(lane == k, exps[k] / den, tg)
        tp = jnp.where(lane == k, sk, tp)
    ti_ref[...] = ti
    tg_ref[...] = tg
    tp_ref[...] = tp.astype(I32)
    cnt_ref[...] = carry + jnp.sum(chosen, axis=0, keepdims=True)


def _outproj_kernel(yh_ref, ym_ref, x_ref, wa_ref, wb_ref, b_ref, g_ref, be_ref, rw_ref, rb_ref,
                    o_ref, oc_ref, ti_ref, tg_ref, tp_ref, cnt_ref):
    mix = (jnp.dot(yh_ref[...], wa_ref[...], preferred_element_type=F32)
           + jnp.dot(ym_ref[...], wb_ref[...], preferred_element_type=F32) + b_ref[...])
    y = _layer_norm(DN_ALPHA * x_ref[...] + mix, g_ref[...], be_ref[...])
    o_ref[...] = y
    oc_ref[...] = _to_slabs(y)
    _route(y, rw_ref, rb_ref, ti_ref, tg_ref, tp_ref, cnt_ref)


def _out_proj_ln_route(y_hy, y_ml, x2d, w_out_bf, b_out, g, be, rw_pad, rb_pad):
    t = x2d.shape[0]
    tm = OUT_TM
    vec = lambda: pl.BlockSpec((1, D_MODEL), lambda i: (0, 0))
    lane_blk = lambda: pl.BlockSpec((tm, LANES), lambda i: (i, 0))
    return pl.pallas_call(
        _outproj_kernel,
        grid=(t // tm,),
        in_specs=[pl.BlockSpec((tm, D_HY), lambda i: (i, 0)),
                  pl.BlockSpec((tm, D_ML), lambda i: (i, 0)),
                  pl.BlockSpec((tm, D_MODEL), lambda i: (i, 0)),
                  _const_spec((D_HY, D_MODEL), lambda i: (0, 0)),
                  _const_spec((D_ML, D_MODEL), lambda i: (1, 0)),
                  vec(), vec(), vec(),
                  _const_spec((D_MODEL, LANES), lambda i: (0, 0)),
                  pl.BlockSpec((1, LANES), lambda i: (0, 0))],
        out_specs=[pl.BlockSpec((tm, D_MODEL), lambda i: (i, 0)),
                   pl.BlockSpec((tm * ROW_CHUNKS, LANES), lambda i: (i, 0)),
                   lane_blk(), lane_blk(), lane_blk(), pl.BlockSpec((1, LANES), lambda i: (0, 0))],
        out_shape=[jax.ShapeDtypeStruct((t, D_MODEL), F32),
                   jax.ShapeDtypeStruct((t * ROW_CHUNKS, LANES), U32),
                   jax.ShapeDtypeStruct((t, LANES), I32), jax.ShapeDtypeStruct((t, LANES), F32),
                   jax.ShapeDtypeStruct((t, LANES), I32), jax.ShapeDtypeStruct((1, LANES), F32)],
        compiler_params=_cparams(("arbitrary",)),
        name="out_proj_ln1_route")(y_hy, y_ml, x2d, w_out_bf, w_out_bf, b_out, g, be, rw_pad, rb_pad)


def _expert_kernel(te_ref, tr_ref, tb_ref, rt_ref, x_hbm, wg_ref, wu_ref, wd_ref, bg_ref, bu_ref, bd_ref, y_hbm,
                   stage_ref, xb_ref, acc_ref, ring_ref, wgb_ref, wub_ref, wdb_ref, gsem, osem):
    g = pl.program_id(0)
    n_items = pl.num_programs(0) - 1
    nf = D_FF // EXP_TF
    n_tiles = n_items // nf
    item = jnp.maximum(g - 1, 0)
    s = item // nf
    j = item % nf
    rows = tr_ref[s]
    active = jnp.logical_and(g >= 1, rows > 0)
    cast_slot = g % 2
    use_slot = (g + 1) % 2
    slab = EXP_CHUNK * ROW_CHUNKS
    per_step = EXP_ROWS // nf

    def row_copy(tok, r):
        return pltpu.make_async_copy(
            x_hbm.at[pl.ds(pl.multiple_of(tok * ROW_CHUNKS, ROW_CHUNKS), ROW_CHUNKS), :],
            stage_ref.at[pl.ds(pl.multiple_of(r * ROW_CHUNKS, ROW_CHUNKS), ROW_CHUNKS), :], gsem)

    def wait_gather():
        pltpu.make_async_copy(x_hbm.at[pl.ds(0, EXP_ROWS * ROW_CHUNKS), :], stage_ref, gsem).wait()

    def cast_weights():
        wgb_ref[cast_slot] = wg_ref[0].astype(BF16)
        wub_ref[cast_slot] = wu_ref[0].astype(BF16)
        wdb_ref[cast_slot] = wd_ref[0].astype(BF16)

    @pl.when(g == 0)
    def _():
        stage_ref[...] = jnp.zeros_like(stage_ref)
        acc_ref[...] = jnp.zeros_like(acc_ref)
        base = tb_ref[0]

        def group(q, carry):
            for u in range(GATHER_UNROLL):
                r = q * GATHER_UNROLL + u
                row_copy(rt_ref[base + r], r).start()
            return carry

        lax.fori_loop(0, EXP_ROWS // GATHER_UNROLL, group, 0)

        cast_weights()

    @pl.when(active)
    def _():
        @pl.when(j == 0)
        def _():
            wait_gather()
            for i in range(EXP_ROWS // EXP_CHUNK):
                xb_ref[i * EXP_CHUNK:(i + 1) * EXP_CHUNK, :] = _from_slabs(
                    stage_ref[i * slab:(i + 1) * slab, :]).astype(BF16)

        def step_body(m):
            cast_weights()
            nxt_base = tb_ref[jnp.minimum(s + 1, n_tiles - 1)]
            for u in range(per_step):
                r = j * per_step + u
                row_copy(rt_ref[nxt_base + r], r).start()

            xb = xb_ref[0:m, :]
            gate = jnp.dot(xb, wgb_ref[use_slot], preferred_element_type=F32) + bg_ref[0]
            up = jnp.dot(xb, wub_ref[use_slot], preferred_element_type=F32) + bu_ref[0]
            gate = jnp.minimum(gate, SWIGLU_LIMIT)
            up = jnp.clip(up, -SWIGLU_LIMIT, SWIGLU_LIMIT)
            act = (up + 1.0) * (gate * jax.nn.sigmoid(SWIGLU_ALPHA * gate))
            part = jnp.dot(act.astype(BF16), wdb_ref[use_slot], preferred_element_type=F32)
            acc_ref[0:m, :] = jnp.where(j == 0, jnp.broadcast_to(bd_ref[0], part.shape), acc_ref[0:m, :]) + part

        lo = 0
        for m in EXP_M_SIZES:
            @pl.when(jnp.logical_and(rows > lo, rows <= m))
            def _(m=m):
                step_body(m)
            lo = m

        @pl.when(j == nf - 1)
        def _():
            nchunk = (rows + EXP_CHUNK - 1) // EXP_CHUNK

            def chunk_copy(i, slot):
                dst0 = pl.multiple_of((s * EXP_ROWS + i * EXP_CHUNK) * ROW_CHUNKS, slab)
                return pltpu.make_async_copy(ring_ref.at[slot], y_hbm.at[pl.ds(dst0, slab), :], osem.at[slot])

            def emit(i, carry):
                slot = i % 2

                @pl.when(i >= 2)
                def _():
                    chunk_copy(i - 2, slot).wait()

                r0 = pl.multiple_of(i * EXP_CHUNK, EXP_CHUNK)
                ring_ref[slot] = _to_slabs(acc_ref[pl.ds(r0, EXP_CHUNK), :])
                chunk_copy(i, slot).start()
                return carry

            lax.fori_loop(0, nchunk, emit, 0)
            for back in range(2):
                @pl.when(nchunk > back)
                def _():
                    last = nchunk - 1 - back
                    chunk_copy(last, last % 2).wait()

    @pl.when(g == n_items)
    def _():
        wait_gather()


def _experts(tile_e, tile_rows, tile_base, row_tok, used_tiles, x1c, w_gu, b_gu, w_down, b_down):
    n_tiles = tile_e.shape[0]
    nf = D_FF // EXP_TF
    n_items = n_tiles * nf

    def item_block(item, te, tr):
        s = item // nf
        return te[s], jnp.where(tr[s] > 0, item % nf, nf - 1)

    def cast_item(g, te, tr):
        return item_block(jnp.minimum(g, n_items - 1), te, tr)

    def use_item(g, te, tr):
        return item_block(jnp.maximum(g - 1, 0), te, tr)

    def w_gate(g, te, tr, tb, rt):
        e, j = cast_item(g, te, tr)
        return e, 0, j

    def w_up(g, te, tr, tb, rt):
        e, j = cast_item(g, te, tr)
        return e, 0, nf + j

    def w_down_map(g, te, tr, tb, rt):
        e, j = cast_item(g, te, tr)
        return e, j, 0

    def b_gate(g, te, tr, tb, rt):
        e, j = use_item(g, te, tr)
        return e, 0, j

    def b_up(g, te, tr, tb, rt):
        e, j = use_item(g, te, tr)
        return e, 0, nf + j

    def b_down_map(g, te, tr, tb, rt):
        e, _ = use_item(g, te, tr)
        return e, 0, 0

    grid_spec = pltpu.PrefetchScalarGridSpec(
        num_scalar_prefetch=4,
        grid=(used_tiles * nf + 1,),
        in_specs=[pl.BlockSpec(memory_space=pl.ANY),
                  pl.BlockSpec((1, D_MODEL, EXP_TF), w_gate),
                  pl.BlockSpec((1, D_MODEL, EXP_TF), w_up),
                  pl.BlockSpec((1, EXP_TF, D_MODEL), w_down_map),
                  pl.BlockSpec((1, 1, EXP_TF), b_gate),
                  pl.BlockSpec((1, 1, EXP_TF), b_up),
                  pl.BlockSpec((1, 1, D_MODEL), b_down_map)],
        out_specs=pl.BlockSpec(memory_space=pl.ANY),
        scratch_shapes=[pltpu.VMEM((EXP_ROWS * ROW_CHUNKS, LANES), U32),
                        pltpu.VMEM((EXP_ROWS, D_MODEL), BF16),
                        pltpu.VMEM((EXP_ROWS, D_MODEL), F32),
                        pltpu.VMEM((2, EXP_CHUNK * ROW_CHUNKS, LANES), U32),
                        pltpu.VMEM((2, D_MODEL, EXP_TF), BF16),
                        pltpu.VMEM((2, D_MODEL, EXP_TF), BF16),
                        pltpu.VMEM((2, EXP_TF, D_MODEL), BF16),
                        pltpu.SemaphoreType.DMA(()),
                        pltpu.SemaphoreType.DMA((2,))])
    return pl.pallas_call(
        _expert_kernel,
        grid_spec=grid_spec,
        out_shape=jax.ShapeDtypeStruct((n_tiles * EXP_ROWS * ROW_CHUNKS, LANES), U32),
        compiler_params=_cparams(("arbitrary",)),
        name="experts")(tile_e, tile_rows, tile_base, row_tok, x1c, w_gu, w_gu, w_down, b_gu, b_gu, b_down)


def _combine_kernel(dest_ref, y_hbm, x_ref, tg_ref, g_ref, be_ref, o_ref, buf_ref, sem):
    tm = o_ref.shape[0]
    i = pl.program_id(0)
    n = pl.num_programs(0)

    def row_copy(src_row, slot, k, t):
        return pltpu.make_async_copy(
            y_hbm.at[pl.ds(pl.multiple_of(src_row * ROW_CHUNKS, ROW_CHUNKS), ROW_CHUNKS), :],
            buf_ref.at[slot, k, pl.ds(pl.multiple_of(t * ROW_CHUNKS, ROW_CHUNKS), ROW_CHUNKS), :],
            sem.at[slot])

    def start_tile(tile, slot):
        base = tile * tm * TOP_K

        def body(t2, carry):
            for u in range(2):
                t = t2 * 2 + u
                for k in range(TOP_K):
                    row_copy(dest_ref[base + t * TOP_K + k], slot, k, t).start()
            return carry

        lax.fori_loop(0, tm // 2, body, 0)

    def wait_tile(slot):
        for k in range(TOP_K):
            pltpu.make_async_copy(y_hbm.at[pl.ds(0, tm * ROW_CHUNKS), :], buf_ref.at[slot, k], sem.at[slot]).wait()

    @pl.when(i == 0)
    def _():
        start_tile(0, 0)

    @pl.when(i + 1 < n)
    def _():
        start_tile(jnp.minimum(i + 1, n - 1), (i + 1) % 2)

    slot = i % 2
    wait_tile(slot)
    tg = tg_ref[...]
    ff = jnp.zeros((tm, D_MODEL), F32)
    for k in range(TOP_K):
        ff = ff + tg[:, k:k + 1] * _from_slabs(buf_ref[slot, k])
    o_ref[...] = _layer_norm(DN_ALPHA * x_ref[...] + ff, g_ref[...], be_ref[...])


def _combine_ln(dest_flat, y_buf, x1, tg, g, be):
    t = tg.shape[0]
    tm = CMB_TM
    grid_spec = pltpu.PrefetchScalarGridSpec(
        num_scalar_prefetch=1,
        grid=(t // tm,),
        in_specs=[pl.BlockSpec(memory_space=pl.ANY),
                  pl.BlockSpec((tm, D_MODEL), lambda i, d: (i, 0)),
                  pl.BlockSpec((tm, LANES), lambda i, d: (i, 0)),
                  pl.BlockSpec((1, D_MODEL), lambda i, d: (0, 0)),
                  pl.BlockSpec((1, D_MODEL), lambda i, d: (0, 0))],
        out_specs=pl.BlockSpec((tm, D_MODEL), lambda i, d: (i, 0)),
        scratch_shapes=[pltpu.VMEM((2, TOP_K, tm * ROW_CHUNKS, LANES), U32), pltpu.SemaphoreType.DMA((2,))])
    return pl.pallas_call(
        _combine_kernel,
        grid_spec=grid_spec,
        out_shape=jax.ShapeDtypeStruct((t, D_MODEL), F32),
        compiler_params=_cparams(("arbitrary",)),
        name="combine_ln2")(dest_flat, y_buf, x1, tg, g, be)


def _dft_kernel(cd_ref, sd_ref, ca_ref, sa_ref, c_ref, s_ref):
    cd, sd = cd_ref[0], sd_ref[0]
    ca, sa = ca_ref[0], sa_ref[0]
    c_ref[0] = (cd * ca - sd * sa).astype(c_ref.dtype)
    s_ref[0] = (sd * ca + cd * sa).astype(s_ref.dtype)


def _dft_tables(seq):
    n = 2 * seq
    half = seq // 2
    nblk = half // DFT_TB
    idx = jnp.arange(half, dtype=I32)
    off = jnp.arange(DFT_TB, dtype=I32)
    start = jnp.arange(nblk, dtype=I32) * DFT_TB

    def angle(prod):
        return (prod % n).astype(F32) * (2.0 * math.pi / n)

    ang_d = jnp.stack([angle(off[:, None] * (2 * idx)[None, :]),
                       angle(off[:, None] * (2 * idx + 1)[None, :]),
                       angle((2 * off + 1)[:, None] * idx[None, :])])
    ang_a = jnp.stack([angle(start[:, None] * (2 * idx)[None, :]),
                       angle(start[:, None] * (2 * idx + 1)[None, :]),
                       angle((2 * start)[:, None] * idx[None, :])]).reshape(3 * nblk, 1, half)
    small = pl.BlockSpec((1, DFT_TB, half), lambda k, a: (k, 0, 0))
    base = pl.BlockSpec((1, 1, half), lambda k, a: (k * nblk + a, 0, 0))
    out = pl.BlockSpec((1, DFT_TB, half), lambda k, a: (k, a, 0))
    sds = jax.ShapeDtypeStruct((3, half, half), BF16)
    return pl.pallas_call(
        _dft_kernel, grid=(3, nblk), in_specs=[small, small, base, base], out_specs=[out, out],
        out_shape=[sds, sds], compiler_params=_cparams(("arbitrary", "arbitrary")),
        name="dft_tables")(jnp.cos(ang_d), jnp.sin(ang_d), jnp.cos(ang_a), jnp.sin(ang_a))


def _filter_features(seq):
    t = jnp.linspace(0.0, 1.0, seq, dtype=F32)[:, None]
    bands = (HY_EMB - 1) // 2
    fb = jnp.linspace(1e-4, bands - 1, bands, dtype=F32)[None]
    w = 2.0 * math.pi * jnp.arange(seq, dtype=F32)[:, None] / seq
    z = jnp.concatenate([t, jnp.cos(fb * w), -jnp.sin(fb * w)], -1)
    z = jnp.concatenate([z[0::2], z[1::2]], axis=0)
    return jnp.pad(z, ((0, 0), (0, LANES - HY_EMB)))


def _mixer(x, w_in, b_in, hy_conv_w, hy_conv_b, hy_filt_w1, hy_filt_b1, hy_filt_w2, hy_filt_b2,
           hy_filt_w3, hy_filt_freq, hy_skip, hy_norm_w, ml_conv_w, ml_conv_b, ml_norm_w):
    bsz, seq, _ = x.shape
    t = bsz * seq
    x2d = x.reshape(t, D_MODEL)
    n_main = w_in.shape[1] - N_GATE_COLS
    w_t = jnp.swapaxes(w_in, 0, 1)
    wg = jnp.pad(w_t[n_main:], ((0, LANES - N_GATE_COLS), (0, 0)))
    bg = jnp.pad(b_in[None, n_main:], ((0, 0), (0, LANES - N_GATE_COLS)))
    proj, gates = _in_proj(x2d, w_t[:n_main].astype(BF16), b_in[None, :n_main], wg, bg)
    proj3 = proj.reshape(bsz, seq, n_main)
    gates = gates[:, :N_GATE_COLS]
    g5 = gates.reshape(bsz, seq, 4, ML_HEADS)
    grow = g5.transpose(0, 3, 2, 1).reshape(bsz, ML_HEADS, 4, seq // CHUNK, CHUNK)
    gtr = grow.transpose(0, 1, 2, 4, 3)

    cmat, smat = _dft_tables(seq)
    zpad = _filter_features(seq)
    w1pad = jnp.pad(hy_filt_w1, ((0, LANES - HY_EMB), (0, 0)))
    deltas = jnp.abs(jnp.linspace(math.log(HY_DECAY_TARGET) / HY_SLOW_PCT,
                                  math.log(HY_DECAY_TARGET) / HY_FAST_PCT, D_HY, dtype=F32))[None]
    kr, ki, km = _hyena_filters(zpad, w1pad, hy_filt_b1[None], hy_filt_w2, hy_filt_b2[None],
                                hy_filt_freq, hy_filt_w3, deltas, cmat, smat)
    y_hy = _hyena(proj3, hy_conv_w, hy_conv_b[None], cmat, smat, kr, ki, km, hy_skip, hy_norm_w[None])
    y_ml = _mlstm(proj3, ml_conv_w, ml_conv_b[None], grow, gtr, ml_norm_w[None])
    return y_hy.reshape(t, D_HY), y_ml.reshape(t, D_ML), x2d


def _moe_tables(top_i, slot, counts):
    t = top_i.shape[0]
    n_tiles = N_EXPERTS + (t * TOP_K) // EXP_ROWS
    ntile = (counts + EXP_ROWS - 1) // EXP_ROWS
    ends = jnp.cumsum(ntile)
    starts = ends - ntile
    total = ends[-1]
    s_idx = jnp.arange(n_tiles, dtype=I32)
    valid = s_idx < total
    s_eff = jnp.where(valid, s_idx, jnp.maximum(total - 1, 0))
    tile_e = jnp.minimum(jnp.sum((s_eff[:, None] >= ends[None, :]).astype(I32), axis=1), N_EXPERTS - 1)
    local = s_eff - starts[tile_e]
    tile_rows = jnp.where(valid, jnp.clip(counts[tile_e] - local * EXP_ROWS, 0, EXP_ROWS), 0).astype(I32)
    tok = jnp.arange(t, dtype=I32)[:, None]
    row_tok = jnp.pad(jnp.sort((top_i * t + tok).reshape(-1)) % t, (0, EXP_ROWS))
    first = jnp.cumsum(counts) - counts
    tile_base = (first[tile_e] + local * EXP_ROWS).astype(I32)
    onehot = top_i[:, :, None] == jnp.arange(N_EXPERTS, dtype=I32)
    dest = jnp.sum(jnp.where(onehot, starts * EXP_ROWS, 0), axis=-1) + slot
    return (tile_e.astype(I32), tile_rows, tile_base, row_tok.astype(I32), total.astype(I32),
            dest.astype(I32).reshape(-1))


def kernel(x, w_in, b_in, hy_conv_w, hy_conv_b, hy_filt_w1, hy_filt_b1, hy_filt_w2, hy_filt_b2, hy_filt_w3, hy_filt_freq, hy_skip, hy_norm_w, ml_conv_w, ml_conv_b, ml_norm_w, w_out, b_out, ln1_g, ln1_b, router_w, router_b, w_gu, b_gu, w_down, b_down, ln2_g, ln2_b):
    bsz, seq, _ = x.shape
    l = 0
    y_hy, y_ml, x2d = _mixer(x, w_in[l], b_in[l], hy_conv_w[l], hy_conv_b[l], hy_filt_w1[l], hy_filt_b1[l],
                             hy_filt_w2[l], hy_filt_b2[l], hy_filt_w3[l], hy_filt_freq[l], hy_skip[l],
                             hy_norm_w[l], ml_conv_w[l], ml_conv_b[l], ml_norm_w[l])
    rw = jnp.pad(router_w[l], ((0, 0), (0, LANES - N_EXPERTS)))
    rb = jnp.pad(router_b[l][None], ((0, 0), (0, LANES - N_EXPERTS)), constant_values=-1e30)
    x1, x1c, top_i, top_g, slot, cnt = _out_proj_ln_route(
        y_hy, y_ml, x2d, w_out[l].astype(BF16), b_out[l][None], ln1_g[l][None], ln1_b[l][None], rw, rb)
    counts = cnt[0, :N_EXPERTS].astype(I32)
    tile_e, tile_rows, tile_base, row_tok, used, dest = _moe_tables(top_i[:, :TOP_K], slot[:, :TOP_K], counts)
    y_buf = _experts(tile_e, tile_rows, tile_base, row_tok, used, x1c, w_gu[l], b_gu[l][:, None, :], w_down[l],
                     b_down[l][:, None, :])
    out = _combine_ln(dest, y_buf, x1, top_g, ln2_g[l][None], ln2_b[l][None])
    return out.reshape(bsz, seq, D_MODEL)
```

```python
import functools
import math

import jax
import jax.numpy as jnp
from jax import lax
from jax.experimental import pallas as pl
from jax.experimental.pallas import tpu as pltpu

F32 = jnp.float32
BF16 = jnp.bfloat16
I32 = jnp.int32
U32 = jnp.uint32
HP = lax.Precision.HIGHEST

D_MODEL = 2048
D_HY = 1024
D_ML = 1024
ML_HEADS = 8
HEAD_DIM = 128
CHUNK = 128
N_GATE_COLS = 32
HY_EMB = 33
N_EXPERTS = 32
TOP_K = 4
D_FF = 2048
SWIGLU_LIMIT = 7.0
SWIGLU_ALPHA = 1.702
LN_EPS = 1e-5
DN_ALPHA = 2.0 ** 0.25
HY_DECAY_TARGET = 1e-2
HY_FAST_PCT = 0.3
HY_SLOW_PCT = 1.5

LANES = 128
ROW_CHUNKS = D_MODEL // (2 * LANES)
VMEM_LIMIT = 60 * 1024 * 1024

PROJ_TM = 2048
PROJ_TN = 512
PROJ_GATE_ROWS = 512
HY_CW = 256
HY_FC = 512
OUT_TM = 512
EXP_ROWS = 1152
EXP_M_SIZES = (1024, 1088, 1152)
EXP_CHUNK = 128
EXP_TF = 256
GATHER_UNROLL = 8
CMB_TM = 256
DFT_TB = 512


def _cparams(sem):
    return pltpu.CompilerParams(dimension_semantics=sem, vmem_limit_bytes=VMEM_LIMIT)


def _split(a):
    hi = a.astype(BF16)
    return hi, (a - hi.astype(F32)).astype(BF16)


def _dot3(a, b, dims):
    a_hi, a_lo = _split(a)
    b_hi, b_lo = _split(b)
    mm = functools.partial(lax.dot_general, dimension_numbers=dims, preferred_element_type=F32)
    return mm(a_hi, b_hi) + (mm(a_hi, b_lo) + mm(a_lo, b_hi))


NN = (((1,), (0,)), ((), ()))
NT = (((1,), (1,)), ((), ()))


def _const_spec(shape, index_map):
    return pl.BlockSpec(shape, index_map, pipeline_mode=pl.Buffered(1))


def _proj_kernel(x_ref, w_ref, b_ref, wg_ref, bg_ref, o_ref, og_ref, xb_ref):
    @pl.when(pl.program_id(1) == 0)
    def _():
        xb_ref[...] = x_ref[...].astype(BF16)
        wg_hi, wg_lo = _split(wg_ref[...])
        mm = functools.partial(lax.dot_general, dimension_numbers=NT, preferred_element_type=F32)
        for r in range(x_ref.shape[0] // PROJ_GATE_ROWS):
            rows = slice(r * PROJ_GATE_ROWS, (r + 1) * PROJ_GATE_ROWS)
            x_hi = xb_ref[rows, :]
            x_lo = (x_ref[rows, :] - x_hi.astype(F32)).astype(BF16)
            og_ref[rows, :] = mm(x_hi, wg_hi) + (mm(x_hi, wg_lo) + mm(x_lo, wg_hi)) + bg_ref[...]

    acc = lax.dot_general(xb_ref[...], w_ref[...], NT, preferred_element_type=F32)
    o_ref[...] = (acc + b_ref[...]).astype(o_ref.dtype)


def _in_proj(x2d, wt_bf, b_row, wt_gate, b_gate):
    m, k = x2d.shape
    n = wt_bf.shape[0]
    return pl.pallas_call(
        _proj_kernel,
        grid=(m // PROJ_TM, n // PROJ_TN),
        in_specs=[pl.BlockSpec((PROJ_TM, k), lambda i, j: (i, 0)),
                  pl.BlockSpec((PROJ_TN, k), lambda i, j: (j, 0)),
                  pl.BlockSpec((1, PROJ_TN), lambda i, j: (0, j)),
                  _const_spec((LANES, k), lambda i, j: (0, 0)),
                  _const_spec((1, LANES), lambda i, j: (0, 0))],
        out_specs=[pl.BlockSpec((PROJ_TM, PROJ_TN), lambda i, j: (i, j)),
                   pl.BlockSpec((PROJ_TM, LANES), lambda i, j: (i, 0))],
        out_shape=[jax.ShapeDtypeStruct((m, n), BF16), jax.ShapeDtypeStruct((m, LANES), F32)],
        scratch_shapes=[pltpu.VMEM((PROJ_TM, k), BF16)],
        compiler_params=_cparams(("arbitrary", "arbitrary")),
        name="in_proj")(x2d, wt_bf, b_row, wt_gate, b_gate)


def _filter_kernel(z_ref, w1_ref, b1_ref, w2_ref, b2_ref, fq_ref, w3f_ref, w3b_ref, dl_ref,
                   c_ref, s_ref, kr_ref, ki_ref, km_ref, h_ref):
    seq = z_ref.shape[0]
    inv_n = 1.0 / (2 * seq)
    z = z_ref[...]

    @pl.when(jnp.logical_and(pl.program_id(0) == 0, pl.program_id(1) == 0))
    def _():
        h1 = jnp.sin(fq_ref[0:1, :] * (jnp.dot(z, w1_ref[...], precision=HP, preferred_element_type=F32)
                                       + b1_ref[...]))
        h_ref[...] = jnp.sin(fq_ref[1:2, :] * (jnp.dot(h1, w2_ref[...], precision=HP, preferred_element_type=F32)
                                               + b2_ref[...]))

    half = seq // 2
    h = h_ref[...]
    win = jnp.exp(-z[:, 0:1] * dl_ref[...])
    fwd = _dot3(h, w3f_ref[...], NN) * win
    bwd = _dot3(h, w3b_ref[...], NN) * win
    row = lax.broadcasted_iota(I32, fwd.shape, 0)
    bwd = jnp.where(row == 0, 0.0, bwd)
    inv = 1.0 / jnp.sum(jnp.abs(fwd) + jnp.abs(bwd), axis=0, keepdims=True)
    ks = ((fwd + bwd) * inv)
    kd = ((fwd - bwd) * inv)
    ksb = ks.astype(BF16)
    kdb = kd.astype(BF16)
    ec = jnp.dot(c_ref[0], ksb[:half], preferred_element_type=F32)
    oc = jnp.dot(c_ref[1], ksb[half:], preferred_element_type=F32)
    es = jnp.dot(s_ref[0], kdb[:half], preferred_element_type=F32)
    os_ = jnp.dot(s_ref[1], kdb[half:], preferred_element_type=F32)
    rowh = lax.broadcasted_iota(I32, ec.shape, 0)
    wf = jnp.where(rowh == 0, inv_n, 2.0 * inv_n)
    kr_ref[0, :half, :] = (ec + oc) * wf
    kr_ref[0, half:, :] = (ec - oc) * wf
    ki_ref[0, :half, :] = -(es + os_) * wf
    ki_ref[0, half:, :] = (es - os_) * wf
    alt = jnp.where((rowh & 1) == 0, 1.0, -1.0)
    km_ref[0, 0:1, :] = jnp.sum(ks[:half] * alt, axis=0, keepdims=True) * (2.0 * inv_n)
    km_ref[0, 1:2, :] = -jnp.sum(kd[half:] * alt, axis=0, keepdims=True) * (2.0 * inv_n)


def _hyena_filters(zpad, w1pad, b1, w2, b2, freq, w3, deltas, cmat, smat):
    seq = zpad.shape[0]
    nb = D_HY // HY_CW
    hid = w2.shape[0]
    full = lambda shape: pl.BlockSpec(shape, lambda o, c: (0,) * len(shape))
    out_sds = jax.ShapeDtypeStruct((2, seq, D_HY), F32)
    return pl.pallas_call(
        _filter_kernel,
        grid=(2, nb),
        in_specs=[full(zpad.shape), full(w1pad.shape), full(b1.shape), full(w2.shape), full(b2.shape),
                  full(freq.shape),
                  pl.BlockSpec((hid, HY_CW), lambda o, c: (0, o * 2 * nb + c)),
                  pl.BlockSpec((hid, HY_CW), lambda o, c: (0, o * 2 * nb + nb + c)),
                  pl.BlockSpec((1, HY_CW), lambda o, c: (0, c)),
                  _const_spec(cmat.shape, lambda o, c: (0, 0, 0)),
                  _const_spec(smat.shape, lambda o, c: (0, 0, 0))],
        out_specs=[pl.BlockSpec((1, seq, HY_CW), lambda o, c: (o, 0, c)),
                   pl.BlockSpec((1, seq, HY_CW), lambda o, c: (o, 0, c)),
                   pl.BlockSpec((1, 2, HY_CW), lambda o, c: (o, 0, c))],
        out_shape=[out_sds, out_sds, jax.ShapeDtypeStruct((2, 2, D_HY), F32)],
        scratch_shapes=[pltpu.VMEM((seq, hid), F32)],
        compiler_params=_cparams(("arbitrary", "arbitrary")),
        name="hyena_filters")(zpad, w1pad, b1, w2, b2, freq, w3, w3, deltas, cmat, smat)


def _short_conv(u, w_ref, b_ref, row, seq):
    prev = jnp.where(row == 0, 0.0, pltpu.roll(u, 1, 0))
    nxt = jnp.where(row == seq - 1, 0.0, pltpu.roll(u, seq - 1, 0))
    return w_ref[0:1, :] * prev + w_ref[1:2, :] * u + w_ref[2:3, :] * nxt + b_ref[...]


def _hyena_kernel(uv_ref, u1_ref, u2_ref, wv_ref, w1_ref, w2_ref, bv_ref, b1_ref, b2_ref,
                  c_ref, s_ref, kr_ref, ki_ref, km_ref, skip_ref, nw_ref, o_ref,
                  a_ref, b_ref, t_ref, zb_ref, s1_ref, s2_ref, s3_ref, s4_ref, md_ref):
    seq = uv_ref.shape[1]
    cw = uv_ref.shape[2]
    half = seq // 2
    nblk = half // HY_FC
    row = lax.broadcasted_iota(I32, (seq, LANES), 0)
    alt_half = jnp.where((lax.broadcasted_iota(I32, (half, cw), 0) & 1) == 0, 1.0, -1.0)
    alt_blk = jnp.where((lax.broadcasted_iota(I32, (HY_FC, cw), 0) & 1) == 0, 1.0, -1.0)
    groups = [slice(g * LANES, (g + 1) * LANES) for g in range(cw // LANES)]

    def conv_to(dst_ref, u_ref, w_ref, bias_ref):
        for gi, gs in enumerate(groups):
            t_ref[gi] = _short_conv(u_ref[0, :, gs].astype(F32), w_ref.at[:, gs], bias_ref.at[:, gs], row, seq)
            dst_ref[0:half, gs] = t_ref[gi, pl.ds(0, half, stride=2), :]
            dst_ref[half:seq, gs] = t_ref[gi, pl.ds(1, half, stride=2), :]

    def spectrum(zin_ref, o):
        z = zin_ref[...]
        zb_ref[...] = z.astype(BF16)
        am = jnp.sum(z[:half] * alt_half, axis=0, keepdims=True)
        bm = jnp.sum(z[half:] * alt_half, axis=0, keepdims=True)
        krm, kim = km_ref[o, 0:1, :], km_ref[o, 1:2, :]
        md_ref[0:1, :] = am * krm + bm * kim
        md_ref[1:2, :] = am * kim - bm * krm
        ze = zb_ref[0:half, :]
        zo = zb_ref[half:seq, :]
        for gb in range(nblk):
            lo = slice(gb * HY_FC, (gb + 1) * HY_FC)
            hi = slice(half + gb * HY_FC, half + (gb + 1) * HY_FC)
            ec = jnp.dot(c_ref[0, lo, :], ze, preferred_element_type=F32)
            oc = jnp.dot(c_ref[1, lo, :], zo, preferred_element_type=F32)
            es = jnp.dot(s_ref[0, lo, :], ze, preferred_element_type=F32)
            os_ = jnp.dot(s_ref[1, lo, :], zo, preferred_element_type=F32)
            a_lo, b_lo = ec + oc, es + os_
            a_hi, b_hi = ec - oc, os_ - es
            krl, kil = kr_ref[o, lo, :], ki_ref[o, lo, :]
            krh, kih = kr_ref[o, hi, :], ki_ref[o, hi, :]
            pr = a_lo * krl + b_lo * kil
            pi = a_lo * kil - b_lo * krl
            qr = a_hi * krh + b_hi * kih
            qi = a_hi * kih - b_hi * krh
            s1_ref[lo, :] = (pr + qr).astype(BF16)
            s2_ref[lo, :] = (pi - qi).astype(BF16)
            s3_ref[lo, :] = (pr - qr).astype(BF16)
            s4_ref[lo, :] = (pi + qi).astype(BF16)

    def conv_rows(ub, parity):
        us = slice(ub * HY_FC, (ub + 1) * HY_FC)
        if parity == 0:
            y = jnp.dot(c_ref[0, us, :], s1_ref[...], preferred_element_type=F32)
            y = y - jnp.dot(s_ref[0, us, :], s2_ref[...], preferred_element_type=F32)
            y = y + md_ref[0:1, :] * alt_blk
        else:
            y = jnp.dot(c_ref[2, us, :], s3_ref[...], preferred_element_type=F32)
            y = y - jnp.dot(s_ref[2, us, :], s4_ref[...], preferred_element_type=F32)
            y = y - md_ref[1:2, :] * alt_blk
        return slice(parity * half + ub * HY_FC, parity * half + (ub + 1) * HY_FC), y

    blocks = [(ub, parity) for parity in range(2) for ub in range(nblk)]
    conv_to(a_ref, uv_ref, wv_ref, bv_ref)
    conv_to(b_ref, u1_ref, w1_ref, b1_ref)
    spectrum(a_ref, 0)
    for ub, parity in blocks:
        rows, y = conv_rows(ub, parity)
        b_ref[rows, :] = b_ref[rows, :] * (y + skip_ref[0:1, :] * a_ref[rows, :])
    conv_to(a_ref, u2_ref, w2_ref, b2_ref)
    spectrum(b_ref, 1)
    for ub, parity in blocks:
        rows, y = conv_rows(ub, parity)
        z = a_ref[rows, :] * (y + skip_ref[1:2, :] * b_ref[rows, :])
        for gi, gs in enumerate(groups):
            zg = z[:, gs]
            mu = jnp.mean(zg, axis=-1, keepdims=True)
            zc = zg - mu
            var = jnp.mean(zc * zc, axis=-1, keepdims=True)
            t_ref[gi, pl.ds(2 * ub * HY_FC + parity, HY_FC, stride=2), :] = (
                zc * lax.rsqrt(var + LN_EPS) * nw_ref[:, gs])
    for gi, gs in enumerate(groups):
        o_ref[0, :, gs] = t_ref[gi].astype(o_ref.dtype)


def _hyena(proj3, conv_w, conv_b, cmat, smat, kr, ki, km, skip, norm_w):
    bsz, seq, _ = proj3.shape
    half = seq // 2
    nb = D_HY // HY_CW
    u_spec = lambda off: pl.BlockSpec((1, seq, HY_CW), lambda c, b: (b, 0, off + c))
    w_spec = lambda off: pl.BlockSpec((3, HY_CW), lambda c, b: (0, off + c))
    b_spec = lambda off: pl.BlockSpec((1, HY_CW), lambda c, b: (0, off + c))
    return pl.pallas_call(
        _hyena_kernel,
        grid=(nb, bsz),
        in_specs=[u_spec(0), u_spec(nb), u_spec(2 * nb),
                  w_spec(0), w_spec(nb), w_spec(2 * nb),
                  b_spec(0), b_spec(nb), b_spec(2 * nb),
                  _const_spec(cmat.shape, lambda c, b: (0, 0, 0)),
                  _const_spec(smat.shape, lambda c, b: (0, 0, 0)),
                  _const_spec((2, seq, HY_CW), lambda c, b: (0, 0, c)),
                  _const_spec((2, seq, HY_CW), lambda c, b: (0, 0, c)),
                  pl.BlockSpec((2, 2, HY_CW), lambda c, b: (0, 0, c)),
                  pl.BlockSpec((2, HY_CW), lambda c, b: (0, c)),
                  pl.BlockSpec((1, HY_CW), lambda c, b: (0, c))],
        out_specs=pl.BlockSpec((1, seq, HY_CW), lambda c, b: (b, 0, c)),
        out_shape=jax.ShapeDtypeStruct((bsz, seq, D_HY), BF16),
        scratch_shapes=[pltpu.VMEM((seq, HY_CW), F32), pltpu.VMEM((seq, HY_CW), F32),
                        pltpu.VMEM((HY_CW // LANES, seq, LANES), F32), pltpu.VMEM((seq, HY_CW), BF16),
                        pltpu.VMEM((half, HY_CW), BF16), pltpu.VMEM((half, HY_CW), BF16),
                        pltpu.VMEM((half, HY_CW), BF16), pltpu.VMEM((half, HY_CW), BF16),
                        pltpu.VMEM((2, HY_CW), F32)],
        compiler_params=_cparams(("arbitrary", "arbitrary")),
        name="hyena")(proj3, proj3, proj3, conv_w, conv_w, conv_w, conv_b, conv_b, conv_b,
                      cmat, smat, kr, ki, km, skip, norm_w)


def _mlstm_kernel(qp_ref, kp_ref, v_ref, og_ref, wq_ref, wk_ref, bq_ref, bk_ref, gr_ref, gt_ref,
                  nw_ref, o_ref, qb_ref, kb_ref, hacc_ref, cst_ref, nst_ref):
    seq = qp_ref.shape[1]
    d = qp_ref.shape[2]
    nchunk = seq // CHUNK
    row = lax.broadcasted_iota(I32, (seq, d), 0)
    q = _short_conv(qp_ref[0].astype(F32), wq_ref, bq_ref, row, seq)
    k = _short_conv(kp_ref[0].astype(F32), wk_ref, bk_ref, row, seq)
    qb_ref[...] = (q * jax.nn.sigmoid(q)).astype(BF16)
    kb_ref[...] = ((k * jax.nn.sigmoid(k)) * (d ** -0.5)).astype(BF16)

    ti = lax.broadcasted_iota(I32, (CHUNK, CHUNK), 0)
    si = lax.broadcasted_iota(I32, (CHUNK, CHUNK), 1)
    lower = ti >= si
    upper = ti <= si
    lower_f = lower.astype(F32)
    upper_f = upper.astype(F32)
    nt = (((1,), (1,)), ((), ()))
    tn = (((0,), (0,)), ((), ()))
    chunk_rows = [slice(c * CHUNK, (c + 1) * CHUNK) for c in range(nchunk)]

    for direction in range(2):
        f_idx, i_idx = 2 * direction + 1, 2 * direction
        mask = lower if direction == 0 else upper
        order = list(range(nchunk)) if direction == 0 else list(range(nchunk - 1, -1, -1))
        lf_r = jax.nn.log_sigmoid(gr_ref[0, 0, f_idx])
        b_r = jnp.dot(lf_r, upper_f if direction == 0 else lower_f, precision=HP, preferred_element_type=F32)
        rterm = b_r - gr_ref[0, 0, i_idx]
        b_last = jnp.sum(lf_r, axis=-1, keepdims=True)
        lf_c = jax.nn.log_sigmoid(gt_ref[0, 0, f_idx])
        b_c = jnp.dot(lower_f if direction == 0 else upper_f, lf_c, precision=HP, preferred_element_type=F32)
        i_c = gt_ref[0, 0, i_idx]

        bcol, gcol, gmax, rowmax, blast = [], [], [], [], []
        for c in range(nchunk):
            bc = jnp.broadcast_to(b_c[:, c:c + 1], (CHUNK, CHUNK))
            ic = jnp.broadcast_to(i_c[:, c:c + 1], (CHUNK, CHUNK))
            bl = jnp.broadcast_to(b_last[c:c + 1, :], (1, CHUNK))
            dmat = jnp.where(mask, bc - rterm[c:c + 1, :], -jnp.inf)
            g = bl - bc + ic
            bcol.append(bc)
            gcol.append(g)
            blast.append(bl)
            rowmax.append(jnp.max(dmat, axis=-1, keepdims=True))
            gmax.append(jnp.max(g, axis=0, keepdims=True))

        m = jnp.zeros((1, CHUNK), F32)
        m_in, m_out = [None] * nchunk, [None] * nchunk
        for c in order:
            m_in[c] = m
            m = jnp.maximum(blast[c] + m, gmax[c])
            m_out[c] = m

        cmat = jnp.zeros((d, d), F32)
        nvec = jnp.zeros((1, d), F32)
        for c in order:
            kc = kb_ref[chunk_rows[c], :]
            vc = v_ref[0, chunk_rows[c], :]
            cst_ref[c] = cmat.astype(BF16)
            nst_ref[c] = nvec
            wg = jnp.exp(gcol[c] - m_out[c])
            decay = jnp.exp(blast[c] + m_in[c] - m_out[c])
            upd = lax.dot_general((wg * vc.astype(F32)).astype(BF16), kc, tn, preferred_element_type=F32)
            cmat = decay * cmat + upd
            nvec = decay * nvec + jnp.sum(wg * kc.astype(F32), axis=0, keepdims=True)

        for c in range(nchunk):
            rs = chunk_rows[c]
            qc = qb_ref[rs, :]
            kc = kb_ref[rs, :]
            vc = v_ref[0, rs, :]
            dmat = jnp.where(mask, bcol[c] - rterm[c:c + 1, :], -jnp.inf)
            inter = bcol[c] + m_in[c]
            m_t = jnp.maximum(inter, rowmax[c])
            p = jnp.exp(dmat - m_t)
            inter_w = jnp.exp(inter - m_t)
            s = lax.dot_general(qc, kc, nt, preferred_element_type=F32) * p
            cq = lax.dot_general(qc, cst_ref[c], nt, preferred_element_type=F32)
            num = jnp.dot(s.astype(BF16), vc, preferred_element_type=F32) + inter_w * cq
            nq = jnp.sum(qc.astype(F32) * nst_ref[c], axis=-1, keepdims=True)
            den = jnp.sum(s, axis=-1, keepdims=True) + inter_w * nq
            h = num / jnp.maximum(jnp.abs(den), jnp.exp(-m_t))
            if direction == 0:
                hacc_ref[rs, :] = h
            else:
                hacc_ref[rs, :] += h

    h = hacc_ref[...]
    mu = jnp.mean(h, axis=-1, keepdims=True)
    hc = h - mu
    var = jnp.mean(hc * hc, axis=-1, keepdims=True)
    y = hc * lax.rsqrt(var + LN_EPS) * nw_ref[...] * jax.nn.sigmoid(og_ref[0].astype(F32))
    o_ref[0] = y.astype(o_ref.dtype)


def _mlstm(proj3, conv_w, conv_b, grow, gtr, norm_w):
    bsz, seq, _ = proj3.shape
    d = HEAD_DIM
    nchunk = seq // CHUNK
    hy_blocks = 3 * D_HY // d
    qoff, koff, voff, ooff = hy_blocks, hy_blocks + ML_HEADS, hy_blocks + 2 * ML_HEADS, hy_blocks + 3 * ML_HEADS
    p_spec = lambda off: pl.BlockSpec((1, seq, d), lambda b, h: (b, 0, off + h))
    return pl.pallas_call(
        _mlstm_kernel,
        grid=(bsz, ML_HEADS),
        in_specs=[p_spec(qoff), p_spec(koff), p_spec(voff), p_spec(ooff),
                  pl.BlockSpec((3, d), lambda b, h: (0, h)),
                  pl.BlockSpec((3, d), lambda b, h: (0, ML_HEADS + h)),
                  pl.BlockSpec((1, d), lambda b, h: (0, h)),
                  pl.BlockSpec((1, d), lambda b, h: (0, ML_HEADS + h)),
                  pl.BlockSpec((1, 1, 4, nchunk, CHUNK), lambda b, h: (b, h, 0, 0, 0)),
                  pl.BlockSpec((1, 1, 4, CHUNK, nchunk), lambda b, h: (b, h, 0, 0, 0)),
                  pl.BlockSpec((1, d), lambda b, h: (0, h))],
        out_specs=pl.BlockSpec((1, seq, d), lambda b, h: (b, 0, h)),
        out_shape=jax.ShapeDtypeStruct((bsz, seq, D_ML), BF16),
        scratch_shapes=[pltpu.VMEM((seq, d), BF16), pltpu.VMEM((seq, d), BF16),
                        pltpu.VMEM((seq, d), F32),
                        pltpu.VMEM((nchunk, d, d), BF16), pltpu.VMEM((nchunk, 1, d), F32)],
        compiler_params=_cparams(("arbitrary", "arbitrary")),
        name="mlstm")(proj3, proj3, proj3, proj3, conv_w, conv_w, conv_b, conv_b, grow, gtr, norm_w)


def _to_slabs(y):
    n = y.shape[0]
    half = D_MODEL // 2
    lo = lax.bitcast_convert_type(y[:, :half].astype(BF16).astype(F32), U32) >> 16
    hi = lax.bitcast_convert_type(y[:, half:].astype(BF16).astype(F32), U32) & jnp.uint32(0xFFFF0000)
    words = hi | lo
    parts = jnp.stack([words[:, c * LANES:(c + 1) * LANES] for c in range(ROW_CHUNKS)], axis=0)
    return pltpu.einshape("crl->rcl", parts).reshape(n * ROW_CHUNKS, LANES)


def _from_slabs(v):
    n = v.shape[0] // ROW_CHUNKS
    parts = pltpu.einshape("rcl->crl", v.reshape(n, ROW_CHUNKS, LANES))
    lo = [lax.bitcast_convert_type(parts[c] << 16, F32) for c in range(ROW_CHUNKS)]
    hi = [lax.bitcast_convert_type(parts[c] & jnp.uint32(0xFFFF0000), F32) for c in range(ROW_CHUNKS)]
    return jnp.concatenate(lo + hi, axis=-1)


def _layer_norm(u, g, b):
    mu = jnp.mean(u, axis=-1, keepdims=True)
    uc = u - mu
    var = jnp.mean(uc * uc, axis=-1, keepdims=True)
    return uc * lax.rsqrt(var + LN_EPS) * g + b


def _route(x, w_ref, b_ref, ti_ref, tg_ref, tp_ref, cnt_ref):
    tm = ti_ref.shape[0]

    @pl.when(pl.program_id(0) == 0)
    def _():
        cnt_ref[...] = jnp.zeros_like(cnt_ref)

    logits = _dot3(x, w_ref[...], NN) + b_ref[...]
    lane = lax.broadcasted_iota(I32, (tm, LANES), 1)
    work = logits
    vals, idxs = [], []
    chosen = jnp.zeros((tm, LANES), F32)
    for _ in range(TOP_K):
        mx = jnp.max(work, axis=-1, keepdims=True)
        idx = jnp.min(jnp.where(work == mx, lane, LANES), axis=-1, keepdims=True)
        hit = lane == idx
        vals.append(mx)
        idxs.append(idx)
        chosen = jnp.where(hit, 1.0, chosen)
        work = jnp.where(hit, -jnp.inf, work)
    exps = [jnp.exp(v - vals[0]) for v in vals]
    den = exps[0] + exps[1] + exps[2] + exps[3]
    ri = lax.broadcasted_iota(I32, (tm, tm), 0)
    ci = lax.broadcasted_iota(I32, (tm, tm), 1)
    strict_lower = (ri > ci).astype(BF16)
    carry = cnt_ref[...]
    slot = carry + jnp.dot(strict_lower, chosen.astype(BF16), preferred_element_type=F32)
    ti = jnp.zeros((tm, LANES), I32)
    tg = jnp.zeros((tm, LANES), F32)
    tp = jnp.zeros((tm, LANES), F32)
    for k in range(TOP_K):
        sk = jnp.sum(jnp.where(lane == idxs[k], slot, 0.0), axis=-1, keepdims=True)
        ti = jnp.where(lane == k, idxs[k], ti)
        tg = jnp.where(lane == k, exps[k] / den, tg)
        tp = jnp.where(lane == k, sk, tp)
    ti_ref[...] = ti
    tg_ref[...] = tg
    tp_ref[...] = tp.astype(I32)
    cnt_ref[...] = carry + jnp.sum(chosen, axis=0, keepdims=True)


def _outproj_kernel(yh_ref, ym_ref, x_ref, wa_ref, wb_ref, b_ref, g_ref, be_ref, rw_ref, rb_ref,
                    o_ref, oc_ref, ti_ref, tg_ref, tp_ref, cnt_ref):
    mix = (jnp.dot(yh_ref[...], wa_ref[...], preferred_element_type=F32)
           + jnp.dot(ym_ref[...], wb_ref[...], preferred_element_type=F32) + b_ref[...])
    y = _layer_norm(DN_ALPHA * x_ref[...] + mix, g_ref[...], be_ref[...])
    o_ref[...] = y
    oc_ref[...] = _to_slabs(y)
    _route(y, rw_ref, rb_ref, ti_ref, tg_ref, tp_ref, cnt_ref)


def _out_proj_ln_route(y_hy, y_ml, x2d, w_out_bf, b_out, g, be, rw_pad, rb_pad):
    t = x2d.shape[0]
    tm = OUT_TM
    vec = lambda: pl.BlockSpec((1, D_MODEL), lambda i: (0, 0))
    lane_blk = lambda: pl.BlockSpec((tm, LANES), lambda i: (i, 0))
    return pl.pallas_call(
        _outproj_kernel,
        grid=(t // tm,),
        in_specs=[pl.BlockSpec((tm, D_HY), lambda i: (i, 0)),
                  pl.BlockSpec((tm, D_ML), lambda i: (i, 0)),
                  pl.BlockSpec((tm, D_MODEL), lambda i: (i, 0)),
                  _const_spec((D_HY, D_MODEL), lambda i: (0, 0)),
                  _const_spec((D_ML, D_MODEL), lambda i: (1, 0)),
                  vec(), vec(), vec(),
                  _const_spec((D_MODEL, LANES), lambda i: (0, 0)),
                  pl.BlockSpec((1, LANES), lambda i: (0, 0))],
        out_specs=[pl.BlockSpec((tm, D_MODEL), lambda i: (i, 0)),
                   pl.BlockSpec((tm * ROW_CHUNKS, LANES), lambda i: (i, 0)),
                   lane_blk(), lane_blk(), lane_blk(), pl.BlockSpec((1, LANES), lambda i: (0, 0))],
        out_shape=[jax.ShapeDtypeStruct((t, D_MODEL), F32),
                   jax.ShapeDtypeStruct((t * ROW_CHUNKS, LANES), U32),
                   jax.ShapeDtypeStruct((t, LANES), I32), jax.ShapeDtypeStruct((t, LANES), F32),
                   jax.ShapeDtypeStruct((t, LANES), I32), jax.ShapeDtypeStruct((1, LANES), F32)],
        compiler_params=_cparams(("arbitrary",)),
        name="out_proj_ln1_route")(y_hy, y_ml, x2d, w_out_bf, w_out_bf, b_out, g, be, rw_pad, rb_pad)


def _expert_kernel(te_ref, tr_ref, tb_ref, rt_ref, x_hbm, wg_ref, wu_ref, wd_ref, bg_ref, bu_ref, bd_ref, y_hbm,
                   stage_ref, xb_ref, acc_ref, ring_ref, wgb_ref, wub_ref, wdb_ref, gsem, osem):
    g = pl.program_id(0)
    n_items = pl.num_programs(0) - 1
    nf = D_FF // EXP_TF
    n_tiles = n_items // nf
    item = jnp.maximum(g - 1, 0)
    s = item // nf
    j = item % nf
    rows = tr_ref[s]
    active = jnp.logical_and(g >= 1, rows > 0)
    cast_slot = g % 2
    use_slot = (g + 1) % 2
    slab = EXP_CHUNK * ROW_CHUNKS
    per_step = EXP_ROWS // nf

    def row_copy(tok, r):
        return pltpu.make_async_copy(
            x_hbm.at[pl.ds(pl.multiple_of(tok * ROW_CHUNKS, ROW_CHUNKS), ROW_CHUNKS), :],
            stage_ref.at[pl.ds(pl.multiple_of(r * ROW_CHUNKS, ROW_CHUNKS), ROW_CHUNKS), :], gsem)

    def wait_gather():
        pltpu.make_async_copy(x_hbm.at[pl.ds(0, EXP_ROWS * ROW_CHUNKS), :], stage_ref, gsem).wait()

    def cast_weights():
        wgb_ref[cast_slot] = wg_ref[0].astype(BF16)
        wub_ref[cast_slot] = wu_ref[0].astype(BF16)
        wdb_ref[cast_slot] = wd_ref[0].astype(BF16)

    @pl.when(g == 0)
    def _():
        stage_ref[...] = jnp.zeros_like(stage_ref)
        acc_ref[...] = jnp.zeros_like(acc_ref)
        base = tb_ref[0]

        def group(q, carry):
            for u in range(GATHER_UNROLL):
                r = q * GATHER_UNROLL + u
                row_copy(rt_ref[base + r], r).start()
            return carry

        lax.fori_loop(0, EXP_ROWS // GATHER_UNROLL, group, 0)

        cast_weights()

    @pl.when(active)
    def _():
        @pl.when(j == 0)
        def _():
            wait_gather()
            for i in range(EXP_ROWS // EXP_CHUNK):
                xb_ref[i * EXP_CHUNK:(i + 1) * EXP_CHUNK, :] = _from_slabs(
                    stage_ref[i * slab:(i + 1) * slab, :]).astype(BF16)

        def step_body(m):
            cast_weights()
            nxt_base = tb_ref[jnp.minimum(s + 1, n_tiles - 1)]
            for u in range(per_step):
                r = j * per_step + u
                row_copy(rt_ref[nxt_base + r], r).start()

            xb = xb_ref[0:m, :]
            gate = jnp.dot(xb, wgb_ref[use_slot], preferred_element_type=F32) + bg_ref[0]
            up = jnp.dot(xb, wub_ref[use_slot], preferred_element_type=F32) + bu_ref[0]
            gate = jnp.minimum(gate, SWIGLU_LIMIT)
            up = jnp.clip(up, -SWIGLU_LIMIT, SWIGLU_LIMIT)
            act = (up + 1.0) * (gate * jax.nn.sigmoid(SWIGLU_ALPHA * gate))
            part = jnp.dot(act.astype(BF16), wdb_ref[use_slot], preferred_element_type=F32)
            acc_ref[0:m, :] = jnp.where(j == 0, jnp.broadcast_to(bd_ref[0], part.shape), acc_ref[0:m, :]) + part

        lo = 0
        for m in EXP_M_SIZES:
            @pl.when(jnp.logical_and(rows > lo, rows <= m))
            def _(m=m):
                step_body(m)
            lo = m

        @pl.when(j == nf - 1)
        def _():
            nchunk = (rows + EXP_CHUNK - 1) // EXP_CHUNK

            def chunk_copy(i, slot):
                dst0 = pl.multiple_of((s * EXP_ROWS + i * EXP_CHUNK) * ROW_CHUNKS, slab)
                return pltpu.make_async_copy(ring_ref.at[slot], y_hbm.at[pl.ds(dst0, slab), :], osem.at[slot])

            def emit(i, carry):
                slot = i % 2

                @pl.when(i >= 2)
                def _():
                    chunk_copy(i - 2, slot).wait()

                r0 = pl.multiple_of(i * EXP_CHUNK, EXP_CHUNK)
                ring_ref[slot] = _to_slabs(acc_ref[pl.ds(r0, EXP_CHUNK), :])
                chunk_copy(i, slot).start()
                return carry

            lax.fori_loop(0, nchunk, emit, 0)
            for back in range(2):
                @pl.when(nchunk > back)
                def _():
                    last = nchunk - 1 - back
                    chunk_copy(last, last % 2).wait()

    @pl.when(g == n_items)
    def _():
        wait_gather()


def _experts(tile_e, tile_rows, tile_base, row_tok, used_tiles, x1c, w_gu, b_gu, w_down, b_down):
    n_tiles = tile_e.shape[0]
    nf = D_FF // EXP_TF
    n_items = n_tiles * nf

    def item_block(item, te, tr):
        s = item // nf
        return te[s], jnp.where(tr[s] > 0, item % nf, nf - 1)

    def cast_item(g, te, tr):
        return item_block(jnp.minimum(g, n_items - 1), te, tr)

    def use_item(g, te, tr):
        return item_block(jnp.maximum(g - 1, 0), te, tr)

    def w_gate(g, te, tr, tb, rt):
        e, j = cast_item(g, te, tr)
        return e, 0, j

    def w_up(g, te, tr, tb, rt):
        e, j = cast_item(g, te, tr)
        return e, 0, nf + j

    def w_down_map(g, te, tr, tb, rt):
        e, j = cast_item(g, te, tr)
        return e, j, 0

    def b_gate(g, te, tr, tb, rt):
        e, j = use_item(g, te, tr)
        return e, 0, j

    def b_up(g, te, tr, tb, rt):
        e, j = use_item(g, te, tr)
        return e, 0, nf + j

    def b_down_map(g, te, tr, tb, rt):
        e, _ = use_item(g, te, tr)
        return e, 0, 0

    grid_spec = pltpu.PrefetchScalarGridSpec(
        num_scalar_prefetch=4,
        grid=(used_tiles * nf + 1,),
        in_specs=[pl.BlockSpec(memory_space=pl.ANY),
                  pl.BlockSpec((1, D_MODEL, EXP_TF), w_gate),
                  pl.BlockSpec((1, D_MODEL, EXP_TF), w_up),
                  pl.BlockSpec((1, EXP_TF, D_MODEL), w_down_map),
                  pl.BlockSpec((1, 1, EXP_TF), b_gate),
                  pl.BlockSpec((1, 1, EXP_TF), b_up),
                  pl.BlockSpec((1, 1, D_MODEL), b_down_map)],
        out_specs=pl.BlockSpec(memory_space=pl.ANY),
        scratch_shapes=[pltpu.VMEM((EXP_ROWS * ROW_CHUNKS, LANES), U32),
                        pltpu.VMEM((EXP_ROWS, D_MODEL), BF16),
                        pltpu.VMEM((EXP_ROWS, D_MODEL), F32),
                        pltpu.VMEM((2, EXP_CHUNK * ROW_CHUNKS, LANES), U32),
                        pltpu.VMEM((2, D_MODEL, EXP_TF), BF16),
                        pltpu.VMEM((2, D_MODEL, EXP_TF), BF16),
                        pltpu.VMEM((2, EXP_TF, D_MODEL), BF16),
                        pltpu.SemaphoreType.DMA(()),
                        pltpu.SemaphoreType.DMA((2,))])
    return pl.pallas_call(
        _expert_kernel,
        grid_spec=grid_spec,
        out_shape=jax.ShapeDtypeStruct((n_tiles * EXP_ROWS * ROW_CHUNKS, LANES), U32),
        compiler_params=_cparams(("arbitrary",)),
        name="experts")(tile_e, tile_rows, tile_base, row_tok, x1c, w_gu, w_gu, w_down, b_gu, b_gu, b_down)


def _combine_kernel(dest_ref, y_hbm, x_ref, tg_ref, g_ref, be_ref, o_ref, buf_ref, sem):
    tm = o_ref.shape[0]
    i = pl.program_id(0)
    n = pl.num_programs(0)

    def row_copy(src_row, slot, k, t):
        return pltpu.make_async_copy(
            y_hbm.at[pl.ds(pl.multiple_of(src_row * ROW_CHUNKS, ROW_CHUNKS), ROW_CHUNKS), :],
            buf_ref.at[slot, k, pl.ds(pl.multiple_of(t * ROW_CHUNKS, ROW_CHUNKS), ROW_CHUNKS), :],
            sem.at[slot])

    def start_tile(tile, slot):
        base = tile * tm * TOP_K

        def body(t2, carry):
            for u in range(2):
                t = t2 * 2 + u
                for k in range(TOP_K):
                    row_copy(dest_ref[base + t * TOP_K + k], slot, k, t).start()
            return carry

        lax.fori_loop(0, tm // 2, body, 0)

    def wait_tile(slot):
        for k in range(TOP_K):
            pltpu.make_async_copy(y_hbm.at[pl.ds(0, tm * ROW_CHUNKS), :], buf_ref.at[slot, k], sem.at[slot]).wait()

    @pl.when(i == 0)
    def _():
        start_tile(0, 0)

    @pl.when(i + 1 < n)
    def _():
        start_tile(jnp.minimum(i + 1, n - 1), (i + 1) % 2)

    slot = i % 2
    wait_tile(slot)
    tg = tg_ref[...]
    ff = jnp.zeros((tm, D_MODEL), F32)
    for k in range(TOP_K):
        ff = ff + tg[:, k:k + 1] * _from_slabs(buf_ref[slot, k])
    o_ref[...] = _layer_norm(DN_ALPHA * x_ref[...] + ff, g_ref[...], be_ref[...])


def _combine_ln(dest_flat, y_buf, x1, tg, g, be):
    t = tg.shape[0]
    tm = CMB_TM
    grid_spec = pltpu.PrefetchScalarGridSpec(
        num_scalar_prefetch=1,
        grid=(t // tm,),
        in_specs=[pl.BlockSpec(memory_space=pl.ANY),
                  pl.BlockSpec((tm, D_MODEL), lambda i, d: (i, 0)),
                  pl.BlockSpec((tm, LANES), lambda i, d: (i, 0)),
                  pl.BlockSpec((1, D_MODEL), lambda i, d: (0, 0)),
                  pl.BlockSpec((1, D_MODEL), lambda i, d: (0, 0))],
        out_specs=pl.BlockSpec((tm, D_MODEL), lambda i, d: (i, 0)),
        scratch_shapes=[pltpu.VMEM((2, TOP_K, tm * ROW_CHUNKS, LANES), U32), pltpu.SemaphoreType.DMA((2,))])
    return pl.pallas_call(
        _combine_kernel,
        grid_spec=grid_spec,
        out_shape=jax.ShapeDtypeStruct((t, D_MODEL), F32),
        compiler_params=_cparams(("arbitrary",)),
        name="combine_ln2")(dest_flat, y_buf, x1, tg, g, be)


def _dft_kernel(cd_ref, sd_ref, ca_ref, sa_ref, c_ref, s_ref):
    cd, sd = cd_ref[0], sd_ref[0]
    ca, sa = ca_ref[0], sa_ref[0]
    c_ref[0] = (cd * ca - sd * sa).astype(c_ref.dtype)
    s_ref[0] = (sd * ca + cd * sa).astype(s_ref.dtype)


def _dft_tables(seq):
    n = 2 * seq
    half = seq // 2
    nblk = half // DFT_TB
    idx = jnp.arange(half, dtype=I32)
    off = jnp.arange(DFT_TB, dtype=I32)
    start = jnp.arange(nblk, dtype=I32) * DFT_TB

    def angle(prod):
        return (prod % n).astype(F32) * (2.0 * math.pi / n)

    ang_d = jnp.stack([angle(off[:, None] * (2 * idx)[None, :]),
                       angle(off[:, None] * (2 * idx + 1)[None, :]),
                       angle((2 * off + 1)[:, None] * idx[None, :])])
    ang_a = jnp.stack([angle(start[:, None] * (2 * idx)[None, :]),
                       angle(start[:, None] * (2 * idx + 1)[None, :]),
                       angle((2 * start)[:, None] * idx[None, :])]).reshape(3 * nblk, 1, half)
    small = pl.BlockSpec((1, DFT_TB, half), lambda k, a: (k, 0, 0))
    base = pl.BlockSpec((1, 1, half), lambda k, a: (k * nblk + a, 0, 0))
    out = pl.BlockSpec((1, DFT_TB, half), lambda k, a: (k, a, 0))
    sds = jax.ShapeDtypeStruct((3, half, half), BF16)
    return pl.pallas_call(
        _dft_kernel, grid=(3, nblk), in_specs=[small, small, base, base], out_specs=[out, out],
        out_shape=[sds, sds], compiler_params=_cparams(("arbitrary", "arbitrary")),
        name="dft_tables")(jnp.cos(ang_d), jnp.sin(ang_d), jnp.cos(ang_a), jnp.sin(ang_a))


def _filter_features(seq):
    t = jnp.linspace(0.0, 1.0, seq, dtype=F32)[:, None]
    bands = (HY_EMB - 1) // 2
    fb = jnp.linspace(1e-4, bands - 1, bands, dtype=F32)[None]
    w = 2.0 * math.pi * jnp.arange(seq, dtype=F32)[:, None] / seq
    z = jnp.concatenate([t, jnp.cos(fb * w), -jnp.sin(fb * w)], -1)
    z = jnp.concatenate([z[0::2], z[1::2]], axis=0)
    return jnp.pad(z, ((0, 0), (0, LANES - HY_EMB)))


def _mixer(x, w_in, b_in, hy_conv_w, hy_conv_b, hy_filt_w1, hy_filt_b1, hy_filt_w2, hy_filt_b2,
           hy_filt_w3, hy_filt_freq, hy_skip, hy_norm_w, ml_conv_w, ml_conv_b, ml_norm_w):
    bsz, seq, _ = x.shape
    t = bsz * seq
    x2d = x.reshape(t, D_MODEL)
    n_main = w_in.shape[1] - N_GATE_COLS
    w_t = jnp.swapaxes(w_in, 0, 1)
    wg = jnp.pad(w_t[n_main:], ((0, LANES - N_GATE_COLS), (0, 0)))
    bg = jnp.pad(b_in[None, n_main:], ((0, 0), (0, LANES - N_GATE_COLS)))
    proj, gates = _in_proj(x2d, w_t[:n_main].astype(BF16), b_in[None, :n_main], wg, bg)
    proj3 = proj.reshape(bsz, seq, n_main)
    gates = gates[:, :N_GATE_COLS]
    g5 = gates.reshape(bsz, seq, 4, ML_HEADS)
    grow = g5.transpose(0, 3, 2, 1).reshape(bsz, ML_HEADS, 4, seq // CHUNK, CHUNK)
    gtr = grow.transpose(0, 1, 2, 4, 3)

    cmat, smat = _dft_tables(seq)
    zpad = _filter_features(seq)
    w1pad = jnp.pad(hy_filt_w1, ((0, LANES - HY_EMB), (0, 0)))
    deltas = jnp.abs(jnp.linspace(math.log(HY_DECAY_TARGET) / HY_SLOW_PCT,
                                  math.log(HY_DECAY_TARGET) / HY_FAST_PCT, D_HY, dtype=F32))[None]
    kr, ki, km = _hyena_filters(zpad, w1pad, hy_filt_b1[None], hy_filt_w2, hy_filt_b2[None],
                                hy_filt_freq, hy_filt_w3, deltas, cmat, smat)
    y_hy = _hyena(proj3, hy_conv_w, hy_conv_b[None], cmat, smat, kr, ki, km, hy_skip, hy_norm_w[None])
    y_ml = _mlstm(proj3, ml_conv_w, ml_conv_b[None], grow, gtr, ml_norm_w[None])
    return y_hy.reshape(t, D_HY), y_ml.reshape(t, D_ML), x2d


def _moe_tables(top_i, slot, counts):
    t = top_i.shape[0]
    n_tiles = N_EXPERTS + (t * TOP_K) // EXP_ROWS
    ntile = (counts + EXP_ROWS - 1) // EXP_ROWS
    ends = jnp.cumsum(ntile)
    starts = ends - ntile
    total = ends[-1]
    s_idx = jnp.arange(n_tiles, dtype=I32)
    valid = s_idx < total
    s_eff = jnp.where(valid, s_idx, jnp.maximum(total - 1, 0))
    tile_e = jnp.minimum(jnp.sum((s_eff[:, None] >= ends[None, :]).astype(I32), axis=1), N_EXPERTS - 1)
    local = s_eff - starts[tile_e]
    tile_rows = jnp.where(valid, jnp.clip(counts[tile_e] - local * EXP_ROWS, 0, EXP_ROWS), 0).astype(I32)
    tok = jnp.arange(t, dtype=I32)[:, None]
    row_tok = jnp.pad(jnp.sort((top_i * t + tok).reshape(-1)) % t, (0, EXP_ROWS))
    first = jnp.cumsum(counts) - counts
    tile_base = (first[tile_e] + local * EXP_ROWS).astype(I32)
    onehot = top_i[:, :, None] == jnp.arange(N_EXPERTS, dtype=I32)
    dest = jnp.sum(jnp.where(onehot, starts * EXP_ROWS, 0), axis=-1) + slot
    return (tile_e.astype(I32), tile_rows, tile_base, row_tok.astype(I32), total.astype(I32),
            dest.astype(I32).reshape(-1))


def kernel(x, w_in, b_in, hy_conv_w, hy_conv_b, hy_filt_w1, hy_filt_b1, hy_filt_w2, hy_filt_b2, hy_filt_w3, hy_filt_freq, hy_skip, hy_norm_w, ml_conv_w, ml_conv_b, ml_norm_w, w_out, b_out, ln1_g, ln1_b, router_w, router_b, w_gu, b_gu, w_down, b_down, ln2_g, ln2_b):
    bsz, seq, _ = x.shape
    l = 0
    y_hy, y_ml, x2d = _mixer(x, w_in[l], b_in[l], hy_conv_w[l], hy_conv_b[l], hy_filt_w1[l], hy_filt_b1[l],
                             hy_filt_w2[l], hy_filt_b2[l], hy_filt_w3[l], hy_filt_freq[l], hy_skip[l],
                             hy_norm_w[l], ml_conv_w[l], ml_conv_b[l], ml_norm_w[l])
    rw = jnp.pad(router_w[l], ((0, 0), (0, LANES - N_EXPERTS)))
    rb = jnp.pad(router_b[l][None], ((0, 0), (0, LANES - N_EXPERTS)), constant_values=-1e30)
    x1, x1c, top_i, top_g, slot, cnt = _out_proj_ln_route(
        y_hy, y_ml, x2d, w_out[l].astype(BF16), b_out[l][None], ln1_g[l][None], ln1_b[l][None], rw, rb)
    counts = cnt[0, :N_EXPERTS].astype(I32)
    tile_e, tile_rows, tile_base, row_tok, used, dest = _moe_tables(top_i[:, :TOP_K], slot[:, :TOP_K], counts)
    y_buf = _experts(tile_e, tile_rows, tile_base, row_tok, used, x1c, w_gu[l], b_gu[l][:, None, :], w_down[l],
                     b_down[l][:, None, :])
    out = _combine_ln(dest, y_buf, x1, top_g, ln2_g[l][None], ln2_b[l][None])
    return out.reshape(bsz, seq, D_MODEL)
```

```python
import functools
import math

import jax
import jax.numpy as jnp
from jax import lax
from jax.experimental import pallas as pl
from jax.experimental.pallas import tpu as pltpu

F32 = jnp.float32
BF16 = jnp.bfloat16
I32 = jnp.int32
U32 = jnp.uint32
HP = lax.Precision.HIGHEST

D_MODEL = 2048
D_HY = 1024
D_ML = 1024
ML_HEADS = 8
HEAD_DIM = 128
CHUNK = 128
N_GATE_COLS = 32
HY_EMB = 33
N_EXPERTS = 32
TOP_K = 4
D_FF = 2048
SWIGLU_LIMIT = 7.0
SWIGLU_ALPHA = 1.702
LN_EPS = 1e-5
DN_ALPHA = 2.0 ** 0.25
HY_DECAY_TARGET = 1e-2
HY_FAST_PCT = 0.3
HY_SLOW_PCT = 1.5

LANES = 128
ROW_CHUNKS = D_MODEL // (2 * LANES)
VMEM_LIMIT = 60 * 1024 * 1024

PROJ_TM = 2048
PROJ_TN = 512
PROJ_GATE_ROWS = 512
HY_CW = 256
HY_FC = 512
OUT_TM = 512
EXP_ROWS = 1152
EXP_M_SIZES = (1024, 1056, 1088, 1120, 1152)
EXP_CHUNK = 128
EXP_TF = 256
GATHER_UNROLL = 8
CMB_TM = 128
DFT_TB = 128


def _cparams(sem):
    return pltpu.CompilerParams(dimension_semantics=sem, vmem_limit_bytes=VMEM_LIMIT)


def _split(a):
    hi = a.astype(BF16)
    return hi, (a - hi.astype(F32)).astype(BF16)


def _dot3(a, b, dims):
    a_hi, a_lo = _split(a)
    b_hi, b_lo = _split(b)
    mm = functools.partial(lax.dot_general, dimension_numbers=dims, preferred_element_type=F32)
    return mm(a_hi, b_hi) + (mm(a_hi, b_lo) + mm(a_lo, b_hi))


NN = (((1,), (0,)), ((), ()))
NT = (((1,), (1,)), ((), ()))


def _const_spec(shape, index_map):
    return pl.BlockSpec(shape, index_map, pipeline_mode=pl.Buffered(1))


def _proj_kernel(x_ref, w_ref, b_ref, wg_ref, bg_ref, o_ref, og_ref, xb_ref):
    @pl.when(pl.program_id(1) == 0)
    def _():
        xb_ref[...] = x_ref[...].astype(BF16)
        wg_hi, wg_lo = _split(wg_ref[...])
        mm = functools.partial(lax.dot_general, dimension_numbers=NT, preferred_element_type=F32)
        for r in range(x_ref.shape[0] // PROJ_GATE_ROWS):
            rows = slice(r * PROJ_GATE_ROWS, (r + 1) * PROJ_GATE_ROWS)
            x_hi = xb_ref[rows, :]
            x_lo = (x_ref[rows, :] - x_hi.astype(F32)).astype(BF16)
            og_ref[rows, :] = mm(x_hi, wg_hi) + (mm(x_hi, wg_lo) + mm(x_lo, wg_hi)) + bg_ref[...]

    acc = lax.dot_general(xb_ref[...], w_ref[...], NT, preferred_element_type=F32)
    o_ref[...] = (acc + b_ref[...]).astype(o_ref.dtype)


def _in_proj(x2d, wt_bf, b_row, wt_gate, b_gate):
    m, k = x2d.shape
    n = wt_bf.shape[0]
    return pl.pallas_call(
        _proj_kernel,
        grid=(m // PROJ_TM, n // PROJ_TN),
        in_specs=[pl.BlockSpec((PROJ_TM, k), lambda i, j: (i, 0)),
                  pl.BlockSpec((PROJ_TN, k), lambda i, j: (j, 0)),
                  pl.BlockSpec((1, PROJ_TN), lambda i, j: (0, j)),
                  _const_spec((LANES, k), lambda i, j: (0, 0)),
                  _const_spec((1, LANES), lambda i, j: (0, 0))],
        out_specs=[pl.BlockSpec((PROJ_TM, PROJ_TN), lambda i, j: (i, j)),
                   pl.BlockSpec((PROJ_TM, LANES), lambda i, j: (i, 0))],
        out_shape=[jax.ShapeDtypeStruct((m, n), BF16), jax.ShapeDtypeStruct((m, LANES), F32)],
        scratch_shapes=[pltpu.VMEM((PROJ_TM, k), BF16)],
        compiler_params=_cparams(("arbitrary", "arbitrary")),
        name="in_proj")(x2d, wt_bf, b_row, wt_gate, b_gate)


def _filter_kernel(z_ref, w1_ref, b1_ref, w2_ref, b2_ref, fq_ref, w3f_ref, w3b_ref, dl_ref,
                   c_ref, s_ref, kr_ref, ki_ref, km_ref, h_ref):
    seq = z_ref.shape[0]
    inv_n = 1.0 / (2 * seq)
    z = z_ref[...]

    @pl.when(jnp.logical_and(pl.program_id(0) == 0, pl.program_id(1) == 0))
    def _():
        h1 = jnp.sin(fq_ref[0:1, :] * (jnp.dot(z, w1_ref[...], precision=HP, preferred_element_type=F32)
                                       + b1_ref[...]))
        h_ref[...] = jnp.sin(fq_ref[1:2, :] * (jnp.dot(h1, w2_ref[...], precision=HP, preferred_element_type=F32)
                                               + b2_ref[...]))

    half = seq // 2
    h = h_ref[...]
    win = jnp.exp(-z[:, 0:1] * dl_ref[...])
    fwd = _dot3(h, w3f_ref[...], NN) * win
    bwd = _dot3(h, w3b_ref[...], NN) * win
    row = lax.broadcasted_iota(I32, fwd.shape, 0)
    bwd = jnp.where(row == 0, 0.0, bwd)
    inv = 1.0 / jnp.sum(jnp.abs(fwd) + jnp.abs(bwd), axis=0, keepdims=True)
    ks = ((fwd + bwd) * inv)
    kd = ((fwd - bwd) * inv)
    ksb = ks.astype(BF16)
    kdb = kd.astype(BF16)
    ec = jnp.dot(c_ref[0], ksb[:half], preferred_element_type=F32)
    oc = jnp.dot(c_ref[1], ksb[half:], preferred_element_type=F32)
    es = jnp.dot(s_ref[0], kdb[:half], preferred_element_type=F32)
    os_ = jnp.dot(s_ref[1], kdb[half:], preferred_element_type=F32)
    rowh = lax.broadcasted_iota(I32, ec.shape, 0)
    wf = jnp.where(rowh == 0, inv_n, 2.0 * inv_n)
    kr_ref[0, :half, :] = (ec + oc) * wf
    kr_ref[0, half:, :] = (ec - oc) * wf
    ki_ref[0, :half, :] = -(es + os_) * wf
    ki_ref[0, half:, :] = (es - os_) * wf
    alt = jnp.where((rowh & 1) == 0, 1.0, -1.0)
    km_ref[0, 0:1, :] = jnp.sum(ks[:half] * alt, axis=0, keepdims=True) * (2.0 * inv_n)
    km_ref[0, 1:2, :] = -jnp.sum(kd[half:] * alt, axis=0, keepdims=True) * (2.0 * inv_n)


def _hyena_filters(zpad, w1pad, b1, w2, b2, freq, w3, deltas, cmat, smat):
    seq = zpad.shape[0]
    nb = D_HY // HY_CW
    hid = w2.shape[0]
    full = lambda shape: pl.BlockSpec(shape, lambda o, c: (0,) * len(shape))
    out_sds = jax.ShapeDtypeStruct((2, seq, D_HY), F32)
    return pl.pallas_call(
        _filter_kernel,
        grid=(2, nb),
        in_specs=[full(zpad.shape), full(w1pad.shape), full(b1.shape), full(w2.shape), full(b2.shape),
                  full(freq.shape),
                  pl.BlockSpec((hid, HY_CW), lambda o, c: (0, o * 2 * nb + c)),
                  pl.BlockSpec((hid, HY_CW), lambda o, c: (0, o * 2 * nb + nb + c)),
                  pl.BlockSpec((1, HY_CW), lambda o, c: (0, c)),
                  _const_spec(cmat.shape, lambda o, c: (0, 0, 0)),
                  _const_spec(smat.shape, lambda o, c: (0, 0, 0))],
        out_specs=[pl.BlockSpec((1, seq, HY_CW), lambda o, c: (o, 0, c)),
                   pl.BlockSpec((1, seq, HY_CW), lambda o, c: (o, 0, c)),
                   pl.BlockSpec((1, 2, HY_CW), lambda o, c: (o, 0, c))],
        out_shape=[out_sds, out_sds, jax.ShapeDtypeStruct((2, 2, D_HY), F32)],
        scratch_shapes=[pltpu.VMEM((seq, hid), F32)],
        compiler_params=_cparams(("arbitrary", "arbitrary")),
        name="hyena_filters")(zpad, w1pad, b1, w2, b2, freq, w3, w3, deltas, cmat, smat)


def _short_conv(u, w_ref, b_ref, row, seq):
    prev = jnp.where(row == 0, 0.0, pltpu.roll(u, 1, 0))
    nxt = jnp.where(row == seq - 1, 0.0, pltpu.roll(u, seq - 1, 0))
    return w_ref[0:1, :] * prev + w_ref[1:2, :] * u + w_ref[2:3, :] * nxt + b_ref[...]


def _hyena_kernel(uv_ref, u1_ref, u2_ref, wv_ref, w1_ref, w2_ref, bv_ref, b1_ref, b2_ref,
                  c_ref, s_ref, kr_ref, ki_ref, km_ref, skip_ref, nw_ref, o_ref,
                  a_ref, b_ref, t_ref, zb_ref, s1_ref, s2_ref, s3_ref, s4_ref, md_ref):
    seq = uv_ref.shape[1]
    cw = uv_ref.shape[2]
    half = seq // 2
    nblk = half // HY_FC
    row = lax.broadcasted_iota(I32, (seq, LANES), 0)
    alt_half = jnp.where((lax.broadcasted_iota(I32, (half, cw), 0) & 1) == 0, 1.0, -1.0)
    alt_blk = jnp.where((lax.broadcasted_iota(I32, (HY_FC, cw), 0) & 1) == 0, 1.0, -1.0)
    groups = [slice(g * LANES, (g + 1) * LANES) for g in range(cw // LANES)]

    def conv_to(dst_ref, u_ref, w_ref, bias_ref):
        for gi, gs in enumerate(groups):
            t_ref[gi] = _short_conv(u_ref[0, :, gs].astype(F32), w_ref.at[:, gs], bias_ref.at[:, gs], row, seq)
            dst_ref[0:half, gs] = t_ref[gi, pl.ds(0, half, stride=2), :]
            dst_ref[half:seq, gs] = t_ref[gi, pl.ds(1, half, stride=2), :]

    def spectrum(zin_ref, o):
        z = zin_ref[...]
        zb_ref[...] = z.astype(BF16)
        am = jnp.sum(z[:half] * alt_half, axis=0, keepdims=True)
        bm = jnp.sum(z[half:] * alt_half, axis=0, keepdims=True)
        krm, kim = km_ref[o, 0:1, :], km_ref[o, 1:2, :]
        md_ref[0:1, :] = am * krm + bm * kim
        md_ref[1:2, :] = am * kim - bm * krm
        ze = zb_ref[0:half, :]
        zo = zb_ref[half:seq, :]
        for gb in range(nblk):
            lo = slice(gb * HY_FC, (gb + 1) * HY_FC)
            hi = slice(half + gb * HY_FC, half + (gb + 1) * HY_FC)
            ec = jnp.dot(c_ref[0, lo, :], ze, preferred_element_type=F32)
            oc = jnp.dot(c_ref[1, lo, :], zo, preferred_element_type=F32)
            es = jnp.dot(s_ref[0, lo, :], ze, preferred_element_type=F32)
            os_ = jnp.dot(s_ref[1, lo, :], zo, preferred_element_type=F32)
            a_lo, b_lo = ec + oc, es + os_
            a_hi, b_hi = ec - oc, os_ - es
            krl, kil = kr_ref[o, lo, :], ki_ref[o, lo, :]
            krh, kih = kr_ref[o, hi, :], ki_ref[o, hi, :]
            pr = a_lo * krl + b_lo * kil
            pi = a_lo * kil - b_lo * krl
            qr = a_hi * krh + b_hi * kih
            qi = a_hi * kih - b_hi * krh
            s1_ref[lo, :] = (pr + qr).astype(BF16)
            s2_ref[lo, :] = (pi - qi).astype(BF16)
            s3_ref[lo, :] = (pr - qr).astype(BF16)
            s4_ref[lo, :] = (pi + qi).astype(BF16)

    def conv_rows(ub, parity):
        us = slice(ub * HY_FC, (ub + 1) * HY_FC)
        if parity == 0:
            y = jnp.dot(c_ref[0, us, :], s1_ref[...], preferred_element_type=F32)
            y = y - jnp.dot(s_ref[0, us, :], s2_ref[...], preferred_element_type=F32)
            y = y + md_ref[0:1, :] * alt_blk
        else:
            y = jnp.dot(c_ref[2, us, :], s3_ref[...], preferred_element_type=F32)
            y = y - jnp.dot(s_ref[2, us, :], s4_ref[...], preferred_element_type=F32)
            y = y - md_ref[1:2, :] * alt_blk
        return slice(parity * half + ub * HY_FC, parity * half + (ub + 1) * HY_FC), y

    blocks = [(ub, parity) for parity in range(2) for ub in range(nblk)]
    conv_to(a_ref, uv_ref, wv_ref, bv_ref)
    conv_to(b_ref, u1_ref, w1_ref, b1_ref)
    spectrum(a_ref, 0)
    for ub, parity in blocks:
        rows, y = conv_rows(ub, parity)
        b_ref[rows, :] = b_ref[rows, :] * (y + skip_ref[0:1, :] * a_ref[rows, :])
    conv_to(a_ref, u2_ref, w2_ref, b2_ref)
    spectrum(b_ref, 1)
    for ub, parity in blocks:
        rows, y = conv_rows(ub, parity)
        z = a_ref[rows, :] * (y + skip_ref[1:2, :] * b_ref[rows, :])
        for gi, gs in enumerate(groups):
            zg = z[:, gs]
            mu = jnp.mean(zg, axis=-1, keepdims=True)
            zc = zg - mu
            var = jnp.mean(zc * zc, axis=-1, keepdims=True)
            t_ref[gi, pl.ds(2 * ub * HY_FC + parity, HY_FC, stride=2), :] = (
                zc * lax.rsqrt(var + LN_EPS) * nw_ref[:, gs])
    for gi, gs in enumerate(groups):
        o_ref[0, :, gs] = t_ref[gi].astype(o_ref.dtype)


def _hyena(proj3, conv_w, conv_b, cmat, smat, kr, ki, km, skip, norm_w):
    bsz, seq, _ = proj3.shape
    half = seq // 2
    nb = D_HY // HY_CW
    u_spec = lambda off: pl.BlockSpec((1, seq, HY_CW), lambda c, b: (b, 0, off + c))
    w_spec = lambda off: pl.BlockSpec((3, HY_CW), lambda c, b: (0, off + c))
    b_spec = lambda off: pl.BlockSpec((1, HY_CW), lambda c, b: (0, off + c))
    return pl.pallas_call(
        _hyena_kernel,
        grid=(nb, bsz),
        in_specs=[u_spec(0), u_spec(nb), u_spec(2 * nb),
                  w_spec(0), w_spec(nb), w_spec(2 * nb),
                  b_spec(0), b_spec(nb), b_spec(2 * nb),
                  _const_spec(cmat.shape, lambda c, b: (0, 0, 0)),
                  _const_spec(smat.shape, lambda c, b: (0, 0, 0)),
                  _const_spec((2, seq, HY_CW), lambda c, b: (0, 0, c)),
                  _const_spec((2, seq, HY_CW), lambda c, b: (0, 0, c)),
                  pl.BlockSpec((2, 2, HY_CW), lambda c, b: (0, 0, c)),
                  pl.BlockSpec((2, HY_CW), lambda c, b: (0, c)),
                  pl.BlockSpec((1, HY_CW), lambda c, b: (0, c))],
        out_specs=pl.BlockSpec((1, seq, HY_CW), lambda c, b: (b, 0, c)),
        out_shape=jax.ShapeDtypeStruct((bsz, seq, D_HY), BF16),
        scratch_shapes=[pltpu.VMEM((seq, HY_CW), F32), pltpu.VMEM((seq, HY_CW), F32),
                        pltpu.VMEM((HY_CW // LANES, seq, LANES), F32), pltpu.VMEM((seq, HY_CW), BF16),
                        pltpu.VMEM((half, HY_CW), BF16), pltpu.VMEM((half, HY_CW), BF16),
                        pltpu.VMEM((half, HY_CW), BF16), pltpu.VMEM((half, HY_CW), BF16),
                        pltpu.VMEM((2, HY_CW), F32)],
        compiler_params=_cparams(("arbitrary", "arbitrary")),
        name="hyena")(proj3, proj3, proj3, conv_w, conv_w, conv_w, conv_b, conv_b, conv_b,
                      cmat, smat, kr, ki, km, skip, norm_w)


def _mlstm_kernel(qp_ref, kp_ref, v_ref, og_ref, wq_ref, wk_ref, bq_ref, bk_ref, gr_ref, gt_ref,
                  nw_ref, o_ref, qb_ref, kb_ref, hacc_ref, cst_ref, nst_ref):
    seq = qp_ref.shape[1]
    d = qp_ref.shape[2]
    nchunk = seq // CHUNK
    row = lax.broadcasted_iota(I32, (seq, d), 0)
    q = _short_conv(qp_ref[0].astype(F32), wq_ref, bq_ref, row, seq)
    k = _short_conv(kp_ref[0].astype(F32), wk_ref, bk_ref, row, seq)
    qb_ref[...] = (q * jax.nn.sigmoid(q)).astype(BF16)
    kb_ref[...] = ((k * jax.nn.sigmoid(k)) * (d ** -0.5)).astype(BF16)

    ti = lax.broadcasted_iota(I32, (CHUNK, CHUNK), 0)
    si = lax.broadcasted_iota(I32, (CHUNK, CHUNK), 1)
    lower = ti >= si
    upper = ti <= si
    lower_f = lower.astype(F32)
    upper_f = upper.astype(F32)
    nt = (((1,), (1,)), ((), ()))
    tn = (((0,), (0,)), ((), ()))
    chunk_rows = [slice(c * CHUNK, (c + 1) * CHUNK) for c in range(nchunk)]

    per_dir = []
    for direction in range(2):
        f_idx, i_idx = 2 * direction + 1, 2 * direction
        mask = lower if direction == 0 else upper
        order = list(range(nchunk)) if direction == 0 else list(range(nchunk - 1, -1, -1))
        lf_r = jax.nn.log_sigmoid(gr_ref[0, 0, f_idx])
        b_r = jnp.dot(lf_r, upper_f if direction == 0 else lower_f, precision=HP, preferred_element_type=F32)
        rterm = b_r - gr_ref[0, 0, i_idx]
        b_last = jnp.sum(lf_r, axis=-1, keepdims=True)
        lf_c = jax.nn.log_sigmoid(gt_ref[0, 0, f_idx])
        b_c = jnp.dot(lower_f if direction == 0 else upper_f, lf_c, precision=HP, preferred_element_type=F32)
        i_c = gt_ref[0, 0, i_idx]

        bcol, gcol, gmax, rowmax, blast = [], [], [], [], []
        for c in range(nchunk):
            bc = jnp.broadcast_to(b_c[:, c:c + 1], (CHUNK, CHUNK))
            ic = jnp.broadcast_to(i_c[:, c:c + 1], (CHUNK, CHUNK))
            bl = jnp.broadcast_to(b_last[c:c + 1, :], (1, CHUNK))
            dmat = jnp.where(mask, bc - rterm[c:c + 1, :], -jnp.inf)
            g = bl - bc + ic
            bcol.append(bc)
            gcol.append(g)
            blast.append(bl)
            rowmax.append(jnp.max(dmat, axis=-1, keepdims=True))
            gmax.append(jnp.max(g, axis=0, keepdims=True))

        m = jnp.zeros((1, CHUNK), F32)
        m_in, m_out = [None] * nchunk, [None] * nchunk
        for c in order:
            m_in[c] = m
            m = jnp.maximum(blast[c] + m, gmax[c])
            m_out[c] = m

        cmat = jnp.zeros((d, d), F32)
        nvec = jnp.zeros((1, d), F32)
        for c in order:
            kc = kb_ref[chunk_rows[c], :]
            vc = v_ref[0, chunk_rows[c], :]
            cst_ref[direction, c] = cmat.astype(BF16)
            nst_ref[direction, c] = nvec
            wg = jnp.exp(gcol[c] - m_out[c])
            decay = jnp.exp(blast[c] + m_in[c] - m_out[c])
            upd = lax.dot_general((wg * vc.astype(F32)).astype(BF16), kc, tn, preferred_element_type=F32)
            cmat = decay * cmat + upd
            nvec = decay * nvec + jnp.sum(wg * kc.astype(F32), axis=0, keepdims=True)
        per_dir.append((mask, bcol, rterm, rowmax, m_in))

    for c in range(nchunk):
        rs = chunk_rows[c]
        qc = qb_ref[rs, :]
        kc = kb_ref[rs, :]
        vc = v_ref[0, rs, :]
        qk = lax.dot_general(qc, kc, nt, preferred_element_type=F32)
        h_sum = None
        for direction, (mask, bcol, rterm, rowmax, m_in) in enumerate(per_dir):
            dmat = jnp.where(mask, bcol[c] - rterm[c:c + 1, :], -jnp.inf)
            inter = bcol[c] + m_in[c]
            m_t = jnp.maximum(inter, rowmax[c])
            s = qk * jnp.exp(dmat - m_t)
            inter_w = jnp.exp(inter - m_t)
            cq = lax.dot_general(qc, cst_ref[direction, c], nt, preferred_element_type=F32)
            num = jnp.dot(s.astype(BF16), vc, preferred_element_type=F32) + inter_w * cq
            nq = jnp.sum(qc.astype(F32) * nst_ref[direction, c], axis=-1, keepdims=True)
            den = jnp.sum(s, axis=-1, keepdims=True) + inter_w * nq
            h = num / jnp.maximum(jnp.abs(den), jnp.exp(-m_t))
            h_sum = h if h_sum is None else h_sum + h
        hacc_ref[rs, :] = h_sum

    h = hacc_ref[...]
    mu = jnp.mean(h, axis=-1, keepdims=True)
    hc = h - mu
    var = jnp.mean(hc * hc, axis=-1, keepdims=True)
    y = hc * lax.rsqrt(var + LN_EPS) * nw_ref[...] * jax.nn.sigmoid(og_ref[0].astype(F32))
    o_ref[0] = y.astype(o_ref.dtype)


def _mlstm(proj3, conv_w, conv_b, grow, gtr, norm_w):
    bsz, seq, _ = proj3.shape
    d = HEAD_DIM
    nchunk = seq // CHUNK
    hy_blocks = 3 * D_HY // d
    qoff, koff, voff, ooff = hy_blocks, hy_blocks + ML_HEADS, hy_blocks + 2 * ML_HEADS, hy_blocks + 3 * ML_HEADS
    p_spec = lambda off: pl.BlockSpec((1, seq, d), lambda b, h: (b, 0, off + h))
    return pl.pallas_call(
        _mlstm_kernel,
        grid=(bsz, ML_HEADS),
        in_specs=[p_spec(qoff), p_spec(koff), p_spec(voff), p_spec(ooff),
                  pl.BlockSpec((3, d), lambda b, h: (0, h)),
                  pl.BlockSpec((3, d), lambda b, h: (0, ML_HEADS + h)),
                  pl.BlockSpec((1, d), lambda b, h: (0, h)),
                  pl.BlockSpec((1, d), lambda b, h: (0, ML_HEADS + h)),
                  pl.BlockSpec((1, 1, 4, nchunk, CHUNK), lambda b, h: (b, h, 0, 0, 0)),
                  pl.BlockSpec((1, 1, 4, CHUNK, nchunk), lambda b, h: (b, h, 0, 0, 0)),
                  pl.BlockSpec((1, d), lambda b, h: (0, h))],
        out_specs=pl.BlockSpec((1, seq, d), lambda b, h: (b, 0, h)),
        out_shape=jax.ShapeDtypeStruct((bsz, seq, D_ML), BF16),
        scratch_shapes=[pltpu.VMEM((seq, d), BF16), pltpu.VMEM((seq, d), BF16),
                        pltpu.VMEM((seq, d), F32),
                        pltpu.VMEM((2, nchunk, d, d), BF16), pltpu.VMEM((2, nchunk, 1, d), F32)],
        compiler_params=_cparams(("arbitrary", "arbitrary")),
        name="mlstm")(proj3, proj3, proj3, proj3, conv_w, conv_w, conv_b, conv_b, grow, gtr, norm_w)


def _to_slabs(y):
    n = y.shape[0]
    half = D_MODEL // 2
    lo = lax.bitcast_convert_type(y[:, :half].astype(BF16).astype(F32), U32) >> 16
    hi = lax.bitcast_convert_type(y[:, half:].astype(BF16).astype(F32), U32) & jnp.uint32(0xFFFF0000)
    words = hi | lo
    parts = jnp.stack([words[:, c * LANES:(c + 1) * LANES] for c in range(ROW_CHUNKS)], axis=0)
    return pltpu.einshape("crl->rcl", parts).reshape(n * ROW_CHUNKS, LANES)


def _from_slabs(v):
    n = v.shape[0] // ROW_CHUNKS
    parts = pltpu.einshape("rcl->crl", v.reshape(n, ROW_CHUNKS, LANES))
    lo = [lax.bitcast_convert_type(parts[c] << 16, F32) for c in range(ROW_CHUNKS)]
    hi = [lax.bitcast_convert_type(parts[c] & jnp.uint32(0xFFFF0000), F32) for c in range(ROW_CHUNKS)]
    return jnp.concatenate(lo + hi, axis=-1)


def _layer_norm(u, g, b):
    mu = jnp.mean(u, axis=-1, keepdims=True)
    uc = u - mu
    var = jnp.mean(uc * uc, axis=-1, keepdims=True)
    return uc * lax.rsqrt(var + LN_EPS) * g + b


def _route(x, w_ref, b_ref, ti_ref, tg_ref, tp_ref, cnt_ref):
    tm = ti_ref.shape[0]

    @pl.when(pl.program_id(0) == 0)
    def _():
        cnt_ref[...] = jnp.zeros_like(cnt_ref)

    logits = _dot3(x, w_ref[...], NN) + b_ref[...]
    lane = lax.broadcasted_iota(I32, (tm, LANES), 1)
    work = logits
    vals, idxs = [], []
    chosen = jnp.zeros((tm, LANES), F32)
    for _ in range(TOP_K):
        mx = jnp.max(work, axis=-1, keepdims=True)
        idx = jnp.min(jnp.where(work == mx, lane, LANES), axis=-1, keepdims=True)
        hit = lane == idx
        vals.append(mx)
        idxs.append(idx)
        chosen = jnp.where(hit, 1.0, chosen)
        work = jnp.where(hit, -jnp.inf, work)
    exps = [jnp.exp(v - vals[0]) for v in vals]
    den = exps[0] + exps[1] + exps[2] + exps[3]
    ri = lax.broadcasted_iota(I32, (tm, tm), 0)
    ci = lax.broadcasted_iota(I32, (tm, tm), 1)
    strict_lower = (ri > ci).astype(BF16)
    carry = cnt_ref[...]
    slot = carry + jnp.dot(strict_lower, chosen.astype(BF16), preferred_element_type=F32)
    ti = jnp.zeros((tm, LANES), I32)
    tg = jnp.zeros((tm, LANES), F32)
    tp = jnp.zeros((tm, LANES), F32)
    for k in range(TOP_K):
        sk = jnp.sum(jnp.where(lane == idxs[k], slot, 0.0), axis=-1, keepdims=True)
        ti = jnp.where(lane == k, idxs[k], ti)
        tg = jnp.where(lane == k, exps[k] / den, tg)
        tp = jnp.where(lane == k, sk, tp)
    ti_ref[...] = ti
    tg_ref[...] = tg
    tp_ref[...] = tp.astype(I32)
    cnt_ref[...] = carry + jnp.sum(chosen, axis=0, keepdims=True)


def _outproj_kernel(yh_ref, ym_ref, x_ref, wa_ref, wb_ref, b_ref, g_ref, be_ref, rw_ref, rb_ref,
                    o_ref, oc_ref, ti_ref, tg_ref, tp_ref, cnt_ref):
    mix = (jnp.dot(yh_ref[...], wa_ref[...], preferred_element_type=F32)
           + jnp.dot(ym_ref[...], wb_ref[...], preferred_element_type=F32) + b_ref[...])
    y = _layer_norm(DN_ALPHA * x_ref[...] + mix, g_ref[...], be_ref[...])
    o_ref[...] = y
    oc_ref[...] = _to_slabs(y)
    _route(y, rw_ref, rb_ref, ti_ref, tg_ref, tp_ref, cnt_ref)


def _out_proj_ln_route(y_hy, y_ml, x2d, w_out_bf, b_out, g, be, rw_pad, rb_pad):
    t = x2d.shape[0]
    tm = OUT_TM
    vec = lambda: pl.BlockSpec((1, D_MODEL), lambda i: (0, 0))
    lane_blk = lambda: pl.BlockSpec((tm, LANES), lambda i: (i, 0))
    return pl.pallas_call(
        _outproj_kernel,
        grid=(t // tm,),
        in_specs=[pl.BlockSpec((tm, D_HY), lambda i: (i, 0)),
                  pl.BlockSpec((tm, D_ML), lambda i: (i, 0)),
                  pl.BlockSpec((tm, D_MODEL), lambda i: (i, 0)),
                  _const_spec((D_HY, D_MODEL), lambda i: (0, 0)),
                  _const_spec((D_ML, D_MODEL), lambda i: (1, 0)),
                  vec(), vec(), vec(),
                  _const_spec((D_MODEL, LANES), lambda i: (0, 0)),
                  pl.BlockSpec((1, LANES), lambda i: (0, 0))],
        out_specs=[pl.BlockSpec((tm, D_MODEL), lambda i: (i, 0)),
                   pl.BlockSpec((tm * ROW_CHUNKS, LANES), lambda i: (i, 0)),
                   lane_blk(), lane_blk(), lane_blk(), pl.BlockSpec((1, LANES), lambda i: (0, 0))],
        out_shape=[jax.ShapeDtypeStruct((t, D_MODEL), F32),
                   jax.ShapeDtypeStruct((t * ROW_CHUNKS, LANES), U32),
                   jax.ShapeDtypeStruct((t, LANES), I32), jax.ShapeDtypeStruct((t, LANES), F32),
                   jax.ShapeDtypeStruct((t, LANES), I32), jax.ShapeDtypeStruct((1, LANES), F32)],
        compiler_params=_cparams(("arbitrary",)),
        name="out_proj_ln1_route")(y_hy, y_ml, x2d, w_out_bf, w_out_bf, b_out, g, be, rw_pad, rb_pad)


def _expert_kernel(te_ref, tr_ref, tb_ref, rt_ref, x_hbm, wg_ref, wu_ref, wd_ref, bg_ref, bu_ref, bd_ref, y_hbm,
                   stage_ref, xb_ref, acc_ref, ring_ref, wgb_ref, wub_ref, wdb_ref, gsem, osem):
    g = pl.program_id(0)
    n_items = pl.num_programs(0) - 1
    nf = D_FF // EXP_TF
    n_tiles = n_items // nf
    item = jnp.maximum(g - 1, 0)
    s = item // nf
    j = item % nf
    rows = tr_ref[s]
    active = jnp.logical_and(g >= 1, rows > 0)
    cast_slot = g % 2
    use_slot = (g + 1) % 2
    slab = EXP_CHUNK * ROW_CHUNKS
    per_step = EXP_ROWS // nf

    def row_copy(tok, r):
        return pltpu.make_async_copy(
            x_hbm.at[pl.ds(pl.multiple_of(tok * ROW_CHUNKS, ROW_CHUNKS), ROW_CHUNKS), :],
            stage_ref.at[pl.ds(pl.multiple_of(r * ROW_CHUNKS, ROW_CHUNKS), ROW_CHUNKS), :], gsem)

    def wait_gather():
        pltpu.make_async_copy(x_hbm.at[pl.ds(0, EXP_ROWS * ROW_CHUNKS), :], stage_ref, gsem).wait()

    def cast_weights():
        wgb_ref[cast_slot] = wg_ref[0].astype(BF16)
        wub_ref[cast_slot] = wu_ref[0].astype(BF16)
        wdb_ref[cast_slot] = wd_ref[0].astype(BF16)

    @pl.when(g == 0)
    def _():
        stage_ref[...] = jnp.zeros_like(stage_ref)
        acc_ref[...] = jnp.zeros_like(acc_ref)
        base = tb_ref[0]

        def group(q, carry):
            for u in range(GATHER_UNROLL):
                r = q * GATHER_UNROLL + u
                row_copy(rt_ref[base + r], r).start()
            return carry

        lax.fori_loop(0, EXP_ROWS // GATHER_UNROLL, group, 0)

        cast_weights()

    @pl.when(active)
    def _():
        @pl.when(j == 0)
        def _():
            wait_gather()
            for i in range(EXP_ROWS // EXP_CHUNK):
                xb_ref[i * EXP_CHUNK:(i + 1) * EXP_CHUNK, :] = _from_slabs(
                    stage_ref[i * slab:(i + 1) * slab, :]).astype(BF16)

        def step_body(m):
            cast_weights()
            nxt_base = tb_ref[jnp.minimum(s + 1, n_tiles - 1)]
            for u in range(per_step):
                r = j * per_step + u
                row_copy(rt_ref[nxt_base + r], r).start()

            xb = xb_ref[0:m, :]
            gate = jnp.dot(xb, wgb_ref[use_slot], preferred_element_type=F32) + bg_ref[0]
            up = jnp.dot(xb, wub_ref[use_slot], preferred_element_type=F32) + bu_ref[0]
            gate = jnp.minimum(gate, SWIGLU_LIMIT)
            up = jnp.clip(up, -SWIGLU_LIMIT, SWIGLU_LIMIT)
            act = (up + 1.0) * (gate * jax.nn.sigmoid(SWIGLU_ALPHA * gate))
            part = jnp.dot(act.astype(BF16), wdb_ref[use_slot], preferred_element_type=F32)
            acc_ref[0:m, :] = jnp.where(j == 0, jnp.broadcast_to(bd_ref[0], part.shape), acc_ref[0:m, :]) + part

        lo = 0
        for m in EXP_M_SIZES:
            @pl.when(jnp.logical_and(rows > lo, rows <= m))
            def _(m=m):
                step_body(m)
            lo = m

        @pl.when(j == nf - 1)
        def _():
            nchunk = (rows + EXP_CHUNK - 1) // EXP_CHUNK

            def chunk_copy(i, slot):
                dst0 = pl.multiple_of((s * EXP_ROWS + i * EXP_CHUNK) * ROW_CHUNKS, slab)
                return pltpu.make_async_copy(ring_ref.at[slot], y_hbm.at[pl.ds(dst0, slab), :], osem.at[slot])

            def emit(i, carry):
                slot = i % 2

                @pl.when(i >= 2)
                def _():
                    chunk_copy(i - 2, slot).wait()

                r0 = pl.multiple_of(i * EXP_CHUNK, EXP_CHUNK)
                ring_ref[slot] = _to_slabs(acc_ref[pl.ds(r0, EXP_CHUNK), :])
                chunk_copy(i, slot).start()
                return carry

            lax.fori_loop(0, nchunk, emit, 0)
            for back in range(2):
                @pl.when(nchunk > back)
                def _():
                    last = nchunk - 1 - back
                    chunk_copy(last, last % 2).wait()

    @pl.when(g == n_items)
    def _():
        wait_gather()


def _experts(tile_e, tile_rows, tile_base, row_tok, used_tiles, x1c, w_gu, b_gu, w_down, b_down):
    n_tiles = tile_e.shape[0]
    nf = D_FF // EXP_TF
    n_items = n_tiles * nf

    def item_block(item, te, tr):
        s = item // nf
        return te[s], jnp.where(tr[s] > 0, item % nf, nf - 1)

    def cast_item(g, te, tr):
        return item_block(jnp.minimum(g, n_items - 1), te, tr)

    def use_item(g, te, tr):
        return item_block(jnp.maximum(g - 1, 0), te, tr)

    def w_gate(g, te, tr, tb, rt):
        e, j = cast_item(g, te, tr)
        return e, 0, j

    def w_up(g, te, tr, tb, rt):
        e, j = cast_item(g, te, tr)
        return e, 0, nf + j

    def w_down_map(g, te, tr, tb, rt):
        e, j = cast_item(g, te, tr)
        return e, j, 0

    def b_gate(g, te, tr, tb, rt):
        e, j = use_item(g, te, tr)
        return e, 0, j

    def b_up(g, te, tr, tb, rt):
        e, j = use_item(g, te, tr)
        return e, 0, nf + j

    def b_down_map(g, te, tr, tb, rt):
        e, _ = use_item(g, te, tr)
        return e, 0, 0

    grid_spec = pltpu.PrefetchScalarGridSpec(
        num_scalar_prefetch=4,
        grid=(used_tiles * nf + 1,),
        in_specs=[pl.BlockSpec(memory_space=pl.ANY),
                  pl.BlockSpec((1, D_MODEL, EXP_TF), w_gate),
                  pl.BlockSpec((1, D_MODEL, EXP_TF), w_up),
                  pl.BlockSpec((1, EXP_TF, D_MODEL), w_down_map),
                  pl.BlockSpec((1, 1, EXP_TF), b_gate),
                  pl.BlockSpec((1, 1, EXP_TF), b_up),
                  pl.BlockSpec((1, 1, D_MODEL), b_down_map)],
        out_specs=pl.BlockSpec(memory_space=pl.ANY),
        scratch_shapes=[pltpu.VMEM((EXP_ROWS * ROW_CHUNKS, LANES), U32),
                        pltpu.VMEM((EXP_ROWS, D_MODEL), BF16),
                        pltpu.VMEM((EXP_ROWS, D_MODEL), F32),
                        pltpu.VMEM((2, EXP_CHUNK * ROW_CHUNKS, LANES), U32),
                        pltpu.VMEM((2, D_MODEL, EXP_TF), BF16),
                        pltpu.VMEM((2, D_MODEL, EXP_TF), BF16),
                        pltpu.VMEM((2, EXP_TF, D_MODEL), BF16),
                        pltpu.SemaphoreType.DMA(()),
                        pltpu.SemaphoreType.DMA((2,))])
    return pl.pallas_call(
        _expert_kernel,
        grid_spec=grid_spec,
        out_shape=jax.ShapeDtypeStruct((n_tiles * EXP_ROWS * ROW_CHUNKS, LANES), U32),
        compiler_params=_cparams(("arbitrary",)),
        name="experts")(tile_e, tile_rows, tile_base, row_tok, x1c, w_gu, w_gu, w_down, b_gu, b_gu, b_down)


def _combine_kernel(dest_ref, y_hbm, x_ref, tg_ref, g_ref, be_ref, o_ref, buf_ref, sem):
    tm = o_ref.shape[0]
    i = pl.program_id(0)
    n = pl.num_programs(0)

    def row_copy(src_row, slot, k, t):
        return pltpu.make_async_copy(
            y_hbm.at[pl.ds(pl.multiple_of(src_row * ROW_CHUNKS, ROW_CHUNKS), ROW_CHUNKS), :],
            buf_ref.at[slot, k, pl.ds(pl.multiple_of(t * ROW_CHUNKS, ROW_CHUNKS), ROW_CHUNKS), :],
            sem.at[slot])

    def start_tile(tile, slot):
        base = tile * tm * TOP_K

        def body(t2, carry):
            for u in range(2):
                t = t2 * 2 + u
                for k in range(TOP_K):
                    row_copy(dest_ref[base + t * TOP_K + k], slot, k, t).start()
            return carry

        lax.fori_loop(0, tm // 2, body, 0)

    def wait_tile(slot):
        for k in range(TOP_K):
            pltpu.make_async_copy(y_hbm.at[pl.ds(0, tm * ROW_CHUNKS), :], buf_ref.at[slot, k], sem.at[slot]).wait()

    @pl.when(i == 0)
    def _():
        start_tile(0, 0)

    @pl.when(i + 1 < n)
    def _():
        start_tile(jnp.minimum(i + 1, n - 1), (i + 1) % 2)

    slot = i % 2
    wait_tile(slot)
    tg = tg_ref[...]
    ff = jnp.zeros((tm, D_MODEL), F32)
    for k in range(TOP_K):
        ff = ff + tg[:, k:k + 1] * _from_slabs(buf_ref[slot, k])
    o_ref[...] = _layer_norm(DN_ALPHA * x_ref[...] + ff, g_ref[...], be_ref[...])


def _combine_ln(dest_flat, y_buf, x1, tg, g, be):
    t = tg.shape[0]
    tm = CMB_TM
    grid_spec = pltpu.PrefetchScalarGridSpec(
        num_scalar_prefetch=1,
        grid=(t // tm,),
        in_specs=[pl.BlockSpec(memory_space=pl.ANY),
                  pl.BlockSpec((tm, D_MODEL), lambda i, d: (i, 0)),
                  pl.BlockSpec((tm, LANES), lambda i, d: (i, 0)),
                  pl.BlockSpec((1, D_MODEL), lambda i, d: (0, 0)),
                  pl.BlockSpec((1, D_MODEL), lambda i, d: (0, 0))],
        out_specs=pl.BlockSpec((tm, D_MODEL), lambda i, d: (i, 0)),
        scratch_shapes=[pltpu.VMEM((2, TOP_K, tm * ROW_CHUNKS, LANES), U32), pltpu.SemaphoreType.DMA((2,))])
    return pl.pallas_call(
        _combine_kernel,
        grid_spec=grid_spec,
        out_shape=jax.ShapeDtypeStruct((t, D_MODEL), F32),
        compiler_params=_cparams(("arbitrary",)),
        name="combine_ln2")(dest_flat, y_buf, x1, tg, g, be)


def _dft_kernel(cd_ref, sd_ref, ca_ref, sa_ref, c_ref, s_ref):
    cd, sd = cd_ref[0], sd_ref[0]
    ca, sa = ca_ref[0], sa_ref[0]
    c_ref[0] = (cd * ca - sd * sa).astype(c_ref.dtype)
    s_ref[0] = (sd * ca + cd * sa).astype(s_ref.dtype)


def _dft_tables(seq):
    n = 2 * seq
    half = seq // 2
    nblk = half // DFT_TB
    idx = jnp.arange(half, dtype=I32)
    off = jnp.arange(DFT_TB, dtype=I32)
    start = jnp.arange(nblk, dtype=I32) * DFT_TB

    def angle(prod):
        return (prod % n).astype(F32) * (2.0 * math.pi / n)

    ang_d = jnp.stack([angle(off[:, None] * (2 * idx)[None, :]),
                       angle(off[:, None] * (2 * idx + 1)[None, :]),
                       angle((2 * off + 1)[:, None] * idx[None, :])])
    ang_a = jnp.stack([angle(start[:, None] * (2 * idx)[None, :]),
                       angle(start[:, None] * (2 * idx + 1)[None, :]),
                       angle((2 * start)[:, None] * idx[None, :])]).reshape(3 * nblk, 1, half)
    small = pl.BlockSpec((1, DFT_TB, half), lambda k, a: (k, 0, 0))
    base = pl.BlockSpec((1, 1, half), lambda k, a: (k * nblk + a, 0, 0))
    out = pl.BlockSpec((1, DFT_TB, half), lambda k, a: (k, a, 0))
    sds = jax.ShapeDtypeStruct((3, half, half), BF16)
    return pl.pallas_call(
        _dft_kernel, grid=(3, nblk), in_specs=[small, small, base, base], out_specs=[out, out],
        out_shape=[sds, sds], compiler_params=_cparams(("arbitrary", "arbitrary")),
        name="dft_tables")(jnp.cos(ang_d), jnp.sin(ang_d), jnp.cos(ang_a), jnp.sin(ang_a))


def _filter_features(seq):
    t = jnp.linspace(0.0, 1.0, seq, dtype=F32)[:, None]
    bands = (HY_EMB - 1) // 2
    fb = jnp.linspace(1e-4, bands - 1, bands, dtype=F32)[None]
    w = 2.0 * math.pi * jnp.arange(seq, dtype=F32)[:, None] / seq
    z = jnp.concatenate([t, jnp.cos(fb * w), -jnp.sin(fb * w)], -1)
    z = jnp.concatenate([z[0::2], z[1::2]], axis=0)
    return jnp.pad(z, ((0, 0), (0, LANES - HY_EMB)))


def _mixer(x, w_in, b_in, hy_conv_w, hy_conv_b, hy_filt_w1, hy_filt_b1, hy_filt_w2, hy_filt_b2,
           hy_filt_w3, hy_filt_freq, hy_skip, hy_norm_w, ml_conv_w, ml_conv_b, ml_norm_w):
    bsz, seq, _ = x.shape
    t = bsz * seq
    x2d = x.reshape(t, D_MODEL)
    n_main = w_in.shape[1] - N_GATE_COLS
    w_t = jnp.swapaxes(w_in, 0, 1)
    wg = jnp.pad(w_t[n_main:], ((0, LANES - N_GATE_COLS), (0, 0)))
    bg = jnp.pad(b_in[None, n_main:], ((0, 0), (0, LANES - N_GATE_COLS)))
    proj, gates = _in_proj(x2d, w_t[:n_main].astype(BF16), b_in[None, :n_main], wg, bg)
    proj3 = proj.reshape(bsz, seq, n_main)
    gates = gates[:, :N_GATE_COLS]
    g5 = gates.reshape(bsz, seq, 4, ML_HEADS)
    grow = g5.transpose(0, 3, 2, 1).reshape(bsz, ML_HEADS, 4, seq // CHUNK, CHUNK)
    gtr = grow.transpose(0, 1, 2, 4, 3)

    cmat, smat = _dft_tables(seq)
    zpad = _filter_features(seq)
    w1pad = jnp.pad(hy_filt_w1, ((0, LANES - HY_EMB), (0, 0)))
    deltas = jnp.abs(jnp.linspace(math.log(HY_DECAY_TARGET) / HY_SLOW_PCT,
                                  math.log(HY_DECAY_TARGET) / HY_FAST_PCT, D_HY, dtype=F32))[None]
    kr, ki, km = _hyena_filters(zpad, w1pad, hy_filt_b1[None], hy_filt_w2, hy_filt_b2[None],
                                hy_filt_freq, hy_filt_w3, deltas, cmat, smat)
    y_hy = _hyena(proj3, hy_conv_w, hy_conv_b[None], cmat, smat, kr, ki, km, hy_skip, hy_norm_w[None])
    y_ml = _mlstm(proj3, ml_conv_w, ml_conv_b[None], grow, gtr, ml_norm_w[None])
    return y_hy.reshape(t, D_HY), y_ml.reshape(t, D_ML), x2d


def _moe_tables(top_i, slot, counts):
    t = top_i.shape[0]
    n_tiles = N_EXPERTS + (t * TOP_K) // EXP_ROWS
    ntile = (counts + EXP_ROWS - 1) // EXP_ROWS
    ends = jnp.cumsum(ntile)
    starts = ends - ntile
    total = ends[-1]
    s_idx = jnp.arange(n_tiles, dtype=I32)
    valid = s_idx < total
    s_eff = jnp.where(valid, s_idx, jnp.maximum(total - 1, 0))
    tile_e = jnp.minimum(jnp.sum((s_eff[:, None] >= ends[None, :]).astype(I32), axis=1), N_EXPERTS - 1)
    local = s_eff - starts[tile_e]
    tile_rows = jnp.where(valid, jnp.clip(counts[tile_e] - local * EXP_ROWS, 0, EXP_ROWS), 0).astype(I32)
    tok = jnp.arange(t, dtype=I32)[:, None]
    row_tok = jnp.pad(jnp.sort((top_i * t + tok).reshape(-1)) % t, (0, EXP_ROWS))
    first = jnp.cumsum(counts) - counts
    tile_base = (first[tile_e] + local * EXP_ROWS).astype(I32)
    onehot = top_i[:, :, None] == jnp.arange(N_EXPERTS, dtype=I32)
    dest = jnp.sum(jnp.where(onehot, starts * EXP_ROWS, 0), axis=-1) + slot
    return (tile_e.astype(I32), tile_rows, tile_base, row_tok.astype(I32), total.astype(I32),
            dest.astype(I32).reshape(-1))


def kernel(x, w_in, b_in, hy_conv_w, hy_conv_b, hy_filt_w1, hy_filt_b1, hy_filt_w2, hy_filt_b2, hy_filt_w3, hy_filt_freq, hy_skip, hy_norm_w, ml_conv_w, ml_conv_b, ml_norm_w, w_out, b_out, ln1_g, ln1_b, router_w, router_b, w_gu, b_gu, w_down, b_down, ln2_g, ln2_b):
    bsz, seq, _ = x.shape
    l = 0
    y_hy, y_ml, x2d = _mixer(x, w_in[l], b_in[l], hy_conv_w[l], hy_conv_b[l], hy_filt_w1[l], hy_filt_b1[l],
                             hy_filt_w2[l], hy_filt_b2[l], hy_filt_w3[l], hy_filt_freq[l], hy_skip[l],
                             hy_norm_w[l], ml_conv_w[l], ml_conv_b[l], ml_norm_w[l])
    rw = jnp.pad(router_w[l], ((0, 0), (0, LANES - N_EXPERTS)))
    rb = jnp.pad(router_b[l][None], ((0, 0), (0, LANES - N_EXPERTS)), constant_values=-1e30)
    x1, x1c, top_i, top_g, slot, cnt = _out_proj_ln_route(
        y_hy, y_ml, x2d, w_out[l].astype(BF16), b_out[l][None], ln1_g[l][None], ln1_b[l][None], rw, rb)
    counts = cnt[0, :N_EXPERTS].astype(I32)
    tile_e, tile_rows, tile_base, row_tok, used, dest = _moe_tables(top_i[:, :TOP_K], slot[:, :TOP_K], counts)
    y_buf = _experts(tile_e, tile_rows, tile_base, row_tok, used, x1c, w_gu[l], b_gu[l][:, None, :], w_down[l],
                     b_down[l][:, None, :])
    out = _combine_ln(dest, y_buf, x1, top_g, ln2_g[l][None], ln2_b[l][None])
    return out.reshape(bsz, seq, D_MODEL)
```

```python
import functools
import math

import jax
import jax.numpy as jnp
from jax import lax
from jax.experimental import pallas as pl
from jax.experimental.pallas import tpu as pltpu

F32 = jnp.float32
BF16 = jnp.bfloat16
I32 = jnp.int32
U32 = jnp.uint32
HP = lax.Precision.HIGHEST

D_MODEL = 2048
D_HY = 1024
D_ML = 1024
ML_HEADS = 8
HEAD_DIM = 128
CHUNK = 128
N_GATE_COLS = 32
HY_EMB = 33
N_EXPERTS = 32
TOP_K = 4
D_FF = 2048
SWIGLU_LIMIT = 7.0
SWIGLU_ALPHA = 1.702
LN_EPS = 1e-5
DN_ALPHA = 2.0 ** 0.25
HY_DECAY_TARGET = 1e-2
HY_FAST_PCT = 0.3
HY_SLOW_PCT = 1.5

LANES = 128
ROW_CHUNKS = D_MODEL // (2 * LANES)
VMEM_LIMIT = 60 * 1024 * 1024

PROJ_TM = 2048
PROJ_TN = 512
PROJ_GATE_ROWS = 512
HY_CW = 256
HY_FC = 512
OUT_TM = 512
EXP_ROWS = 1152
EXP_M_SIZES = (1024, 1088, 1152)
EXP_CHUNK = 128
EXP_TF = 256
GATHER_UNROLL = 8
CMB_TM = 128
DFT_TB = 128


def _cparams(sem):
    return pltpu.CompilerParams(dimension_semantics=sem, vmem_limit_bytes=VMEM_LIMIT)


def _split(a):
    hi = a.astype(BF16)
    return hi, (a - hi.astype(F32)).astype(BF16)


def _dot3(a, b, dims):
    a_hi, a_lo = _split(a)
    b_hi, b_lo = _split(b)
    mm = functools.partial(lax.dot_general, dimension_numbers=dims, preferred_element_type=F32)
    return mm(a_hi, b_hi) + (mm(a_hi, b_lo) + mm(a_lo, b_hi))


NN = (((1,), (0,)), ((), ()))
NT = (((1,), (1,)), ((), ()))


def _const_spec(shape, index_map):
    return pl.BlockSpec(shape, index_map, pipeline_mode=pl.Buffered(1))


def _proj_kernel(x_ref, w_ref, b_ref, wg_ref, bg_ref, o_ref, og_ref, xb_ref):
    @pl.when(pl.program_id(1) == 0)
    def _():
        xb_ref[...] = x_ref[...].astype(BF16)
        wg_hi, wg_lo = _split(wg_ref[...])
        mm = functools.partial(lax.dot_general, dimension_numbers=NT, preferred_element_type=F32)
        for r in range(x_ref.shape[0] // PROJ_GATE_ROWS):
            rows = slice(r * PROJ_GATE_ROWS, (r + 1) * PROJ_GATE_ROWS)
            x_hi = xb_ref[rows, :]
            x_lo = (x_ref[rows, :] - x_hi.astype(F32)).astype(BF16)
            og_ref[rows, :] = mm(x_hi, wg_hi) + (mm(x_hi, wg_lo) + mm(x_lo, wg_hi)) + bg_ref[...]

    acc = lax.dot_general(xb_ref[...], w_ref[...], NT, preferred_element_type=F32)
    o_ref[...] = (acc + b_ref[...]).astype(o_ref.dtype)


def _in_proj(x2d, wt_bf, b_row, wt_gate, b_gate):
    m, k = x2d.shape
    n = wt_bf.shape[0]
    return pl.pallas_call(
        _proj_kernel,
        grid=(m // PROJ_TM, n // PROJ_TN),
        in_specs=[pl.BlockSpec((PROJ_TM, k), lambda i, j: (i, 0)),
                  pl.BlockSpec((PROJ_TN, k), lambda i, j: (j, 0)),
                  pl.BlockSpec((1, PROJ_TN), lambda i, j: (0, j)),
                  _const_spec((LANES, k), lambda i, j: (0, 0)),
                  _const_spec((1, LANES), lambda i, j: (0, 0))],
        out_specs=[pl.BlockSpec((PROJ_TM, PROJ_TN), lambda i, j: (i, j)),
                   pl.BlockSpec((PROJ_TM, LANES), lambda i, j: (i, 0))],
        out_shape=[jax.ShapeDtypeStruct((m, n), BF16), jax.ShapeDtypeStruct((m, LANES), F32)],
        scratch_shapes=[pltpu.VMEM((PROJ_TM, k), BF16)],
        compiler_params=_cparams(("arbitrary", "arbitrary")),
        name="in_proj")(x2d, wt_bf, b_row, wt_gate, b_gate)


def _filter_kernel(z_ref, w1_ref, b1_ref, w2_ref, b2_ref, fq_ref, w3f_ref, w3b_ref, dl_ref,
                   c_ref, s_ref, kr_ref, ki_ref, km_ref, h_ref):
    seq = z_ref.shape[0]
    inv_n = 1.0 / (2 * seq)
    z = z_ref[...]

    @pl.when(jnp.logical_and(pl.program_id(0) == 0, pl.program_id(1) == 0))
    def _():
        h1 = jnp.sin(fq_ref[0:1, :] * (jnp.dot(z, w1_ref[...], precision=HP, preferred_element_type=F32)
                                       + b1_ref[...]))
        h_ref[...] = jnp.sin(fq_ref[1:2, :] * (jnp.dot(h1, w2_ref[...], precision=HP, preferred_element_type=F32)
                                               + b2_ref[...]))

    half = seq // 2
    h = h_ref[...]
    win = jnp.exp(-z[:, 0:1] * dl_ref[...])
    fwd = _dot3(h, w3f_ref[...], NN) * win
    bwd = _dot3(h, w3b_ref[...], NN) * win
    row = lax.broadcasted_iota(I32, fwd.shape, 0)
    bwd = jnp.where(row == 0, 0.0, bwd)
    inv = 1.0 / jnp.sum(jnp.abs(fwd) + jnp.abs(bwd), axis=0, keepdims=True)
    ks = ((fwd + bwd) * inv)
    kd = ((fwd - bwd) * inv)
    ksb = ks.astype(BF16)
    kdb = kd.astype(BF16)
    ec = jnp.dot(c_ref[0], ksb[:half], preferred_element_type=F32)
    oc = jnp.dot(c_ref[1], ksb[half:], preferred_element_type=F32)
    es = jnp.dot(s_ref[0], kdb[:half], preferred_element_type=F32)
    os_ = jnp.dot(s_ref[1], kdb[half:], preferred_element_type=F32)
    rowh = lax.broadcasted_iota(I32, ec.shape, 0)
    wf = jnp.where(rowh == 0, inv_n, 2.0 * inv_n)
    kr_ref[0, :half, :] = (ec + oc) * wf
    kr_ref[0, half:, :] = (ec - oc) * wf
    ki_ref[0, :half, :] = -(es + os_) * wf
    ki_ref[0, half:, :] = (es - os_) * wf
    alt = jnp.where((rowh & 1) == 0, 1.0, -1.0)
    km_ref[0, 0:1, :] = jnp.sum(ks[:half] * alt, axis=0, keepdims=True) * (2.0 * inv_n)
    km_ref[0, 1:2, :] = -jnp.sum(kd[half:] * alt, axis=0, keepdims=True) * (2.0 * inv_n)


def _hyena_filters(zpad, w1pad, b1, w2, b2, freq, w3, deltas, cmat, smat):
    seq = zpad.shape[0]
    nb = D_HY // HY_CW
    hid = w2.shape[0]
    full = lambda shape: pl.BlockSpec(shape, lambda o, c: (0,) * len(shape))
    out_sds = jax.ShapeDtypeStruct((2, seq, D_HY), F32)
    return pl.pallas_call(
        _filter_kernel,
        grid=(2, nb),
        in_specs=[full(zpad.shape), full(w1pad.shape), full(b1.shape), full(w2.shape), full(b2.shape),
                  full(freq.shape),
                  pl.BlockSpec((hid, HY_CW), lambda o, c: (0, o * 2 * nb + c)),
                  pl.BlockSpec((hid, HY_CW), lambda o, c: (0, o * 2 * nb + nb + c)),
                  pl.BlockSpec((1, HY_CW), lambda o, c: (0, c)),
                  _const_spec(cmat.shape, lambda o, c: (0, 0, 0)),
                  _const_spec(smat.shape, lambda o, c: (0, 0, 0))],
        out_specs=[pl.BlockSpec((1, seq, HY_CW), lambda o, c: (o, 0, c)),
                   pl.BlockSpec((1, seq, HY_CW), lambda o, c: (o, 0, c)),
                   pl.BlockSpec((1, 2, HY_CW), lambda o, c: (o, 0, c))],
        out_shape=[out_sds, out_sds, jax.ShapeDtypeStruct((2, 2, D_HY), F32)],
        scratch_shapes=[pltpu.VMEM((seq, hid), F32)],
        compiler_params=_cparams(("arbitrary", "arbitrary")),
        name="hyena_filters")(zpad, w1pad, b1, w2, b2, freq, w3, w3, deltas, cmat, smat)


def _short_conv(u, w_ref, b_ref, row, seq):
    prev = jnp.where(row == 0, 0.0, pltpu.roll(u, 1, 0))
    nxt = jnp.where(row == seq - 1, 0.0, pltpu.roll(u, seq - 1, 0))
    return w_ref[0:1, :] * prev + w_ref[1:2, :] * u + w_ref[2:3, :] * nxt + b_ref[...]


def _hyena_kernel(uv_ref, u1_ref, u2_ref, wv_ref, w1_ref, w2_ref, bv_ref, b1_ref, b2_ref,
                  c_ref, s_ref, kr_ref, ki_ref, km_ref, skip_ref, nw_ref, o_ref,
                  a_ref, b_ref, t_ref, zb_ref, s1_ref, s2_ref, s3_ref, s4_ref, md_ref):
    seq = uv_ref.shape[1]
    cw = uv_ref.shape[2]
    half = seq // 2
    nblk = half // HY_FC
    row = lax.broadcasted_iota(I32, (seq, LANES), 0)
    alt_half = jnp.where((lax.broadcasted_iota(I32, (half, cw), 0) & 1) == 0, 1.0, -1.0)
    alt_blk = jnp.where((lax.broadcasted_iota(I32, (HY_FC, cw), 0) & 1) == 0, 1.0, -1.0)
    groups = [slice(g * LANES, (g + 1) * LANES) for g in range(cw // LANES)]

    def conv_to(dst_ref, u_ref, w_ref, bias_ref):
        for gi, gs in enumerate(groups):
            t_ref[gi] = _short_conv(u_ref[0, :, gs].astype(F32), w_ref.at[:, gs], bias_ref.at[:, gs], row, seq)
            dst_ref[0:half, gs] = t_ref[gi, pl.ds(0, half, stride=2), :]
            dst_ref[half:seq, gs] = t_ref[gi, pl.ds(1, half, stride=2), :]

    def spectrum(zin_ref, o):
        z = zin_ref[...]
        zb_ref[...] = z.astype(BF16)
        am = jnp.sum(z[:half] * alt_half, axis=0, keepdims=True)
        bm = jnp.sum(z[half:] * alt_half, axis=0, keepdims=True)
        krm, kim = km_ref[o, 0:1, :], km_ref[o, 1:2, :]
        md_ref[0:1, :] = am * krm + bm * kim
        md_ref[1:2, :] = am * kim - bm * krm
        ze = zb_ref[0:half, :]
        zo = zb_ref[half:seq, :]
        for gb in range(nblk):
            lo = slice(gb * HY_FC, (gb + 1) * HY_FC)
            hi = slice(half + gb * HY_FC, half + (gb + 1) * HY_FC)
            ec = jnp.dot(c_ref[0, lo, :], ze, preferred_element_type=F32)
            oc = jnp.dot(c_ref[1, lo, :], zo, preferred_element_type=F32)
            es = jnp.dot(s_ref[0, lo, :], ze, preferred_element_type=F32)
            os_ = jnp.dot(s_ref[1, lo, :], zo, preferred_element_type=F32)
            a_lo, b_lo = ec + oc, es + os_
            a_hi, b_hi = ec - oc, os_ - es
            krl, kil = kr_ref[o, lo, :], ki_ref[o, lo, :]
            krh, kih = kr_ref[o, hi, :], ki_ref[o, hi, :]
            pr = a_lo * krl + b_lo * kil
            pi = a_lo * kil - b_lo * krl
            qr = a_hi * krh + b_hi * kih
            qi = a_hi * kih - b_hi * krh
            s1_ref[lo, :] = (pr + qr).astype(BF16)
            s2_ref[lo, :] = (pi - qi).astype(BF16)
            s3_ref[lo, :] = (pr - qr).astype(BF16)
            s4_ref[lo, :] = (pi + qi).astype(BF16)

    def conv_rows(ub, parity):
        us = slice(ub * HY_FC, (ub + 1) * HY_FC)
        if parity == 0:
            y = jnp.dot(c_ref[0, us, :], s1_ref[...], preferred_element_type=F32)
            y = y - jnp.dot(s_ref[0, us, :], s2_ref[...], preferred_element_type=F32)
            y = y + md_ref[0:1, :] * alt_blk
        else:
            y = jnp.dot(c_ref[2, us, :], s3_ref[...], preferred_element_type=F32)
            y = y - jnp.dot(s_ref[2, us, :], s4_ref[...], preferred_element_type=F32)
            y = y - md_ref[1:2, :] * alt_blk
        return slice(parity * half + ub * HY_FC, parity * half + (ub + 1) * HY_FC), y

    blocks = [(ub, parity) for parity in range(2) for ub in range(nblk)]
    conv_to(a_ref, uv_ref, wv_ref, bv_ref)
    conv_to(b_ref, u1_ref, w1_ref, b1_ref)
    spectrum(a_ref, 0)
    for ub, parity in blocks:
        rows, y = conv_rows(ub, parity)
        b_ref[rows, :] = b_ref[rows, :] * (y + skip_ref[0:1, :] * a_ref[rows, :])
    conv_to(a_ref, u2_ref, w2_ref, b2_ref)
    spectrum(b_ref, 1)
    for ub, parity in blocks:
        rows, y = conv_rows(ub, parity)
        z = a_ref[rows, :] * (y + skip_ref[1:2, :] * b_ref[rows, :])
        for gi, gs in enumerate(groups):
            zg = z[:, gs]
            mu = jnp.mean(zg, axis=-1, keepdims=True)
            zc = zg - mu
            var = jnp.mean(zc * zc, axis=-1, keepdims=True)
            t_ref[gi, pl.ds(2 * ub * HY_FC + parity, HY_FC, stride=2), :] = (
                zc * lax.rsqrt(var + LN_EPS) * nw_ref[:, gs])
    for gi, gs in enumerate(groups):
        o_ref[0, :, gs] = t_ref[gi].astype(o_ref.dtype)


def _hyena(proj3, conv_w, conv_b, cmat, smat, kr, ki, km, skip, norm_w):
    bsz, seq, _ = proj3.shape
    half = seq // 2
    nb = D_HY // HY_CW
    u_spec = lambda off: pl.BlockSpec((1, seq, HY_CW), lambda c, b: (b, 0, off + c))
    w_spec = lambda off: pl.BlockSpec((3, HY_CW), lambda c, b: (0, off + c))
    b_spec = lambda off: pl.BlockSpec((1, HY_CW), lambda c, b: (0, off + c))
    return pl.pallas_call(
        _hyena_kernel,
        grid=(nb, bsz),
        in_specs=[u_spec(0), u_spec(nb), u_spec(2 * nb),
                  w_spec(0), w_spec(nb), w_spec(2 * nb),
                  b_spec(0), b_spec(nb), b_spec(2 * nb),
                  _const_spec(cmat.shape, lambda c, b: (0, 0, 0)),
                  _const_spec(smat.shape, lambda c, b: (0, 0, 0)),
                  _const_spec((2, seq, HY_CW), lambda c, b: (0, 0, c)),
                  _const_spec((2, seq, HY_CW), lambda c, b: (0, 0, c)),
                  pl.BlockSpec((2, 2, HY_CW), lambda c, b: (0, 0, c)),
                  pl.BlockSpec((2, HY_CW), lambda c, b: (0, c)),
                  pl.BlockSpec((1, HY_CW), lambda c, b: (0, c))],
        out_specs=pl.BlockSpec((1, seq, HY_CW), lambda c, b: (b, 0, c)),
        out_shape=jax.ShapeDtypeStruct((bsz, seq, D_HY), BF16),
        scratch_shapes=[pltpu.VMEM((seq, HY_CW), F32), pltpu.VMEM((seq, HY_CW), F32),
                        pltpu.VMEM((HY_CW // LANES, seq, LANES), F32), pltpu.VMEM((seq, HY_CW), BF16),
                        pltpu.VMEM((half, HY_CW), BF16), pltpu.VMEM((half, HY_CW), BF16),
                        pltpu.VMEM((half, HY_CW), BF16), pltpu.VMEM((half, HY_CW), BF16),
                        pltpu.VMEM((2, HY_CW), F32)],
        compiler_params=_cparams(("arbitrary", "arbitrary")),
        name="hyena")(proj3, proj3, proj3, conv_w, conv_w, conv_w, conv_b, conv_b, conv_b,
                      cmat, smat, kr, ki, km, skip, norm_w)


def _mlstm_kernel(qp_ref, kp_ref, v_ref, og_ref, wq_ref, wk_ref, bq_ref, bk_ref, gr_ref, gt_ref,
                  nw_ref, o_ref, qb_ref, kb_ref, hacc_ref, cst_ref, nst_ref):
    seq = qp_ref.shape[1]
    d = qp_ref.shape[2]
    nchunk = seq // CHUNK
    row = lax.broadcasted_iota(I32, (seq, d), 0)
    q = _short_conv(qp_ref[0].astype(F32), wq_ref, bq_ref, row, seq)
    k = _short_conv(kp_ref[0].astype(F32), wk_ref, bk_ref, row, seq)
    qb_ref[...] = (q * jax.nn.sigmoid(q)).astype(BF16)
    kb_ref[...] = ((k * jax.nn.sigmoid(k)) * (d ** -0.5)).astype(BF16)

    ti = lax.broadcasted_iota(I32, (CHUNK, CHUNK), 0)
    si = lax.broadcasted_iota(I32, (CHUNK, CHUNK), 1)
    lower = ti >= si
    upper = ti <= si
    lower_f = lower.astype(F32)
    upper_f = upper.astype(F32)
    nt = (((1,), (1,)), ((), ()))
    tn = (((0,), (0,)), ((), ()))
    chunk_rows = [slice(c * CHUNK, (c + 1) * CHUNK) for c in range(nchunk)]

    per_dir = []
    for direction in range(2):
        f_idx, i_idx = 2 * direction + 1, 2 * direction
        mask = lower if direction == 0 else upper
        order = list(range(nchunk)) if direction == 0 else list(range(nchunk - 1, -1, -1))
        lf_r = jax.nn.log_sigmoid(gr_ref[0, 0, f_idx])
        b_r = jnp.dot(lf_r, upper_f if direction == 0 else lower_f, precision=HP, preferred_element_type=F32)
        rterm = b_r - gr_ref[0, 0, i_idx]
        b_last = jnp.sum(lf_r, axis=-1, keepdims=True)
        lf_c = jax.nn.log_sigmoid(gt_ref[0, 0, f_idx])
        b_c = jnp.dot(lower_f if direction == 0 else upper_f, lf_c, precision=HP, preferred_element_type=F32)
        i_c = gt_ref[0, 0, i_idx]

        bcol, gcol, gmax, rowmax, blast = [], [], [], [], []
        for c in range(nchunk):
            bc = jnp.broadcast_to(b_c[:, c:c + 1], (CHUNK, CHUNK))
            ic = jnp.broadcast_to(i_c[:, c:c + 1], (CHUNK, CHUNK))
            bl = jnp.broadcast_to(b_last[c:c + 1, :], (1, CHUNK))
            dmat = jnp.where(mask, bc - rterm[c:c + 1, :], -jnp.inf)
            g = bl - bc + ic
            bcol.append(bc)
            gcol.append(g)
            blast.append(bl)
            rowmax.append(jnp.max(dmat, axis=-1, keepdims=True))
            gmax.append(jnp.max(g, axis=0, keepdims=True))

        m = jnp.zeros((1, CHUNK), F32)
        m_in, m_out = [None] * nchunk, [None] * nchunk
        for c in order:
            m_in[c] = m
            m = jnp.maximum(blast[c] + m, gmax[c])
            m_out[c] = m

        cmat = jnp.zeros((d, d), F32)
        nvec = jnp.zeros((1, d), F32)
        for c in order:
            kc = kb_ref[chunk_rows[c], :]
            vc = v_ref[0, chunk_rows[c], :]
            cst_ref[direction, c] = cmat.astype(BF16)
            nst_ref[direction, c] = nvec
            wg = jnp.exp(gcol[c] - m_out[c])
            decay = jnp.exp(blast[c] + m_in[c] - m_out[c])
            upd = lax.dot_general((wg * vc.astype(F32)).astype(BF16), kc, tn, preferred_element_type=F32)
            cmat = decay * cmat + upd
            nvec = decay * nvec + jnp.sum(wg * kc.astype(F32), axis=0, keepdims=True)
        per_dir.append((mask, bcol, rterm, rowmax, m_in))

    for c in range(nchunk):
        rs = chunk_rows[c]
        qc = qb_ref[rs, :]
        kc = kb_ref[rs, :]
        vc = v_ref[0, rs, :]
        qk = lax.dot_general(qc, kc, nt, preferred_element_type=F32)
        h_sum = None
        for direction, (mask, bcol, rterm, rowmax, m_in) in enumerate(per_dir):
            dmat = jnp.where(mask, bcol[c] - rterm[c:c + 1, :], -jnp.inf)
            inter = bcol[c] + m_in[c]
            m_t = jnp.maximum(inter, rowmax[c])
            s = qk * jnp.exp(dmat - m_t)
            inter_w = jnp.exp(inter - m_t)
            cq = lax.dot_general(qc, cst_ref[direction, c], nt, preferred_element_type=F32)
            num = jnp.dot(s.astype(BF16), vc, preferred_element_type=F32) + inter_w * cq
            nq = jnp.sum(qc.astype(F32) * nst_ref[direction, c], axis=-1, keepdims=True)
            den = jnp.sum(s, axis=-1, keepdims=True) + inter_w * nq
            h = num / jnp.maximum(jnp.abs(den), jnp.exp(-m_t))
            h_sum = h if h_sum is None else h_sum + h
        hacc_ref[rs, :] = h_sum

    h = hacc_ref[...]
    mu = jnp.mean(h, axis=-1, keepdims=True)
    hc = h - mu
    var = jnp.mean(hc * hc, axis=-1, keepdims=True)
    y = hc * lax.rsqrt(var + LN_EPS) * nw_ref[...] * jax.nn.sigmoid(og_ref[0].astype(F32))
    o_ref[0] = y.astype(o_ref.dtype)


def _mlstm(proj3, conv_w, conv_b, grow, gtr, norm_w):
    bsz, seq, _ = proj3.shape
    d = HEAD_DIM
    nchunk = seq // CHUNK
    hy_blocks = 3 * D_HY // d
    qoff, koff, voff, ooff = hy_blocks, hy_blocks + ML_HEADS, hy_blocks + 2 * ML_HEADS, hy_blocks + 3 * ML_HEADS
    p_spec = lambda off: pl.BlockSpec((1, seq, d), lambda b, h: (b, 0, off + h))
    return pl.pallas_call(
        _mlstm_kernel,
        grid=(bsz, ML_HEADS),
        in_specs=[p_spec(qoff), p_spec(koff), p_spec(voff), p_spec(ooff),
                  pl.BlockSpec((3, d), lambda b, h: (0, h)),
                  pl.BlockSpec((3, d), lambda b, h: (0, ML_HEADS + h)),
                  pl.BlockSpec((1, d), lambda b, h: (0, h)),
                  pl.BlockSpec((1, d), lambda b, h: (0, ML_HEADS + h)),
                  pl.BlockSpec((1, 1, 4, nchunk, CHUNK), lambda b, h: (b, h, 0, 0, 0)),
                  pl.BlockSpec((1, 1, 4, CHUNK, nchunk), lambda b, h: (b, h, 0, 0, 0)),
                  pl.BlockSpec((1, d), lambda b, h: (0, h))],
        out_specs=pl.BlockSpec((1, seq, d), lambda b, h: (b, 0, h)),
        out_shape=jax.ShapeDtypeStruct((bsz, seq, D_ML), BF16),
        scratch_shapes=[pltpu.VMEM((seq, d), BF16), pltpu.VMEM((seq, d), BF16),
                        pltpu.VMEM((seq, d), F32),
                        pltpu.VMEM((2, nchunk, d, d), BF16), pltpu.VMEM((2, nchunk, 1, d), F32)],
        compiler_params=_cparams(("arbitrary", "arbitrary")),
        name="mlstm")(proj3, proj3, proj3, proj3, conv_w, conv_w, conv_b, conv_b, grow, gtr, norm_w)


def _to_slabs(y):
    n = y.shape[0]
    half = D_MODEL // 2
    lo = lax.bitcast_convert_type(y[:, :half].astype(BF16).astype(F32), U32) >> 16
    hi = lax.bitcast_convert_type(y[:, half:].astype(BF16).astype(F32), U32) & jnp.uint32(0xFFFF0000)
    words = hi | lo
    parts = jnp.stack([words[:, c * LANES:(c + 1) * LANES] for c in range(ROW_CHUNKS)], axis=0)
    return pltpu.einshape("crl->rcl", parts).reshape(n * ROW_CHUNKS, LANES)


def _from_slabs(v):
    n = v.shape[0] // ROW_CHUNKS
    parts = pltpu.einshape("rcl->crl", v.reshape(n, ROW_CHUNKS, LANES))
    lo = [lax.bitcast_convert_type(parts[c] << 16, F32) for c in range(ROW_CHUNKS)]
    hi = [lax.bitcast_convert_type(parts[c] & jnp.uint32(0xFFFF0000), F32) for c in range(ROW_CHUNKS)]
    return jnp.concatenate(lo + hi, axis=-1)


def _layer_norm(u, g, b):
    mu = jnp.mean(u, axis=-1, keepdims=True)
    uc = u - mu
    var = jnp.mean(uc * uc, axis=-1, keepdims=True)
    return uc * lax.rsqrt(var + LN_EPS) * g + b


def _route(x, w_ref, b_ref, ti_ref, tg_ref, tp_ref, cnt_ref):
    tm = ti_ref.shape[0]

    @pl.when(pl.program_id(0) == 0)
    def _():
        cnt_ref[...] = jnp.zeros_like(cnt_ref)

    logits = _dot3(x, w_ref[...], NN) + b_ref[...]
    lane = lax.broadcasted_iota(I32, (tm, LANES), 1)
    work = logits
    vals, idxs = [], []
    chosen = jnp.zeros((tm, LANES), F32)
    for _ in range(TOP_K):
        mx = jnp.max(work, axis=-1, keepdims=True)
        idx = jnp.min(jnp.where(work == mx, lane, LANES), axis=-1, keepdims=True)
        hit = lane == idx
        vals.append(mx)
        idxs.append(idx)
        chosen = jnp.where(hit, 1.0, chosen)
        work = jnp.where(hit, -jnp.inf, work)
    exps = [jnp.exp(v - vals[0]) for v in vals]
    den = exps[0] + exps[1] + exps[2] + exps[3]
    ri = lax.broadcasted_iota(I32, (tm, tm), 0)
    ci = lax.broadcasted_iota(I32, (tm, tm), 1)
    strict_lower = (ri > ci).astype(BF16)
    carry = cnt_ref[...]
    slot = carry + jnp.dot(strict_lower, chosen.astype(BF16), preferred_element_type=F32)
    ti = jnp.zeros((tm, LANES), I32)
    tg = jnp.zeros((tm, LANES), F32)
    tp = jnp.zeros((tm, LANES), F32)
    for k in range(TOP_K):
        sk = jnp.sum(jnp.where(lane == idxs[k], slot, 0.0), axis=-1, keepdims=True)
        ti = jnp.where(lane == k, idxs[k], ti)
        tg = jnp.where(lane == k, exps[k] / den, tg)
        tp = jnp.where(lane == k, sk, tp)
    ti_ref[...] = ti
    tg_ref[...] = tg
    tp_ref[...] = tp.astype(I32)
    cnt_ref[...] = carry + jnp.sum(chosen, axis=0, keepdims=True)


def _outproj_kernel(yh_ref, ym_ref, x_ref, wa_ref, wb_ref, b_ref, g_ref, be_ref, rw_ref, rb_ref,
                    o_ref, oc_ref, ti_ref, tg_ref, tp_ref, cnt_ref):
    mix = (jnp.dot(yh_ref[...], wa_ref[...], preferred_element_type=F32)
           + jnp.dot(ym_ref[...], wb_ref[...], preferred_element_type=F32) + b_ref[...])
    y = _layer_norm(DN_ALPHA * x_ref[...] + mix, g_ref[...], be_ref[...])
    o_ref[...] = y
    oc_ref[...] = _to_slabs(y)
    _route(y, rw_ref, rb_ref, ti_ref, tg_ref, tp_ref, cnt_ref)


def _out_proj_ln_route(y_hy, y_ml, x2d, w_out_bf, b_out, g, be, rw_pad, rb_pad):
    t = x2d.shape[0]
    tm = OUT_TM
    vec = lambda: pl.BlockSpec((1, D_MODEL), lambda i: (0, 0))
    lane_blk = lambda: pl.BlockSpec((tm, LANES), lambda i: (i, 0))
    return pl.pallas_call(
        _outproj_kernel,
        grid=(t // tm,),
        in_specs=[pl.BlockSpec((tm, D_HY), lambda i: (i, 0)),
                  pl.BlockSpec((tm, D_ML), lambda i: (i, 0)),
                  pl.BlockSpec((tm, D_MODEL), lambda i: (i, 0)),
                  _const_spec((D_HY, D_MODEL), lambda i: (0, 0)),
                  _const_spec((D_ML, D_MODEL), lambda i: (1, 0)),
                  vec(), vec(), vec(),
                  _const_spec((D_MODEL, LANES), lambda i: (0, 0)),
                  pl.BlockSpec((1, LANES), lambda i: (0, 0))],
        out_specs=[pl.BlockSpec((tm, D_MODEL), lambda i: (i, 0)),
                   pl.BlockSpec((tm * ROW_CHUNKS, LANES), lambda i: (i, 0)),
                   lane_blk(), lane_blk(), lane_blk(), pl.BlockSpec((1, LANES), lambda i: (0, 0))],
        out_shape=[jax.ShapeDtypeStruct((t, D_MODEL), F32),
                   jax.ShapeDtypeStruct((t * ROW_CHUNKS, LANES), U32),
                   jax.ShapeDtypeStruct((t, LANES), I32), jax.ShapeDtypeStruct((t, LANES), F32),
                   jax.ShapeDtypeStruct((t, LANES), I32), jax.ShapeDtypeStruct((1, LANES), F32)],
        compiler_params=_cparams(("arbitrary",)),
        name="out_proj_ln1_route")(y_hy, y_ml, x2d, w_out_bf, w_out_bf, b_out, g, be, rw_pad, rb_pad)


def _expert_kernel(te_ref, tr_ref, tb_ref, rt_ref, x_hbm, wg_ref, wu_ref, wd_ref, bg_ref, bu_ref, bd_ref, y_hbm,
                   stage_ref, xb_ref, acc_ref, ring_ref, wgb_ref, wub_ref, wdb_ref, gsem, osem):
    g = pl.program_id(0)
    n_items = pl.num_programs(0) - 1
    nf = D_FF // EXP_TF
    n_tiles = n_items // nf
    item = jnp.maximum(g - 1, 0)
    s = item // nf
    j = item % nf
    rows = tr_ref[s]
    active = jnp.logical_and(g >= 1, rows > 0)
    cast_slot = g % 2
    use_slot = (g + 1) % 2
    slab = EXP_CHUNK * ROW_CHUNKS
    per_step = EXP_ROWS // nf

    def row_copy(tok, r):
        return pltpu.make_async_copy(
            x_hbm.at[pl.ds(pl.multiple_of(tok * ROW_CHUNKS, ROW_CHUNKS), ROW_CHUNKS), :],
            stage_ref.at[pl.ds(pl.multiple_of(r * ROW_CHUNKS, ROW_CHUNKS), ROW_CHUNKS), :], gsem)

    def wait_gather():
        pltpu.make_async_copy(x_hbm.at[pl.ds(0, EXP_ROWS * ROW_CHUNKS), :], stage_ref, gsem).wait()

    def cast_weights():
        wgb_ref[cast_slot] = wg_ref[0].astype(BF16)
        wub_ref[cast_slot] = wu_ref[0].astype(BF16)
        wdb_ref[cast_slot] = wd_ref[0].astype(BF16)

    @pl.when(g == 0)
    def _():
        stage_ref[...] = jnp.zeros_like(stage_ref)
        acc_ref[...] = jnp.zeros_like(acc_ref)
        base = tb_ref[0]

        def group(q, carry):
            for u in range(GATHER_UNROLL):
                r = q * GATHER_UNROLL + u
                row_copy(rt_ref[base + r], r).start()
            return carry

        lax.fori_loop(0, EXP_ROWS // GATHER_UNROLL, group, 0)

        cast_weights()

    @pl.when(active)
    def _():
        @pl.when(j == 0)
        def _():
            wait_gather()
            for i in range(EXP_ROWS // EXP_CHUNK):
                xb_ref[i * EXP_CHUNK:(i + 1) * EXP_CHUNK, :] = _from_slabs(
                    stage_ref[i * slab:(i + 1) * slab, :]).astype(BF16)

        def step_body(m):
            cast_weights()
            nxt_base = tb_ref[jnp.minimum(s + 1, n_tiles - 1)]
            for u in range(per_step):
                r = j * per_step + u
                row_copy(rt_ref[nxt_base + r], r).start()

            xb = xb_ref[0:m, :]
            gate = jnp.dot(xb, wgb_ref[use_slot], preferred_element_type=F32) + bg_ref[0]
            up = jnp.dot(xb, wub_ref[use_slot], preferred_element_type=F32) + bu_ref[0]
            gate = jnp.minimum(gate, SWIGLU_LIMIT)
            up = jnp.clip(up, -SWIGLU_LIMIT, SWIGLU_LIMIT)
            act = (up + 1.0) * (gate * jax.nn.sigmoid(SWIGLU_ALPHA * gate))
            part = jnp.dot(act.astype(BF16), wdb_ref[use_slot], preferred_element_type=F32)
            acc_ref[0:m, :] = jnp.where(j == 0, jnp.broadcast_to(bd_ref[0], part.shape), acc_ref[0:m, :]) + part

        lo = 0
        for m in EXP_M_SIZES:
            @pl.when(jnp.logical_and(rows > lo, rows <= m))
            def _(m=m):
                step_body(m)
            lo = m

        @pl.when(j == nf - 1)
        def _():
            nchunk = (rows + EXP_CHUNK - 1) // EXP_CHUNK

            def chunk_copy(i, slot):
                dst0 = pl.multiple_of((s * EXP_ROWS + i * EXP_CHUNK) * ROW_CHUNKS, slab)
                return pltpu.make_async_copy(ring_ref.at[slot], y_hbm.at[pl.ds(dst0, slab), :], osem.at[slot])

            def emit(i, carry):
                slot = i % 2

                @pl.when(i >= 2)
                def _():
                    chunk_copy(i - 2, slot).wait()

                r0 = pl.multiple_of(i * EXP_CHUNK, EXP_CHUNK)
                ring_ref[slot] = _to_slabs(acc_ref[pl.ds(r0, EXP_CHUNK), :])
                chunk_copy(i, slot).start()
                return carry

            lax.fori_loop(0, nchunk, emit, 0)
            for back in range(2):
                @pl.when(nchunk > back)
                def _():
                    last = nchunk - 1 - back
                    chunk_copy(last, last % 2).wait()

    @pl.when(g == n_items)
    def _():
        wait_gather()


def _experts(tile_e, tile_rows, tile_base, row_tok, used_tiles, x1c, w_gu, b_gu, w_down, b_down):
    n_tiles = tile_e.shape[0]
    nf = D_FF // EXP_TF
    n_items = n_tiles * nf

    def item_block(item, te, tr):
        s = item // nf
        return te[s], jnp.where(tr[s] > 0, item % nf, nf - 1)

    def cast_item(g, te, tr):
        return item_block(jnp.minimum(g, n_items - 1), te, tr)

    def use_item(g, te, tr):
        return item_block(jnp.maximum(g - 1, 0), te, tr)

    def w_gate(g, te, tr, tb, rt):
        e, j = cast_item(g, te, tr)
        return e, 0, j

    def w_up(g, te, tr, tb, rt):
        e, j = cast_item(g, te, tr)
        return e, 0, nf + j

    def w_down_map(g, te, tr, tb, rt):
        e, j = cast_item(g, te, tr)
        return e, j, 0

    def b_gate(g, te, tr, tb, rt):
        e, j = use_item(g, te, tr)
        return e, 0, j

    def b_up(g, te, tr, tb, rt):
        e, j = use_item(g, te, tr)
        return e, 0, nf + j

    def b_down_map(g, te, tr, tb, rt):
        e, _ = use_item(g, te, tr)
        return e, 0, 0

    grid_spec = pltpu.PrefetchScalarGridSpec(
        num_scalar_prefetch=4,
        grid=(used_tiles * nf + 1,),
        in_specs=[pl.BlockSpec(memory_space=pl.ANY),
                  pl.BlockSpec((1, D_MODEL, EXP_TF), w_gate),
                  pl.BlockSpec((1, D_MODEL, EXP_TF), w_up),
                  pl.BlockSpec((1, EXP_TF, D_MODEL), w_down_map),
                  pl.BlockSpec((1, 1, EXP_TF), b_gate),
                  pl.BlockSpec((1, 1, EXP_TF), b_up),
                  pl.BlockSpec((1, 1, D_MODEL), b_down_map)],
        out_specs=pl.BlockSpec(memory_space=pl.ANY),
        scratch_shapes=[pltpu.VMEM((EXP_ROWS * ROW_CHUNKS, LANES), U32),
                        pltpu.VMEM((EXP_ROWS, D_MODEL), BF16),
                        pltpu.VMEM((EXP_ROWS, D_MODEL), F32),
                        pltpu.VMEM((2, EXP_CHUNK * ROW_CHUNKS, LANES), U32),
                        pltpu.VMEM((2, D_MODEL, EXP_TF), BF16),
                        pltpu.VMEM((2, D_MODEL, EXP_TF), BF16),
                        pltpu.VMEM((2, EXP_TF, D_MODEL), BF16),
                        pltpu.SemaphoreType.DMA(()),
                        pltpu.SemaphoreType.DMA((2,))])
    return pl.pallas_call(
        _expert_kernel,
        grid_spec=grid_spec,
        out_shape=jax.ShapeDtypeStruct((n_tiles * EXP_ROWS * ROW_CHUNKS, LANES), U32),
        compiler_params=_cparams(("arbitrary",)),
        name="experts")(tile_e, tile_rows, tile_base, row_tok, x1c, w_gu, w_gu, w_down, b_gu, b_gu, b_down)


def _combine_kernel(dest_ref, y_hbm, x_ref, tg_ref, g_ref, be_ref, o_ref, buf_ref, sem):
    tm = o_ref.shape[0]
    i = pl.program_id(0)
    n = pl.num_programs(0)

    def row_copy(src_row, slot, k, t):
        return pltpu.make_async_copy(
            y_hbm.at[pl.ds(pl.multiple_of(src_row * ROW_CHUNKS, ROW_CHUNKS), ROW_CHUNKS), :],
            buf_ref.at[slot, k, pl.ds(pl.multiple_of(t * ROW_CHUNKS, ROW_CHUNKS), ROW_CHUNKS), :],
            sem.at[slot])

    def start_tile(tile, slot):
        base = tile * tm * TOP_K

        def body(t2, carry):
            for u in range(2):
                t = t2 * 2 + u
                for k in range(TOP_K):
                    row_copy(dest_ref[base + t * TOP_K + k], slot, k, t).start()
            return carry

        lax.fori_loop(0, tm // 2, body, 0)

    def wait_tile(slot):
        for k in range(TOP_K):
            pltpu.make_async_copy(y_hbm.at[pl.ds(0, tm * ROW_CHUNKS), :], buf_ref.at[slot, k], sem.at[slot]).wait()

    @pl.when(i == 0)
    def _():
        start_tile(0, 0)

    @pl.when(i + 1 < n)
    def _():
        start_tile(jnp.minimum(i + 1, n - 1), (i + 1) % 2)

    slot = i % 2
    wait_tile(slot)
    tg = tg_ref[...]
    ff = jnp.zeros((tm, D_MODEL), F32)
    for k in range(TOP_K):
        ff = ff + tg[:, k:k + 1] * _from_slabs(buf_ref[slot, k])
    o_ref[...] = _layer_norm(DN_ALPHA * x_ref[...] + ff, g_ref[...], be_ref[...])


def _combine_ln(dest_flat, y_buf, x1, tg, g, be):
    t = tg.shape[0]
    tm = CMB_TM
    grid_spec = pltpu.PrefetchScalarGridSpec(
        num_scalar_prefetch=1,
        grid=(t // tm,),
        in_specs=[pl.BlockSpec(memory_space=pl.ANY),
                  pl.BlockSpec((tm, D_MODEL), lambda i, d: (i, 0)),
                  pl.BlockSpec((tm, LANES), lambda i, d: (i, 0)),
                  pl.BlockSpec((1, D_MODEL), lambda i, d: (0, 0)),
                  pl.BlockSpec((1, D_MODEL), lambda i, d: (0, 0))],
        out_specs=pl.BlockSpec((tm, D_MODEL), lambda i, d: (i, 0)),
        scratch_shapes=[pltpu.VMEM((2, TOP_K, tm * ROW_CHUNKS, LANES), U32), pltpu.SemaphoreType.DMA((2,))])
    return pl.pallas_call(
        _combine_kernel,
        grid_spec=grid_spec,
        out_shape=jax.ShapeDtypeStruct((t, D_MODEL), F32),
        compiler_params=_cparams(("arbitrary",)),
        name="combine_ln2")(dest_flat, y_buf, x1, tg, g, be)


def _dft_kernel(cd_ref, sd_ref, ca_ref, sa_ref, c_ref, s_ref):
    cd, sd = cd_ref[0], sd_ref[0]
    ca, sa = ca_ref[0], sa_ref[0]
    c_ref[0] = (cd * ca - sd * sa).astype(c_ref.dtype)
    s_ref[0] = (sd * ca + cd * sa).astype(s_ref.dtype)


def _dft_tables(seq):
    n = 2 * seq
    half = seq // 2
    nblk = half // DFT_TB
    idx = jnp.arange(half, dtype=I32)
    off = jnp.arange(DFT_TB, dtype=I32)
    start = jnp.arange(nblk, dtype=I32) * DFT_TB

    def angle(prod):
        return (prod % n).astype(F32) * (2.0 * math.pi / n)

    ang_d = jnp.stack([angle(off[:, None] * (2 * idx)[None, :]),
                       angle(off[:, None] * (2 * idx + 1)[None, :]),
                       angle((2 * off + 1)[:, None] * idx[None, :])])
    ang_a = jnp.stack([angle(start[:, None] * (2 * idx)[None, :]),
                       angle(start[:, None] * (2 * idx + 1)[None, :]),
                       angle((2 * start)[:, None] * idx[None, :])]).reshape(3 * nblk, 1, half)
    small = pl.BlockSpec((1, DFT_TB, half), lambda k, a: (k, 0, 0))
    base = pl.BlockSpec((1, 1, half), lambda k, a: (k * nblk + a, 0, 0))
    out = pl.BlockSpec((1, DFT_TB, half), lambda k, a: (k, a, 0))
    sds = jax.ShapeDtypeStruct((3, half, half), BF16)
    return pl.pallas_call(
        _dft_kernel, grid=(3, nblk), in_specs=[small, small, base, base], out_specs=[out, out],
        out_shape=[sds, sds], compiler_params=_cparams(("arbitrary", "arbitrary")),
        name="dft_tables")(jnp.cos(ang_d), jnp.sin(ang_d), jnp.cos(ang_a), jnp.sin(ang_a))


def _filter_features(seq):
    t = jnp.linspace(0.0, 1.0, seq, dtype=F32)[:, None]
    bands = (HY_EMB - 1) // 2
    fb = jnp.linspace(1e-4, bands - 1, bands, dtype=F32)[None]
    w = 2.0 * math.pi * jnp.arange(seq, dtype=F32)[:, None] / seq
    z = jnp.concatenate([t, jnp.cos(fb * w), -jnp.sin(fb * w)], -1)
    z = jnp.concatenate([z[0::2], z[1::2]], axis=0)
    return jnp.pad(z, ((0, 0), (0, LANES - HY_EMB)))


def _mixer(x, w_in, b_in, hy_conv_w, hy_conv_b, hy_filt_w1, hy_filt_b1, hy_filt_w2, hy_filt_b2,
           hy_filt_w3, hy_filt_freq, hy_skip, hy_norm_w, ml_conv_w, ml_conv_b, ml_norm_w):
    bsz, seq, _ = x.shape
    t = bsz * seq
    x2d = x.reshape(t, D_MODEL)
    n_main = w_in.shape[1] - N_GATE_COLS
    w_t = jnp.swapaxes(w_in, 0, 1)
    wg = jnp.pad(w_t[n_main:], ((0, LANES - N_GATE_COLS), (0, 0)))
    bg = jnp.pad(b_in[None, n_main:], ((0, 0), (0, LANES - N_GATE_COLS)))
    proj, gates = _in_proj(x2d, w_t[:n_main].astype(BF16), b_in[None, :n_main], wg, bg)
    proj3 = proj.reshape(bsz, seq, n_main)
    gates = gates[:, :N_GATE_COLS]
    g5 = gates.reshape(bsz, seq, 4, ML_HEADS)
    grow = g5.transpose(0, 3, 2, 1).reshape(bsz, ML_HEADS, 4, seq // CHUNK, CHUNK)
    gtr = grow.transpose(0, 1, 2, 4, 3)

    cmat, smat = _dft_tables(seq)
    zpad = _filter_features(seq)
    w1pad = jnp.pad(hy_filt_w1, ((0, LANES - HY_EMB), (0, 0)))
    deltas = jnp.abs(jnp.linspace(math.log(HY_DECAY_TARGET) / HY_SLOW_PCT,
                                  math.log(HY_DECAY_TARGET) / HY_FAST_PCT, D_HY, dtype=F32))[None]
    kr, ki, km = _hyena_filters(zpad, w1pad, hy_filt_b1[None], hy_filt_w2, hy_filt_b2[None],
                                hy_filt_freq, hy_filt_w3, deltas, cmat, smat)
    y_hy = _hyena(proj3, hy_conv_w, hy_conv_b[None], cmat, smat, kr, ki, km, hy_skip, hy_norm_w[None])
    y_ml = _mlstm(proj3, ml_conv_w, ml_conv_b[None], grow, gtr, ml_norm_w[None])
    return y_hy.reshape(t, D_HY), y_ml.reshape(t, D_ML), x2d


def _moe_tables(top_i, slot, counts):
    t = top_i.shape[0]
    n_tiles = N_EXPERTS + (t * TOP_K) // EXP_ROWS
    ntile = (counts + EXP_ROWS - 1) // EXP_ROWS
    ends = jnp.cumsum(ntile)
    starts = ends - ntile
    total = ends[-1]
    s_idx = jnp.arange(n_tiles, dtype=I32)
    valid = s_idx < total
    s_eff = jnp.where(valid, s_idx, jnp.maximum(total - 1, 0))
    tile_e = jnp.minimum(jnp.sum((s_eff[:, None] >= ends[None, :]).astype(I32), axis=1), N_EXPERTS - 1)
    local = s_eff - starts[tile_e]
    tile_rows = jnp.where(valid, jnp.clip(counts[tile_e] - local * EXP_ROWS, 0, EXP_ROWS), 0).astype(I32)
    tok = jnp.arange(t, dtype=I32)[:, None]
    row_tok = jnp.pad(jnp.sort((top_i * t + tok).reshape(-1)) % t, (0, EXP_ROWS))
    first = jnp.cumsum(counts) - counts
    tile_base = (first[tile_e] + local * EXP_ROWS).astype(I32)
    onehot = top_i[:, :, None] == jnp.arange(N_EXPERTS, dtype=I32)
    dest = jnp.sum(jnp.where(onehot, starts * EXP_ROWS, 0), axis=-1) + slot
    return (tile_e.astype(I32), tile_rows, tile_base, row_tok.astype(I32), total.astype(I32),
            dest.astype(I32).reshape(-1))


def kernel(x, w_in, b_in, hy_conv_w, hy_conv_b, hy_filt_w1, hy_filt_b1, hy_filt_w2, hy_filt_b2, hy_filt_w3, hy_filt_freq, hy_skip, hy_norm_w, ml_conv_w, ml_conv_b, ml_norm_w, w_out, b_out, ln1_g, ln1_b, router_w, router_b, w_gu, b_gu, w_down, b_down, ln2_g, ln2_b):
    bsz, seq, _ = x.shape
    l = 0
    y_hy, y_ml, x2d = _mixer(x, w_in[l], b_in[l], hy_conv_w[l], hy_conv_b[l], hy_filt_w1[l], hy_filt_b1[l],
                             hy_filt_w2[l], hy_filt_b2[l], hy_filt_w3[l], hy_filt_freq[l], hy_skip[l],
                             hy_norm_w[l], ml_conv_w[l], ml_conv_b[l], ml_norm_w[l])
    rw = jnp.pad(router_w[l], ((0, 0), (0, LANES - N_EXPERTS)))
    rb = jnp.pad(router_b[l][None], ((0, 0), (0, LANES - N_EXPERTS)), constant_values=-1e30)
    x1, x1c, top_i, top_g, slot, cnt = _out_proj_ln_route(
        y_hy, y_ml, x2d, w_out[l].astype(BF16), b_out[l][None], ln1_g[l][None], ln1_b[l][None], rw, rb)
    counts = cnt[0, :N_EXPERTS].astype(I32)
    tile_e, tile_rows, tile_base, row_tok, used, dest = _moe_tables(top_i[:, :TOP_K], slot[:, :TOP_K], counts)
    y_buf = _experts(tile_e, tile_rows, tile_base, row_tok, used, x1c, w_gu[l], b_gu[l][:, None, :], w_down[l],
                     b_down[l][:, None, :])
    out = _combine_ln(dest, y_buf, x1, top_g, ln2_g[l][None], ln2_b[l][None])
    return out.reshape(bsz, seq, D_MODEL)
```

```python
import functools
import math

import jax
import jax.numpy as jnp
from jax import lax
from jax.experimental import pallas as pl
from jax.experimental.pallas import tpu as pltpu

F32 = jnp.float32
BF16 = jnp.bfloat16
I32 = jnp.int32
U32 = jnp.uint32
HP = lax.Precision.HIGHEST

D_MODEL = 2048
D_HY = 1024
D_ML = 1024
ML_HEADS = 8
HEAD_DIM = 128
CHUNK = 128
N_GATE_COLS = 32
HY_EMB = 33
N_EXPERTS = 32
TOP_K = 4
D_FF = 2048
SWIGLU_LIMIT = 7.0
SWIGLU_ALPHA = 1.702
LN_EPS = 1e-5
DN_ALPHA = 2.0 ** 0.25
HY_DECAY_TARGET = 1e-2
HY_FAST_PCT = 0.3
HY_SLOW_PCT = 1.5

LANES = 128
ROW_CHUNKS = D_MODEL // (2 * LANES)
VMEM_LIMIT = 60 * 1024 * 1024

PROJ_TM = 2048
PROJ_TN = 512
PROJ_GATE_ROWS = 512
HY_CW = 256
HY_FC = 512
OUT_TM = 512
EXP_ROWS = 1152
EXP_M_SIZES = (1024, 1088, 1152)
EXP_CHUNK = 128
EXP_TF = 256
GATHER_UNROLL = 8
CMB_TM = 128
DFT_TB = 128


def _cparams(sem):
    return pltpu.CompilerParams(dimension_semantics=sem, vmem_limit_bytes=VMEM_LIMIT)


def _split(a):
    hi = a.astype(BF16)
    return hi, (a - hi.astype(F32)).astype(BF16)


def _dot3(a, b, dims):
    a_hi, a_lo = _split(a)
    b_hi, b_lo = _split(b)
    mm = functools.partial(lax.dot_general, dimension_numbers=dims, preferred_element_type=F32)
    return mm(a_hi, b_hi) + (mm(a_hi, b_lo) + mm(a_lo, b_hi))


NN = (((1,), (0,)), ((), ()))
NT = (((1,), (1,)), ((), ()))


def _const_spec(shape, index_map):
    return pl.BlockSpec(shape, index_map, pipeline_mode=pl.Buffered(1))


def _proj_kernel(x_ref, w_ref, b_ref, wg_ref, bg_ref, o_ref, og_ref, xb_ref):
    @pl.when(pl.program_id(1) == 0)
    def _():
        xb_ref[...] = x_ref[...].astype(BF16)
        wg_hi, wg_lo = _split(wg_ref[...])
        mm = functools.partial(lax.dot_general, dimension_numbers=NT, preferred_element_type=F32)
        for r in range(x_ref.shape[0] // PROJ_GATE_ROWS):
            rows = slice(r * PROJ_GATE_ROWS, (r + 1) * PROJ_GATE_ROWS)
            x_hi = xb_ref[rows, :]
            x_lo = (x_ref[rows, :] - x_hi.astype(F32)).astype(BF16)
            og_ref[rows, :] = mm(x_hi, wg_hi) + (mm(x_hi, wg_lo) + mm(x_lo, wg_hi)) + bg_ref[...]

    acc = lax.dot_general(xb_ref[...], w_ref[...], NT, preferred_element_type=F32)
    o_ref[...] = (acc + b_ref[...]).astype(o_ref.dtype)


def _in_proj(x2d, wt_bf, b_row, wt_gate, b_gate):
    m, k = x2d.shape
    n = wt_bf.shape[0]
    return pl.pallas_call(
        _proj_kernel,
        grid=(m // PROJ_TM, n // PROJ_TN),
        in_specs=[pl.BlockSpec((PROJ_TM, k), lambda i, j: (i, 0)),
                  pl.BlockSpec((PROJ_TN, k), lambda i, j: (j, 0)),
                  pl.BlockSpec((1, PROJ_TN), lambda i, j: (0, j)),
                  _const_spec((LANES, k), lambda i, j: (0, 0)),
                  _const_spec((1, LANES), lambda i, j: (0, 0))],
        out_specs=[pl.BlockSpec((PROJ_TM, PROJ_TN), lambda i, j: (i, j)),
                   pl.BlockSpec((PROJ_TM, LANES), lambda i, j: (i, 0))],
        out_shape=[jax.ShapeDtypeStruct((m, n), BF16), jax.ShapeDtypeStruct((m, LANES), F32)],
        scratch_shapes=[pltpu.VMEM((PROJ_TM, k), BF16)],
        compiler_params=_cparams(("arbitrary", "arbitrary")),
        name="in_proj")(x2d, wt_bf, b_row, wt_gate, b_gate)


def _filter_kernel(z_ref, w1_ref, b1_ref, w2_ref, b2_ref, fq_ref, w3f_ref, w3b_ref, dl_ref,
                   c_ref, s_ref, kr_ref, ki_ref, km_ref, h_ref):
    seq = z_ref.shape[0]
    inv_n = 1.0 / (2 * seq)
    z = z_ref[...]

    @pl.when(jnp.logical_and(pl.program_id(0) == 0, pl.program_id(1) == 0))
    def _():
        h1 = jnp.sin(fq_ref[0:1, :] * (jnp.dot(z, w1_ref[...], precision=HP, preferred_element_type=F32)
                                       + b1_ref[...]))
        h_ref[...] = jnp.sin(fq_ref[1:2, :] * (jnp.dot(h1, w2_ref[...], precision=HP, preferred_element_type=F32)
                                               + b2_ref[...]))

    half = seq // 2
    h = h_ref[...]
    win = jnp.exp(-z[:, 0:1] * dl_ref[...])
    fwd = _dot3(h, w3f_ref[...], NN) * win
    bwd = _dot3(h, w3b_ref[...], NN) * win
    row = lax.broadcasted_iota(I32, fwd.shape, 0)
    bwd = jnp.where(row == 0, 0.0, bwd)
    inv = 1.0 / jnp.sum(jnp.abs(fwd) + jnp.abs(bwd), axis=0, keepdims=True)
    ks = ((fwd + bwd) * inv)
    kd = ((fwd - bwd) * inv)
    ksb = ks.astype(BF16)
    kdb = kd.astype(BF16)
    ec = jnp.dot(c_ref[0], ksb[:half], preferred_element_type=F32)
    oc = jnp.dot(c_ref[1], ksb[half:], preferred_element_type=F32)
    es = jnp.dot(s_ref[0], kdb[:half], preferred_element_type=F32)
    os_ = jnp.dot(s_ref[1], kdb[half:], preferred_element_type=F32)
    rowh = lax.broadcasted_iota(I32, ec.shape, 0)
    wf = jnp.where(rowh == 0, inv_n, 2.0 * inv_n)
    kr_ref[0, :half, :] = (ec + oc) * wf
    kr_ref[0, half:, :] = (ec - oc) * wf
    ki_ref[0, :half, :] = -(es + os_) * wf
    ki_ref[0, half:, :] = (es - os_) * wf
    alt = jnp.where((rowh & 1) == 0, 1.0, -1.0)
    km_ref[0, 0:1, :] = jnp.sum(ks[:half] * alt, axis=0, keepdims=True) * (2.0 * inv_n)
    km_ref[0, 1:2, :] = -jnp.sum(kd[half:] * alt, axis=0, keepdims=True) * (2.0 * inv_n)


def _hyena_filters(zpad, w1pad, b1, w2, b2, freq, w3, deltas, cmat, smat):
    seq = zpad.shape[0]
    nb = D_HY // HY_CW
    hid = w2.shape[0]
    full = lambda shape: pl.BlockSpec(shape, lambda o, c: (0,) * len(shape))
    out_sds = jax.ShapeDtypeStruct((2, seq, D_HY), F32)
    return pl.pallas_call(
        _filter_kernel,
        grid=(2, nb),
        in_specs=[full(zpad.shape), full(w1pad.shape), full(b1.shape), full(w2.shape), full(b2.shape),
                  full(freq.shape),
                  pl.BlockSpec((hid, HY_CW), lambda o, c: (0, o * 2 * nb + c)),
                  pl.BlockSpec((hid, HY_CW), lambda o, c: (0, o * 2 * nb + nb + c)),
                  pl.BlockSpec((1, HY_CW), lambda o, c: (0, c)),
                  _const_spec(cmat.shape, lambda o, c: (0, 0, 0)),
                  _const_spec(smat.shape, lambda o, c: (0, 0, 0))],
        out_specs=[pl.BlockSpec((1, seq, HY_CW), lambda o, c: (o, 0, c)),
                   pl.BlockSpec((1, seq, HY_CW), lambda o, c: (o, 0, c)),
                   pl.BlockSpec((1, 2, HY_CW), lambda o, c: (o, 0, c))],
        out_shape=[out_sds, out_sds, jax.ShapeDtypeStruct((2, 2, D_HY), F32)],
        scratch_shapes=[pltpu.VMEM((seq, hid), F32)],
        compiler_params=_cparams(("arbitrary", "arbitrary")),
        name="hyena_filters")(zpad, w1pad, b1, w2, b2, freq, w3, w3, deltas, cmat, smat)


def _short_conv(u, w_ref, b_ref, row, seq):
    prev = jnp.where(row == 0, 0.0, pltpu.roll(u, 1, 0))
    nxt = jnp.where(row == seq - 1, 0.0, pltpu.roll(u, seq - 1, 0))
    return w_ref[0:1, :] * prev + w_ref[1:2, :] * u + w_ref[2:3, :] * nxt + b_ref[...]


def _hyena_kernel(uv_ref, u1_ref, u2_ref, wv_ref, w1_ref, w2_ref, bv_ref, b1_ref, b2_ref,
                  c_ref, s_ref, kr_ref, ki_ref, km_ref, skip_ref, nw_ref, o_ref,
                  a_ref, b_ref, t_ref, zb_ref, s1_ref, s2_ref, s3_ref, s4_ref, md_ref):
    seq = uv_ref.shape[1]
    cw = uv_ref.shape[2]
    half = seq // 2
    nblk = half // HY_FC
    row = lax.broadcasted_iota(I32, (seq, LANES), 0)
    alt_half = jnp.where((lax.broadcasted_iota(I32, (half, cw), 0) & 1) == 0, 1.0, -1.0)
    alt_blk = jnp.where((lax.broadcasted_iota(I32, (HY_FC, cw), 0) & 1) == 0, 1.0, -1.0)
    groups = [slice(g * LANES, (g + 1) * LANES) for g in range(cw // LANES)]

    def conv_to(dst_ref, u_ref, w_ref, bias_ref):
        for gi, gs in enumerate(groups):
            t_ref[gi] = _short_conv(u_ref[0, :, gs].astype(F32), w_ref.at[:, gs], bias_ref.at[:, gs], row, seq)
            dst_ref[0:half, gs] = t_ref[gi, pl.ds(0, half, stride=2), :]
            dst_ref[half:seq, gs] = t_ref[gi, pl.ds(1, half, stride=2), :]

    def spectrum(zin_ref, o):
        z = zin_ref[...]
        zb_ref[...] = z.astype(BF16)
        am = jnp.sum(z[:half] * alt_half, axis=0, keepdims=True)
        bm = jnp.sum(z[half:] * alt_half, axis=0, keepdims=True)
        krm, kim = km_ref[o, 0:1, :], km_ref[o, 1:2, :]
        md_ref[0:1, :] = am * krm + bm * kim
        md_ref[1:2, :] = am * kim - bm * krm
        ze = zb_ref[0:half, :]
        zo = zb_ref[half:seq, :]
        for gb in range(nblk):
            lo = slice(gb * HY_FC, (gb + 1) * HY_FC)
            hi = slice(half + gb * HY_FC, half + (gb + 1) * HY_FC)
            ec = jnp.dot(c_ref[0, lo, :], ze, preferred_element_type=F32)
            oc = jnp.dot(c_ref[1, lo, :], zo, preferred_element_type=F32)
            es = jnp.dot(s_ref[0, lo, :], ze, preferred_element_type=F32)
            os_ = jnp.dot(s_ref[1, lo, :], zo, preferred_element_type=F32)
            a_lo, b_lo = ec + oc, es + os_
            a_hi, b_hi = ec - oc, os_ - es
            krl, kil = kr_ref[o, lo, :], ki_ref[o, lo, :]
            krh, kih = kr_ref[o, hi, :], ki_ref[o, hi, :]
            pr = a_lo * krl + b_lo * kil
            pi = a_lo * kil - b_lo * krl
            qr = a_hi * krh + b_hi * kih
            qi = a_hi * kih - b_hi * krh
            s1_ref[lo, :] = (pr + qr).astype(BF16)
            s2_ref[lo, :] = (pi - qi).astype(BF16)
            s3_ref[lo, :] = (pr - qr).astype(BF16)
            s4_ref[lo, :] = (pi + qi).astype(BF16)

    def conv_rows(ub, parity):
        us = slice(ub * HY_FC, (ub + 1) * HY_FC)
        if parity == 0:
            y = jnp.dot(c_ref[0, us, :], s1_ref[...], preferred_element_type=F32)
            y = y - jnp.dot(s_ref[0, us, :], s2_ref[...], preferred_element_type=F32)
            y = y + md_ref[0:1, :] * alt_blk
        else:
            y = jnp.dot(c_ref[2, us, :], s3_ref[...], preferred_element_type=F32)
            y = y - jnp.dot(s_ref[2, us, :], s4_ref[...], preferred_element_type=F32)
            y = y - md_ref[1:2, :] * alt_blk
        return slice(parity * half + ub * HY_FC, parity * half + (ub + 1) * HY_FC), y

    blocks = [(ub, parity) for parity in range(2) for ub in range(nblk)]
    conv_to(a_ref, uv_ref, wv_ref, bv_ref)
    conv_to(b_ref, u1_ref, w1_ref, b1_ref)
    spectrum(a_ref, 0)
    for ub, parity in blocks:
        rows, y = conv_rows(ub, parity)
        b_ref[rows, :] = b_ref[rows, :] * (y + skip_ref[0:1, :] * a_ref[rows, :])
    conv_to(a_ref, u2_ref, w2_ref, b2_ref)
    spectrum(b_ref, 1)
    for ub, parity in blocks:
        rows, y = conv_rows(ub, parity)
        z = a_ref[rows, :] * (y + skip_ref[1:2, :] * b_ref[rows, :])
        for gi, gs in enumerate(groups):
            zg = z[:, gs]
            mu = jnp.mean(zg, axis=-1, keepdims=True)
            zc = zg - mu
            var = jnp.mean(zc * zc, axis=-1, keepdims=True)
            t_ref[gi, pl.ds(2 * ub * HY_FC + parity, HY_FC, stride=2), :] = (
                zc * lax.rsqrt(var + LN_EPS) * nw_ref[:, gs])
    for gi, gs in enumerate(groups):
        o_ref[0, :, gs] = t_ref[gi].astype(o_ref.dtype)


def _hyena(proj3, conv_w, conv_b, cmat, smat, kr, ki, km, skip, norm_w):
    bsz, seq, _ = proj3.shape
    half = seq // 2
    nb = D_HY // HY_CW
    u_spec = lambda off: pl.BlockSpec((1, seq, HY_CW), lambda c, b: (b, 0, off + c))
    w_spec = lambda off: pl.BlockSpec((3, HY_CW), lambda c, b: (0, off + c))
    b_spec = lambda off: pl.BlockSpec((1, HY_CW), lambda c, b: (0, off + c))
    return pl.pallas_call(
        _hyena_kernel,
        grid=(nb, bsz),
        in_specs=[u_spec(0), u_spec(nb), u_spec(2 * nb),
                  w_spec(0), w_spec(nb), w_spec(2 * nb),
                  b_spec(0), b_spec(nb), b_spec(2 * nb),
                  _const_spec(cmat.shape, lambda c, b: (0, 0, 0)),
                  _const_spec(smat.shape, lambda c, b: (0, 0, 0)),
                  _const_spec((2, seq, HY_CW), lambda c, b: (0, 0, c)),
                  _const_spec((2, seq, HY_CW), lambda c, b: (0, 0, c)),
                  pl.BlockSpec((2, 2, HY_CW), lambda c, b: (0, 0, c)),
                  pl.BlockSpec((2, HY_CW), lambda c, b: (0, c)),
                  pl.BlockSpec((1, HY_CW), lambda c, b: (0, c))],
        out_specs=pl.BlockSpec((1, seq, HY_CW), lambda c, b: (b, 0, c)),
        out_shape=jax.ShapeDtypeStruct((bsz, seq, D_HY), BF16),
        scratch_shapes=[pltpu.VMEM((seq, HY_CW), F32), pltpu.VMEM((seq, HY_CW), F32),
                        pltpu.VMEM((HY_CW // LANES, seq, LANES), F32), pltpu.VMEM((seq, HY_CW), BF16),
                        pltpu.VMEM((half, HY_CW), BF16), pltpu.VMEM((half, HY_CW), BF16),
                        pltpu.VMEM((half, HY_CW), BF16), pltpu.VMEM((half, HY_CW), BF16),
                        pltpu.VMEM((2, HY_CW), F32)],
        compiler_params=_cparams(("arbitrary", "arbitrary")),
        name="hyena")(proj3, proj3, proj3, conv_w, conv_w, conv_w, conv_b, conv_b, conv_b,
                      cmat, smat, kr, ki, km, skip, norm_w)


def _mlstm_kernel(qp_ref, kp_ref, v_ref, og_ref, wq_ref, wk_ref, bq_ref, bk_ref, gr_ref, gt_ref,
                  nw_ref, o_ref, qb_ref, kb_ref, cst_ref, nst_ref):
    seq = qp_ref.shape[1]
    d = qp_ref.shape[2]
    nchunk = seq // CHUNK
    row = lax.broadcasted_iota(I32, (seq, d), 0)
    q = _short_conv(qp_ref[0].astype(F32), wq_ref, bq_ref, row, seq)
    k = _short_conv(kp_ref[0].astype(F32), wk_ref, bk_ref, row, seq)
    qb_ref[...] = (q * jax.nn.sigmoid(q)).astype(BF16)
    kb_ref[...] = ((k * jax.nn.sigmoid(k)) * (d ** -0.5)).astype(BF16)

    ti = lax.broadcasted_iota(I32, (CHUNK, CHUNK), 0)
    si = lax.broadcasted_iota(I32, (CHUNK, CHUNK), 1)
    lower = ti >= si
    upper = ti <= si
    lower_f = lower.astype(F32)
    upper_f = upper.astype(F32)
    nt = (((1,), (1,)), ((), ()))
    tn = (((0,), (0,)), ((), ()))
    chunk_rows = [slice(c * CHUNK, (c + 1) * CHUNK) for c in range(nchunk)]

    per_dir = []
    for direction in range(2):
        f_idx, i_idx = 2 * direction + 1, 2 * direction
        mask = lower if direction == 0 else upper
        order = list(range(nchunk)) if direction == 0 else list(range(nchunk - 1, -1, -1))
        lf_r = jax.nn.log_sigmoid(gr_ref[0, 0, f_idx])
        b_r = jnp.dot(lf_r, upper_f if direction == 0 else lower_f, precision=HP, preferred_element_type=F32)
        rterm = b_r - gr_ref[0, 0, i_idx]
        b_last = jnp.sum(lf_r, axis=-1, keepdims=True)
        lf_c = jax.nn.log_sigmoid(gt_ref[0, 0, f_idx])
        b_c = jnp.dot(lower_f if direction == 0 else upper_f, lf_c, precision=HP, preferred_element_type=F32)
        i_c = gt_ref[0, 0, i_idx]

        bcol, gcol, gmax, rowmax, blast = [], [], [], [], []
        for c in range(nchunk):
            bc = jnp.broadcast_to(b_c[:, c:c + 1], (CHUNK, CHUNK))
            ic = jnp.broadcast_to(i_c[:, c:c + 1], (CHUNK, CHUNK))
            bl = jnp.broadcast_to(b_last[c:c + 1, :], (1, CHUNK))
            dmat = jnp.where(mask, bc - rterm[c:c + 1, :], -jnp.inf)
            g = bl - bc + ic
            bcol.append(bc)
            gcol.append(g)
            blast.append(bl)
            rowmax.append(jnp.max(dmat, axis=-1, keepdims=True))
            gmax.append(jnp.max(g, axis=0, keepdims=True))

        m = jnp.zeros((1, CHUNK), F32)
        m_in, m_out = [None] * nchunk, [None] * nchunk
        for c in order:
            m_in[c] = m
            m = jnp.maximum(blast[c] + m, gmax[c])
            m_out[c] = m

        cmat = jnp.zeros((d, d), F32)
        nvec = jnp.zeros((1, d), F32)
        for c in order:
            kc = kb_ref[chunk_rows[c], :]
            vc = v_ref[0, chunk_rows[c], :]
            cst_ref[direction, c] = cmat.astype(BF16)
            nst_ref[direction, c] = nvec
            wg = jnp.exp(gcol[c] - m_out[c])
            decay = jnp.exp(blast[c] + m_in[c] - m_out[c])
            upd = lax.dot_general((wg * vc.astype(F32)).astype(BF16), kc, tn, preferred_element_type=F32)
            cmat = decay * cmat + upd
            nvec = decay * nvec + jnp.sum(wg * kc.astype(F32), axis=0, keepdims=True)
        per_dir.append((mask, bcol, rterm, rowmax, m_in))

    for c in range(nchunk):
        rs = chunk_rows[c]
        qc = qb_ref[rs, :]
        kc = kb_ref[rs, :]
        vc = v_ref[0, rs, :]
        qk = lax.dot_general(qc, kc, nt, preferred_element_type=F32)
        h_sum = None
        for direction, (mask, bcol, rterm, rowmax, m_in) in enumerate(per_dir):
            dmat = jnp.where(mask, bcol[c] - rterm[c:c + 1, :], -jnp.inf)
            inter = bcol[c] + m_in[c]
            m_t = jnp.maximum(inter, rowmax[c])
            s = qk * jnp.exp(dmat - m_t)
            inter_w = jnp.exp(inter - m_t)
            cq = lax.dot_general(qc, cst_ref[direction, c], nt, preferred_element_type=F32)
            num = jnp.dot(s.astype(BF16), vc, preferred_element_type=F32) + inter_w * cq
            nq = jnp.sum(qc.astype(F32) * nst_ref[direction, c], axis=-1, keepdims=True)
            den = jnp.sum(s, axis=-1, keepdims=True) + inter_w * nq
            h = num / jnp.maximum(jnp.abs(den), jnp.exp(-m_t))
            h_sum = h if h_sum is None else h_sum + h
        mu = jnp.mean(h_sum, axis=-1, keepdims=True)
        hc = h_sum - mu
        var = jnp.mean(hc * hc, axis=-1, keepdims=True)
        y = hc * lax.rsqrt(var + LN_EPS) * nw_ref[...] * jax.nn.sigmoid(og_ref[0, rs, :].astype(F32))
        o_ref[0, rs, :] = y.astype(o_ref.dtype)


def _mlstm(proj3, conv_w, conv_b, grow, gtr, norm_w):
    bsz, seq, _ = proj3.shape
    d = HEAD_DIM
    nchunk = seq // CHUNK
    hy_blocks = 3 * D_HY // d
    qoff, koff, voff, ooff = hy_blocks, hy_blocks + ML_HEADS, hy_blocks + 2 * ML_HEADS, hy_blocks + 3 * ML_HEADS
    p_spec = lambda off: pl.BlockSpec((1, seq, d), lambda b, h: (b, 0, off + h))
    return pl.pallas_call(
        _mlstm_kernel,
        grid=(bsz, ML_HEADS),
        in_specs=[p_spec(qoff), p_spec(koff), p_spec(voff), p_spec(ooff),
                  pl.BlockSpec((3, d), lambda b, h: (0, h)),
                  pl.BlockSpec((3, d), lambda b, h: (0, ML_HEADS + h)),
                  pl.BlockSpec((1, d), lambda b, h: (0, h)),
                  pl.BlockSpec((1, d), lambda b, h: (0, ML_HEADS + h)),
                  pl.BlockSpec((1, 1, 4, nchunk, CHUNK), lambda b, h: (b, h, 0, 0, 0)),
                  pl.BlockSpec((1, 1, 4, CHUNK, nchunk), lambda b, h: (b, h, 0, 0, 0)),
                  pl.BlockSpec((1, d), lambda b, h: (0, h))],
        out_specs=pl.BlockSpec((1, seq, d), lambda b, h: (b, 0, h)),
        out_shape=jax.ShapeDtypeStruct((bsz, seq, D_ML), BF16),
        scratch_shapes=[pltpu.VMEM((seq, d), BF16), pltpu.VMEM((seq, d), BF16),
                        pltpu.VMEM((2, nchunk, d, d), BF16), pltpu.VMEM((2, nchunk, 1, d), F32)],
        compiler_params=_cparams(("arbitrary", "arbitrary")),
        name="mlstm")(proj3, proj3, proj3, proj3, conv_w, conv_w, conv_b, conv_b, grow, gtr, norm_w)


def _to_slabs(y):
    n = y.shape[0]
    half = D_MODEL // 2
    lo = lax.bitcast_convert_type(y[:, :half].astype(BF16).astype(F32), U32) >> 16
    hi = lax.bitcast_convert_type(y[:, half:].astype(BF16).astype(F32), U32) & jnp.uint32(0xFFFF0000)
    words = hi | lo
    parts = jnp.stack([words[:, c * LANES:(c + 1) * LANES] for c in range(ROW_CHUNKS)], axis=0)
    return pltpu.einshape("crl->rcl", parts).reshape(n * ROW_CHUNKS, LANES)


def _from_slabs(v):
    n = v.shape[0] // ROW_CHUNKS
    parts = pltpu.einshape("rcl->crl", v.reshape(n, ROW_CHUNKS, LANES))
    lo = [lax.bitcast_convert_type(parts[c] << 16, F32) for c in range(ROW_CHUNKS)]
    hi = [lax.bitcast_convert_type(parts[c] & jnp.uint32(0xFFFF0000), F32) for c in range(ROW_CHUNKS)]
    return jnp.concatenate(lo + hi, axis=-1)


def _layer_norm(u, g, b):
    mu = jnp.mean(u, axis=-1, keepdims=True)
    uc = u - mu
    var = jnp.mean(uc * uc, axis=-1, keepdims=True)
    return uc * lax.rsqrt(var + LN_EPS) * g + b


def _route(x, w_ref, b_ref, ti_ref, tg_ref, tp_ref, cnt_ref):
    tm = ti_ref.shape[0]

    @pl.when(pl.program_id(0) == 0)
    def _():
        cnt_ref[...] = jnp.zeros_like(cnt_ref)

    logits = _dot3(x, w_ref[...], NN) + b_ref[...]
    lane = lax.broadcasted_iota(I32, (tm, LANES), 1)
    work = logits
    vals, idxs = [], []
    chosen = jnp.zeros((tm, LANES), F32)
    for _ in range(TOP_K):
        mx = jnp.max(work, axis=-1, keepdims=True)
        idx = jnp.min(jnp.where(work == mx, lane, LANES), axis=-1, keepdims=True)
        hit = lane == idx
        vals.append(mx)
        idxs.append(idx)
        chosen = jnp.where(hit, 1.0, chosen)
        work = jnp.where(hit, -jnp.inf, work)
    exps = [jnp.exp(v - vals[0]) for v in vals]
    den = exps[0] + exps[1] + exps[2] + exps[3]
    ri = lax.broadcasted_iota(I32, (tm, tm), 0)
    ci = lax.broadcasted_iota(I32, (tm, tm), 1)
    strict_lower = (ri > ci).astype(BF16)
    carry = cnt_ref[...]
    slot = carry + jnp.dot(strict_lower, chosen.astype(BF16), preferred_element_type=F32)
    ti = jnp.zeros((tm, LANES), I32)
    tg = jnp.zeros((tm, LANES), F32)
    tp = jnp.zeros((tm, LANES), F32)
    for k in range(TOP_K):
        sk = jnp.sum(jnp.where(lane == idxs[k], slot, 0.0), axis=-1, keepdims=True)
        ti = jnp.where(lane == k, idxs[k], ti)
        tg = jnp.where(lane == k, exps[k] / den, tg)
        tp = jnp.where(lane == k, sk, tp)
    ti_ref[...] = ti
    tg_ref[...] = tg
    tp_ref[...] = tp.astype(I32)
    cnt_ref[...] = carry + jnp.sum(chosen, axis=0, keepdims=True)


def _outproj_kernel(yh_ref, ym_ref, x_ref, wa_ref, wb_ref, b_ref, g_ref, be_ref, rw_ref, rb_ref,
                    o_ref, oc_ref, ti_ref, tg_ref, tp_ref, cnt_ref):
    mix = (jnp.dot(yh_ref[...], wa_ref[...], preferred_element_type=F32)
           + jnp.dot(ym_ref[...], wb_ref[...], preferred_element_type=F32) + b_ref[...])
    y = _layer_norm(DN_ALPHA * x_ref[...] + mix, g_ref[...], be_ref[...])
    o_ref[...] = y
    oc_ref[...] = _to_slabs(y)
    _route(y, rw_ref, rb_ref, ti_ref, tg_ref, tp_ref, cnt_ref)


def _out_proj_ln_route(y_hy, y_ml, x2d, w_out_bf, b_out, g, be, rw_pad, rb_pad):
    t = x2d.shape[0]
    tm = OUT_TM
    vec = lambda: pl.BlockSpec((1, D_MODEL), lambda i: (0, 0))
    lane_blk = lambda: pl.BlockSpec((tm, LANES), lambda i: (i, 0))
    return pl.pallas_call(
        _outproj_kernel,
        grid=(t // tm,),
        in_specs=[pl.BlockSpec((tm, D_HY), lambda i: (i, 0)),
                  pl.BlockSpec((tm, D_ML), lambda i: (i, 0)),
                  pl.BlockSpec((tm, D_MODEL), lambda i: (i, 0)),
                  _const_spec((D_HY, D_MODEL), lambda i: (0, 0)),
                  _const_spec((D_ML, D_MODEL), lambda i: (1, 0)),
                  vec(), vec(), vec(),
                  _const_spec((D_MODEL, LANES), lambda i: (0, 0)),
                  pl.BlockSpec((1, LANES), lambda i: (0, 0))],
        out_specs=[pl.BlockSpec((tm, D_MODEL), lambda i: (i, 0)),
                   pl.BlockSpec((tm * ROW_CHUNKS, LANES), lambda i: (i, 0)),
                   lane_blk(), lane_blk(), lane_blk(), pl.BlockSpec((1, LANES), lambda i: (0, 0))],
        out_shape=[jax.ShapeDtypeStruct((t, D_MODEL), F32),
                   jax.ShapeDtypeStruct((t * ROW_CHUNKS, LANES), U32),
                   jax.ShapeDtypeStruct((t, LANES), I32), jax.ShapeDtypeStruct((t, LANES), F32),
                   jax.ShapeDtypeStruct((t, LANES), I32), jax.ShapeDtypeStruct((1, LANES), F32)],
        compiler_params=_cparams(("arbitrary",)),
        name="out_proj_ln1_route")(y_hy, y_ml, x2d, w_out_bf, w_out_bf, b_out, g, be, rw_pad, rb_pad)


def _expert_kernel(te_ref, tr_ref, tb_ref, rt_ref, x_hbm, wg_ref, wu_ref, wd_ref, bg_ref, bu_ref, bd_ref, y_hbm,
                   stage_ref, xb_ref, acc_ref, ring_ref, wgb_ref, wub_ref, wdb_ref, gsem, osem):
    g = pl.program_id(0)
    n_items = pl.num_programs(0) - 1
    nf = D_FF // EXP_TF
    n_tiles = n_items // nf
    item = jnp.maximum(g - 1, 0)
    s = item // nf
    j = item % nf
    rows = tr_ref[s]
    active = jnp.logical_and(g >= 1, rows > 0)
    cast_slot = g % 2
    use_slot = (g + 1) % 2
    slab = EXP_CHUNK * ROW_CHUNKS
    per_step = EXP_ROWS // nf

    def row_copy(tok, r):
        return pltpu.make_async_copy(
            x_hbm.at[pl.ds(pl.multiple_of(tok * ROW_CHUNKS, ROW_CHUNKS), ROW_CHUNKS), :],
            stage_ref.at[pl.ds(pl.multiple_of(r * ROW_CHUNKS, ROW_CHUNKS), ROW_CHUNKS), :], gsem)

    def wait_gather():
        pltpu.make_async_copy(x_hbm.at[pl.ds(0, EXP_ROWS * ROW_CHUNKS), :], stage_ref, gsem).wait()

    def cast_weights():
        wgb_ref[cast_slot] = wg_ref[0].astype(BF16)
        wub_ref[cast_slot] = wu_ref[0].astype(BF16)
        wdb_ref[cast_slot] = wd_ref[0].astype(BF16)

    @pl.when(g == 0)
    def _():
        stage_ref[...] = jnp.zeros_like(stage_ref)
        acc_ref[...] = jnp.zeros_like(acc_ref)
        base = tb_ref[0]

        def group(q, carry):
            for u in range(GATHER_UNROLL):
                r = q * GATHER_UNROLL + u
                row_copy(rt_ref[base + r], r).start()
            return carry

        lax.fori_loop(0, EXP_ROWS // GATHER_UNROLL, group, 0)

        cast_weights()

    @pl.when(active)
    def _():
        @pl.when(j == 0)
        def _():
            wait_gather()
            for i in range(EXP_ROWS // EXP_CHUNK):
                xb_ref[i * EXP_CHUNK:(i + 1) * EXP_CHUNK, :] = _from_slabs(
                    stage_ref[i * slab:(i + 1) * slab, :]).astype(BF16)

        def step_body(m):
            cast_weights()
            nxt_base = tb_ref[jnp.minimum(s + 1, n_tiles - 1)]
            for u in range(per_step):
                r = j * per_step + u
                row_copy(rt_ref[nxt_base + r], r).start()

            xb = xb_ref[0:m, :]
            gate = jnp.dot(xb, wgb_ref[use_slot], preferred_element_type=F32) + bg_ref[0]
            up = jnp.dot(xb, wub_ref[use_slot], preferred_element_type=F32) + bu_ref[0]
            gate = jnp.minimum(gate, SWIGLU_LIMIT)
            up = jnp.clip(up, -SWIGLU_LIMIT, SWIGLU_LIMIT)
            act = (up + 1.0) * (gate * jax.nn.sigmoid(SWIGLU_ALPHA * gate))
            part = jnp.dot(act.astype(BF16), wdb_ref[use_slot], preferred_element_type=F32)
            acc_ref[0:m, :] = jnp.where(j == 0, jnp.broadcast_to(bd_ref[0], part.shape), acc_ref[0:m, :]) + part

        lo = 0
        for m in EXP_M_SIZES:
            @pl.when(jnp.logical_and(rows > lo, rows <= m))
            def _(m=m):
                step_body(m)
            lo = m

        @pl.when(j == nf - 1)
        def _():
            nchunk = (rows + EXP_CHUNK - 1) // EXP_CHUNK

            def chunk_copy(i, slot):
                dst0 = pl.multiple_of((s * EXP_ROWS + i * EXP_CHUNK) * ROW_CHUNKS, slab)
                return pltpu.make_async_copy(ring_ref.at[slot], y_hbm.at[pl.ds(dst0, slab), :], osem.at[slot])

            def emit(i, carry):
                slot = i % 2

                @pl.when(i >= 2)
                def _():
                    chunk_copy(i - 2, slot).wait()

                r0 = pl.multiple_of(i * EXP_CHUNK, EXP_CHUNK)
                ring_ref[slot] = _to_slabs(acc_ref[pl.ds(r0, EXP_CHUNK), :])
                chunk_copy(i, slot).start()
                return carry

            lax.fori_loop(0, nchunk, emit, 0)
            for back in range(2):
                @pl.when(nchunk > back)
                def _():
                    last = nchunk - 1 - back
                    chunk_copy(last, last % 2).wait()

    @pl.when(g == n_items)
    def _():
        wait_gather()


def _experts(tile_e, tile_rows, tile_base, row_tok, used_tiles, x1c, w_gu, b_gu, w_down, b_down):
    n_tiles = tile_e.shape[0]
    nf = D_FF // EXP_TF
    n_items = n_tiles * nf

    def item_block(item, te, tr):
        s = item // nf
        return te[s], jnp.where(tr[s] > 0, item % nf, nf - 1)

    def cast_item(g, te, tr):
        return item_block(jnp.minimum(g, n_items - 1), te, tr)

    def use_item(g, te, tr):
        return item_block(jnp.maximum(g - 1, 0), te, tr)

    def w_gate(g, te, tr, tb, rt):
        e, j = cast_item(g, te, tr)
        return e, 0, j

    def w_up(g, te, tr, tb, rt):
        e, j = cast_item(g, te, tr)
        return e, 0, nf + j

    def w_down_map(g, te, tr, tb, rt):
        e, j = cast_item(g, te, tr)
        return e, j, 0

    def b_gate(g, te, tr, tb, rt):
        e, j = use_item(g, te, tr)
        return e, 0, j

    def b_up(g, te, tr, tb, rt):
        e, j = use_item(g, te, tr)
        return e, 0, nf + j

    def b_down_map(g, te, tr, tb, rt):
        e, _ = use_item(g, te, tr)
        return e, 0, 0

    grid_spec = pltpu.PrefetchScalarGridSpec(
        num_scalar_prefetch=4,
        grid=(used_tiles * nf + 1,),
        in_specs=[pl.BlockSpec(memory_space=pl.ANY),
                  pl.BlockSpec((1, D_MODEL, EXP_TF), w_gate),
                  pl.BlockSpec((1, D_MODEL, EXP_TF), w_up),
                  pl.BlockSpec((1, EXP_TF, D_MODEL), w_down_map),
                  pl.BlockSpec((1, 1, EXP_TF), b_gate),
                  pl.BlockSpec((1, 1, EXP_TF), b_up),
                  pl.BlockSpec((1, 1, D_MODEL), b_down_map)],
        out_specs=pl.BlockSpec(memory_space=pl.ANY),
        scratch_shapes=[pltpu.VMEM((EXP_ROWS * ROW_CHUNKS, LANES), U32),
                        pltpu.VMEM((EXP_ROWS, D_MODEL), BF16),
                        pltpu.VMEM((EXP_ROWS, D_MODEL), F32),
                        pltpu.VMEM((2, EXP_CHUNK * ROW_CHUNKS, LANES), U32),
                        pltpu.VMEM((2, D_MODEL, EXP_TF), BF16),
                        pltpu.VMEM((2, D_MODEL, EXP_TF), BF16),
                        pltpu.VMEM((2, EXP_TF, D_MODEL), BF16),
                        pltpu.SemaphoreType.DMA(()),
                        pltpu.SemaphoreType.DMA((2,))])
    return pl.pallas_call(
        _expert_kernel,
        grid_spec=grid_spec,
        out_shape=jax.ShapeDtypeStruct((n_tiles * EXP_ROWS * ROW_CHUNKS, LANES), U32),
        compiler_params=_cparams(("arbitrary",)),
        name="experts")(tile_e, tile_rows, tile_base, row_tok, x1c, w_gu, w_gu, w_down, b_gu, b_gu, b_down)


def _combine_kernel(dest_ref, y_hbm, x_ref, tg_ref, g_ref, be_ref, o_ref, buf_ref, sem):
    tm = o_ref.shape[0]
    i = pl.program_id(0)
    n = pl.num_programs(0)

    def row_copy(src_row, slot, k, t):
        return pltpu.make_async_copy(
            y_hbm.at[pl.ds(pl.multiple_of(src_row * ROW_CHUNKS, ROW_CHUNKS), ROW_CHUNKS), :],
            buf_ref.at[slot, k, pl.ds(pl.multiple_of(t * ROW_CHUNKS, ROW_CHUNKS), ROW_CHUNKS), :],
            sem.at[slot])

    def start_tile(tile, slot):
        base = tile * tm * TOP_K

        def body(t2, carry):
            for u in range(2):
                t = t2 * 2 + u
                for k in range(TOP_K):
                    row_copy(dest_ref[base + t * TOP_K + k], slot, k, t).start()
            return carry

        lax.fori_loop(0, tm // 2, body, 0)

    def wait_tile(slot):
        for k in range(TOP_K):
            pltpu.make_async_copy(y_hbm.at[pl.ds(0, tm * ROW_CHUNKS), :], buf_ref.at[slot, k], sem.at[slot]).wait()

    @pl.when(i == 0)
    def _():
        start_tile(0, 0)

    @pl.when(i + 1 < n)
    def _():
        start_tile(jnp.minimum(i + 1, n - 1), (i + 1) % 2)

    slot = i % 2
    wait_tile(slot)
    tg = tg_ref[...]
    ff = jnp.zeros((tm, D_MODEL), F32)
    for k in range(TOP_K):
        ff = ff + tg[:, k:k + 1] * _from_slabs(buf_ref[slot, k])
    o_ref[...] = _layer_norm(DN_ALPHA * x_ref[...] + ff, g_ref[...], be_ref[...])


def _combine_ln(dest_flat, y_buf, x1, tg, g, be):
    t = tg.shape[0]
    tm = CMB_TM
    grid_spec = pltpu.PrefetchScalarGridSpec(
        num_scalar_prefetch=1,
        grid=(t // tm,),
        in_specs=[pl.BlockSpec(memory_space=pl.ANY),
                  pl.BlockSpec((tm, D_MODEL), lambda i, d: (i, 0)),
                  pl.BlockSpec((tm, LANES), lambda i, d: (i, 0)),
                  pl.BlockSpec((1, D_MODEL), lambda i, d: (0, 0)),
                  pl.BlockSpec((1, D_MODEL), lambda i, d: (0, 0))],
        out_specs=pl.BlockSpec((tm, D_MODEL), lambda i, d: (i, 0)),
        scratch_shapes=[pltpu.VMEM((2, TOP_K, tm * ROW_CHUNKS, LANES), U32), pltpu.SemaphoreType.DMA((2,))])
    return pl.pallas_call(
        _combine_kernel,
        grid_spec=grid_spec,
        out_shape=jax.ShapeDtypeStruct((t, D_MODEL), F32),
        compiler_params=_cparams(("arbitrary",)),
        name="combine_ln2")(dest_flat, y_buf, x1, tg, g, be)


def _dft_kernel(cd_ref, sd_ref, ca_ref, sa_ref, c_ref, s_ref):
    cd, sd = cd_ref[0], sd_ref[0]
    ca, sa = ca_ref[0], sa_ref[0]
    c_ref[0] = (cd * ca - sd * sa).astype(c_ref.dtype)
    s_ref[0] = (sd * ca + cd * sa).astype(s_ref.dtype)


def _dft_tables(seq):
    n = 2 * seq
    half = seq // 2
    nblk = half // DFT_TB
    idx = jnp.arange(half, dtype=I32)
    off = jnp.arange(DFT_TB, dtype=I32)
    start = jnp.arange(nblk, dtype=I32) * DFT_TB

    def angle(prod):
        return (prod % n).astype(F32) * (2.0 * math.pi / n)

    ang_d = jnp.stack([angle(off[:, None] * (2 * idx)[None, :]),
                       angle(off[:, None] * (2 * idx + 1)[None, :]),
                       angle((2 * off + 1)[:, None] * idx[None, :])])
    ang_a = jnp.stack([angle(start[:, None] * (2 * idx)[None, :]),
                       angle(start[:, None] * (2 * idx + 1)[None, :]),
                       angle((2 * start)[:, None] * idx[None, :])]).reshape(3 * nblk, 1, half)
    small = pl.BlockSpec((1, DFT_TB, half), lambda k, a: (k, 0, 0))
    base = pl.BlockSpec((1, 1, half), lambda k, a: (k * nblk + a, 0, 0))
    out = pl.BlockSpec((1, DFT_TB, half), lambda k, a: (k, a, 0))
    sds = jax.ShapeDtypeStruct((3, half, half), BF16)
    return pl.pallas_call(
        _dft_kernel, grid=(3, nblk), in_specs=[small, small, base, base], out_specs=[out, out],
        out_shape=[sds, sds], compiler_params=_cparams(("arbitrary", "arbitrary")),
        name="dft_tables")(jnp.cos(ang_d), jnp.sin(ang_d), jnp.cos(ang_a), jnp.sin(ang_a))


def _filter_features(seq):
    t = jnp.linspace(0.0, 1.0, seq, dtype=F32)[:, None]
    bands = (HY_EMB - 1) // 2
    fb = jnp.linspace(1e-4, bands - 1, bands, dtype=F32)[None]
    w = 2.0 * math.pi * jnp.arange(seq, dtype=F32)[:, None] / seq
    z = jnp.concatenate([t, jnp.cos(fb * w), -jnp.sin(fb * w)], -1)
    z = jnp.concatenate([z[0::2], z[1::2]], axis=0)
    return jnp.pad(z, ((0, 0), (0, LANES - HY_EMB)))


def _mixer(x, w_in, b_in, hy_conv_w, hy_conv_b, hy_filt_w1, hy_filt_b1, hy_filt_w2, hy_filt_b2,
           hy_filt_w3, hy_filt_freq, hy_skip, hy_norm_w, ml_conv_w, ml_conv_b, ml_norm_w):
    bsz, seq, _ = x.shape
    t = bsz * seq
    x2d = x.reshape(t, D_MODEL)
    n_main = w_in.shape[1] - N_GATE_COLS
    w_t = jnp.swapaxes(w_in, 0, 1)
    wg = jnp.pad(w_t[n_main:], ((0, LANES - N_GATE_COLS), (0, 0)))
    bg = jnp.pad(b_in[None, n_main:], ((0, 0), (0, LANES - N_GATE_COLS)))
    proj, gates = _in_proj(x2d, w_t[:n_main].astype(BF16), b_in[None, :n_main], wg, bg)
    proj3 = proj.reshape(bsz, seq, n_main)
    gates = gates[:, :N_GATE_COLS]
    g5 = gates.reshape(bsz, seq, 4, ML_HEADS)
    grow = g5.transpose(0, 3, 2, 1).reshape(bsz, ML_HEADS, 4, seq // CHUNK, CHUNK)
    gtr = grow.transpose(0, 1, 2, 4, 3)

    cmat, smat = _dft_tables(seq)
    zpad = _filter_features(seq)
    w1pad = jnp.pad(hy_filt_w1, ((0, LANES - HY_EMB), (0, 0)))
    deltas = jnp.abs(jnp.linspace(math.log(HY_DECAY_TARGET) / HY_SLOW_PCT,
                                  math.log(HY_DECAY_TARGET) / HY_FAST_PCT, D_HY, dtype=F32))[None]
    kr, ki, km = _hyena_filters(zpad, w1pad, hy_filt_b1[None], hy_filt_w2, hy_filt_b2[None],
                                hy_filt_freq, hy_filt_w3, deltas, cmat, smat)
    y_hy = _hyena(proj3, hy_conv_w, hy_conv_b[None], cmat, smat, kr, ki, km, hy_skip, hy_norm_w[None])
    y_ml = _mlstm(proj3, ml_conv_w, ml_conv_b[None], grow, gtr, ml_norm_w[None])
    return y_hy.reshape(t, D_HY), y_ml.reshape(t, D_ML), x2d


def _moe_tables(top_i, slot, counts):
    t = top_i.shape[0]
    n_tiles = N_EXPERTS + (t * TOP_K) // EXP_ROWS
    ntile = (counts + EXP_ROWS - 1) // EXP_ROWS
    ends = jnp.cumsum(ntile)
    starts = ends - ntile
    total = ends[-1]
    s_idx = jnp.arange(n_tiles, dtype=I32)
    valid = s_idx < total
    s_eff = jnp.where(valid, s_idx, jnp.maximum(total - 1, 0))
    tile_e = jnp.minimum(jnp.sum((s_eff[:, None] >= ends[None, :]).astype(I32), axis=1), N_EXPERTS - 1)
    local = s_eff - starts[tile_e]
    tile_rows = jnp.where(valid, jnp.clip(counts[tile_e] - local * EXP_ROWS, 0, EXP_ROWS), 0).astype(I32)
    tok = jnp.arange(t, dtype=I32)[:, None]
    row_tok = jnp.pad(jnp.sort((top_i * t + tok).reshape(-1)) % t, (0, EXP_ROWS))
    first = jnp.cumsum(counts) - counts
    tile_base = (first[tile_e] + local * EXP_ROWS).astype(I32)
    onehot = top_i[:, :, None] == jnp.arange(N_EXPERTS, dtype=I32)
    dest = jnp.sum(jnp.where(onehot, starts * EXP_ROWS, 0), axis=-1) + slot
    return (tile_e.astype(I32), tile_rows, tile_base, row_tok.astype(I32), total.astype(I32),
            dest.astype(I32).reshape(-1))


def kernel(x, w_in, b_in, hy_conv_w, hy_conv_b, hy_filt_w1, hy_filt_b1, hy_filt_w2, hy_filt_b2, hy_filt_w3, hy_filt_freq, hy_skip, hy_norm_w, ml_conv_w, ml_conv_b, ml_norm_w, w_out, b_out, ln1_g, ln1_b, router_w, router_b, w_gu, b_gu, w_down, b_down, ln2_g, ln2_b):
    bsz, seq, _ = x.shape
    l = 0
    y_hy, y_ml, x2d = _mixer(x, w_in[l], b_in[l], hy_conv_w[l], hy_conv_b[l], hy_filt_w1[l], hy_filt_b1[l],
                             hy_filt_w2[l], hy_filt_b2[l], hy_filt_w3[l], hy_filt_freq[l], hy_skip[l],
                             hy_norm_w[l], ml_conv_w[l], ml_conv_b[l], ml_norm_w[l])
    rw = jnp.pad(router_w[l], ((0, 0), (0, LANES - N_EXPERTS)))
    rb = jnp.pad(router_b[l][None], ((0, 0), (0, LANES - N_EXPERTS)), constant_values=-1e30)
    x1, x1c, top_i, top_g, slot, cnt = _out_proj_ln_route(
        y_hy, y_ml, x2d, w_out[l].astype(BF16), b_out[l][None], ln1_g[l][None], ln1_b[l][None], rw, rb)
    counts = cnt[0, :N_EXPERTS].astype(I32)
    tile_e, tile_rows, tile_base, row_tok, used, dest = _moe_tables(top_i[:, :TOP_K], slot[:, :TOP_K], counts)
    y_buf = _experts(tile_e, tile_rows, tile_base, row_tok, used, x1c, w_gu[l], b_gu[l][:, None, :], w_down[l],
                     b_down[l][:, None, :])
    out = _combine_ln(dest, y_buf, x1, top_g, ln2_g[l][None], ln2_b[l][None])
    return out.reshape(bsz, seq, D_MODEL)
```
